```python
import math
import jax, jax.numpy as jnp
from jax import lax
import numpy as np

D_MODEL = 1024
BATCH = 2
SEQ = 8192
DEPTH = 4

N_MIXERS = 2
FOURIER_GROUPS = 4
GROUP_DIM = D_MODEL // FOURIER_GROUPS
HEAD_DIM = 64
N_HEADS = D_MODEL // HEAD_DIM
N_KV_HEADS = N_HEADS // 4
GQA_GROUP = N_HEADS // N_KV_HEADS
WINDOW = 128
BLOCK = 128
N_BUCKETS = 32
MAX_DISTANCE = 128
N_EXPERTS = 16
CAPACITY_FACTOR = 2
D_EXPERT = 2 * D_MODEL
EPS = 1e-6
NEG_INF = -1e30
N_FOURIER_LAYERS = (DEPTH + 1) // 2
N_ATTN_LAYERS = DEPTH // 2

kernel_name = "hybrid_fnet_swa_ec_moe_encoder"


def rms_norm(x, gain):
    xf = x.astype(jnp.float32)
    y = xf * lax.rsqrt(jnp.mean(xf * xf, axis=-1, keepdims=True) + EPS)
    return (y * gain.astype(jnp.float32)).astype(x.dtype)


def modulate(x, gain, shift, scale):
    return rms_norm(x, gain) * (1 + scale[:, None, :]) + shift[:, None, :]


def t5_buckets(rel):
    half = N_BUCKETS // 2
    max_exact = half // 2
    ret = jnp.where(rel > 0, half, 0)
    n = jnp.abs(rel)
    nf = jnp.maximum(n, 1).astype(jnp.float32)
    large = max_exact + (jnp.log(nf / max_exact) / math.log(MAX_DISTANCE / max_exact)
                         * (half - max_exact)).astype(jnp.int32)
    large = jnp.minimum(large, half - 1)
    return ret + jnp.where(n < max_exact, n, large)


def fourier_mix(h, w_out):
    B, S, D = h.shape
    hg = h.astype(jnp.float32).reshape(B, S, FOURIER_GROUPS, GROUP_DIM)
    mixed = jnp.fft.fft2(hg, axes=(1, 3), norm="ortho").real
    return mixed.reshape(B, S, D).astype(h.dtype) @ w_out


def windowed_gqa(h, w_qkv, w_out, q_gain, k_gain, sink, rel_bias):
    B, S, D = h.shape
    nb = S // BLOCK
    qkv = h @ w_qkv
    q, k, v = jnp.split(qkv, [N_HEADS * HEAD_DIM, (N_HEADS + N_KV_HEADS) * HEAD_DIM], axis=-1)
    q = rms_norm(q.reshape(B, S, N_HEADS, HEAD_DIM), q_gain)
    k = rms_norm(k.reshape(B, S, N_KV_HEADS, HEAD_DIM), k_gain)
    v = v.reshape(B, S, N_KV_HEADS, HEAD_DIM)
    qb = q.reshape(B, nb, BLOCK, N_KV_HEADS, GQA_GROUP, HEAD_DIM)

    def band(t):
        tp = jnp.pad(t, ((0, 0), (BLOCK, BLOCK), (0, 0), (0, 0)))
        tp = tp.reshape(B, nb + 2, BLOCK, N_KV_HEADS, HEAD_DIM)
        return jnp.concatenate([tp[:, :-2], tp[:, 1:-1], tp[:, 2:]], axis=2)

    kb, vb = band(k), band(v)
    scores = jnp.einsum('bnqhgd,bnjhd->bnhgqj', qb, kb).astype(jnp.float32) * (HEAD_DIM ** -0.5)

    q_off = jnp.arange(BLOCK)
    k_off = jnp.arange(3 * BLOCK) - BLOCK
    rel = k_off[None, :] - q_off[:, None]
    bias = rel_bias.astype(jnp.float32)[t5_buckets(rel)]
    bias = jnp.transpose(bias, (2, 0, 1)).reshape(N_KV_HEADS, GQA_GROUP, BLOCK, 3 * BLOCK)
    key_pos = jnp.arange(nb)[:, None] * BLOCK + k_off[None, :]
    valid = (jnp.abs(rel) <= WINDOW)[None] & ((key_pos >= 0) & (key_pos < S))[:, None, :]
    scores = jnp.where(valid[None, :, None, None], scores + bias, NEG_INF)

    sink_l = sink.astype(jnp.float32).reshape(N_KV_HEADS, GQA_GROUP)[None, None, :, :, None, None]
    m = jnp.maximum(scores.max(axis=-1, keepdims=True), sink_l)
    p = jnp.exp(scores - m)
    probs = (p / (p.sum(axis=-1, keepdims=True) + jnp.exp(sink_l - m))).astype(h.dtype)
    out = jnp.einsum('bnhgqj,bnjhd->bnqhgd', probs, vb).reshape(B, S, D)
    return out @ w_out


def ec_moe(h, w_router, w_gate, w_up, w_down):
    B, S, D = h.shape
    cap = CAPACITY_FACTOR * S // N_EXPERTS
    logits = jnp.einsum('bsd,de->bse', h, w_router).astype(jnp.float32)
    affinity = jax.nn.softmax(logits, axis=-1)
    gate, idx = lax.top_k(jnp.swapaxes(affinity, 1, 2), cap)
    xin = jax.vmap(lambda hb, ib: hb[ib])(h, idx)
    g = jnp.einsum('becd,edf->becf', xin, w_gate)
    u = jnp.einsum('becd,edf->becf', xin, w_up)
    y = jnp.einsum('becf,efd->becd', jax.nn.silu(g) * u, w_down)
    y = y * gate[..., None].astype(y.dtype)
    flat_idx = (idx + jnp.arange(B)[:, None, None] * S).reshape(-1)
    out = jnp.zeros((B * S, D), dtype=y.dtype).at[flat_idx].add(y.reshape(-1, D))
    return out.reshape(B, S, D)


def setup_inputs(seed: int = 0) -> dict:
    key = jax.random.key(seed)
    ks = jax.random.split(key, 20)
    D, F, E = D_MODEL, D_EXPERT, N_EXPERTS
    QKV = (N_HEADS + 2 * N_KV_HEADS) * HEAD_DIM
    nrm = jax.random.normal
    f32 = jnp.float32
    return {
        "x": nrm(ks[0], (BATCH, SEQ, D), f32),
        "c": nrm(ks[1], (BATCH, D), f32),
        "w_ada": nrm(ks[2], (DEPTH, D, 6 * D), f32) * (0.5 * D ** -0.5),
        "b_ada": nrm(ks[3], (DEPTH, 6 * D), f32) * 0.01,
        "norm_mix": 1.0 + 0.02 * nrm(ks[4], (DEPTH, D), f32),
        "norm_ffn": 1.0 + 0.02 * nrm(ks[5], (DEPTH, D), f32),
        "w_fourier_out": nrm(ks[6], (N_FOURIER_LAYERS, D, D), f32) * D ** -0.5,
        "w_qkv": nrm(ks[7], (N_ATTN_LAYERS, D, QKV), f32) * D ** -0.5,
        "w_attn_out": nrm(ks[8], (N_ATTN_LAYERS, N_HEADS * HEAD_DIM, D), f32) * (N_HEADS * HEAD_DIM) ** -0.5,
        "q_gain": 1.0 + 0.02 * nrm(ks[9], (N_ATTN_LAYERS, HEAD_DIM), f32),
        "k_gain": 1.0 + 0.02 * nrm(ks[10], (N_ATTN_LAYERS, HEAD_DIM), f32),
        "sink": nrm(ks[11], (N_ATTN_LAYERS, N_HEADS), f32),
        "rel_bias": 0.5 * nrm(ks[12], (N_BUCKETS, N_HEADS), f32),
        "w_router": nrm(ks[13], (DEPTH, D, E), f32) * D ** -0.5,
        "w_gate": nrm(ks[14], (DEPTH, E, D, F), f32) * D ** -0.5,
        "w_up": nrm(ks[15], (DEPTH, E, D, F), f32) * D ** -0.5,
        "w_down": nrm(ks[16], (DEPTH, E, F, D), f32) * F ** -0.5,
    }


def reference(x, c, w_ada, b_ada, norm_mix, norm_ffn, w_fourier_out, w_qkv, w_attn_out,
              q_gain, k_gain, sink, rel_bias, w_router, w_gate, w_up, w_down):
    c_act = jax.nn.silu(c)
    for layer in range(DEPTH):
        mod = c_act @ w_ada[layer] + b_ada[layer]
        sh1, sc1, g1, sh2, sc2, g2 = jnp.split(mod, 6, axis=-1)
        h = modulate(x, norm_mix[layer], sh1, sc1)
        j = layer // N_MIXERS
        if layer % N_MIXERS == 0:
            y = fourier_mix(h, w_fourier_out[j])
        else:
            y = windowed_gqa(h, w_qkv[j], w_attn_out[j], q_gain[j], k_gain[j], sink[j], rel_bias)
        x = x + g1[:, None, :] * y
        h = modulate(x, norm_ffn[layer], sh2, sc2)
        x = x + g2[:, None, :] * ec_moe(h, w_router[layer], w_gate[layer], w_up[layer], w_down[layer])
    return x
```

```python
import functools
import math

import numpy as np
import jax
import jax.numpy as jnp
from jax import lax
from jax.experimental import pallas as pl
from jax.experimental.pallas import tpu as pltpu

D = 1024
B = 2
S = 8192
DEPTH = 4
GROUPS = 4
GD = D // GROUPS
HD = 64
NH = 16
NKV = 4
GQA = NH // NKV
WINDOW = 128
BLK = 128
NBUCKETS = 32
MAXDIST = 128
E = 16
CAP = 2 * S // E
F = 2 * D
EPS = 1e-6
NEG_INF = -1e30

N1 = 128
N2 = 64
INV_NORM = 1.0 / math.sqrt(S * GD)

F32 = jnp.float32
BF16 = jnp.bfloat16


def _cparams(sem, vmem_mb=48):
    return pltpu.CompilerParams(dimension_semantics=sem, vmem_limit_bytes=vmem_mb * 1024 * 1024)


def _dft_tables():
    c = np.arange(GD)
    ang0 = 2.0 * np.pi * ((c[:, None] * c[None, :]) % GD) / GD
    m0 = np.concatenate([np.cos(ang0), -np.sin(ang0)], axis=1)
    k2 = np.arange(N2)
    ang2 = 2.0 * np.pi * ((k2[:, None] * k2[None, :]) % N2) / N2
    m2 = np.concatenate([np.cos(ang2), np.sin(ang2)], axis=1)
    return jnp.asarray(m0, BF16), jnp.asarray(m2, BF16)


def _stage1_table():
    s2 = lax.broadcasted_iota(jnp.int32, (N2, N1, N1), 0)
    k1 = lax.broadcasted_iota(jnp.int32, (N2, N1, N1), 1)
    s1 = lax.broadcasted_iota(jnp.int32, (N2, N1, N1), 2)
    m = (k1 * (N2 * s1 + s2)) % S
    th = m.astype(F32) * (2.0 * math.pi / S)
    co, si = jnp.cos(th), jnp.sin(th)
    top = jnp.concatenate([co, si], axis=2)
    bot = jnp.concatenate([-si, co], axis=2)
    return jnp.concatenate([top, bot], axis=1).astype(BF16)


def _bucket_table():
    q_off = np.arange(BLK)
    k_off = np.arange(3 * BLK) - BLK
    rel = k_off[None, :] - q_off[:, None]
    half = NBUCKETS // 2
    max_exact = half // 2
    ret = np.where(rel > 0, half, 0)
    n = np.abs(rel)
    nf = np.maximum(n, 1).astype(np.float32)
    ratio = (np.log(nf / np.float32(max_exact)) / np.float32(math.log(MAXDIST / max_exact))).astype(np.float32)
    large = max_exact + (ratio * np.float32(half - max_exact)).astype(np.int32)
    large = np.minimum(large, half - 1)
    bucket = ret + np.where(n < max_exact, n, large)
    band = np.abs(rel) <= WINDOW
    onehot = np.zeros((NBUCKETS, BLK * 3 * BLK), np.float32)
    flat_b = bucket.reshape(-1)
    flat_band = band.reshape(-1)
    onehot[flat_b[flat_band], np.nonzero(flat_band)[0]] = 1.0
    neg = np.where(flat_band, 0.0, NEG_INF).astype(np.float32)[None, :]
    return jnp.asarray(onehot), jnp.asarray(neg)


def _modulate(x, gain, shift, scale):
    ms = jnp.mean(x * x, axis=-1, keepdims=True)
    return x * lax.rsqrt(ms + EPS) * (gain * (1.0 + scale)) + shift


def _sigmoid(x):
    return 1.0 / (1.0 + jnp.exp(-x))


ADA_TN = 768


def _ada_kernel(ct_ref, w_ref, b_ref, o_ref):
    ct = ct_ref[...]
    ca = ct * _sigmoid(ct)
    w = w_ref[0]
    for b in range(B):
        o_ref[0, b:b + 1, :] = jnp.sum(w * ca[:, b:b + 1], axis=0, keepdims=True) + b_ref[0]


def _ada(c, w_ada, b_ada):
    out = pl.pallas_call(
        _ada_kernel,
        grid=(DEPTH, 6 * D // ADA_TN),
        in_specs=[
            pl.BlockSpec((D, B), lambda l, j: (0, 0)),
            pl.BlockSpec((1, D, ADA_TN), lambda l, j: (l, 0, j)),
            pl.BlockSpec((1, 1, ADA_TN), lambda l, j: (l, 0, j)),
        ],
        out_specs=pl.BlockSpec((1, B, ADA_TN), lambda l, j: (l, 0, j)),
        out_shape=jax.ShapeDtypeStruct((DEPTH, B, 6 * D), F32),
        compiler_params=_cparams(("arbitrary", "arbitrary"), 32),
        name="ada_mod",
    )(c.T, w_ada, b_ada.reshape(DEPTH, 1, 6 * D))
    return out.reshape(DEPTH, B, 6, D)


def _mod_spec(layer):
    return pl.BlockSpec((1, 1, 6, D), lambda b, i: (layer, b, 0, 0))


def _gain_spec(layer):
    return pl.BlockSpec((1, 1, D), lambda b, i: (layer, 0, 0))


F0_TM = 512
F1_J = 8
F2_TK = 8
F3_M = 4


def _f0_kernel(x_ref, mod_ref, gain_ref, m0_ref, o_ref):
    m = mod_ref[0, 0]
    h = _modulate(x_ref[0], gain_ref[0], m[0:1], m[1:2]).astype(BF16)
    m0 = m0_ref[...]
    for g in range(GROUPS):
        r = jnp.dot(h[:, g * GD:(g + 1) * GD], m0, preferred_element_type=F32)
        o_ref[0, 0, :, g * GD:(g + 1) * GD] = r[:, :GD].astype(BF16)
        o_ref[0, 1, :, g * GD:(g + 1) * GD] = r[:, GD:].astype(BF16)


def _f1_kernel(w_ref, m1_ref, o_ref):
    for j in range(F1_J):
        o_ref[0, :, j * D:(j + 1) * D] = jnp.dot(
            m1_ref[j], w_ref[0, :, j * D:(j + 1) * D], preferred_element_type=F32).astype(BF16)


def _f2_kernel(u_ref, m2_ref, o_ref):
    m2 = m2_ref[...]
    for k in range(F2_TK):
        u = jnp.concatenate([u_ref[0, 0, k], u_ref[0, 1, k]], axis=0)
        o_ref[0, k] = jnp.dot(m2, u, preferred_element_type=F32).astype(BF16)


def _f3_kernel(mp_ref, x_ref, w_ref, mod_ref, o_ref):
    g1 = mod_ref[0, 0][2:3]
    a = jnp.concatenate([mp_ref[0, :, j * D:(j + 1) * D] for j in range(F3_M)], axis=0)
    y = jnp.dot(a, w_ref[...], preferred_element_type=F32)
    o_ref[0] = x_ref[0] + (g1 * INV_NORM) * y


def _fourier_layer(x, mod_all, gain, w_out_bf, layer, m0, m1, m2):
    wc = pl.pallas_call(
        _f0_kernel,
        grid=(B, S // F0_TM),
        in_specs=[
            pl.BlockSpec((1, F0_TM, D), lambda b, i: (b, i, 0)),
            _mod_spec(layer), _gain_spec(layer),
            pl.BlockSpec((GD, 2 * GD), lambda b, i: (0, 0)),
        ],
        out_specs=pl.BlockSpec((1, 2, F0_TM, D), lambda b, i: (b, 0, i, 0)),
        out_shape=jax.ShapeDtypeStruct((B, 2, S, D), BF16),
        compiler_params=_cparams(("arbitrary", "arbitrary")),
        name="fourier_chan",
    )(x, mod_all, gain, m0)
    wc = wc.reshape(B, 2 * N1, N2 * D)
    u = pl.pallas_call(
        _f1_kernel,
        grid=(B, N2 // F1_J),
        in_specs=[
            pl.BlockSpec((1, 2 * N1, F1_J * D), lambda b, i: (b, 0, i)),
            pl.BlockSpec((F1_J, 2 * N1, 2 * N1), lambda b, i: (i, 0, 0)),
        ],
        out_specs=pl.BlockSpec((1, 2 * N1, F1_J * D), lambda b, i: (b, 0, i)),
        out_shape=jax.ShapeDtypeStruct((B, 2 * N1, N2 * D), BF16),
        compiler_params=_cparams(("arbitrary", "arbitrary")),
        name="fourier_seq1",
    )(wc, m1)
    u = u.reshape(B, 2, N1, N2, D)
    mp = pl.pallas_call(
        _f2_kernel,
        grid=(B, N1 // F2_TK),
        in_specs=[
            pl.BlockSpec((1, 2, F2_TK, N2, D), lambda b, i: (b, 0, i, 0, 0)),
            pl.BlockSpec((N2, 2 * N2), lambda b, i: (0, 0)),
        ],
        out_specs=pl.BlockSpec((1, F2_TK, N2, D), lambda b, i: (b, i, 0, 0)),
        out_shape=jax.ShapeDtypeStruct((B, N1, N2, D), BF16),
        compiler_params=_cparams(("arbitrary", "arbitrary")),
        name="fourier_seq2",
    )(u, m2)
    mp = mp.reshape(B, N1, N2 * D)
    return pl.pallas_call(
        _f3_kernel,
        grid=(B, N2 // F3_M),
        in_specs=[
            pl.BlockSpec((1, N1, F3_M * D), lambda b, i: (b, 0, i)),
            pl.BlockSpec((1, F3_M * N1, D), lambda b, i: (b, i, 0)),
            pl.BlockSpec((D, D), lambda b, i: (0, 0)),
            _mod_spec(layer),
        ],
        out_specs=pl.BlockSpec((1, F3_M * N1, D), lambda b, i: (b, i, 0)),
        out_shape=jax.ShapeDtypeStruct((B, S, D), F32),
        compiler_params=_cparams(("arbitrary", "arbitrary")),
        name="fourier_out",
    )(mp, x, w_out_bf, mod_all)


QKV_TM = 512
QKV_N = (NH + 2 * NKV) * HD
TQ = 256
NSB = TQ // BLK
KWIN = TQ + 2 * BLK
NBLK = S // BLK


def _bias_kernel(rbt_ref, oh_ref, neg_ref, o_ref):
    rbt = rbt_ref[...]
    acc = jnp.zeros(o_ref.shape, F32)
    for k in range(NBUCKETS):
        acc = acc + rbt[:, k:k + 1] * oh_ref[k:k + 1, :]
    o_ref[...] = acc + neg_ref[...]


def _bias_table(rel_bias, onehot, neg):
    tn = 3 * BLK * 16
    out = pl.pallas_call(
        _bias_kernel,
        grid=(BLK * 3 * BLK // tn,),
        in_specs=[
            pl.BlockSpec((NH, NBUCKETS), lambda j: (0, 0)),
            pl.BlockSpec((NBUCKETS, tn), lambda j: (0, j)),
            pl.BlockSpec((1, tn), lambda j: (0, j)),
        ],
        out_specs=pl.BlockSpec((NH, tn), lambda j: (0, j)),
        out_shape=jax.ShapeDtypeStruct((NH, BLK * 3 * BLK), F32),
        compiler_params=_cparams(("arbitrary",)),
        name="rel_bias_table",
    )(rel_bias.T, onehot, neg)
    return out.reshape(NKV, GQA * BLK, 3 * BLK)


def _qkv_kernel(x_ref, mod_ref, gain_ref, w_ref, q_ref, k_ref, v_ref):
    m = mod_ref[0, 0]
    h = _modulate(x_ref[0], gain_ref[0], m[0:1], m[1:2]).astype(BF16)
    r = jnp.dot(h, w_ref[...], preferred_element_type=F32)
    q_ref[0] = r[:, :NH * HD]
    k_ref[0] = r[:, NH * HD:(NH + NKV) * HD]
    v_ref[0] = r[:, (NH + NKV) * HD:]


def _head_norm(t, gain):
    ms = jnp.mean(t * t, axis=-1, keepdims=True)
    return (t * lax.rsqrt(ms + EPS) * gain).astype(BF16)


def _attn_kernel(q_ref, kp_ref, kc_ref, kn_ref, vp_ref, vc_ref, vn_ref, x_ref, w_ref, mod_ref,
                 qg_ref, kg_ref, bias_ref, sink_ref, o_ref, att_ref):
    i = pl.program_id(1)
    kwin = jnp.concatenate([kp_ref[0], kc_ref[0], kn_ref[0]], axis=0)
    vwin = jnp.concatenate([vp_ref[0], vc_ref[0], vn_ref[0]], axis=0).astype(BF16)
    q = q_ref[0]
    qg = qg_ref[0]
    kg = kg_ref[0]
    col = lax.broadcasted_iota(jnp.int32, (1, 3 * BLK), 1)
    for g in range(NKV):
        kh = _head_norm(kwin[:, g * HD:(g + 1) * HD], kg)
        vh = vwin[:, g * HD:(g + 1) * HD]
        bias = bias_ref[g]
        sink = sink_ref[g]
        for r in range(NSB):
            qs = jnp.concatenate(
                [q[r * BLK:(r + 1) * BLK, (GQA * g + hh) * HD:(GQA * g + hh + 1) * HD] for hh in range(GQA)],
                axis=0)
            qs = _head_norm(qs, qg)
            kr = kh[r * BLK:r * BLK + 3 * BLK]
            vr = vh[r * BLK:r * BLK + 3 * BLK]
            s = lax.dot_general(qs, kr, (((1,), (1,)), ((), ())), preferred_element_type=F32)
            s = s * (HD ** -0.5) + bias
            key_pos = i * TQ + (r - 1) * BLK + col
            s = jnp.where((key_pos >= 0) & (key_pos < S), s, NEG_INF)
            mx = jnp.maximum(jnp.max(s, axis=-1, keepdims=True), sink)
            p = jnp.exp(s - mx)
            den = jnp.sum(p, axis=-1, keepdims=True) + jnp.exp(sink - mx)
            pn = (p * (1.0 / den)).astype(BF16)
            o = jnp.dot(pn, vr, preferred_element_type=F32)
            for hh in range(GQA):
                hd0 = (GQA * g + hh) * HD
                att_ref[r * BLK:(r + 1) * BLK, hd0:hd0 + HD] = o[hh * BLK:(hh + 1) * BLK]
    g1 = mod_ref[0, 0][2:3]
    y = jnp.dot(att_ref[...].astype(BF16), w_ref[...], preferred_element_type=F32)
    o_ref[0] = x_ref[0] + g1 * y


def _attn_layer(x, mod_all, gain, w_qkv_bf, w_out_bf, q_gain, k_gain, sink, bias_tab, layer):
    q, k, v = pl.pallas_call(
        _qkv_kernel,
        grid=(B, S // QKV_TM),
        in_specs=[
            pl.BlockSpec((1, QKV_TM, D), lambda b, i: (b, i, 0)),
            _mod_spec(layer), _gain_spec(layer),
            pl.BlockSpec((D, QKV_N), lambda b, i: (0, 0)),
        ],
        out_specs=[
            pl.BlockSpec((1, QKV_TM, NH * HD), lambda b, i: (b, i, 0)),
            pl.BlockSpec((1, QKV_TM, NKV * HD), lambda b, i: (b, i, 0)),
            pl.BlockSpec((1, QKV_TM, NKV * HD), lambda b, i: (b, i, 0)),
        ],
        out_shape=[
            jax.ShapeDtypeStruct((B, S, NH * HD), F32),
            jax.ShapeDtypeStruct((B, S, NKV * HD), F32),
            jax.ShapeDtypeStruct((B, S, NKV * HD), F32),
        ],
        compiler_params=_cparams(("arbitrary", "arbitrary")),
        name="attn_qkv",
    )(x, mod_all, gain, w_qkv_bf)

    kvw = NKV * HD
    prev_spec = pl.BlockSpec((1, BLK, kvw), lambda b, i: (b, jnp.maximum(i * NSB - 1, 0), 0))
    cur_spec = pl.BlockSpec((1, TQ, kvw), lambda b, i: (b, i, 0))
    next_spec = pl.BlockSpec((1, BLK, kvw), lambda b, i: (b, jnp.minimum((i + 1) * NSB, NBLK - 1), 0))
    sink_tab = jnp.broadcast_to(sink.reshape(NKV, GQA, 1, 1), (NKV, GQA, BLK, 1)).reshape(NKV, GQA * BLK, 1)
    return pl.pallas_call(
        _attn_kernel,
        grid=(B, S // TQ),
        in_specs=[
            pl.BlockSpec((1, TQ, NH * HD), lambda b, i: (b, i, 0)),
            prev_spec, cur_spec, next_spec, prev_spec, cur_spec, next_spec,
            pl.BlockSpec((1, TQ, D), lambda b, i: (b, i, 0)),
            pl.BlockSpec((NH * HD, D), lambda b, i: (0, 0)),
            _mod_spec(layer),
            pl.BlockSpec((1, HD), lambda b, i: (0, 0)),
            pl.BlockSpec((1, HD), lambda b, i: (0, 0)),
            pl.BlockSpec((NKV, GQA * BLK, 3 * BLK), lambda b, i: (0, 0, 0)),
            pl.BlockSpec((NKV, GQA * BLK, 1), lambda b, i: (0, 0, 0)),
        ],
        out_specs=pl.BlockSpec((1, TQ, D), lambda b, i: (b, i, 0)),
        out_shape=jax.ShapeDtypeStruct((B, S, D), F32),
        scratch_shapes=[pltpu.VMEM((TQ, NH * HD), F32)],
        compiler_params=_cparams(("arbitrary", "arbitrary")),
        name="attn_core",
    )(q, k, k, k, v, v, v, x, w_out_bf, mod_all, q_gain.reshape(1, HD), k_gain.reshape(1, HD),
      bias_tab, sink_tab)


RT_TM = 512
EP = 128
FFN_TF = 512
FFN_RT = 512


def _router_kernel(x_ref, mod_ref, gain_ref, wr_ref, h_ref, aff_ref):
    m = mod_ref[0, 0]
    h = _modulate(x_ref[0], gain_ref[0], m[3:4], m[4:5])
    h_ref[0] = h
    logits = jnp.dot(h, wr_ref[...], preferred_element_type=F32, precision=lax.Precision.HIGHEST)
    lt = logits.T[:E]
    mx = jnp.max(lt, axis=0, keepdims=True)
    ex = jnp.exp(lt - mx)
    aff_ref[0] = ex / jnp.sum(ex, axis=0, keepdims=True)


def _select_kernel(aff_ref, tri_ref, pos_ref):
    aff = aff_ref[...]
    bits = pltpu.bitcast(aff, jnp.int32)
    rows = aff.shape[0]

    def count_ge(v):
        return jnp.sum((bits >= v).astype(jnp.int32), axis=1, keepdims=True)

    def body(t, cur):
        cand = cur | (jnp.int32(1) << (30 - t))
        return jnp.where(count_ge(cand) >= CAP, cand, cur)

    thr = lax.fori_loop(0, 31, body, jnp.zeros((rows, 1), jnp.int32))
    gt = bits > thr
    eq = bits == thr
    need = CAP - jnp.sum(gt.astype(jnp.int32), axis=1, keepdims=True)
    tri = tri_ref[...]

    def cumsum_excl(mask_f):
        off = jnp.zeros((rows, 1), F32)
        outs = []
        for j in range(S // 128):
            mj = mask_f[:, j * 128:(j + 1) * 128]
            loc = jnp.dot(mj.astype(BF16), tri, preferred_element_type=F32)
            outs.append(loc - mj + off)
            off = off + loc[:, 127:128]
        return jnp.concatenate(outs, axis=1)

    eq_rank = cumsum_excl(eq.astype(F32))
    sel = gt | (eq & (eq_rank < need.astype(F32)))
    pos = cumsum_excl(sel.astype(F32))
    pos_ref[...] = jnp.where(sel, pos.astype(jnp.int32), -1)


def _ffn_kernel(x_ref, wg_ref, wu_ref, wd_ref, gate_ref, mod_ref, o_ref):
    f = pl.program_id(1)
    wg = wg_ref[0].astype(BF16)
    wu = wu_ref[0].astype(BF16)
    wd = wd_ref[0].astype(BF16)
    for b in range(B):
        for r in range(CAP // FFN_RT):
            rows = slice(r * FFN_RT, (r + 1) * FFN_RT)
            xr = x_ref[b, 0, rows, :].astype(BF16)
            g = jnp.dot(xr, wg, preferred_element_type=F32)
            u = jnp.dot(xr, wu, preferred_element_type=F32)
            a = (g * _sigmoid(g) * u).astype(BF16)
            y = jnp.dot(a, wd, preferred_element_type=F32)

            @pl.when(f == 0)
            def _():
                o_ref[b, 0, rows, :] = y

            @pl.when(f > 0)
            def _():
                o_ref[b, 0, rows, :] += y

    @pl.when(f == pl.num_programs(1) - 1)
    def _():
        for b in range(B):
            g2 = mod_ref[0, b][5:6]
            o_ref[b, 0] = o_ref[b, 0] * gate_ref[b, 0] * g2


def _moe_layer(x, mod_all, gain, w_router, w_gate, w_up, w_down, layer, tri):
    wr = jnp.pad(w_router, ((0, 0), (0, EP - E)))
    h, aff = pl.pallas_call(
        _router_kernel,
        grid=(B, S // RT_TM),
        in_specs=[
            pl.BlockSpec((1, RT_TM, D), lambda b, i: (b, i, 0)),
            _mod_spec(layer), _gain_spec(layer),
            pl.BlockSpec((D, EP), lambda b, i: (0, 0)),
        ],
        out_specs=[
            pl.BlockSpec((1, RT_TM, D), lambda b, i: (b, i, 0)),
            pl.BlockSpec((1, E, RT_TM), lambda b, i: (b, 0, i)),
        ],
        out_shape=[
            jax.ShapeDtypeStruct((B, S, D), F32),
            jax.ShapeDtypeStruct((B, E, S), F32),
        ],
        compiler_params=_cparams(("arbitrary", "arbitrary")),
        name="moe_router",
    )(x, mod_all, gain, wr)

    aff2 = aff.reshape(B * E, S)
    pos = pl.pallas_call(
        _select_kernel,
        grid=(1,),
        in_specs=[
            pl.BlockSpec((B * E, S), lambda i: (0, 0)),
            pl.BlockSpec((128, 128), lambda i: (0, 0)),
        ],
        out_specs=pl.BlockSpec((B * E, S), lambda i: (0, 0)),
        out_shape=jax.ShapeDtypeStruct((B * E, S), jnp.int32),
        compiler_params=_cparams(("arbitrary",)),
        name="moe_select",
    )(aff2, tri)

    key = jnp.where(pos < 0, S + 1, pos)
    idx = jnp.argsort(key, axis=1)[:, :CAP].astype(jnp.int32)
    gate = jnp.take_along_axis(aff2, idx, axis=1)
    flat_idx = idx + (jnp.arange(B * E, dtype=jnp.int32) // E)[:, None] * S
    xin = h.reshape(B * S, D)[flat_idx.reshape(-1)].reshape(B, E, CAP, D)

    y = pl.pallas_call(
        _ffn_kernel,
        grid=(E, F // FFN_TF),
        in_specs=[
            pl.BlockSpec((B, 1, CAP, D), lambda e, f: (0, e, 0, 0)),
            pl.BlockSpec((1, D, FFN_TF), lambda e, f: (e, 0, f)),
            pl.BlockSpec((1, D, FFN_TF), lambda e, f: (e, 0, f)),
            pl.BlockSpec((1, FFN_TF, D), lambda e, f: (e, f, 0)),
            pl.BlockSpec((B, 1, CAP, 1), lambda e, f: (0, e, 0, 0)),
            pl.BlockSpec((1, B, 6, D), lambda e, f: (layer, 0, 0, 0)),
        ],
        out_specs=pl.BlockSpec((B, 1, CAP, D), lambda e, f: (0, e, 0, 0)),
        out_shape=jax.ShapeDtypeStruct((B, E, CAP, D), F32),
        compiler_params=_cparams(("arbitrary", "arbitrary"), 56),
        name="moe_ffn",
    )(xin, w_gate, w_up, w_down, gate.reshape(B, E, CAP, 1), mod_all)

    out = x.reshape(B * S, D).at[flat_idx.reshape(-1)].add(y.reshape(-1, D))
    return out.reshape(B, S, D)


def kernel(x, c, w_ada, b_ada, norm_mix, norm_ffn, w_fourier_out, w_qkv, w_attn_out, q_gain, k_gain,
           sink, rel_bias, w_router, w_gate, w_up, w_down):
    m0, m2 = _dft_tables()
    m1 = _stage1_table()
    onehot, neg = _bucket_table()
    tri = jnp.asarray(np.triu(np.ones((128, 128), np.float32)), BF16)
    mod_all = _ada(c, w_ada, b_ada)
    bias_tab = _bias_table(rel_bias, onehot, neg)
    gain_mix = norm_mix.reshape(DEPTH, 1, D)
    gain_ffn = norm_ffn.reshape(DEPTH, 1, D)
    for layer in range(DEPTH):
        j = layer // 2
        if layer % 2 == 0:
            x = _fourier_layer(x, mod_all, gain_mix, w_fourier_out[j].astype(BF16), layer, m0, m1, m2)
        else:
            x = _attn_layer(x, mod_all, gain_mix, w_qkv[j].astype(BF16), w_attn_out[j].astype(BF16),
                            q_gain[j], k_gain[j], sink[j], bias_tab, layer)
        x = _moe_layer(x, mod_all, gain_ffn, w_router[layer], w_gate[layer], w_up[layer], w_down[layer],
                       layer, tri)
    return x
```

```python
import functools
import math

import numpy as np
import jax
import jax.numpy as jnp
from jax import lax
from jax.experimental import pallas as pl
from jax.experimental.pallas import tpu as pltpu
from jax.experimental.pallas import tpu_sc as plsc

D = 1024
B = 2
S = 8192
DEPTH = 4
GROUPS = 4
GD = D // GROUPS
HD = 64
NH = 16
NKV = 4
GQA = NH // NKV
WINDOW = 128
BLK = 128
NBUCKETS = 32
MAXDIST = 128
E = 16
CAP = 2 * S // E
F = 2 * D
EPS = 1e-6
NEG_INF = -1e30

N1 = 128
N2 = 64
INV_NORM = 1.0 / math.sqrt(S * GD)

F32 = jnp.float32
BF16 = jnp.bfloat16


def _cparams(sem, vmem_mb=48):
    return pltpu.CompilerParams(dimension_semantics=sem, vmem_limit_bytes=vmem_mb * 1024 * 1024)


def _dft_tables():
    c = np.arange(GD)
    ang0 = 2.0 * np.pi * ((c[:, None] * c[None, :]) % GD) / GD
    m0 = np.concatenate([np.cos(ang0), -np.sin(ang0)], axis=1)
    k2 = np.arange(N2)
    ang2 = 2.0 * np.pi * ((k2[:, None] * k2[None, :]) % N2) / N2
    m2 = np.concatenate([np.cos(ang2), np.sin(ang2)], axis=1)
    return jnp.asarray(m0, BF16), jnp.asarray(m2, BF16)


def _stage1_table():
    s2 = lax.broadcasted_iota(jnp.int32, (N2, N1, N1), 0)
    k1 = lax.broadcasted_iota(jnp.int32, (N2, N1, N1), 1)
    s1 = lax.broadcasted_iota(jnp.int32, (N2, N1, N1), 2)
    m = (k1 * (N2 * s1 + s2)) % S
    th = m.astype(F32) * (2.0 * math.pi / S)
    co, si = jnp.cos(th), jnp.sin(th)
    top = jnp.concatenate([co, si], axis=2)
    bot = jnp.concatenate([-si, co], axis=2)
    return jnp.concatenate([top, bot], axis=1).astype(BF16)


def _bucket_table():
    q_off = np.arange(BLK)
    k_off = np.arange(3 * BLK) - BLK
    rel = k_off[None, :] - q_off[:, None]
    half = NBUCKETS // 2
    max_exact = half // 2
    ret = np.where(rel > 0, half, 0)
    n = np.abs(rel)
    nf = np.maximum(n, 1).astype(np.float32)
    ratio = (np.log(nf / np.float32(max_exact)) / np.float32(math.log(MAXDIST / max_exact))).astype(np.float32)
    large = max_exact + (ratio * np.float32(half - max_exact)).astype(np.int32)
    large = np.minimum(large, half - 1)
    bucket = ret + np.where(n < max_exact, n, large)
    band = np.abs(rel) <= WINDOW
    onehot = np.zeros((NBUCKETS, BLK * 3 * BLK), np.float32)
    flat_b = bucket.reshape(-1)
    flat_band = band.reshape(-1)
    onehot[flat_b[flat_band], np.nonzero(flat_band)[0]] = 1.0
    neg = np.where(flat_band, 0.0, NEG_INF).astype(np.float32)[None, :]
    return jnp.asarray(onehot), jnp.asarray(neg)


def _modulate(x, gain, shift, scale):
    ms = jnp.mean(x * x, axis=-1, keepdims=True)
    return x * lax.rsqrt(ms + EPS) * (gain * (1.0 + scale)) + shift


def _sigmoid(x):
    return 1.0 / (1.0 + jnp.exp(-x))


ADA_TN = 768


def _ada_kernel(ct_ref, w_ref, b_ref, o_ref):
    ct = ct_ref[...]
    ca = ct * _sigmoid(ct)
    w = w_ref[0]
    for b in range(B):
        o_ref[0, b:b + 1, :] = jnp.sum(w * ca[:, b:b + 1], axis=0, keepdims=True) + b_ref[0]


def _ada(c, w_ada, b_ada):
    out = pl.pallas_call(
        _ada_kernel,
        grid=(DEPTH, 6 * D // ADA_TN),
        in_specs=[
            pl.BlockSpec((D, B), lambda l, j: (0, 0)),
            pl.BlockSpec((1, D, ADA_TN), lambda l, j: (l, 0, j)),
            pl.BlockSpec((1, 1, ADA_TN), lambda l, j: (l, 0, j)),
        ],
        out_specs=pl.BlockSpec((1, B, ADA_TN), lambda l, j: (l, 0, j)),
        out_shape=jax.ShapeDtypeStruct((DEPTH, B, 6 * D), F32),
        compiler_params=_cparams(("arbitrary", "arbitrary"), 32),
        name="ada_mod",
    )(c.T, w_ada, b_ada.reshape(DEPTH, 1, 6 * D))
    return out.reshape(DEPTH, B, 6, D)


def _mod_spec(layer):
    return pl.BlockSpec((1, 1, 6, D), lambda b, i: (layer, b, 0, 0))


def _gain_spec(layer):
    return pl.BlockSpec((1, 1, D), lambda b, i: (layer, 0, 0))


F0_TM = 512
F1_J = 8
F2_TK = 8
F3_M = 4


def _f0_kernel(x_ref, mod_ref, gain_ref, m0_ref, o_ref):
    m = mod_ref[0, 0]
    h = _modulate(x_ref[0], gain_ref[0], m[0:1], m[1:2]).astype(BF16)
    m0 = m0_ref[...]
    for g in range(GROUPS):
        r = jnp.dot(h[:, g * GD:(g + 1) * GD], m0, preferred_element_type=F32)
        o_ref[0, 0, :, g * GD:(g + 1) * GD] = r[:, :GD].astype(BF16)
        o_ref[0, 1, :, g * GD:(g + 1) * GD] = r[:, GD:].astype(BF16)


def _f1_kernel(w_ref, m1_ref, o_ref):
    for j in range(F1_J):
        o_ref[0, :, j * D:(j + 1) * D] = jnp.dot(
            m1_ref[j], w_ref[0, :, j * D:(j + 1) * D], preferred_element_type=F32).astype(BF16)


def _f2_kernel(u_ref, m2_ref, o_ref):
    m2 = m2_ref[...]
    for k in range(F2_TK):
        u = jnp.concatenate([u_ref[0, 0, k], u_ref[0, 1, k]], axis=0)
        o_ref[0, k] = jnp.dot(m2, u, preferred_element_type=F32).astype(BF16)


def _f3_kernel(mp_ref, x_ref, w_ref, mod_ref, o_ref):
    g1 = mod_ref[0, 0][2:3]
    a = jnp.concatenate([mp_ref[0, :, j * D:(j + 1) * D] for j in range(F3_M)], axis=0)
    y = jnp.dot(a, w_ref[...], preferred_element_type=F32)
    o_ref[0] = x_ref[0] + (g1 * INV_NORM) * y


def _fourier_layer(x, mod_all, gain, w_out_bf, layer, m0, m1, m2):
    wc = pl.pallas_call(
        _f0_kernel,
        grid=(B, S // F0_TM),
        in_specs=[
            pl.BlockSpec((1, F0_TM, D), lambda b, i: (b, i, 0)),
            _mod_spec(layer), _gain_spec(layer),
            pl.BlockSpec((GD, 2 * GD), lambda b, i: (0, 0)),
        ],
        out_specs=pl.BlockSpec((1, 2, F0_TM, D), lambda b, i: (b, 0, i, 0)),
        out_shape=jax.ShapeDtypeStruct((B, 2, S, D), BF16),
        compiler_params=_cparams(("arbitrary", "arbitrary")),
        name="fourier_chan",
    )(x, mod_all, gain, m0)
    wc = wc.reshape(B, 2 * N1, N2 * D)
    u = pl.pallas_call(
        _f1_kernel,
        grid=(B, N2 // F1_J),
        in_specs=[
            pl.BlockSpec((1, 2 * N1, F1_J * D), lambda b, i: (b, 0, i)),
            pl.BlockSpec((F1_J, 2 * N1, 2 * N1), lambda b, i: (i, 0, 0)),
        ],
        out_specs=pl.BlockSpec((1, 2 * N1, F1_J * D), lambda b, i: (b, 0, i)),
        out_shape=jax.ShapeDtypeStruct((B, 2 * N1, N2 * D), BF16),
        compiler_params=_cparams(("arbitrary", "arbitrary")),
        name="fourier_seq1",
    )(wc, m1)
    u = u.reshape(B, 2, N1, N2, D)
    mp = pl.pallas_call(
        _f2_kernel,
        grid=(B, N1 // F2_TK),
        in_specs=[
            pl.BlockSpec((1, 2, F2_TK, N2, D), lambda b, i: (b, 0, i, 0, 0)),
            pl.BlockSpec((N2, 2 * N2), lambda b, i: (0, 0)),
        ],
        out_specs=pl.BlockSpec((1, F2_TK, N2, D), lambda b, i: (b, i, 0, 0)),
        out_shape=jax.ShapeDtypeStruct((B, N1, N2, D), BF16),
        compiler_params=_cparams(("arbitrary", "arbitrary")),
        name="fourier_seq2",
    )(u, m2)
    mp = mp.reshape(B, N1, N2 * D)
    return pl.pallas_call(
        _f3_kernel,
        grid=(B, N2 // F3_M),
        in_specs=[
            pl.BlockSpec((1, N1, F3_M * D), lambda b, i: (b, 0, i)),
            pl.BlockSpec((1, F3_M * N1, D), lambda b, i: (b, i, 0)),
            pl.BlockSpec((D, D), lambda b, i: (0, 0)),
            _mod_spec(layer),
        ],
        out_specs=pl.BlockSpec((1, F3_M * N1, D), lambda b, i: (b, i, 0)),
        out_shape=jax.ShapeDtypeStruct((B, S, D), F32),
        compiler_params=_cparams(("arbitrary", "arbitrary")),
        name="fourier_out",
    )(mp, x, w_out_bf, mod_all)


QKV_TM = 512
QKV_N = (NH + 2 * NKV) * HD
TQ = 256
NSB = TQ // BLK
KWIN = TQ + 2 * BLK
NBLK = S // BLK


def _bias_kernel(rbt_ref, oh_ref, neg_ref, o_ref):
    rbt = rbt_ref[...]
    acc = jnp.zeros(o_ref.shape, F32)
    for k in range(NBUCKETS):
        acc = acc + rbt[:, k:k + 1] * oh_ref[k:k + 1, :]
    o_ref[...] = acc + neg_ref[...]


def _bias_table(rel_bias, onehot, neg):
    tn = 3 * BLK * 16
    out = pl.pallas_call(
        _bias_kernel,
        grid=(BLK * 3 * BLK // tn,),
        in_specs=[
            pl.BlockSpec((NH, NBUCKETS), lambda j: (0, 0)),
            pl.BlockSpec((NBUCKETS, tn), lambda j: (0, j)),
            pl.BlockSpec((1, tn), lambda j: (0, j)),
        ],
        out_specs=pl.BlockSpec((NH, tn), lambda j: (0, j)),
        out_shape=jax.ShapeDtypeStruct((NH, BLK * 3 * BLK), F32),
        compiler_params=_cparams(("arbitrary",)),
        name="rel_bias_table",
    )(rel_bias.T, onehot, neg)
    return out.reshape(NKV, GQA * BLK, 3 * BLK)


def _qkv_kernel(x_ref, mod_ref, gain_ref, w_ref, q_ref, k_ref, v_ref):
    m = mod_ref[0, 0]
    h = _modulate(x_ref[0], gain_ref[0], m[0:1], m[1:2]).astype(BF16)
    r = jnp.dot(h, w_ref[...], preferred_element_type=F32)
    q_ref[0] = r[:, :NH * HD]
    k_ref[0] = r[:, NH * HD:(NH + NKV) * HD]
    v_ref[0] = r[:, (NH + NKV) * HD:]


def _head_norm(t, gain):
    ms = jnp.mean(t * t, axis=-1, keepdims=True)
    return (t * lax.rsqrt(ms + EPS) * gain).astype(BF16)


def _attn_kernel(q_ref, kp_ref, kc_ref, kn_ref, vp_ref, vc_ref, vn_ref, x_ref, w_ref, mod_ref,
                 qg_ref, kg_ref, bias_ref, sink_ref, o_ref, att_ref):
    i = pl.program_id(1)
    kwin = jnp.concatenate([kp_ref[0], kc_ref[0], kn_ref[0]], axis=0)
    vwin = jnp.concatenate([vp_ref[0], vc_ref[0], vn_ref[0]], axis=0).astype(BF16)
    q = q_ref[0]
    qg = qg_ref[0]
    kg = kg_ref[0]
    col = lax.broadcasted_iota(jnp.int32, (1, 3 * BLK), 1)
    for g in range(NKV):
        kh = _head_norm(kwin[:, g * HD:(g + 1) * HD], kg)
        vh = vwin[:, g * HD:(g + 1) * HD]
        bias = bias_ref[g]
        sink = sink_ref[g]
        for r in range(NSB):
            qs = jnp.concatenate(
                [q[r * BLK:(r + 1) * BLK, (GQA * g + hh) * HD:(GQA * g + hh + 1) * HD] for hh in range(GQA)],
                axis=0)
            qs = _head_norm(qs, qg)
            kr = kh[r * BLK:r * BLK + 3 * BLK]
            vr = vh[r * BLK:r * BLK + 3 * BLK]
            s = lax.dot_general(qs, kr, (((1,), (1,)), ((), ())), preferred_element_type=F32)
            s = s * (HD ** -0.5) + bias
            key_pos = i * TQ + (r - 1) * BLK + col
            s = jnp.where((key_pos >= 0) & (key_pos < S), s, NEG_INF)
            mx = jnp.maximum(jnp.max(s, axis=-1, keepdims=True), sink)
            p = jnp.exp(s - mx)
            den = jnp.sum(p, axis=-1, keepdims=True) + jnp.exp(sink - mx)
            pn = (p * (1.0 / den)).astype(BF16)
            o = jnp.dot(pn, vr, preferred_element_type=F32)
            for hh in range(GQA):
                hd0 = (GQA * g + hh) * HD
                att_ref[r * BLK:(r + 1) * BLK, hd0:hd0 + HD] = o[hh * BLK:(hh + 1) * BLK]
    g1 = mod_ref[0, 0][2:3]
    y = jnp.dot(att_ref[...].astype(BF16), w_ref[...], preferred_element_type=F32)
    o_ref[0] = x_ref[0] + g1 * y


def _attn_layer(x, mod_all, gain, w_qkv_bf, w_out_bf, q_gain, k_gain, sink, bias_tab, layer):
    q, k, v = pl.pallas_call(
        _qkv_kernel,
        grid=(B, S // QKV_TM),
        in_specs=[
            pl.BlockSpec((1, QKV_TM, D), lambda b, i: (b, i, 0)),
            _mod_spec(layer), _gain_spec(layer),
            pl.BlockSpec((D, QKV_N), lambda b, i: (0, 0)),
        ],
        out_specs=[
            pl.BlockSpec((1, QKV_TM, NH * HD), lambda b, i: (b, i, 0)),
            pl.BlockSpec((1, QKV_TM, NKV * HD), lambda b, i: (b, i, 0)),
            pl.BlockSpec((1, QKV_TM, NKV * HD), lambda b, i: (b, i, 0)),
        ],
        out_shape=[
            jax.ShapeDtypeStruct((B, S, NH * HD), F32),
            jax.ShapeDtypeStruct((B, S, NKV * HD), F32),
            jax.ShapeDtypeStruct((B, S, NKV * HD), F32),
        ],
        compiler_params=_cparams(("arbitrary", "arbitrary")),
        name="attn_qkv",
    )(x, mod_all, gain, w_qkv_bf)

    kvw = NKV * HD
    prev_spec = pl.BlockSpec((1, BLK, kvw), lambda b, i: (b, jnp.maximum(i * NSB - 1, 0), 0))
    cur_spec = pl.BlockSpec((1, TQ, kvw), lambda b, i: (b, i, 0))
    next_spec = pl.BlockSpec((1, BLK, kvw), lambda b, i: (b, jnp.minimum((i + 1) * NSB, NBLK - 1), 0))
    sink_tab = jnp.broadcast_to(sink.reshape(NKV, GQA, 1, 1), (NKV, GQA, BLK, 1)).reshape(NKV, GQA * BLK, 1)
    return pl.pallas_call(
        _attn_kernel,
        grid=(B, S // TQ),
        in_specs=[
            pl.BlockSpec((1, TQ, NH * HD), lambda b, i: (b, i, 0)),
            prev_spec, cur_spec, next_spec, prev_spec, cur_spec, next_spec,
            pl.BlockSpec((1, TQ, D), lambda b, i: (b, i, 0)),
            pl.BlockSpec((NH * HD, D), lambda b, i: (0, 0)),
            _mod_spec(layer),
            pl.BlockSpec((1, HD), lambda b, i: (0, 0)),
            pl.BlockSpec((1, HD), lambda b, i: (0, 0)),
            pl.BlockSpec((NKV, GQA * BLK, 3 * BLK), lambda b, i: (0, 0, 0)),
            pl.BlockSpec((NKV, GQA * BLK, 1), lambda b, i: (0, 0, 0)),
        ],
        out_specs=pl.BlockSpec((1, TQ, D), lambda b, i: (b, i, 0)),
        out_shape=jax.ShapeDtypeStruct((B, S, D), F32),
        scratch_shapes=[pltpu.VMEM((TQ, NH * HD), F32)],
        compiler_params=_cparams(("arbitrary", "arbitrary")),
        name="attn_core",
    )(q, k, k, k, v, v, v, x, w_out_bf, mod_all, q_gain.reshape(1, HD), k_gain.reshape(1, HD),
      bias_tab, sink_tab)


RT_TM = 512
EP = 128
FFN_TF = 512
FFN_RT = 512


def _router_kernel(x_ref, mod_ref, gain_ref, wr_ref, h_ref, aff_ref):
    m = mod_ref[0, 0]
    h = _modulate(x_ref[0], gain_ref[0], m[3:4], m[4:5])
    h_ref[0] = h
    logits = jnp.dot(h, wr_ref[...], preferred_element_type=F32, precision=lax.Precision.HIGHEST)
    lt = logits.T[:E]
    mx = jnp.max(lt, axis=0, keepdims=True)
    ex = jnp.exp(lt - mx)
    aff_ref[0] = ex / jnp.sum(ex, axis=0, keepdims=True)


def _select_kernel(aff_ref, tri_ref, pos_ref):
    aff = aff_ref[...]
    bits = pltpu.bitcast(aff, jnp.int32)
    rows = aff.shape[0]

    def count_ge(v):
        return jnp.sum((bits >= v).astype(jnp.int32), axis=1, keepdims=True)

    def body(t, cur):
        cand = cur | (jnp.int32(1) << (30 - t))
        return jnp.where(count_ge(cand) >= CAP, cand, cur)

    thr = lax.fori_loop(0, 31, body, jnp.zeros((rows, 1), jnp.int32))
    gt = bits > thr
    eq = bits == thr
    need = CAP - jnp.sum(gt.astype(jnp.int32), axis=1, keepdims=True)
    tri = tri_ref[...]

    def cumsum_excl(mask_f):
        off = jnp.zeros((rows, 1), F32)
        outs = []
        for j in range(S // 128):
            mj = mask_f[:, j * 128:(j + 1) * 128]
            loc = jnp.dot(mj.astype(BF16), tri, preferred_element_type=F32)
            outs.append(loc - mj + off)
            off = off + loc[:, 127:128]
        return jnp.concatenate(outs, axis=1)

    eq_rank = cumsum_excl(eq.astype(F32))
    sel = gt | (eq & (eq_rank < need.astype(F32)))
    pos = cumsum_excl(sel.astype(F32))
    pos_ref[...] = jnp.where(sel, pos.astype(jnp.int32), -1)


SC_CORES = 2
SC_SUBCORES = 16
DISPATCH_ROWS = 64
COMBINE_RANGE = 1024
COMBINE_ROWS = 32
SLAB = 128
NSLAB = D // SLAB


def _sc_mesh():
    return plsc.VectorSubcoreMesh(core_axis_name="c", subcore_axis_name="s",
                                  num_cores=SC_CORES, num_subcores=SC_SUBCORES)


def _dispatch_body(pos_hbm, aff_hbm, h_hbm, xin_hbm, idx_hbm, gate_hbm, pos_v, aff_v, idx_v, gate_v, rows_v, sem):
    w = lax.axis_index("s") * SC_CORES + lax.axis_index("c")
    base = (w // E) * S
    pltpu.sync_copy(pos_hbm.at[w], pos_v)
    pltpu.sync_copy(aff_hbm.at[w], aff_v)

    @pl.loop(0, S // 16)
    def _(i):
        p = pos_v[pl.ds(i * 16, 16)]
        m = p >= 0
        tok = lax.iota(jnp.int32, 16) + (i * 16 + base)
        plsc.store_scatter(idx_v, [p], tok, mask=m)
        plsc.store_scatter(gate_v, [p], aff_v[pl.ds(i * 16, 16)], mask=m)

    pltpu.sync_copy(idx_v, idx_hbm.at[w])
    pltpu.sync_copy(gate_v, gate_hbm.at[w])

    @pl.loop(0, CAP // DISPATCH_ROWS)
    def _(j):
        pltpu.async_copy(h_hbm.at[idx_v.at[pl.ds(j * DISPATCH_ROWS, DISPATCH_ROWS)]], rows_v, sem).wait()
        pltpu.sync_copy(rows_v, xin_hbm.at[w, pl.ds(j * DISPATCH_ROWS, DISPATCH_ROWS)])


def _dispatch(pos, aff, h):
    return pl.kernel(
        _dispatch_body, mesh=_sc_mesh(),
        out_type=[jax.ShapeDtypeStruct((B * E, CAP, D), F32),
                  jax.ShapeDtypeStruct((B * E, CAP), jnp.int32),
                  jax.ShapeDtypeStruct((B * E, CAP), F32)],
        scratch_types=[pltpu.VMEM((S,), jnp.int32), pltpu.VMEM((S,), F32),
                       pltpu.VMEM((CAP,), jnp.int32), pltpu.VMEM((CAP,), F32),
                       pltpu.VMEM((DISPATCH_ROWS, D), F32), pltpu.SemaphoreType.DMA],
        compiler_params=pltpu.CompilerParams(needs_layout_passes=False),
        name="moe_dispatch",
    )(pos, aff, h)


def _combine_body(y_hbm, idx_hbm, x_hbm, out_hbm, idx_v, li_v, *bufs):
    rows = bufs[:NSLAB]
    accs = bufs[NSLAB:]
    b = lax.axis_index("c")
    s = lax.axis_index("s")
    w = b * E + s
    share = COMBINE_RANGE // SC_SUBCORES
    pltpu.sync_copy(idx_hbm.at[w], idx_v)
    lane = lax.iota(jnp.int32, 16)

    @pl.loop(0, S // COMBINE_RANGE)
    def _(r):
        t0 = b * S + r * COMBINE_RANGE
        row0 = t0 + s * share
        pltpu.sync_copy(tuple(x_hbm.at[pl.ds(row0, share), pl.ds(q * SLAB, SLAB)] for q in range(NSLAB)),
                        tuple(accs[q].at[pl.ds(s * share, share)] for q in range(NSLAB)))
        plsc.subcore_barrier()

        @pl.loop(0, CAP // COMBINE_ROWS)
        def _(j):
            hits = jnp.zeros((16,), jnp.int32)
            for v in range(COMBINE_ROWS // 16):
                t = idx_v[pl.ds(j * COMBINE_ROWS + v * 16, 16)] - t0
                ok = (t >= 0) & (t < COMBINE_RANGE)
                li_v[pl.ds(v * 16, 16)] = jnp.where(ok, t, COMBINE_RANGE + lane)
                hits = hits + plsc.all_reduce_population_count(ok)

            @pl.when(jnp.max(hits) > 0)
            def _():
                pltpu.sync_copy(
                    tuple(y_hbm.at[w, pl.ds(j * COMBINE_ROWS, COMBINE_ROWS), pl.ds(q * SLAB, SLAB)]
                          for q in range(NSLAB)),
                    tuple(rows))
                pltpu.sync_copy(tuple(rows), tuple(accs[q].at[li_v] for q in range(NSLAB)), add=True)

        plsc.subcore_barrier()
        pltpu.sync_copy(tuple(accs[q].at[pl.ds(s * share, share)] for q in range(NSLAB)),
                        tuple(out_hbm.at[pl.ds(row0, share), pl.ds(q * SLAB, SLAB)] for q in range(NSLAB)))


def _combine(y, idx, x):
    return pl.kernel(
        _combine_body, mesh=_sc_mesh(),
        out_type=jax.ShapeDtypeStruct((B * S, D), F32),
        scratch_types=[pltpu.VMEM((CAP,), jnp.int32), pltpu.VMEM((COMBINE_ROWS,), jnp.int32)]
        + [pltpu.VMEM((COMBINE_ROWS, SLAB), F32) for _ in range(NSLAB)]
        + [pltpu.VMEM_SHARED((COMBINE_RANGE + 16, SLAB), F32) for _ in range(NSLAB)],
        compiler_params=pltpu.CompilerParams(needs_layout_passes=False),
        name="moe_combine",
    )(y, idx, x)


def _ffn_kernel(x_ref, wg_ref, wu_ref, wd_ref, gate_ref, mod_ref, o_ref):
    f = pl.program_id(1)
    wg = wg_ref[0, 0].astype(BF16)
    wu = wu_ref[0, 0].astype(BF16)
    wd = wd_ref[0, 0].astype(BF16)
    for b in range(B):
        for r in range(CAP // FFN_RT):
            rows = slice(r * FFN_RT, (r + 1) * FFN_RT)
            xr = x_ref[b, 0, rows, :].astype(BF16)
            g = jnp.dot(xr, wg, preferred_element_type=F32)
            u = jnp.dot(xr, wu, preferred_element_type=F32)
            a = (g * _sigmoid(g) * u).astype(BF16)
            y = jnp.dot(a, wd, preferred_element_type=F32)

            @pl.when(f == 0)
            def _():
                o_ref[b, 0, rows, :] = y

            @pl.when(f > 0)
            def _():
                o_ref[b, 0, rows, :] += y

    @pl.when(f == pl.num_programs(1) - 1)
    def _():
        for b in range(B):
            g2 = mod_ref[0, b][5:6]
            o_ref[b, 0] = o_ref[b, 0] * gate_ref[b, 0] * g2


def _moe_layer(x, mod_all, gain, w_router, w_gate, w_up, w_down, layer, tri):
    wr = jnp.pad(w_router, ((0, 0), (0, EP - E)))
    h, aff = pl.pallas_call(
        _router_kernel,
        grid=(B, S // RT_TM),
        in_specs=[
            pl.BlockSpec((1, RT_TM, D), lambda b, i: (b, i, 0)),
            _mod_spec(layer), _gain_spec(layer),
            pl.BlockSpec((D, EP), lambda b, i: (0, 0)),
        ],
        out_specs=[
            pl.BlockSpec((1, RT_TM, D), lambda b, i: (b, i, 0)),
            pl.BlockSpec((1, E, RT_TM), lambda b, i: (b, 0, i)),
        ],
        out_shape=[
            jax.ShapeDtypeStruct((B, S, D), F32),
            jax.ShapeDtypeStruct((B, E, S), F32),
        ],
        compiler_params=_cparams(("arbitrary", "arbitrary")),
        name="moe_router",
    )(x, mod_all, gain, wr)

    aff2 = aff.reshape(B * E, S)
    pos = pl.pallas_call(
        _select_kernel,
        grid=(1,),
        in_specs=[
            pl.BlockSpec((B * E, S), lambda i: (0, 0)),
            pl.BlockSpec((128, 128), lambda i: (0, 0)),
        ],
        out_specs=pl.BlockSpec((B * E, S), lambda i: (0, 0)),
        out_shape=jax.ShapeDtypeStruct((B * E, S), jnp.int32),
        compiler_params=_cparams(("arbitrary",)),
        name="moe_select",
    )(aff2, tri)

    xin, idx, gate = _dispatch(pos, aff2, h.reshape(B * S, D))

    y = pl.pallas_call(
        _ffn_kernel,
        grid=(E, F // FFN_TF),
        in_specs=[
            pl.BlockSpec((B, 1, CAP, D), lambda e, f: (0, e, 0, 0)),
            pl.BlockSpec((1, 1, D, FFN_TF), lambda e, f: (layer, e, 0, f)),
            pl.BlockSpec((1, 1, D, FFN_TF), lambda e, f: (layer, e, 0, f)),
            pl.BlockSpec((1, 1, FFN_TF, D), lambda e, f: (layer, e, f, 0)),
            pl.BlockSpec((B, 1, CAP, 1), lambda e, f: (0, e, 0, 0)),
            pl.BlockSpec((1, B, 6, D), lambda e, f: (layer, 0, 0, 0)),
        ],
        out_specs=pl.BlockSpec((B, 1, CAP, D), lambda e, f: (0, e, 0, 0)),
        out_shape=jax.ShapeDtypeStruct((B, E, CAP, D), F32),
        compiler_params=_cparams(("arbitrary", "arbitrary"), 56),
        name="moe_ffn",
    )(xin.reshape(B, E, CAP, D), w_gate, w_up, w_down, gate.reshape(B, E, CAP, 1), mod_all)

    out = _combine(y.reshape(B * E, CAP, D), idx, x.reshape(B * S, D))
    return out.reshape(B, S, D)


def kernel(x, c, w_ada, b_ada, norm_mix, norm_ffn, w_fourier_out, w_qkv, w_attn_out, q_gain, k_gain,
           sink, rel_bias, w_router, w_gate, w_up, w_down):
    m0, m2 = _dft_tables()
    m1 = _stage1_table()
    onehot, neg = _bucket_table()
    tri = jnp.asarray(np.triu(np.ones((128, 128), np.float32)), BF16)
    mod_all = _ada(c, w_ada, b_ada)
    bias_tab = _bias_table(rel_bias, onehot, neg)
    gain_mix = norm_mix.reshape(DEPTH, 1, D)
    gain_ffn = norm_ffn.reshape(DEPTH, 1, D)
    for layer in range(DEPTH):
        j = layer // 2
        if layer % 2 == 0:
            x = _fourier_layer(x, mod_all, gain_mix, w_fourier_out[j].astype(BF16), layer, m0, m1, m2)
        else:
            x = _attn_layer(x, mod_all, gain_mix, w_qkv[j].astype(BF16), w_attn_out[j].astype(BF16),
                            q_gain[j], k_gain[j], sink[j], bias_tab, layer)
        x = _moe_layer(x, mod_all, gain_ffn, w_router[layer], w_gate, w_up, w_down, layer, tri)
    return x
```

```python
import functools
import math

import numpy as np
import jax
import jax.numpy as jnp
from jax import lax
from jax.experimental import pallas as pl
from jax.experimental.pallas import tpu as pltpu
from jax.experimental.pallas import tpu_sc as plsc

D = 1024
B = 2
S = 8192
DEPTH = 4
GROUPS = 4
GD = D // GROUPS
HD = 64
NH = 16
NKV = 4
GQA = NH // NKV
WINDOW = 128
BLK = 128
NBUCKETS = 32
MAXDIST = 128
E = 16
CAP = 2 * S // E
F = 2 * D
EPS = 1e-6
NEG_INF = -1e30

N1 = 128
N2 = 64
INV_NORM = 1.0 / math.sqrt(S * GD)

F32 = jnp.float32
BF16 = jnp.bfloat16


def _cparams(sem, vmem_mb=48):
    return pltpu.CompilerParams(dimension_semantics=sem, vmem_limit_bytes=vmem_mb * 1024 * 1024)


def _dft_tables():
    c = np.arange(GD)
    ang0 = 2.0 * np.pi * ((c[:, None] * c[None, :]) % GD) / GD
    m0 = np.concatenate([np.cos(ang0), -np.sin(ang0)], axis=1)
    k2 = np.arange(N2)
    ang2 = 2.0 * np.pi * ((k2[:, None] * k2[None, :]) % N2) / N2
    m2 = np.concatenate([np.cos(ang2), np.sin(ang2)], axis=1)
    return jnp.asarray(m0, BF16), jnp.asarray(m2, BF16)


def _stage1_table():
    s2 = lax.broadcasted_iota(jnp.int32, (N2, N1, N1), 0)
    k1 = lax.broadcasted_iota(jnp.int32, (N2, N1, N1), 1)
    s1 = lax.broadcasted_iota(jnp.int32, (N2, N1, N1), 2)
    m = (k1 * (N2 * s1 + s2)) % S
    th = m.astype(F32) * (2.0 * math.pi / S)
    co, si = jnp.cos(th), jnp.sin(th)
    top = jnp.concatenate([co, si], axis=2)
    bot = jnp.concatenate([-si, co], axis=2)
    return jnp.concatenate([top, bot], axis=1).astype(BF16)


def _bucket_table():
    q_off = np.arange(BLK)
    k_off = np.arange(3 * BLK) - BLK
    rel = k_off[:, None] - q_off[None, :]
    half = NBUCKETS // 2
    max_exact = half // 2
    ret = np.where(rel > 0, half, 0)
    n = np.abs(rel)
    nf = np.maximum(n, 1).astype(np.float32)
    ratio = (np.log(nf / np.float32(max_exact)) / np.float32(math.log(MAXDIST / max_exact))).astype(np.float32)
    large = max_exact + (ratio * np.float32(half - max_exact)).astype(np.int32)
    large = np.minimum(large, half - 1)
    bucket = ret + np.where(n < max_exact, n, large)
    return jnp.asarray(np.where(np.abs(rel) <= WINDOW, bucket, -1).astype(np.int32))


def _modulate(x, gain, shift, scale):
    ms = jnp.mean(x * x, axis=-1, keepdims=True)
    return x * lax.rsqrt(ms + EPS) * (gain * (1.0 + scale)) + shift


def _sigmoid(x):
    return 1.0 / (1.0 + jnp.exp(-x))


ADA_TN = 768


def _ada_kernel(ct_ref, w_ref, b_ref, o_ref):
    ct = ct_ref[...]
    ca = ct * _sigmoid(ct)
    w = w_ref[0]
    for b in range(B):
        o_ref[0, b:b + 1, :] = jnp.sum(w * ca[:, b:b + 1], axis=0, keepdims=True) + b_ref[0]


def _ada(c, w_ada, b_ada):
    out = pl.pallas_call(
        _ada_kernel,
        grid=(DEPTH, 6 * D // ADA_TN),
        in_specs=[
            pl.BlockSpec((D, B), lambda l, j: (0, 0)),
            pl.BlockSpec((1, D, ADA_TN), lambda l, j: (l, 0, j)),
            pl.BlockSpec((1, 1, ADA_TN), lambda l, j: (l, 0, j)),
        ],
        out_specs=pl.BlockSpec((1, B, ADA_TN), lambda l, j: (l, 0, j)),
        out_shape=jax.ShapeDtypeStruct((DEPTH, B, 6 * D), F32),
        compiler_params=_cparams(("arbitrary", "arbitrary"), 32),
        name="ada_mod",
    )(c.T, w_ada, b_ada.reshape(DEPTH, 1, 6 * D))
    return out.reshape(DEPTH, B, 6, D)


def _mod_spec(layer):
    return pl.BlockSpec((1, 1, 6, D), lambda b, i: (layer, b, 0, 0))


def _gain_spec(layer):
    return pl.BlockSpec((1, 1, D), lambda b, i: (layer, 0, 0))


F0_TM = 512
F1_J = 8
F2_TK = 8
F3_M = 4


def _f0_kernel(x_ref, mod_ref, gain_ref, m0_ref, o_ref):
    m = mod_ref[0, 0]
    h = _modulate(x_ref[0], gain_ref[0], m[0:1], m[1:2]).astype(BF16)
    m0 = m0_ref[...]
    for g in range(GROUPS):
        r = jnp.dot(h[:, g * GD:(g + 1) * GD], m0, preferred_element_type=F32)
        o_ref[0, 0, :, g * GD:(g + 1) * GD] = r[:, :GD].astype(BF16)
        o_ref[0, 1, :, g * GD:(g + 1) * GD] = r[:, GD:].astype(BF16)


def _f1_kernel(w_ref, m1_ref, o_ref):
    for j in range(F1_J):
        o_ref[0, :, j * D:(j + 1) * D] = jnp.dot(
            m1_ref[j], w_ref[0, :, j * D:(j + 1) * D], preferred_element_type=F32).astype(BF16)


def _f2_kernel(u_ref, m2_ref, o_ref):
    m2 = m2_ref[...]
    for k in range(F2_TK):
        u = jnp.concatenate([u_ref[0, 0, k], u_ref[0, 1, k]], axis=0)
        o_ref[0, k] = jnp.dot(m2, u, preferred_element_type=F32).astype(BF16)


def _f3_kernel(mp_ref, x_ref, w_ref, mod_ref, o_ref):
    g1 = mod_ref[0, 0][2:3]
    a = jnp.concatenate([mp_ref[0, :, j * D:(j + 1) * D] for j in range(F3_M)], axis=0)
    y = jnp.dot(a, w_ref[...], preferred_element_type=F32)
    o_ref[0] = x_ref[0] + (g1 * INV_NORM) * y


def _fourier_layer(x, mod_all, gain, w_out_bf, layer, m0, m1, m2):
    wc = pl.pallas_call(
        _f0_kernel,
        grid=(B, S // F0_TM),
        in_specs=[
            pl.BlockSpec((1, F0_TM, D), lambda b, i: (b, i, 0)),
            _mod_spec(layer), _gain_spec(layer),
            pl.BlockSpec((GD, 2 * GD), lambda b, i: (0, 0)),
        ],
        out_specs=pl.BlockSpec((1, 2, F0_TM, D), lambda b, i: (b, 0, i, 0)),
        out_shape=jax.ShapeDtypeStruct((B, 2, S, D), BF16),
        compiler_params=_cparams(("arbitrary", "arbitrary")),
        name="fourier_chan",
    )(x, mod_all, gain, m0)
    wc = wc.reshape(B, 2 * N1, N2 * D)
    u = pl.pallas_call(
        _f1_kernel,
        grid=(B, N2 // F1_J),
        in_specs=[
            pl.BlockSpec((1, 2 * N1, F1_J * D), lambda b, i: (b, 0, i)),
            pl.BlockSpec((F1_J, 2 * N1, 2 * N1), lambda b, i: (i, 0, 0)),
        ],
        out_specs=pl.BlockSpec((1, 2 * N1, F1_J * D), lambda b, i: (b, 0, i)),
        out_shape=jax.ShapeDtypeStruct((B, 2 * N1, N2 * D), BF16),
        compiler_params=_cparams(("arbitrary", "arbitrary")),
        name="fourier_seq1",
    )(wc, m1)
    u = u.reshape(B, 2, N1, N2, D)
    mp = pl.pallas_call(
        _f2_kernel,
        grid=(B, N1 // F2_TK),
        in_specs=[
            pl.BlockSpec((1, 2, F2_TK, N2, D), lambda b, i: (b, 0, i, 0, 0)),
            pl.BlockSpec((N2, 2 * N2), lambda b, i: (0, 0)),
        ],
        out_specs=pl.BlockSpec((1, F2_TK, N2, D), lambda b, i: (b, i, 0, 0)),
        out_shape=jax.ShapeDtypeStruct((B, N1, N2, D), BF16),
        compiler_params=_cparams(("arbitrary", "arbitrary")),
        name="fourier_seq2",
    )(u, m2)
    mp = mp.reshape(B, N1, N2 * D)
    return pl.pallas_call(
        _f3_kernel,
        grid=(B, N2 // F3_M),
        in_specs=[
            pl.BlockSpec((1, N1, F3_M * D), lambda b, i: (b, 0, i)),
            pl.BlockSpec((1, F3_M * N1, D), lambda b, i: (b, i, 0)),
            pl.BlockSpec((D, D), lambda b, i: (0, 0)),
            _mod_spec(layer),
        ],
        out_specs=pl.BlockSpec((1, F3_M * N1, D), lambda b, i: (b, i, 0)),
        out_shape=jax.ShapeDtypeStruct((B, S, D), F32),
        compiler_params=_cparams(("arbitrary", "arbitrary")),
        name="fourier_out",
    )(mp, x, w_out_bf, mod_all)


QKV_TM = 512
TQ = 256
NSB = TQ // BLK
KWIN = TQ + 2 * BLK
NBLK = S // BLK
KPAD = 128
VROWS = HD + 16
ATT_AHEAD = 2


def _bias_kernel(rb_ref, bucket_ref, o_ref):
    h = pl.program_id(0)
    bucket = bucket_ref[...]
    acc = jnp.full(bucket.shape, NEG_INF, F32)
    for k in range(NBUCKETS):
        acc = jnp.where(bucket == k, rb_ref[k, h], acc)
    o_ref[0] = acc


def _bias_table(rel_bias, bucket_t):
    return pl.pallas_call(
        _bias_kernel,
        grid=(NH,),
        in_specs=[
            pl.BlockSpec(memory_space=pltpu.SMEM),
            pl.BlockSpec((3 * BLK, BLK), lambda h: (0, 0)),
        ],
        out_specs=pl.BlockSpec((1, 3 * BLK, BLK), lambda h: (h // GQA, 0, h % GQA)),
        out_shape=jax.ShapeDtypeStruct((NKV, 3 * BLK, GQA * BLK), F32),
        compiler_params=_cparams(("arbitrary",)),
        name="rel_bias_table",
    )(rel_bias, bucket_t)


def _qkv_kernel(x_ref, mod_ref, gain_ref, wqt_ref, wk_ref, wvt_ref, qg_ref, kg_ref, qt_ref, k_ref, vt_ref):
    m = mod_ref[0, 0]
    h = _modulate(x_ref[0], gain_ref[0], m[0:1], m[1:2]).astype(BF16)
    nt = (((1,), (1,)), ((), ()))
    qt = lax.dot_general(wqt_ref[...], h, nt, preferred_element_type=F32)
    tm = qt.shape[1]
    q3 = qt.reshape(NH, HD, tm)
    q3 = q3 * lax.rsqrt(jnp.mean(q3 * q3, axis=1, keepdims=True) + EPS)
    qt_ref[0] = (q3.reshape(NH * HD, tm) * qg_ref[...]).astype(BF16)
    k = jnp.dot(h, wk_ref[...], preferred_element_type=F32)
    for g in range(NKV):
        kg = k[:, g * KPAD:(g + 1) * KPAD]
        ms = jnp.sum(kg * kg, axis=-1, keepdims=True) * (1.0 / HD)
        k_ref[0, :, g * KPAD:(g + 1) * KPAD] = (kg * lax.rsqrt(ms + EPS) * kg_ref[...]).astype(BF16)
    vt_ref[0] = lax.dot_general(wvt_ref[...], h, nt, preferred_element_type=F32).astype(BF16)


def _attn_kernel(qt_ref, kp_ref, kc_ref, kn_ref, vp_ref, vc_ref, vn_ref, x_ref, w_ref, mod_ref,
                 bias_ref, sink_ref, o_ref, att_ref):
    i = pl.program_id(1)
    kwin = jnp.concatenate([kp_ref[0], kc_ref[0], kn_ref[0]], axis=0)
    vwin = jnp.concatenate([vp_ref[0], vc_ref[0], vn_ref[0]], axis=1)
    ones_rows = (lax.broadcasted_iota(jnp.int32, (VROWS - HD, KWIN), 0) == 0).astype(BF16)
    vaug = [jnp.concatenate([vwin[g * HD:(g + 1) * HD], ones_rows], axis=0) for g in range(NKV)]
    row = lax.broadcasted_iota(jnp.int32, (3 * BLK, 1), 0)
    items = [(g, r) for g in range(NKV) for r in range(NSB)]

    def scores(g, r):
        kr = kwin[r * BLK:r * BLK + 3 * BLK, g * KPAD:g * KPAD + HD]
        qg = jnp.concatenate(
            [qt_ref[0, (GQA * g + hh) * HD:(GQA * g + hh + 1) * HD, r * BLK:(r + 1) * BLK] for hh in range(GQA)],
            axis=1)
        return jnp.dot(kr, qg, preferred_element_type=F32)

    def probs(s, g, r):
        sink = sink_ref[g]
        s = s + bias_ref[g]
        key_pos = i * TQ + (r - 1) * BLK + row
        s = jnp.where((key_pos >= 0) & (key_pos < S), s, NEG_INF)
        mx = jnp.maximum(jnp.max(s, axis=0, keepdims=True), sink)
        return jnp.exp(s - mx).astype(BF16), jnp.exp(sink - mx)

    pending = [scores(*items[n]) for n in range(ATT_AHEAD)]
    for n, (g, r) in enumerate(items):
        s = pending.pop(0)
        if n + ATT_AHEAD < len(items):
            pending.append(scores(*items[n + ATT_AHEAD]))
        p, psink = probs(s, g, r)
        ot = jnp.dot(vaug[g][:, r * BLK:r * BLK + 3 * BLK], p, preferred_element_type=F32)
        ot = ot[:HD] * (1.0 / (ot[HD:HD + 1] + psink))
        for hh in range(GQA):
            hd0 = (GQA * g + hh) * HD
            att_ref[hd0:hd0 + HD, r * BLK:(r + 1) * BLK] = ot[:, hh * BLK:(hh + 1) * BLK]
    g1 = mod_ref[0, 0][2:3]
    att = att_ref[...].T.astype(BF16)
    y = jnp.dot(att, w_ref[...], preferred_element_type=F32)
    o_ref[0] = x_ref[0] + g1 * y


def _attn_layer(x, mod_all, gain, w_qkv, w_out_bf, q_gain, k_gain, sink, bias_tab, layer):
    wq, wk, wv = w_qkv[:, :NH * HD], w_qkv[:, NH * HD:(NH + NKV) * HD], w_qkv[:, (NH + NKV) * HD:]
    wqt = wq.T.astype(BF16)
    wvt = wv.T.astype(BF16)
    wk_pad = jnp.pad(wk.reshape(D, NKV, HD), ((0, 0), (0, 0), (0, KPAD - HD))).reshape(D, NKV * KPAD).astype(BF16)
    qg_col = jnp.tile(q_gain * (HD ** -0.5), NH).reshape(NH * HD, 1)
    kg_row = jnp.pad(k_gain, (0, KPAD - HD)).reshape(1, KPAD)
    qt, k, vt = pl.pallas_call(
        _qkv_kernel,
        grid=(B, S // QKV_TM),
        in_specs=[
            pl.BlockSpec((1, QKV_TM, D), lambda b, i: (b, i, 0)),
            _mod_spec(layer), _gain_spec(layer),
            pl.BlockSpec((NH * HD, D), lambda b, i: (0, 0)),
            pl.BlockSpec((D, NKV * KPAD), lambda b, i: (0, 0)),
            pl.BlockSpec((NKV * HD, D), lambda b, i: (0, 0)),
            pl.BlockSpec((NH * HD, 1), lambda b, i: (0, 0)),
            pl.BlockSpec((1, KPAD), lambda b, i: (0, 0)),
        ],
        out_specs=[
            pl.BlockSpec((1, NH * HD, QKV_TM), lambda b, i: (b, 0, i)),
            pl.BlockSpec((1, QKV_TM, NKV * KPAD), lambda b, i: (b, i, 0)),
            pl.BlockSpec((1, NKV * HD, QKV_TM), lambda b, i: (b, 0, i)),
        ],
        out_shape=[
            jax.ShapeDtypeStruct((B, NH * HD, S), BF16),
            jax.ShapeDtypeStruct((B, S, NKV * KPAD), BF16),
            jax.ShapeDtypeStruct((B, NKV * HD, S), BF16),
        ],
        compiler_params=_cparams(("arbitrary", "arbitrary")),
        name="attn_qkv",
    )(x, mod_all, gain, wqt, wk_pad, wvt, qg_col, kg_row)

    kw = NKV * KPAD
    vw = NKV * HD
    kprev = pl.BlockSpec((1, BLK, kw), lambda b, i: (b, jnp.maximum(i * NSB - 1, 0), 0))
    kcur = pl.BlockSpec((1, TQ, kw), lambda b, i: (b, i, 0))
    knext = pl.BlockSpec((1, BLK, kw), lambda b, i: (b, jnp.minimum((i + 1) * NSB, NBLK - 1), 0))
    vprev = pl.BlockSpec((1, vw, BLK), lambda b, i: (b, 0, jnp.maximum(i * NSB - 1, 0)))
    vcur = pl.BlockSpec((1, vw, TQ), lambda b, i: (b, 0, i))
    vnext = pl.BlockSpec((1, vw, BLK), lambda b, i: (b, 0, jnp.minimum((i + 1) * NSB, NBLK - 1)))
    sink_row = jnp.repeat(sink, BLK).reshape(NKV, 1, GQA * BLK)
    return pl.pallas_call(
        _attn_kernel,
        grid=(B, S // TQ),
        in_specs=[
            pl.BlockSpec((1, NH * HD, TQ), lambda b, i: (b, 0, i)),
            kprev, kcur, knext, vprev, vcur, vnext,
            pl.BlockSpec((1, TQ, D), lambda b, i: (b, i, 0)),
            pl.BlockSpec((NH * HD, D), lambda b, i: (0, 0)),
            _mod_spec(layer),
            pl.BlockSpec((NKV, 3 * BLK, GQA * BLK), lambda b, i: (0, 0, 0)),
            pl.BlockSpec((NKV, 1, GQA * BLK), lambda b, i: (0, 0, 0)),
        ],
        out_specs=pl.BlockSpec((1, TQ, D), lambda b, i: (b, i, 0)),
        out_shape=jax.ShapeDtypeStruct((B, S, D), F32),
        scratch_shapes=[pltpu.VMEM((NH * HD, TQ), F32)],
        compiler_params=_cparams(("arbitrary", "arbitrary")),
        name="attn_core",
    )(qt, k, k, k, vt, vt, vt, x, w_out_bf, mod_all, bias_tab, sink_row)


RT_TM = 512
EP = 128
FFN_TF = 512
FFN_RT = 512


def _router_kernel(x_ref, mod_ref, gain_ref, wr_ref, h_ref, aff_ref):
    m = mod_ref[0, 0]
    h = _modulate(x_ref[0], gain_ref[0], m[3:4], m[4:5])
    h_ref[0] = _pack_bf16_pairs(h)
    logits = jnp.dot(h, wr_ref[...], preferred_element_type=F32, precision=lax.Precision.HIGHEST)
    lt = logits.T[:E]
    mx = jnp.max(lt, axis=0, keepdims=True)
    ex = jnp.exp(lt - mx)
    aff_ref[0] = ex / jnp.sum(ex, axis=0, keepdims=True)


def _select_kernel(aff_ref, tri_ref, pos_ref):
    aff = aff_ref[...]
    bits = pltpu.bitcast(aff, jnp.int32)
    rows = aff.shape[0]

    def count_ge(v):
        return jnp.sum((bits >= v).astype(jnp.int32), axis=1, keepdims=True)

    def body(t, cur):
        cand = cur | (jnp.int32(1) << (30 - t))
        return jnp.where(count_ge(cand) >= CAP, cand, cur)

    thr = lax.fori_loop(0, 31, body, jnp.zeros((rows, 1), jnp.int32))
    gt = bits > thr
    eq = bits == thr
    need = CAP - jnp.sum(gt.astype(jnp.int32), axis=1, keepdims=True)
    tri = tri_ref[...]

    def cumsum_excl(mask_f):
        off = jnp.zeros((rows, 1), F32)
        outs = []
        for j in range(S // 128):
            mj = mask_f[:, j * 128:(j + 1) * 128]
            loc = jnp.dot(mj.astype(BF16), tri, preferred_element_type=F32)
            outs.append(loc - mj + off)
            off = off + loc[:, 127:128]
        return jnp.concatenate(outs, axis=1)

    eq_rank = cumsum_excl(eq.astype(F32))
    sel = gt | (eq & (eq_rank < need.astype(F32)))
    pos = cumsum_excl(sel.astype(F32))
    pos_ref[...] = jnp.where(sel, pos.astype(jnp.int32), -1)


SC_CORES = 2
SC_SUBCORES = 16
DISPATCH_ROWS = 64
COMBINE_RANGE = 1024
COMBINE_ROWS = 32
SLAB = 128
NSLAB = D // SLAB


def _sc_mesh():
    return plsc.VectorSubcoreMesh(core_axis_name="c", subcore_axis_name="s",
                                  num_cores=SC_CORES, num_subcores=SC_SUBCORES)


def _dispatch_body(pos_hbm, aff_hbm, h_hbm, xin_hbm, idx_hbm, gate_hbm, pos_v, aff_v, idx_v, gate_v,
                   rows_a, rows_b, gsem_a, gsem_b, wsem_a, wsem_b):
    w = lax.axis_index("s") * SC_CORES + lax.axis_index("c")
    base = (w // E) * S
    pltpu.sync_copy(pos_hbm.at[w], pos_v)
    pltpu.sync_copy(aff_hbm.at[w], aff_v)

    @pl.loop(0, S // 16)
    def _(i):
        p = pos_v[pl.ds(i * 16, 16)]
        m = p >= 0
        tok = lax.iota(jnp.int32, 16) + (i * 16 + base)
        plsc.store_scatter(idx_v, [p], tok, mask=m)
        plsc.store_scatter(gate_v, [p], aff_v[pl.ds(i * 16, 16)], mask=m)

    pltpu.sync_copy(idx_v, idx_hbm.at[w])
    pltpu.sync_copy(gate_v, gate_hbm.at[w])

    bufs, gsems, wsems = (rows_a, rows_b), (gsem_a, gsem_b), (wsem_a, wsem_b)
    nchunk = CAP // DISPATCH_ROWS

    def gather(j):
        return pltpu.async_copy(h_hbm.at[idx_v.at[pl.ds(j * DISPATCH_ROWS, DISPATCH_ROWS)]], bufs[j % 2], gsems[j % 2])

    pending_gather = gather(0)
    writes = [None, None]
    for j in range(nchunk):
        pending_gather.wait()
        writes[j % 2] = pltpu.async_copy(bufs[j % 2], xin_hbm.at[w, pl.ds(j * DISPATCH_ROWS, DISPATCH_ROWS)],
                                         wsems[j % 2])
        if j + 1 < nchunk:
            if writes[(j + 1) % 2] is not None:
                writes[(j + 1) % 2].wait()
            pending_gather = gather(j + 1)
    writes[(nchunk - 2) % 2].wait()
    writes[(nchunk - 1) % 2].wait()


def _dispatch(pos, aff, h):
    return pl.kernel(
        _dispatch_body, mesh=_sc_mesh(),
        out_type=[jax.ShapeDtypeStruct((B * E, CAP, D // 2), jnp.int32),
                  jax.ShapeDtypeStruct((B * E, CAP), jnp.int32),
                  jax.ShapeDtypeStruct((B * E, CAP), F32)],
        scratch_types=[pltpu.VMEM((S,), jnp.int32), pltpu.VMEM((S,), F32),
                       pltpu.VMEM((CAP,), jnp.int32), pltpu.VMEM((CAP,), F32),
                       pltpu.VMEM((DISPATCH_ROWS, D // 2), jnp.int32), pltpu.VMEM((DISPATCH_ROWS, D // 2), jnp.int32),
                       pltpu.SemaphoreType.DMA, pltpu.SemaphoreType.DMA,
                       pltpu.SemaphoreType.DMA, pltpu.SemaphoreType.DMA],
        compiler_params=pltpu.CompilerParams(needs_layout_passes=False),
        name="moe_dispatch",
    )(pos, aff, h)


def _combine_body(y_hbm, idx_hbm, x_hbm, out_hbm, idx_v, li_v, *bufs):
    rows = bufs[:NSLAB]
    accs = bufs[NSLAB:]
    b = lax.axis_index("c")
    s = lax.axis_index("s")
    w = b * E + s
    share = COMBINE_RANGE // SC_SUBCORES
    pltpu.sync_copy(idx_hbm.at[w], idx_v)
    lane = lax.iota(jnp.int32, 16)

    @pl.loop(0, S // COMBINE_RANGE)
    def _(r):
        t0 = b * S + r * COMBINE_RANGE
        row0 = t0 + s * share
        pltpu.sync_copy(tuple(x_hbm.at[pl.ds(row0, share), pl.ds(q * SLAB, SLAB)] for q in range(NSLAB)),
                        tuple(accs[q].at[pl.ds(s * share, share)] for q in range(NSLAB)))
        plsc.subcore_barrier()

        @pl.loop(0, CAP // COMBINE_ROWS)
        def _(j):
            hits = jnp.zeros((16,), jnp.int32)
            for v in range(COMBINE_ROWS // 16):
                t = idx_v[pl.ds(j * COMBINE_ROWS + v * 16, 16)] - t0
                ok = (t >= 0) & (t < COMBINE_RANGE)
                li_v[pl.ds(v * 16, 16)] = jnp.where(ok, t, COMBINE_RANGE + lane)
                hits = hits + plsc.all_reduce_population_count(ok)

            @pl.when(jnp.max(hits) > 0)
            def _():
                pltpu.sync_copy(
                    tuple(y_hbm.at[w, pl.ds(j * COMBINE_ROWS, COMBINE_ROWS), pl.ds(q * SLAB, SLAB)]
                          for q in range(NSLAB)),
                    tuple(rows))
                pltpu.sync_copy(tuple(rows), tuple(accs[q].at[li_v] for q in range(NSLAB)), add=True)

        plsc.subcore_barrier()
        pltpu.sync_copy(tuple(accs[q].at[pl.ds(s * share, share)] for q in range(NSLAB)),
                        tuple(out_hbm.at[pl.ds(row0, share), pl.ds(q * SLAB, SLAB)] for q in range(NSLAB)))


def _combine(y, idx, x):
    return pl.kernel(
        _combine_body, mesh=_sc_mesh(),
        out_type=jax.ShapeDtypeStruct((B * S, D), F32),
        scratch_types=[pltpu.VMEM((CAP,), jnp.int32), pltpu.VMEM((COMBINE_ROWS,), jnp.int32)]
        + [pltpu.VMEM((COMBINE_ROWS, SLAB), F32) for _ in range(NSLAB)]
        + [pltpu.VMEM_SHARED((COMBINE_RANGE + 16, SLAB), F32) for _ in range(NSLAB)],
        compiler_params=pltpu.CompilerParams(needs_layout_passes=False),
        name="moe_combine",
    )(y, idx, x)


def _pack_bf16_pairs(h):
    hb = h.astype(BF16).astype(F32)
    lo = pltpu.bitcast(hb[:, :D // 2], jnp.int32)
    hi = pltpu.bitcast(hb[:, D // 2:], jnp.int32)
    return hi | lax.shift_right_logical(lo, jnp.int32(16))


def _ffn_kernel(x_ref, wg_ref, wu_ref, wd_ref, gate_ref, mod_ref, o_ref, xb_ref):
    f = pl.program_id(1)
    last = pl.num_programs(1) - 1

    @pl.when(f == 0)
    def _():
        for b in range(B):
            xp = x_ref[b, 0]
            xb_ref[b * CAP:(b + 1) * CAP, :D // 2] = pltpu.bitcast(xp << 16, F32).astype(BF16)
            xb_ref[b * CAP:(b + 1) * CAP, D // 2:] = pltpu.bitcast(xp & jnp.int32(-65536), F32).astype(BF16)

    wg = wg_ref[0, 0].astype(BF16)
    wu = wu_ref[0, 0].astype(BF16)
    wd = wd_ref[0, 0].astype(BF16)
    for b in range(B):
        g2 = mod_ref[0, b][5:6]
        for r in range(CAP // FFN_RT):
            rows = slice(r * FFN_RT, (r + 1) * FFN_RT)
            xr = xb_ref[b * CAP + r * FFN_RT:b * CAP + (r + 1) * FFN_RT, :]
            g = jnp.dot(xr, wg, preferred_element_type=F32)
            u = jnp.dot(xr, wu, preferred_element_type=F32)
            a = (g * _sigmoid(g) * u).astype(BF16)
            y = jnp.dot(a, wd, preferred_element_type=F32)

            @pl.when(f == 0)
            def _():
                o_ref[b, 0, rows, :] = y

            @pl.when((f > 0) & (f < last))
            def _():
                o_ref[b, 0, rows, :] += y

            @pl.when(f == last)
            def _():
                o_ref[b, 0, rows, :] = (o_ref[b, 0, rows, :] + y) * gate_ref[b, 0, rows, :] * g2


def _moe_layer(x, mod_all, gain, w_router, w_gate, w_up, w_down, layer, tri):
    wr = jnp.pad(w_router, ((0, 0), (0, EP - E)))
    h, aff = pl.pallas_call(
        _router_kernel,
        grid=(B, S // RT_TM),
        in_specs=[
            pl.BlockSpec((1, RT_TM, D), lambda b, i: (b, i, 0)),
            _mod_spec(layer), _gain_spec(layer),
            pl.BlockSpec((D, EP), lambda b, i: (0, 0)),
        ],
        out_specs=[
            pl.BlockSpec((1, RT_TM, D // 2), lambda b, i: (b, i, 0)),
            pl.BlockSpec((1, E, RT_TM), lambda b, i: (b, 0, i)),
        ],
        out_shape=[
            jax.ShapeDtypeStruct((B, S, D // 2), jnp.int32),
            jax.ShapeDtypeStruct((B, E, S), F32),
        ],
        compiler_params=_cparams(("arbitrary", "arbitrary")),
        name="moe_router",
    )(x, mod_all, gain, wr)

    aff2 = aff.reshape(B * E, S)
    pos = pl.pallas_call(
        _select_kernel,
        grid=(1,),
        in_specs=[
            pl.BlockSpec((B * E, S), lambda i: (0, 0)),
            pl.BlockSpec((128, 128), lambda i: (0, 0)),
        ],
        out_specs=pl.BlockSpec((B * E, S), lambda i: (0, 0)),
        out_shape=jax.ShapeDtypeStruct((B * E, S), jnp.int32),
        compiler_params=_cparams(("arbitrary",)),
        name="moe_select",
    )(aff2, tri)

    xin, idx, gate = _dispatch(pos, aff2, h.reshape(B * S, D // 2))

    y = pl.pallas_call(
        _ffn_kernel,
        grid=(E, F // FFN_TF),
        in_specs=[
            pl.BlockSpec((B, 1, CAP, D // 2), lambda e, f: (0, e, 0, 0)),
            pl.BlockSpec((1, 1, D, FFN_TF), lambda e, f: (layer, e, 0, f)),
            pl.BlockSpec((1, 1, D, FFN_TF), lambda e, f: (layer, e, 0, f)),
            pl.BlockSpec((1, 1, FFN_TF, D), lambda e, f: (layer, e, f, 0)),
            pl.BlockSpec((B, 1, CAP, 1), lambda e, f: (0, e, 0, 0)),
            pl.BlockSpec((1, B, 6, D), lambda e, f: (layer, 0, 0, 0)),
        ],
        out_specs=pl.BlockSpec((B, 1, CAP, D), lambda e, f: (0, e, 0, 0)),
        out_shape=jax.ShapeDtypeStruct((B, E, CAP, D), F32),
        scratch_shapes=[pltpu.VMEM((B * CAP, D), BF16)],
        compiler_params=_cparams(("arbitrary", "arbitrary"), 56),
        name="moe_ffn",
    )(xin.reshape(B, E, CAP, D // 2), w_gate, w_up, w_down, gate.reshape(B, E, CAP, 1), mod_all)

    out = _combine(y.reshape(B * E, CAP, D), idx, x.reshape(B * S, D))
    return out.reshape(B, S, D)


def kernel(x, c, w_ada, b_ada, norm_mix, norm_ffn, w_fourier_out, w_qkv, w_attn_out, q_gain, k_gain,
           sink, rel_bias, w_router, w_gate, w_up, w_down):
    m0, m2 = _dft_tables()
    m1 = _stage1_table()
    bucket_t = _bucket_table()
    tri = jnp.asarray(np.triu(np.ones((128, 128), np.float32)), BF16)
    mod_all = _ada(c, w_ada, b_ada)
    bias_tab = _bias_table(rel_bias, bucket_t)
    gain_mix = norm_mix.reshape(DEPTH, 1, D)
    gain_ffn = norm_ffn.reshape(DEPTH, 1, D)
    for layer in range(DEPTH):
        j = layer // 2
        if layer % 2 == 0:
            x = _fourier_layer(x, mod_all, gain_mix, w_fourier_out[j].astype(BF16), layer, m0, m1, m2)
        else:
            x = _attn_layer(x, mod_all, gain_mix, w_qkv[j], w_attn_out[j].astype(BF16),
                            q_gain[j], k_gain[j], sink[j], bias_tab, layer)
        x = _moe_layer(x, mod_all, gain_ffn, w_router[layer], w_gate, w_up, w_down, layer, tri)
    return x
```

```python
import functools
import math

import numpy as np
import jax
import jax.numpy as jnp
from jax import lax
from jax.experimental import pallas as pl
from jax.experimental.pallas import tpu as pltpu
from jax.experimental.pallas import tpu_sc as plsc

D = 1024
B = 2
S = 8192
DEPTH = 4
GROUPS = 4
GD = D // GROUPS
HD = 64
NH = 16
NKV = 4
GQA = NH // NKV
WINDOW = 128
BLK = 128
NBUCKETS = 32
MAXDIST = 128
E = 16
CAP = 2 * S // E
F = 2 * D
EPS = 1e-6
NEG_INF = -1e30

N1 = 128
N2 = 64
INV_NORM = 1.0 / math.sqrt(S * GD)

F32 = jnp.float32
BF16 = jnp.bfloat16


def _cparams(sem, vmem_mb=48):
    return pltpu.CompilerParams(dimension_semantics=sem, vmem_limit_bytes=vmem_mb * 1024 * 1024)


def _dft_tables():
    c = np.arange(GD)
    ang0 = 2.0 * np.pi * ((c[:, None] * c[None, :]) % GD) / GD
    m0 = np.concatenate([np.cos(ang0), -np.sin(ang0)], axis=1)
    k2 = np.arange(N2)
    ang2 = 2.0 * np.pi * ((k2[:, None] * k2[None, :]) % N2) / N2
    m2 = np.concatenate([np.cos(ang2), np.sin(ang2)], axis=1)
    return jnp.asarray(m0, BF16), jnp.asarray(m2, BF16)


def _stage1_table():
    s2 = lax.broadcasted_iota(jnp.int32, (N2, N1, N1), 0)
    k1 = lax.broadcasted_iota(jnp.int32, (N2, N1, N1), 1)
    s1 = lax.broadcasted_iota(jnp.int32, (N2, N1, N1), 2)
    m = (k1 * (N2 * s1 + s2)) % S
    th = m.astype(F32) * (2.0 * math.pi / S)
    co, si = jnp.cos(th), jnp.sin(th)
    top = jnp.concatenate([co, si], axis=2)
    bot = jnp.concatenate([-si, co], axis=2)
    return jnp.concatenate([top, bot], axis=1).astype(BF16)


def _bucket_table():
    q_off = np.arange(BLK)
    k_off = np.arange(3 * BLK) - BLK
    rel = k_off[:, None] - q_off[None, :]
    half = NBUCKETS // 2
    max_exact = half // 2
    ret = np.where(rel > 0, half, 0)
    n = np.abs(rel)
    nf = np.maximum(n, 1).astype(np.float32)
    ratio = (np.log(nf / np.float32(max_exact)) / np.float32(math.log(MAXDIST / max_exact))).astype(np.float32)
    large = max_exact + (ratio * np.float32(half - max_exact)).astype(np.int32)
    large = np.minimum(large, half - 1)
    bucket = ret + np.where(n < max_exact, n, large)
    return jnp.asarray(np.where(np.abs(rel) <= WINDOW, bucket, -1).astype(np.int32))


def _modulate(x, gain, shift, scale):
    ms = jnp.mean(x * x, axis=-1, keepdims=True)
    return x * lax.rsqrt(ms + EPS) * (gain * (1.0 + scale)) + shift


def _sigmoid(x):
    return 1.0 / (1.0 + jnp.exp(-x))


ADA_TN = 768


def _ada_kernel(ct_ref, w_ref, b_ref, o_ref):
    ct = ct_ref[...]
    ca = ct * _sigmoid(ct)
    w = w_ref[0]
    for b in range(B):
        o_ref[0, b:b + 1, :] = jnp.sum(w * ca[:, b:b + 1], axis=0, keepdims=True) + b_ref[0]


def _ada(c, w_ada, b_ada):
    out = pl.pallas_call(
        _ada_kernel,
        grid=(DEPTH, 6 * D // ADA_TN),
        in_specs=[
            pl.BlockSpec((D, B), lambda l, j: (0, 0)),
            pl.BlockSpec((1, D, ADA_TN), lambda l, j: (l, 0, j)),
            pl.BlockSpec((1, 1, ADA_TN), lambda l, j: (l, 0, j)),
        ],
        out_specs=pl.BlockSpec((1, B, ADA_TN), lambda l, j: (l, 0, j)),
        out_shape=jax.ShapeDtypeStruct((DEPTH, B, 6 * D), F32),
        compiler_params=_cparams(("arbitrary", "arbitrary"), 32),
        name="ada_mod",
    )(c.T, w_ada, b_ada.reshape(DEPTH, 1, 6 * D))
    return out.reshape(DEPTH, B, 6, D)


def _mod_spec(layer):
    return pl.BlockSpec((1, 1, 6, D), lambda b, i: (layer, b, 0, 0))


def _gain_spec(layer):
    return pl.BlockSpec((1, 1, D), lambda b, i: (layer, 0, 0))


F0_TM = 512
F1_J = 8
F2_TK = 8
F3_M = 4


def _f0_kernel(x_ref, mod_ref, gain_ref, m0_ref, o_ref):
    m = mod_ref[0, 0]
    h = _modulate(x_ref[0], gain_ref[0], m[0:1], m[1:2]).astype(BF16)
    m0 = m0_ref[...]
    for g in range(GROUPS):
        r = jnp.dot(h[:, g * GD:(g + 1) * GD], m0, preferred_element_type=F32)
        o_ref[0, 0, :, g * GD:(g + 1) * GD] = r[:, :GD].astype(BF16)
        o_ref[0, 1, :, g * GD:(g + 1) * GD] = r[:, GD:].astype(BF16)


def _f1_kernel(w_ref, m1_ref, o_ref):
    for j in range(F1_J):
        o_ref[0, :, j * D:(j + 1) * D] = jnp.dot(
            m1_ref[j], w_ref[0, :, j * D:(j + 1) * D], preferred_element_type=F32).astype(BF16)


def _f2_kernel(u_ref, m2_ref, o_ref):
    m2 = m2_ref[...]
    for k in range(F2_TK):
        u = jnp.concatenate([u_ref[0, 0, k], u_ref[0, 1, k]], axis=0)
        o_ref[0, k] = jnp.dot(m2, u, preferred_element_type=F32).astype(BF16)


def _f3_kernel(mp_ref, x_ref, w_ref, mod_ref, o_ref):
    g1 = mod_ref[0, 0][2:3]
    a = jnp.concatenate([mp_ref[0, :, j * D:(j + 1) * D] for j in range(F3_M)], axis=0)
    y = jnp.dot(a, w_ref[...], preferred_element_type=F32)
    o_ref[0] = x_ref[0] + (g1 * INV_NORM) * y


def _fourier_layer(x, mod_all, gain, w_out_bf, layer, m0, m1, m2):
    wc = pl.pallas_call(
        _f0_kernel,
        grid=(B, S // F0_TM),
        in_specs=[
            pl.BlockSpec((1, F0_TM, D), lambda b, i: (b, i, 0)),
            _mod_spec(layer), _gain_spec(layer),
            pl.BlockSpec((GD, 2 * GD), lambda b, i: (0, 0)),
        ],
        out_specs=pl.BlockSpec((1, 2, F0_TM, D), lambda b, i: (b, 0, i, 0)),
        out_shape=jax.ShapeDtypeStruct((B, 2, S, D), BF16),
        compiler_params=_cparams(("arbitrary", "arbitrary")),
        name="fourier_chan",
    )(x, mod_all, gain, m0)
    wc = wc.reshape(B, 2 * N1, N2 * D)
    u = pl.pallas_call(
        _f1_kernel,
        grid=(B, N2 // F1_J),
        in_specs=[
            pl.BlockSpec((1, 2 * N1, F1_J * D), lambda b, i: (b, 0, i)),
            pl.BlockSpec((F1_J, 2 * N1, 2 * N1), lambda b, i: (i, 0, 0)),
        ],
        out_specs=pl.BlockSpec((1, 2 * N1, F1_J * D), lambda b, i: (b, 0, i)),
        out_shape=jax.ShapeDtypeStruct((B, 2 * N1, N2 * D), BF16),
        compiler_params=_cparams(("arbitrary", "arbitrary")),
        name="fourier_seq1",
    )(wc, m1)
    u = u.reshape(B, 2, N1, N2, D)
    mp = pl.pallas_call(
        _f2_kernel,
        grid=(B, N1 // F2_TK),
        in_specs=[
            pl.BlockSpec((1, 2, F2_TK, N2, D), lambda b, i: (b, 0, i, 0, 0)),
            pl.BlockSpec((N2, 2 * N2), lambda b, i: (0, 0)),
        ],
        out_specs=pl.BlockSpec((1, F2_TK, N2, D), lambda b, i: (b, i, 0, 0)),
        out_shape=jax.ShapeDtypeStruct((B, N1, N2, D), BF16),
        compiler_params=_cparams(("arbitrary", "arbitrary")),
        name="fourier_seq2",
    )(u, m2)
    mp = mp.reshape(B, N1, N2 * D)
    return pl.pallas_call(
        _f3_kernel,
        grid=(B, N2 // F3_M),
        in_specs=[
            pl.BlockSpec((1, N1, F3_M * D), lambda b, i: (b, 0, i)),
            pl.BlockSpec((1, F3_M * N1, D), lambda b, i: (b, i, 0)),
            pl.BlockSpec((D, D), lambda b, i: (0, 0)),
            _mod_spec(layer),
        ],
        out_specs=pl.BlockSpec((1, F3_M * N1, D), lambda b, i: (b, i, 0)),
        out_shape=jax.ShapeDtypeStruct((B, S, D), F32),
        compiler_params=_cparams(("arbitrary", "arbitrary")),
        name="fourier_out",
    )(mp, x, w_out_bf, mod_all)


QKV_TM = 512
TQ = 256
NSB = TQ // BLK
KWIN = TQ + 2 * BLK
NBLK = S // BLK
KPAD = 128
VROWS = HD + 16
ATT_AHEAD = 2


def _bias_kernel(rb_ref, bucket_ref, o_ref):
    h = pl.program_id(0)
    bucket = bucket_ref[...]
    acc = jnp.full(bucket.shape, NEG_INF, F32)
    for k in range(NBUCKETS):
        acc = jnp.where(bucket == k, rb_ref[k, h], acc)
    o_ref[0] = acc


def _bias_table(rel_bias, bucket_t):
    return pl.pallas_call(
        _bias_kernel,
        grid=(NH,),
        in_specs=[
            pl.BlockSpec(memory_space=pltpu.SMEM),
            pl.BlockSpec((3 * BLK, BLK), lambda h: (0, 0)),
        ],
        out_specs=pl.BlockSpec((1, 3 * BLK, BLK), lambda h: (h // GQA, 0, h % GQA)),
        out_shape=jax.ShapeDtypeStruct((NKV, 3 * BLK, GQA * BLK), F32),
        compiler_params=_cparams(("arbitrary",)),
        name="rel_bias_table",
    )(rel_bias, bucket_t)


def _qkv_kernel(x_ref, mod_ref, gain_ref, wqt_ref, wk_ref, wvt_ref, qg_ref, kg_ref, qt_ref, k_ref, vt_ref):
    m = mod_ref[0, 0]
    h = _modulate(x_ref[0], gain_ref[0], m[0:1], m[1:2]).astype(BF16)
    nt = (((1,), (1,)), ((), ()))
    qt = lax.dot_general(wqt_ref[...], h, nt, preferred_element_type=F32)
    tm = qt.shape[1]
    q3 = qt.reshape(NH, HD, tm)
    q3 = q3 * lax.rsqrt(jnp.mean(q3 * q3, axis=1, keepdims=True) + EPS)
    qt_ref[0] = (q3.reshape(NH * HD, tm) * qg_ref[...]).astype(BF16)
    k = jnp.dot(h, wk_ref[...], preferred_element_type=F32)
    for g in range(NKV):
        kg = k[:, g * KPAD:(g + 1) * KPAD]
        ms = jnp.sum(kg * kg, axis=-1, keepdims=True) * (1.0 / HD)
        k_ref[0, :, g * KPAD:(g + 1) * KPAD] = (kg * lax.rsqrt(ms + EPS) * kg_ref[...]).astype(BF16)
    vt_ref[0] = lax.dot_general(wvt_ref[...], h, nt, preferred_element_type=F32).astype(BF16)


def _attn_kernel(qt_ref, kp_ref, kc_ref, kn_ref, vp_ref, vc_ref, vn_ref, x_ref, w_ref, mod_ref,
                 bias_ref, sink_ref, o_ref, att_ref):
    i = pl.program_id(1)
    kwin = jnp.concatenate([kp_ref[0], kc_ref[0], kn_ref[0]], axis=0)
    vwin = jnp.concatenate([vp_ref[0], vc_ref[0], vn_ref[0]], axis=1)
    ones_rows = (lax.broadcasted_iota(jnp.int32, (VROWS - HD, KWIN), 0) == 0).astype(BF16)
    vaug = [jnp.concatenate([vwin[g * HD:(g + 1) * HD], ones_rows], axis=0) for g in range(NKV)]
    row = lax.broadcasted_iota(jnp.int32, (3 * BLK, 1), 0)
    items = [(g, r) for g in range(NKV) for r in range(NSB)]

    def scores(g, r):
        kr = kwin[r * BLK:r * BLK + 3 * BLK, g * KPAD:g * KPAD + HD]
        qg = jnp.concatenate(
            [qt_ref[0, (GQA * g + hh) * HD:(GQA * g + hh + 1) * HD, r * BLK:(r + 1) * BLK] for hh in range(GQA)],
            axis=1)
        return jnp.dot(kr, qg, preferred_element_type=F32)

    def probs(s, g, r):
        sink = sink_ref[g]
        s = s + bias_ref[g]
        key_pos = i * TQ + (r - 1) * BLK + row
        s = jnp.where((key_pos >= 0) & (key_pos < S), s, NEG_INF)
        mx = jnp.maximum(jnp.max(s, axis=0, keepdims=True), sink)
        return jnp.exp(s - mx).astype(BF16), jnp.exp(sink - mx)

    pending = [scores(*items[n]) for n in range(ATT_AHEAD)]
    for n, (g, r) in enumerate(items):
        s = pending.pop(0)
        if n + ATT_AHEAD < len(items):
            pending.append(scores(*items[n + ATT_AHEAD]))
        p, psink = probs(s, g, r)
        ot = jnp.dot(vaug[g][:, r * BLK:r * BLK + 3 * BLK], p, preferred_element_type=F32)
        ot = ot[:HD] * (1.0 / (ot[HD:HD + 1] + psink))
        for hh in range(GQA):
            hd0 = (GQA * g + hh) * HD
            att_ref[hd0:hd0 + HD, r * BLK:(r + 1) * BLK] = ot[:, hh * BLK:(hh + 1) * BLK]
    g1 = mod_ref[0, 0][2:3]
    att = att_ref[...].T.astype(BF16)
    y = jnp.dot(att, w_ref[...], preferred_element_type=F32)
    o_ref[0] = x_ref[0] + g1 * y


def _attn_layer(x, mod_all, gain, w_qkv, w_out_bf, q_gain, k_gain, sink, bias_tab, layer):
    wq, wk, wv = w_qkv[:, :NH * HD], w_qkv[:, NH * HD:(NH + NKV) * HD], w_qkv[:, (NH + NKV) * HD:]
    wqt = wq.T.astype(BF16)
    wvt = wv.T.astype(BF16)
    wk_pad = jnp.pad(wk.reshape(D, NKV, HD), ((0, 0), (0, 0), (0, KPAD - HD))).reshape(D, NKV * KPAD).astype(BF16)
    qg_col = jnp.tile(q_gain * (HD ** -0.5), NH).reshape(NH * HD, 1)
    kg_row = jnp.pad(k_gain, (0, KPAD - HD)).reshape(1, KPAD)
    qt, k, vt = pl.pallas_call(
        _qkv_kernel,
        grid=(B, S // QKV_TM),
        in_specs=[
            pl.BlockSpec((1, QKV_TM, D), lambda b, i: (b, i, 0)),
            _mod_spec(layer), _gain_spec(layer),
            pl.BlockSpec((NH * HD, D), lambda b, i: (0, 0)),
            pl.BlockSpec((D, NKV * KPAD), lambda b, i: (0, 0)),
            pl.BlockSpec((NKV * HD, D), lambda b, i: (0, 0)),
            pl.BlockSpec((NH * HD, 1), lambda b, i: (0, 0)),
            pl.BlockSpec((1, KPAD), lambda b, i: (0, 0)),
        ],
        out_specs=[
            pl.BlockSpec((1, NH * HD, QKV_TM), lambda b, i: (b, 0, i)),
            pl.BlockSpec((1, QKV_TM, NKV * KPAD), lambda b, i: (b, i, 0)),
            pl.BlockSpec((1, NKV * HD, QKV_TM), lambda b, i: (b, 0, i)),
        ],
        out_shape=[
            jax.ShapeDtypeStruct((B, NH * HD, S), BF16),
            jax.ShapeDtypeStruct((B, S, NKV * KPAD), BF16),
            jax.ShapeDtypeStruct((B, NKV * HD, S), BF16),
        ],
        compiler_params=_cparams(("arbitrary", "arbitrary")),
        name="attn_qkv",
    )(x, mod_all, gain, wqt, wk_pad, wvt, qg_col, kg_row)

    kw = NKV * KPAD
    vw = NKV * HD
    kprev = pl.BlockSpec((1, BLK, kw), lambda b, i: (b, jnp.maximum(i * NSB - 1, 0), 0))
    kcur = pl.BlockSpec((1, TQ, kw), lambda b, i: (b, i, 0))
    knext = pl.BlockSpec((1, BLK, kw), lambda b, i: (b, jnp.minimum((i + 1) * NSB, NBLK - 1), 0))
    vprev = pl.BlockSpec((1, vw, BLK), lambda b, i: (b, 0, jnp.maximum(i * NSB - 1, 0)))
    vcur = pl.BlockSpec((1, vw, TQ), lambda b, i: (b, 0, i))
    vnext = pl.BlockSpec((1, vw, BLK), lambda b, i: (b, 0, jnp.minimum((i + 1) * NSB, NBLK - 1)))
    sink_row = jnp.repeat(sink, BLK).reshape(NKV, 1, GQA * BLK)
    return pl.pallas_call(
        _attn_kernel,
        grid=(B, S // TQ),
        in_specs=[
            pl.BlockSpec((1, NH * HD, TQ), lambda b, i: (b, 0, i)),
            kprev, kcur, knext, vprev, vcur, vnext,
            pl.BlockSpec((1, TQ, D), lambda b, i: (b, i, 0)),
            pl.BlockSpec((NH * HD, D), lambda b, i: (0, 0)),
            _mod_spec(layer),
            pl.BlockSpec((NKV, 3 * BLK, GQA * BLK), lambda b, i: (0, 0, 0)),
            pl.BlockSpec((NKV, 1, GQA * BLK), lambda b, i: (0, 0, 0)),
        ],
        out_specs=pl.BlockSpec((1, TQ, D), lambda b, i: (b, i, 0)),
        out_shape=jax.ShapeDtypeStruct((B, S, D), F32),
        scratch_shapes=[pltpu.VMEM((NH * HD, TQ), F32)],
        compiler_params=_cparams(("arbitrary", "arbitrary")),
        name="attn_core",
    )(qt, k, k, k, vt, vt, vt, x, w_out_bf, mod_all, bias_tab, sink_row)


RT_TM = 512
EP = 128
FFN_TF = 512
FFN_RT = 512


def _router_kernel(x_ref, mod_ref, gain_ref, wr_ref, h_ref, aff_ref):
    m = mod_ref[0, 0]
    h = _modulate(x_ref[0], gain_ref[0], m[3:4], m[4:5])
    h_ref[0] = _pack_bf16_pairs(h)
    h1 = h.astype(BF16)
    h2 = (h - h1.astype(F32)).astype(BF16)
    wr = wr_ref[...]
    part = jnp.dot(h1, wr, preferred_element_type=F32)
    logits = part[:, :EP] + part[:, EP:] + jnp.dot(h2, wr[:, :EP], preferred_element_type=F32)
    lt = logits.T[:E]
    mx = jnp.max(lt, axis=0, keepdims=True)
    ex = jnp.exp(lt - mx)
    aff_ref[0] = ex / jnp.sum(ex, axis=0, keepdims=True)


def _select_kernel(aff_ref, tri_ref, pos_ref):
    aff = aff_ref[...]
    bits = pltpu.bitcast(aff, jnp.int32)
    rows = aff.shape[0]

    def count_ge(v):
        return jnp.sum((bits >= v).astype(jnp.int32), axis=1, keepdims=True)

    def body(t, cur):
        cand = cur | (jnp.int32(1) << (30 - t))
        return jnp.where(count_ge(cand) >= CAP, cand, cur)

    thr = lax.fori_loop(0, 31, body, jnp.zeros((rows, 1), jnp.int32))
    gt = bits > thr
    eq = bits == thr
    need = CAP - jnp.sum(gt.astype(jnp.int32), axis=1, keepdims=True)
    tri = tri_ref[...]

    def cumsum_excl(mask_f):
        off = jnp.zeros((rows, 1), F32)
        outs = []
        for j in range(S // 128):
            mj = mask_f[:, j * 128:(j + 1) * 128]
            loc = jnp.dot(mj.astype(BF16), tri, preferred_element_type=F32)
            outs.append(loc - mj + off)
            off = off + loc[:, 127:128]
        return jnp.concatenate(outs, axis=1)

    eq_rank = cumsum_excl(eq.astype(F32))
    sel = gt | (eq & (eq_rank < need.astype(F32)))
    pos = cumsum_excl(sel.astype(F32))
    pos_ref[...] = jnp.where(sel, pos.astype(jnp.int32), -1)


SC_CORES = 2
SC_SUBCORES = 16
DISPATCH_ROWS = 64
COMBINE_RANGE = 1024
COMBINE_ROWS = 32
SLAB = 128
NSLAB = D // SLAB


def _sc_mesh():
    return plsc.VectorSubcoreMesh(core_axis_name="c", subcore_axis_name="s",
                                  num_cores=SC_CORES, num_subcores=SC_SUBCORES)


def _dispatch_body(pos_hbm, aff_hbm, h_hbm, xin_hbm, idx_hbm, gate_hbm, pos_v, aff_v, idx_v, gate_v,
                   rows_a, rows_b, gsem_a, gsem_b, wsem_a, wsem_b):
    w = lax.axis_index("s") * SC_CORES + lax.axis_index("c")
    base = (w // E) * S
    pltpu.sync_copy(pos_hbm.at[w], pos_v)
    pltpu.sync_copy(aff_hbm.at[w], aff_v)

    @pl.loop(0, S // 16)
    def _(i):
        p = pos_v[pl.ds(i * 16, 16)]
        m = p >= 0
        tok = lax.iota(jnp.int32, 16) + (i * 16 + base)
        plsc.store_scatter(idx_v, [p], tok, mask=m)
        plsc.store_scatter(gate_v, [p], aff_v[pl.ds(i * 16, 16)], mask=m)

    pltpu.sync_copy(idx_v, idx_hbm.at[w])
    pltpu.sync_copy(gate_v, gate_hbm.at[w])

    bufs, gsems, wsems = (rows_a, rows_b), (gsem_a, gsem_b), (wsem_a, wsem_b)
    nchunk = CAP // DISPATCH_ROWS

    def gather(j):
        return pltpu.async_copy(h_hbm.at[idx_v.at[pl.ds(j * DISPATCH_ROWS, DISPATCH_ROWS)]], bufs[j % 2], gsems[j % 2])

    pending_gather = gather(0)
    writes = [None, None]
    for j in range(nchunk):
        pending_gather.wait()
        writes[j % 2] = pltpu.async_copy(bufs[j % 2], xin_hbm.at[w, pl.ds(j * DISPATCH_ROWS, DISPATCH_ROWS)],
                                         wsems[j % 2])
        if j + 1 < nchunk:
            if writes[(j + 1) % 2] is not None:
                writes[(j + 1) % 2].wait()
            pending_gather = gather(j + 1)
    writes[(nchunk - 2) % 2].wait()
    writes[(nchunk - 1) % 2].wait()


def _dispatch(pos, aff, h):
    return pl.kernel(
        _dispatch_body, mesh=_sc_mesh(),
        out_type=[jax.ShapeDtypeStruct((B * E, CAP, D // 2), jnp.int32),
                  jax.ShapeDtypeStruct((B * E, CAP), jnp.int32),
                  jax.ShapeDtypeStruct((B * E, CAP), F32)],
        scratch_types=[pltpu.VMEM((S,), jnp.int32), pltpu.VMEM((S,), F32),
                       pltpu.VMEM((CAP,), jnp.int32), pltpu.VMEM((CAP,), F32),
                       pltpu.VMEM((DISPATCH_ROWS, D // 2), jnp.int32), pltpu.VMEM((DISPATCH_ROWS, D // 2), jnp.int32),
                       pltpu.SemaphoreType.DMA, pltpu.SemaphoreType.DMA,
                       pltpu.SemaphoreType.DMA, pltpu.SemaphoreType.DMA],
        compiler_params=pltpu.CompilerParams(needs_layout_passes=False),
        name="moe_dispatch",
    )(pos, aff, h)


def _combine_body(y_hbm, idx_hbm, x_hbm, out_hbm, idx_v, li_v, *bufs):
    rows = bufs[:NSLAB]
    accs = bufs[NSLAB:]
    b = lax.axis_index("c")
    s = lax.axis_index("s")
    w = b * E + s
    share = COMBINE_RANGE // SC_SUBCORES
    pltpu.sync_copy(idx_hbm.at[w], idx_v)
    lane = lax.iota(jnp.int32, 16)

    @pl.loop(0, S // COMBINE_RANGE)
    def _(r):
        t0 = b * S + r * COMBINE_RANGE
        row0 = t0 + s * share
        pltpu.sync_copy(tuple(x_hbm.at[pl.ds(row0, share), pl.ds(q * SLAB, SLAB)] for q in range(NSLAB)),
                        tuple(accs[q].at[pl.ds(s * share, share)] for q in range(NSLAB)))
        plsc.subcore_barrier()

        @pl.loop(0, CAP // COMBINE_ROWS)
        def _(j):
            hits = jnp.zeros((16,), jnp.int32)
            for v in range(COMBINE_ROWS // 16):
                t = idx_v[pl.ds(j * COMBINE_ROWS + v * 16, 16)] - t0
                ok = (t >= 0) & (t < COMBINE_RANGE)
                li_v[pl.ds(v * 16, 16)] = jnp.where(ok, t, COMBINE_RANGE + lane)
                hits = hits + plsc.all_reduce_population_count(ok)

            @pl.when(jnp.max(hits) > 0)
            def _():
                pltpu.sync_copy(
                    tuple(y_hbm.at[w, pl.ds(j * COMBINE_ROWS, COMBINE_ROWS), pl.ds(q * SLAB, SLAB)]
                          for q in range(NSLAB)),
                    tuple(rows))
                pltpu.sync_copy(tuple(rows), tuple(accs[q].at[li_v] for q in range(NSLAB)), add=True)

        plsc.subcore_barrier()
        pltpu.sync_copy(tuple(accs[q].at[pl.ds(s * share, share)] for q in range(NSLAB)),
                        tuple(out_hbm.at[pl.ds(row0, share), pl.ds(q * SLAB, SLAB)] for q in range(NSLAB)))


def _combine(y, idx, x):
    return pl.kernel(
        _combine_body, mesh=_sc_mesh(),
        out_type=jax.ShapeDtypeStruct((B * S, D), F32),
        scratch_types=[pltpu.VMEM((CAP,), jnp.int32), pltpu.VMEM((COMBINE_ROWS,), jnp.int32)]
        + [pltpu.VMEM((COMBINE_ROWS, SLAB), F32) for _ in range(NSLAB)]
        + [pltpu.VMEM_SHARED((COMBINE_RANGE + 16, SLAB), F32) for _ in range(NSLAB)],
        compiler_params=pltpu.CompilerParams(needs_layout_passes=False),
        name="moe_combine",
    )(y, idx, x)


def _pack_bf16_pairs(h):
    hb = h.astype(BF16).astype(F32)
    lo = pltpu.bitcast(hb[:, :D // 2], jnp.int32)
    hi = pltpu.bitcast(hb[:, D // 2:], jnp.int32)
    return hi | lax.shift_right_logical(lo, jnp.int32(16))


def _ffn_kernel(x_ref, wg_ref, wu_ref, wd_ref, gate_ref, mod_ref, o_ref, xb_ref):
    f = pl.program_id(1)
    last = pl.num_programs(1) - 1

    @pl.when(f == 0)
    def _():
        for b in range(B):
            xp = x_ref[b, 0]
            xb_ref[b * CAP:(b + 1) * CAP, :D // 2] = pltpu.bitcast(xp << 16, F32).astype(BF16)
            xb_ref[b * CAP:(b + 1) * CAP, D // 2:] = pltpu.bitcast(xp & jnp.int32(-65536), F32).astype(BF16)

    wg = wg_ref[0, 0].astype(BF16)
    wu = wu_ref[0, 0].astype(BF16)
    wd = wd_ref[0, 0].astype(BF16)
    is_first = f == 0
    is_last = f == last
    for b in range(B):
        g2 = jnp.where(is_last, mod_ref[0, b][5:6], 1.0)
        for r in range(CAP // FFN_RT):
            rows = slice(r * FFN_RT, (r + 1) * FFN_RT)
            xr = xb_ref[b * CAP + r * FFN_RT:b * CAP + (r + 1) * FFN_RT, :]
            g = jnp.dot(xr, wg, preferred_element_type=F32)
            u = jnp.dot(xr, wu, preferred_element_type=F32)
            a = (g * _sigmoid(g) * u).astype(BF16)
            y = jnp.dot(a, wd, preferred_element_type=F32)
            prev = jnp.where(is_first, 0.0, o_ref[b, 0, rows, :])
            gate = jnp.where(is_last, gate_ref[b, 0, rows, :], 1.0)
            o_ref[b, 0, rows, :] = (prev + y) * gate * g2


def _moe_layer(x, mod_all, gain, w_router, w_gate, w_up, w_down, layer, tri):
    wr = jnp.pad(w_router, ((0, 0), (0, EP - E)))
    wr1 = wr.astype(BF16)
    wr = jnp.concatenate([wr1, (wr - wr1.astype(F32)).astype(BF16)], axis=1)
    h, aff = pl.pallas_call(
        _router_kernel,
        grid=(B, S // RT_TM),
        in_specs=[
            pl.BlockSpec((1, RT_TM, D), lambda b, i: (b, i, 0)),
            _mod_spec(layer), _gain_spec(layer),
            pl.BlockSpec((D, 2 * EP), lambda b, i: (0, 0)),
        ],
        out_specs=[
            pl.BlockSpec((1, RT_TM, D // 2), lambda b, i: (b, i, 0)),
            pl.BlockSpec((1, E, RT_TM), lambda b, i: (b, 0, i)),
        ],
        out_shape=[
            jax.ShapeDtypeStruct((B, S, D // 2), jnp.int32),
            jax.ShapeDtypeStruct((B, E, S), F32),
        ],
        compiler_params=_cparams(("arbitrary", "arbitrary")),
        name="moe_router",
    )(x, mod_all, gain, wr)

    aff2 = aff.reshape(B * E, S)
    pos = pl.pallas_call(
        _select_kernel,
        grid=(1,),
        in_specs=[
            pl.BlockSpec((B * E, S), lambda i: (0, 0)),
            pl.BlockSpec((128, 128), lambda i: (0, 0)),
        ],
        out_specs=pl.BlockSpec((B * E, S), lambda i: (0, 0)),
        out_shape=jax.ShapeDtypeStruct((B * E, S), jnp.int32),
        compiler_params=_cparams(("arbitrary",)),
        name="moe_select",
    )(aff2, tri)

    xin, idx, gate = _dispatch(pos, aff2, h.reshape(B * S, D // 2))

    y = pl.pallas_call(
        _ffn_kernel,
        grid=(E, F // FFN_TF),
        in_specs=[
            pl.BlockSpec((B, 1, CAP, D // 2), lambda e, f: (0, e, 0, 0)),
            pl.BlockSpec((1, 1, D, FFN_TF), lambda e, f: (layer, e, 0, f)),
            pl.BlockSpec((1, 1, D, FFN_TF), lambda e, f: (layer, e, 0, f)),
            pl.BlockSpec((1, 1, FFN_TF, D), lambda e, f: (layer, e, f, 0)),
            pl.BlockSpec((B, 1, CAP, 1), lambda e, f: (0, e, 0, 0)),
            pl.BlockSpec((1, B, 6, D), lambda e, f: (layer, 0, 0, 0)),
        ],
        out_specs=pl.BlockSpec((B, 1, CAP, D), lambda e, f: (0, e, 0, 0)),
        out_shape=jax.ShapeDtypeStruct((B, E, CAP, D), F32),
        scratch_shapes=[pltpu.VMEM((B * CAP, D), BF16)],
        compiler_params=_cparams(("arbitrary", "arbitrary"), 56),
        name="moe_ffn",
    )(xin.reshape(B, E, CAP, D // 2), w_gate, w_up, w_down, gate.reshape(B, E, CAP, 1), mod_all)

    out = _combine(y.reshape(B * E, CAP, D), idx, x.reshape(B * S, D))
    return out.reshape(B, S, D)


def kernel(x, c, w_ada, b_ada, norm_mix, norm_ffn, w_fourier_out, w_qkv, w_attn_out, q_gain, k_gain,
           sink, rel_bias, w_router, w_gate, w_up, w_down):
    m0, m2 = _dft_tables()
    m1 = _stage1_table()
    bucket_t = _bucket_table()
    tri = jnp.asarray(np.triu(np.ones((128, 128), np.float32)), BF16)
    mod_all = _ada(c, w_ada, b_ada)
    bias_tab = _bias_table(rel_bias, bucket_t)
    gain_mix = norm_mix.reshape(DEPTH, 1, D)
    gain_ffn = norm_ffn.reshape(DEPTH, 1, D)
    for layer in range(DEPTH):
        j = layer // 2
        if layer % 2 == 0:
            x = _fourier_layer(x, mod_all, gain_mix, w_fourier_out[j].astype(BF16), layer, m0, m1, m2)
        else:
            x = _attn_layer(x, mod_all, gain_mix, w_qkv[j], w_attn_out[j].astype(BF16),
                            q_gain[j], k_gain[j], sink[j], bias_tab, layer)
        x = _moe_layer(x, mod_all, gain_ffn, w_router[layer], w_gate, w_up, w_down, layer, tri)
    return x
```

```python
import math

import numpy as np
import jax
import jax.numpy as jnp
from jax import lax
from jax.experimental import pallas as pl
from jax.experimental.pallas import tpu as pltpu
from jax.experimental.pallas import tpu_sc as plsc

D = 1024
B = 2
S = 8192
DEPTH = 4
GROUPS = 4
GD = D // GROUPS
HD = 64
NH = 16
NKV = 4
GQA = NH // NKV
WINDOW = 128
BLK = 128
NBUCKETS = 32
MAXDIST = 128
E = 16
CAP = 2 * S // E
F = 2 * D
EPS = 1e-6
NEG_INF = -1e30

N1 = 128
N2 = 64
INV_NORM = 1.0 / math.sqrt(S * GD)

F32 = jnp.float32
BF16 = jnp.bfloat16


def _cparams(sem, vmem_mb=48):
    return pltpu.CompilerParams(dimension_semantics=sem, vmem_limit_bytes=vmem_mb * 1024 * 1024)


def _dft_tables():
    c = np.arange(GD)
    ang0 = 2.0 * np.pi * ((c[:, None] * c[None, :]) % GD) / GD
    m0 = np.concatenate([np.cos(ang0), -np.sin(ang0)], axis=1)
    k2 = np.arange(N2)
    ang2 = 2.0 * np.pi * ((k2[:, None] * k2[None, :]) % N2) / N2
    m2 = np.concatenate([np.cos(ang2), np.sin(ang2)], axis=1)
    return jnp.asarray(m0, BF16), jnp.asarray(m2, BF16)


def _stage1_table():
    s2 = lax.broadcasted_iota(jnp.int32, (N2, N1, N1), 0)
    k1 = lax.broadcasted_iota(jnp.int32, (N2, N1, N1), 1)
    s1 = lax.broadcasted_iota(jnp.int32, (N2, N1, N1), 2)
    m = (k1 * (N2 * s1 + s2)) % S
    th = m.astype(F32) * (2.0 * math.pi / S)
    co, si = jnp.cos(th), jnp.sin(th)
    top = jnp.concatenate([co, si], axis=2)
    bot = jnp.concatenate([-si, co], axis=2)
    return jnp.concatenate([top, bot], axis=1).astype(BF16)


def _bucket_table():
    q_off = np.arange(BLK)
    k_off = np.arange(3 * BLK) - BLK
    rel = k_off[:, None] - q_off[None, :]
    half = NBUCKETS // 2
    max_exact = half // 2
    ret = np.where(rel > 0, half, 0)
    n = np.abs(rel)
    nf = np.maximum(n, 1).astype(np.float32)
    ratio = (np.log(nf / np.float32(max_exact)) / np.float32(math.log(MAXDIST / max_exact))).astype(np.float32)
    large = max_exact + (ratio * np.float32(half - max_exact)).astype(np.int32)
    large = np.minimum(large, half - 1)
    bucket = ret + np.where(n < max_exact, n, large)
    return jnp.asarray(np.where(np.abs(rel) <= WINDOW, bucket, -1).astype(np.int32))


def _modulate(x, gain, shift, scale):
    ms = jnp.mean(x * x, axis=-1, keepdims=True)
    return x * lax.rsqrt(ms + EPS) * (gain * (1.0 + scale)) + shift


def _sigmoid(x):
    return 1.0 / (1.0 + jnp.exp(-x))


ADA_TN = 768


def _ada_kernel(ct_ref, w_ref, b_ref, o_ref):
    ct = ct_ref[...]
    ca = ct * _sigmoid(ct)
    w = w_ref[0]
    for b in range(B):
        o_ref[0, b:b + 1, :] = jnp.sum(w * ca[:, b:b + 1], axis=0, keepdims=True) + b_ref[0]


def _ada(c, w_ada, b_ada):
    out = pl.pallas_call(
        _ada_kernel,
        grid=(DEPTH, 6 * D // ADA_TN),
        in_specs=[
            pl.BlockSpec((D, B), lambda l, j: (0, 0)),
            pl.BlockSpec((1, D, ADA_TN), lambda l, j: (l, 0, j)),
            pl.BlockSpec((1, 1, ADA_TN), lambda l, j: (l, 0, j)),
        ],
        out_specs=pl.BlockSpec((1, B, ADA_TN), lambda l, j: (l, 0, j)),
        out_shape=jax.ShapeDtypeStruct((DEPTH, B, 6 * D), F32),
        compiler_params=_cparams(("arbitrary", "arbitrary"), 32),
        name="ada_mod",
    )(c.T, w_ada, b_ada.reshape(DEPTH, 1, 6 * D))
    return out.reshape(DEPTH, B, 6, D)


def _mod_spec(layer, b):
    return pl.BlockSpec((1, 1, 6, D), lambda *_: (layer, b, 0, 0))


def _gain_spec(layer):
    return pl.BlockSpec((1, 1, D), lambda *_: (layer, 0, 0))


def _x_spec(x, b, tm):
    if x.ndim == 3:
        return pl.BlockSpec((None, tm, D), lambda i: (b, i, 0))
    return pl.BlockSpec((tm, D), lambda i: (i, 0))


F0_TM = 512
F1_J = 8
F2_TK = 8
F3_M = 4


def _f0_kernel(x_ref, mod_ref, gain_ref, m0_ref, o_ref):
    m = mod_ref[0, 0]
    h = _modulate(x_ref[...], gain_ref[0], m[0:1], m[1:2]).astype(BF16)
    m0 = m0_ref[...]
    for g in range(GROUPS):
        r = jnp.dot(h[:, g * GD:(g + 1) * GD], m0, preferred_element_type=F32)
        o_ref[0, :, g * GD:(g + 1) * GD] = r[:, :GD].astype(BF16)
        o_ref[1, :, g * GD:(g + 1) * GD] = r[:, GD:].astype(BF16)


def _f1_kernel(w_ref, m1_ref, o_ref):
    for j in range(F1_J):
        o_ref[:, j * D:(j + 1) * D] = jnp.dot(
            m1_ref[j], w_ref[:, j * D:(j + 1) * D], preferred_element_type=F32).astype(BF16)


def _f2_kernel(u_ref, m2_ref, o_ref):
    m2 = m2_ref[...]
    for k in range(F2_TK):
        u = jnp.concatenate([u_ref[0, k], u_ref[1, k]], axis=0)
        o_ref[k] = jnp.dot(m2, u, preferred_element_type=F32).astype(BF16)


def _f3_kernel(mp_ref, x_ref, w_ref, mod_ref, o_ref):
    g1 = mod_ref[0, 0][2:3]
    a = jnp.concatenate([mp_ref[:, j * D:(j + 1) * D] for j in range(F3_M)], axis=0)
    y = jnp.dot(a, w_ref[...], preferred_element_type=F32)
    o_ref[...] = x_ref[...] + (g1 * INV_NORM) * y


def _fourier_layer(x, b, mod_all, gain, w_out_bf, layer, m0, m1, m2):
    wc = pl.pallas_call(
        _f0_kernel,
        grid=(S // F0_TM,),
        in_specs=[
            _x_spec(x, b, F0_TM),
            _mod_spec(layer, b), _gain_spec(layer),
            pl.BlockSpec((GD, 2 * GD), lambda i: (0, 0)),
        ],
        out_specs=pl.BlockSpec((2, F0_TM, D), lambda i: (0, i, 0)),
        out_shape=jax.ShapeDtypeStruct((2, S, D), BF16),
        compiler_params=_cparams(("arbitrary",)),
        name="fourier_chan",
    )(x, mod_all, gain, m0)
    wc = wc.reshape(2 * N1, N2 * D)
    u = pl.pallas_call(
        _f1_kernel,
        grid=(N2 // F1_J,),
        in_specs=[
            pl.BlockSpec((2 * N1, F1_J * D), lambda i: (0, i)),
            pl.BlockSpec((F1_J, 2 * N1, 2 * N1), lambda i: (i, 0, 0)),
        ],
        out_specs=pl.BlockSpec((2 * N1, F1_J * D), lambda i: (0, i)),
        out_shape=jax.ShapeDtypeStruct((2 * N1, N2 * D), BF16),
        compiler_params=_cparams(("arbitrary",)),
        name="fourier_seq1",
    )(wc, m1)
    u = u.reshape(2, N1, N2, D)
    mp = pl.pallas_call(
        _f2_kernel,
        grid=(N1 // F2_TK,),
        in_specs=[
            pl.BlockSpec((2, F2_TK, N2, D), lambda i: (0, i, 0, 0)),
            pl.BlockSpec((N2, 2 * N2), lambda i: (0, 0)),
        ],
        out_specs=pl.BlockSpec((F2_TK, N2, D), lambda i: (i, 0, 0)),
        out_shape=jax.ShapeDtypeStruct((N1, N2, D), BF16),
        compiler_params=_cparams(("arbitrary",)),
        name="fourier_seq2",
    )(u, m2)
    mp = mp.reshape(N1, N2 * D)
    return pl.pallas_call(
        _f3_kernel,
        grid=(N2 // F3_M,),
        in_specs=[
            pl.BlockSpec((N1, F3_M * D), lambda i: (0, i)),
            _x_spec(x, b, F3_M * N1),
            pl.BlockSpec((D, D), lambda i: (0, 0)),
            _mod_spec(layer, b),
        ],
        out_specs=pl.BlockSpec((F3_M * N1, D), lambda i: (i, 0)),
        out_shape=jax.ShapeDtypeStruct((S, D), F32),
        compiler_params=_cparams(("arbitrary",)),
        name="fourier_out",
    )(mp, x, w_out_bf, mod_all)


QKV_TM = 512
TQ = 256
NSB = TQ // BLK
KWIN = TQ + 2 * BLK
NBLK = S // BLK
KPAD = 128
VROWS = HD + 16
ATT_AHEAD = 2


def _bias_kernel(rb_ref, bucket_ref, o_ref):
    h = pl.program_id(0)
    bucket = bucket_ref[...]
    acc = jnp.full(bucket.shape, NEG_INF, F32)
    for k in range(NBUCKETS):
        acc = jnp.where(bucket == k, rb_ref[k, h], acc)
    o_ref[0] = acc


def _bias_table(rel_bias, bucket_t):
    return pl.pallas_call(
        _bias_kernel,
        grid=(NH,),
        in_specs=[
            pl.BlockSpec(memory_space=pltpu.SMEM),
            pl.BlockSpec((3 * BLK, BLK), lambda h: (0, 0)),
        ],
        out_specs=pl.BlockSpec((1, 3 * BLK, BLK), lambda h: (h // GQA, 0, h % GQA)),
        out_shape=jax.ShapeDtypeStruct((NKV, 3 * BLK, GQA * BLK), F32),
        compiler_params=_cparams(("arbitrary",)),
        name="rel_bias_table",
    )(rel_bias, bucket_t)


def _qkv_kernel(x_ref, mod_ref, gain_ref, wqt_ref, wk_ref, wvt_ref, qg_ref, kg_ref, qt_ref, k_ref, vt_ref):
    m = mod_ref[0, 0]
    h = _modulate(x_ref[...], gain_ref[0], m[0:1], m[1:2]).astype(BF16)
    nt = (((1,), (1,)), ((), ()))
    qt = lax.dot_general(wqt_ref[...], h, nt, preferred_element_type=F32)
    tm = qt.shape[1]
    q3 = qt.reshape(NH, HD, tm)
    q3 = q3 * lax.rsqrt(jnp.mean(q3 * q3, axis=1, keepdims=True) + EPS)
    qt_ref[...] = (q3.reshape(NH * HD, tm) * qg_ref[...]).astype(BF16)
    k = jnp.dot(h, wk_ref[...], preferred_element_type=F32)
    for g in range(NKV):
        kg = k[:, g * KPAD:(g + 1) * KPAD]
        ms = jnp.sum(kg * kg, axis=-1, keepdims=True) * (1.0 / HD)
        k_ref[:, g * KPAD:(g + 1) * KPAD] = (kg * lax.rsqrt(ms + EPS) * kg_ref[...]).astype(BF16)
    vt_ref[...] = lax.dot_general(wvt_ref[...], h, nt, preferred_element_type=F32).astype(BF16)


def _attn_kernel(qt_ref, kp_ref, kc_ref, kn_ref, vp_ref, vc_ref, vn_ref, x_ref, w_ref, mod_ref,
                 bias_ref, sink_ref, o_ref, att_ref):
    i = pl.program_id(0)
    kwin = jnp.concatenate([kp_ref[...], kc_ref[...], kn_ref[...]], axis=0)
    vwin = jnp.concatenate([vp_ref[...], vc_ref[...], vn_ref[...]], axis=1)
    ones_rows = (lax.broadcasted_iota(jnp.int32, (VROWS - HD, KWIN), 0) == 0).astype(BF16)
    vaug = [jnp.concatenate([vwin[g * HD:(g + 1) * HD], ones_rows], axis=0) for g in range(NKV)]
    row = lax.broadcasted_iota(jnp.int32, (3 * BLK, 1), 0)
    items = [(g, r) for g in range(NKV) for r in range(NSB)]

    def scores(g, r):
        kr = kwin[r * BLK:r * BLK + 3 * BLK, g * KPAD:g * KPAD + HD]
        qg = jnp.concatenate(
            [qt_ref[(GQA * g + hh) * HD:(GQA * g + hh + 1) * HD, r * BLK:(r + 1) * BLK] for hh in range(GQA)],
            axis=1)
        return jnp.dot(kr, qg, preferred_element_type=F32)

    def probs(s, g, r):
        sink = sink_ref[g]
        s = s + bias_ref[g]
        key_pos = i * TQ + (r - 1) * BLK + row
        s = jnp.where((key_pos >= 0) & (key_pos < S), s, NEG_INF)
        mx = jnp.maximum(jnp.max(s, axis=0, keepdims=True), sink)
        return jnp.exp(s - mx).astype(BF16), jnp.exp(sink - mx)

    pending = [scores(*items[n]) for n in range(ATT_AHEAD)]
    for n, (g, r) in enumerate(items):
        s = pending.pop(0)
        if n + ATT_AHEAD < len(items):
            pending.append(scores(*items[n + ATT_AHEAD]))
        p, psink = probs(s, g, r)
        ot = jnp.dot(vaug[g][:, r * BLK:r * BLK + 3 * BLK], p, preferred_element_type=F32)
        ot = ot[:HD] * (1.0 / (ot[HD:HD + 1] + psink))
        for hh in range(GQA):
            hd0 = (GQA * g + hh) * HD
            att_ref[hd0:hd0 + HD, r * BLK:(r + 1) * BLK] = ot[:, hh * BLK:(hh + 1) * BLK]
    g1 = mod_ref[0, 0][2:3]
    att = att_ref[...].T.astype(BF16)
    y = jnp.dot(att, w_ref[...], preferred_element_type=F32)
    o_ref[...] = x_ref[...] + g1 * y


def _attn_weights(w_qkv, q_gain, k_gain, sink):
    wq, wk, wv = w_qkv[:, :NH * HD], w_qkv[:, NH * HD:(NH + NKV) * HD], w_qkv[:, (NH + NKV) * HD:]
    wqt = wq.T.astype(BF16)
    wvt = wv.T.astype(BF16)
    wk_pad = jnp.pad(wk.reshape(D, NKV, HD), ((0, 0), (0, 0), (0, KPAD - HD))).reshape(D, NKV * KPAD).astype(BF16)
    qg_col = jnp.tile(q_gain * (HD ** -0.5), NH).reshape(NH * HD, 1)
    kg_row = jnp.pad(k_gain, (0, KPAD - HD)).reshape(1, KPAD)
    sink_row = jnp.repeat(sink, BLK).reshape(NKV, 1, GQA * BLK)
    return wqt, wk_pad, wvt, qg_col, kg_row, sink_row


def _attn_layer(x, b, mod_all, gain, aw, w_out_bf, bias_tab, layer):
    wqt, wk_pad, wvt, qg_col, kg_row, sink_row = aw
    qt, k, vt = pl.pallas_call(
        _qkv_kernel,
        grid=(S // QKV_TM,),
        in_specs=[
            _x_spec(x, b, QKV_TM),
            _mod_spec(layer, b), _gain_spec(layer),
            pl.BlockSpec((NH * HD, D), lambda i: (0, 0)),
            pl.BlockSpec((D, NKV * KPAD), lambda i: (0, 0)),
            pl.BlockSpec((NKV * HD, D), lambda i: (0, 0)),
            pl.BlockSpec((NH * HD, 1), lambda i: (0, 0)),
            pl.BlockSpec((1, KPAD), lambda i: (0, 0)),
        ],
        out_specs=[
            pl.BlockSpec((NH * HD, QKV_TM), lambda i: (0, i)),
            pl.BlockSpec((QKV_TM, NKV * KPAD), lambda i: (i, 0)),
            pl.BlockSpec((NKV * HD, QKV_TM), lambda i: (0, i)),
        ],
        out_shape=[
            jax.ShapeDtypeStruct((NH * HD, S), BF16),
            jax.ShapeDtypeStruct((S, NKV * KPAD), BF16),
            jax.ShapeDtypeStruct((NKV * HD, S), BF16),
        ],
        compiler_params=_cparams(("arbitrary",)),
        name="attn_qkv",
    )(x, mod_all, gain, wqt, wk_pad, wvt, qg_col, kg_row)

    kw = NKV * KPAD
    vw = NKV * HD
    kprev = pl.BlockSpec((BLK, kw), lambda i: (jnp.maximum(i * NSB - 1, 0), 0))
    kcur = pl.BlockSpec((TQ, kw), lambda i: (i, 0))
    knext = pl.BlockSpec((BLK, kw), lambda i: (jnp.minimum((i + 1) * NSB, NBLK - 1), 0))
    vprev = pl.BlockSpec((vw, BLK), lambda i: (0, jnp.maximum(i * NSB - 1, 0)))
    vcur = pl.BlockSpec((vw, TQ), lambda i: (0, i))
    vnext = pl.BlockSpec((vw, BLK), lambda i: (0, jnp.minimum((i + 1) * NSB, NBLK - 1)))
    return pl.pallas_call(
        _attn_kernel,
        grid=(S // TQ,),
        in_specs=[
            pl.BlockSpec((NH * HD, TQ), lambda i: (0, i)),
            kprev, kcur, knext, vprev, vcur, vnext,
            _x_spec(x, b, TQ),
            pl.BlockSpec((NH * HD, D), lambda i: (0, 0)),
            _mod_spec(layer, b),
            pl.BlockSpec((NKV, 3 * BLK, GQA * BLK), lambda i: (0, 0, 0)),
            pl.BlockSpec((NKV, 1, GQA * BLK), lambda i: (0, 0, 0)),
        ],
        out_specs=pl.BlockSpec((TQ, D), lambda i: (i, 0)),
        out_shape=jax.ShapeDtypeStruct((S, D), F32),
        scratch_shapes=[pltpu.VMEM((NH * HD, TQ), F32)],
        compiler_params=_cparams(("arbitrary",)),
        name="attn_core",
    )(qt, k, k, k, vt, vt, vt, x, w_out_bf, mod_all, bias_tab, sink_row)


RT_TM = 512
EP = 128
FFN_TF = 512
FFN_RT = 512


def _pack_bf16_pairs(h):
    hb = h.astype(BF16).astype(F32)
    lo = pltpu.bitcast(hb[:, :D // 2], jnp.int32)
    hi = pltpu.bitcast(hb[:, D // 2:], jnp.int32)
    return hi | lax.shift_right_logical(lo, jnp.int32(16))


def _router_kernel(x_ref, mod_ref, gain_ref, wr_ref, h_ref, aff_ref):
    m = mod_ref[0, 0]
    h = _modulate(x_ref[...], gain_ref[0], m[3:4], m[4:5])
    h_ref[...] = _pack_bf16_pairs(h)
    h1 = h.astype(BF16)
    h2 = (h - h1.astype(F32)).astype(BF16)
    wr = wr_ref[...]
    part = jnp.dot(h1, wr, preferred_element_type=F32)
    logits = part[:, :EP] + part[:, EP:] + jnp.dot(h2, wr[:, :EP], preferred_element_type=F32)
    lt = logits.T[:E]
    mx = jnp.max(lt, axis=0, keepdims=True)
    ex = jnp.exp(lt - mx)
    aff_ref[...] = ex / jnp.sum(ex, axis=0, keepdims=True)


def _select_kernel(aff_ref, tri_ref, pos_ref):
    aff = aff_ref[...]
    bits = pltpu.bitcast(aff, jnp.int32)
    rows = aff.shape[0]

    def count_ge(v):
        return jnp.sum((bits >= v).astype(jnp.int32), axis=1, keepdims=True)

    def body(t, cur):
        cand = cur | (jnp.int32(1) << (30 - t))
        return jnp.where(count_ge(cand) >= CAP, cand, cur)

    thr = lax.fori_loop(0, 31, body, jnp.zeros((rows, 1), jnp.int32))
    gt = bits > thr
    eq = bits == thr
    need = CAP - jnp.sum(gt.astype(jnp.int32), axis=1, keepdims=True)
    tri = tri_ref[...]

    def cumsum_excl(mask_f):
        off = jnp.zeros((rows, 1), F32)
        outs = []
        for j in range(S // 128):
            mj = mask_f[:, j * 128:(j + 1) * 128]
            loc = jnp.dot(mj.astype(BF16), tri, preferred_element_type=F32)
            outs.append(loc - mj + off)
            off = off + loc[:, 127:128]
        return jnp.concatenate(outs, axis=1)

    eq_rank = cumsum_excl(eq.astype(F32))
    sel = gt | (eq & (eq_rank < need.astype(F32)))
    pos = cumsum_excl(sel.astype(F32))
    pos_ref[...] = jnp.where(sel, pos.astype(jnp.int32), -1)


SC_CORES = 2
SC_SUBCORES = 16
SC_TILES = SC_CORES * SC_SUBCORES
DISPATCH_SLOTS = CAP * E // SC_TILES
DISPATCH_ROWS = 64
COMBINE_RANGE = 1024
COMBINE_ROWS = 32
SLAB = 128
NSLAB = D // SLAB


def _sc_mesh():
    return plsc.VectorSubcoreMesh(core_axis_name="c", subcore_axis_name="s",
                                  num_cores=SC_CORES, num_subcores=SC_SUBCORES)


def _dispatch_body(pos_hbm, aff_hbm, h_hbm, xin_hbm, idx_hbm, gate_hbm, pos_v, aff_v, idx_v, gate_v,
                   rows_a, rows_b, gsem_a, gsem_b, wsem_a, wsem_b):
    w = lax.axis_index("s") * SC_CORES + lax.axis_index("c")
    e = w // 2
    lo = (w % 2) * DISPATCH_SLOTS
    pltpu.sync_copy(pos_hbm.at[e], pos_v)
    pltpu.sync_copy(aff_hbm.at[e], aff_v)

    @pl.loop(0, S // 16)
    def _(i):
        p = pos_v[pl.ds(i * 16, 16)] - lo
        m = (p >= 0) & (p < DISPATCH_SLOTS)
        tok = lax.iota(jnp.int32, 16) + i * 16
        plsc.store_scatter(idx_v, [p], tok, mask=m)
        plsc.store_scatter(gate_v, [p], aff_v[pl.ds(i * 16, 16)], mask=m)

    pltpu.sync_copy(idx_v, idx_hbm.at[e, pl.ds(lo, DISPATCH_SLOTS)])
    pltpu.sync_copy(gate_v, gate_hbm.at[e, pl.ds(lo, DISPATCH_SLOTS)])

    bufs, gsems, wsems = (rows_a, rows_b), (gsem_a, gsem_b), (wsem_a, wsem_b)
    nchunk = DISPATCH_SLOTS // DISPATCH_ROWS

    def gather(j):
        return pltpu.async_copy(h_hbm.at[idx_v.at[pl.ds(j * DISPATCH_ROWS, DISPATCH_ROWS)]], bufs[j % 2], gsems[j % 2])

    pending_gather = gather(0)
    writes = [None, None]
    for j in range(nchunk):
        pending_gather.wait()
        writes[j % 2] = pltpu.async_copy(
            bufs[j % 2], xin_hbm.at[e, pl.ds(lo + j * DISPATCH_ROWS, DISPATCH_ROWS)], wsems[j % 2])
        if j + 1 < nchunk:
            if writes[(j + 1) % 2] is not None:
                writes[(j + 1) % 2].wait()
            pending_gather = gather(j + 1)
    writes[(nchunk - 2) % 2].wait()
    writes[(nchunk - 1) % 2].wait()


def _dispatch(pos, aff, h):
    return pl.kernel(
        _dispatch_body, mesh=_sc_mesh(),
        out_type=[jax.ShapeDtypeStruct((E, CAP, D // 2), jnp.int32),
                  jax.ShapeDtypeStruct((E, CAP), jnp.int32),
                  jax.ShapeDtypeStruct((E, CAP), F32)],
        scratch_types=[pltpu.VMEM((S,), jnp.int32), pltpu.VMEM((S,), F32),
                       pltpu.VMEM((DISPATCH_SLOTS,), jnp.int32), pltpu.VMEM((DISPATCH_SLOTS,), F32),
                       pltpu.VMEM((DISPATCH_ROWS, D // 2), jnp.int32), pltpu.VMEM((DISPATCH_ROWS, D // 2), jnp.int32),
                       pltpu.SemaphoreType.DMA, pltpu.SemaphoreType.DMA,
                       pltpu.SemaphoreType.DMA, pltpu.SemaphoreType.DMA],
        compiler_params=pltpu.CompilerParams(needs_layout_passes=False),
        name="moe_dispatch",
    )(pos, aff, h)


def _combine_body(y_hbm, idx_hbm, x_hbm, out_hbm, idx_v, li_v, *bufs):
    rows = bufs[:NSLAB]
    accs = bufs[NSLAB:]
    c = lax.axis_index("c")
    s = lax.axis_index("s")
    share = COMBINE_RANGE // SC_SUBCORES
    pltpu.sync_copy(idx_hbm.at[s], idx_v)
    lane = lax.iota(jnp.int32, 16)

    @pl.loop(0, S // COMBINE_RANGE // SC_CORES)
    def _(r):
        t0 = (r * SC_CORES + c) * COMBINE_RANGE
        row0 = t0 + s * share
        pltpu.sync_copy(tuple(x_hbm.at[pl.ds(row0, share), pl.ds(q * SLAB, SLAB)] for q in range(NSLAB)),
                        tuple(accs[q].at[pl.ds(s * share, share)] for q in range(NSLAB)))
        plsc.subcore_barrier()

        @pl.loop(0, CAP // COMBINE_ROWS)
        def _(j):
            hits = jnp.zeros((16,), jnp.int32)
            for v in range(COMBINE_ROWS // 16):
                t = idx_v[pl.ds(j * COMBINE_ROWS + v * 16, 16)] - t0
                ok = (t >= 0) & (t < COMBINE_RANGE)
                li_v[pl.ds(v * 16, 16)] = jnp.where(ok, t, COMBINE_RANGE + lane)
                hits = hits + plsc.all_reduce_population_count(ok)

            @pl.when(jnp.max(hits) > 0)
            def _():
                pltpu.sync_copy(
                    tuple(y_hbm.at[s, pl.ds(j * COMBINE_ROWS, COMBINE_ROWS), pl.ds(q * SLAB, SLAB)]
                          for q in range(NSLAB)),
                    tuple(rows))
                pltpu.sync_copy(tuple(rows), tuple(accs[q].at[li_v] for q in range(NSLAB)), add=True)

        plsc.subcore_barrier()
        pltpu.sync_copy(tuple(accs[q].at[pl.ds(s * share, share)] for q in range(NSLAB)),
                        tuple(out_hbm.at[pl.ds(row0, share), pl.ds(q * SLAB, SLAB)] for q in range(NSLAB)))


def _combine(y, idx, x):
    return pl.kernel(
        _combine_body, mesh=_sc_mesh(),
        out_type=jax.ShapeDtypeStruct((S, D), F32),
        scratch_types=[pltpu.VMEM((CAP,), jnp.int32), pltpu.VMEM((COMBINE_ROWS,), jnp.int32)]
        + [pltpu.VMEM((COMBINE_ROWS, SLAB), F32) for _ in range(NSLAB)]
        + [pltpu.VMEM_SHARED((COMBINE_RANGE + 16, SLAB), F32) for _ in range(NSLAB)],
        compiler_params=pltpu.CompilerParams(needs_layout_passes=False),
        name="moe_combine",
    )(y, idx, x)


def _ffn_kernel(x_ref, wg_ref, wu_ref, wd_ref, gate_ref, mod_ref, o_ref, xb_ref):
    f = pl.program_id(1)
    last = pl.num_programs(1) - 1

    @pl.when(f == 0)
    def _():
        xp = x_ref[0]
        xb_ref[:, :D // 2] = pltpu.bitcast(xp << 16, F32).astype(BF16)
        xb_ref[:, D // 2:] = pltpu.bitcast(xp & jnp.int32(-65536), F32).astype(BF16)

    wg = wg_ref[0, 0].astype(BF16)
    wu = wu_ref[0, 0].astype(BF16)
    wd = wd_ref[0, 0].astype(BF16)
    is_first = f == 0
    is_last = f == last
    g2 = jnp.where(is_last, mod_ref[0, 0][5:6], 1.0)
    for r in range(CAP // FFN_RT):
        rows = slice(r * FFN_RT, (r + 1) * FFN_RT)
        xr = xb_ref[rows, :]
        g = jnp.dot(xr, wg, preferred_element_type=F32)
        u = jnp.dot(xr, wu, preferred_element_type=F32)
        a = (g * _sigmoid(g) * u).astype(BF16)
        y = jnp.dot(a, wd, preferred_element_type=F32)
        prev = jnp.where(is_first, 0.0, o_ref[0, rows, :])
        gate = jnp.where(is_last, gate_ref[0, rows, :], 1.0)
        o_ref[0, rows, :] = (prev + y) * gate * g2


def _router(x, b, mod_all, gain, wr, layer):
    return pl.pallas_call(
        _router_kernel,
        grid=(S // RT_TM,),
        in_specs=[
            _x_spec(x, b, RT_TM),
            _mod_spec(layer, b), _gain_spec(layer),
            pl.BlockSpec((D, 2 * EP), lambda i: (0, 0)),
        ],
        out_specs=[
            pl.BlockSpec((RT_TM, D // 2), lambda i: (i, 0)),
            pl.BlockSpec((E, RT_TM), lambda i: (0, i)),
        ],
        out_shape=[
            jax.ShapeDtypeStruct((S, D // 2), jnp.int32),
            jax.ShapeDtypeStruct((E, S), F32),
        ],
        compiler_params=_cparams(("arbitrary",)),
        name="moe_router",
    )(x, mod_all, gain, wr)


def _select(aff, tri):
    return pl.pallas_call(
        _select_kernel,
        grid=(1,),
        in_specs=[
            pl.BlockSpec((E, S), lambda i: (0, 0)),
            pl.BlockSpec((128, 128), lambda i: (0, 0)),
        ],
        out_specs=pl.BlockSpec((E, S), lambda i: (0, 0)),
        out_shape=jax.ShapeDtypeStruct((E, S), jnp.int32),
        compiler_params=_cparams(("arbitrary",)),
        name="moe_select",
    )(aff, tri)


def _ffn(xin, gate, b, mod_all, w_gate, w_up, w_down, layer):
    return pl.pallas_call(
        _ffn_kernel,
        grid=(E, F // FFN_TF),
        in_specs=[
            pl.BlockSpec((1, CAP, D // 2), lambda e, f: (e, 0, 0)),
            pl.BlockSpec((1, 1, D, FFN_TF), lambda e, f: (layer, e, 0, f)),
            pl.BlockSpec((1, 1, D, FFN_TF), lambda e, f: (layer, e, 0, f)),
            pl.BlockSpec((1, 1, FFN_TF, D), lambda e, f: (layer, e, f, 0)),
            pl.BlockSpec((1, CAP, 1), lambda e, f: (e, 0, 0)),
            pl.BlockSpec((1, 1, 6, D), lambda e, f: (layer, b, 0, 0)),
        ],
        out_specs=pl.BlockSpec((1, CAP, D), lambda e, f: (e, 0, 0)),
        out_shape=jax.ShapeDtypeStruct((E, CAP, D), F32),
        scratch_shapes=[pltpu.VMEM((CAP, D), BF16)],
        compiler_params=_cparams(("arbitrary", "arbitrary"), 48),
        name="moe_ffn",
    )(xin, w_gate, w_up, w_down, gate.reshape(E, CAP, 1), mod_all)


def kernel(x, c, w_ada, b_ada, norm_mix, norm_ffn, w_fourier_out, w_qkv, w_attn_out, q_gain, k_gain,
           sink, rel_bias, w_router, w_gate, w_up, w_down):
    m0, m2 = _dft_tables()
    m1 = _stage1_table()
    bucket_t = _bucket_table()
    tri = jnp.asarray(np.triu(np.ones((128, 128), np.float32)), BF16)
    mod_all = _ada(c, w_ada, b_ada)
    bias_tab = _bias_table(rel_bias, bucket_t)
    gain_mix = norm_mix.reshape(DEPTH, 1, D)
    gain_ffn = norm_ffn.reshape(DEPTH, 1, D)
    xs = [x, x]
    for layer in range(DEPTH):
        j = layer // 2
        if layer % 2 == 0:
            w_out_bf = w_fourier_out[j].astype(BF16)
            xs = [_fourier_layer(xs[b], b, mod_all, gain_mix, w_out_bf, layer, m0, m1, m2) for b in range(B)]
        else:
            aw = _attn_weights(w_qkv[j], q_gain[j], k_gain[j], sink[j])
            w_out_bf = w_attn_out[j].astype(BF16)
            xs = [_attn_layer(xs[b], b, mod_all, gain_mix, aw, w_out_bf, bias_tab, layer) for b in range(B)]
        wr = jnp.pad(w_router[layer], ((0, 0), (0, EP - E)))
        wr1 = wr.astype(BF16)
        wr = jnp.concatenate([wr1, (wr - wr1.astype(F32)).astype(BF16)], axis=1)
        routed = [_router(xs[b], b, mod_all, gain_ffn, wr, layer) for b in range(B)]
        pos = [_select(routed[b][1], tri) for b in range(B)]
        disp = [_dispatch(pos[b], routed[b][1], routed[b][0]) for b in range(B)]
        ys = [_ffn(disp[b][0], disp[b][2], b, mod_all, w_gate, w_up, w_down, layer) for b in range(B)]
        xs = [_combine(ys[b], disp[b][1], xs[b]) for b in range(B)]
    return jnp.stack(xs)
```

```python
import math

import numpy as np
import jax
import jax.numpy as jnp
from jax import lax
from jax.experimental import pallas as pl
from jax.experimental.pallas import tpu as pltpu
from jax.experimental.pallas import tpu_sc as plsc

D = 1024
B = 2
S = 8192
DEPTH = 4
GROUPS = 4
GD = D // GROUPS
HD = 64
NH = 16
NKV = 4
GQA = NH // NKV
WINDOW = 128
BLK = 128
NBUCKETS = 32
MAXDIST = 128
E = 16
CAP = 2 * S // E
F = 2 * D
EPS = 1e-6
NEG_INF = -1e30

N1 = 128
N2 = 64
INV_NORM = 1.0 / math.sqrt(S * GD)

F32 = jnp.float32
BF16 = jnp.bfloat16


def _cparams(sem, vmem_mb=48):
    return pltpu.CompilerParams(dimension_semantics=sem, vmem_limit_bytes=vmem_mb * 1024 * 1024)


def _dft_tables():
    c = np.arange(GD)
    ang0 = 2.0 * np.pi * ((c[:, None] * c[None, :]) % GD) / GD
    m0 = np.concatenate([np.cos(ang0), -np.sin(ang0)], axis=1)
    k2 = np.arange(N2)
    ang2 = 2.0 * np.pi * ((k2[:, None] * k2[None, :]) % N2) / N2
    m2 = np.concatenate([np.cos(ang2), np.sin(ang2)], axis=1)
    return jnp.asarray(m0, BF16), jnp.asarray(m2, BF16)


def _stage1_table():
    s2 = lax.broadcasted_iota(jnp.int32, (N2, N1, N1), 0)
    k1 = lax.broadcasted_iota(jnp.int32, (N2, N1, N1), 1)
    s1 = lax.broadcasted_iota(jnp.int32, (N2, N1, N1), 2)
    m = (k1 * (N2 * s1 + s2)) % S
    th = m.astype(F32) * (2.0 * math.pi / S)
    co, si = jnp.cos(th), jnp.sin(th)
    top = jnp.concatenate([co, si], axis=2)
    bot = jnp.concatenate([-si, co], axis=2)
    return jnp.concatenate([top, bot], axis=1).astype(BF16)


def _bucket_table():
    q_off = np.arange(BLK)
    k_off = np.arange(3 * BLK) - BLK
    rel = k_off[:, None] - q_off[None, :]
    half = NBUCKETS // 2
    max_exact = half // 2
    ret = np.where(rel > 0, half, 0)
    n = np.abs(rel)
    nf = np.maximum(n, 1).astype(np.float32)
    ratio = (np.log(nf / np.float32(max_exact)) / np.float32(math.log(MAXDIST / max_exact))).astype(np.float32)
    large = max_exact + (ratio * np.float32(half - max_exact)).astype(np.int32)
    large = np.minimum(large, half - 1)
    bucket = ret + np.where(n < max_exact, n, large)
    return jnp.asarray(np.where(np.abs(rel) <= WINDOW, bucket, -1).astype(np.int32))


def _modulate(x, gain, shift, scale):
    ms = jnp.mean(x * x, axis=-1, keepdims=True)
    return x * lax.rsqrt(ms + EPS) * (gain * (1.0 + scale)) + shift


def _sigmoid(x):
    return 1.0 / (1.0 + jnp.exp(-x))


ADA_TN = 768


def _ada_kernel(ct_ref, w_ref, b_ref, o_ref):
    ct = ct_ref[...]
    ca = ct * _sigmoid(ct)
    w = w_ref[0]
    for b in range(B):
        o_ref[0, b:b + 1, :] = jnp.sum(w * ca[:, b:b + 1], axis=0, keepdims=True) + b_ref[0]


def _ada(c, w_ada, b_ada):
    out = pl.pallas_call(
        _ada_kernel,
        grid=(DEPTH, 6 * D // ADA_TN),
        in_specs=[
            pl.BlockSpec((D, B), lambda l, j: (0, 0)),
            pl.BlockSpec((1, D, ADA_TN), lambda l, j: (l, 0, j)),
            pl.BlockSpec((1, 1, ADA_TN), lambda l, j: (l, 0, j)),
        ],
        out_specs=pl.BlockSpec((1, B, ADA_TN), lambda l, j: (l, 0, j)),
        out_shape=jax.ShapeDtypeStruct((DEPTH, B, 6 * D), F32),
        compiler_params=_cparams(("arbitrary", "arbitrary"), 32),
        name="ada_mod",
    )(c.T, w_ada, b_ada.reshape(DEPTH, 1, 6 * D))
    return out.reshape(DEPTH, B, 6, D)


def _mod_spec(layer, b):
    return pl.BlockSpec((1, 1, 6, D), lambda *_: (layer, b, 0, 0))


def _gain_spec(layer):
    return pl.BlockSpec((1, 1, D), lambda *_: (layer, 0, 0))


def _x_spec(x, b, tm):
    if x.ndim == 3:
        return pl.BlockSpec((None, tm, D), lambda i: (b, i, 0))
    return pl.BlockSpec((tm, D), lambda i: (i, 0))


F0_TM = 512
F1_J = 8
F2_TK = 8
F3_M = 4


def _f0_kernel(x_ref, mod_ref, gain_ref, m0_ref, o_ref):
    m = mod_ref[0, 0]
    h = _modulate(x_ref[...], gain_ref[0], m[0:1], m[1:2]).astype(BF16)
    m0 = m0_ref[...]
    for g in range(GROUPS):
        r = jnp.dot(h[:, g * GD:(g + 1) * GD], m0, preferred_element_type=F32)
        o_ref[0, :, g * GD:(g + 1) * GD] = r[:, :GD].astype(BF16)
        o_ref[1, :, g * GD:(g + 1) * GD] = r[:, GD:].astype(BF16)


def _f1_kernel(w_ref, m1_ref, o_ref):
    for j in range(F1_J):
        o_ref[:, j * D:(j + 1) * D] = jnp.dot(
            m1_ref[j], w_ref[:, j * D:(j + 1) * D], preferred_element_type=F32).astype(BF16)


def _f2_kernel(u_ref, m2_ref, o_ref):
    m2 = m2_ref[...]
    for k in range(F2_TK):
        u = jnp.concatenate([u_ref[0, k], u_ref[1, k]], axis=0)
        o_ref[k] = jnp.dot(m2, u, preferred_element_type=F32).astype(BF16)


def _f3_kernel(mp_ref, x_ref, w_ref, mod_ref, o_ref):
    g1 = mod_ref[0, 0][2:3]
    a = jnp.concatenate([mp_ref[:, j * D:(j + 1) * D] for j in range(F3_M)], axis=0)
    y = jnp.dot(a, w_ref[...], preferred_element_type=F32)
    o_ref[...] = x_ref[...] + (g1 * INV_NORM) * y


def _fourier_layer(x, b, mod_all, gain, w_out_bf, layer, m0, m1, m2):
    wc = pl.pallas_call(
        _f0_kernel,
        grid=(S // F0_TM,),
        in_specs=[
            _x_spec(x, b, F0_TM),
            _mod_spec(layer, b), _gain_spec(layer),
            pl.BlockSpec((GD, 2 * GD), lambda i: (0, 0)),
        ],
        out_specs=pl.BlockSpec((2, F0_TM, D), lambda i: (0, i, 0)),
        out_shape=jax.ShapeDtypeStruct((2, S, D), BF16),
        compiler_params=_cparams(("arbitrary",)),
        name="fourier_chan",
    )(x, mod_all, gain, m0)
    wc = wc.reshape(2 * N1, N2 * D)
    u = pl.pallas_call(
        _f1_kernel,
        grid=(N2 // F1_J,),
        in_specs=[
            pl.BlockSpec((2 * N1, F1_J * D), lambda i: (0, i)),
            pl.BlockSpec((F1_J, 2 * N1, 2 * N1), lambda i: (i, 0, 0)),
        ],
        out_specs=pl.BlockSpec((2 * N1, F1_J * D), lambda i: (0, i)),
        out_shape=jax.ShapeDtypeStruct((2 * N1, N2 * D), BF16),
        compiler_params=_cparams(("arbitrary",)),
        name="fourier_seq1",
    )(wc, m1)
    u = u.reshape(2, N1, N2, D)
    mp = pl.pallas_call(
        _f2_kernel,
        grid=(N1 // F2_TK,),
        in_specs=[
            pl.BlockSpec((2, F2_TK, N2, D), lambda i: (0, i, 0, 0)),
            pl.BlockSpec((N2, 2 * N2), lambda i: (0, 0)),
        ],
        out_specs=pl.BlockSpec((F2_TK, N2, D), lambda i: (i, 0, 0)),
        out_shape=jax.ShapeDtypeStruct((N1, N2, D), BF16),
        compiler_params=_cparams(("arbitrary",)),
        name="fourier_seq2",
    )(u, m2)
    mp = mp.reshape(N1, N2 * D)
    return pl.pallas_call(
        _f3_kernel,
        grid=(N2 // F3_M,),
        in_specs=[
            pl.BlockSpec((N1, F3_M * D), lambda i: (0, i)),
            _x_spec(x, b, F3_M * N1),
            pl.BlockSpec((D, D), lambda i: (0, 0)),
            _mod_spec(layer, b),
        ],
        out_specs=pl.BlockSpec((F3_M * N1, D), lambda i: (i, 0)),
        out_shape=jax.ShapeDtypeStruct((S, D), F32),
        compiler_params=_cparams(("arbitrary",)),
        name="fourier_out",
    )(mp, x, w_out_bf, mod_all)


QKV_TM = 512
TQ = 256
NSB = TQ // BLK
KWIN = TQ + 2 * BLK
NBLK = S // BLK
KPAD = 128
VROWS = HD + 16
ATT_AHEAD = 2


def _bias_kernel(rb_ref, bucket_ref, o_ref):
    h = pl.program_id(0)
    bucket = bucket_ref[...]
    acc = jnp.full(bucket.shape, NEG_INF, F32)
    for k in range(NBUCKETS):
        acc = jnp.where(bucket == k, rb_ref[k, h], acc)
    o_ref[0] = acc


def _bias_table(rel_bias, bucket_t):
    return pl.pallas_call(
        _bias_kernel,
        grid=(NH,),
        in_specs=[
            pl.BlockSpec(memory_space=pltpu.SMEM),
            pl.BlockSpec((3 * BLK, BLK), lambda h: (0, 0)),
        ],
        out_specs=pl.BlockSpec((1, 3 * BLK, BLK), lambda h: (h // GQA, 0, h % GQA)),
        out_shape=jax.ShapeDtypeStruct((NKV, 3 * BLK, GQA * BLK), F32),
        compiler_params=_cparams(("arbitrary",)),
        name="rel_bias_table",
    )(rel_bias, bucket_t)


def _qkv_kernel(x_ref, mod_ref, gain_ref, wqt_ref, wk_ref, wvt_ref, qg_ref, kg_ref, qt_ref, k_ref, vt_ref):
    m = mod_ref[0, 0]
    h = _modulate(x_ref[...], gain_ref[0], m[0:1], m[1:2]).astype(BF16)
    nt = (((1,), (1,)), ((), ()))
    qt = lax.dot_general(wqt_ref[...], h, nt, preferred_element_type=F32)
    tm = qt.shape[1]
    q3 = qt.reshape(NH, HD, tm)
    q3 = q3 * lax.rsqrt(jnp.mean(q3 * q3, axis=1, keepdims=True) + EPS)
    qt_ref[...] = (q3.reshape(NH * HD, tm) * qg_ref[...]).astype(BF16)
    k = jnp.dot(h, wk_ref[...], preferred_element_type=F32)
    for g in range(NKV):
        kg = k[:, g * KPAD:(g + 1) * KPAD]
        ms = jnp.sum(kg * kg, axis=-1, keepdims=True) * (1.0 / HD)
        k_ref[:, g * KPAD:(g + 1) * KPAD] = (kg * lax.rsqrt(ms + EPS) * kg_ref[...]).astype(BF16)
    vt_ref[...] = lax.dot_general(wvt_ref[...], h, nt, preferred_element_type=F32).astype(BF16)


def _attn_kernel(qt_ref, kp_ref, kc_ref, kn_ref, vp_ref, vc_ref, vn_ref, x_ref, w_ref, mod_ref,
                 bias_ref, sink_ref, o_ref, att_ref):
    i = pl.program_id(0)
    kwin = jnp.concatenate([kp_ref[...], kc_ref[...], kn_ref[...]], axis=0)
    vwin = jnp.concatenate([vp_ref[...], vc_ref[...], vn_ref[...]], axis=1)
    ones_rows = (lax.broadcasted_iota(jnp.int32, (VROWS - HD, KWIN), 0) == 0).astype(BF16)
    vaug = [jnp.concatenate([vwin[g * HD:(g + 1) * HD], ones_rows], axis=0) for g in range(NKV)]
    key_pos = i * TQ - BLK + lax.broadcasted_iota(jnp.int32, (KWIN, 1), 0)
    key_mask = jnp.where((key_pos >= 0) & (key_pos < S), 0.0, NEG_INF).astype(BF16)
    lane = lax.broadcasted_iota(jnp.int32, (1, NKV * KPAD), 1)
    kwin = jnp.where(lane % KPAD == HD, key_mask, kwin)
    q_ones = (lax.broadcasted_iota(jnp.int32, (KPAD - HD, GQA * BLK), 0) == 0).astype(BF16)
    items = [(g, r) for g in range(NKV) for r in range(NSB)]

    def scores(g, r):
        kr = kwin[r * BLK:r * BLK + 3 * BLK, g * KPAD:(g + 1) * KPAD]
        qg = jnp.concatenate(
            [qt_ref[(GQA * g + hh) * HD:(GQA * g + hh + 1) * HD, r * BLK:(r + 1) * BLK] for hh in range(GQA)],
            axis=1)
        qa = jnp.concatenate([qg, q_ones], axis=0)
        return jnp.dot(kr, qa, preferred_element_type=F32)

    def probs(s, g, r):
        sink = sink_ref[g]
        s = s + bias_ref[g]
        mx = jnp.maximum(jnp.max(s, axis=0, keepdims=True), sink)
        return jnp.exp(s - mx).astype(BF16), jnp.exp(sink - mx)

    pending = [scores(*items[n]) for n in range(ATT_AHEAD)]
    for n, (g, r) in enumerate(items):
        s = pending.pop(0)
        if n + ATT_AHEAD < len(items):
            pending.append(scores(*items[n + ATT_AHEAD]))
        p, psink = probs(s, g, r)
        ot = jnp.dot(vaug[g][:, r * BLK:r * BLK + 3 * BLK], p, preferred_element_type=F32)
        ot = ot[:HD] * (1.0 / (ot[HD:HD + 1] + psink))
        for hh in range(GQA):
            hd0 = (GQA * g + hh) * HD
            att_ref[hd0:hd0 + HD, r * BLK:(r + 1) * BLK] = ot[:, hh * BLK:(hh + 1) * BLK]
    g1 = mod_ref[0, 0][2:3]
    att = att_ref[...].T.astype(BF16)
    y = jnp.dot(att, w_ref[...], preferred_element_type=F32)
    o_ref[...] = x_ref[...] + g1 * y


def _attn_weights(w_qkv, q_gain, k_gain, sink):
    wq, wk, wv = w_qkv[:, :NH * HD], w_qkv[:, NH * HD:(NH + NKV) * HD], w_qkv[:, (NH + NKV) * HD:]
    wqt = wq.T.astype(BF16)
    wvt = wv.T.astype(BF16)
    wk_pad = jnp.pad(wk.reshape(D, NKV, HD), ((0, 0), (0, 0), (0, KPAD - HD))).reshape(D, NKV * KPAD).astype(BF16)
    qg_col = jnp.tile(q_gain * (HD ** -0.5), NH).reshape(NH * HD, 1)
    kg_row = jnp.pad(k_gain, (0, KPAD - HD)).reshape(1, KPAD)
    sink_row = jnp.repeat(sink, BLK).reshape(NKV, 1, GQA * BLK)
    return wqt, wk_pad, wvt, qg_col, kg_row, sink_row


def _attn_layer(x, b, mod_all, gain, aw, w_out_bf, bias_tab, layer):
    wqt, wk_pad, wvt, qg_col, kg_row, sink_row = aw
    qt, k, vt = pl.pallas_call(
        _qkv_kernel,
        grid=(S // QKV_TM,),
        in_specs=[
            _x_spec(x, b, QKV_TM),
            _mod_spec(layer, b), _gain_spec(layer),
            pl.BlockSpec((NH * HD, D), lambda i: (0, 0)),
            pl.BlockSpec((D, NKV * KPAD), lambda i: (0, 0)),
            pl.BlockSpec((NKV * HD, D), lambda i: (0, 0)),
            pl.BlockSpec((NH * HD, 1), lambda i: (0, 0)),
            pl.BlockSpec((1, KPAD), lambda i: (0, 0)),
        ],
        out_specs=[
            pl.BlockSpec((NH * HD, QKV_TM), lambda i: (0, i)),
            pl.BlockSpec((QKV_TM, NKV * KPAD), lambda i: (i, 0)),
            pl.BlockSpec((NKV * HD, QKV_TM), lambda i: (0, i)),
        ],
        out_shape=[
            jax.ShapeDtypeStruct((NH * HD, S), BF16),
            jax.ShapeDtypeStruct((S, NKV * KPAD), BF16),
            jax.ShapeDtypeStruct((NKV * HD, S), BF16),
        ],
        compiler_params=_cparams(("arbitrary",)),
        name="attn_qkv",
    )(x, mod_all, gain, wqt, wk_pad, wvt, qg_col, kg_row)

    kw = NKV * KPAD
    vw = NKV * HD
    kprev = pl.BlockSpec((BLK, kw), lambda i: (jnp.maximum(i * NSB - 1, 0), 0))
    kcur = pl.BlockSpec((TQ, kw), lambda i: (i, 0))
    knext = pl.BlockSpec((BLK, kw), lambda i: (jnp.minimum((i + 1) * NSB, NBLK - 1), 0))
    vprev = pl.BlockSpec((vw, BLK), lambda i: (0, jnp.maximum(i * NSB - 1, 0)))
    vcur = pl.BlockSpec((vw, TQ), lambda i: (0, i))
    vnext = pl.BlockSpec((vw, BLK), lambda i: (0, jnp.minimum((i + 1) * NSB, NBLK - 1)))
    return pl.pallas_call(
        _attn_kernel,
        grid=(S // TQ,),
        in_specs=[
            pl.BlockSpec((NH * HD, TQ), lambda i: (0, i)),
            kprev, kcur, knext, vprev, vcur, vnext,
            _x_spec(x, b, TQ),
            pl.BlockSpec((NH * HD, D), lambda i: (0, 0)),
            _mod_spec(layer, b),
            pl.BlockSpec((NKV, 3 * BLK, GQA * BLK), lambda i: (0, 0, 0)),
            pl.BlockSpec((NKV, 1, GQA * BLK), lambda i: (0, 0, 0)),
        ],
        out_specs=pl.BlockSpec((TQ, D), lambda i: (i, 0)),
        out_shape=jax.ShapeDtypeStruct((S, D), F32),
        scratch_shapes=[pltpu.VMEM((NH * HD, TQ), F32)],
        compiler_params=_cparams(("arbitrary",)),
        name="attn_core",
    )(qt, k, k, k, vt, vt, vt, x, w_out_bf, mod_all, bias_tab, sink_row)


RT_TM = 512
EP = 128
FFN_TF = 1024
FFN_RT = 512


def _pack_bf16_pairs(h):
    hb = h.astype(BF16).astype(F32)
    lo = pltpu.bitcast(hb[:, :D // 2], jnp.int32)
    hi = pltpu.bitcast(hb[:, D // 2:], jnp.int32)
    return hi | lax.shift_right_logical(lo, jnp.int32(16))


def _router_kernel(x_ref, mod_ref, gain_ref, wr_ref, h_ref, aff_ref):
    m = mod_ref[0, 0]
    h = _modulate(x_ref[...], gain_ref[0], m[3:4], m[4:5])
    h_ref[...] = _pack_bf16_pairs(h)
    h1 = h.astype(BF16)
    h2 = (h - h1.astype(F32)).astype(BF16)
    wr = wr_ref[...]
    part = jnp.dot(h1, wr, preferred_element_type=F32)
    logits = part[:, :EP] + part[:, EP:] + jnp.dot(h2, wr[:, :EP], preferred_element_type=F32)
    lt = logits.T[:E]
    mx = jnp.max(lt, axis=0, keepdims=True)
    ex = jnp.exp(lt - mx)
    aff_ref[...] = ex / jnp.sum(ex, axis=0, keepdims=True)


def _select_kernel(aff_ref, tri_ref, pos_ref):
    aff = aff_ref[...]
    bits = pltpu.bitcast(aff, jnp.int32)
    rows = aff.shape[0]

    def count_ge(v):
        return jnp.sum((bits >= v).astype(jnp.int32), axis=1, keepdims=True)

    def body(t, cur):
        cand = cur | (jnp.int32(1) << (30 - t))
        return jnp.where(count_ge(cand) >= CAP, cand, cur)

    thr = lax.fori_loop(0, 31, body, jnp.zeros((rows, 1), jnp.int32))
    gt = bits > thr
    eq = bits == thr
    need = CAP - jnp.sum(gt.astype(jnp.int32), axis=1, keepdims=True)
    tri = tri_ref[...]

    def cumsum_excl(mask_f):
        off = jnp.zeros((rows, 1), F32)
        outs = []
        for j in range(S // 128):
            mj = mask_f[:, j * 128:(j + 1) * 128]
            loc = jnp.dot(mj.astype(BF16), tri, preferred_element_type=F32)
            outs.append(loc - mj + off)
            off = off + loc[:, 127:128]
        return jnp.concatenate(outs, axis=1)

    eq_rank = cumsum_excl(eq.astype(F32))
    sel = gt | (eq & (eq_rank < need.astype(F32)))
    pos = cumsum_excl(sel.astype(F32))
    pos_ref[...] = jnp.where(sel, pos.astype(jnp.int32), -1)


SC_CORES = 2
SC_SUBCORES = 16
SC_TILES = SC_CORES * SC_SUBCORES
DISPATCH_SLOTS = CAP * E // SC_TILES
DISPATCH_ROWS = 64
COMBINE_RANGE = 1024
COMBINE_ROWS = 32
SLAB = 128
NSLAB = D // SLAB


def _sc_mesh():
    return plsc.VectorSubcoreMesh(core_axis_name="c", subcore_axis_name="s",
                                  num_cores=SC_CORES, num_subcores=SC_SUBCORES)


def _dispatch_body(pos_hbm, aff_hbm, h_hbm, xin_hbm, idx_hbm, gate_hbm, pos_v, aff_v, idx_v, gate_v,
                   rows_a, rows_b, gsem_a, gsem_b, wsem_a, wsem_b):
    w = lax.axis_index("s") * SC_CORES + lax.axis_index("c")
    e = w // 2
    lo = (w % 2) * DISPATCH_SLOTS
    pltpu.sync_copy(pos_hbm.at[e], pos_v)
    pltpu.sync_copy(aff_hbm.at[e], aff_v)

    @pl.loop(0, S // 16)
    def _(i):
        p = pos_v[pl.ds(i * 16, 16)] - lo
        m = (p >= 0) & (p < DISPATCH_SLOTS)
        tok = lax.iota(jnp.int32, 16) + i * 16
        plsc.store_scatter(idx_v, [p], tok, mask=m)
        plsc.store_scatter(gate_v, [p], aff_v[pl.ds(i * 16, 16)], mask=m)

    pltpu.sync_copy(idx_v, idx_hbm.at[e, pl.ds(lo, DISPATCH_SLOTS)])
    pltpu.sync_copy(gate_v, gate_hbm.at[e, pl.ds(lo, DISPATCH_SLOTS)])

    bufs, gsems, wsems = (rows_a, rows_b), (gsem_a, gsem_b), (wsem_a, wsem_b)
    nchunk = DISPATCH_SLOTS // DISPATCH_ROWS

    def gather(j):
        return pltpu.async_copy(h_hbm.at[idx_v.at[pl.ds(j * DISPATCH_ROWS, DISPATCH_ROWS)]], bufs[j % 2], gsems[j % 2])

    pending_gather = gather(0)
    writes = [None, None]
    for j in range(nchunk):
        pending_gather.wait()
        writes[j % 2] = pltpu.async_copy(
            bufs[j % 2], xin_hbm.at[e, pl.ds(lo + j * DISPATCH_ROWS, DISPATCH_ROWS)], wsems[j % 2])
        if j + 1 < nchunk:
            if writes[(j + 1) % 2] is not None:
                writes[(j + 1) % 2].wait()
            pending_gather = gather(j + 1)
    writes[(nchunk - 2) % 2].wait()
    writes[(nchunk - 1) % 2].wait()


def _dispatch(pos, aff, h):
    return pl.kernel(
        _dispatch_body, mesh=_sc_mesh(),
        out_type=[jax.ShapeDtypeStruct((E, CAP, D // 2), jnp.int32),
                  jax.ShapeDtypeStruct((E, CAP), jnp.int32),
                  jax.ShapeDtypeStruct((E, CAP), F32)],
        scratch_types=[pltpu.VMEM((S,), jnp.int32), pltpu.VMEM((S,), F32),
                       pltpu.VMEM((DISPATCH_SLOTS,), jnp.int32), pltpu.VMEM((DISPATCH_SLOTS,), F32),
                       pltpu.VMEM((DISPATCH_ROWS, D // 2), jnp.int32), pltpu.VMEM((DISPATCH_ROWS, D // 2), jnp.int32),
                       pltpu.SemaphoreType.DMA, pltpu.SemaphoreType.DMA,
                       pltpu.SemaphoreType.DMA, pltpu.SemaphoreType.DMA],
        compiler_params=pltpu.CompilerParams(needs_layout_passes=False),
        name="moe_dispatch",
    )(pos, aff, h)


def _combine_body(y_hbm, idx_hbm, x_hbm, out_hbm, idx_v, li_v, *bufs):
    rows = bufs[:NSLAB]
    accs = bufs[NSLAB:]
    c = lax.axis_index("c")
    s = lax.axis_index("s")
    share = COMBINE_RANGE // SC_SUBCORES
    pltpu.sync_copy(idx_hbm.at[s], idx_v)
    lane = lax.iota(jnp.int32, 16)

    @pl.loop(0, S // COMBINE_RANGE // SC_CORES)
    def _(r):
        t0 = (r * SC_CORES + c) * COMBINE_RANGE
        row0 = t0 + s * share
        pltpu.sync_copy(tuple(x_hbm.at[pl.ds(row0, share), pl.ds(q * SLAB, SLAB)] for q in range(NSLAB)),
                        tuple(accs[q].at[pl.ds(s * share, share)] for q in range(NSLAB)))
        plsc.subcore_barrier()

        @pl.loop(0, CAP // COMBINE_ROWS)
        def _(j):
            hits = jnp.zeros((16,), jnp.int32)
            for v in range(COMBINE_ROWS // 16):
                t = idx_v[pl.ds(j * COMBINE_ROWS + v * 16, 16)] - t0
                ok = (t >= 0) & (t < COMBINE_RANGE)
                li_v[pl.ds(v * 16, 16)] = jnp.where(ok, t, COMBINE_RANGE + lane)
                hits = hits + plsc.all_reduce_population_count(ok)

            @pl.when(jnp.max(hits) > 0)
            def _():
                pltpu.sync_copy(
                    tuple(y_hbm.at[s, pl.ds(j * COMBINE_ROWS, COMBINE_ROWS), pl.ds(q * SLAB, SLAB)]
                          for q in range(NSLAB)),
                    tuple(rows))
                pltpu.sync_copy(tuple(rows), tuple(accs[q].at[li_v] for q in range(NSLAB)), add=True)

        plsc.subcore_barrier()
        pltpu.sync_copy(tuple(accs[q].at[pl.ds(s * share, share)] for q in range(NSLAB)),
                        tuple(out_hbm.at[pl.ds(row0, share), pl.ds(q * SLAB, SLAB)] for q in range(NSLAB)))


def _combine(y, idx, x):
    return pl.kernel(
        _combine_body, mesh=_sc_mesh(),
        out_type=jax.ShapeDtypeStruct((S, D), F32),
        scratch_types=[pltpu.VMEM((CAP,), jnp.int32), pltpu.VMEM((COMBINE_ROWS,), jnp.int32)]
        + [pltpu.VMEM((COMBINE_ROWS, SLAB), F32) for _ in range(NSLAB)]
        + [pltpu.VMEM_SHARED((COMBINE_RANGE + 16, SLAB), F32) for _ in range(NSLAB)],
        compiler_params=pltpu.CompilerParams(needs_layout_passes=False),
        name="moe_combine",
    )(y, idx, x)


def _ffn_kernel(x_ref, wg_ref, wu_ref, wd_ref, gate_ref, mod_ref, o_ref, xb_ref):
    f = pl.program_id(1)
    last = pl.num_programs(1) - 1

    @pl.when(f == 0)
    def _():
        xp = x_ref[0]
        xb_ref[:, :D // 2] = pltpu.bitcast(xp << 16, F32).astype(BF16)
        xb_ref[:, D // 2:] = pltpu.bitcast(xp & jnp.int32(-65536), F32).astype(BF16)

    wg = wg_ref[0, 0].astype(BF16)
    wu = wu_ref[0, 0].astype(BF16)
    wd = wd_ref[0, 0].astype(BF16)
    is_first = f == 0
    is_last = f == last
    g2 = jnp.where(is_last, mod_ref[0, 0][5:6], 1.0)
    for r in range(CAP // FFN_RT):
        rows = slice(r * FFN_RT, (r + 1) * FFN_RT)
        xr = xb_ref[rows, :]
        g = jnp.dot(xr, wg, preferred_element_type=F32)
        u = jnp.dot(xr, wu, preferred_element_type=F32)
        a = (g * _sigmoid(g) * u).astype(BF16)
        y = jnp.dot(a, wd, preferred_element_type=F32)
        prev = jnp.where(is_first, 0.0, o_ref[0, rows, :])
        gate = jnp.where(is_last, gate_ref[0, rows, :], 1.0)
        o_ref[0, rows, :] = (prev + y) * gate * g2


def _router(x, b, mod_all, gain, wr, layer):
    return pl.pallas_call(
        _router_kernel,
        grid=(S // RT_TM,),
        in_specs=[
            _x_spec(x, b, RT_TM),
            _mod_spec(layer, b), _gain_spec(layer),
            pl.BlockSpec((D, 2 * EP), lambda i: (0, 0)),
        ],
        out_specs=[
            pl.BlockSpec((RT_TM, D // 2), lambda i: (i, 0)),
            pl.BlockSpec((E, RT_TM), lambda i: (0, i)),
        ],
        out_shape=[
            jax.ShapeDtypeStruct((S, D // 2), jnp.int32),
            jax.ShapeDtypeStruct((E, S), F32),
        ],
        compiler_params=_cparams(("arbitrary",)),
        name="moe_router",
    )(x, mod_all, gain, wr)


def _select(aff, tri):
    return pl.pallas_call(
        _select_kernel,
        grid=(1,),
        in_specs=[
            pl.BlockSpec((E, S), lambda i: (0, 0)),
            pl.BlockSpec((128, 128), lambda i: (0, 0)),
        ],
        out_specs=pl.BlockSpec((E, S), lambda i: (0, 0)),
        out_shape=jax.ShapeDtypeStruct((E, S), jnp.int32),
        compiler_params=_cparams(("arbitrary",)),
        name="moe_select",
    )(aff, tri)


def _ffn(xin, gate, b, mod_all, w_gate, w_up, w_down, layer):
    return pl.pallas_call(
        _ffn_kernel,
        grid=(E, F // FFN_TF),
        in_specs=[
            pl.BlockSpec((1, CAP, D // 2), lambda e, f: (e, 0, 0)),
            pl.BlockSpec((1, 1, D, FFN_TF), lambda e, f: (layer, e, 0, f)),
            pl.BlockSpec((1, 1, D, FFN_TF), lambda e, f: (layer, e, 0, f)),
            pl.BlockSpec((1, 1, FFN_TF, D), lambda e, f: (layer, e, f, 0)),
            pl.BlockSpec((1, CAP, 1), lambda e, f: (e, 0, 0)),
            pl.BlockSpec((1, 1, 6, D), lambda e, f: (layer, b, 0, 0)),
        ],
        out_specs=pl.BlockSpec((1, CAP, D), lambda e, f: (e, 0, 0)),
        out_shape=jax.ShapeDtypeStruct((E, CAP, D), F32),
        scratch_shapes=[pltpu.VMEM((CAP, D), BF16)],
        compiler_params=_cparams(("arbitrary", "arbitrary"), 48),
        name="moe_ffn",
    )(xin, w_gate, w_up, w_down, gate.reshape(E, CAP, 1), mod_all)


def kernel(x, c, w_ada, b_ada, norm_mix, norm_ffn, w_fourier_out, w_qkv, w_attn_out, q_gain, k_gain,
           sink, rel_bias, w_router, w_gate, w_up, w_down):
    m0, m2 = _dft_tables()
    m1 = _stage1_table()
    bucket_t = _bucket_table()
    tri = jnp.asarray(np.triu(np.ones((128, 128), np.float32)), BF16)
    mod_all = _ada(c, w_ada, b_ada)
    bias_tab = _bias_table(rel_bias, bucket_t)
    gain_mix = norm_mix.reshape(DEPTH, 1, D)
    gain_ffn = norm_ffn.reshape(DEPTH, 1, D)
    xs = [x, x]
    for layer in range(DEPTH):
        j = layer // 2
        if layer % 2 == 0:
            w_out_bf = w_fourier_out[j].astype(BF16)
            xs = [_fourier_layer(xs[b], b, mod_all, gain_mix, w_out_bf, layer, m0, m1, m2) for b in range(B)]
        else:
            aw = _attn_weights(w_qkv[j], q_gain[j], k_gain[j], sink[j])
            w_out_bf = w_attn_out[j].astype(BF16)
            xs = [_attn_layer(xs[b], b, mod_all, gain_mix, aw, w_out_bf, bias_tab, layer) for b in range(B)]
        wr = jnp.pad(w_router[layer], ((0, 0), (0, EP - E)))
        wr1 = wr.astype(BF16)
        wr = jnp.concatenate([wr1, (wr - wr1.astype(F32)).astype(BF16)], axis=1)
        routed = [_router(xs[b], b, mod_all, gain_ffn, wr, layer) for b in range(B)]
        pos = [_select(routed[b][1], tri) for b in range(B)]
        disp = [_dispatch(pos[b], routed[b][1], routed[b][0]) for b in range(B)]
        ys = [_ffn(disp[b][0], disp[b][2], b, mod_all, w_gate, w_up, w_down, layer) for b in range(B)]
        xs = [_combine(ys[b], disp[b][1], xs[b]) for b in range(B)]
    return jnp.stack(xs)
```

```python
import math

import numpy as np
import jax
import jax.numpy as jnp
from jax import lax
from jax.experimental import pallas as pl
from jax.experimental.pallas import tpu as pltpu
from jax.experimental.pallas import tpu_sc as plsc

D = 1024
B = 2
S = 8192
DEPTH = 4
GROUPS = 4
GD = D // GROUPS
HD = 64
NH = 16
NKV = 4
GQA = NH // NKV
WINDOW = 128
BLK = 128
NBUCKETS = 32
MAXDIST = 128
E = 16
CAP = 2 * S // E
F = 2 * D
EPS = 1e-6
NEG_INF = -1e30

N1 = 128
N2 = 64
INV_NORM = 1.0 / math.sqrt(S * GD)

F32 = jnp.float32
BF16 = jnp.bfloat16


def _cparams(sem, vmem_mb=48):
    return pltpu.CompilerParams(dimension_semantics=sem, vmem_limit_bytes=vmem_mb * 1024 * 1024)


def _dft_tables():
    c = np.arange(GD)
    ang0 = 2.0 * np.pi * ((c[:, None] * c[None, :]) % GD) / GD
    m0 = np.concatenate([np.cos(ang0), -np.sin(ang0)], axis=1)
    k2 = np.arange(N2)
    ang2 = 2.0 * np.pi * ((k2[:, None] * k2[None, :]) % N2) / N2
    m2 = np.concatenate([np.cos(ang2), np.sin(ang2)], axis=1)
    return jnp.asarray(m0, BF16), jnp.asarray(m2, BF16)


def _stage1_table():
    s2 = np.arange(N2)[:, None, None]
    k1 = np.arange(N1)[None, :, None]
    s1 = np.arange(N1)[None, None, :]
    th = ((k1 * (N2 * s1 + s2)) % S) * (2.0 * np.pi / S)
    co, si = np.cos(th), np.sin(th)
    top = np.concatenate([co, si], axis=2)
    bot = np.concatenate([-si, co], axis=2)
    return jnp.asarray(np.concatenate([top, bot], axis=1).astype(np.float32), BF16)


def _bucket_table():
    q_off = np.arange(BLK)
    k_off = np.arange(3 * BLK) - BLK
    rel = k_off[:, None] - q_off[None, :]
    half = NBUCKETS // 2
    max_exact = half // 2
    ret = np.where(rel > 0, half, 0)
    n = np.abs(rel)
    nf = np.maximum(n, 1).astype(np.float32)
    ratio = (np.log(nf / np.float32(max_exact)) / np.float32(math.log(MAXDIST / max_exact))).astype(np.float32)
    large = max_exact + (ratio * np.float32(half - max_exact)).astype(np.int32)
    large = np.minimum(large, half - 1)
    bucket = ret + np.where(n < max_exact, n, large)
    return jnp.asarray(np.where(np.abs(rel) <= WINDOW, bucket, -1).astype(np.int32))


def _modulate(x, gain, shift, scale):
    ms = jnp.mean(x * x, axis=-1, keepdims=True)
    return x * lax.rsqrt(ms + EPS) * (gain * (1.0 + scale)) + shift


def _sigmoid(x):
    return 1.0 / (1.0 + jnp.exp(-x))


ADA_TN = 768


def _ada_kernel(ct_ref, w_ref, b_ref, o_ref):
    ct = ct_ref[...]
    ca = ct * _sigmoid(ct)
    w = w_ref[0]
    for b in range(B):
        o_ref[0, b:b + 1, :] = jnp.sum(w * ca[:, b:b + 1], axis=0, keepdims=True) + b_ref[0]


def _ada(c, w_ada, b_ada):
    out = pl.pallas_call(
        _ada_kernel,
        grid=(DEPTH, 6 * D // ADA_TN),
        in_specs=[
            pl.BlockSpec((D, B), lambda l, j: (0, 0)),
            pl.BlockSpec((1, D, ADA_TN), lambda l, j: (l, 0, j)),
            pl.BlockSpec((1, 1, ADA_TN), lambda l, j: (l, 0, j)),
        ],
        out_specs=pl.BlockSpec((1, B, ADA_TN), lambda l, j: (l, 0, j)),
        out_shape=jax.ShapeDtypeStruct((DEPTH, B, 6 * D), F32),
        compiler_params=_cparams(("arbitrary", "arbitrary"), 32),
        name="ada_mod",
    )(c.T, w_ada, b_ada.reshape(DEPTH, 1, 6 * D))
    return out.reshape(DEPTH, B, 6, D)


def _mod_spec(layer, b):
    return pl.BlockSpec((1, 1, 6, D), lambda *_: (layer, b, 0, 0))


def _gain_spec(layer):
    return pl.BlockSpec((1, 1, D), lambda *_: (layer, 0, 0))


def _x_spec(x, b, tm):
    if x.ndim == 3:
        return pl.BlockSpec((None, tm, D), lambda i: (b, i, 0))
    return pl.BlockSpec((tm, D), lambda i: (i, 0))


F0_TM = 512
F1_J = 8
F2_TK = 8
F3_M = 4


def _f0_kernel(x_ref, mod_ref, gain_ref, m0_ref, o_ref):
    m = mod_ref[0, 0]
    h = _modulate(x_ref[...], gain_ref[0], m[0:1], m[1:2]).astype(BF16)
    m0 = m0_ref[...]
    for g in range(GROUPS):
        r = jnp.dot(h[:, g * GD:(g + 1) * GD], m0, preferred_element_type=F32)
        o_ref[0, :, g * GD:(g + 1) * GD] = r[:, :GD].astype(BF16)
        o_ref[1, :, g * GD:(g + 1) * GD] = r[:, GD:].astype(BF16)


def _f1_kernel(w_ref, m1_ref, o_ref):
    for j in range(F1_J):
        o_ref[:, j * D:(j + 1) * D] = jnp.dot(
            m1_ref[j], w_ref[:, j * D:(j + 1) * D], preferred_element_type=F32).astype(BF16)


def _f2_kernel(u_ref, m2_ref, o_ref):
    m2 = m2_ref[...]
    for k in range(F2_TK):
        u = jnp.concatenate([u_ref[0, k], u_ref[1, k]], axis=0)
        o_ref[k] = jnp.dot(m2, u, preferred_element_type=F32).astype(BF16)


def _f3_kernel(mp_ref, x_ref, w_ref, mod_ref, o_ref):
    g1 = mod_ref[0, 0][2:3]
    a = jnp.concatenate([mp_ref[:, j * D:(j + 1) * D] for j in range(F3_M)], axis=0)
    y = jnp.dot(a, w_ref[...], preferred_element_type=F32)
    o_ref[...] = x_ref[...] + (g1 * INV_NORM) * y


def _fourier_layer(x, b, mod_all, gain, w_out_bf, layer, m0, m1, m2):
    wc = pl.pallas_call(
        _f0_kernel,
        grid=(S // F0_TM,),
        in_specs=[
            _x_spec(x, b, F0_TM),
            _mod_spec(layer, b), _gain_spec(layer),
            pl.BlockSpec((GD, 2 * GD), lambda i: (0, 0)),
        ],
        out_specs=pl.BlockSpec((2, F0_TM, D), lambda i: (0, i, 0)),
        out_shape=jax.ShapeDtypeStruct((2, S, D), BF16),
        compiler_params=_cparams(("arbitrary",)),
        name="fourier_chan",
    )(x, mod_all, gain, m0)
    wc = wc.reshape(2 * N1, N2 * D)
    u = pl.pallas_call(
        _f1_kernel,
        grid=(N2 // F1_J,),
        in_specs=[
            pl.BlockSpec((2 * N1, F1_J * D), lambda i: (0, i)),
            pl.BlockSpec((F1_J, 2 * N1, 2 * N1), lambda i: (i, 0, 0)),
        ],
        out_specs=pl.BlockSpec((2 * N1, F1_J * D), lambda i: (0, i)),
        out_shape=jax.ShapeDtypeStruct((2 * N1, N2 * D), BF16),
        compiler_params=_cparams(("arbitrary",)),
        name="fourier_seq1",
    )(wc, m1)
    u = u.reshape(2, N1, N2, D)
    mp = pl.pallas_call(
        _f2_kernel,
        grid=(N1 // F2_TK,),
        in_specs=[
            pl.BlockSpec((2, F2_TK, N2, D), lambda i: (0, i, 0, 0)),
            pl.BlockSpec((N2, 2 * N2), lambda i: (0, 0)),
        ],
        out_specs=pl.BlockSpec((F2_TK, N2, D), lambda i: (i, 0, 0)),
        out_shape=jax.ShapeDtypeStruct((N1, N2, D), BF16),
        compiler_params=_cparams(("arbitrary",)),
        name="fourier_seq2",
    )(u, m2)
    mp = mp.reshape(N1, N2 * D)
    return pl.pallas_call(
        _f3_kernel,
        grid=(N2 // F3_M,),
        in_specs=[
            pl.BlockSpec((N1, F3_M * D), lambda i: (0, i)),
            _x_spec(x, b, F3_M * N1),
            pl.BlockSpec((D, D), lambda i: (0, 0)),
            _mod_spec(layer, b),
        ],
        out_specs=pl.BlockSpec((F3_M * N1, D), lambda i: (i, 0)),
        out_shape=jax.ShapeDtypeStruct((S, D), F32),
        compiler_params=_cparams(("arbitrary",)),
        name="fourier_out",
    )(mp, x, w_out_bf, mod_all)


QKV_TM = 512
TQ = 256
NSB = TQ // BLK
KWIN = TQ + 2 * BLK
NBLK = S // BLK
KPAD = 128
VROWS = HD + 16
ATT_AHEAD = 2


def _bias_kernel(rb_ref, bucket_ref, o_ref):
    h = pl.program_id(0)
    bucket = bucket_ref[...]
    acc = jnp.full(bucket.shape, NEG_INF, F32)
    for k in range(NBUCKETS):
        acc = jnp.where(bucket == k, rb_ref[k, h], acc)
    o_ref[0] = acc


def _bias_table(rel_bias, bucket_t):
    return pl.pallas_call(
        _bias_kernel,
        grid=(NH,),
        in_specs=[
            pl.BlockSpec(memory_space=pltpu.SMEM),
            pl.BlockSpec((3 * BLK, BLK), lambda h: (0, 0)),
        ],
        out_specs=pl.BlockSpec((1, 3 * BLK, BLK), lambda h: (h // GQA, 0, h % GQA)),
        out_shape=jax.ShapeDtypeStruct((NKV, 3 * BLK, GQA * BLK), F32),
        compiler_params=_cparams(("arbitrary",)),
        name="rel_bias_table",
    )(rel_bias, bucket_t)


def _qkv_kernel(x_ref, mod_ref, gain_ref, wqt_ref, wk_ref, wvt_ref, qg_ref, kg_ref, qt_ref, k_ref, vt_ref):
    m = mod_ref[0, 0]
    h = _modulate(x_ref[...], gain_ref[0], m[0:1], m[1:2]).astype(BF16)
    nt = (((1,), (1,)), ((), ()))
    qt = lax.dot_general(wqt_ref[...], h, nt, preferred_element_type=F32)
    tm = qt.shape[1]
    q3 = qt.reshape(NH, HD, tm)
    q3 = q3 * lax.rsqrt(jnp.mean(q3 * q3, axis=1, keepdims=True) + EPS)
    qt_ref[...] = (q3.reshape(NH * HD, tm) * qg_ref[...]).astype(BF16)
    k = jnp.dot(h, wk_ref[...], preferred_element_type=F32)
    for g in range(NKV):
        kg = k[:, g * KPAD:(g + 1) * KPAD]
        ms = jnp.sum(kg * kg, axis=-1, keepdims=True) * (1.0 / HD)
        k_ref[:, g * KPAD:(g + 1) * KPAD] = (kg * lax.rsqrt(ms + EPS) * kg_ref[...]).astype(BF16)
    vt_ref[...] = lax.dot_general(wvt_ref[...], h, nt, preferred_element_type=F32).astype(BF16)


def _attn_kernel(qt_ref, kp_ref, kc_ref, kn_ref, vp_ref, vc_ref, vn_ref, x_ref, w_ref, mod_ref,
                 bias_ref, sink_ref, o_ref, att_ref, s_ref):
    i = pl.program_id(0)
    kwin = jnp.concatenate([kp_ref[...], kc_ref[...], kn_ref[...]], axis=0)
    vwin = jnp.concatenate([vp_ref[...], vc_ref[...], vn_ref[...]], axis=1)
    ones_rows = (lax.broadcasted_iota(jnp.int32, (VROWS - HD, KWIN), 0) == 0).astype(BF16)
    vaug = [jnp.concatenate([vwin[g * HD:(g + 1) * HD], ones_rows], axis=0) for g in range(NKV)]
    key_pos = i * TQ - BLK + lax.broadcasted_iota(jnp.int32, (KWIN, 1), 0)
    key_mask = jnp.where((key_pos >= 0) & (key_pos < S), 0.0, NEG_INF).astype(BF16)
    lane = lax.broadcasted_iota(jnp.int32, (1, NKV * KPAD), 1)
    kwin = jnp.where(lane % KPAD == HD, key_mask, kwin)
    q_ones = (lax.broadcasted_iota(jnp.int32, (KPAD - HD, GQA * BLK), 0) == 0).astype(BF16)
    items = [(g, r) for g in range(NKV) for r in range(NSB)]

    def scores(g, r):
        kr = kwin[r * BLK:r * BLK + 3 * BLK, g * KPAD:(g + 1) * KPAD]
        qg = jnp.concatenate(
            [qt_ref[(GQA * g + hh) * HD:(GQA * g + hh + 1) * HD, r * BLK:(r + 1) * BLK] for hh in range(GQA)],
            axis=1)
        qa = jnp.concatenate([qg, q_ones], axis=0)
        return jnp.dot(kr, qa, preferred_element_type=F32)

    def probs(s, g, r):
        sink = sink_ref[g]
        s = s + bias_ref[g]
        mx = jnp.maximum(jnp.max(s, axis=0, keepdims=True), sink)
        return jnp.exp(s - mx).astype(BF16), jnp.exp(sink - mx)

    nslot = ATT_AHEAD + 1
    for n in range(ATT_AHEAD):
        s_ref[n % nslot] = scores(*items[n])
    for n, (g, r) in enumerate(items):
        if n + ATT_AHEAD < len(items):
            s_ref[(n + ATT_AHEAD) % nslot] = scores(*items[n + ATT_AHEAD])
        p, psink = probs(s_ref[n % nslot], g, r)
        ot = jnp.dot(vaug[g][:, r * BLK:r * BLK + 3 * BLK], p, preferred_element_type=F32)
        ot = ot[:HD] * (1.0 / (ot[HD:HD + 1] + psink))
        for hh in range(GQA):
            hd0 = (GQA * g + hh) * HD
            att_ref[hd0:hd0 + HD, r * BLK:(r + 1) * BLK] = ot[:, hh * BLK:(hh + 1) * BLK]
    g1 = mod_ref[0, 0][2:3]
    att = att_ref[...].T.astype(BF16)
    y = jnp.dot(att, w_ref[...], preferred_element_type=F32)
    o_ref[...] = x_ref[...] + g1 * y


def _attn_weights(w_qkv, q_gain, k_gain, sink):
    wq, wk, wv = w_qkv[:, :NH * HD], w_qkv[:, NH * HD:(NH + NKV) * HD], w_qkv[:, (NH + NKV) * HD:]
    wqt = wq.T.astype(BF16)
    wvt = wv.T.astype(BF16)
    wk_pad = jnp.pad(wk.reshape(D, NKV, HD), ((0, 0), (0, 0), (0, KPAD - HD))).reshape(D, NKV * KPAD).astype(BF16)
    qg_col = jnp.tile(q_gain * (HD ** -0.5), NH).reshape(NH * HD, 1)
    kg_row = jnp.pad(k_gain, (0, KPAD - HD)).reshape(1, KPAD)
    sink_row = jnp.repeat(sink, BLK).reshape(NKV, 1, GQA * BLK)
    return wqt, wk_pad, wvt, qg_col, kg_row, sink_row


def _attn_layer(x, b, mod_all, gain, aw, w_out_bf, bias_tab, layer):
    wqt, wk_pad, wvt, qg_col, kg_row, sink_row = aw
    qt, k, vt = pl.pallas_call(
        _qkv_kernel,
        grid=(S // QKV_TM,),
        in_specs=[
            _x_spec(x, b, QKV_TM),
            _mod_spec(layer, b), _gain_spec(layer),
            pl.BlockSpec((NH * HD, D), lambda i: (0, 0)),
            pl.BlockSpec((D, NKV * KPAD), lambda i: (0, 0)),
            pl.BlockSpec((NKV * HD, D), lambda i: (0, 0)),
            pl.BlockSpec((NH * HD, 1), lambda i: (0, 0)),
            pl.BlockSpec((1, KPAD), lambda i: (0, 0)),
        ],
        out_specs=[
            pl.BlockSpec((NH * HD, QKV_TM), lambda i: (0, i)),
            pl.BlockSpec((QKV_TM, NKV * KPAD), lambda i: (i, 0)),
            pl.BlockSpec((NKV * HD, QKV_TM), lambda i: (0, i)),
        ],
        out_shape=[
            jax.ShapeDtypeStruct((NH * HD, S), BF16),
            jax.ShapeDtypeStruct((S, NKV * KPAD), BF16),
            jax.ShapeDtypeStruct((NKV * HD, S), BF16),
        ],
        compiler_params=_cparams(("arbitrary",)),
        name="attn_qkv",
    )(x, mod_all, gain, wqt, wk_pad, wvt, qg_col, kg_row)

    kw = NKV * KPAD
    vw = NKV * HD
    kprev = pl.BlockSpec((BLK, kw), lambda i: (jnp.maximum(i * NSB - 1, 0), 0))
    kcur = pl.BlockSpec((TQ, kw), lambda i: (i, 0))
    knext = pl.BlockSpec((BLK, kw), lambda i: (jnp.minimum((i + 1) * NSB, NBLK - 1), 0))
    vprev = pl.BlockSpec((vw, BLK), lambda i: (0, jnp.maximum(i * NSB - 1, 0)))
    vcur = pl.BlockSpec((vw, TQ), lambda i: (0, i))
    vnext = pl.BlockSpec((vw, BLK), lambda i: (0, jnp.minimum((i + 1) * NSB, NBLK - 1)))
    return pl.pallas_call(
        _attn_kernel,
        grid=(S // TQ,),
        in_specs=[
            pl.BlockSpec((NH * HD, TQ), lambda i: (0, i)),
            kprev, kcur, knext, vprev, vcur, vnext,
            _x_spec(x, b, TQ),
            pl.BlockSpec((NH * HD, D), lambda i: (0, 0)),
            _mod_spec(layer, b),
            pl.BlockSpec((NKV, 3 * BLK, GQA * BLK), lambda i: (0, 0, 0)),
            pl.BlockSpec((NKV, 1, GQA * BLK), lambda i: (0, 0, 0)),
        ],
        out_specs=pl.BlockSpec((TQ, D), lambda i: (i, 0)),
        out_shape=jax.ShapeDtypeStruct((S, D), F32),
        scratch_shapes=[pltpu.VMEM((NH * HD, TQ), F32), pltpu.VMEM((ATT_AHEAD + 1, 3 * BLK, GQA * BLK), F32)],
        compiler_params=_cparams(("arbitrary",)),
        name="attn_core",
    )(qt, k, k, k, vt, vt, vt, x, w_out_bf, mod_all, bias_tab, sink_row)


RT_TM = 512
EP = 128
FFN_TF = 1024
FFN_RT = 512
FFN_WT = 256


def _pack_bf16_pairs(h):
    hb = h.astype(BF16).astype(F32)
    lo = pltpu.bitcast(hb[:, :D // 2], jnp.int32)
    hi = pltpu.bitcast(hb[:, D // 2:], jnp.int32)
    return hi | lax.shift_right_logical(lo, jnp.int32(16))


def _router_kernel(x_ref, mod_ref, gain_ref, wr_ref, h_ref, aff_ref):
    m = mod_ref[0, 0]
    h = _modulate(x_ref[...], gain_ref[0], m[3:4], m[4:5])
    h_ref[...] = _pack_bf16_pairs(h)
    h1 = h.astype(BF16)
    h2 = (h - h1.astype(F32)).astype(BF16)
    wr = wr_ref[...]
    part = jnp.dot(h1, wr, preferred_element_type=F32)
    logits = part[:, :EP] + part[:, EP:] + jnp.dot(h2, wr[:, :EP], preferred_element_type=F32)
    lt = logits.T[:E]
    mx = jnp.max(lt, axis=0, keepdims=True)
    ex = jnp.exp(lt - mx)
    aff_ref[...] = ex / jnp.sum(ex, axis=0, keepdims=True)


def _select_kernel(aff_ref, tri_ref, pos_ref):
    aff = aff_ref[...]
    bits = pltpu.bitcast(aff, jnp.int32)
    rows = aff.shape[0]

    def count_ge(v):
        return jnp.sum((bits >= v).astype(jnp.int32), axis=1, keepdims=True)

    def body(t, cur):
        cand = cur | (jnp.int32(1) << (30 - t))
        return jnp.where(count_ge(cand) >= CAP, cand, cur)

    thr = lax.fori_loop(0, 31, body, jnp.zeros((rows, 1), jnp.int32))
    gt = bits > thr
    eq = bits == thr
    need = CAP - jnp.sum(gt.astype(jnp.int32), axis=1, keepdims=True)
    tri = tri_ref[...]

    def cumsum_excl(mask_f):
        off = jnp.zeros((rows, 1), F32)
        outs = []
        for j in range(S // 128):
            mj = mask_f[:, j * 128:(j + 1) * 128]
            loc = jnp.dot(mj.astype(BF16), tri, preferred_element_type=F32)
            outs.append(loc - mj + off)
            off = off + loc[:, 127:128]
        return jnp.concatenate(outs, axis=1)

    eq_rank = cumsum_excl(eq.astype(F32))
    sel = gt | (eq & (eq_rank < need.astype(F32)))
    pos = cumsum_excl(sel.astype(F32))
    pos_ref[...] = jnp.where(sel, pos.astype(jnp.int32), -1)


SC_CORES = 2
SC_SUBCORES = 16
SC_TILES = SC_CORES * SC_SUBCORES
DISPATCH_SLOTS = CAP * E // SC_TILES
DISPATCH_ROWS = 64
COMBINE_RANGE = 1024
COMBINE_ROWS = 32
SLAB = 128
NSLAB = D // SLAB


def _sc_mesh():
    return plsc.VectorSubcoreMesh(core_axis_name="c", subcore_axis_name="s",
                                  num_cores=SC_CORES, num_subcores=SC_SUBCORES)


def _dispatch_body(pos_hbm, aff_hbm, h_hbm, xin_hbm, idx_hbm, gate_hbm, pos_v, aff_v, idx_v, gate_v,
                   rows_a, rows_b, gsem_a, gsem_b, wsem_a, wsem_b):
    w = lax.axis_index("s") * SC_CORES + lax.axis_index("c")
    e = w // 2
    lo = (w % 2) * DISPATCH_SLOTS
    pltpu.sync_copy(pos_hbm.at[e], pos_v)
    pltpu.sync_copy(aff_hbm.at[e], aff_v)

    @pl.loop(0, S // 16)
    def _(i):
        p = pos_v[pl.ds(i * 16, 16)] - lo
        m = (p >= 0) & (p < DISPATCH_SLOTS)
        tok = lax.iota(jnp.int32, 16) + i * 16
        plsc.store_scatter(idx_v, [p], tok, mask=m)
        plsc.store_scatter(gate_v, [p], aff_v[pl.ds(i * 16, 16)], mask=m)

    pltpu.sync_copy(idx_v, idx_hbm.at[e, pl.ds(lo, DISPATCH_SLOTS)])
    pltpu.sync_copy(gate_v, gate_hbm.at[e, pl.ds(lo, DISPATCH_SLOTS)])

    bufs, gsems, wsems = (rows_a, rows_b), (gsem_a, gsem_b), (wsem_a, wsem_b)
    nchunk = DISPATCH_SLOTS // DISPATCH_ROWS

    def gather(j):
        return pltpu.async_copy(h_hbm.at[idx_v.at[pl.ds(j * DISPATCH_ROWS, DISPATCH_ROWS)]], bufs[j % 2], gsems[j % 2])

    pending_gather = gather(0)
    writes = [None, None]
    for j in range(nchunk):
        pending_gather.wait()
        writes[j % 2] = pltpu.async_copy(
            bufs[j % 2], xin_hbm.at[e, pl.ds(lo + j * DISPATCH_ROWS, DISPATCH_ROWS)], wsems[j % 2])
        if j + 1 < nchunk:
            if writes[(j + 1) % 2] is not None:
                writes[(j + 1) % 2].wait()
            pending_gather = gather(j + 1)
    writes[(nchunk - 2) % 2].wait()
    writes[(nchunk - 1) % 2].wait()


def _dispatch(pos, aff, h):
    return pl.kernel(
        _dispatch_body, mesh=_sc_mesh(),
        out_type=[jax.ShapeDtypeStruct((E, CAP, D // 2), jnp.int32),
                  jax.ShapeDtypeStruct((E, CAP), jnp.int32),
                  jax.ShapeDtypeStruct((E, CAP), F32)],
        scratch_types=[pltpu.VMEM((S,), jnp.int32), pltpu.VMEM((S,), F32),
                       pltpu.VMEM((DISPATCH_SLOTS,), jnp.int32), pltpu.VMEM((DISPATCH_SLOTS,), F32),
                       pltpu.VMEM((DISPATCH_ROWS, D // 2), jnp.int32), pltpu.VMEM((DISPATCH_ROWS, D // 2), jnp.int32),
                       pltpu.SemaphoreType.DMA, pltpu.SemaphoreType.DMA,
                       pltpu.SemaphoreType.DMA, pltpu.SemaphoreType.DMA],
        compiler_params=pltpu.CompilerParams(needs_layout_passes=False),
        name="moe_dispatch",
    )(pos, aff, h)


def _combine_body(y_hbm, idx_hbm, x_hbm, out_hbm, idx_v, li_v, *bufs):
    rows = bufs[:NSLAB]
    accs = bufs[NSLAB:]
    c = lax.axis_index("c")
    s = lax.axis_index("s")
    share = COMBINE_RANGE // SC_SUBCORES
    pltpu.sync_copy(idx_hbm.at[s], idx_v)
    lane = lax.iota(jnp.int32, 16)

    @pl.loop(0, S // COMBINE_RANGE // SC_CORES)
    def _(r):
        t0 = (r * SC_CORES + c) * COMBINE_RANGE
        row0 = t0 + s * share
        pltpu.sync_copy(tuple(x_hbm.at[pl.ds(row0, share), pl.ds(q * SLAB, SLAB)] for q in range(NSLAB)),
                        tuple(accs[q].at[pl.ds(s * share, share)] for q in range(NSLAB)))
        plsc.subcore_barrier()

        @pl.loop(0, CAP // COMBINE_ROWS)
        def _(j):
            hits = jnp.zeros((16,), jnp.int32)
            for v in range(COMBINE_ROWS // 16):
                t = idx_v[pl.ds(j * COMBINE_ROWS + v * 16, 16)] - t0
                ok = (t >= 0) & (t < COMBINE_RANGE)
                li_v[pl.ds(v * 16, 16)] = jnp.where(ok, t, COMBINE_RANGE + lane)
                hits = hits + plsc.all_reduce_population_count(ok)

            @pl.when(jnp.max(hits) > 0)
            def _():
                pltpu.sync_copy(
                    tuple(y_hbm.at[s, pl.ds(j * COMBINE_ROWS, COMBINE_ROWS), pl.ds(q * SLAB, SLAB)]
                          for q in range(NSLAB)),
                    tuple(rows))
                pltpu.sync_copy(tuple(rows), tuple(accs[q].at[li_v] for q in range(NSLAB)), add=True)

        plsc.subcore_barrier()
        pltpu.sync_copy(tuple(accs[q].at[pl.ds(s * share, share)] for q in range(NSLAB)),
                        tuple(out_hbm.at[pl.ds(row0, share), pl.ds(q * SLAB, SLAB)] for q in range(NSLAB)))


def _combine(y, idx, x):
    return pl.kernel(
        _combine_body, mesh=_sc_mesh(),
        out_type=jax.ShapeDtypeStruct((S, D), F32),
        scratch_types=[pltpu.VMEM((CAP,), jnp.int32), pltpu.VMEM((COMBINE_ROWS,), jnp.int32)]
        + [pltpu.VMEM((COMBINE_ROWS, SLAB), F32) for _ in range(NSLAB)]
        + [pltpu.VMEM_SHARED((COMBINE_RANGE + 16, SLAB), F32) for _ in range(NSLAB)],
        compiler_params=pltpu.CompilerParams(needs_layout_passes=False),
        name="moe_combine",
    )(y, idx, x)


def _ffn_kernel(x_ref, wg_ref, wu_ref, wd_ref, gate_ref, mod_ref, o_ref, xb_ref):
    f = pl.program_id(1)
    last = pl.num_programs(1) - 1

    @pl.when(f == 0)
    def _():
        xp = x_ref[0]
        xb_ref[:, :D // 2] = pltpu.bitcast(xp << 16, F32).astype(BF16)
        xb_ref[:, D // 2:] = pltpu.bitcast(xp & jnp.int32(-65536), F32).astype(BF16)

    nt = FFN_TF // FFN_WT
    wg = [wg_ref[0, 0, :, j * FFN_WT:(j + 1) * FFN_WT].astype(BF16) for j in range(nt)]
    wu = [wu_ref[0, 0, :, j * FFN_WT:(j + 1) * FFN_WT].astype(BF16) for j in range(nt)]
    wd = wd_ref[0, 0].astype(BF16)
    is_first = f == 0
    is_last = f == last
    g2 = jnp.where(is_last, mod_ref[0, 0][5:6], 1.0)
    g_row = gate_ref[pl.ds(pl.program_id(0), 1), :]
    g_col = jnp.broadcast_to(g_row, (128, CAP)).T[:, 0:1]
    for r in range(CAP // FFN_RT):
        rows = slice(r * FFN_RT, (r + 1) * FFN_RT)
        xr = xb_ref[rows, :]
        acts = []
        for j in range(nt):
            g = jnp.dot(xr, wg[j], preferred_element_type=F32)
            u = jnp.dot(xr, wu[j], preferred_element_type=F32)
            acts.append((g * _sigmoid(g) * u).astype(BF16))
        y = jnp.dot(jnp.concatenate(acts, axis=1), wd, preferred_element_type=F32)
        prev = jnp.where(is_first, 0.0, o_ref[0, rows, :])
        gate = jnp.where(is_last, g_col[rows, :], 1.0)
        o_ref[0, rows, :] = (prev + y) * gate * g2


def _router(x, b, mod_all, gain, wr, layer):
    return pl.pallas_call(
        _router_kernel,
        grid=(S // RT_TM,),
        in_specs=[
            _x_spec(x, b, RT_TM),
            _mod_spec(layer, b), _gain_spec(layer),
            pl.BlockSpec((D, 2 * EP), lambda i: (0, 0)),
        ],
        out_specs=[
            pl.BlockSpec((RT_TM, D // 2), lambda i: (i, 0)),
            pl.BlockSpec((E, RT_TM), lambda i: (0, i)),
        ],
        out_shape=[
            jax.ShapeDtypeStruct((S, D // 2), jnp.int32),
            jax.ShapeDtypeStruct((E, S), F32),
        ],
        compiler_params=_cparams(("arbitrary",)),
        name="moe_router",
    )(x, mod_all, gain, wr)


def _select(aff, tri):
    return pl.pallas_call(
        _select_kernel,
        grid=(1,),
        in_specs=[
            pl.BlockSpec((E, S), lambda i: (0, 0)),
            pl.BlockSpec((128, 128), lambda i: (0, 0)),
        ],
        out_specs=pl.BlockSpec((E, S), lambda i: (0, 0)),
        out_shape=jax.ShapeDtypeStruct((E, S), jnp.int32),
        compiler_params=_cparams(("arbitrary",)),
        name="moe_select",
    )(aff, tri)


def _ffn(xin, gate, b, mod_all, w_gate, w_up, w_down, layer):
    return pl.pallas_call(
        _ffn_kernel,
        grid=(E, F // FFN_TF),
        in_specs=[
            pl.BlockSpec((1, CAP, D // 2), lambda e, f: (e, 0, 0)),
            pl.BlockSpec((1, 1, D, FFN_TF), lambda e, f: (layer, e, 0, f)),
            pl.BlockSpec((1, 1, D, FFN_TF), lambda e, f: (layer, e, 0, f)),
            pl.BlockSpec((1, 1, FFN_TF, D), lambda e, f: (layer, e, f, 0)),
            pl.BlockSpec((E, CAP), lambda e, f: (0, 0)),
            pl.BlockSpec((1, 1, 6, D), lambda e, f: (layer, b, 0, 0)),
        ],
        out_specs=pl.BlockSpec((1, CAP, D), lambda e, f: (e, 0, 0)),
        out_shape=jax.ShapeDtypeStruct((E, CAP, D), F32),
        scratch_shapes=[pltpu.VMEM((CAP, D), BF16)],
        compiler_params=_cparams(("arbitrary", "arbitrary"), 48),
        name="moe_ffn",
    )(xin, w_gate, w_up, w_down, gate, mod_all)


def kernel(x, c, w_ada, b_ada, norm_mix, norm_ffn, w_fourier_out, w_qkv, w_attn_out, q_gain, k_gain,
           sink, rel_bias, w_router, w_gate, w_up, w_down):
    m0, m2 = _dft_tables()
    m1 = _stage1_table()
    bucket_t = _bucket_table()
    tri = jnp.asarray(np.triu(np.ones((128, 128), np.float32)), BF16)
    mod_all = _ada(c, w_ada, b_ada)
    bias_tab = _bias_table(rel_bias, bucket_t)
    gain_mix = norm_mix.reshape(DEPTH, 1, D)
    gain_ffn = norm_ffn.reshape(DEPTH, 1, D)
    xs = [x, x]
    for layer in range(DEPTH):
        j = layer // 2
        if layer % 2 == 0:
            w_out_bf = w_fourier_out[j].astype(BF16)
            xs = [_fourier_layer(xs[b], b, mod_all, gain_mix, w_out_bf, layer, m0, m1, m2) for b in range(B)]
        else:
            aw = _attn_weights(w_qkv[j], q_gain[j], k_gain[j], sink[j])
            w_out_bf = w_attn_out[j].astype(BF16)
            xs = [_attn_layer(xs[b], b, mod_all, gain_mix, aw, w_out_bf, bias_tab, layer) for b in range(B)]
        wr = jnp.pad(w_router[layer], ((0, 0), (0, EP - E)))
        wr1 = wr.astype(BF16)
        wr = jnp.concatenate([wr1, (wr - wr1.astype(F32)).astype(BF16)], axis=1)
        routed = [_router(xs[b], b, mod_all, gain_ffn, wr, layer) for b in range(B)]
        pos = [_select(routed[b][1], tri) for b in range(B)]
        disp = [_dispatch(pos[b], routed[b][1], routed[b][0]) for b in range(B)]
        ys = [_ffn(disp[b][0], disp[b][2], b, mod_all, w_gate, w_up, w_down, layer) for b in range(B)]
        xs = [_combine(ys[b], disp[b][1], xs[b]) for b in range(B)]
    return jnp.stack(xs)
```

```python
import math

import numpy as np
import jax
import jax.numpy as jnp
from jax import lax
from jax.experimental import pallas as pl
from jax.experimental.pallas import tpu as pltpu
from jax.experimental.pallas import tpu_sc as plsc

D = 1024
B = 2
S = 8192
DEPTH = 4
GROUPS = 4
GD = D // GROUPS
HD = 64
NH = 16
NKV = 4
GQA = NH // NKV
WINDOW = 128
BLK = 128
NBUCKETS = 32
MAXDIST = 128
E = 16
CAP = 2 * S // E
F = 2 * D
EPS = 1e-6
NEG_INF = -1e30

N1 = 128
N2 = 64
INV_NORM = 1.0 / math.sqrt(S * GD)

F32 = jnp.float32
BF16 = jnp.bfloat16


def _cparams(sem, vmem_mb=48):
    return pltpu.CompilerParams(dimension_semantics=sem, vmem_limit_bytes=vmem_mb * 1024 * 1024)


def _dft_tables():
    c = np.arange(GD)
    ang0 = 2.0 * np.pi * ((c[:, None] * c[None, :]) % GD) / GD
    m0 = np.concatenate([np.cos(ang0), -np.sin(ang0)], axis=1)
    k2 = np.arange(N2)
    ang2 = 2.0 * np.pi * ((k2[:, None] * k2[None, :]) % N2) / N2
    m2 = np.stack([np.cos(ang2), np.sin(ang2)], axis=2).reshape(N2, 2 * N2)
    return jnp.asarray(m0, BF16), jnp.asarray(m2, BF16)


def _stage1_table():
    s2 = np.arange(N2)[:, None, None]
    k1 = np.arange(N1)[None, :, None]
    s1 = np.arange(N1)[None, None, :]
    th = ((k1 * (N2 * s1 + s2)) % S) * (2.0 * np.pi / S)
    co, si = np.cos(th), np.sin(th)
    top = np.concatenate([co, si], axis=2)
    bot = np.concatenate([-si, co], axis=2)
    return jnp.asarray(np.concatenate([top, bot], axis=1).astype(np.float32), BF16)


def _bucket_table():
    q_off = np.arange(BLK)
    k_off = np.arange(3 * BLK) - BLK
    rel = k_off[:, None] - q_off[None, :]
    half = NBUCKETS // 2
    max_exact = half // 2
    ret = np.where(rel > 0, half, 0)
    n = np.abs(rel)
    nf = np.maximum(n, 1).astype(np.float32)
    ratio = (np.log(nf / np.float32(max_exact)) / np.float32(math.log(MAXDIST / max_exact))).astype(np.float32)
    large = max_exact + (ratio * np.float32(half - max_exact)).astype(np.int32)
    large = np.minimum(large, half - 1)
    bucket = ret + np.where(n < max_exact, n, large)
    return jnp.asarray(np.where(np.abs(rel) <= WINDOW, bucket, -1).astype(np.int32))


def _modulate(x, gain, shift, scale):
    ms = jnp.mean(x * x, axis=-1, keepdims=True)
    return x * lax.rsqrt(ms + EPS) * (gain * (1.0 + scale)) + shift


def _sigmoid(x):
    return 1.0 / (1.0 + jnp.exp(-x))


ADA_TN = 768


def _ada_kernel(ct_ref, w_ref, b_ref, o_ref):
    ct = ct_ref[...]
    ca = ct * _sigmoid(ct)
    w = w_ref[0]
    for b in range(B):
        o_ref[0, b:b + 1, :] = jnp.sum(w * ca[:, b:b + 1], axis=0, keepdims=True) + b_ref[0]


def _ada(c, w_ada, b_ada):
    out = pl.pallas_call(
        _ada_kernel,
        grid=(DEPTH, 6 * D // ADA_TN),
        in_specs=[
            pl.BlockSpec((D, B), lambda l, j: (0, 0)),
            pl.BlockSpec((1, D, ADA_TN), lambda l, j: (l, 0, j)),
            pl.BlockSpec((1, 1, ADA_TN), lambda l, j: (l, 0, j)),
        ],
        out_specs=pl.BlockSpec((1, B, ADA_TN), lambda l, j: (l, 0, j)),
        out_shape=jax.ShapeDtypeStruct((DEPTH, B, 6 * D), F32),
        compiler_params=_cparams(("arbitrary", "arbitrary"), 32),
        name="ada_mod",
    )(c.T, w_ada, b_ada.reshape(DEPTH, 1, 6 * D))
    return out.reshape(DEPTH, B, 6, D)


def _mod_spec(layer, b):
    return pl.BlockSpec((1, 1, 6, D), lambda *_: (layer, b, 0, 0))


def _gain_spec(layer):
    return pl.BlockSpec((1, 1, D), lambda *_: (layer, 0, 0))


def _x_spec(x, b, tm):
    if x.ndim == 3:
        return pl.BlockSpec((None, tm, D), lambda i: (b, i, 0))
    return pl.BlockSpec((tm, D), lambda i: (i, 0))


F0_TM = 512
SUB = 8


def _pack_bf16_pairs(h):
    hb = h.astype(BF16).astype(F32)
    lo = pltpu.bitcast(hb[:, :D // 2], jnp.int32)
    hi = pltpu.bitcast(hb[:, D // 2:], jnp.int32)
    return hi | lax.shift_right_logical(lo, jnp.int32(16))


def _unpack_bf16_pairs(xp):
    lo = pltpu.bitcast(xp << 16, F32).astype(BF16)
    hi = pltpu.bitcast(xp & jnp.int32(-65536), F32).astype(BF16)
    return jnp.concatenate([lo, hi], axis=1)


def _f0_kernel(x_ref, mod_ref, gain_ref, m0_ref, o_ref):
    m = mod_ref[0, 0]
    h = _modulate(x_ref[...], gain_ref[0], m[0:1], m[1:2]).astype(BF16)
    m0 = m0_ref[...]
    r = [jnp.dot(h[:, g * GD:(g + 1) * GD], m0, preferred_element_type=F32) for g in range(GROUPS)]
    o_ref[0] = _pack_bf16_pairs(jnp.concatenate([rg[:, :GD] for rg in r], axis=1))
    o_ref[1] = _pack_bf16_pairs(jnp.concatenate([rg[:, GD:] for rg in r], axis=1))


def _f1_kernel(w_ref, m1_ref, o_ref, scr_ref):
    for j in range(SUB):
        scr_ref[j] = w_ref[:, j, :]
    for j in range(SUB):
        w = _unpack_bf16_pairs(scr_ref[j])
        o_ref[j] = _pack_bf16_pairs(jnp.dot(m1_ref[j], w, preferred_element_type=F32))


def _f2_kernel(u_ref, m2_ref, o_ref, scr_ref):
    m2 = m2_ref[...]
    for k in range(SUB):
        scr_ref[k] = u_ref[:, k, :]
    for k in range(SUB):
        u = _unpack_bf16_pairs(scr_ref[k])
        o_ref[k] = _pack_bf16_pairs(jnp.dot(m2, u, preferred_element_type=F32))


def _f3_kernel(mp_ref, x_ref, w_ref, mod_ref, o_ref, scr_ref):
    g1 = mod_ref[0, 0][2:3]
    for j in range(SUB):
        scr_ref[j * N1:(j + 1) * N1, :] = mp_ref[:, j, :]
    a = _unpack_bf16_pairs(scr_ref[...])
    y = jnp.dot(a, w_ref[...], preferred_element_type=F32)
    o_ref[...] = x_ref[...] + (g1 * INV_NORM) * y


def _fourier_layer(x, b, mod_all, gain, w_out_bf, layer, m0, m1, m2):
    wc = pl.pallas_call(
        _f0_kernel,
        grid=(S // F0_TM,),
        in_specs=[
            _x_spec(x, b, F0_TM),
            _mod_spec(layer, b), _gain_spec(layer),
            pl.BlockSpec((GD, 2 * GD), lambda i: (0, 0)),
        ],
        out_specs=pl.BlockSpec((2, F0_TM, D // 2), lambda i: (0, i, 0)),
        out_shape=jax.ShapeDtypeStruct((2, S, D // 2), jnp.int32),
        compiler_params=_cparams(("arbitrary",)),
        name="fourier_chan",
    )(x, mod_all, gain, m0)
    wc = wc.reshape(2 * N1, N2, D // 2)
    u = pl.pallas_call(
        _f1_kernel,
        grid=(N2 // SUB,),
        in_specs=[
            pl.BlockSpec((2 * N1, SUB, D // 2), lambda i: (0, i, 0)),
            pl.BlockSpec((SUB, 2 * N1, 2 * N1), lambda i: (i, 0, 0)),
        ],
        out_specs=pl.BlockSpec((SUB, 2 * N1, D // 2), lambda i: (i, 0, 0)),
        out_shape=jax.ShapeDtypeStruct((N2, 2 * N1, D // 2), jnp.int32),
        scratch_shapes=[pltpu.VMEM((SUB, 2 * N1, D // 2), jnp.int32)],
        compiler_params=_cparams(("arbitrary",)),
        name="fourier_seq1",
    )(wc, m1)
    u = u.reshape(2 * N2, N1, D // 2)
    mp = pl.pallas_call(
        _f2_kernel,
        grid=(N1 // SUB,),
        in_specs=[
            pl.BlockSpec((2 * N2, SUB, D // 2), lambda i: (0, i, 0)),
            pl.BlockSpec((N2, 2 * N2), lambda i: (0, 0)),
        ],
        out_specs=pl.BlockSpec((SUB, N2, D // 2), lambda i: (i, 0, 0)),
        out_shape=jax.ShapeDtypeStruct((N1, N2, D // 2), jnp.int32),
        scratch_shapes=[pltpu.VMEM((SUB, 2 * N2, D // 2), jnp.int32)],
        compiler_params=_cparams(("arbitrary",)),
        name="fourier_seq2",
    )(u, m2)
    return pl.pallas_call(
        _f3_kernel,
        grid=(N2 // SUB,),
        in_specs=[
            pl.BlockSpec((N1, SUB, D // 2), lambda i: (0, i, 0)),
            _x_spec(x, b, SUB * N1),
            pl.BlockSpec((D, D), lambda i: (0, 0)),
            _mod_spec(layer, b),
        ],
        out_specs=pl.BlockSpec((SUB * N1, D), lambda i: (i, 0)),
        out_shape=jax.ShapeDtypeStruct((S, D), F32),
        scratch_shapes=[pltpu.VMEM((SUB * N1, D // 2), jnp.int32)],
        compiler_params=_cparams(("arbitrary",)),
        name="fourier_out",
    )(mp, x, w_out_bf, mod_all)


QKV_TM = 512
TQ = 256
NSB = TQ // BLK
KWIN = TQ + 2 * BLK
NBLK = S // BLK
KPAD = 128
VROWS = HD + 16
ATT_AHEAD = 2


def _bias_kernel(rb_ref, bucket_ref, o_ref):
    h = pl.program_id(0)
    bucket = bucket_ref[...]
    acc = jnp.full(bucket.shape, NEG_INF, F32)
    for k in range(NBUCKETS):
        acc = jnp.where(bucket == k, rb_ref[k, h], acc)
    o_ref[0] = acc


def _bias_table(rel_bias, bucket_t):
    return pl.pallas_call(
        _bias_kernel,
        grid=(NH,),
        in_specs=[
            pl.BlockSpec(memory_space=pltpu.SMEM),
            pl.BlockSpec((3 * BLK, BLK), lambda h: (0, 0)),
        ],
        out_specs=pl.BlockSpec((1, 3 * BLK, BLK), lambda h: (h // GQA, 0, h % GQA)),
        out_shape=jax.ShapeDtypeStruct((NKV, 3 * BLK, GQA * BLK), F32),
        compiler_params=_cparams(("arbitrary",)),
        name="rel_bias_table",
    )(rel_bias, bucket_t)


def _qkv_kernel(x_ref, mod_ref, gain_ref, wqt_ref, wk_ref, wvt_ref, qg_ref, kg_ref, qt_ref, k_ref, vt_ref):
    m = mod_ref[0, 0]
    h = _modulate(x_ref[...], gain_ref[0], m[0:1], m[1:2]).astype(BF16)
    nt = (((1,), (1,)), ((), ()))
    qt = lax.dot_general(wqt_ref[...], h, nt, preferred_element_type=F32)
    tm = qt.shape[1]
    q3 = qt.reshape(NH, HD, tm)
    q3 = q3 * lax.rsqrt(jnp.mean(q3 * q3, axis=1, keepdims=True) + EPS)
    qt_ref[...] = (q3.reshape(NH * HD, tm) * qg_ref[...]).astype(BF16)
    k = jnp.dot(h, wk_ref[...], preferred_element_type=F32)
    for g in range(NKV):
        kg = k[:, g * KPAD:(g + 1) * KPAD]
        ms = jnp.sum(kg * kg, axis=-1, keepdims=True) * (1.0 / HD)
        k_ref[:, g * KPAD:(g + 1) * KPAD] = (kg * lax.rsqrt(ms + EPS) * kg_ref[...]).astype(BF16)
    vt_ref[...] = lax.dot_general(wvt_ref[...], h, nt, preferred_element_type=F32).astype(BF16)


def _attn_kernel(qt_ref, kp_ref, kc_ref, kn_ref, vp_ref, vc_ref, vn_ref, x_ref, w_ref, mod_ref,
                 bias_ref, sink_ref, o_ref, att_ref, s_ref):
    i = pl.program_id(0)
    kwin = jnp.concatenate([kp_ref[...], kc_ref[...], kn_ref[...]], axis=0)
    vwin = jnp.concatenate([vp_ref[...], vc_ref[...], vn_ref[...]], axis=1)
    ones_rows = (lax.broadcasted_iota(jnp.int32, (VROWS - HD, KWIN), 0) == 0).astype(BF16)
    vaug = [jnp.concatenate([vwin[g * HD:(g + 1) * HD], ones_rows], axis=0) for g in range(NKV)]
    key_pos = i * TQ - BLK + lax.broadcasted_iota(jnp.int32, (KWIN, 1), 0)
    key_mask = jnp.where((key_pos >= 0) & (key_pos < S), 0.0, NEG_INF).astype(BF16)
    lane = lax.broadcasted_iota(jnp.int32, (1, NKV * KPAD), 1)
    kwin = jnp.where(lane % KPAD == HD, key_mask, kwin)
    q_ones = (lax.broadcasted_iota(jnp.int32, (KPAD - HD, GQA * BLK), 0) == 0).astype(BF16)
    items = [(g, r) for g in range(NKV) for r in range(NSB)]

    def scores(g, r):
        kr = kwin[r * BLK:r * BLK + 3 * BLK, g * KPAD:(g + 1) * KPAD]
        qg = jnp.concatenate(
            [qt_ref[(GQA * g + hh) * HD:(GQA * g + hh + 1) * HD, r * BLK:(r + 1) * BLK] for hh in range(GQA)],
            axis=1)
        qa = jnp.concatenate([qg, q_ones], axis=0)
        return jnp.dot(kr, qa, preferred_element_type=F32)

    def probs(s, g, r):
        sink = sink_ref[g]
        s = s + bias_ref[g]
        mx = jnp.maximum(jnp.max(s, axis=0, keepdims=True), sink)
        return jnp.exp(s - mx).astype(BF16), jnp.exp(sink - mx)

    nslot = ATT_AHEAD + 1
    for n in range(ATT_AHEAD):
        s_ref[n % nslot] = scores(*items[n])
    for n, (g, r) in enumerate(items):
        if n + ATT_AHEAD < len(items):
            s_ref[(n + ATT_AHEAD) % nslot] = scores(*items[n + ATT_AHEAD])
        p, psink = probs(s_ref[n % nslot], g, r)
        ot = jnp.dot(vaug[g][:, r * BLK:r * BLK + 3 * BLK], p, preferred_element_type=F32)
        ot = ot[:HD] * (1.0 / (ot[HD:HD + 1] + psink))
        for hh in range(GQA):
            hd0 = (GQA * g + hh) * HD
            att_ref[hd0:hd0 + HD, r * BLK:(r + 1) * BLK] = ot[:, hh * BLK:(hh + 1) * BLK]
    g1 = mod_ref[0, 0][2:3]
    att = att_ref[...].T.astype(BF16)
    y = jnp.dot(att, w_ref[...], preferred_element_type=F32)
    o_ref[...] = x_ref[...] + g1 * y


def _attn_weights(w_qkv, q_gain, k_gain, sink):
    wq, wk, wv = w_qkv[:, :NH * HD], w_qkv[:, NH * HD:(NH + NKV) * HD], w_qkv[:, (NH + NKV) * HD:]
    wqt = wq.T.astype(BF16)
    wvt = wv.T.astype(BF16)
    wk_pad = jnp.pad(wk.reshape(D, NKV, HD), ((0, 0), (0, 0), (0, KPAD - HD))).reshape(D, NKV * KPAD).astype(BF16)
    qg_col = jnp.tile(q_gain * (HD ** -0.5), NH).reshape(NH * HD, 1)
    kg_row = jnp.pad(k_gain, (0, KPAD - HD)).reshape(1, KPAD)
    sink_row = jnp.repeat(sink, BLK).reshape(NKV, 1, GQA * BLK)
    return wqt, wk_pad, wvt, qg_col, kg_row, sink_row


def _attn_layer(x, b, mod_all, gain, aw, w_out_bf, bias_tab, layer):
    wqt, wk_pad, wvt, qg_col, kg_row, sink_row = aw
    qt, k, vt = pl.pallas_call(
        _qkv_kernel,
        grid=(S // QKV_TM,),
        in_specs=[
            _x_spec(x, b, QKV_TM),
            _mod_spec(layer, b), _gain_spec(layer),
            pl.BlockSpec((NH * HD, D), lambda i: (0, 0)),
            pl.BlockSpec((D, NKV * KPAD), lambda i: (0, 0)),
            pl.BlockSpec((NKV * HD, D), lambda i: (0, 0)),
            pl.BlockSpec((NH * HD, 1), lambda i: (0, 0)),
            pl.BlockSpec((1, KPAD), lambda i: (0, 0)),
        ],
        out_specs=[
            pl.BlockSpec((NH * HD, QKV_TM), lambda i: (0, i)),
            pl.BlockSpec((QKV_TM, NKV * KPAD), lambda i: (i, 0)),
            pl.BlockSpec((NKV * HD, QKV_TM), lambda i: (0, i)),
        ],
        out_shape=[
            jax.ShapeDtypeStruct((NH * HD, S), BF16),
            jax.ShapeDtypeStruct((S, NKV * KPAD), BF16),
            jax.ShapeDtypeStruct((NKV * HD, S), BF16),
        ],
        compiler_params=_cparams(("arbitrary",)),
        name="attn_qkv",
    )(x, mod_all, gain, wqt, wk_pad, wvt, qg_col, kg_row)

    kw = NKV * KPAD
    vw = NKV * HD
    kprev = pl.BlockSpec((BLK, kw), lambda i: (jnp.maximum(i * NSB - 1, 0), 0))
    kcur = pl.BlockSpec((TQ, kw), lambda i: (i, 0))
    knext = pl.BlockSpec((BLK, kw), lambda i: (jnp.minimum((i + 1) * NSB, NBLK - 1), 0))
    vprev = pl.BlockSpec((vw, BLK), lambda i: (0, jnp.maximum(i * NSB - 1, 0)))
    vcur = pl.BlockSpec((vw, TQ), lambda i: (0, i))
    vnext = pl.BlockSpec((vw, BLK), lambda i: (0, jnp.minimum((i + 1) * NSB, NBLK - 1)))
    return pl.pallas_call(
        _attn_kernel,
        grid=(S // TQ,),
        in_specs=[
            pl.BlockSpec((NH * HD, TQ), lambda i: (0, i)),
            kprev, kcur, knext, vprev, vcur, vnext,
            _x_spec(x, b, TQ),
            pl.BlockSpec((NH * HD, D), lambda i: (0, 0)),
            _mod_spec(layer, b),
            pl.BlockSpec((NKV, 3 * BLK, GQA * BLK), lambda i: (0, 0, 0)),
            pl.BlockSpec((NKV, 1, GQA * BLK), lambda i: (0, 0, 0)),
        ],
        out_specs=pl.BlockSpec((TQ, D), lambda i: (i, 0)),
        out_shape=jax.ShapeDtypeStruct((S, D), F32),
        scratch_shapes=[pltpu.VMEM((NH * HD, TQ), F32), pltpu.VMEM((ATT_AHEAD + 1, 3 * BLK, GQA * BLK), F32)],
        compiler_params=_cparams(("arbitrary",)),
        name="attn_core",
    )(qt, k, k, k, vt, vt, vt, x, w_out_bf, mod_all, bias_tab, sink_row)


RT_TM = 512
EP = 128
FFN_TF = 1024
FFN_RT = 512
FFN_WT = 256


def _router_kernel(x_ref, mod_ref, gain_ref, wr_ref, h_ref, aff_ref):
    m = mod_ref[0, 0]
    h = _modulate(x_ref[...], gain_ref[0], m[3:4], m[4:5])
    h_ref[...] = _pack_bf16_pairs(h)
    h1 = h.astype(BF16)
    h2 = (h - h1.astype(F32)).astype(BF16)
    wr = wr_ref[...]
    part = jnp.dot(h1, wr, preferred_element_type=F32)
    logits = part[:, :EP] + part[:, EP:] + jnp.dot(h2, wr[:, :EP], preferred_element_type=F32)
    lt = logits.T[:E]
    mx = jnp.max(lt, axis=0, keepdims=True)
    ex = jnp.exp(lt - mx)
    aff_ref[...] = ex / jnp.sum(ex, axis=0, keepdims=True)


def _select_kernel(aff_ref, tri_ref, pos_ref):
    aff = aff_ref[...]
    bits = pltpu.bitcast(aff, jnp.int32)
    rows = aff.shape[0]

    def count_ge(v):
        return jnp.sum((bits >= v).astype(jnp.int32), axis=1, keepdims=True)

    def body(t, cur):
        cand = cur | (jnp.int32(1) << (30 - t))
        return jnp.where(count_ge(cand) >= CAP, cand, cur)

    thr = lax.fori_loop(0, 31, body, jnp.zeros((rows, 1), jnp.int32))
    gt = bits > thr
    eq = bits == thr
    need = CAP - jnp.sum(gt.astype(jnp.int32), axis=1, keepdims=True)
    tri = tri_ref[...]

    def cumsum_excl(mask_f):
        off = jnp.zeros((rows, 1), F32)
        outs = []
        for j in range(S // 128):
            mj = mask_f[:, j * 128:(j + 1) * 128]
            loc = jnp.dot(mj.astype(BF16), tri, preferred_element_type=F32)
            outs.append(loc - mj + off)
            off = off + loc[:, 127:128]
        return jnp.concatenate(outs, axis=1)

    eq_rank = cumsum_excl(eq.astype(F32))
    sel = gt | (eq & (eq_rank < need.astype(F32)))
    pos = cumsum_excl(sel.astype(F32))
    pos_ref[...] = jnp.where(sel, pos.astype(jnp.int32), -1)


SC_CORES = 2
SC_SUBCORES = 16
SC_TILES = SC_CORES * SC_SUBCORES
DISPATCH_SLOTS = CAP * E // SC_TILES
DISPATCH_ROWS = 64
COMBINE_RANGE = 1024
COMBINE_ROWS = 32
SLAB = 128
NSLAB = D // SLAB


def _sc_mesh():
    return plsc.VectorSubcoreMesh(core_axis_name="c", subcore_axis_name="s",
                                  num_cores=SC_CORES, num_subcores=SC_SUBCORES)


def _dispatch_body(pos_hbm, aff_hbm, h_hbm, xin_hbm, idx_hbm, gate_hbm, pos_v, aff_v, idx_v, gate_v,
                   rows_a, rows_b, gsem_a, gsem_b, wsem_a, wsem_b):
    w = lax.axis_index("s") * SC_CORES + lax.axis_index("c")
    e = w // 2
    lo = (w % 2) * DISPATCH_SLOTS
    pltpu.sync_copy(pos_hbm.at[e], pos_v)
    pltpu.sync_copy(aff_hbm.at[e], aff_v)

    @pl.loop(0, S // 16)
    def _(i):
        p = pos_v[pl.ds(i * 16, 16)] - lo
        m = (p >= 0) & (p < DISPATCH_SLOTS)
        tok = lax.iota(jnp.int32, 16) + i * 16
        plsc.store_scatter(idx_v, [p], tok, mask=m)
        plsc.store_scatter(gate_v, [p], aff_v[pl.ds(i * 16, 16)], mask=m)

    pltpu.sync_copy(idx_v, idx_hbm.at[e, pl.ds(lo, DISPATCH_SLOTS)])
    pltpu.sync_copy(gate_v, gate_hbm.at[e, pl.ds(lo, DISPATCH_SLOTS)])

    bufs, gsems, wsems = (rows_a, rows_b), (gsem_a, gsem_b), (wsem_a, wsem_b)
    nchunk = DISPATCH_SLOTS // DISPATCH_ROWS

    def gather(j):
        return pltpu.async_copy(h_hbm.at[idx_v.at[pl.ds(j * DISPATCH_ROWS, DISPATCH_ROWS)]], bufs[j % 2], gsems[j % 2])

    pending_gather = gather(0)
    writes = [None, None]
    for j in range(nchunk):
        pending_gather.wait()
        writes[j % 2] = pltpu.async_copy(
            bufs[j % 2], xin_hbm.at[e, pl.ds(lo + j * DISPATCH_ROWS, DISPATCH_ROWS)], wsems[j % 2])
        if j + 1 < nchunk:
            if writes[(j + 1) % 2] is not None:
                writes[(j + 1) % 2].wait()
            pending_gather = gather(j + 1)
    writes[(nchunk - 2) % 2].wait()
    writes[(nchunk - 1) % 2].wait()


def _dispatch(pos, aff, h):
    return pl.kernel(
        _dispatch_body, mesh=_sc_mesh(),
        out_type=[jax.ShapeDtypeStruct((E, CAP, D // 2), jnp.int32),
                  jax.ShapeDtypeStruct((E, CAP), jnp.int32),
                  jax.ShapeDtypeStruct((E, CAP), F32)],
        scratch_types=[pltpu.VMEM((S,), jnp.int32), pltpu.VMEM((S,), F32),
                       pltpu.VMEM((DISPATCH_SLOTS,), jnp.int32), pltpu.VMEM((DISPATCH_SLOTS,), F32),
                       pltpu.VMEM((DISPATCH_ROWS, D // 2), jnp.int32), pltpu.VMEM((DISPATCH_ROWS, D // 2), jnp.int32),
                       pltpu.SemaphoreType.DMA, pltpu.SemaphoreType.DMA,
                       pltpu.SemaphoreType.DMA, pltpu.SemaphoreType.DMA],
        compiler_params=pltpu.CompilerParams(needs_layout_passes=False),
        name="moe_dispatch",
    )(pos, aff, h)


def _combine_body(y_hbm, idx_hbm, x_hbm, out_hbm, idx_v, li_v, *bufs):
    rows = bufs[:NSLAB]
    accs = bufs[NSLAB:]
    c = lax.axis_index("c")
    s = lax.axis_index("s")
    share = COMBINE_RANGE // SC_SUBCORES
    pltpu.sync_copy(idx_hbm.at[s], idx_v)
    lane = lax.iota(jnp.int32, 16)

    @pl.loop(0, S // COMBINE_RANGE // SC_CORES)
    def _(r):
        t0 = (r * SC_CORES + c) * COMBINE_RANGE
        row0 = t0 + s * share
        pltpu.sync_copy(tuple(x_hbm.at[pl.ds(row0, share), pl.ds(q * SLAB, SLAB)] for q in range(NSLAB)),
                        tuple(accs[q].at[pl.ds(s * share, share)] for q in range(NSLAB)))
        plsc.subcore_barrier()

        @pl.loop(0, CAP // COMBINE_ROWS)
        def _(j):
            hits = jnp.zeros((16,), jnp.int32)
            for v in range(COMBINE_ROWS // 16):
                t = idx_v[pl.ds(j * COMBINE_ROWS + v * 16, 16)] - t0
                ok = (t >= 0) & (t < COMBINE_RANGE)
                li_v[pl.ds(v * 16, 16)] = jnp.where(ok, t, COMBINE_RANGE + lane)
                hits = hits + plsc.all_reduce_population_count(ok)

            @pl.when(jnp.max(hits) > 0)
            def _():
                pltpu.sync_copy(
                    tuple(y_hbm.at[s, pl.ds(j * COMBINE_ROWS, COMBINE_ROWS), pl.ds(q * SLAB, SLAB)]
                          for q in range(NSLAB)),
                    tuple(rows))
                pltpu.sync_copy(tuple(rows), tuple(accs[q].at[li_v] for q in range(NSLAB)), add=True)

        plsc.subcore_barrier()
        pltpu.sync_copy(tuple(accs[q].at[pl.ds(s * share, share)] for q in range(NSLAB)),
                        tuple(out_hbm.at[pl.ds(row0, share), pl.ds(q * SLAB, SLAB)] for q in range(NSLAB)))


def _combine(y, idx, x):
    return pl.kernel(
        _combine_body, mesh=_sc_mesh(),
        out_type=jax.ShapeDtypeStruct((S, D), F32),
        scratch_types=[pltpu.VMEM((CAP,), jnp.int32), pltpu.VMEM((COMBINE_ROWS,), jnp.int32)]
        + [pltpu.VMEM((COMBINE_ROWS, SLAB), F32) for _ in range(NSLAB)]
        + [pltpu.VMEM_SHARED((COMBINE_RANGE + 16, SLAB), F32) for _ in range(NSLAB)],
        compiler_params=pltpu.CompilerParams(needs_layout_passes=False),
        name="moe_combine",
    )(y, idx, x)


def _ffn_kernel(x_ref, wg_ref, wu_ref, wd_ref, gate_ref, mod_ref, o_ref, xb_ref):
    f = pl.program_id(1)
    last = pl.num_programs(1) - 1

    @pl.when(f == 0)
    def _():
        xp = x_ref[0]
        xb_ref[:, :D // 2] = pltpu.bitcast(xp << 16, F32).astype(BF16)
        xb_ref[:, D // 2:] = pltpu.bitcast(xp & jnp.int32(-65536), F32).astype(BF16)

    nt = FFN_TF // FFN_WT
    wg = [wg_ref[0, 0, :, j * FFN_WT:(j + 1) * FFN_WT].astype(BF16) for j in range(nt)]
    wu = [wu_ref[0, 0, :, j * FFN_WT:(j + 1) * FFN_WT].astype(BF16) for j in range(nt)]
    wd = wd_ref[0, 0].astype(BF16)
    is_first = f == 0
    is_last = f == last
    g2 = jnp.where(is_last, mod_ref[0, 0][5:6], 1.0)
    g_row = gate_ref[pl.ds(pl.program_id(0), 1), :]
    g_col = jnp.broadcast_to(g_row, (128, CAP)).T[:, 0:1]
    for r in range(CAP // FFN_RT):
        rows = slice(r * FFN_RT, (r + 1) * FFN_RT)
        xr = xb_ref[rows, :]
        acts = []
        for j in range(nt):
            g = jnp.dot(xr, wg[j], preferred_element_type=F32)
            u = jnp.dot(xr, wu[j], preferred_element_type=F32)
            acts.append((g * _sigmoid(g) * u).astype(BF16))
        y = jnp.dot(jnp.concatenate(acts, axis=1), wd, preferred_element_type=F32)
        prev = jnp.where(is_first, 0.0, o_ref[0, rows, :])
        gate = jnp.where(is_last, g_col[rows, :], 1.0)
        o_ref[0, rows, :] = (prev + y) * gate * g2


def _router(x, b, mod_all, gain, wr, layer):
    return pl.pallas_call(
        _router_kernel,
        grid=(S // RT_TM,),
        in_specs=[
            _x_spec(x, b, RT_TM),
            _mod_spec(layer, b), _gain_spec(layer),
            pl.BlockSpec((D, 2 * EP), lambda i: (0, 0)),
        ],
        out_specs=[
            pl.BlockSpec((RT_TM, D // 2), lambda i: (i, 0)),
            pl.BlockSpec((E, RT_TM), lambda i: (0, i)),
        ],
        out_shape=[
            jax.ShapeDtypeStruct((S, D // 2), jnp.int32),
            jax.ShapeDtypeStruct((E, S), F32),
        ],
        compiler_params=_cparams(("arbitrary",)),
        name="moe_router",
    )(x, mod_all, gain, wr)


def _select(aff, tri):
    return pl.pallas_call(
        _select_kernel,
        grid=(1,),
        in_specs=[
            pl.BlockSpec((E, S), lambda i: (0, 0)),
            pl.BlockSpec((128, 128), lambda i: (0, 0)),
        ],
        out_specs=pl.BlockSpec((E, S), lambda i: (0, 0)),
        out_shape=jax.ShapeDtypeStruct((E, S), jnp.int32),
        compiler_params=_cparams(("arbitrary",)),
        name="moe_select",
    )(aff, tri)


def _ffn(xin, gate, b, mod_all, w_gate, w_up, w_down, layer):
    return pl.pallas_call(
        _ffn_kernel,
        grid=(E, F // FFN_TF),
        in_specs=[
            pl.BlockSpec((1, CAP, D // 2), lambda e, f: (e, 0, 0)),
            pl.BlockSpec((1, 1, D, FFN_TF), lambda e, f: (layer, e, 0, f)),
            pl.BlockSpec((1, 1, D, FFN_TF), lambda e, f: (layer, e, 0, f)),
            pl.BlockSpec((1, 1, FFN_TF, D), lambda e, f: (layer, e, f, 0)),
            pl.BlockSpec((E, CAP), lambda e, f: (0, 0)),
            pl.BlockSpec((1, 1, 6, D), lambda e, f: (layer, b, 0, 0)),
        ],
        out_specs=pl.BlockSpec((1, CAP, D), lambda e, f: (e, 0, 0)),
        out_shape=jax.ShapeDtypeStruct((E, CAP, D), F32),
        scratch_shapes=[pltpu.VMEM((CAP, D), BF16)],
        compiler_params=_cparams(("arbitrary", "arbitrary"), 48),
        name="moe_ffn",
    )(xin, w_gate, w_up, w_down, gate, mod_all)


def kernel(x, c, w_ada, b_ada, norm_mix, norm_ffn, w_fourier_out, w_qkv, w_attn_out, q_gain, k_gain,
           sink, rel_bias, w_router, w_gate, w_up, w_down):
    m0, m2 = _dft_tables()
    m1 = _stage1_table()
    bucket_t = _bucket_table()
    tri = jnp.asarray(np.triu(np.ones((128, 128), np.float32)), BF16)
    mod_all = _ada(c, w_ada, b_ada)
    bias_tab = _bias_table(rel_bias, bucket_t)
    gain_mix = norm_mix.reshape(DEPTH, 1, D)
    gain_ffn = norm_ffn.reshape(DEPTH, 1, D)
    xs = [x, x]
    for layer in range(DEPTH):
        j = layer // 2
        if layer % 2 == 0:
            w_out_bf = w_fourier_out[j].astype(BF16)
            xs = [_fourier_layer(xs[b], b, mod_all, gain_mix, w_out_bf, layer, m0, m1, m2) for b in range(B)]
        else:
            aw = _attn_weights(w_qkv[j], q_gain[j], k_gain[j], sink[j])
            w_out_bf = w_attn_out[j].astype(BF16)
            xs = [_attn_layer(xs[b], b, mod_all, gain_mix, aw, w_out_bf, bias_tab, layer) for b in range(B)]
        wr = jnp.pad(w_router[layer], ((0, 0), (0, EP - E)))
        wr1 = wr.astype(BF16)
        wr = jnp.concatenate([wr1, (wr - wr1.astype(F32)).astype(BF16)], axis=1)
        routed = [_router(xs[b], b, mod_all, gain_ffn, wr, layer) for b in range(B)]
        pos = [_select(routed[b][1], tri) for b in range(B)]
        disp = [_dispatch(pos[b], routed[b][1], routed[b][0]) for b in range(B)]
        ys = [_ffn(disp[b][0], disp[b][2], b, mod_all, w_gate, w_up, w_down, layer) for b in range(B)]
        xs = [_combine(ys[b], disp[b][1], xs[b]) for b in range(B)]
    return jnp.stack(xs)
```

```python
import math

import numpy as np
import jax
import jax.numpy as jnp
from jax import lax
from jax.experimental import pallas as pl
from jax.experimental.pallas import tpu as pltpu
from jax.experimental.pallas import tpu_sc as plsc

D = 1024
B = 2
S = 8192
DEPTH = 4
GROUPS = 4
GD = D // GROUPS
HD = 64
NH = 16
NKV = 4
GQA = NH // NKV
WINDOW = 128
BLK = 128
NBUCKETS = 32
MAXDIST = 128
E = 16
CAP = 2 * S // E
F = 2 * D
EPS = 1e-6
NEG_INF = -1e30

N1 = 128
N2 = 64
INV_NORM = 1.0 / math.sqrt(S * GD)

F32 = jnp.float32
BF16 = jnp.bfloat16


def _cparams(sem, vmem_mb=48):
    return pltpu.CompilerParams(dimension_semantics=sem, vmem_limit_bytes=vmem_mb * 1024 * 1024)


def _dft_tables():
    c = np.arange(GD)
    ang0 = 2.0 * np.pi * ((c[:, None] * c[None, :]) % GD) / GD
    m0 = np.concatenate([np.cos(ang0), -np.sin(ang0)], axis=1)
    k2 = np.arange(N2)
    ang2 = 2.0 * np.pi * ((k2[:, None] * k2[None, :]) % N2) / N2
    m2 = np.stack([np.cos(ang2), np.sin(ang2)], axis=2).reshape(N2, 2 * N2)
    return jnp.asarray(m0, BF16), jnp.asarray(m2, BF16)


def _stage1_table():
    s2 = np.arange(N2)[:, None, None]
    k1 = np.arange(N1)[None, :, None]
    s1 = np.arange(N1)[None, None, :]
    th = ((k1 * (N2 * s1 + s2)) % S) * (2.0 * np.pi / S)
    co, si = np.cos(th), np.sin(th)
    top = np.concatenate([co, si], axis=2)
    bot = np.concatenate([-si, co], axis=2)
    return jnp.asarray(np.concatenate([top, bot], axis=1).astype(np.float32), BF16)


def _bucket_table():
    q_off = np.arange(BLK)
    k_off = np.arange(3 * BLK) - BLK
    rel = k_off[:, None] - q_off[None, :]
    half = NBUCKETS // 2
    max_exact = half // 2
    ret = np.where(rel > 0, half, 0)
    n = np.abs(rel)
    nf = np.maximum(n, 1).astype(np.float32)
    ratio = (np.log(nf / np.float32(max_exact)) / np.float32(math.log(MAXDIST / max_exact))).astype(np.float32)
    large = max_exact + (ratio * np.float32(half - max_exact)).astype(np.int32)
    large = np.minimum(large, half - 1)
    bucket = ret + np.where(n < max_exact, n, large)
    return jnp.asarray(np.where(np.abs(rel) <= WINDOW, bucket, -1).astype(np.int32))


def _modulate(x, gain, shift, scale):
    ms = jnp.mean(x * x, axis=-1, keepdims=True)
    return x * lax.rsqrt(ms + EPS) * (gain * (1.0 + scale)) + shift


def _sigmoid(x):
    return 1.0 / (1.0 + jnp.exp(-x))


ADA_TN = 768


def _ada_kernel(ct_ref, w_ref, b_ref, o_ref):
    ct = ct_ref[...]
    ca = ct * _sigmoid(ct)
    w = w_ref[0]
    for b in range(B):
        o_ref[0, b:b + 1, :] = jnp.sum(w * ca[:, b:b + 1], axis=0, keepdims=True) + b_ref[0]


def _ada(c, w_ada, b_ada):
    out = pl.pallas_call(
        _ada_kernel,
        grid=(DEPTH, 6 * D // ADA_TN),
        in_specs=[
            pl.BlockSpec((D, B), lambda l, j: (0, 0)),
            pl.BlockSpec((1, D, ADA_TN), lambda l, j: (l, 0, j)),
            pl.BlockSpec((1, 1, ADA_TN), lambda l, j: (l, 0, j)),
        ],
        out_specs=pl.BlockSpec((1, B, ADA_TN), lambda l, j: (l, 0, j)),
        out_shape=jax.ShapeDtypeStruct((DEPTH, B, 6 * D), F32),
        compiler_params=_cparams(("arbitrary", "arbitrary"), 32),
        name="ada_mod",
    )(c.T, w_ada, b_ada.reshape(DEPTH, 1, 6 * D))
    return out.reshape(DEPTH, B, 6, D)


def _mod_spec(layer, b):
    return pl.BlockSpec((1, 1, 6, D), lambda *_: (layer, b, 0, 0))


def _gain_spec(layer):
    return pl.BlockSpec((1, 1, D), lambda *_: (layer, 0, 0))


def _x_spec(x, b, tm):
    if x.ndim == 3:
        return pl.BlockSpec((None, tm, D), lambda i: (b, i, 0))
    return pl.BlockSpec((tm, D), lambda i: (i, 0))


EP = 128

_ROUTED_SHAPES = [jax.ShapeDtypeStruct((S, D), F32),
                  jax.ShapeDtypeStruct((S, D // 2), jnp.int32),
                  jax.ShapeDtypeStruct((E, S), F32)]


def _routed_specs(tm):
    return [pl.BlockSpec((tm, D), lambda i: (i, 0)),
            pl.BlockSpec((tm, D // 2), lambda i: (i, 0)),
            pl.BlockSpec((E, tm), lambda i: (0, i))]


F0_TM = 512
SUB = 8


def _pack_bf16_pairs(h):
    hb = h.astype(BF16).astype(F32)
    lo = pltpu.bitcast(hb[:, :D // 2], jnp.int32)
    hi = pltpu.bitcast(hb[:, D // 2:], jnp.int32)
    return hi | lax.shift_right_logical(lo, jnp.int32(16))


def _unpack_bf16_pairs(xp):
    lo = pltpu.bitcast(xp << 16, F32).astype(BF16)
    hi = pltpu.bitcast(xp & jnp.int32(-65536), F32).astype(BF16)
    return jnp.concatenate([lo, hi], axis=1)


def _route(xn, m, gain_ffn, wr):
    h = _modulate(xn, gain_ffn, m[3:4], m[4:5])
    h1 = h.astype(BF16)
    h2 = (h - h1.astype(F32)).astype(BF16)
    part = jnp.dot(h1, wr, preferred_element_type=F32)
    logits = part[:, :EP] + part[:, EP:] + jnp.dot(h2, wr[:, :EP], preferred_element_type=F32)
    lt = logits.T[:E]
    ex = jnp.exp(lt - jnp.max(lt, axis=0, keepdims=True))
    return _pack_bf16_pairs(h), ex / jnp.sum(ex, axis=0, keepdims=True)


def _f0_kernel(x_ref, mod_ref, gain_ref, m0_ref, o_ref):
    m = mod_ref[0, 0]
    h = _modulate(x_ref[...], gain_ref[0], m[0:1], m[1:2]).astype(BF16)
    m0 = m0_ref[...]
    r = [jnp.dot(h[:, g * GD:(g + 1) * GD], m0, preferred_element_type=F32) for g in range(GROUPS)]
    o_ref[0] = _pack_bf16_pairs(jnp.concatenate([rg[:, :GD] for rg in r], axis=1))
    o_ref[1] = _pack_bf16_pairs(jnp.concatenate([rg[:, GD:] for rg in r], axis=1))


def _f1_kernel(w_ref, m1_ref, o_ref, scr_ref):
    for j in range(SUB):
        scr_ref[j] = w_ref[:, j, :]
    for j in range(SUB):
        w = _unpack_bf16_pairs(scr_ref[j])
        o_ref[j] = _pack_bf16_pairs(jnp.dot(m1_ref[j], w, preferred_element_type=F32))


def _f2_kernel(u_ref, m2_ref, o_ref, scr_ref):
    m2 = m2_ref[...]
    for k in range(SUB):
        scr_ref[k] = u_ref[:, k, :]
    for k in range(SUB):
        u = _unpack_bf16_pairs(scr_ref[k])
        o_ref[k] = _pack_bf16_pairs(jnp.dot(m2, u, preferred_element_type=F32))


def _f3_kernel(mp_ref, x_ref, w_ref, mod_ref, gf_ref, wr_ref, o_ref, h_ref, aff_ref, scr_ref):
    g1 = mod_ref[0, 0][2:3]
    for j in range(SUB):
        scr_ref[j * N1:(j + 1) * N1, :] = mp_ref[:, j, :]
    a = _unpack_bf16_pairs(scr_ref[...])
    y = jnp.dot(a, w_ref[...], preferred_element_type=F32)
    xn = x_ref[...] + (g1 * INV_NORM) * y
    o_ref[...] = xn
    h_ref[...], aff_ref[...] = _route(xn, mod_ref[0, 0], gf_ref[0], wr_ref[...])


def _fourier_layer(x, b, mod_all, gain, gain_ffn, wr, w_out_bf, layer, m0, m1, m2):
    wc = pl.pallas_call(
        _f0_kernel,
        grid=(S // F0_TM,),
        in_specs=[
            _x_spec(x, b, F0_TM),
            _mod_spec(layer, b), _gain_spec(layer),
            pl.BlockSpec((GD, 2 * GD), lambda i: (0, 0)),
        ],
        out_specs=pl.BlockSpec((2, F0_TM, D // 2), lambda i: (0, i, 0)),
        out_shape=jax.ShapeDtypeStruct((2, S, D // 2), jnp.int32),
        compiler_params=_cparams(("arbitrary",)),
        name="fourier_chan",
    )(x, mod_all, gain, m0)
    wc = wc.reshape(2 * N1, N2, D // 2)
    u = pl.pallas_call(
        _f1_kernel,
        grid=(N2 // SUB,),
        in_specs=[
            pl.BlockSpec((2 * N1, SUB, D // 2), lambda i: (0, i, 0)),
            pl.BlockSpec((SUB, 2 * N1, 2 * N1), lambda i: (i, 0, 0)),
        ],
        out_specs=pl.BlockSpec((SUB, 2 * N1, D // 2), lambda i: (i, 0, 0)),
        out_shape=jax.ShapeDtypeStruct((N2, 2 * N1, D // 2), jnp.int32),
        scratch_shapes=[pltpu.VMEM((SUB, 2 * N1, D // 2), jnp.int32)],
        compiler_params=_cparams(("arbitrary",)),
        name="fourier_seq1",
    )(wc, m1)
    u = u.reshape(2 * N2, N1, D // 2)
    mp = pl.pallas_call(
        _f2_kernel,
        grid=(N1 // SUB,),
        in_specs=[
            pl.BlockSpec((2 * N2, SUB, D // 2), lambda i: (0, i, 0)),
            pl.BlockSpec((N2, 2 * N2), lambda i: (0, 0)),
        ],
        out_specs=pl.BlockSpec((SUB, N2, D // 2), lambda i: (i, 0, 0)),
        out_shape=jax.ShapeDtypeStruct((N1, N2, D // 2), jnp.int32),
        scratch_shapes=[pltpu.VMEM((SUB, 2 * N2, D // 2), jnp.int32)],
        compiler_params=_cparams(("arbitrary",)),
        name="fourier_seq2",
    )(u, m2)
    return pl.pallas_call(
        _f3_kernel,
        grid=(N2 // SUB,),
        in_specs=[
            pl.BlockSpec((N1, SUB, D // 2), lambda i: (0, i, 0)),
            _x_spec(x, b, SUB * N1),
            pl.BlockSpec((D, D), lambda i: (0, 0)),
            _mod_spec(layer, b), _gain_spec(layer),
            pl.BlockSpec((D, 2 * EP), lambda i: (0, 0)),
        ],
        out_specs=_routed_specs(SUB * N1),
        out_shape=_ROUTED_SHAPES,
        scratch_shapes=[pltpu.VMEM((SUB * N1, D // 2), jnp.int32)],
        compiler_params=_cparams(("arbitrary",)),
        name="fourier_out",
    )(mp, x, w_out_bf, mod_all, gain_ffn, wr)


QKV_TM = 512
TQ = 256
NSB = TQ // BLK
KWIN = TQ + 2 * BLK
NBLK = S // BLK
KPAD = 128
VROWS = HD + 16
ATT_AHEAD = 2


def _bias_kernel(rb_ref, bucket_ref, o_ref):
    h = pl.program_id(0)
    bucket = bucket_ref[...]
    acc = jnp.full(bucket.shape, NEG_INF, F32)
    for k in range(NBUCKETS):
        acc = jnp.where(bucket == k, rb_ref[k, h], acc)
    o_ref[0] = acc


def _bias_table(rel_bias, bucket_t):
    return pl.pallas_call(
        _bias_kernel,
        grid=(NH,),
        in_specs=[
            pl.BlockSpec(memory_space=pltpu.SMEM),
            pl.BlockSpec((3 * BLK, BLK), lambda h: (0, 0)),
        ],
        out_specs=pl.BlockSpec((1, 3 * BLK, BLK), lambda h: (h // GQA, 0, h % GQA)),
        out_shape=jax.ShapeDtypeStruct((NKV, 3 * BLK, GQA * BLK), F32),
        compiler_params=_cparams(("arbitrary",)),
        name="rel_bias_table",
    )(rel_bias, bucket_t)


def _qkv_kernel(x_ref, mod_ref, gain_ref, wqt_ref, wk_ref, wvt_ref, qg_ref, kg_ref, qt_ref, k_ref, vt_ref):
    m = mod_ref[0, 0]
    h = _modulate(x_ref[...], gain_ref[0], m[0:1], m[1:2]).astype(BF16)
    nt = (((1,), (1,)), ((), ()))
    qt = lax.dot_general(wqt_ref[...], h, nt, preferred_element_type=F32)
    tm = qt.shape[1]
    q3 = qt.reshape(NH, HD, tm)
    q3 = q3 * lax.rsqrt(jnp.mean(q3 * q3, axis=1, keepdims=True) + EPS)
    qt_ref[...] = (q3.reshape(NH * HD, tm) * qg_ref[...]).astype(BF16)
    k = jnp.dot(h, wk_ref[...], preferred_element_type=F32)
    for g in range(NKV):
        kg = k[:, g * KPAD:(g + 1) * KPAD]
        ms = jnp.sum(kg * kg, axis=-1, keepdims=True) * (1.0 / HD)
        k_ref[:, g * KPAD:(g + 1) * KPAD] = (kg * lax.rsqrt(ms + EPS) * kg_ref[...]).astype(BF16)
    vt_ref[...] = lax.dot_general(wvt_ref[...], h, nt, preferred_element_type=F32).astype(BF16)


def _attn_kernel(qt_ref, kp_ref, kc_ref, kn_ref, vp_ref, vc_ref, vn_ref, x_ref, w_ref, mod_ref,
                 bias_ref, sink_ref, gf_ref, wr_ref, o_ref, h_ref, aff_ref, att_ref, s_ref):
    i = pl.program_id(0)
    kwin = jnp.concatenate([kp_ref[...], kc_ref[...], kn_ref[...]], axis=0)
    vwin = jnp.concatenate([vp_ref[...], vc_ref[...], vn_ref[...]], axis=1)
    ones_rows = (lax.broadcasted_iota(jnp.int32, (VROWS - HD, KWIN), 0) == 0).astype(BF16)
    vaug = [jnp.concatenate([vwin[g * HD:(g + 1) * HD], ones_rows], axis=0) for g in range(NKV)]
    key_pos = i * TQ - BLK + lax.broadcasted_iota(jnp.int32, (KWIN, 1), 0)
    key_mask = jnp.where((key_pos >= 0) & (key_pos < S), 0.0, NEG_INF).astype(BF16)
    lane = lax.broadcasted_iota(jnp.int32, (1, NKV * KPAD), 1)
    kwin = jnp.where(lane % KPAD == HD, key_mask, kwin)
    q_ones = (lax.broadcasted_iota(jnp.int32, (KPAD - HD, GQA * BLK), 0) == 0).astype(BF16)
    items = [(g, r) for g in range(NKV) for r in range(NSB)]

    def scores(g, r):
        kr = kwin[r * BLK:r * BLK + 3 * BLK, g * KPAD:(g + 1) * KPAD]
        qg = jnp.concatenate(
            [qt_ref[(GQA * g + hh) * HD:(GQA * g + hh + 1) * HD, r * BLK:(r + 1) * BLK] for hh in range(GQA)],
            axis=1)
        qa = jnp.concatenate([qg, q_ones], axis=0)
        return jnp.dot(kr, qa, preferred_element_type=F32)

    def probs(s, g, r):
        sink = sink_ref[g]
        s = s + bias_ref[g]
        mx = jnp.maximum(jnp.max(s, axis=0, keepdims=True), sink)
        return jnp.exp(s - mx).astype(BF16), jnp.exp(sink - mx)

    nslot = ATT_AHEAD + 1
    for n in range(ATT_AHEAD):
        s_ref[n % nslot] = scores(*items[n])
    for n, (g, r) in enumerate(items):
        if n + ATT_AHEAD < len(items):
            s_ref[(n + ATT_AHEAD) % nslot] = scores(*items[n + ATT_AHEAD])
        p, psink = probs(s_ref[n % nslot], g, r)
        ot = jnp.dot(vaug[g][:, r * BLK:r * BLK + 3 * BLK], p, preferred_element_type=F32)
        ot = ot[:HD] * (1.0 / (ot[HD:HD + 1] + psink))
        for hh in range(GQA):
            hd0 = (GQA * g + hh) * HD
            att_ref[hd0:hd0 + HD, r * BLK:(r + 1) * BLK] = ot[:, hh * BLK:(hh + 1) * BLK]
    g1 = mod_ref[0, 0][2:3]
    att = att_ref[...].T.astype(BF16)
    y = jnp.dot(att, w_ref[...], preferred_element_type=F32)
    xn = x_ref[...] + g1 * y
    o_ref[...] = xn
    h_ref[...], aff_ref[...] = _route(xn, mod_ref[0, 0], gf_ref[0], wr_ref[...])


def _attn_weights(w_qkv, q_gain, k_gain, sink):
    wq, wk, wv = w_qkv[:, :NH * HD], w_qkv[:, NH * HD:(NH + NKV) * HD], w_qkv[:, (NH + NKV) * HD:]
    wqt = wq.T.astype(BF16)
    wvt = wv.T.astype(BF16)
    wk_pad = jnp.pad(wk.reshape(D, NKV, HD), ((0, 0), (0, 0), (0, KPAD - HD))).reshape(D, NKV * KPAD).astype(BF16)
    qg_col = jnp.tile(q_gain * (HD ** -0.5), NH).reshape(NH * HD, 1)
    kg_row = jnp.pad(k_gain, (0, KPAD - HD)).reshape(1, KPAD)
    sink_row = jnp.repeat(sink, BLK).reshape(NKV, 1, GQA * BLK)
    return wqt, wk_pad, wvt, qg_col, kg_row, sink_row


def _attn_layer(x, b, mod_all, gain, gain_ffn, wr, aw, w_out_bf, bias_tab, layer):
    wqt, wk_pad, wvt, qg_col, kg_row, sink_row = aw
    qt, k, vt = pl.pallas_call(
        _qkv_kernel,
        grid=(S // QKV_TM,),
        in_specs=[
            _x_spec(x, b, QKV_TM),
            _mod_spec(layer, b), _gain_spec(layer),
            pl.BlockSpec((NH * HD, D), lambda i: (0, 0)),
            pl.BlockSpec((D, NKV * KPAD), lambda i: (0, 0)),
            pl.BlockSpec((NKV * HD, D), lambda i: (0, 0)),
            pl.BlockSpec((NH * HD, 1), lambda i: (0, 0)),
            pl.BlockSpec((1, KPAD), lambda i: (0, 0)),
        ],
        out_specs=[
            pl.BlockSpec((NH * HD, QKV_TM), lambda i: (0, i)),
            pl.BlockSpec((QKV_TM, NKV * KPAD), lambda i: (i, 0)),
            pl.BlockSpec((NKV * HD, QKV_TM), lambda i: (0, i)),
        ],
        out_shape=[
            jax.ShapeDtypeStruct((NH * HD, S), BF16),
            jax.ShapeDtypeStruct((S, NKV * KPAD), BF16),
            jax.ShapeDtypeStruct((NKV * HD, S), BF16),
        ],
        compiler_params=_cparams(("arbitrary",)),
        name="attn_qkv",
    )(x, mod_all, gain, wqt, wk_pad, wvt, qg_col, kg_row)

    kw = NKV * KPAD
    vw = NKV * HD
    kprev = pl.BlockSpec((BLK, kw), lambda i: (jnp.maximum(i * NSB - 1, 0), 0))
    kcur = pl.BlockSpec((TQ, kw), lambda i: (i, 0))
    knext = pl.BlockSpec((BLK, kw), lambda i: (jnp.minimum((i + 1) * NSB, NBLK - 1), 0))
    vprev = pl.BlockSpec((vw, BLK), lambda i: (0, jnp.maximum(i * NSB - 1, 0)))
    vcur = pl.BlockSpec((vw, TQ), lambda i: (0, i))
    vnext = pl.BlockSpec((vw, BLK), lambda i: (0, jnp.minimum((i + 1) * NSB, NBLK - 1)))
    return pl.pallas_call(
        _attn_kernel,
        grid=(S // TQ,),
        in_specs=[
            pl.BlockSpec((NH * HD, TQ), lambda i: (0, i)),
            kprev, kcur, knext, vprev, vcur, vnext,
            _x_spec(x, b, TQ),
            pl.BlockSpec((NH * HD, D), lambda i: (0, 0)),
            _mod_spec(layer, b),
            pl.BlockSpec((NKV, 3 * BLK, GQA * BLK), lambda i: (0, 0, 0)),
            pl.BlockSpec((NKV, 1, GQA * BLK), lambda i: (0, 0, 0)),
            _gain_spec(layer),
            pl.BlockSpec((D, 2 * EP), lambda i: (0, 0)),
        ],
        out_specs=_routed_specs(TQ),
        out_shape=_ROUTED_SHAPES,
        scratch_shapes=[pltpu.VMEM((NH * HD, TQ), F32), pltpu.VMEM((ATT_AHEAD + 1, 3 * BLK, GQA * BLK), F32)],
        compiler_params=_cparams(("arbitrary",)),
        name="attn_core",
    )(qt, k, k, k, vt, vt, vt, x, w_out_bf, mod_all, bias_tab, sink_row, gain_ffn, wr)


FFN_TF = 1024
FFN_RT = 512
FFN_WT = 256


def _select_kernel(aff_ref, tri_ref, pos_ref):
    aff = aff_ref[...]
    bits = pltpu.bitcast(aff, jnp.int32)
    rows = aff.shape[0]

    def count_ge(v):
        return jnp.sum((bits >= v).astype(jnp.int32), axis=1, keepdims=True)

    def body(t, cur):
        cand = cur | (jnp.int32(1) << (30 - t))
        return jnp.where(count_ge(cand) >= CAP, cand, cur)

    thr = lax.fori_loop(0, 31, body, jnp.zeros((rows, 1), jnp.int32))
    gt = bits > thr
    eq = bits == thr
    need = CAP - jnp.sum(gt.astype(jnp.int32), axis=1, keepdims=True)
    tri = tri_ref[...]

    def cumsum_excl(mask_f):
        off = jnp.zeros((rows, 1), F32)
        outs = []
        for j in range(S // 128):
            mj = mask_f[:, j * 128:(j + 1) * 128]
            loc = jnp.dot(mj.astype(BF16), tri, preferred_element_type=F32)
            outs.append(loc - mj + off)
            off = off + loc[:, 127:128]
        return jnp.concatenate(outs, axis=1)

    eq_rank = cumsum_excl(eq.astype(F32))
    sel = gt | (eq & (eq_rank < need.astype(F32)))
    pos = cumsum_excl(sel.astype(F32))
    pos_ref[...] = jnp.where(sel, pos.astype(jnp.int32), -1)


SC_CORES = 2
SC_SUBCORES = 16
SC_TILES = SC_CORES * SC_SUBCORES
DISPATCH_SLOTS = CAP * E // SC_TILES
DISPATCH_ROWS = 64
COMBINE_RANGE = 1024
COMBINE_ROWS = 32
SLAB = 128
NSLAB = D // SLAB


def _sc_mesh():
    return plsc.VectorSubcoreMesh(core_axis_name="c", subcore_axis_name="s",
                                  num_cores=SC_CORES, num_subcores=SC_SUBCORES)


def _dispatch_body(pos_hbm, aff_hbm, h_hbm, xin_hbm, idx_hbm, gate_hbm, pos_v, aff_v, idx_v, gate_v,
                   rows_a, rows_b, gsem_a, gsem_b, wsem_a, wsem_b):
    w = lax.axis_index("s") * SC_CORES + lax.axis_index("c")
    e = w // 2
    lo = (w % 2) * DISPATCH_SLOTS
    pltpu.sync_copy(pos_hbm.at[e], pos_v)
    pltpu.sync_copy(aff_hbm.at[e], aff_v)

    @pl.loop(0, S // 16)
    def _(i):
        p = pos_v[pl.ds(i * 16, 16)] - lo
        m = (p >= 0) & (p < DISPATCH_SLOTS)
        tok = lax.iota(jnp.int32, 16) + i * 16
        plsc.store_scatter(idx_v, [p], tok, mask=m)
        plsc.store_scatter(gate_v, [p], aff_v[pl.ds(i * 16, 16)], mask=m)

    pltpu.sync_copy(idx_v, idx_hbm.at[e, pl.ds(lo, DISPATCH_SLOTS)])
    pltpu.sync_copy(gate_v, gate_hbm.at[e, pl.ds(lo, DISPATCH_SLOTS)])

    bufs, gsems, wsems = (rows_a, rows_b), (gsem_a, gsem_b), (wsem_a, wsem_b)
    nchunk = DISPATCH_SLOTS // DISPATCH_ROWS

    def gather(j):
        return pltpu.async_copy(h_hbm.at[idx_v.at[pl.ds(j * DISPATCH_ROWS, DISPATCH_ROWS)]], bufs[j % 2], gsems[j % 2])

    pending_gather = gather(0)
    writes = [None, None]
    for j in range(nchunk):
        pending_gather.wait()
        writes[j % 2] = pltpu.async_copy(
            bufs[j % 2], xin_hbm.at[e, pl.ds(lo + j * DISPATCH_ROWS, DISPATCH_ROWS)], wsems[j % 2])
        if j + 1 < nchunk:
            if writes[(j + 1) % 2] is not None:
                writes[(j + 1) % 2].wait()
            pending_gather = gather(j + 1)
    writes[(nchunk - 2) % 2].wait()
    writes[(nchunk - 1) % 2].wait()


def _dispatch(pos, aff, h):
    return pl.kernel(
        _dispatch_body, mesh=_sc_mesh(),
        out_type=[jax.ShapeDtypeStruct((E, CAP, D // 2), jnp.int32),
                  jax.ShapeDtypeStruct((E, CAP), jnp.int32),
                  jax.ShapeDtypeStruct((E, CAP), F32)],
        scratch_types=[pltpu.VMEM((S,), jnp.int32), pltpu.VMEM((S,), F32),
                       pltpu.VMEM((DISPATCH_SLOTS,), jnp.int32), pltpu.VMEM((DISPATCH_SLOTS,), F32),
                       pltpu.VMEM((DISPATCH_ROWS, D // 2), jnp.int32), pltpu.VMEM((DISPATCH_ROWS, D // 2), jnp.int32),
                       pltpu.SemaphoreType.DMA, pltpu.SemaphoreType.DMA,
                       pltpu.SemaphoreType.DMA, pltpu.SemaphoreType.DMA],
        compiler_params=pltpu.CompilerParams(needs_layout_passes=False),
        name="moe_dispatch",
    )(pos, aff, h)


def _combine_body(y_hbm, idx_hbm, x_hbm, out_hbm, idx_v, li_v, *bufs):
    rows = bufs[:NSLAB]
    accs = bufs[NSLAB:]
    c = lax.axis_index("c")
    s = lax.axis_index("s")
    share = COMBINE_RANGE // SC_SUBCORES
    pltpu.sync_copy(idx_hbm.at[s], idx_v)
    lane = lax.iota(jnp.int32, 16)

    @pl.loop(0, S // COMBINE_RANGE // SC_CORES)
    def _(r):
        t0 = (r * SC_CORES + c) * COMBINE_RANGE
        row0 = t0 + s * share
        pltpu.sync_copy(tuple(x_hbm.at[pl.ds(row0, share), pl.ds(q * SLAB, SLAB)] for q in range(NSLAB)),
                        tuple(accs[q].at[pl.ds(s * share, share)] for q in range(NSLAB)))
        plsc.subcore_barrier()

        @pl.loop(0, CAP // COMBINE_ROWS)
        def _(j):
            hits = jnp.zeros((16,), jnp.int32)
            for v in range(COMBINE_ROWS // 16):
                t = idx_v[pl.ds(j * COMBINE_ROWS + v * 16, 16)] - t0
                ok = (t >= 0) & (t < COMBINE_RANGE)
                li_v[pl.ds(v * 16, 16)] = jnp.where(ok, t, COMBINE_RANGE + lane)
                hits = hits + plsc.all_reduce_population_count(ok)

            @pl.when(jnp.max(hits) > 0)
            def _():
                pltpu.sync_copy(
                    tuple(y_hbm.at[s, pl.ds(j * COMBINE_ROWS, COMBINE_ROWS), pl.ds(q * SLAB, SLAB)]
                          for q in range(NSLAB)),
                    tuple(rows))
                pltpu.sync_copy(tuple(rows), tuple(accs[q].at[li_v] for q in range(NSLAB)), add=True)

        plsc.subcore_barrier()
        pltpu.sync_copy(tuple(accs[q].at[pl.ds(s * share, share)] for q in range(NSLAB)),
                        tuple(out_hbm.at[pl.ds(row0, share), pl.ds(q * SLAB, SLAB)] for q in range(NSLAB)))


def _combine(y, idx, x):
    return pl.kernel(
        _combine_body, mesh=_sc_mesh(),
        out_type=jax.ShapeDtypeStruct((S, D), F32),
        scratch_types=[pltpu.VMEM((CAP,), jnp.int32), pltpu.VMEM((COMBINE_ROWS,), jnp.int32)]
        + [pltpu.VMEM((COMBINE_ROWS, SLAB), F32) for _ in range(NSLAB)]
        + [pltpu.VMEM_SHARED((COMBINE_RANGE + 16, SLAB), F32) for _ in range(NSLAB)],
        compiler_params=pltpu.CompilerParams(needs_layout_passes=False),
        name="moe_combine",
    )(y, idx, x)


def _ffn_kernel(x_ref, wg_ref, wu_ref, wd_ref, gate_ref, mod_ref, o_ref):
    f = pl.program_id(1)
    last = pl.num_programs(1) - 1
    nt = FFN_TF // FFN_WT
    wg = [wg_ref[0, 0, :, j * FFN_WT:(j + 1) * FFN_WT].astype(BF16) for j in range(nt)]
    wu = [wu_ref[0, 0, :, j * FFN_WT:(j + 1) * FFN_WT].astype(BF16) for j in range(nt)]
    wd = wd_ref[0, 0].astype(BF16)
    is_first = f == 0
    is_last = f == last
    g2 = jnp.where(is_last, mod_ref[0, 0][5:6], 1.0)
    g_row = gate_ref[pl.ds(pl.program_id(0), 1), :]
    g_col = jnp.broadcast_to(g_row, (128, CAP)).T[:, 0:1]
    for r in range(CAP // FFN_RT):
        rows = slice(r * FFN_RT, (r + 1) * FFN_RT)
        xr = _unpack_bf16_pairs(x_ref[0, rows, :])
        acts = []
        for j in range(nt):
            g = jnp.dot(xr, wg[j], preferred_element_type=F32)
            u = jnp.dot(xr, wu[j], preferred_element_type=F32)
            acts.append((g * _sigmoid(g) * u).astype(BF16))
        y = jnp.dot(jnp.concatenate(acts, axis=1), wd, preferred_element_type=F32)
        prev = jnp.where(is_first, 0.0, o_ref[0, rows, :])
        gate = jnp.where(is_last, g_col[rows, :], 1.0)
        o_ref[0, rows, :] = (prev + y) * gate * g2


def _select(aff, tri):
    return pl.pallas_call(
        _select_kernel,
        grid=(1,),
        in_specs=[
            pl.BlockSpec((E, S), lambda i: (0, 0)),
            pl.BlockSpec((128, 128), lambda i: (0, 0)),
        ],
        out_specs=pl.BlockSpec((E, S), lambda i: (0, 0)),
        out_shape=jax.ShapeDtypeStruct((E, S), jnp.int32),
        compiler_params=_cparams(("arbitrary",)),
        name="moe_select",
    )(aff, tri)


def _ffn(xin, gate, b, mod_all, w_gate, w_up, w_down, layer):
    return pl.pallas_call(
        _ffn_kernel,
        grid=(E, F // FFN_TF),
        in_specs=[
            pl.BlockSpec((1, CAP, D // 2), lambda e, f: (e, 0, 0)),
            pl.BlockSpec((1, 1, D, FFN_TF), lambda e, f: (layer, e, 0, f)),
            pl.BlockSpec((1, 1, D, FFN_TF), lambda e, f: (layer, e, 0, f)),
            pl.BlockSpec((1, 1, FFN_TF, D), lambda e, f: (layer, e, f, 0)),
            pl.BlockSpec((E, CAP), lambda e, f: (0, 0)),
            pl.BlockSpec((1, 1, 6, D), lambda e, f: (layer, b, 0, 0)),
        ],
        out_specs=pl.BlockSpec((1, CAP, D), lambda e, f: (e, 0, 0)),
        out_shape=jax.ShapeDtypeStruct((E, CAP, D), F32),
        compiler_params=_cparams(("arbitrary", "arbitrary"), 48),
        name="moe_ffn",
    )(xin, w_gate, w_up, w_down, gate, mod_all)


def kernel(x, c, w_ada, b_ada, norm_mix, norm_ffn, w_fourier_out, w_qkv, w_attn_out, q_gain, k_gain,
           sink, rel_bias, w_router, w_gate, w_up, w_down):
    m0, m2 = _dft_tables()
    m1 = _stage1_table()
    bucket_t = _bucket_table()
    tri = jnp.asarray(np.triu(np.ones((128, 128), np.float32)), BF16)
    mod_all = _ada(c, w_ada, b_ada)
    bias_tab = _bias_table(rel_bias, bucket_t)
    gain_mix = norm_mix.reshape(DEPTH, 1, D)
    gain_ffn = norm_ffn.reshape(DEPTH, 1, D)
    xs = [x, x]
    for layer in range(DEPTH):
        j = layer // 2
        wr = jnp.pad(w_router[layer], ((0, 0), (0, EP - E)))
        wr1 = wr.astype(BF16)
        wr = jnp.concatenate([wr1, (wr - wr1.astype(F32)).astype(BF16)], axis=1)
        if layer % 2 == 0:
            w_out_bf = w_fourier_out[j].astype(BF16)
            routed = [_fourier_layer(xs[b], b, mod_all, gain_mix, gain_ffn, wr, w_out_bf, layer, m0, m1, m2)
                      for b in range(B)]
        else:
            aw = _attn_weights(w_qkv[j], q_gain[j], k_gain[j], sink[j])
            w_out_bf = w_attn_out[j].astype(BF16)
            routed = [_attn_layer(xs[b], b, mod_all, gain_mix, gain_ffn, wr, aw, w_out_bf, bias_tab, layer)
                      for b in range(B)]
        xs = [routed[b][0] for b in range(B)]
        pos = [_select(routed[b][2], tri) for b in range(B)]
        disp = [_dispatch(pos[b], routed[b][2], routed[b][1]) for b in range(B)]
        ys = [_ffn(disp[b][0], disp[b][2], b, mod_all, w_gate, w_up, w_down, layer) for b in range(B)]
        xs = [_combine(ys[b], disp[b][1], xs[b]) for b in range(B)]
    return jnp.stack(xs)
```

```python
import math

import numpy as np
import jax
import jax.numpy as jnp
from jax import lax
from jax.experimental import pallas as pl
from jax.experimental.pallas import tpu as pltpu
from jax.experimental.pallas import tpu_sc as plsc

D = 1024
B = 2
S = 8192
DEPTH = 4
GROUPS = 4
GD = D // GROUPS
HD = 64
NH = 16
NKV = 4
GQA = NH // NKV
WINDOW = 128
BLK = 128
NBUCKETS = 32
MAXDIST = 128
E = 16
CAP = 2 * S // E
F = 2 * D
EPS = 1e-6
NEG_INF = -1e30

N1 = 128
N2 = 64
INV_NORM = 1.0 / math.sqrt(S * GD)

F32 = jnp.float32
BF16 = jnp.bfloat16


def _cparams(sem, vmem_mb=48):
    return pltpu.CompilerParams(dimension_semantics=sem, vmem_limit_bytes=vmem_mb * 1024 * 1024)


def _dft_tables():
    c = np.arange(GD)
    ang0 = 2.0 * np.pi * ((c[:, None] * c[None, :]) % GD) / GD
    m0 = np.concatenate([np.cos(ang0), -np.sin(ang0)], axis=1)
    k2 = np.arange(N2)
    ang2 = 2.0 * np.pi * ((k2[:, None] * k2[None, :]) % N2) / N2
    m2 = np.stack([np.cos(ang2), np.sin(ang2)], axis=2).reshape(N2, 2 * N2)
    return jnp.asarray(m0, BF16), jnp.asarray(m2, BF16)


def _stage1_table():
    s2 = np.arange(N2)[:, None, None]
    k1 = np.arange(N1)[None, :, None]
    s1 = np.arange(N1)[None, None, :]
    th = ((k1 * (N2 * s1 + s2)) % S) * (2.0 * np.pi / S)
    co, si = np.cos(th), np.sin(th)
    top = np.concatenate([co, si], axis=2)
    bot = np.concatenate([-si, co], axis=2)
    return jnp.asarray(np.concatenate([top, bot], axis=1).astype(np.float32), BF16)


def _bucket_table():
    q_off = np.arange(BLK)
    k_off = np.arange(3 * BLK) - BLK
    rel = k_off[:, None] - q_off[None, :]
    half = NBUCKETS // 2
    max_exact = half // 2
    ret = np.where(rel > 0, half, 0)
    n = np.abs(rel)
    nf = np.maximum(n, 1).astype(np.float32)
    ratio = (np.log(nf / np.float32(max_exact)) / np.float32(math.log(MAXDIST / max_exact))).astype(np.float32)
    large = max_exact + (ratio * np.float32(half - max_exact)).astype(np.int32)
    large = np.minimum(large, half - 1)
    bucket = ret + np.where(n < max_exact, n, large)
    return jnp.asarray(np.where(np.abs(rel) <= WINDOW, bucket, -1).astype(np.int32))


def _modulate(x, gain, shift, scale):
    ms = jnp.mean(x * x, axis=-1, keepdims=True)
    return x * lax.rsqrt(ms + EPS) * (gain * (1.0 + scale)) + shift


def _sigmoid(x):
    return 1.0 / (1.0 + jnp.exp(-x))


ADA_TN = 768


def _ada_kernel(ct_ref, w_ref, b_ref, o_ref):
    ct = ct_ref[...]
    ca = ct * _sigmoid(ct)
    w = w_ref[0]
    for b in range(B):
        o_ref[0, b:b + 1, :] = jnp.sum(w * ca[:, b:b + 1], axis=0, keepdims=True) + b_ref[0]


def _ada(c, w_ada, b_ada):
    out = pl.pallas_call(
        _ada_kernel,
        grid=(DEPTH, 6 * D // ADA_TN),
        in_specs=[
            pl.BlockSpec((D, B), lambda l, j: (0, 0)),
            pl.BlockSpec((1, D, ADA_TN), lambda l, j: (l, 0, j)),
            pl.BlockSpec((1, 1, ADA_TN), lambda l, j: (l, 0, j)),
        ],
        out_specs=pl.BlockSpec((1, B, ADA_TN), lambda l, j: (l, 0, j)),
        out_shape=jax.ShapeDtypeStruct((DEPTH, B, 6 * D), F32),
        compiler_params=_cparams(("arbitrary", "arbitrary"), 32),
        name="ada_mod",
    )(c.T, w_ada, b_ada.reshape(DEPTH, 1, 6 * D))
    return out.reshape(DEPTH, B, 6, D)


def _mod_spec(layer, b):
    return pl.BlockSpec((1, 1, 6, D), lambda *_: (layer, b, 0, 0))


def _gain_spec(layer):
    return pl.BlockSpec((1, 1, D), lambda *_: (layer, 0, 0))


def _x_spec(x, b, tm):
    if x.ndim == 3:
        return pl.BlockSpec((None, tm, D), lambda i: (b, i, 0))
    return pl.BlockSpec((tm, D), lambda i: (i, 0))


EP = 128

_ROUTED_SHAPES = [jax.ShapeDtypeStruct((S, D), F32),
                  jax.ShapeDtypeStruct((S, D // 2), jnp.int32),
                  jax.ShapeDtypeStruct((E, S), F32)]


def _routed_specs(tm):
    return [pl.BlockSpec((tm, D), lambda i: (i, 0)),
            pl.BlockSpec((tm, D // 2), lambda i: (i, 0)),
            pl.BlockSpec((E, tm), lambda i: (0, i))]


F0_TM = 1024
SUB = 8


def _pack_bf16_pairs(h):
    hb = h.astype(BF16).astype(F32)
    lo = pltpu.bitcast(hb[:, :D // 2], jnp.int32)
    hi = pltpu.bitcast(hb[:, D // 2:], jnp.int32)
    return hi | lax.shift_right_logical(lo, jnp.int32(16))


def _unpack_bf16_pairs(xp):
    lo = pltpu.bitcast(xp << 16, F32).astype(BF16)
    hi = pltpu.bitcast(xp & jnp.int32(-65536), F32).astype(BF16)
    return jnp.concatenate([lo, hi], axis=1)


def _route(xn, m, gain_ffn, wr):
    h = _modulate(xn, gain_ffn, m[3:4], m[4:5])
    h1 = h.astype(BF16)
    h2 = (h - h1.astype(F32)).astype(BF16)
    part = jnp.dot(h1, wr, preferred_element_type=F32)
    logits = part[:, :EP] + part[:, EP:] + jnp.dot(h2, wr[:, :EP], preferred_element_type=F32)
    lt = logits.T[:E]
    ex = jnp.exp(lt - jnp.max(lt, axis=0, keepdims=True))
    return _pack_bf16_pairs(h), ex / jnp.sum(ex, axis=0, keepdims=True)


def _f0_kernel(x_ref, mod_ref, gain_ref, m0_ref, o_ref):
    m = mod_ref[0, 0]
    h = _modulate(x_ref[...], gain_ref[0], m[0:1], m[1:2]).astype(BF16)
    m0 = m0_ref[...]
    r = [jnp.dot(h[:, g * GD:(g + 1) * GD], m0, preferred_element_type=F32) for g in range(GROUPS)]
    o_ref[0] = _pack_bf16_pairs(jnp.concatenate([rg[:, :GD] for rg in r], axis=1))
    o_ref[1] = _pack_bf16_pairs(jnp.concatenate([rg[:, GD:] for rg in r], axis=1))


def _f1_kernel(w_ref, m1_ref, o_ref, scr_ref):
    for j in range(SUB):
        scr_ref[j] = w_ref[:, j, :]
    for j in range(SUB):
        w = _unpack_bf16_pairs(scr_ref[j])
        o_ref[j] = _pack_bf16_pairs(jnp.dot(m1_ref[j], w, preferred_element_type=F32))


def _f2_kernel(u_ref, m2_ref, o_ref, scr_ref):
    m2 = m2_ref[...]
    for k in range(SUB):
        scr_ref[k] = u_ref[:, k, :]
    for k in range(SUB):
        u = _unpack_bf16_pairs(scr_ref[k])
        o_ref[k] = _pack_bf16_pairs(jnp.dot(m2, u, preferred_element_type=F32))


def _f3_kernel(mp_ref, x_ref, w_ref, mod_ref, gf_ref, wr_ref, o_ref, h_ref, aff_ref, scr_ref):
    g1 = mod_ref[0, 0][2:3]
    for j in range(SUB):
        scr_ref[j * N1:(j + 1) * N1, :] = mp_ref[:, j, :]
    a = _unpack_bf16_pairs(scr_ref[...])
    y = jnp.dot(a, w_ref[...], preferred_element_type=F32)
    xn = x_ref[...] + (g1 * INV_NORM) * y
    o_ref[...] = xn
    h_ref[...], aff_ref[...] = _route(xn, mod_ref[0, 0], gf_ref[0], wr_ref[...])


def _fourier_layer(x, b, mod_all, gain, gain_ffn, wr, w_out_bf, layer, m0, m1, m2):
    wc = pl.pallas_call(
        _f0_kernel,
        grid=(S // F0_TM,),
        in_specs=[
            _x_spec(x, b, F0_TM),
            _mod_spec(layer, b), _gain_spec(layer),
            pl.BlockSpec((GD, 2 * GD), lambda i: (0, 0)),
        ],
        out_specs=pl.BlockSpec((2, F0_TM, D // 2), lambda i: (0, i, 0)),
        out_shape=jax.ShapeDtypeStruct((2, S, D // 2), jnp.int32),
        compiler_params=_cparams(("arbitrary",)),
        name="fourier_chan",
    )(x, mod_all, gain, m0)
    wc = wc.reshape(2 * N1, N2, D // 2)
    u = pl.pallas_call(
        _f1_kernel,
        grid=(N2 // SUB,),
        in_specs=[
            pl.BlockSpec((2 * N1, SUB, D // 2), lambda i: (0, i, 0)),
            pl.BlockSpec((SUB, 2 * N1, 2 * N1), lambda i: (i, 0, 0)),
        ],
        out_specs=pl.BlockSpec((SUB, 2 * N1, D // 2), lambda i: (i, 0, 0)),
        out_shape=jax.ShapeDtypeStruct((N2, 2 * N1, D // 2), jnp.int32),
        scratch_shapes=[pltpu.VMEM((SUB, 2 * N1, D // 2), jnp.int32)],
        compiler_params=_cparams(("arbitrary",)),
        name="fourier_seq1",
    )(wc, m1)
    u = u.reshape(2 * N2, N1, D // 2)
    mp = pl.pallas_call(
        _f2_kernel,
        grid=(N1 // SUB,),
        in_specs=[
            pl.BlockSpec((2 * N2, SUB, D // 2), lambda i: (0, i, 0)),
            pl.BlockSpec((N2, 2 * N2), lambda i: (0, 0)),
        ],
        out_specs=pl.BlockSpec((SUB, N2, D // 2), lambda i: (i, 0, 0)),
        out_shape=jax.ShapeDtypeStruct((N1, N2, D // 2), jnp.int32),
        scratch_shapes=[pltpu.VMEM((SUB, 2 * N2, D // 2), jnp.int32)],
        compiler_params=_cparams(("arbitrary",)),
        name="fourier_seq2",
    )(u, m2)
    return pl.pallas_call(
        _f3_kernel,
        grid=(N2 // SUB,),
        in_specs=[
            pl.BlockSpec((N1, SUB, D // 2), lambda i: (0, i, 0)),
            _x_spec(x, b, SUB * N1),
            pl.BlockSpec((D, D), lambda i: (0, 0)),
            _mod_spec(layer, b), _gain_spec(layer),
            pl.BlockSpec((D, 2 * EP), lambda i: (0, 0)),
        ],
        out_specs=_routed_specs(SUB * N1),
        out_shape=_ROUTED_SHAPES,
        scratch_shapes=[pltpu.VMEM((SUB * N1, D // 2), jnp.int32)],
        compiler_params=_cparams(("arbitrary",)),
        name="fourier_out",
    )(mp, x, w_out_bf, mod_all, gain_ffn, wr)


QKV_TM = 512
TQ = 512
NSB = TQ // BLK
KWIN = TQ + 2 * BLK
NBLK = S // BLK
KPAD = 128
VROWS = HD + 16
ATT_AHEAD = 1


def _bias_kernel(rb_ref, bucket_ref, o_ref):
    h = pl.program_id(0)
    bucket = bucket_ref[...]
    acc = jnp.full(bucket.shape, NEG_INF, F32)
    for k in range(NBUCKETS):
        acc = jnp.where(bucket == k, rb_ref[k, h], acc)
    o_ref[0] = acc


def _bias_table(rel_bias, bucket_t):
    return pl.pallas_call(
        _bias_kernel,
        grid=(NH,),
        in_specs=[
            pl.BlockSpec(memory_space=pltpu.SMEM),
            pl.BlockSpec((3 * BLK, BLK), lambda h: (0, 0)),
        ],
        out_specs=pl.BlockSpec((1, 3 * BLK, BLK), lambda h: (h // GQA, 0, h % GQA)),
        out_shape=jax.ShapeDtypeStruct((NKV, 3 * BLK, GQA * BLK), F32),
        compiler_params=_cparams(("arbitrary",)),
        name="rel_bias_table",
    )(rel_bias, bucket_t)


def _qkv_kernel(x_ref, mod_ref, gain_ref, wqt_ref, wk_ref, wvt_ref, qg_ref, kg_ref, qt_ref, k_ref, vt_ref):
    m = mod_ref[0, 0]
    h = _modulate(x_ref[...], gain_ref[0], m[0:1], m[1:2]).astype(BF16)
    nt = (((1,), (1,)), ((), ()))
    qt = lax.dot_general(wqt_ref[...], h, nt, preferred_element_type=F32)
    tm = qt.shape[1]
    q3 = qt.reshape(NH, HD, tm)
    q3 = q3 * lax.rsqrt(jnp.mean(q3 * q3, axis=1, keepdims=True) + EPS)
    qt_ref[...] = (q3.reshape(NH * HD, tm) * qg_ref[...]).astype(BF16)
    k = jnp.dot(h, wk_ref[...], preferred_element_type=F32)
    for g in range(NKV):
        kg = k[:, g * KPAD:(g + 1) * KPAD]
        ms = jnp.sum(kg * kg, axis=-1, keepdims=True) * (1.0 / HD)
        k_ref[:, g * KPAD:(g + 1) * KPAD] = (kg * lax.rsqrt(ms + EPS) * kg_ref[...]).astype(BF16)
    vt_ref[...] = lax.dot_general(wvt_ref[...], h, nt, preferred_element_type=F32).astype(BF16)


def _attn_kernel(qt_ref, kp_ref, kc_ref, kn_ref, vp_ref, vc_ref, vn_ref, x_ref, w_ref, mod_ref,
                 bias_ref, sink_ref, gf_ref, wr_ref, o_ref, h_ref, aff_ref, att_ref, s_ref):
    i = pl.program_id(0)
    kwin = jnp.concatenate([kp_ref[...], kc_ref[...], kn_ref[...]], axis=0)
    vwin = jnp.concatenate([vp_ref[...], vc_ref[...], vn_ref[...]], axis=1)
    ones_rows = (lax.broadcasted_iota(jnp.int32, (VROWS - HD, KWIN), 0) == 0).astype(BF16)
    vaug = [jnp.concatenate([vwin[g * HD:(g + 1) * HD], ones_rows], axis=0) for g in range(NKV)]
    key_pos = i * TQ - BLK + lax.broadcasted_iota(jnp.int32, (KWIN, 1), 0)
    key_mask = jnp.where((key_pos >= 0) & (key_pos < S), 0.0, NEG_INF).astype(BF16)
    lane = lax.broadcasted_iota(jnp.int32, (1, NKV * KPAD), 1)
    kwin = jnp.where(lane % KPAD == HD, key_mask, kwin)
    q_ones = (lax.broadcasted_iota(jnp.int32, (KPAD - HD, GQA * BLK), 0) == 0).astype(BF16)
    items = [(g, r) for g in range(NKV) for r in range(NSB)]

    def scores(g, r):
        kr = kwin[r * BLK:r * BLK + 3 * BLK, g * KPAD:(g + 1) * KPAD]
        qg = jnp.concatenate(
            [qt_ref[(GQA * g + hh) * HD:(GQA * g + hh + 1) * HD, r * BLK:(r + 1) * BLK] for hh in range(GQA)],
            axis=1)
        qa = jnp.concatenate([qg, q_ones], axis=0)
        return jnp.dot(kr, qa, preferred_element_type=F32)

    def probs(s, g, r):
        sink = sink_ref[g]
        s = s + bias_ref[g]
        mx = jnp.maximum(jnp.max(s, axis=0, keepdims=True), sink)
        return jnp.exp(s - mx).astype(BF16), jnp.exp(sink - mx)

    nslot = ATT_AHEAD + 1
    for n in range(ATT_AHEAD):
        s_ref[n % nslot] = scores(*items[n])
    for n, (g, r) in enumerate(items):
        if n + ATT_AHEAD < len(items):
            s_ref[(n + ATT_AHEAD) % nslot] = scores(*items[n + ATT_AHEAD])
        p, psink = probs(s_ref[n % nslot], g, r)
        ot = jnp.dot(vaug[g][:, r * BLK:r * BLK + 3 * BLK], p, preferred_element_type=F32)
        ot = ot[:HD] * (1.0 / (ot[HD:HD + 1] + psink))
        for hh in range(GQA):
            hd0 = (GQA * g + hh) * HD
            att_ref[hd0:hd0 + HD, r * BLK:(r + 1) * BLK] = ot[:, hh * BLK:(hh + 1) * BLK]
    g1 = mod_ref[0, 0][2:3]
    att = att_ref[...].T.astype(BF16)
    y = jnp.dot(att, w_ref[...], preferred_element_type=F32)
    xn = x_ref[...] + g1 * y
    o_ref[...] = xn
    h_ref[...], aff_ref[...] = _route(xn, mod_ref[0, 0], gf_ref[0], wr_ref[...])


def _attn_weights(w_qkv, q_gain, k_gain, sink):
    wq, wk, wv = w_qkv[:, :NH * HD], w_qkv[:, NH * HD:(NH + NKV) * HD], w_qkv[:, (NH + NKV) * HD:]
    wqt = wq.T.astype(BF16)
    wvt = wv.T.astype(BF16)
    wk_pad = jnp.pad(wk.reshape(D, NKV, HD), ((0, 0), (0, 0), (0, KPAD - HD))).reshape(D, NKV * KPAD).astype(BF16)
    qg_col = jnp.tile(q_gain * (HD ** -0.5), NH).reshape(NH * HD, 1)
    kg_row = jnp.pad(k_gain, (0, KPAD - HD)).reshape(1, KPAD)
    sink_row = jnp.repeat(sink, BLK).reshape(NKV, 1, GQA * BLK)
    return wqt, wk_pad, wvt, qg_col, kg_row, sink_row


def _attn_layer(x, b, mod_all, gain, gain_ffn, wr, aw, w_out_bf, bias_tab, layer):
    wqt, wk_pad, wvt, qg_col, kg_row, sink_row = aw
    qt, k, vt = pl.pallas_call(
        _qkv_kernel,
        grid=(S // QKV_TM,),
        in_specs=[
            _x_spec(x, b, QKV_TM),
            _mod_spec(layer, b), _gain_spec(layer),
            pl.BlockSpec((NH * HD, D), lambda i: (0, 0)),
            pl.BlockSpec((D, NKV * KPAD), lambda i: (0, 0)),
            pl.BlockSpec((NKV * HD, D), lambda i: (0, 0)),
            pl.BlockSpec((NH * HD, 1), lambda i: (0, 0)),
            pl.BlockSpec((1, KPAD), lambda i: (0, 0)),
        ],
        out_specs=[
            pl.BlockSpec((NH * HD, QKV_TM), lambda i: (0, i)),
            pl.BlockSpec((QKV_TM, NKV * KPAD), lambda i: (i, 0)),
            pl.BlockSpec((NKV * HD, QKV_TM), lambda i: (0, i)),
        ],
        out_shape=[
            jax.ShapeDtypeStruct((NH * HD, S), BF16),
            jax.ShapeDtypeStruct((S, NKV * KPAD), BF16),
            jax.ShapeDtypeStruct((NKV * HD, S), BF16),
        ],
        compiler_params=_cparams(("arbitrary",)),
        name="attn_qkv",
    )(x, mod_all, gain, wqt, wk_pad, wvt, qg_col, kg_row)

    kw = NKV * KPAD
    vw = NKV * HD
    kprev = pl.BlockSpec((BLK, kw), lambda i: (jnp.maximum(i * NSB - 1, 0), 0))
    kcur = pl.BlockSpec((TQ, kw), lambda i: (i, 0))
    knext = pl.BlockSpec((BLK, kw), lambda i: (jnp.minimum((i + 1) * NSB, NBLK - 1), 0))
    vprev = pl.BlockSpec((vw, BLK), lambda i: (0, jnp.maximum(i * NSB - 1, 0)))
    vcur = pl.BlockSpec((vw, TQ), lambda i: (0, i))
    vnext = pl.BlockSpec((vw, BLK), lambda i: (0, jnp.minimum((i + 1) * NSB, NBLK - 1)))
    return pl.pallas_call(
        _attn_kernel,
        grid=(S // TQ,),
        in_specs=[
            pl.BlockSpec((NH * HD, TQ), lambda i: (0, i)),
            kprev, kcur, knext, vprev, vcur, vnext,
            _x_spec(x, b, TQ),
            pl.BlockSpec((NH * HD, D), lambda i: (0, 0)),
            _mod_spec(layer, b),
            pl.BlockSpec((NKV, 3 * BLK, GQA * BLK), lambda i: (0, 0, 0)),
            pl.BlockSpec((NKV, 1, GQA * BLK), lambda i: (0, 0, 0)),
            _gain_spec(layer),
            pl.BlockSpec((D, 2 * EP), lambda i: (0, 0)),
        ],
        out_specs=_routed_specs(TQ),
        out_shape=_ROUTED_SHAPES,
        scratch_shapes=[pltpu.VMEM((NH * HD, TQ), F32), pltpu.VMEM((ATT_AHEAD + 1, 3 * BLK, GQA * BLK), F32)],
        compiler_params=_cparams(("arbitrary",)),
        name="attn_core",
    )(qt, k, k, k, vt, vt, vt, x, w_out_bf, mod_all, bias_tab, sink_row, gain_ffn, wr)


FFN_TF = 1024
FFN_RT = 512
FFN_WT = 256


def _select_kernel(aff_ref, tri_ref, pos_ref):
    aff = aff_ref[...]
    bits = pltpu.bitcast(aff, jnp.int32)
    rows = aff.shape[0]

    def count_ge(v):
        return jnp.sum((bits >= v).astype(jnp.int32), axis=1, keepdims=True)

    def body(t, cur):
        cand = cur | (jnp.int32(1) << (30 - t))
        return jnp.where(count_ge(cand) >= CAP, cand, cur)

    thr = lax.fori_loop(0, 31, body, jnp.zeros((rows, 1), jnp.int32))
    gt = bits > thr
    eq = bits == thr
    need = CAP - jnp.sum(gt.astype(jnp.int32), axis=1, keepdims=True)
    tri = tri_ref[...]

    def cumsum_excl(mask_f):
        off = jnp.zeros((rows, 1), F32)
        outs = []
        for j in range(S // 128):
            mj = mask_f[:, j * 128:(j + 1) * 128]
            loc = jnp.dot(mj.astype(BF16), tri, preferred_element_type=F32)
            outs.append(loc - mj + off)
            off = off + loc[:, 127:128]
        return jnp.concatenate(outs, axis=1)

    eq_rank = cumsum_excl(eq.astype(F32))
    sel = gt | (eq & (eq_rank < need.astype(F32)))
    pos = cumsum_excl(sel.astype(F32))
    pos_ref[...] = jnp.where(sel, pos.astype(jnp.int32), -1)


SC_CORES = 2
SC_SUBCORES = 16
SC_TILES = SC_CORES * SC_SUBCORES
DISPATCH_SLOTS = CAP * E // SC_TILES
DISPATCH_ROWS = 64
COMBINE_RANGE = 1024
COMBINE_ROWS = 32
SLAB = 128
NSLAB = D // SLAB


def _sc_mesh():
    return plsc.VectorSubcoreMesh(core_axis_name="c", subcore_axis_name="s",
                                  num_cores=SC_CORES, num_subcores=SC_SUBCORES)


def _dispatch_body(pos_hbm, aff_hbm, h_hbm, xin_hbm, idx_hbm, gate_hbm, pos_v, aff_v, idx_v, gate_v,
                   rows_a, rows_b, gsem_a, gsem_b, wsem_a, wsem_b):
    w = lax.axis_index("s") * SC_CORES + lax.axis_index("c")
    e = w // 2
    lo = (w % 2) * DISPATCH_SLOTS
    pltpu.sync_copy(pos_hbm.at[e], pos_v)
    pltpu.sync_copy(aff_hbm.at[e], aff_v)

    @pl.loop(0, S // 16)
    def _(i):
        p = pos_v[pl.ds(i * 16, 16)] - lo
        m = (p >= 0) & (p < DISPATCH_SLOTS)
        tok = lax.iota(jnp.int32, 16) + i * 16
        plsc.store_scatter(idx_v, [p], tok, mask=m)
        plsc.store_scatter(gate_v, [p], aff_v[pl.ds(i * 16, 16)], mask=m)

    pltpu.sync_copy(idx_v, idx_hbm.at[e, pl.ds(lo, DISPATCH_SLOTS)])
    pltpu.sync_copy(gate_v, gate_hbm.at[e, pl.ds(lo, DISPATCH_SLOTS)])

    bufs, gsems, wsems = (rows_a, rows_b), (gsem_a, gsem_b), (wsem_a, wsem_b)
    nchunk = DISPATCH_SLOTS // DISPATCH_ROWS

    def gather(j):
        return pltpu.async_copy(h_hbm.at[idx_v.at[pl.ds(j * DISPATCH_ROWS, DISPATCH_ROWS)]], bufs[j % 2], gsems[j % 2])

    pending_gather = gather(0)
    writes = [None, None]
    for j in range(nchunk):
        pending_gather.wait()
        writes[j % 2] = pltpu.async_copy(
            bufs[j % 2], xin_hbm.at[e, pl.ds(lo + j * DISPATCH_ROWS, DISPATCH_ROWS)], wsems[j % 2])
        if j + 1 < nchunk:
            if writes[(j + 1) % 2] is not None:
                writes[(j + 1) % 2].wait()
            pending_gather = gather(j + 1)
    writes[(nchunk - 2) % 2].wait()
    writes[(nchunk - 1) % 2].wait()


def _dispatch(pos, aff, h):
    return pl.kernel(
        _dispatch_body, mesh=_sc_mesh(),
        out_type=[jax.ShapeDtypeStruct((E, CAP, D // 2), jnp.int32),
                  jax.ShapeDtypeStruct((E, CAP), jnp.int32),
                  jax.ShapeDtypeStruct((E, CAP), F32)],
        scratch_types=[pltpu.VMEM((S,), jnp.int32), pltpu.VMEM((S,), F32),
                       pltpu.VMEM((DISPATCH_SLOTS,), jnp.int32), pltpu.VMEM((DISPATCH_SLOTS,), F32),
                       pltpu.VMEM((DISPATCH_ROWS, D // 2), jnp.int32), pltpu.VMEM((DISPATCH_ROWS, D // 2), jnp.int32),
                       pltpu.SemaphoreType.DMA, pltpu.SemaphoreType.DMA,
                       pltpu.SemaphoreType.DMA, pltpu.SemaphoreType.DMA],
        compiler_params=pltpu.CompilerParams(needs_layout_passes=False),
        name="moe_dispatch",
    )(pos, aff, h)


def _combine_body(y_hbm, idx_hbm, x_hbm, out_hbm, idx_v, li_v, *bufs):
    rows = bufs[:NSLAB]
    accs = bufs[NSLAB:]
    c = lax.axis_index("c")
    s = lax.axis_index("s")
    share = COMBINE_RANGE // SC_SUBCORES
    pltpu.sync_copy(idx_hbm.at[s], idx_v)
    lane = lax.iota(jnp.int32, 16)

    @pl.loop(0, S // COMBINE_RANGE // SC_CORES)
    def _(r):
        t0 = (r * SC_CORES + c) * COMBINE_RANGE
        row0 = t0 + s * share
        pltpu.sync_copy(tuple(x_hbm.at[pl.ds(row0, share), pl.ds(q * SLAB, SLAB)] for q in range(NSLAB)),
                        tuple(accs[q].at[pl.ds(s * share, share)] for q in range(NSLAB)))
        plsc.subcore_barrier()

        @pl.loop(0, CAP // COMBINE_ROWS)
        def _(j):
            hits = jnp.zeros((16,), jnp.int32)
            for v in range(COMBINE_ROWS // 16):
                t = idx_v[pl.ds(j * COMBINE_ROWS + v * 16, 16)] - t0
                ok = (t >= 0) & (t < COMBINE_RANGE)
                li_v[pl.ds(v * 16, 16)] = jnp.where(ok, t, COMBINE_RANGE + lane)
                hits = hits + plsc.all_reduce_population_count(ok)

            @pl.when(jnp.max(hits) > 0)
            def _():
                pltpu.sync_copy(
                    tuple(y_hbm.at[s, pl.ds(j * COMBINE_ROWS, COMBINE_ROWS), pl.ds(q * SLAB, SLAB)]
                          for q in range(NSLAB)),
                    tuple(rows))
                pltpu.sync_copy(tuple(rows), tuple(accs[q].at[li_v] for q in range(NSLAB)), add=True)

        plsc.subcore_barrier()
        pltpu.sync_copy(tuple(accs[q].at[pl.ds(s * share, share)] for q in range(NSLAB)),
                        tuple(out_hbm.at[pl.ds(row0, share), pl.ds(q * SLAB, SLAB)] for q in range(NSLAB)))


def _combine(y, idx, x, out_rows=S):
    return pl.kernel(
        _combine_body, mesh=_sc_mesh(),
        out_type=jax.ShapeDtypeStruct((out_rows, D), F32),
        scratch_types=[pltpu.VMEM((CAP,), jnp.int32), pltpu.VMEM((COMBINE_ROWS,), jnp.int32)]
        + [pltpu.VMEM((COMBINE_ROWS, SLAB), F32) for _ in range(NSLAB)]
        + [pltpu.VMEM_SHARED((COMBINE_RANGE + 16, SLAB), F32) for _ in range(NSLAB)],
        compiler_params=pltpu.CompilerParams(needs_layout_passes=False),
        name="moe_combine",
    )(y, idx, x)


def _ffn_kernel(x_ref, wg_ref, wu_ref, wd_ref, gate_ref, mod_ref, o_ref):
    f = pl.program_id(1)
    last = pl.num_programs(1) - 1
    nt = FFN_TF // FFN_WT
    wg = [wg_ref[0, 0, :, j * FFN_WT:(j + 1) * FFN_WT].astype(BF16) for j in range(nt)]
    wu = [wu_ref[0, 0, :, j * FFN_WT:(j + 1) * FFN_WT].astype(BF16) for j in range(nt)]
    wd = wd_ref[0, 0].astype(BF16)
    is_first = f == 0
    is_last = f == last
    g2 = jnp.where(is_last, mod_ref[0, 0][5:6], 1.0)
    g_row = gate_ref[pl.ds(pl.program_id(0), 1), :]
    g_col = jnp.broadcast_to(g_row, (128, CAP)).T[:, 0:1]
    for r in range(CAP // FFN_RT):
        rows = slice(r * FFN_RT, (r + 1) * FFN_RT)
        xr = _unpack_bf16_pairs(x_ref[0, rows, :])
        acts = []
        for j in range(nt):
            g = jnp.dot(xr, wg[j], preferred_element_type=F32)
            u = jnp.dot(xr, wu[j], preferred_element_type=F32)
            acts.append((g * _sigmoid(g) * u).astype(BF16))
        y = jnp.dot(jnp.concatenate(acts, axis=1), wd, preferred_element_type=F32)
        prev = jnp.where(is_first, 0.0, o_ref[0, rows, :])
        gate = jnp.where(is_last, g_col[rows, :], 1.0)
        o_ref[0, rows, :] = (prev + y) * gate * g2


def _select(aff, tri):
    return pl.pallas_call(
        _select_kernel,
        grid=(1,),
        in_specs=[
            pl.BlockSpec((E, S), lambda i: (0, 0)),
            pl.BlockSpec((128, 128), lambda i: (0, 0)),
        ],
        out_specs=pl.BlockSpec((E, S), lambda i: (0, 0)),
        out_shape=jax.ShapeDtypeStruct((E, S), jnp.int32),
        compiler_params=_cparams(("arbitrary",)),
        name="moe_select",
    )(aff, tri)


def _ffn(xin, gate, b, mod_all, w_gate, w_up, w_down, layer):
    return pl.pallas_call(
        _ffn_kernel,
        grid=(E, F // FFN_TF),
        in_specs=[
            pl.BlockSpec((1, CAP, D // 2), lambda e, f: (e, 0, 0)),
            pl.BlockSpec((1, 1, D, FFN_TF), lambda e, f: (layer, e, 0, f)),
            pl.BlockSpec((1, 1, D, FFN_TF), lambda e, f: (layer, e, 0, f)),
            pl.BlockSpec((1, 1, FFN_TF, D), lambda e, f: (layer, e, f, 0)),
            pl.BlockSpec((E, CAP), lambda e, f: (0, 0)),
            pl.BlockSpec((1, 1, 6, D), lambda e, f: (layer, b, 0, 0)),
        ],
        out_specs=pl.BlockSpec((1, CAP, D), lambda e, f: (e, 0, 0)),
        out_shape=jax.ShapeDtypeStruct((E, CAP, D), F32),
        compiler_params=_cparams(("arbitrary", "arbitrary"), 48),
        name="moe_ffn",
    )(xin, w_gate, w_up, w_down, gate, mod_all)


def kernel(x, c, w_ada, b_ada, norm_mix, norm_ffn, w_fourier_out, w_qkv, w_attn_out, q_gain, k_gain,
           sink, rel_bias, w_router, w_gate, w_up, w_down):
    m0, m2 = _dft_tables()
    m1 = _stage1_table()
    bucket_t = _bucket_table()
    tri = jnp.asarray(np.triu(np.ones((128, 128), np.float32)), BF16)
    mod_all = _ada(c, w_ada, b_ada)
    bias_tab = _bias_table(rel_bias, bucket_t)
    gain_mix = norm_mix.reshape(DEPTH, 1, D)
    gain_ffn = norm_ffn.reshape(DEPTH, 1, D)
    xs = [x, x]
    for layer in range(DEPTH):
        j = layer // 2
        wr = jnp.pad(w_router[layer], ((0, 0), (0, EP - E)))
        wr1 = wr.astype(BF16)
        wr = jnp.concatenate([wr1, (wr - wr1.astype(F32)).astype(BF16)], axis=1)
        if layer % 2 == 0:
            w_out_bf = w_fourier_out[j].astype(BF16)
            routed = [_fourier_layer(xs[b], b, mod_all, gain_mix, gain_ffn, wr, w_out_bf, layer, m0, m1, m2)
                      for b in range(B)]
        else:
            aw = _attn_weights(w_qkv[j], q_gain[j], k_gain[j], sink[j])
            w_out_bf = w_attn_out[j].astype(BF16)
            routed = [_attn_layer(xs[b], b, mod_all, gain_mix, gain_ffn, wr, aw, w_out_bf, bias_tab, layer)
                      for b in range(B)]
        xs = [routed[b][0] for b in range(B)]
        pos = [_select(routed[b][2], tri) for b in range(B)]
        disp = [_dispatch(pos[b], routed[b][2], routed[b][1]) for b in range(B)]
        ys = [_ffn(disp[b][0], disp[b][2], b, mod_all, w_gate, w_up, w_down, layer) for b in range(B)]
        if layer < DEPTH - 1:
            xs = [_combine(ys[b], disp[b][1], xs[b]) for b in range(B)]
    out = _combine(ys[0], disp[0][1], xs[0], out_rows=B * S)
    for b in range(1, B):
        out = lax.dynamic_update_slice(out, _combine(ys[b], disp[b][1], xs[b]), (b * S, 0))
    return out.reshape(B, S, D)
```

```python
import math

import numpy as np
import jax
import jax.numpy as jnp
from jax import lax
from jax.experimental import pallas as pl
from jax.experimental.pallas import tpu as pltpu
from jax.experimental.pallas import tpu_sc as plsc

D = 1024
B = 2
S = 8192
DEPTH = 4
GROUPS = 4
GD = D // GROUPS
HD = 64
NH = 16
NKV = 4
GQA = NH // NKV
WINDOW = 128
BLK = 128
NBUCKETS = 32
MAXDIST = 128
E = 16
CAP = 2 * S // E
F = 2 * D
EPS = 1e-6
NEG_INF = -1e30

N1 = 128
N2 = 64
INV_NORM = 1.0 / math.sqrt(S * GD)

F32 = jnp.float32
BF16 = jnp.bfloat16


def _cparams(sem, vmem_mb=48):
    return pltpu.CompilerParams(dimension_semantics=sem, vmem_limit_bytes=vmem_mb * 1024 * 1024)


def _dft_tables():
    c = np.arange(GD)
    ang0 = 2.0 * np.pi * ((c[:, None] * c[None, :]) % GD) / GD
    m0 = np.concatenate([np.cos(ang0), -np.sin(ang0)], axis=1)
    k2 = np.arange(N2)
    ang2 = 2.0 * np.pi * ((k2[:, None] * k2[None, :]) % N2) / N2
    m2 = np.stack([np.cos(ang2), np.sin(ang2)], axis=2).reshape(N2, 2 * N2)
    return jnp.asarray(m0, BF16), jnp.asarray(m2, BF16)


def _stage1_table():
    s2 = np.arange(N2)[:, None, None]
    k1 = np.arange(N1)[None, :, None]
    s1 = np.arange(N1)[None, None, :]
    th = ((k1 * (N2 * s1 + s2)) % S) * (2.0 * np.pi / S)
    co, si = np.cos(th), np.sin(th)
    top = np.concatenate([co, si], axis=2)
    bot = np.concatenate([-si, co], axis=2)
    return jnp.asarray(np.concatenate([top, bot], axis=1).astype(np.float32), BF16)


def _bucket_table():
    q_off = np.arange(BLK)
    k_off = np.arange(3 * BLK) - BLK
    rel = k_off[:, None] - q_off[None, :]
    half = NBUCKETS // 2
    max_exact = half // 2
    ret = np.where(rel > 0, half, 0)
    n = np.abs(rel)
    nf = np.maximum(n, 1).astype(np.float32)
    ratio = (np.log(nf / np.float32(max_exact)) / np.float32(math.log(MAXDIST / max_exact))).astype(np.float32)
    large = max_exact + (ratio * np.float32(half - max_exact)).astype(np.int32)
    large = np.minimum(large, half - 1)
    bucket = ret + np.where(n < max_exact, n, large)
    return jnp.asarray(np.where(np.abs(rel) <= WINDOW, bucket, -1).astype(np.int32))


def _modulate(x, gain, shift, scale):
    ms = jnp.mean(x * x, axis=-1, keepdims=True)
    return x * lax.rsqrt(ms + EPS) * (gain * (1.0 + scale)) + shift


def _sigmoid(x):
    return 1.0 / (1.0 + jnp.exp(-x))


ADA_TN = 768


def _ada_kernel(ct_ref, w_ref, b_ref, o_ref):
    ct = ct_ref[...]
    ca = ct * _sigmoid(ct)
    w = w_ref[0]
    for b in range(B):
        o_ref[0, b:b + 1, :] = jnp.sum(w * ca[:, b:b + 1], axis=0, keepdims=True) + b_ref[0]


def _ada(c, w_ada, b_ada):
    out = pl.pallas_call(
        _ada_kernel,
        grid=(DEPTH, 6 * D // ADA_TN),
        in_specs=[
            pl.BlockSpec((D, B), lambda l, j: (0, 0)),
            pl.BlockSpec((1, D, ADA_TN), lambda l, j: (l, 0, j)),
            pl.BlockSpec((1, 1, ADA_TN), lambda l, j: (l, 0, j)),
        ],
        out_specs=pl.BlockSpec((1, B, ADA_TN), lambda l, j: (l, 0, j)),
        out_shape=jax.ShapeDtypeStruct((DEPTH, B, 6 * D), F32),
        compiler_params=_cparams(("arbitrary", "arbitrary"), 32),
        name="ada_mod",
    )(c.T, w_ada, b_ada.reshape(DEPTH, 1, 6 * D))
    return out.reshape(DEPTH, B, 6, D)


def _mod_spec(layer, b):
    return pl.BlockSpec((1, 1, 6, D), lambda *_: (layer, b, 0, 0))


def _gain_spec(layer):
    return pl.BlockSpec((1, 1, D), lambda *_: (layer, 0, 0))


def _x_spec(x, b, tm):
    if x.ndim == 3:
        return pl.BlockSpec((None, tm, D), lambda i: (b, i, 0))
    return pl.BlockSpec((tm, D), lambda i: (i, 0))


EP = 128

_ROUTED_SHAPES = [jax.ShapeDtypeStruct((S, D), F32),
                  jax.ShapeDtypeStruct((S, D // 2), jnp.int32),
                  jax.ShapeDtypeStruct((E, S), F32)]


def _routed_specs(tm):
    return [pl.BlockSpec((tm, D), lambda i: (i, 0)),
            pl.BlockSpec((tm, D // 2), lambda i: (i, 0)),
            pl.BlockSpec((E, tm), lambda i: (0, i))]


F0_TM = 1024
SUB = 8


def _pack_bf16_pairs(h):
    hb = h.astype(BF16).astype(F32)
    lo = pltpu.bitcast(hb[:, :D // 2], jnp.int32)
    hi = pltpu.bitcast(hb[:, D // 2:], jnp.int32)
    return hi | lax.shift_right_logical(lo, jnp.int32(16))


def _unpack_bf16_pairs(xp):
    lo = pltpu.bitcast(xp << 16, F32).astype(BF16)
    hi = pltpu.bitcast(xp & jnp.int32(-65536), F32).astype(BF16)
    return jnp.concatenate([lo, hi], axis=1)


def _route(xn, m, gain_ffn, wr):
    h = _modulate(xn, gain_ffn, m[3:4], m[4:5])
    h1 = h.astype(BF16)
    h2 = (h - h1.astype(F32)).astype(BF16)
    part = jnp.dot(h1, wr, preferred_element_type=F32)
    logits = part[:, :EP] + part[:, EP:] + jnp.dot(h2, wr[:, :EP], preferred_element_type=F32)
    lt = logits.T[:E]
    ex = jnp.exp(lt - jnp.max(lt, axis=0, keepdims=True))
    return _pack_bf16_pairs(h), ex / jnp.sum(ex, axis=0, keepdims=True)


def _f0_kernel(x_ref, mod_ref, gain_ref, m0_ref, o_ref):
    m = mod_ref[0, 0]
    h = _modulate(x_ref[...], gain_ref[0], m[0:1], m[1:2]).astype(BF16)
    m0 = m0_ref[...]
    r = [jnp.dot(h[:, g * GD:(g + 1) * GD], m0, preferred_element_type=F32) for g in range(GROUPS)]
    o_ref[0] = _pack_bf16_pairs(jnp.concatenate([rg[:, :GD] for rg in r], axis=1))
    o_ref[1] = _pack_bf16_pairs(jnp.concatenate([rg[:, GD:] for rg in r], axis=1))


def _f1_kernel(w_ref, m1_ref, o_ref, scr_ref):
    for j in range(SUB):
        scr_ref[j] = w_ref[:, j, :]
    for j in range(SUB):
        w = _unpack_bf16_pairs(scr_ref[j])
        o_ref[j] = _pack_bf16_pairs(jnp.dot(m1_ref[j], w, preferred_element_type=F32))


def _f2_kernel(u_ref, m2_ref, o_ref, scr_ref):
    m2 = m2_ref[...]
    for k in range(SUB):
        scr_ref[k] = u_ref[:, k, :]
    for k in range(SUB):
        u = _unpack_bf16_pairs(scr_ref[k])
        o_ref[k] = _pack_bf16_pairs(jnp.dot(m2, u, preferred_element_type=F32))


def _f3_kernel(mp_ref, x_ref, w_ref, mod_ref, gf_ref, wr_ref, o_ref, h_ref, aff_ref, scr_ref):
    g1 = mod_ref[0, 0][2:3]
    for j in range(SUB):
        scr_ref[j * N1:(j + 1) * N1, :] = mp_ref[:, j, :]
    a = _unpack_bf16_pairs(scr_ref[...])
    y = jnp.dot(a, w_ref[...], preferred_element_type=F32)
    xn = x_ref[...] + (g1 * INV_NORM) * y
    o_ref[...] = xn
    h_ref[...], aff_ref[...] = _route(xn, mod_ref[0, 0], gf_ref[0], wr_ref[...])


def _fourier_layer(x, b, mod_all, gain, gain_ffn, wr, w_out_bf, layer, m0, m1, m2):
    wc = pl.pallas_call(
        _f0_kernel,
        grid=(S // F0_TM,),
        in_specs=[
            _x_spec(x, b, F0_TM),
            _mod_spec(layer, b), _gain_spec(layer),
            pl.BlockSpec((GD, 2 * GD), lambda i: (0, 0)),
        ],
        out_specs=pl.BlockSpec((2, F0_TM, D // 2), lambda i: (0, i, 0)),
        out_shape=jax.ShapeDtypeStruct((2, S, D // 2), jnp.int32),
        compiler_params=_cparams(("arbitrary",)),
        name="fourier_chan",
    )(x, mod_all, gain, m0)
    wc = wc.reshape(2 * N1, N2, D // 2)
    u = pl.pallas_call(
        _f1_kernel,
        grid=(N2 // SUB,),
        in_specs=[
            pl.BlockSpec((2 * N1, SUB, D // 2), lambda i: (0, i, 0)),
            pl.BlockSpec((SUB, 2 * N1, 2 * N1), lambda i: (i, 0, 0)),
        ],
        out_specs=pl.BlockSpec((SUB, 2 * N1, D // 2), lambda i: (i, 0, 0)),
        out_shape=jax.ShapeDtypeStruct((N2, 2 * N1, D // 2), jnp.int32),
        scratch_shapes=[pltpu.VMEM((SUB, 2 * N1, D // 2), jnp.int32)],
        compiler_params=_cparams(("arbitrary",)),
        name="fourier_seq1",
    )(wc, m1)
    u = u.reshape(2 * N2, N1, D // 2)
    mp = pl.pallas_call(
        _f2_kernel,
        grid=(N1 // SUB,),
        in_specs=[
            pl.BlockSpec((2 * N2, SUB, D // 2), lambda i: (0, i, 0)),
            pl.BlockSpec((N2, 2 * N2), lambda i: (0, 0)),
        ],
        out_specs=pl.BlockSpec((SUB, N2, D // 2), lambda i: (i, 0, 0)),
        out_shape=jax.ShapeDtypeStruct((N1, N2, D // 2), jnp.int32),
        scratch_shapes=[pltpu.VMEM((SUB, 2 * N2, D // 2), jnp.int32)],
        compiler_params=_cparams(("arbitrary",)),
        name="fourier_seq2",
    )(u, m2)
    return pl.pallas_call(
        _f3_kernel,
        grid=(N2 // SUB,),
        in_specs=[
            pl.BlockSpec((N1, SUB, D // 2), lambda i: (0, i, 0)),
            _x_spec(x, b, SUB * N1),
            pl.BlockSpec((D, D), lambda i: (0, 0)),
            _mod_spec(layer, b), _gain_spec(layer),
            pl.BlockSpec((D, 2 * EP), lambda i: (0, 0)),
        ],
        out_specs=_routed_specs(SUB * N1),
        out_shape=_ROUTED_SHAPES,
        scratch_shapes=[pltpu.VMEM((SUB * N1, D // 2), jnp.int32)],
        compiler_params=_cparams(("arbitrary",)),
        name="fourier_out",
    )(mp, x, w_out_bf, mod_all, gain_ffn, wr)


QKV_TM = 512
TQ = 512
NSB = TQ // BLK
KWIN = TQ + 2 * BLK
NBLK = S // BLK
KPAD = 128
VROWS = HD + 16
ATT_AHEAD = 1
LOG2E = math.log2(math.e)


def _bias_kernel(rb_ref, bucket_ref, o_ref):
    h = pl.program_id(0)
    bucket = bucket_ref[...]
    acc = jnp.full(bucket.shape, NEG_INF, F32)
    for k in range(NBUCKETS):
        acc = jnp.where(bucket == k, rb_ref[k, h] * LOG2E, acc)
    o_ref[0] = acc


def _bias_table(rel_bias, bucket_t):
    return pl.pallas_call(
        _bias_kernel,
        grid=(NH,),
        in_specs=[
            pl.BlockSpec(memory_space=pltpu.SMEM),
            pl.BlockSpec((3 * BLK, BLK), lambda h: (0, 0)),
        ],
        out_specs=pl.BlockSpec((1, 3 * BLK, BLK), lambda h: (h // GQA, 0, h % GQA)),
        out_shape=jax.ShapeDtypeStruct((NKV, 3 * BLK, GQA * BLK), F32),
        compiler_params=_cparams(("arbitrary",)),
        name="rel_bias_table",
    )(rel_bias, bucket_t)


def _qkv_kernel(x_ref, mod_ref, gain_ref, wqt_ref, wk_ref, wvt_ref, qg_ref, kg_ref, qt_ref, k_ref, vt_ref):
    m = mod_ref[0, 0]
    h = _modulate(x_ref[...], gain_ref[0], m[0:1], m[1:2]).astype(BF16)
    nt = (((1,), (1,)), ((), ()))
    qt = lax.dot_general(wqt_ref[...], h, nt, preferred_element_type=F32)
    tm = qt.shape[1]
    q3 = qt.reshape(NH, HD, tm)
    q3 = q3 * lax.rsqrt(jnp.mean(q3 * q3, axis=1, keepdims=True) + EPS)
    qt_ref[...] = (q3.reshape(NH * HD, tm) * qg_ref[...]).astype(BF16)
    k = jnp.dot(h, wk_ref[...], preferred_element_type=F32)
    for g in range(NKV):
        kg = k[:, g * KPAD:(g + 1) * KPAD]
        ms = jnp.sum(kg * kg, axis=-1, keepdims=True) * (1.0 / HD)
        k_ref[:, g * KPAD:(g + 1) * KPAD] = (kg * lax.rsqrt(ms + EPS) * kg_ref[...]).astype(BF16)
    vt_ref[...] = lax.dot_general(wvt_ref[...], h, nt, preferred_element_type=F32).astype(BF16)


def _attn_kernel(qt_ref, kp_ref, kc_ref, kn_ref, vp_ref, vc_ref, vn_ref, x_ref, w_ref, mod_ref,
                 bias_ref, sink_ref, gf_ref, wr_ref, o_ref, h_ref, aff_ref, att_ref, s_ref):
    i = pl.program_id(0)
    kwin = jnp.concatenate([kp_ref[...], kc_ref[...], kn_ref[...]], axis=0)
    vwin = jnp.concatenate([vp_ref[...], vc_ref[...], vn_ref[...]], axis=1)
    ones_rows = (lax.broadcasted_iota(jnp.int32, (VROWS - HD, KWIN), 0) == 0).astype(BF16)
    vaug = [jnp.concatenate([vwin[g * HD:(g + 1) * HD], ones_rows], axis=0) for g in range(NKV)]
    key_pos = i * TQ - BLK + lax.broadcasted_iota(jnp.int32, (KWIN, 1), 0)
    key_mask = jnp.where((key_pos >= 0) & (key_pos < S), 0.0, NEG_INF).astype(BF16)
    lane = lax.broadcasted_iota(jnp.int32, (1, NKV * KPAD), 1)
    kwin = jnp.where(lane % KPAD == HD, key_mask, kwin)
    q_ones = (lax.broadcasted_iota(jnp.int32, (KPAD - HD, GQA * BLK), 0) == 0).astype(BF16)
    items = [(g, r) for g in range(NKV) for r in range(NSB)]

    def scores(g, r):
        kr = kwin[r * BLK:r * BLK + 3 * BLK, g * KPAD:(g + 1) * KPAD]
        qg = jnp.concatenate(
            [qt_ref[(GQA * g + hh) * HD:(GQA * g + hh + 1) * HD, r * BLK:(r + 1) * BLK] for hh in range(GQA)],
            axis=1)
        qa = jnp.concatenate([qg, q_ones], axis=0)
        return jnp.dot(kr, qa, preferred_element_type=F32)

    def probs(s, g, r):
        sink = sink_ref[g] * LOG2E
        s = s + bias_ref[g]
        mx = jnp.maximum(jnp.max(s, axis=0, keepdims=True), sink)
        return jnp.exp2(s - mx).astype(BF16), jnp.exp2(sink - mx)

    nslot = ATT_AHEAD + 1
    for n in range(ATT_AHEAD):
        s_ref[n % nslot] = scores(*items[n])
    for n, (g, r) in enumerate(items):
        if n + ATT_AHEAD < len(items):
            s_ref[(n + ATT_AHEAD) % nslot] = scores(*items[n + ATT_AHEAD])
        p, psink = probs(s_ref[n % nslot], g, r)
        ot = jnp.dot(vaug[g][:, r * BLK:r * BLK + 3 * BLK], p, preferred_element_type=F32)
        ot = ot[:HD] * (1.0 / (ot[HD:HD + 1] + psink))
        for hh in range(GQA):
            hd0 = (GQA * g + hh) * HD
            att_ref[hd0:hd0 + HD, r * BLK:(r + 1) * BLK] = ot[:, hh * BLK:(hh + 1) * BLK]
    g1 = mod_ref[0, 0][2:3]
    att = att_ref[...].T.astype(BF16)
    y = jnp.dot(att, w_ref[...], preferred_element_type=F32)
    xn = x_ref[...] + g1 * y
    o_ref[...] = xn
    h_ref[...], aff_ref[...] = _route(xn, mod_ref[0, 0], gf_ref[0], wr_ref[...])


def _attn_weights(w_qkv, q_gain, k_gain, sink):
    wq, wk, wv = w_qkv[:, :NH * HD], w_qkv[:, NH * HD:(NH + NKV) * HD], w_qkv[:, (NH + NKV) * HD:]
    wqt = wq.T.astype(BF16)
    wvt = wv.T.astype(BF16)
    wk_pad = jnp.pad(wk.reshape(D, NKV, HD), ((0, 0), (0, 0), (0, KPAD - HD))).reshape(D, NKV * KPAD).astype(BF16)
    qg_col = jnp.tile(q_gain * (HD ** -0.5 * LOG2E), NH).reshape(NH * HD, 1)
    kg_row = jnp.pad(k_gain, (0, KPAD - HD)).reshape(1, KPAD)
    sink_row = jnp.repeat(sink, BLK).reshape(NKV, 1, GQA * BLK)
    return wqt, wk_pad, wvt, qg_col, kg_row, sink_row


def _attn_layer(x, b, mod_all, gain, gain_ffn, wr, aw, w_out_bf, bias_tab, layer):
    wqt, wk_pad, wvt, qg_col, kg_row, sink_row = aw
    qt, k, vt = pl.pallas_call(
        _qkv_kernel,
        grid=(S // QKV_TM,),
        in_specs=[
            _x_spec(x, b, QKV_TM),
            _mod_spec(layer, b), _gain_spec(layer),
            pl.BlockSpec((NH * HD, D), lambda i: (0, 0)),
            pl.BlockSpec((D, NKV * KPAD), lambda i: (0, 0)),
            pl.BlockSpec((NKV * HD, D), lambda i: (0, 0)),
            pl.BlockSpec((NH * HD, 1), lambda i: (0, 0)),
            pl.BlockSpec((1, KPAD), lambda i: (0, 0)),
        ],
        out_specs=[
            pl.BlockSpec((NH * HD, QKV_TM), lambda i: (0, i)),
            pl.BlockSpec((QKV_TM, NKV * KPAD), lambda i: (i, 0)),
            pl.BlockSpec((NKV * HD, QKV_TM), lambda i: (0, i)),
        ],
        out_shape=[
            jax.ShapeDtypeStruct((NH * HD, S), BF16),
            jax.ShapeDtypeStruct((S, NKV * KPAD), BF16),
            jax.ShapeDtypeStruct((NKV * HD, S), BF16),
        ],
        compiler_params=_cparams(("arbitrary",)),
        name="attn_qkv",
    )(x, mod_all, gain, wqt, wk_pad, wvt, qg_col, kg_row)

    kw = NKV * KPAD
    vw = NKV * HD
    kprev = pl.BlockSpec((BLK, kw), lambda i: (jnp.maximum(i * NSB - 1, 0), 0))
    kcur = pl.BlockSpec((TQ, kw), lambda i: (i, 0))
    knext = pl.BlockSpec((BLK, kw), lambda i: (jnp.minimum((i + 1) * NSB, NBLK - 1), 0))
    vprev = pl.BlockSpec((vw, BLK), lambda i: (0, jnp.maximum(i * NSB - 1, 0)))
    vcur = pl.BlockSpec((vw, TQ), lambda i: (0, i))
    vnext = pl.BlockSpec((vw, BLK), lambda i: (0, jnp.minimum((i + 1) * NSB, NBLK - 1)))
    return pl.pallas_call(
        _attn_kernel,
        grid=(S // TQ,),
        in_specs=[
            pl.BlockSpec((NH * HD, TQ), lambda i: (0, i)),
            kprev, kcur, knext, vprev, vcur, vnext,
            _x_spec(x, b, TQ),
            pl.BlockSpec((NH * HD, D), lambda i: (0, 0)),
            _mod_spec(layer, b),
            pl.BlockSpec((NKV, 3 * BLK, GQA * BLK), lambda i: (0, 0, 0)),
            pl.BlockSpec((NKV, 1, GQA * BLK), lambda i: (0, 0, 0)),
            _gain_spec(layer),
            pl.BlockSpec((D, 2 * EP), lambda i: (0, 0)),
        ],
        out_specs=_routed_specs(TQ),
        out_shape=_ROUTED_SHAPES,
        scratch_shapes=[pltpu.VMEM((NH * HD, TQ), F32), pltpu.VMEM((ATT_AHEAD + 1, 3 * BLK, GQA * BLK), F32)],
        compiler_params=_cparams(("arbitrary",)),
        name="attn_core",
    )(qt, k, k, k, vt, vt, vt, x, w_out_bf, mod_all, bias_tab, sink_row, gain_ffn, wr)


FFN_TF = 1024
FFN_RT = 512
FFN_WT = 256


def _select_kernel(aff_ref, tri_ref, pos_ref):
    aff = aff_ref[...]
    bits = pltpu.bitcast(aff, jnp.int32)
    rows = aff.shape[0]

    def count_ge(v):
        return jnp.sum((bits >= v).astype(jnp.int32), axis=1, keepdims=True)

    def body(t, cur):
        cand = cur | (jnp.int32(1) << (30 - t))
        return jnp.where(count_ge(cand) >= CAP, cand, cur)

    thr = lax.fori_loop(0, 31, body, jnp.zeros((rows, 1), jnp.int32))
    gt = bits > thr
    eq = bits == thr
    need = CAP - jnp.sum(gt.astype(jnp.int32), axis=1, keepdims=True)
    tri = tri_ref[...]

    def cumsum_excl(mask_f):
        off = jnp.zeros((rows, 1), F32)
        outs = []
        for j in range(S // 128):
            mj = mask_f[:, j * 128:(j + 1) * 128]
            loc = jnp.dot(mj.astype(BF16), tri, preferred_element_type=F32)
            outs.append(loc - mj + off)
            off = off + loc[:, 127:128]
        return jnp.concatenate(outs, axis=1)

    eq_rank = cumsum_excl(eq.astype(F32))
    sel = gt | (eq & (eq_rank < need.astype(F32)))
    pos = cumsum_excl(sel.astype(F32))
    pos_ref[...] = jnp.where(sel, pos.astype(jnp.int32), -1)


SC_CORES = 2
SC_SUBCORES = 16
SC_TILES = SC_CORES * SC_SUBCORES
DISPATCH_SLOTS = CAP * E // SC_TILES
DISPATCH_ROWS = 64
COMBINE_RANGE = 1024
COMBINE_ROWS = 32
SLAB = 128
NSLAB = D // SLAB


def _sc_mesh():
    return plsc.VectorSubcoreMesh(core_axis_name="c", subcore_axis_name="s",
                                  num_cores=SC_CORES, num_subcores=SC_SUBCORES)


def _dispatch_body(pos_hbm, aff_hbm, h_hbm, xin_hbm, idx_hbm, gate_hbm, pos_v, aff_v, idx_v, gate_v,
                   rows_a, rows_b, gsem_a, gsem_b, wsem_a, wsem_b):
    w = lax.axis_index("s") * SC_CORES + lax.axis_index("c")
    e = w // 2
    lo = (w % 2) * DISPATCH_SLOTS
    pltpu.sync_copy(pos_hbm.at[e], pos_v)
    pltpu.sync_copy(aff_hbm.at[e], aff_v)

    @pl.loop(0, S // 16)
    def _(i):
        p = pos_v[pl.ds(i * 16, 16)] - lo
        m = (p >= 0) & (p < DISPATCH_SLOTS)
        tok = lax.iota(jnp.int32, 16) + i * 16
        plsc.store_scatter(idx_v, [p], tok, mask=m)
        plsc.store_scatter(gate_v, [p], aff_v[pl.ds(i * 16, 16)], mask=m)

    pltpu.sync_copy(idx_v, idx_hbm.at[e, pl.ds(lo, DISPATCH_SLOTS)])
    pltpu.sync_copy(gate_v, gate_hbm.at[e, pl.ds(lo, DISPATCH_SLOTS)])

    bufs, gsems, wsems = (rows_a, rows_b), (gsem_a, gsem_b), (wsem_a, wsem_b)
    nchunk = DISPATCH_SLOTS // DISPATCH_ROWS

    def gather(j):
        return pltpu.async_copy(h_hbm.at[idx_v.at[pl.ds(j * DISPATCH_ROWS, DISPATCH_ROWS)]], bufs[j % 2], gsems[j % 2])

    pending_gather = gather(0)
    writes = [None, None]
    for j in range(nchunk):
        pending_gather.wait()
        writes[j % 2] = pltpu.async_copy(
            bufs[j % 2], xin_hbm.at[e, pl.ds(lo + j * DISPATCH_ROWS, DISPATCH_ROWS)], wsems[j % 2])
        if j + 1 < nchunk:
            if writes[(j + 1) % 2] is not None:
                writes[(j + 1) % 2].wait()
            pending_gather = gather(j + 1)
    writes[(nchunk - 2) % 2].wait()
    writes[(nchunk - 1) % 2].wait()


def _dispatch(pos, aff, h):
    return pl.kernel(
        _dispatch_body, mesh=_sc_mesh(),
        out_type=[jax.ShapeDtypeStruct((E, CAP, D // 2), jnp.int32),
                  jax.ShapeDtypeStruct((E, CAP), jnp.int32),
                  jax.ShapeDtypeStruct((E, CAP), F32)],
        scratch_types=[pltpu.VMEM((S,), jnp.int32), pltpu.VMEM((S,), F32),
                       pltpu.VMEM((DISPATCH_SLOTS,), jnp.int32), pltpu.VMEM((DISPATCH_SLOTS,), F32),
                       pltpu.VMEM((DISPATCH_ROWS, D // 2), jnp.int32), pltpu.VMEM((DISPATCH_ROWS, D // 2), jnp.int32),
                       pltpu.SemaphoreType.DMA, pltpu.SemaphoreType.DMA,
                       pltpu.SemaphoreType.DMA, pltpu.SemaphoreType.DMA],
        compiler_params=pltpu.CompilerParams(needs_layout_passes=False),
        name="moe_dispatch",
    )(pos, aff, h)


def _combine_body(y_hbm, idx_hbm, x_hbm, out_hbm, idx_v, li_v, *bufs):
    rows = bufs[:NSLAB]
    accs = bufs[NSLAB:]
    c = lax.axis_index("c")
    s = lax.axis_index("s")
    share = COMBINE_RANGE // SC_SUBCORES
    pltpu.sync_copy(idx_hbm.at[s], idx_v)
    lane = lax.iota(jnp.int32, 16)

    @pl.loop(0, S // COMBINE_RANGE // SC_CORES)
    def _(r):
        t0 = (r * SC_CORES + c) * COMBINE_RANGE
        row0 = t0 + s * share
        pltpu.sync_copy(tuple(x_hbm.at[pl.ds(row0, share), pl.ds(q * SLAB, SLAB)] for q in range(NSLAB)),
                        tuple(accs[q].at[pl.ds(s * share, share)] for q in range(NSLAB)))
        plsc.subcore_barrier()

        @pl.loop(0, CAP // COMBINE_ROWS)
        def _(j):
            hits = jnp.zeros((16,), jnp.int32)
            for v in range(COMBINE_ROWS // 16):
                t = idx_v[pl.ds(j * COMBINE_ROWS + v * 16, 16)] - t0
                ok = (t >= 0) & (t < COMBINE_RANGE)
                li_v[pl.ds(v * 16, 16)] = jnp.where(ok, t, COMBINE_RANGE + lane)
                hits = hits + plsc.all_reduce_population_count(ok)

            @pl.when(jnp.max(hits) > 0)
            def _():
                pltpu.sync_copy(
                    tuple(y_hbm.at[s, pl.ds(j * COMBINE_ROWS, COMBINE_ROWS), pl.ds(q * SLAB, SLAB)]
                          for q in range(NSLAB)),
                    tuple(rows))
                pltpu.sync_copy(tuple(rows), tuple(accs[q].at[li_v] for q in range(NSLAB)), add=True)

        plsc.subcore_barrier()
        pltpu.sync_copy(tuple(accs[q].at[pl.ds(s * share, share)] for q in range(NSLAB)),
                        tuple(out_hbm.at[pl.ds(row0, share), pl.ds(q * SLAB, SLAB)] for q in range(NSLAB)))


def _combine(y, idx, x, out_rows=S):
    return pl.kernel(
        _combine_body, mesh=_sc_mesh(),
        out_type=jax.ShapeDtypeStruct((out_rows, D), F32),
        scratch_types=[pltpu.VMEM((CAP,), jnp.int32), pltpu.VMEM((COMBINE_ROWS,), jnp.int32)]
        + [pltpu.VMEM((COMBINE_ROWS, SLAB), F32) for _ in range(NSLAB)]
        + [pltpu.VMEM_SHARED((COMBINE_RANGE + 16, SLAB), F32) for _ in range(NSLAB)],
        compiler_params=pltpu.CompilerParams(needs_layout_passes=False),
        name="moe_combine",
    )(y, idx, x)


def _ffn_kernel(x_ref, wg_ref, wu_ref, wd_ref, gate_ref, mod_ref, o_ref):
    f = pl.program_id(1)
    last = pl.num_programs(1) - 1
    nt = FFN_TF // FFN_WT
    wg = [wg_ref[0, 0, :, j * FFN_WT:(j + 1) * FFN_WT].astype(BF16) for j in range(nt)]
    wu = [wu_ref[0, 0, :, j * FFN_WT:(j + 1) * FFN_WT].astype(BF16) for j in range(nt)]
    wd = wd_ref[0, 0].astype(BF16)
    is_first = f == 0
    is_last = f == last
    g2 = jnp.where(is_last, mod_ref[0, 0][5:6], 1.0)
    g_row = gate_ref[pl.ds(pl.program_id(0), 1), :]
    g_col = jnp.broadcast_to(g_row, (128, CAP)).T[:, 0:1]
    for r in range(CAP // FFN_RT):
        rows = slice(r * FFN_RT, (r + 1) * FFN_RT)
        xr = _unpack_bf16_pairs(x_ref[0, rows, :])
        acts = []
        for j in range(nt):
            g = jnp.dot(xr, wg[j], preferred_element_type=F32)
            u = jnp.dot(xr, wu[j], preferred_element_type=F32)
            acts.append((g * _sigmoid(g) * u).astype(BF16))
        y = jnp.dot(jnp.concatenate(acts, axis=1), wd, preferred_element_type=F32)
        prev = jnp.where(is_first, 0.0, o_ref[0, rows, :])
        gate = jnp.where(is_last, g_col[rows, :], 1.0)
        o_ref[0, rows, :] = (prev + y) * gate * g2


def _select(aff, tri):
    return pl.pallas_call(
        _select_kernel,
        grid=(1,),
        in_specs=[
            pl.BlockSpec((E, S), lambda i: (0, 0)),
            pl.BlockSpec((128, 128), lambda i: (0, 0)),
        ],
        out_specs=pl.BlockSpec((E, S), lambda i: (0, 0)),
        out_shape=jax.ShapeDtypeStruct((E, S), jnp.int32),
        compiler_params=_cparams(("arbitrary",)),
        name="moe_select",
    )(aff, tri)


def _ffn(xin, gate, b, mod_all, w_gate, w_up, w_down, layer):
    return pl.pallas_call(
        _ffn_kernel,
        grid=(E, F // FFN_TF),
        in_specs=[
            pl.BlockSpec((1, CAP, D // 2), lambda e, f: (e, 0, 0)),
            pl.BlockSpec((1, 1, D, FFN_TF), lambda e, f: (layer, e, 0, f)),
            pl.BlockSpec((1, 1, D, FFN_TF), lambda e, f: (layer, e, 0, f)),
            pl.BlockSpec((1, 1, FFN_TF, D), lambda e, f: (layer, e, f, 0)),
            pl.BlockSpec((E, CAP), lambda e, f: (0, 0)),
            pl.BlockSpec((1, 1, 6, D), lambda e, f: (layer, b, 0, 0)),
        ],
        out_specs=pl.BlockSpec((1, CAP, D), lambda e, f: (e, 0, 0)),
        out_shape=jax.ShapeDtypeStruct((E, CAP, D), F32),
        compiler_params=_cparams(("arbitrary", "arbitrary"), 48),
        name="moe_ffn",
    )(xin, w_gate, w_up, w_down, gate, mod_all)


def kernel(x, c, w_ada, b_ada, norm_mix, norm_ffn, w_fourier_out, w_qkv, w_attn_out, q_gain, k_gain,
           sink, rel_bias, w_router, w_gate, w_up, w_down):
    m0, m2 = _dft_tables()
    m1 = _stage1_table()
    bucket_t = _bucket_table()
    tri = jnp.asarray(np.triu(np.ones((128, 128), np.float32)), BF16)
    mod_all = _ada(c, w_ada, b_ada)
    bias_tab = _bias_table(rel_bias, bucket_t)
    gain_mix = norm_mix.reshape(DEPTH, 1, D)
    gain_ffn = norm_ffn.reshape(DEPTH, 1, D)
    xs = [x, x]
    for layer in range(DEPTH):
        j = layer // 2
        wr = jnp.pad(w_router[layer], ((0, 0), (0, EP - E)))
        wr1 = wr.astype(BF16)
        wr = jnp.concatenate([wr1, (wr - wr1.astype(F32)).astype(BF16)], axis=1)
        if layer % 2 == 0:
            w_out_bf = w_fourier_out[j].astype(BF16)
            routed = [_fourier_layer(xs[b], b, mod_all, gain_mix, gain_ffn, wr, w_out_bf, layer, m0, m1, m2)
                      for b in range(B)]
        else:
            aw = _attn_weights(w_qkv[j], q_gain[j], k_gain[j], sink[j])
            w_out_bf = w_attn_out[j].astype(BF16)
            routed = [_attn_layer(xs[b], b, mod_all, gain_mix, gain_ffn, wr, aw, w_out_bf, bias_tab, layer)
                      for b in range(B)]
        xs = [routed[b][0] for b in range(B)]
        pos = [_select(routed[b][2], tri) for b in range(B)]
        disp = [_dispatch(pos[b], routed[b][2], routed[b][1]) for b in range(B)]
        ys = [_ffn(disp[b][0], disp[b][2], b, mod_all, w_gate, w_up, w_down, layer) for b in range(B)]
        if layer < DEPTH - 1:
            xs = [_combine(ys[b], disp[b][1], xs[b]) for b in range(B)]
    out = _combine(ys[0], disp[0][1], xs[0], out_rows=B * S)
    for b in range(1, B):
        out = lax.dynamic_update_slice(out, _combine(ys[b], disp[b][1], xs[b]), (b * S, 0))
    return out.reshape(B, S, D)
```

```python
import math

import numpy as np
import jax
import jax.numpy as jnp
from jax import lax
from jax.experimental import pallas as pl
from jax.experimental.pallas import tpu as pltpu
from jax.experimental.pallas import tpu_sc as plsc

D = 1024
B = 2
S = 8192
DEPTH = 4
GROUPS = 4
GD = D // GROUPS
HD = 64
NH = 16
NKV = 4
GQA = NH // NKV
WINDOW = 128
BLK = 128
NBUCKETS = 32
MAXDIST = 128
E = 16
CAP = 2 * S // E
F = 2 * D
EPS = 1e-6
NEG_INF = -1e30

N1 = 128
N2 = 64
INV_NORM = 1.0 / math.sqrt(S * GD)

F32 = jnp.float32
BF16 = jnp.bfloat16


def _cparams(sem, vmem_mb=48):
    return pltpu.CompilerParams(dimension_semantics=sem, vmem_limit_bytes=vmem_mb * 1024 * 1024)


def _dft_tables():
    c = np.arange(GD)
    ang0 = 2.0 * np.pi * ((c[:, None] * c[None, :]) % GD) / GD
    m0 = np.concatenate([np.cos(ang0), -np.sin(ang0)], axis=1)
    k2 = np.arange(N2)
    ang2 = 2.0 * np.pi * ((k2[:, None] * k2[None, :]) % N2) / N2
    m2 = np.stack([np.cos(ang2), np.sin(ang2)], axis=2).reshape(N2, 2 * N2)
    return jnp.asarray(m0, BF16), jnp.asarray(m2, BF16)


def _stage1_table():
    s2 = np.arange(N2)[:, None, None]
    k1 = np.arange(N1)[None, :, None]
    s1 = np.arange(N1)[None, None, :]
    th = ((k1 * (N2 * s1 + s2)) % S) * (2.0 * np.pi / S)
    co, si = np.cos(th), np.sin(th)
    top = np.concatenate([co, si], axis=2)
    bot = np.concatenate([-si, co], axis=2)
    return jnp.asarray(np.concatenate([top, bot], axis=1).astype(np.float32), BF16)


def _bucket_table():
    q_off = np.arange(BLK)
    k_off = np.arange(3 * BLK) - BLK
    rel = k_off[:, None] - q_off[None, :]
    half = NBUCKETS // 2
    max_exact = half // 2
    ret = np.where(rel > 0, half, 0)
    n = np.abs(rel)
    nf = np.maximum(n, 1).astype(np.float32)
    ratio = (np.log(nf / np.float32(max_exact)) / np.float32(math.log(MAXDIST / max_exact))).astype(np.float32)
    large = max_exact + (ratio * np.float32(half - max_exact)).astype(np.int32)
    large = np.minimum(large, half - 1)
    bucket = ret + np.where(n < max_exact, n, large)
    return jnp.asarray(np.where(np.abs(rel) <= WINDOW, bucket, -1).astype(np.int32))


def _modulate(x, gain, shift, scale):
    ms = jnp.mean(x * x, axis=-1, keepdims=True)
    return x * lax.rsqrt(ms + EPS) * (gain * (1.0 + scale)) + shift


def _sigmoid(x):
    return 1.0 / (1.0 + jnp.exp(-x))


ADA_TN = 768


def _ada_kernel(ct_ref, w_ref, b_ref, o_ref):
    ct = ct_ref[...]
    ca = ct * _sigmoid(ct)
    w = w_ref[0]
    for b in range(B):
        o_ref[0, b:b + 1, :] = jnp.sum(w * ca[:, b:b + 1], axis=0, keepdims=True) + b_ref[0]


def _ada(c, w_ada, b_ada):
    out = pl.pallas_call(
        _ada_kernel,
        grid=(DEPTH, 6 * D // ADA_TN),
        in_specs=[
            pl.BlockSpec((D, B), lambda l, j: (0, 0)),
            pl.BlockSpec((1, D, ADA_TN), lambda l, j: (l, 0, j)),
            pl.BlockSpec((1, 1, ADA_TN), lambda l, j: (l, 0, j)),
        ],
        out_specs=pl.BlockSpec((1, B, ADA_TN), lambda l, j: (l, 0, j)),
        out_shape=jax.ShapeDtypeStruct((DEPTH, B, 6 * D), F32),
        compiler_params=_cparams(("arbitrary", "arbitrary"), 32),
        name="ada_mod",
    )(c.T, w_ada, b_ada.reshape(DEPTH, 1, 6 * D))
    return out.reshape(DEPTH, B, 6, D)


def _mod_spec(layer, b):
    return pl.BlockSpec((1, 1, 6, D), lambda *_: (layer, b, 0, 0))


def _gain_spec(layer):
    return pl.BlockSpec((1, 1, D), lambda *_: (layer, 0, 0))


def _x_spec(x, b, tm):
    if x.ndim == 3:
        return pl.BlockSpec((None, tm, D), lambda i: (b, i, 0))
    return pl.BlockSpec((tm, D), lambda i: (i, 0))


EP = 128

_ROUTED_SHAPES = [jax.ShapeDtypeStruct((S, D), F32),
                  jax.ShapeDtypeStruct((S, D // 2), jnp.int32),
                  jax.ShapeDtypeStruct((E, S), F32)]


def _routed_specs(tm):
    return [pl.BlockSpec((tm, D), lambda i: (i, 0)),
            pl.BlockSpec((tm, D // 2), lambda i: (i, 0)),
            pl.BlockSpec((E, tm), lambda i: (0, i))]


F0_TM = 1024
SUB = 8
SEQ_PICKS = 2 * SUB


def _pack_bf16_pairs(h):
    hb = h.astype(BF16).astype(F32)
    lo = pltpu.bitcast(hb[:, :D // 2], jnp.int32)
    hi = pltpu.bitcast(hb[:, D // 2:], jnp.int32)
    return hi | lax.shift_right_logical(lo, jnp.int32(16))


def _unpack_bf16_pairs(xp):
    lo = pltpu.bitcast(xp << 16, F32).astype(BF16)
    hi = pltpu.bitcast(xp & jnp.int32(-65536), F32).astype(BF16)
    return jnp.concatenate([lo, hi], axis=1)


def _route(xn, m, gain_ffn, wr):
    h = _modulate(xn, gain_ffn, m[3:4], m[4:5])
    h1 = h.astype(BF16)
    h2 = (h - h1.astype(F32)).astype(BF16)
    part = jnp.dot(h1, wr, preferred_element_type=F32)
    logits = part[:, :EP] + part[:, EP:] + jnp.dot(h2, wr[:, :EP], preferred_element_type=F32)
    lt = logits.T[:E]
    ex = jnp.exp(lt - jnp.max(lt, axis=0, keepdims=True))
    return _pack_bf16_pairs(h), ex / jnp.sum(ex, axis=0, keepdims=True)


def _f0_kernel(x_ref, mod_ref, gain_ref, m0_ref, o_ref):
    m = mod_ref[0, 0]
    h = _modulate(x_ref[...], gain_ref[0], m[0:1], m[1:2]).astype(BF16)
    m0 = m0_ref[...]
    r = [jnp.dot(h[:, g * GD:(g + 1) * GD], m0, preferred_element_type=F32) for g in range(GROUPS)]
    o_ref[0] = _pack_bf16_pairs(jnp.concatenate([rg[:, :GD] for rg in r], axis=1))
    o_ref[1] = _pack_bf16_pairs(jnp.concatenate([rg[:, GD:] for rg in r], axis=1))


def _f1_kernel(w_ref, m1_ref, o_ref, scr_ref):
    n = o_ref.shape[0]
    for j in range(n):
        scr_ref[j] = w_ref[:, j, :]
    for j in range(n):
        w = _unpack_bf16_pairs(scr_ref[j])
        o_ref[j] = _pack_bf16_pairs(jnp.dot(m1_ref[j], w, preferred_element_type=F32))


def _f2_kernel(u_ref, m2_ref, o_ref, scr_ref):
    m2 = m2_ref[...]
    n = o_ref.shape[0]
    for k in range(n):
        scr_ref[k] = u_ref[:, k, :]
    for k in range(n):
        u = _unpack_bf16_pairs(scr_ref[k])
        o_ref[k] = _pack_bf16_pairs(jnp.dot(m2, u, preferred_element_type=F32))


def _f3_kernel(mp_ref, x_ref, w_ref, mod_ref, gf_ref, wr_ref, o_ref, h_ref, aff_ref, scr_ref):
    g1 = mod_ref[0, 0][2:3]
    for j in range(SUB):
        scr_ref[j * N1:(j + 1) * N1, :] = mp_ref[:, j, :]
    a = _unpack_bf16_pairs(scr_ref[...])
    y = jnp.dot(a, w_ref[...], preferred_element_type=F32)
    xn = x_ref[...] + (g1 * INV_NORM) * y
    o_ref[...] = xn
    h_ref[...], aff_ref[...] = _route(xn, mod_ref[0, 0], gf_ref[0], wr_ref[...])


def _fourier_layer(x, b, mod_all, gain, gain_ffn, wr, w_out_bf, layer, m0, m1, m2):
    wc = pl.pallas_call(
        _f0_kernel,
        grid=(S // F0_TM,),
        in_specs=[
            _x_spec(x, b, F0_TM),
            _mod_spec(layer, b), _gain_spec(layer),
            pl.BlockSpec((GD, 2 * GD), lambda i: (0, 0)),
        ],
        out_specs=pl.BlockSpec((2, F0_TM, D // 2), lambda i: (0, i, 0)),
        out_shape=jax.ShapeDtypeStruct((2, S, D // 2), jnp.int32),
        compiler_params=_cparams(("arbitrary",)),
        name="fourier_chan",
    )(x, mod_all, gain, m0)
    wc = wc.reshape(2 * N1, N2, D // 2)
    u = pl.pallas_call(
        _f1_kernel,
        grid=(N2 // SEQ_PICKS,),
        in_specs=[
            pl.BlockSpec((2 * N1, SEQ_PICKS, D // 2), lambda i: (0, i, 0)),
            pl.BlockSpec((SEQ_PICKS, 2 * N1, 2 * N1), lambda i: (i, 0, 0)),
        ],
        out_specs=pl.BlockSpec((SEQ_PICKS, 2 * N1, D // 2), lambda i: (i, 0, 0)),
        out_shape=jax.ShapeDtypeStruct((N2, 2 * N1, D // 2), jnp.int32),
        scratch_shapes=[pltpu.VMEM((SEQ_PICKS, 2 * N1, D // 2), jnp.int32)],
        compiler_params=_cparams(("arbitrary",)),
        name="fourier_seq1",
    )(wc, m1)
    u = u.reshape(2 * N2, N1, D // 2)
    mp = pl.pallas_call(
        _f2_kernel,
        grid=(N1 // SEQ_PICKS,),
        in_specs=[
            pl.BlockSpec((2 * N2, SEQ_PICKS, D // 2), lambda i: (0, i, 0)),
            pl.BlockSpec((N2, 2 * N2), lambda i: (0, 0)),
        ],
        out_specs=pl.BlockSpec((SEQ_PICKS, N2, D // 2), lambda i: (i, 0, 0)),
        out_shape=jax.ShapeDtypeStruct((N1, N2, D // 2), jnp.int32),
        scratch_shapes=[pltpu.VMEM((SEQ_PICKS, 2 * N2, D // 2), jnp.int32)],
        compiler_params=_cparams(("arbitrary",)),
        name="fourier_seq2",
    )(u, m2)
    return pl.pallas_call(
        _f3_kernel,
        grid=(N2 // SUB,),
        in_specs=[
            pl.BlockSpec((N1, SUB, D // 2), lambda i: (0, i, 0)),
            _x_spec(x, b, SUB * N1),
            pl.BlockSpec((D, D), lambda i: (0, 0)),
            _mod_spec(layer, b), _gain_spec(layer),
            pl.BlockSpec((D, 2 * EP), lambda i: (0, 0)),
        ],
        out_specs=_routed_specs(SUB * N1),
        out_shape=_ROUTED_SHAPES,
        scratch_shapes=[pltpu.VMEM((SUB * N1, D // 2), jnp.int32)],
        compiler_params=_cparams(("arbitrary",)),
        name="fourier_out",
    )(mp, x, w_out_bf, mod_all, gain_ffn, wr)


QKV_TM = 1024
TQ = 512
NSB = TQ // BLK
KWIN = TQ + 2 * BLK
NBLK = S // BLK
KPAD = 128
VROWS = HD + 16
ATT_AHEAD = 1
LOG2E = math.log2(math.e)


def _bias_kernel(rb_ref, bucket_ref, o_ref):
    h = pl.program_id(0)
    bucket = bucket_ref[...]
    acc = jnp.full(bucket.shape, NEG_INF, F32)
    for k in range(NBUCKETS):
        acc = jnp.where(bucket == k, rb_ref[k, h] * LOG2E, acc)
    o_ref[0] = acc


def _bias_table(rel_bias, bucket_t):
    return pl.pallas_call(
        _bias_kernel,
        grid=(NH,),
        in_specs=[
            pl.BlockSpec(memory_space=pltpu.SMEM),
            pl.BlockSpec((3 * BLK, BLK), lambda h: (0, 0)),
        ],
        out_specs=pl.BlockSpec((1, 3 * BLK, BLK), lambda h: (h // GQA, 0, h % GQA)),
        out_shape=jax.ShapeDtypeStruct((NKV, 3 * BLK, GQA * BLK), F32),
        compiler_params=_cparams(("arbitrary",)),
        name="rel_bias_table",
    )(rel_bias, bucket_t)


def _qkv_kernel(x_ref, mod_ref, gain_ref, wqt_ref, wk_ref, wvt_ref, qg_ref, kg_ref, qt_ref, k_ref, vt_ref):
    m = mod_ref[0, 0]
    h = _modulate(x_ref[...], gain_ref[0], m[0:1], m[1:2]).astype(BF16)
    nt = (((1,), (1,)), ((), ()))
    qt = lax.dot_general(wqt_ref[...], h, nt, preferred_element_type=F32)
    tm = qt.shape[1]
    q3 = qt.reshape(NH, HD, tm)
    q3 = q3 * lax.rsqrt(jnp.mean(q3 * q3, axis=1, keepdims=True) + EPS)
    qt_ref[...] = (q3.reshape(NH * HD, tm) * qg_ref[...]).astype(BF16)
    k = jnp.dot(h, wk_ref[...], preferred_element_type=F32)
    for g in range(NKV):
        kg = k[:, g * KPAD:(g + 1) * KPAD]
        ms = jnp.sum(kg * kg, axis=-1, keepdims=True) * (1.0 / HD)
        k_ref[:, g * KPAD:(g + 1) * KPAD] = (kg * lax.rsqrt(ms + EPS) * kg_ref[...]).astype(BF16)
    vt_ref[...] = lax.dot_general(wvt_ref[...], h, nt, preferred_element_type=F32).astype(BF16)


def _attn_kernel(qt_ref, kp_ref, kc_ref, kn_ref, vp_ref, vc_ref, vn_ref, x_ref, w_ref, mod_ref,
                 bias_ref, sink_ref, gf_ref, wr_ref, o_ref, h_ref, aff_ref, att_ref, s_ref):
    i = pl.program_id(0)
    kwin = jnp.concatenate([kp_ref[...], kc_ref[...], kn_ref[...]], axis=0)
    vwin = jnp.concatenate([vp_ref[...], vc_ref[...], vn_ref[...]], axis=1)
    ones_rows = (lax.broadcasted_iota(jnp.int32, (VROWS - HD, KWIN), 0) == 0).astype(BF16)
    vaug = [jnp.concatenate([vwin[g * HD:(g + 1) * HD], ones_rows], axis=0) for g in range(NKV)]
    key_pos = i * TQ - BLK + lax.broadcasted_iota(jnp.int32, (KWIN, 1), 0)
    key_mask = jnp.where((key_pos >= 0) & (key_pos < S), 0.0, NEG_INF).astype(BF16)
    lane = lax.broadcasted_iota(jnp.int32, (1, NKV * KPAD), 1)
    kwin = jnp.where(lane % KPAD == HD, key_mask, kwin)
    q_ones = (lax.broadcasted_iota(jnp.int32, (KPAD - HD, GQA * BLK), 0) == 0).astype(BF16)
    items = [(g, r) for g in range(NKV) for r in range(NSB)]

    def scores(g, r):
        kr = kwin[r * BLK:r * BLK + 3 * BLK, g * KPAD:(g + 1) * KPAD]
        qg = jnp.concatenate(
            [qt_ref[(GQA * g + hh) * HD:(GQA * g + hh + 1) * HD, r * BLK:(r + 1) * BLK] for hh in range(GQA)],
            axis=1)
        qa = jnp.concatenate([qg, q_ones], axis=0)
        return jnp.dot(kr, qa, preferred_element_type=F32)

    def probs(s, g, r):
        sink = sink_ref[g] * LOG2E
        s = s + bias_ref[g]
        mx = jnp.maximum(jnp.max(s, axis=0, keepdims=True), sink)
        return jnp.exp2(s - mx).astype(BF16), jnp.exp2(sink - mx)

    nslot = ATT_AHEAD + 1
    for n in range(ATT_AHEAD):
        s_ref[n % nslot] = scores(*items[n])
    for n, (g, r) in enumerate(items):
        if n + ATT_AHEAD < len(items):
            s_ref[(n + ATT_AHEAD) % nslot] = scores(*items[n + ATT_AHEAD])
        p, psink = probs(s_ref[n % nslot], g, r)
        ot = jnp.dot(vaug[g][:, r * BLK:r * BLK + 3 * BLK], p, preferred_element_type=F32)
        ot = ot[:HD] * (1.0 / (ot[HD:HD + 1] + psink))
        for hh in range(GQA):
            hd0 = (GQA * g + hh) * HD
            att_ref[hd0:hd0 + HD, r * BLK:(r + 1) * BLK] = ot[:, hh * BLK:(hh + 1) * BLK]
    g1 = mod_ref[0, 0][2:3]
    att = att_ref[...].T.astype(BF16)
    y = jnp.dot(att, w_ref[...], preferred_element_type=F32)
    xn = x_ref[...] + g1 * y
    o_ref[...] = xn
    h_ref[...], aff_ref[...] = _route(xn, mod_ref[0, 0], gf_ref[0], wr_ref[...])


def _attn_weights(w_qkv, q_gain, k_gain, sink):
    wq, wk, wv = w_qkv[:, :NH * HD], w_qkv[:, NH * HD:(NH + NKV) * HD], w_qkv[:, (NH + NKV) * HD:]
    wqt = wq.T.astype(BF16)
    wvt = wv.T.astype(BF16)
    wk_pad = jnp.pad(wk.reshape(D, NKV, HD), ((0, 0), (0, 0), (0, KPAD - HD))).reshape(D, NKV * KPAD).astype(BF16)
    qg_col = jnp.tile(q_gain * (HD ** -0.5 * LOG2E), NH).reshape(NH * HD, 1)
    kg_row = jnp.pad(k_gain, (0, KPAD - HD)).reshape(1, KPAD)
    sink_row = jnp.repeat(sink, BLK).reshape(NKV, 1, GQA * BLK)
    return wqt, wk_pad, wvt, qg_col, kg_row, sink_row


def _attn_layer(x, b, mod_all, gain, gain_ffn, wr, aw, w_out_bf, bias_tab, layer):
    wqt, wk_pad, wvt, qg_col, kg_row, sink_row = aw
    qt, k, vt = pl.pallas_call(
        _qkv_kernel,
        grid=(S // QKV_TM,),
        in_specs=[
            _x_spec(x, b, QKV_TM),
            _mod_spec(layer, b), _gain_spec(layer),
            pl.BlockSpec((NH * HD, D), lambda i: (0, 0)),
            pl.BlockSpec((D, NKV * KPAD), lambda i: (0, 0)),
            pl.BlockSpec((NKV * HD, D), lambda i: (0, 0)),
            pl.BlockSpec((NH * HD, 1), lambda i: (0, 0)),
            pl.BlockSpec((1, KPAD), lambda i: (0, 0)),
        ],
        out_specs=[
            pl.BlockSpec((NH * HD, QKV_TM), lambda i: (0, i)),
            pl.BlockSpec((QKV_TM, NKV * KPAD), lambda i: (i, 0)),
            pl.BlockSpec((NKV * HD, QKV_TM), lambda i: (0, i)),
        ],
        out_shape=[
            jax.ShapeDtypeStruct((NH * HD, S), BF16),
            jax.ShapeDtypeStruct((S, NKV * KPAD), BF16),
            jax.ShapeDtypeStruct((NKV * HD, S), BF16),
        ],
        compiler_params=_cparams(("arbitrary",)),
        name="attn_qkv",
    )(x, mod_all, gain, wqt, wk_pad, wvt, qg_col, kg_row)

    kw = NKV * KPAD
    vw = NKV * HD
    kprev = pl.BlockSpec((BLK, kw), lambda i: (jnp.maximum(i * NSB - 1, 0), 0))
    kcur = pl.BlockSpec((TQ, kw), lambda i: (i, 0))
    knext = pl.BlockSpec((BLK, kw), lambda i: (jnp.minimum((i + 1) * NSB, NBLK - 1), 0))
    vprev = pl.BlockSpec((vw, BLK), lambda i: (0, jnp.maximum(i * NSB - 1, 0)))
    vcur = pl.BlockSpec((vw, TQ), lambda i: (0, i))
    vnext = pl.BlockSpec((vw, BLK), lambda i: (0, jnp.minimum((i + 1) * NSB, NBLK - 1)))
    return pl.pallas_call(
        _attn_kernel,
        grid=(S // TQ,),
        in_specs=[
            pl.BlockSpec((NH * HD, TQ), lambda i: (0, i)),
            kprev, kcur, knext, vprev, vcur, vnext,
            _x_spec(x, b, TQ),
            pl.BlockSpec((NH * HD, D), lambda i: (0, 0)),
            _mod_spec(layer, b),
            pl.BlockSpec((NKV, 3 * BLK, GQA * BLK), lambda i: (0, 0, 0)),
            pl.BlockSpec((NKV, 1, GQA * BLK), lambda i: (0, 0, 0)),
            _gain_spec(layer),
            pl.BlockSpec((D, 2 * EP), lambda i: (0, 0)),
        ],
        out_specs=_routed_specs(TQ),
        out_shape=_ROUTED_SHAPES,
        scratch_shapes=[pltpu.VMEM((NH * HD, TQ), F32), pltpu.VMEM((ATT_AHEAD + 1, 3 * BLK, GQA * BLK), F32)],
        compiler_params=_cparams(("arbitrary",)),
        name="attn_core",
    )(qt, k, k, k, vt, vt, vt, x, w_out_bf, mod_all, bias_tab, sink_row, gain_ffn, wr)


FFN_TF = 1024
FFN_RT = 512
FFN_WT = 256


def _select_kernel(aff_ref, tri_ref, pos_ref):
    aff = aff_ref[...]
    bits = pltpu.bitcast(aff, jnp.int32)
    rows = aff.shape[0]

    def count_ge(v):
        return jnp.sum((bits >= v).astype(jnp.int32), axis=1, keepdims=True)

    def body(t, cur):
        cand = cur | (jnp.int32(1) << (30 - t))
        return jnp.where(count_ge(cand) >= CAP, cand, cur)

    thr = lax.fori_loop(0, 31, body, jnp.zeros((rows, 1), jnp.int32))
    gt = bits > thr
    eq = bits == thr
    need = CAP - jnp.sum(gt.astype(jnp.int32), axis=1, keepdims=True)
    tri = tri_ref[...]

    def cumsum_excl(mask_f):
        off = jnp.zeros((rows, 1), F32)
        outs = []
        for j in range(S // 128):
            mj = mask_f[:, j * 128:(j + 1) * 128]
            loc = jnp.dot(mj.astype(BF16), tri, preferred_element_type=F32)
            outs.append(loc - mj + off)
            off = off + loc[:, 127:128]
        return jnp.concatenate(outs, axis=1)

    eq_rank = cumsum_excl(eq.astype(F32))
    sel = gt | (eq & (eq_rank < need.astype(F32)))
    pos = cumsum_excl(sel.astype(F32))
    pos_ref[...] = jnp.where(sel, pos.astype(jnp.int32), -1)


SC_CORES = 2
SC_SUBCORES = 16
SC_TILES = SC_CORES * SC_SUBCORES
DISPATCH_SLOTS = CAP * E // SC_TILES
DISPATCH_ROWS = 64
COMBINE_RANGE = 1024
COMBINE_ROWS = 32
SLAB = 128
NSLAB = D // SLAB


def _sc_mesh():
    return plsc.VectorSubcoreMesh(core_axis_name="c", subcore_axis_name="s",
                                  num_cores=SC_CORES, num_subcores=SC_SUBCORES)


def _dispatch_body(pos_hbm, aff_hbm, h_hbm, xin_hbm, idx_hbm, gate_hbm, pos_v, aff_v, idx_v, gate_v,
                   rows_a, rows_b, gsem_a, gsem_b, wsem_a, wsem_b):
    w = lax.axis_index("s") * SC_CORES + lax.axis_index("c")
    e = w // 2
    lo = (w % 2) * DISPATCH_SLOTS
    pltpu.sync_copy(pos_hbm.at[e], pos_v)
    pltpu.sync_copy(aff_hbm.at[e], aff_v)

    @pl.loop(0, S // 16)
    def _(i):
        p = pos_v[pl.ds(i * 16, 16)] - lo
        m = (p >= 0) & (p < DISPATCH_SLOTS)
        tok = lax.iota(jnp.int32, 16) + i * 16
        plsc.store_scatter(idx_v, [p], tok, mask=m)
        plsc.store_scatter(gate_v, [p], aff_v[pl.ds(i * 16, 16)], mask=m)

    pltpu.sync_copy(idx_v, idx_hbm.at[e, pl.ds(lo, DISPATCH_SLOTS)])
    pltpu.sync_copy(gate_v, gate_hbm.at[e, pl.ds(lo, DISPATCH_SLOTS)])

    bufs, gsems, wsems = (rows_a, rows_b), (gsem_a, gsem_b), (wsem_a, wsem_b)
    nchunk = DISPATCH_SLOTS // DISPATCH_ROWS

    def gather(j):
        return pltpu.async_copy(h_hbm.at[idx_v.at[pl.ds(j * DISPATCH_ROWS, DISPATCH_ROWS)]], bufs[j % 2], gsems[j % 2])

    pending_gather = gather(0)
    writes = [None, None]
    for j in range(nchunk):
        pending_gather.wait()
        writes[j % 2] = pltpu.async_copy(
            bufs[j % 2], xin_hbm.at[e, pl.ds(lo + j * DISPATCH_ROWS, DISPATCH_ROWS)], wsems[j % 2])
        if j + 1 < nchunk:
            if writes[(j + 1) % 2] is not None:
                writes[(j + 1) % 2].wait()
            pending_gather = gather(j + 1)
    writes[(nchunk - 2) % 2].wait()
    writes[(nchunk - 1) % 2].wait()


def _dispatch(pos, aff, h):
    return pl.kernel(
        _dispatch_body, mesh=_sc_mesh(),
        out_type=[jax.ShapeDtypeStruct((E, CAP, D // 2), jnp.int32),
                  jax.ShapeDtypeStruct((E, CAP), jnp.int32),
                  jax.ShapeDtypeStruct((E, CAP), F32)],
        scratch_types=[pltpu.VMEM((S,), jnp.int32), pltpu.VMEM((S,), F32),
                       pltpu.VMEM((DISPATCH_SLOTS,), jnp.int32), pltpu.VMEM((DISPATCH_SLOTS,), F32),
                       pltpu.VMEM((DISPATCH_ROWS, D // 2), jnp.int32), pltpu.VMEM((DISPATCH_ROWS, D // 2), jnp.int32),
                       pltpu.SemaphoreType.DMA, pltpu.SemaphoreType.DMA,
                       pltpu.SemaphoreType.DMA, pltpu.SemaphoreType.DMA],
        compiler_params=pltpu.CompilerParams(needs_layout_passes=False),
        name="moe_dispatch",
    )(pos, aff, h)


def _combine_body(y_hbm, idx_hbm, x_hbm, out_hbm, idx_v, li_v, *bufs):
    rows = bufs[:NSLAB]
    accs = bufs[NSLAB:]
    c = lax.axis_index("c")
    s = lax.axis_index("s")
    share = COMBINE_RANGE // SC_SUBCORES
    pltpu.sync_copy(idx_hbm.at[s], idx_v)
    lane = lax.iota(jnp.int32, 16)

    @pl.loop(0, S // COMBINE_RANGE // SC_CORES)
    def _(r):
        t0 = (r * SC_CORES + c) * COMBINE_RANGE
        row0 = t0 + s * share
        pltpu.sync_copy(tuple(x_hbm.at[pl.ds(row0, share), pl.ds(q * SLAB, SLAB)] for q in range(NSLAB)),
                        tuple(accs[q].at[pl.ds(s * share, share)] for q in range(NSLAB)))
        plsc.subcore_barrier()

        @pl.loop(0, CAP // COMBINE_ROWS)
        def _(j):
            hits = jnp.zeros((16,), jnp.int32)
            for v in range(COMBINE_ROWS // 16):
                t = idx_v[pl.ds(j * COMBINE_ROWS + v * 16, 16)] - t0
                ok = (t >= 0) & (t < COMBINE_RANGE)
                li_v[pl.ds(v * 16, 16)] = jnp.where(ok, t, COMBINE_RANGE + lane)
                hits = hits + plsc.all_reduce_population_count(ok)

            @pl.when(jnp.max(hits) > 0)
            def _():
                pltpu.sync_copy(
                    tuple(y_hbm.at[s, pl.ds(j * COMBINE_ROWS, COMBINE_ROWS), pl.ds(q * SLAB, SLAB)]
                          for q in range(NSLAB)),
                    tuple(rows))
                pltpu.sync_copy(tuple(rows), tuple(accs[q].at[li_v] for q in range(NSLAB)), add=True)

        plsc.subcore_barrier()
        pltpu.sync_copy(tuple(accs[q].at[pl.ds(s * share, share)] for q in range(NSLAB)),
                        tuple(out_hbm.at[pl.ds(row0, share), pl.ds(q * SLAB, SLAB)] for q in range(NSLAB)))


def _combine(y, idx, x, out_rows=S):
    return pl.kernel(
        _combine_body, mesh=_sc_mesh(),
        out_type=jax.ShapeDtypeStruct((out_rows, D), F32),
        scratch_types=[pltpu.VMEM((CAP,), jnp.int32), pltpu.VMEM((COMBINE_ROWS,), jnp.int32)]
        + [pltpu.VMEM((COMBINE_ROWS, SLAB), F32) for _ in range(NSLAB)]
        + [pltpu.VMEM_SHARED((COMBINE_RANGE + 16, SLAB), F32) for _ in range(NSLAB)],
        compiler_params=pltpu.CompilerParams(needs_layout_passes=False),
        name="moe_combine",
    )(y, idx, x)


def _ffn_kernel(x_ref, wg_ref, wu_ref, wd_ref, gate_ref, mod_ref, o_ref):
    f = pl.program_id(1)
    last = pl.num_programs(1) - 1
    nt = FFN_TF // FFN_WT
    wg = [wg_ref[0, 0, :, j * FFN_WT:(j + 1) * FFN_WT].astype(BF16) for j in range(nt)]
    wu = [wu_ref[0, 0, :, j * FFN_WT:(j + 1) * FFN_WT].astype(BF16) for j in range(nt)]
    wd = wd_ref[0, 0].astype(BF16)
    is_first = f == 0
    is_last = f == last
    g2 = jnp.where(is_last, mod_ref[0, 0][5:6], 1.0)
    g_row = gate_ref[pl.ds(pl.program_id(0), 1), :]
    g_col = jnp.broadcast_to(g_row, (128, CAP)).T[:, 0:1]
    for r in range(CAP // FFN_RT):
        rows = slice(r * FFN_RT, (r + 1) * FFN_RT)
        xr = _unpack_bf16_pairs(x_ref[0, rows, :])
        acts = []
        for j in range(nt):
            g = jnp.dot(xr, wg[j], preferred_element_type=F32)
            u = jnp.dot(xr, wu[j], preferred_element_type=F32)
            acts.append((g * _sigmoid(g) * u).astype(BF16))
        y = jnp.dot(jnp.concatenate(acts, axis=1), wd, preferred_element_type=F32)
        prev = jnp.where(is_first, 0.0, o_ref[0, rows, :])
        gate = jnp.where(is_last, g_col[rows, :], 1.0)
        o_ref[0, rows, :] = (prev + y) * gate * g2


def _select(aff, tri):
    return pl.pallas_call(
        _select_kernel,
        grid=(1,),
        in_specs=[
            pl.BlockSpec((E, S), lambda i: (0, 0)),
            pl.BlockSpec((128, 128), lambda i: (0, 0)),
        ],
        out_specs=pl.BlockSpec((E, S), lambda i: (0, 0)),
        out_shape=jax.ShapeDtypeStruct((E, S), jnp.int32),
        compiler_params=_cparams(("arbitrary",)),
        name="moe_select",
    )(aff, tri)


def _ffn(xin, gate, b, mod_all, w_gate, w_up, w_down, layer):
    return pl.pallas_call(
        _ffn_kernel,
        grid=(E, F // FFN_TF),
        in_specs=[
            pl.BlockSpec((1, CAP, D // 2), lambda e, f: (e, 0, 0)),
            pl.BlockSpec((1, 1, D, FFN_TF), lambda e, f: (layer, e, 0, f)),
            pl.BlockSpec((1, 1, D, FFN_TF), lambda e, f: (layer, e, 0, f)),
            pl.BlockSpec((1, 1, FFN_TF, D), lambda e, f: (layer, e, f, 0)),
            pl.BlockSpec((E, CAP), lambda e, f: (0, 0)),
            pl.BlockSpec((1, 1, 6, D), lambda e, f: (layer, b, 0, 0)),
        ],
        out_specs=pl.BlockSpec((1, CAP, D), lambda e, f: (e, 0, 0)),
        out_shape=jax.ShapeDtypeStruct((E, CAP, D), F32),
        compiler_params=_cparams(("arbitrary", "arbitrary"), 48),
        name="moe_ffn",
    )(xin, w_gate, w_up, w_down, gate, mod_all)


def kernel(x, c, w_ada, b_ada, norm_mix, norm_ffn, w_fourier_out, w_qkv, w_attn_out, q_gain, k_gain,
           sink, rel_bias, w_router, w_gate, w_up, w_down):
    m0, m2 = _dft_tables()
    m1 = _stage1_table()
    bucket_t = _bucket_table()
    tri = jnp.asarray(np.triu(np.ones((128, 128), np.float32)), BF16)
    mod_all = _ada(c, w_ada, b_ada)
    bias_tab = _bias_table(rel_bias, bucket_t)
    gain_mix = norm_mix.reshape(DEPTH, 1, D)
    gain_ffn = norm_ffn.reshape(DEPTH, 1, D)
    xs = [x, x]
    for layer in range(DEPTH):
        j = layer // 2
        wr = jnp.pad(w_router[layer], ((0, 0), (0, EP - E)))
        wr1 = wr.astype(BF16)
        wr = jnp.concatenate([wr1, (wr - wr1.astype(F32)).astype(BF16)], axis=1)
        if layer % 2 == 0:
            w_out_bf = w_fourier_out[j].astype(BF16)
            routed = [_fourier_layer(xs[b], b, mod_all, gain_mix, gain_ffn, wr, w_out_bf, layer, m0, m1, m2)
                      for b in range(B)]
        else:
            aw = _attn_weights(w_qkv[j], q_gain[j], k_gain[j], sink[j])
            w_out_bf = w_attn_out[j].astype(BF16)
            routed = [_attn_layer(xs[b], b, mod_all, gain_mix, gain_ffn, wr, aw, w_out_bf, bias_tab, layer)
                      for b in range(B)]
        xs = [routed[b][0] for b in range(B)]
        pos = [_select(routed[b][2], tri) for b in range(B)]
        disp = [_dispatch(pos[b], routed[b][2], routed[b][1]) for b in range(B)]
        ys = [_ffn(disp[b][0], disp[b][2], b, mod_all, w_gate, w_up, w_down, layer) for b in range(B)]
        if layer < DEPTH - 1:
            xs = [_combine(ys[b], disp[b][1], xs[b]) for b in range(B)]
    out = _combine(ys[0], disp[0][1], xs[0], out_rows=B * S)
    for b in range(1, B):
        out = lax.dynamic_update_slice(out, _combine(ys[b], disp[b][1], xs[b]), (b * S, 0))
    return out.reshape(B, S, D)
```

```python
import math

import numpy as np
import jax
import jax.numpy as jnp
from jax import lax
from jax.experimental import pallas as pl
from jax.experimental.pallas import tpu as pltpu
from jax.experimental.pallas import tpu_sc as plsc

D = 1024
B = 2
S = 8192
DEPTH = 4
GROUPS = 4
GD = D // GROUPS
HD = 64
NH = 16
NKV = 4
GQA = NH // NKV
WINDOW = 128
BLK = 128
NBUCKETS = 32
MAXDIST = 128
E = 16
CAP = 2 * S // E
F = 2 * D
EPS = 1e-6
NEG_INF = -1e30

N1 = 128
N2 = 64
INV_NORM = 1.0 / math.sqrt(S * GD)

F32 = jnp.float32
BF16 = jnp.bfloat16


def _cparams(sem, vmem_mb=48):
    return pltpu.CompilerParams(dimension_semantics=sem, vmem_limit_bytes=vmem_mb * 1024 * 1024)


def _dft_tables():
    c = np.arange(GD)
    ang0 = 2.0 * np.pi * ((c[:, None] * c[None, :]) % GD) / GD
    m0 = np.concatenate([np.cos(ang0), -np.sin(ang0)], axis=1)
    k2 = np.arange(N2)
    ang2 = 2.0 * np.pi * ((k2[:, None] * k2[None, :]) % N2) / N2
    m2 = np.stack([np.cos(ang2), np.sin(ang2)], axis=2).reshape(N2, 2 * N2)
    return jnp.asarray(m0, BF16), jnp.asarray(m2, BF16)


def _stage1_table():
    s2 = np.arange(N2)[:, None, None]
    k1 = np.arange(N1)[None, :, None]
    s1 = np.arange(N1)[None, None, :]
    th = ((k1 * (N2 * s1 + s2)) % S) * (2.0 * np.pi / S)
    co, si = np.cos(th), np.sin(th)
    top = np.concatenate([co, si], axis=2)
    bot = np.concatenate([-si, co], axis=2)
    return jnp.asarray(np.concatenate([top, bot], axis=1).astype(np.float32), BF16)


def _bucket_table():
    q_off = np.arange(BLK)
    k_off = np.arange(3 * BLK) - BLK
    rel = k_off[:, None] - q_off[None, :]
    half = NBUCKETS // 2
    max_exact = half // 2
    ret = np.where(rel > 0, half, 0)
    n = np.abs(rel)
    nf = np.maximum(n, 1).astype(np.float32)
    ratio = (np.log(nf / np.float32(max_exact)) / np.float32(math.log(MAXDIST / max_exact))).astype(np.float32)
    large = max_exact + (ratio * np.float32(half - max_exact)).astype(np.int32)
    large = np.minimum(large, half - 1)
    bucket = ret + np.where(n < max_exact, n, large)
    return jnp.asarray(np.where(np.abs(rel) <= WINDOW, bucket, -1).astype(np.int32))


def _modulate(x, gain, shift, scale):
    ms = jnp.mean(x * x, axis=-1, keepdims=True)
    return x * lax.rsqrt(ms + EPS) * (gain * (1.0 + scale)) + shift


def _sigmoid(x):
    return 1.0 / (1.0 + jnp.exp(-x))


ADA_TN = 1536


def _ada_kernel(ct_ref, w_ref, b_ref, o_ref):
    ct = ct_ref[...]
    ca = ct * _sigmoid(ct)
    w = w_ref[0]
    for b in range(B):
        o_ref[0, b:b + 1, :] = jnp.sum(w * ca[:, b:b + 1], axis=0, keepdims=True) + b_ref[0]


def _ada(c, w_ada, b_ada):
    out = pl.pallas_call(
        _ada_kernel,
        grid=(DEPTH, 6 * D // ADA_TN),
        in_specs=[
            pl.BlockSpec((D, B), lambda l, j: (0, 0)),
            pl.BlockSpec((1, D, ADA_TN), lambda l, j: (l, 0, j)),
            pl.BlockSpec((1, 1, ADA_TN), lambda l, j: (l, 0, j)),
        ],
        out_specs=pl.BlockSpec((1, B, ADA_TN), lambda l, j: (l, 0, j)),
        out_shape=jax.ShapeDtypeStruct((DEPTH, B, 6 * D), F32),
        compiler_params=_cparams(("arbitrary", "arbitrary"), 32),
        name="ada_mod",
    )(c.T, w_ada, b_ada.reshape(DEPTH, 1, 6 * D))
    return out.reshape(DEPTH, B, 6, D)


def _mod_spec(layer, b):
    return pl.BlockSpec((1, 1, 6, D), lambda *_: (layer, b, 0, 0))


def _gain_spec(layer):
    return pl.BlockSpec((1, 1, D), lambda *_: (layer, 0, 0))


def _x_spec(x, b, tm):
    if x.ndim == 3:
        return pl.BlockSpec((None, tm, D), lambda i: (b, i, 0))
    return pl.BlockSpec((tm, D), lambda i: (i, 0))


EP = 128

_ROUTED_SHAPES = [jax.ShapeDtypeStruct((S, D), F32),
                  jax.ShapeDtypeStruct((S, D // 2), jnp.int32),
                  jax.ShapeDtypeStruct((E, S), F32)]


def _routed_specs(tm):
    return [pl.BlockSpec((tm, D), lambda i: (i, 0)),
            pl.BlockSpec((tm, D // 2), lambda i: (i, 0)),
            pl.BlockSpec((E, tm), lambda i: (0, i))]


F0_TM = 1024
SUB = 8
SEQ_PICKS = 2 * SUB


def _pack_bf16_pairs(h):
    hb = h.astype(BF16).astype(F32)
    lo = pltpu.bitcast(hb[:, :D // 2], jnp.int32)
    hi = pltpu.bitcast(hb[:, D // 2:], jnp.int32)
    return hi | lax.shift_right_logical(lo, jnp.int32(16))


def _unpack_bf16_pairs(xp):
    lo = pltpu.bitcast(xp << 16, F32).astype(BF16)
    hi = pltpu.bitcast(xp & jnp.int32(-65536), F32).astype(BF16)
    return jnp.concatenate([lo, hi], axis=1)


def _route(xn, m, gain_ffn, wr):
    h = _modulate(xn, gain_ffn, m[3:4], m[4:5])
    h1 = h.astype(BF16)
    h2 = (h - h1.astype(F32)).astype(BF16)
    part = jnp.dot(h1, wr, preferred_element_type=F32)
    logits = part[:, :EP] + part[:, EP:] + jnp.dot(h2, wr[:, :EP], preferred_element_type=F32)
    lt = logits.T[:E]
    ex = jnp.exp(lt - jnp.max(lt, axis=0, keepdims=True))
    return _pack_bf16_pairs(h), ex / jnp.sum(ex, axis=0, keepdims=True)


def _f0_kernel(x_ref, mod_ref, gain_ref, m0_ref, o_ref):
    m = mod_ref[0, 0]
    h = _modulate(x_ref[...], gain_ref[0], m[0:1], m[1:2]).astype(BF16)
    m0 = m0_ref[...]
    r = [jnp.dot(h[:, g * GD:(g + 1) * GD], m0, preferred_element_type=F32) for g in range(GROUPS)]
    o_ref[0] = _pack_bf16_pairs(jnp.concatenate([rg[:, :GD] for rg in r], axis=1))
    o_ref[1] = _pack_bf16_pairs(jnp.concatenate([rg[:, GD:] for rg in r], axis=1))


def _f1_kernel(w_ref, m1_ref, o_ref, scr_ref):
    n = o_ref.shape[0]
    for j in range(n):
        scr_ref[j] = w_ref[:, j, :]
    for j in range(n):
        w = _unpack_bf16_pairs(scr_ref[j])
        o_ref[j] = _pack_bf16_pairs(jnp.dot(m1_ref[j], w, preferred_element_type=F32))


def _f2_kernel(u_ref, m2_ref, o_ref, scr_ref):
    m2 = m2_ref[...]
    n = o_ref.shape[0]
    for k in range(n):
        scr_ref[k] = u_ref[:, k, :]
    for k in range(n):
        u = _unpack_bf16_pairs(scr_ref[k])
        o_ref[k] = _pack_bf16_pairs(jnp.dot(m2, u, preferred_element_type=F32))


def _f3_kernel(mp_ref, x_ref, w_ref, mod_ref, gf_ref, wr_ref, o_ref, h_ref, aff_ref, scr_ref):
    g1 = mod_ref[0, 0][2:3]
    for j in range(SUB):
        scr_ref[j * N1:(j + 1) * N1, :] = mp_ref[:, j, :]
    a = _unpack_bf16_pairs(scr_ref[...])
    y = jnp.dot(a, w_ref[...], preferred_element_type=F32)
    xn = x_ref[...] + (g1 * INV_NORM) * y
    o_ref[...] = xn
    h_ref[...], aff_ref[...] = _route(xn, mod_ref[0, 0], gf_ref[0], wr_ref[...])


def _fourier_layer(x, b, mod_all, gain, gain_ffn, wr, w_out_bf, layer, m0, m1, m2):
    wc = pl.pallas_call(
        _f0_kernel,
        grid=(S // F0_TM,),
        in_specs=[
            _x_spec(x, b, F0_TM),
            _mod_spec(layer, b), _gain_spec(layer),
            pl.BlockSpec((GD, 2 * GD), lambda i: (0, 0)),
        ],
        out_specs=pl.BlockSpec((2, F0_TM, D // 2), lambda i: (0, i, 0)),
        out_shape=jax.ShapeDtypeStruct((2, S, D // 2), jnp.int32),
        compiler_params=_cparams(("arbitrary",)),
        name="fourier_chan",
    )(x, mod_all, gain, m0)
    wc = wc.reshape(2 * N1, N2, D // 2)
    u = pl.pallas_call(
        _f1_kernel,
        grid=(N2 // SEQ_PICKS,),
        in_specs=[
            pl.BlockSpec((2 * N1, SEQ_PICKS, D // 2), lambda i: (0, i, 0)),
            pl.BlockSpec((SEQ_PICKS, 2 * N1, 2 * N1), lambda i: (i, 0, 0)),
        ],
        out_specs=pl.BlockSpec((SEQ_PICKS, 2 * N1, D // 2), lambda i: (i, 0, 0)),
        out_shape=jax.ShapeDtypeStruct((N2, 2 * N1, D // 2), jnp.int32),
        scratch_shapes=[pltpu.VMEM((SEQ_PICKS, 2 * N1, D // 2), jnp.int32)],
        compiler_params=_cparams(("arbitrary",)),
        name="fourier_seq1",
    )(wc, m1)
    u = u.reshape(2 * N2, N1, D // 2)
    mp = pl.pallas_call(
        _f2_kernel,
        grid=(N1 // SEQ_PICKS,),
        in_specs=[
            pl.BlockSpec((2 * N2, SEQ_PICKS, D // 2), lambda i: (0, i, 0)),
            pl.BlockSpec((N2, 2 * N2), lambda i: (0, 0)),
        ],
        out_specs=pl.BlockSpec((SEQ_PICKS, N2, D // 2), lambda i: (i, 0, 0)),
        out_shape=jax.ShapeDtypeStruct((N1, N2, D // 2), jnp.int32),
        scratch_shapes=[pltpu.VMEM((SEQ_PICKS, 2 * N2, D // 2), jnp.int32)],
        compiler_params=_cparams(("arbitrary",)),
        name="fourier_seq2",
    )(u, m2)
    return pl.pallas_call(
        _f3_kernel,
        grid=(N2 // SUB,),
        in_specs=[
            pl.BlockSpec((N1, SUB, D // 2), lambda i: (0, i, 0)),
            _x_spec(x, b, SUB * N1),
            pl.BlockSpec((D, D), lambda i: (0, 0)),
            _mod_spec(layer, b), _gain_spec(layer),
            pl.BlockSpec((D, 2 * EP), lambda i: (0, 0)),
        ],
        out_specs=_routed_specs(SUB * N1),
        out_shape=_ROUTED_SHAPES,
        scratch_shapes=[pltpu.VMEM((SUB * N1, D // 2), jnp.int32)],
        compiler_params=_cparams(("arbitrary",)),
        name="fourier_out",
    )(mp, x, w_out_bf, mod_all, gain_ffn, wr)


QKV_TM = 1024
TQ = 512
NSB = TQ // BLK
KWIN = TQ + 2 * BLK
NBLK = S // BLK
KPAD = 128
VROWS = HD + 16
ATT_AHEAD = 1
LOG2E = math.log2(math.e)


def _bias_kernel(rb_ref, bucket_ref, o_ref):
    h = pl.program_id(0)
    bucket = bucket_ref[...]
    acc = jnp.full(bucket.shape, NEG_INF, F32)
    for k in range(NBUCKETS):
        acc = jnp.where(bucket == k, rb_ref[k, h] * LOG2E, acc)
    o_ref[0] = acc


def _bias_table(rel_bias, bucket_t):
    return pl.pallas_call(
        _bias_kernel,
        grid=(NH,),
        in_specs=[
            pl.BlockSpec(memory_space=pltpu.SMEM),
            pl.BlockSpec((3 * BLK, BLK), lambda h: (0, 0)),
        ],
        out_specs=pl.BlockSpec((1, 3 * BLK, BLK), lambda h: (h // GQA, 0, h % GQA)),
        out_shape=jax.ShapeDtypeStruct((NKV, 3 * BLK, GQA * BLK), F32),
        compiler_params=_cparams(("arbitrary",)),
        name="rel_bias_table",
    )(rel_bias, bucket_t)


def _qkv_kernel(x_ref, mod_ref, gain_ref, wqt_ref, wk_ref, wvt_ref, qg_ref, kg_ref, qt_ref, k_ref, vt_ref):
    m = mod_ref[0, 0]
    h = _modulate(x_ref[...], gain_ref[0], m[0:1], m[1:2]).astype(BF16)
    nt = (((1,), (1,)), ((), ()))
    qt = lax.dot_general(wqt_ref[...], h, nt, preferred_element_type=F32)
    tm = qt.shape[1]
    q3 = qt.reshape(NH, HD, tm)
    q3 = q3 * lax.rsqrt(jnp.mean(q3 * q3, axis=1, keepdims=True) + EPS)
    qt_ref[...] = (q3.reshape(NH * HD, tm) * qg_ref[...]).astype(BF16)
    k = jnp.dot(h, wk_ref[...], preferred_element_type=F32)
    for g in range(NKV):
        kg = k[:, g * KPAD:(g + 1) * KPAD]
        ms = jnp.sum(kg * kg, axis=-1, keepdims=True) * (1.0 / HD)
        k_ref[:, g * KPAD:(g + 1) * KPAD] = (kg * lax.rsqrt(ms + EPS) * kg_ref[...]).astype(BF16)
    vt_ref[...] = lax.dot_general(wvt_ref[...], h, nt, preferred_element_type=F32).astype(BF16)


def _attn_kernel(qt_ref, kp_ref, kc_ref, kn_ref, vp_ref, vc_ref, vn_ref, x_ref, w_ref, mod_ref,
                 bias_ref, sink_ref, gf_ref, wr_ref, o_ref, h_ref, aff_ref, att_ref, s_ref):
    i = pl.program_id(0)
    kwin = jnp.concatenate([kp_ref[...], kc_ref[...], kn_ref[...]], axis=0)
    vwin = jnp.concatenate([vp_ref[...], vc_ref[...], vn_ref[...]], axis=1)
    ones_rows = (lax.broadcasted_iota(jnp.int32, (VROWS - HD, KWIN), 0) == 0).astype(BF16)
    vaug = [jnp.concatenate([vwin[g * HD:(g + 1) * HD], ones_rows], axis=0) for g in range(NKV)]
    key_pos = i * TQ - BLK + lax.broadcasted_iota(jnp.int32, (KWIN, 1), 0)
    key_mask = jnp.where((key_pos >= 0) & (key_pos < S), 0.0, NEG_INF).astype(BF16)
    lane = lax.broadcasted_iota(jnp.int32, (1, NKV * KPAD), 1)
    kwin = jnp.where(lane % KPAD == HD, key_mask, kwin)
    q_ones = (lax.broadcasted_iota(jnp.int32, (KPAD - HD, GQA * BLK), 0) == 0).astype(BF16)
    items = [(g, r) for g in range(NKV) for r in range(NSB)]

    def scores(g, r):
        kr = kwin[r * BLK:r * BLK + 3 * BLK, g * KPAD:(g + 1) * KPAD]
        qg = jnp.concatenate(
            [qt_ref[(GQA * g + hh) * HD:(GQA * g + hh + 1) * HD, r * BLK:(r + 1) * BLK] for hh in range(GQA)],
            axis=1)
        qa = jnp.concatenate([qg, q_ones], axis=0)
        return jnp.dot(kr, qa, preferred_element_type=F32)

    def probs(s, g, r):
        sink = sink_ref[g] * LOG2E
        s = s + bias_ref[g]
        mx = jnp.maximum(jnp.max(s, axis=0, keepdims=True), sink)
        return jnp.exp2(s - mx).astype(BF16), jnp.exp2(sink - mx)

    nslot = ATT_AHEAD + 1
    for n in range(ATT_AHEAD):
        s_ref[n % nslot] = scores(*items[n])
    for n, (g, r) in enumerate(items):
        if n + ATT_AHEAD < len(items):
            s_ref[(n + ATT_AHEAD) % nslot] = scores(*items[n + ATT_AHEAD])
        p, psink = probs(s_ref[n % nslot], g, r)
        ot = jnp.dot(vaug[g][:, r * BLK:r * BLK + 3 * BLK], p, preferred_element_type=F32)
        ot = ot[:HD] * (1.0 / (ot[HD:HD + 1] + psink))
        for hh in range(GQA):
            hd0 = (GQA * g + hh) * HD
            att_ref[hd0:hd0 + HD, r * BLK:(r + 1) * BLK] = ot[:, hh * BLK:(hh + 1) * BLK]
    g1 = mod_ref[0, 0][2:3]
    att = att_ref[...].T.astype(BF16)
    y = jnp.dot(att, w_ref[...], preferred_element_type=F32)
    xn = x_ref[...] + g1 * y
    o_ref[...] = xn
    h_ref[...], aff_ref[...] = _route(xn, mod_ref[0, 0], gf_ref[0], wr_ref[...])


def _attn_weights(w_qkv, q_gain, k_gain, sink):
    wq, wk, wv = w_qkv[:, :NH * HD], w_qkv[:, NH * HD:(NH + NKV) * HD], w_qkv[:, (NH + NKV) * HD:]
    wqt = wq.T.astype(BF16)
    wvt = wv.T.astype(BF16)
    wk_pad = jnp.pad(wk.reshape(D, NKV, HD), ((0, 0), (0, 0), (0, KPAD - HD))).reshape(D, NKV * KPAD).astype(BF16)
    qg_col = jnp.tile(q_gain * (HD ** -0.5 * LOG2E), NH).reshape(NH * HD, 1)
    kg_row = jnp.pad(k_gain, (0, KPAD - HD)).reshape(1, KPAD)
    sink_row = jnp.repeat(sink, BLK).reshape(NKV, 1, GQA * BLK)
    return wqt, wk_pad, wvt, qg_col, kg_row, sink_row


def _attn_layer(x, b, mod_all, gain, gain_ffn, wr, aw, w_out_bf, bias_tab, layer):
    wqt, wk_pad, wvt, qg_col, kg_row, sink_row = aw
    qt, k, vt = pl.pallas_call(
        _qkv_kernel,
        grid=(S // QKV_TM,),
        in_specs=[
            _x_spec(x, b, QKV_TM),
            _mod_spec(layer, b), _gain_spec(layer),
            pl.BlockSpec((NH * HD, D), lambda i: (0, 0)),
            pl.BlockSpec((D, NKV * KPAD), lambda i: (0, 0)),
            pl.BlockSpec((NKV * HD, D), lambda i: (0, 0)),
            pl.BlockSpec((NH * HD, 1), lambda i: (0, 0)),
            pl.BlockSpec((1, KPAD), lambda i: (0, 0)),
        ],
        out_specs=[
            pl.BlockSpec((NH * HD, QKV_TM), lambda i: (0, i)),
            pl.BlockSpec((QKV_TM, NKV * KPAD), lambda i: (i, 0)),
            pl.BlockSpec((NKV * HD, QKV_TM), lambda i: (0, i)),
        ],
        out_shape=[
            jax.ShapeDtypeStruct((NH * HD, S), BF16),
            jax.ShapeDtypeStruct((S, NKV * KPAD), BF16),
            jax.ShapeDtypeStruct((NKV * HD, S), BF16),
        ],
        compiler_params=_cparams(("arbitrary",)),
        name="attn_qkv",
    )(x, mod_all, gain, wqt, wk_pad, wvt, qg_col, kg_row)

    kw = NKV * KPAD
    vw = NKV * HD
    kprev = pl.BlockSpec((BLK, kw), lambda i: (jnp.maximum(i * NSB - 1, 0), 0))
    kcur = pl.BlockSpec((TQ, kw), lambda i: (i, 0))
    knext = pl.BlockSpec((BLK, kw), lambda i: (jnp.minimum((i + 1) * NSB, NBLK - 1), 0))
    vprev = pl.BlockSpec((vw, BLK), lambda i: (0, jnp.maximum(i * NSB - 1, 0)))
    vcur = pl.BlockSpec((vw, TQ), lambda i: (0, i))
    vnext = pl.BlockSpec((vw, BLK), lambda i: (0, jnp.minimum((i + 1) * NSB, NBLK - 1)))
    return pl.pallas_call(
        _attn_kernel,
        grid=(S // TQ,),
        in_specs=[
            pl.BlockSpec((NH * HD, TQ), lambda i: (0, i)),
            kprev, kcur, knext, vprev, vcur, vnext,
            _x_spec(x, b, TQ),
            pl.BlockSpec((NH * HD, D), lambda i: (0, 0)),
            _mod_spec(layer, b),
            pl.BlockSpec((NKV, 3 * BLK, GQA * BLK), lambda i: (0, 0, 0)),
            pl.BlockSpec((NKV, 1, GQA * BLK), lambda i: (0, 0, 0)),
            _gain_spec(layer),
            pl.BlockSpec((D, 2 * EP), lambda i: (0, 0)),
        ],
        out_specs=_routed_specs(TQ),
        out_shape=_ROUTED_SHAPES,
        scratch_shapes=[pltpu.VMEM((NH * HD, TQ), F32), pltpu.VMEM((ATT_AHEAD + 1, 3 * BLK, GQA * BLK), F32)],
        compiler_params=_cparams(("arbitrary",)),
        name="attn_core",
    )(qt, k, k, k, vt, vt, vt, x, w_out_bf, mod_all, bias_tab, sink_row, gain_ffn, wr)


FFN_TF = 1024
FFN_RT = 512
FFN_WT = 256


def _select_kernel(aff_ref, tri_ref, pos_ref):
    aff = aff_ref[...]
    bits = pltpu.bitcast(aff, jnp.int32)
    rows = aff.shape[0]

    def count_ge(v):
        return jnp.sum((bits >= v).astype(jnp.int32), axis=1, keepdims=True)

    def body(t, cur):
        cand = cur | (jnp.int32(1) << (30 - t))
        return jnp.where(count_ge(cand) >= CAP, cand, cur)

    thr = lax.fori_loop(0, 31, body, jnp.zeros((rows, 1), jnp.int32))
    gt = bits > thr
    eq = bits == thr
    need = CAP - jnp.sum(gt.astype(jnp.int32), axis=1, keepdims=True)
    tri = tri_ref[...]

    def cumsum_excl(mask_f):
        off = jnp.zeros((mask_f.shape[0], 1), F32)
        outs = []
        for j in range(S // 128):
            mj = mask_f[:, j * 128:(j + 1) * 128]
            loc = jnp.dot(mj.astype(BF16), tri, preferred_element_type=F32)
            outs.append(loc - mj + off)
            off = off + loc[:, 127:128]
        return jnp.concatenate(outs, axis=1)

    counts = cumsum_excl(jnp.concatenate([gt, eq], axis=0).astype(F32))
    gt_before, eq_before = counts[:rows], counts[rows:]
    needf = need.astype(F32)
    sel = gt | (eq & (eq_before < needf))
    pos = gt_before + jnp.minimum(eq_before, needf)
    pos_ref[...] = jnp.where(sel, pos.astype(jnp.int32), -1)


SC_CORES = 2
SC_SUBCORES = 16
SC_TILES = SC_CORES * SC_SUBCORES
DISPATCH_SLOTS = CAP * E // SC_TILES
DISPATCH_ROWS = 64
COMBINE_RANGE = 1024
COMBINE_ROWS = 32
SLAB = 128
NSLAB = D // SLAB


def _sc_mesh():
    return plsc.VectorSubcoreMesh(core_axis_name="c", subcore_axis_name="s",
                                  num_cores=SC_CORES, num_subcores=SC_SUBCORES)


def _dispatch_body(pos_hbm, aff_hbm, h_hbm, xin_hbm, idx_hbm, gate_hbm, pos_v, aff_v, idx_v, gate_v,
                   rows_a, rows_b, gsem_a, gsem_b, wsem_a, wsem_b):
    w = lax.axis_index("s") * SC_CORES + lax.axis_index("c")
    e = w // 2
    lo = (w % 2) * DISPATCH_SLOTS
    pltpu.sync_copy(pos_hbm.at[e], pos_v)
    pltpu.sync_copy(aff_hbm.at[e], aff_v)

    @pl.loop(0, S // 16)
    def _(i):
        p = pos_v[pl.ds(i * 16, 16)] - lo
        m = (p >= 0) & (p < DISPATCH_SLOTS)
        tok = lax.iota(jnp.int32, 16) + i * 16
        plsc.store_scatter(idx_v, [p], tok, mask=m)
        plsc.store_scatter(gate_v, [p], aff_v[pl.ds(i * 16, 16)], mask=m)

    pltpu.sync_copy(idx_v, idx_hbm.at[e, pl.ds(lo, DISPATCH_SLOTS)])
    pltpu.sync_copy(gate_v, gate_hbm.at[e, pl.ds(lo, DISPATCH_SLOTS)])

    bufs, gsems, wsems = (rows_a, rows_b), (gsem_a, gsem_b), (wsem_a, wsem_b)
    nchunk = DISPATCH_SLOTS // DISPATCH_ROWS

    def gather(j):
        return pltpu.async_copy(h_hbm.at[idx_v.at[pl.ds(j * DISPATCH_ROWS, DISPATCH_ROWS)]], bufs[j % 2], gsems[j % 2])

    pending_gather = gather(0)
    writes = [None, None]
    for j in range(nchunk):
        pending_gather.wait()
        writes[j % 2] = pltpu.async_copy(
            bufs[j % 2], xin_hbm.at[e, pl.ds(lo + j * DISPATCH_ROWS, DISPATCH_ROWS)], wsems[j % 2])
        if j + 1 < nchunk:
            if writes[(j + 1) % 2] is not None:
                writes[(j + 1) % 2].wait()
            pending_gather = gather(j + 1)
    writes[(nchunk - 2) % 2].wait()
    writes[(nchunk - 1) % 2].wait()


def _dispatch(pos, aff, h):
    return pl.kernel(
        _dispatch_body, mesh=_sc_mesh(),
        out_type=[jax.ShapeDtypeStruct((E, CAP, D // 2), jnp.int32),
                  jax.ShapeDtypeStruct((E, CAP), jnp.int32),
                  jax.ShapeDtypeStruct((E, CAP), F32)],
        scratch_types=[pltpu.VMEM((S,), jnp.int32), pltpu.VMEM((S,), F32),
                       pltpu.VMEM((DISPATCH_SLOTS,), jnp.int32), pltpu.VMEM((DISPATCH_SLOTS,), F32),
                       pltpu.VMEM((DISPATCH_ROWS, D // 2), jnp.int32), pltpu.VMEM((DISPATCH_ROWS, D // 2), jnp.int32),
                       pltpu.SemaphoreType.DMA, pltpu.SemaphoreType.DMA,
                       pltpu.SemaphoreType.DMA, pltpu.SemaphoreType.DMA],
        compiler_params=pltpu.CompilerParams(needs_layout_passes=False),
        name="moe_dispatch",
    )(pos, aff, h)


def _combine_body(y_hbm, idx_hbm, x_hbm, out_hbm, idx_v, li_v, *bufs):
    rows = bufs[:NSLAB]
    accs = bufs[NSLAB:]
    c = lax.axis_index("c")
    s = lax.axis_index("s")
    share = COMBINE_RANGE // SC_SUBCORES
    pltpu.sync_copy(idx_hbm.at[s], idx_v)
    lane = lax.iota(jnp.int32, 16)

    @pl.loop(0, S // COMBINE_RANGE // SC_CORES)
    def _(r):
        t0 = (r * SC_CORES + c) * COMBINE_RANGE
        row0 = t0 + s * share
        pltpu.sync_copy(tuple(x_hbm.at[pl.ds(row0, share), pl.ds(q * SLAB, SLAB)] for q in range(NSLAB)),
                        tuple(accs[q].at[pl.ds(s * share, share)] for q in range(NSLAB)))
        plsc.subcore_barrier()

        @pl.loop(0, CAP // COMBINE_ROWS)
        def _(j):
            hits = jnp.zeros((16,), jnp.int32)
            for v in range(COMBINE_ROWS // 16):
                t = idx_v[pl.ds(j * COMBINE_ROWS + v * 16, 16)] - t0
                ok = (t >= 0) & (t < COMBINE_RANGE)
                li_v[pl.ds(v * 16, 16)] = jnp.where(ok, t, COMBINE_RANGE + lane)
                hits = hits + plsc.all_reduce_population_count(ok)

            @pl.when(jnp.max(hits) > 0)
            def _():
                pltpu.sync_copy(
                    tuple(y_hbm.at[s, pl.ds(j * COMBINE_ROWS, COMBINE_ROWS), pl.ds(q * SLAB, SLAB)]
                          for q in range(NSLAB)),
                    tuple(rows))
                pltpu.sync_copy(tuple(rows), tuple(accs[q].at[li_v] for q in range(NSLAB)), add=True)

        plsc.subcore_barrier()
        pltpu.sync_copy(tuple(accs[q].at[pl.ds(s * share, share)] for q in range(NSLAB)),
                        tuple(out_hbm.at[pl.ds(row0, share), pl.ds(q * SLAB, SLAB)] for q in range(NSLAB)))


def _combine(y, idx, x, out_rows=S):
    return pl.kernel(
        _combine_body, mesh=_sc_mesh(),
        out_type=jax.ShapeDtypeStruct((out_rows, D), F32),
        scratch_types=[pltpu.VMEM((CAP,), jnp.int32), pltpu.VMEM((COMBINE_ROWS,), jnp.int32)]
        + [pltpu.VMEM((COMBINE_ROWS, SLAB), F32) for _ in range(NSLAB)]
        + [pltpu.VMEM_SHARED((COMBINE_RANGE + 16, SLAB), F32) for _ in range(NSLAB)],
        compiler_params=pltpu.CompilerParams(needs_layout_passes=False),
        name="moe_combine",
    )(y, idx, x)


def _ffn_kernel(x_ref, wg_ref, wu_ref, wd_ref, gate_ref, mod_ref, o_ref):
    f = pl.program_id(1)
    last = pl.num_programs(1) - 1
    nt = FFN_TF // FFN_WT
    wg = [wg_ref[0, 0, :, j * FFN_WT:(j + 1) * FFN_WT].astype(BF16) for j in range(nt)]
    wu = [wu_ref[0, 0, :, j * FFN_WT:(j + 1) * FFN_WT].astype(BF16) for j in range(nt)]
    wd = wd_ref[0, 0].astype(BF16)
    is_first = f == 0
    is_last = f == last
    g2 = jnp.where(is_last, mod_ref[0, 0][5:6], 1.0)
    g_row = gate_ref[pl.ds(pl.program_id(0), 1), :]
    g_col = jnp.broadcast_to(g_row, (128, CAP)).T[:, 0:1]
    for r in range(CAP // FFN_RT):
        rows = slice(r * FFN_RT, (r + 1) * FFN_RT)
        xr = _unpack_bf16_pairs(x_ref[0, rows, :])
        acts = []
        for j in range(nt):
            g = jnp.dot(xr, wg[j], preferred_element_type=F32)
            u = jnp.dot(xr, wu[j], preferred_element_type=F32)
            acts.append((g * _sigmoid(g) * u).astype(BF16))
        y = jnp.dot(jnp.concatenate(acts, axis=1), wd, preferred_element_type=F32)
        prev = jnp.where(is_first, 0.0, o_ref[0, rows, :])
        gate = jnp.where(is_last, g_col[rows, :], 1.0)
        o_ref[0, rows, :] = (prev + y) * gate * g2


def _select(aff, tri):
    return pl.pallas_call(
        _select_kernel,
        grid=(1,),
        in_specs=[
            pl.BlockSpec((E, S), lambda i: (0, 0)),
            pl.BlockSpec((128, 128), lambda i: (0, 0)),
        ],
        out_specs=pl.BlockSpec((E, S), lambda i: (0, 0)),
        out_shape=jax.ShapeDtypeStruct((E, S), jnp.int32),
        compiler_params=_cparams(("arbitrary",)),
        name="moe_select",
    )(aff, tri)


def _ffn(xin, gate, b, mod_all, w_gate, w_up, w_down, layer):
    return pl.pallas_call(
        _ffn_kernel,
        grid=(E, F // FFN_TF),
        in_specs=[
            pl.BlockSpec((1, CAP, D // 2), lambda e, f: (e, 0, 0)),
            pl.BlockSpec((1, 1, D, FFN_TF), lambda e, f: (layer, e, 0, f)),
            pl.BlockSpec((1, 1, D, FFN_TF), lambda e, f: (layer, e, 0, f)),
            pl.BlockSpec((1, 1, FFN_TF, D), lambda e, f: (layer, e, f, 0)),
            pl.BlockSpec((E, CAP), lambda e, f: (0, 0)),
            pl.BlockSpec((1, 1, 6, D), lambda e, f: (layer, b, 0, 0)),
        ],
        out_specs=pl.BlockSpec((1, CAP, D), lambda e, f: (e, 0, 0)),
        out_shape=jax.ShapeDtypeStruct((E, CAP, D), F32),
        compiler_params=_cparams(("arbitrary", "arbitrary"), 48),
        name="moe_ffn",
    )(xin, w_gate, w_up, w_down, gate, mod_all)


def kernel(x, c, w_ada, b_ada, norm_mix, norm_ffn, w_fourier_out, w_qkv, w_attn_out, q_gain, k_gain,
           sink, rel_bias, w_router, w_gate, w_up, w_down):
    m0, m2 = _dft_tables()
    m1 = _stage1_table()
    bucket_t = _bucket_table()
    tri = jnp.asarray(np.triu(np.ones((128, 128), np.float32)), BF16)
    mod_all = _ada(c, w_ada, b_ada)
    bias_tab = _bias_table(rel_bias, bucket_t)
    gain_mix = norm_mix.reshape(DEPTH, 1, D)
    gain_ffn = norm_ffn.reshape(DEPTH, 1, D)
    xs = [x, x]
    for layer in range(DEPTH):
        j = layer // 2
        wr = jnp.pad(w_router[layer], ((0, 0), (0, EP - E)))
        wr1 = wr.astype(BF16)
        wr = jnp.concatenate([wr1, (wr - wr1.astype(F32)).astype(BF16)], axis=1)
        if layer % 2 == 0:
            w_out_bf = w_fourier_out[j].astype(BF16)
            routed = [_fourier_layer(xs[b], b, mod_all, gain_mix, gain_ffn, wr, w_out_bf, layer, m0, m1, m2)
                      for b in range(B)]
        else:
            aw = _attn_weights(w_qkv[j], q_gain[j], k_gain[j], sink[j])
            w_out_bf = w_attn_out[j].astype(BF16)
            routed = [_attn_layer(xs[b], b, mod_all, gain_mix, gain_ffn, wr, aw, w_out_bf, bias_tab, layer)
                      for b in range(B)]
        xs = [routed[b][0] for b in range(B)]
        pos = [_select(routed[b][2], tri) for b in range(B)]
        disp = [_dispatch(pos[b], routed[b][2], routed[b][1]) for b in range(B)]
        ys = [_ffn(disp[b][0], disp[b][2], b, mod_all, w_gate, w_up, w_down, layer) for b in range(B)]
        if layer < DEPTH - 1:
            xs = [_combine(ys[b], disp[b][1], xs[b]) for b in range(B)]
    out = _combine(ys[0], disp[0][1], xs[0], out_rows=B * S)
    for b in range(1, B):
        out = lax.dynamic_update_slice(out, _combine(ys[b], disp[b][1], xs[b]), (b * S, 0))
    return out.reshape(B, S, D)
```

```python
import math

import numpy as np
import jax
import jax.numpy as jnp
from jax import lax
from jax.experimental import pallas as pl
from jax.experimental.pallas import tpu as pltpu
from jax.experimental.pallas import tpu_sc as plsc

D = 1024
B = 2
S = 8192
DEPTH = 4
GROUPS = 4
GD = D // GROUPS
HD = 64
NH = 16
NKV = 4
GQA = NH // NKV
WINDOW = 128
BLK = 128
NBUCKETS = 32
MAXDIST = 128
E = 16
CAP = 2 * S // E
F = 2 * D
EPS = 1e-6
NEG_INF = -1e30

N1 = 128
N2 = 64
INV_NORM = 1.0 / math.sqrt(S * GD)

F32 = jnp.float32
BF16 = jnp.bfloat16


def _cparams(sem, vmem_mb=48):
    return pltpu.CompilerParams(dimension_semantics=sem, vmem_limit_bytes=vmem_mb * 1024 * 1024)


def _dft_tables():
    c = np.arange(GD)
    ang0 = 2.0 * np.pi * ((c[:, None] * c[None, :]) % GD) / GD
    m0 = np.concatenate([np.cos(ang0), -np.sin(ang0)], axis=1)
    k2 = np.arange(N2)
    ang2 = 2.0 * np.pi * ((k2[:, None] * k2[None, :]) % N2) / N2
    m2 = np.stack([np.cos(ang2), np.sin(ang2)], axis=2).reshape(N2, 2 * N2)
    return jnp.asarray(m0, BF16), jnp.asarray(m2, BF16)


def _stage1_table():
    s2 = np.arange(N2)[:, None, None]
    k1 = np.arange(N1)[None, :, None]
    s1 = np.arange(N1)[None, None, :]
    th = ((k1 * (N2 * s1 + s2)) % S) * (2.0 * np.pi / S)
    co, si = np.cos(th), np.sin(th)
    top = np.concatenate([co, si], axis=2)
    bot = np.concatenate([-si, co], axis=2)
    return jnp.asarray(np.concatenate([top, bot], axis=1).astype(np.float32), BF16)


def _bucket_table():
    q_off = np.arange(BLK)
    k_off = np.arange(3 * BLK) - BLK
    rel = k_off[:, None] - q_off[None, :]
    half = NBUCKETS // 2
    max_exact = half // 2
    ret = np.where(rel > 0, half, 0)
    n = np.abs(rel)
    nf = np.maximum(n, 1).astype(np.float32)
    ratio = (np.log(nf / np.float32(max_exact)) / np.float32(math.log(MAXDIST / max_exact))).astype(np.float32)
    large = max_exact + (ratio * np.float32(half - max_exact)).astype(np.int32)
    large = np.minimum(large, half - 1)
    bucket = ret + np.where(n < max_exact, n, large)
    return jnp.asarray(np.where(np.abs(rel) <= WINDOW, bucket, -1).astype(np.int32))


def _modulate(x, gain, shift, scale):
    ms = jnp.mean(x * x, axis=-1, keepdims=True)
    return x * lax.rsqrt(ms + EPS) * (gain * (1.0 + scale)) + shift


def _sigmoid(x):
    return 1.0 / (1.0 + jnp.exp(-x))


ADA_TN = 1536


def _ada_kernel(ct_ref, w_ref, b_ref, o_ref):
    ct = ct_ref[...]
    ca = ct * _sigmoid(ct)
    w = w_ref[0]
    for b in range(B):
        o_ref[0, b:b + 1, :] = jnp.sum(w * ca[:, b:b + 1], axis=0, keepdims=True) + b_ref[0]


def _ada(c, w_ada, b_ada):
    out = pl.pallas_call(
        _ada_kernel,
        grid=(DEPTH, 6 * D // ADA_TN),
        in_specs=[
            pl.BlockSpec((D, B), lambda l, j: (0, 0)),
            pl.BlockSpec((1, D, ADA_TN), lambda l, j: (l, 0, j)),
            pl.BlockSpec((1, 1, ADA_TN), lambda l, j: (l, 0, j)),
        ],
        out_specs=pl.BlockSpec((1, B, ADA_TN), lambda l, j: (l, 0, j)),
        out_shape=jax.ShapeDtypeStruct((DEPTH, B, 6 * D), F32),
        compiler_params=_cparams(("arbitrary", "arbitrary"), 32),
        name="ada_mod",
    )(c.T, w_ada, b_ada.reshape(DEPTH, 1, 6 * D))
    return out.reshape(DEPTH, B, 6, D)


def _mod_spec(layer, b):
    return pl.BlockSpec((1, 1, 6, D), lambda *_: (layer, b, 0, 0))


def _gain_spec(layer):
    return pl.BlockSpec((1, 1, D), lambda *_: (layer, 0, 0))


def _x_spec(x, b, tm):
    if x.ndim == 3:
        return pl.BlockSpec((None, tm, D), lambda i: (b, i, 0))
    return pl.BlockSpec((tm, D), lambda i: (i, 0))


EP = 128

_ROUTED_SHAPES = [jax.ShapeDtypeStruct((S, D), F32),
                  jax.ShapeDtypeStruct((S, D // 2), jnp.int32),
                  jax.ShapeDtypeStruct((E, S), F32)]


def _routed_specs(tm):
    return [pl.BlockSpec((tm, D), lambda i: (i, 0)),
            pl.BlockSpec((tm, D // 2), lambda i: (i, 0)),
            pl.BlockSpec((E, tm), lambda i: (0, i))]


F0_TM = 1024
SUB = 8
SEQ_PICKS = 2 * SUB


def _pack_bf16_pairs(h):
    hb = h.astype(BF16).astype(F32)
    lo = pltpu.bitcast(hb[:, :D // 2], jnp.int32)
    hi = pltpu.bitcast(hb[:, D // 2:], jnp.int32)
    return hi | lax.shift_right_logical(lo, jnp.int32(16))


def _unpack_bf16_pairs(xp):
    lo = pltpu.bitcast(xp << 16, F32).astype(BF16)
    hi = pltpu.bitcast(xp & jnp.int32(-65536), F32).astype(BF16)
    return jnp.concatenate([lo, hi], axis=1)


def _route(xn, m, gain_ffn, wr):
    h = _modulate(xn, gain_ffn, m[3:4], m[4:5])
    h1 = h.astype(BF16)
    h2 = (h - h1.astype(F32)).astype(BF16)
    part = jnp.dot(h1, wr, preferred_element_type=F32)
    logits = part[:, :EP] + part[:, EP:] + jnp.dot(h2, wr[:, :EP], preferred_element_type=F32)
    lt = logits.T[:E]
    ex = jnp.exp(lt - jnp.max(lt, axis=0, keepdims=True))
    return _pack_bf16_pairs(h), ex / jnp.sum(ex, axis=0, keepdims=True)


def _f0_kernel(x_ref, mod_ref, gain_ref, m0_ref, o_ref):
    m = mod_ref[0, 0]
    h = _modulate(x_ref[...], gain_ref[0], m[0:1], m[1:2]).astype(BF16)
    m0 = m0_ref[...]
    r = [jnp.dot(h[:, g * GD:(g + 1) * GD], m0, preferred_element_type=F32) for g in range(GROUPS)]
    o_ref[0] = _pack_bf16_pairs(jnp.concatenate([rg[:, :GD] for rg in r], axis=1))
    o_ref[1] = _pack_bf16_pairs(jnp.concatenate([rg[:, GD:] for rg in r], axis=1))


def _f1_kernel(w_ref, m1_ref, o_ref, scr_ref):
    n = o_ref.shape[0]
    for j in range(n):
        scr_ref[j] = w_ref[:, j, :]
    for j in range(n):
        w = _unpack_bf16_pairs(scr_ref[j])
        o_ref[j] = _pack_bf16_pairs(jnp.dot(m1_ref[j], w, preferred_element_type=F32))


def _f2_kernel(u_ref, m2_ref, o_ref, scr_ref):
    m2 = m2_ref[...]
    n = o_ref.shape[0]
    for k in range(n):
        scr_ref[k] = u_ref[:, k, :]
    for k in range(n):
        u = _unpack_bf16_pairs(scr_ref[k])
        o_ref[k] = _pack_bf16_pairs(jnp.dot(m2, u, preferred_element_type=F32))


def _f3_kernel(mp_ref, x_ref, w_ref, mod_ref, gf_ref, wr_ref, o_ref, h_ref, aff_ref, scr_ref):
    g1 = mod_ref[0, 0][2:3]
    for j in range(SUB):
        scr_ref[j * N1:(j + 1) * N1, :] = mp_ref[:, j, :]
    a = _unpack_bf16_pairs(scr_ref[...])
    y = jnp.dot(a, w_ref[...], preferred_element_type=F32)
    xn = x_ref[...] + (g1 * INV_NORM) * y
    o_ref[...] = xn
    h_ref[...], aff_ref[...] = _route(xn, mod_ref[0, 0], gf_ref[0], wr_ref[...])


def _fourier_layer(x, b, mod_all, gain, gain_ffn, wr, w_out_bf, layer, m0, m1, m2):
    wc = pl.pallas_call(
        _f0_kernel,
        grid=(S // F0_TM,),
        in_specs=[
            _x_spec(x, b, F0_TM),
            _mod_spec(layer, b), _gain_spec(layer),
            pl.BlockSpec((GD, 2 * GD), lambda i: (0, 0)),
        ],
        out_specs=pl.BlockSpec((2, F0_TM, D // 2), lambda i: (0, i, 0)),
        out_shape=jax.ShapeDtypeStruct((2, S, D // 2), jnp.int32),
        compiler_params=_cparams(("arbitrary",)),
        name="fourier_chan",
    )(x, mod_all, gain, m0)
    wc = wc.reshape(2 * N1, N2, D // 2)
    u = pl.pallas_call(
        _f1_kernel,
        grid=(N2 // SEQ_PICKS,),
        in_specs=[
            pl.BlockSpec((2 * N1, SEQ_PICKS, D // 2), lambda i: (0, i, 0)),
            pl.BlockSpec((SEQ_PICKS, 2 * N1, 2 * N1), lambda i: (i, 0, 0)),
        ],
        out_specs=pl.BlockSpec((SEQ_PICKS, 2 * N1, D // 2), lambda i: (i, 0, 0)),
        out_shape=jax.ShapeDtypeStruct((N2, 2 * N1, D // 2), jnp.int32),
        scratch_shapes=[pltpu.VMEM((SEQ_PICKS, 2 * N1, D // 2), jnp.int32)],
        compiler_params=_cparams(("arbitrary",)),
        name="fourier_seq1",
    )(wc, m1)
    u = u.reshape(2 * N2, N1, D // 2)
    mp = pl.pallas_call(
        _f2_kernel,
        grid=(N1 // SEQ_PICKS,),
        in_specs=[
            pl.BlockSpec((2 * N2, SEQ_PICKS, D // 2), lambda i: (0, i, 0)),
            pl.BlockSpec((N2, 2 * N2), lambda i: (0, 0)),
        ],
        out_specs=pl.BlockSpec((SEQ_PICKS, N2, D // 2), lambda i: (i, 0, 0)),
        out_shape=jax.ShapeDtypeStruct((N1, N2, D // 2), jnp.int32),
        scratch_shapes=[pltpu.VMEM((SEQ_PICKS, 2 * N2, D // 2), jnp.int32)],
        compiler_params=_cparams(("arbitrary",)),
        name="fourier_seq2",
    )(u, m2)
    return pl.pallas_call(
        _f3_kernel,
        grid=(N2 // SUB,),
        in_specs=[
            pl.BlockSpec((N1, SUB, D // 2), lambda i: (0, i, 0)),
            _x_spec(x, b, SUB * N1),
            pl.BlockSpec((D, D), lambda i: (0, 0)),
            _mod_spec(layer, b), _gain_spec(layer),
            pl.BlockSpec((D, 2 * EP), lambda i: (0, 0)),
        ],
        out_specs=_routed_specs(SUB * N1),
        out_shape=_ROUTED_SHAPES,
        scratch_shapes=[pltpu.VMEM((SUB * N1, D // 2), jnp.int32)],
        compiler_params=_cparams(("arbitrary",)),
        name="fourier_out",
    )(mp, x, w_out_bf, mod_all, gain_ffn, wr)


QKV_TM = 1024
TQ = 512
NSB = TQ // BLK
KWIN = TQ + 2 * BLK
NBLK = S // BLK
KPAD = 128
VROWS = HD + 16
ATT_AHEAD = 1
LOG2E = math.log2(math.e)


def _bias_kernel(rb_ref, bucket_ref, o_ref):
    h = pl.program_id(0)
    bucket = bucket_ref[...]
    acc = jnp.full(bucket.shape, NEG_INF, F32)
    for k in range(NBUCKETS):
        acc = jnp.where(bucket == k, rb_ref[k, h] * LOG2E, acc)
    o_ref[0] = acc


def _bias_table(rel_bias, bucket_t):
    return pl.pallas_call(
        _bias_kernel,
        grid=(NH,),
        in_specs=[
            pl.BlockSpec(memory_space=pltpu.SMEM),
            pl.BlockSpec((3 * BLK, BLK), lambda h: (0, 0)),
        ],
        out_specs=pl.BlockSpec((1, 3 * BLK, BLK), lambda h: (h // GQA, 0, h % GQA)),
        out_shape=jax.ShapeDtypeStruct((NKV, 3 * BLK, GQA * BLK), F32),
        compiler_params=_cparams(("arbitrary",)),
        name="rel_bias_table",
    )(rel_bias, bucket_t)


def _qkv_kernel(x_ref, mod_ref, gain_ref, wqt_ref, wk_ref, wvt_ref, qg_ref, kg_ref, qt_ref, k_ref, vt_ref):
    m = mod_ref[0, 0]
    h = _modulate(x_ref[...], gain_ref[0], m[0:1], m[1:2]).astype(BF16)
    nt = (((1,), (1,)), ((), ()))
    qt = lax.dot_general(wqt_ref[...], h, nt, preferred_element_type=F32)
    tm = qt.shape[1]
    q3 = qt.reshape(NH, HD, tm)
    q3 = q3 * lax.rsqrt(jnp.mean(q3 * q3, axis=1, keepdims=True) + EPS)
    qt_ref[...] = (q3.reshape(NH * HD, tm) * qg_ref[...]).astype(BF16)
    k = jnp.dot(h, wk_ref[...], preferred_element_type=F32)
    for g in range(NKV):
        kg = k[:, g * KPAD:(g + 1) * KPAD]
        ms = jnp.sum(kg * kg, axis=-1, keepdims=True) * (1.0 / HD)
        k_ref[:, g * KPAD:(g + 1) * KPAD] = (kg * lax.rsqrt(ms + EPS) * kg_ref[...]).astype(BF16)
    vt_ref[...] = lax.dot_general(wvt_ref[...], h, nt, preferred_element_type=F32).astype(BF16)


def _attn_kernel(qt_ref, kp_ref, kc_ref, kn_ref, vp_ref, vc_ref, vn_ref, x_ref, w_ref, mod_ref,
                 bias_ref, sink_ref, gf_ref, wr_ref, o_ref, h_ref, aff_ref, att_ref, s_ref):
    i = pl.program_id(0)
    kwin = jnp.concatenate([kp_ref[...], kc_ref[...], kn_ref[...]], axis=0)
    vwin = jnp.concatenate([vp_ref[...], vc_ref[...], vn_ref[...]], axis=1)
    ones_rows = (lax.broadcasted_iota(jnp.int32, (VROWS - HD, KWIN), 0) == 0).astype(BF16)
    vaug = [jnp.concatenate([vwin[g * HD:(g + 1) * HD], ones_rows], axis=0) for g in range(NKV)]
    key_pos = i * TQ - BLK + lax.broadcasted_iota(jnp.int32, (KWIN, 1), 0)
    key_mask = jnp.where((key_pos >= 0) & (key_pos < S), 0.0, NEG_INF).astype(BF16)
    lane = lax.broadcasted_iota(jnp.int32, (1, NKV * KPAD), 1)
    kwin = jnp.where(lane % KPAD == HD, key_mask, kwin)
    q_ones = (lax.broadcasted_iota(jnp.int32, (KPAD - HD, GQA * BLK), 0) == 0).astype(BF16)
    items = [(g, r) for g in range(NKV) for r in range(NSB)]

    def scores(g, r):
        kr = kwin[r * BLK:r * BLK + 3 * BLK, g * KPAD:(g + 1) * KPAD]
        qg = jnp.concatenate(
            [qt_ref[(GQA * g + hh) * HD:(GQA * g + hh + 1) * HD, r * BLK:(r + 1) * BLK] for hh in range(GQA)],
            axis=1)
        qa = jnp.concatenate([qg, q_ones], axis=0)
        return jnp.dot(kr, qa, preferred_element_type=F32)

    def probs(s, g, r):
        sink = sink_ref[g] * LOG2E
        s = s + bias_ref[g]
        mx = jnp.maximum(jnp.max(s, axis=0, keepdims=True), sink)
        return jnp.exp2(s - mx).astype(BF16), jnp.exp2(sink - mx)

    nslot = ATT_AHEAD + 1
    for n in range(ATT_AHEAD):
        s_ref[n % nslot] = scores(*items[n])
    for n, (g, r) in enumerate(items):
        if n + ATT_AHEAD < len(items):
            s_ref[(n + ATT_AHEAD) % nslot] = scores(*items[n + ATT_AHEAD])
        p, psink = probs(s_ref[n % nslot], g, r)
        ot = jnp.dot(vaug[g][:, r * BLK:r * BLK + 3 * BLK], p, preferred_element_type=F32)
        ot = ot[:HD] * (1.0 / (ot[HD:HD + 1] + psink))
        for hh in range(GQA):
            hd0 = (GQA * g + hh) * HD
            att_ref[hd0:hd0 + HD, r * BLK:(r + 1) * BLK] = ot[:, hh * BLK:(hh + 1) * BLK]
    g1 = mod_ref[0, 0][2:3]
    att = att_ref[...].T.astype(BF16)
    y = jnp.dot(att, w_ref[...], preferred_element_type=F32)
    xn = x_ref[...] + g1 * y
    o_ref[...] = xn
    h_ref[...], aff_ref[...] = _route(xn, mod_ref[0, 0], gf_ref[0], wr_ref[...])


def _attn_weights(w_qkv, q_gain, k_gain, sink):
    wq, wk, wv = w_qkv[:, :NH * HD], w_qkv[:, NH * HD:(NH + NKV) * HD], w_qkv[:, (NH + NKV) * HD:]
    wqt = wq.T.astype(BF16)
    wvt = wv.T.astype(BF16)
    wk_pad = jnp.pad(wk.reshape(D, NKV, HD), ((0, 0), (0, 0), (0, KPAD - HD))).reshape(D, NKV * KPAD).astype(BF16)
    qg_col = jnp.tile(q_gain * (HD ** -0.5 * LOG2E), NH).reshape(NH * HD, 1)
    kg_row = jnp.pad(k_gain, (0, KPAD - HD)).reshape(1, KPAD)
    sink_row = jnp.repeat(sink, BLK).reshape(NKV, 1, GQA * BLK)
    return wqt, wk_pad, wvt, qg_col, kg_row, sink_row


def _attn_layer(x, b, mod_all, gain, gain_ffn, wr, aw, w_out_bf, bias_tab, layer):
    wqt, wk_pad, wvt, qg_col, kg_row, sink_row = aw
    qt, k, vt = pl.pallas_call(
        _qkv_kernel,
        grid=(S // QKV_TM,),
        in_specs=[
            _x_spec(x, b, QKV_TM),
            _mod_spec(layer, b), _gain_spec(layer),
            pl.BlockSpec((NH * HD, D), lambda i: (0, 0)),
            pl.BlockSpec((D, NKV * KPAD), lambda i: (0, 0)),
            pl.BlockSpec((NKV * HD, D), lambda i: (0, 0)),
            pl.BlockSpec((NH * HD, 1), lambda i: (0, 0)),
            pl.BlockSpec((1, KPAD), lambda i: (0, 0)),
        ],
        out_specs=[
            pl.BlockSpec((NH * HD, QKV_TM), lambda i: (0, i)),
            pl.BlockSpec((QKV_TM, NKV * KPAD), lambda i: (i, 0)),
            pl.BlockSpec((NKV * HD, QKV_TM), lambda i: (0, i)),
        ],
        out_shape=[
            jax.ShapeDtypeStruct((NH * HD, S), BF16),
            jax.ShapeDtypeStruct((S, NKV * KPAD), BF16),
            jax.ShapeDtypeStruct((NKV * HD, S), BF16),
        ],
        compiler_params=_cparams(("arbitrary",)),
        name="attn_qkv",
    )(x, mod_all, gain, wqt, wk_pad, wvt, qg_col, kg_row)

    kw = NKV * KPAD
    vw = NKV * HD
    kprev = pl.BlockSpec((BLK, kw), lambda i: (jnp.maximum(i * NSB - 1, 0), 0))
    kcur = pl.BlockSpec((TQ, kw), lambda i: (i, 0))
    knext = pl.BlockSpec((BLK, kw), lambda i: (jnp.minimum((i + 1) * NSB, NBLK - 1), 0))
    vprev = pl.BlockSpec((vw, BLK), lambda i: (0, jnp.maximum(i * NSB - 1, 0)))
    vcur = pl.BlockSpec((vw, TQ), lambda i: (0, i))
    vnext = pl.BlockSpec((vw, BLK), lambda i: (0, jnp.minimum((i + 1) * NSB, NBLK - 1)))
    return pl.pallas_call(
        _attn_kernel,
        grid=(S // TQ,),
        in_specs=[
            pl.BlockSpec((NH * HD, TQ), lambda i: (0, i)),
            kprev, kcur, knext, vprev, vcur, vnext,
            _x_spec(x, b, TQ),
            pl.BlockSpec((NH * HD, D), lambda i: (0, 0)),
            _mod_spec(layer, b),
            pl.BlockSpec((NKV, 3 * BLK, GQA * BLK), lambda i: (0, 0, 0)),
            pl.BlockSpec((NKV, 1, GQA * BLK), lambda i: (0, 0, 0)),
            _gain_spec(layer),
            pl.BlockSpec((D, 2 * EP), lambda i: (0, 0)),
        ],
        out_specs=_routed_specs(TQ),
        out_shape=_ROUTED_SHAPES,
        scratch_shapes=[pltpu.VMEM((NH * HD, TQ), F32), pltpu.VMEM((ATT_AHEAD + 1, 3 * BLK, GQA * BLK), F32)],
        compiler_params=_cparams(("arbitrary",)),
        name="attn_core",
    )(qt, k, k, k, vt, vt, vt, x, w_out_bf, mod_all, bias_tab, sink_row, gain_ffn, wr)


FFN_TF = 1024
FFN_RT = 512
FFN_WT = 256


def _select_kernel(aff_ref, tri_ref, pos_ref):
    aff = aff_ref[...]
    bits = pltpu.bitcast(aff, jnp.int32)
    rows = aff.shape[0]

    def count_ge(v):
        return jnp.sum((bits >= v).astype(jnp.int32), axis=1, keepdims=True)

    def body(t, cur):
        sh = 28 - 2 * t
        c1, c2, c3 = cur | (jnp.int32(1) << sh), cur | (jnp.int32(2) << sh), cur | (jnp.int32(3) << sh)
        n1, n2, n3 = count_ge(c1), count_ge(c2), count_ge(c3)
        return jnp.where(n3 >= CAP, c3, jnp.where(n2 >= CAP, c2, jnp.where(n1 >= CAP, c1, cur)))

    top = jnp.full((rows, 1), 1 << 30, jnp.int32)
    thr = lax.fori_loop(0, 15, body, jnp.where(count_ge(top) >= CAP, top, 0))
    gt = bits > thr
    eq = bits == thr
    need = CAP - jnp.sum(gt.astype(jnp.int32), axis=1, keepdims=True)
    tri = tri_ref[...]

    def cumsum_excl(mask_f):
        off = jnp.zeros((mask_f.shape[0], 1), F32)
        outs = []
        for j in range(S // 128):
            mj = mask_f[:, j * 128:(j + 1) * 128]
            loc = jnp.dot(mj.astype(BF16), tri, preferred_element_type=F32)
            outs.append(loc - mj + off)
            off = off + loc[:, 127:128]
        return jnp.concatenate(outs, axis=1)

    counts = cumsum_excl(jnp.concatenate([gt, eq], axis=0).astype(F32))
    gt_before, eq_before = counts[:rows], counts[rows:]
    needf = need.astype(F32)
    sel = gt | (eq & (eq_before < needf))
    pos = gt_before + jnp.minimum(eq_before, needf)
    pos_ref[...] = jnp.where(sel, pos.astype(jnp.int32), -1)


SC_CORES = 2
SC_SUBCORES = 16
SC_TILES = SC_CORES * SC_SUBCORES
DISPATCH_SLOTS = CAP * E // SC_TILES
DISPATCH_ROWS = 64
COMBINE_RANGE = 1024
COMBINE_ROWS = 32
SLAB = 128
NSLAB = D // SLAB


def _sc_mesh():
    return plsc.VectorSubcoreMesh(core_axis_name="c", subcore_axis_name="s",
                                  num_cores=SC_CORES, num_subcores=SC_SUBCORES)


def _dispatch_body(pos_hbm, aff_hbm, h_hbm, xin_hbm, idx_hbm, gate_hbm, pos_v, aff_v, idx_v, gate_v,
                   rows_a, rows_b, gsem_a, gsem_b, wsem_a, wsem_b):
    w = lax.axis_index("s") * SC_CORES + lax.axis_index("c")
    e = w // 2
    lo = (w % 2) * DISPATCH_SLOTS
    pltpu.sync_copy(pos_hbm.at[e], pos_v)
    pltpu.sync_copy(aff_hbm.at[e], aff_v)

    @pl.loop(0, S // 16)
    def _(i):
        p = pos_v[pl.ds(i * 16, 16)] - lo
        m = (p >= 0) & (p < DISPATCH_SLOTS)
        tok = lax.iota(jnp.int32, 16) + i * 16
        plsc.store_scatter(idx_v, [p], tok, mask=m)
        plsc.store_scatter(gate_v, [p], aff_v[pl.ds(i * 16, 16)], mask=m)

    pltpu.sync_copy(idx_v, idx_hbm.at[e, pl.ds(lo, DISPATCH_SLOTS)])
    pltpu.sync_copy(gate_v, gate_hbm.at[e, pl.ds(lo, DISPATCH_SLOTS)])

    bufs, gsems, wsems = (rows_a, rows_b), (gsem_a, gsem_b), (wsem_a, wsem_b)
    nchunk = DISPATCH_SLOTS // DISPATCH_ROWS

    def gather(j):
        return pltpu.async_copy(h_hbm.at[idx_v.at[pl.ds(j * DISPATCH_ROWS, DISPATCH_ROWS)]], bufs[j % 2], gsems[j % 2])

    pending_gather = gather(0)
    writes = [None, None]
    for j in range(nchunk):
        pending_gather.wait()
        writes[j % 2] = pltpu.async_copy(
            bufs[j % 2], xin_hbm.at[e, pl.ds(lo + j * DISPATCH_ROWS, DISPATCH_ROWS)], wsems[j % 2])
        if j + 1 < nchunk:
            if writes[(j + 1) % 2] is not None:
                writes[(j + 1) % 2].wait()
            pending_gather = gather(j + 1)
    writes[(nchunk - 2) % 2].wait()
    writes[(nchunk - 1) % 2].wait()


def _dispatch(pos, aff, h):
    return pl.kernel(
        _dispatch_body, mesh=_sc_mesh(),
        out_type=[jax.ShapeDtypeStruct((E, CAP, D // 2), jnp.int32),
                  jax.ShapeDtypeStruct((E, CAP), jnp.int32),
                  jax.ShapeDtypeStruct((E, CAP), F32)],
        scratch_types=[pltpu.VMEM((S,), jnp.int32), pltpu.VMEM((S,), F32),
                       pltpu.VMEM((DISPATCH_SLOTS,), jnp.int32), pltpu.VMEM((DISPATCH_SLOTS,), F32),
                       pltpu.VMEM((DISPATCH_ROWS, D // 2), jnp.int32), pltpu.VMEM((DISPATCH_ROWS, D // 2), jnp.int32),
                       pltpu.SemaphoreType.DMA, pltpu.SemaphoreType.DMA,
                       pltpu.SemaphoreType.DMA, pltpu.SemaphoreType.DMA],
        compiler_params=pltpu.CompilerParams(needs_layout_passes=False),
        name="moe_dispatch",
    )(pos, aff, h)


def _combine_body(y_hbm, idx_hbm, x_hbm, out_hbm, idx_v, li_v, *bufs):
    rows = bufs[:NSLAB]
    accs = bufs[NSLAB:]
    c = lax.axis_index("c")
    s = lax.axis_index("s")
    share = COMBINE_RANGE // SC_SUBCORES
    pltpu.sync_copy(idx_hbm.at[s], idx_v)
    lane = lax.iota(jnp.int32, 16)

    @pl.loop(0, S // COMBINE_RANGE // SC_CORES)
    def _(r):
        t0 = (r * SC_CORES + c) * COMBINE_RANGE
        row0 = t0 + s * share
        pltpu.sync_copy(tuple(x_hbm.at[pl.ds(row0, share), pl.ds(q * SLAB, SLAB)] for q in range(NSLAB)),
                        tuple(accs[q].at[pl.ds(s * share, share)] for q in range(NSLAB)))
        plsc.subcore_barrier()

        @pl.loop(0, CAP // COMBINE_ROWS)
        def _(j):
            hits = jnp.zeros((16,), jnp.int32)
            for v in range(COMBINE_ROWS // 16):
                t = idx_v[pl.ds(j * COMBINE_ROWS + v * 16, 16)] - t0
                ok = (t >= 0) & (t < COMBINE_RANGE)
                li_v[pl.ds(v * 16, 16)] = jnp.where(ok, t, COMBINE_RANGE + lane)
                hits = hits + plsc.all_reduce_population_count(ok)

            @pl.when(jnp.max(hits) > 0)
            def _():
                pltpu.sync_copy(
                    tuple(y_hbm.at[s, pl.ds(j * COMBINE_ROWS, COMBINE_ROWS), pl.ds(q * SLAB, SLAB)]
                          for q in range(NSLAB)),
                    tuple(rows))
                pltpu.sync_copy(tuple(rows), tuple(accs[q].at[li_v] for q in range(NSLAB)), add=True)

        plsc.subcore_barrier()
        pltpu.sync_copy(tuple(accs[q].at[pl.ds(s * share, share)] for q in range(NSLAB)),
                        tuple(out_hbm.at[pl.ds(row0, share), pl.ds(q * SLAB, SLAB)] for q in range(NSLAB)))


def _combine(y, idx, x, out_rows=S):
    return pl.kernel(
        _combine_body, mesh=_sc_mesh(),
        out_type=jax.ShapeDtypeStruct((out_rows, D), F32),
        scratch_types=[pltpu.VMEM((CAP,), jnp.int32), pltpu.VMEM((COMBINE_ROWS,), jnp.int32)]
        + [pltpu.VMEM((COMBINE_ROWS, SLAB), F32) for _ in range(NSLAB)]
        + [pltpu.VMEM_SHARED((COMBINE_RANGE + 16, SLAB), F32) for _ in range(NSLAB)],
        compiler_params=pltpu.CompilerParams(needs_layout_passes=False),
        name="moe_combine",
    )(y, idx, x)


def _ffn_kernel(x_ref, wg_ref, wu_ref, wd_ref, gate_ref, mod_ref, o_ref):
    f = pl.program_id(1)
    last = pl.num_programs(1) - 1
    nt = FFN_TF // FFN_WT
    wg = [wg_ref[0, 0, :, j * FFN_WT:(j + 1) * FFN_WT].astype(BF16) for j in range(nt)]
    wu = [wu_ref[0, 0, :, j * FFN_WT:(j + 1) * FFN_WT].astype(BF16) for j in range(nt)]
    wd = wd_ref[0, 0].astype(BF16)
    is_first = f == 0
    is_last = f == last
    g2 = jnp.where(is_last, mod_ref[0, 0][5:6], 1.0)
    g_row = gate_ref[pl.ds(pl.program_id(0), 1), :]
    g_col = jnp.broadcast_to(g_row, (128, CAP)).T[:, 0:1]
    for r in range(CAP // FFN_RT):
        rows = slice(r * FFN_RT, (r + 1) * FFN_RT)
        xr = _unpack_bf16_pairs(x_ref[0, rows, :])
        acts = []
        for j in range(nt):
            g = jnp.dot(xr, wg[j], preferred_element_type=F32)
            u = jnp.dot(xr, wu[j], preferred_element_type=F32)
            acts.append((g * _sigmoid(g) * u).astype(BF16))
        y = jnp.dot(jnp.concatenate(acts, axis=1), wd, preferred_element_type=F32)
        prev = jnp.where(is_first, 0.0, o_ref[0, rows, :])
        gate = jnp.where(is_last, g_col[rows, :], 1.0)
        o_ref[0, rows, :] = (prev + y) * gate * g2


def _select(aff, tri):
    return pl.pallas_call(
        _select_kernel,
        grid=(1,),
        in_specs=[
            pl.BlockSpec((E, S), lambda i: (0, 0)),
            pl.BlockSpec((128, 128), lambda i: (0, 0)),
        ],
        out_specs=pl.BlockSpec((E, S), lambda i: (0, 0)),
        out_shape=jax.ShapeDtypeStruct((E, S), jnp.int32),
        compiler_params=_cparams(("arbitrary",)),
        name="moe_select",
    )(aff, tri)


def _ffn(xin, gate, b, mod_all, w_gate, w_up, w_down, layer):
    return pl.pallas_call(
        _ffn_kernel,
        grid=(E, F // FFN_TF),
        in_specs=[
            pl.BlockSpec((1, CAP, D // 2), lambda e, f: (e, 0, 0)),
            pl.BlockSpec((1, 1, D, FFN_TF), lambda e, f: (layer, e, 0, f)),
            pl.BlockSpec((1, 1, D, FFN_TF), lambda e, f: (layer, e, 0, f)),
            pl.BlockSpec((1, 1, FFN_TF, D), lambda e, f: (layer, e, f, 0)),
            pl.BlockSpec((E, CAP), lambda e, f: (0, 0)),
            pl.BlockSpec((1, 1, 6, D), lambda e, f: (layer, b, 0, 0)),
        ],
        out_specs=pl.BlockSpec((1, CAP, D), lambda e, f: (e, 0, 0)),
        out_shape=jax.ShapeDtypeStruct((E, CAP, D), F32),
        compiler_params=_cparams(("arbitrary", "arbitrary"), 48),
        name="moe_ffn",
    )(xin, w_gate, w_up, w_down, gate, mod_all)


def kernel(x, c, w_ada, b_ada, norm_mix, norm_ffn, w_fourier_out, w_qkv, w_attn_out, q_gain, k_gain,
           sink, rel_bias, w_router, w_gate, w_up, w_down):
    m0, m2 = _dft_tables()
    m1 = _stage1_table()
    bucket_t = _bucket_table()
    tri = jnp.asarray(np.triu(np.ones((128, 128), np.float32)), BF16)
    mod_all = _ada(c, w_ada, b_ada)
    bias_tab = _bias_table(rel_bias, bucket_t)
    gain_mix = norm_mix.reshape(DEPTH, 1, D)
    gain_ffn = norm_ffn.reshape(DEPTH, 1, D)
    xs = [x, x]
    for layer in range(DEPTH):
        j = layer // 2
        wr = jnp.pad(w_router[layer], ((0, 0), (0, EP - E)))
        wr1 = wr.astype(BF16)
        wr = jnp.concatenate([wr1, (wr - wr1.astype(F32)).astype(BF16)], axis=1)
        if layer % 2 == 0:
            w_out_bf = w_fourier_out[j].astype(BF16)
            routed = [_fourier_layer(xs[b], b, mod_all, gain_mix, gain_ffn, wr, w_out_bf, layer, m0, m1, m2)
                      for b in range(B)]
        else:
            aw = _attn_weights(w_qkv[j], q_gain[j], k_gain[j], sink[j])
            w_out_bf = w_attn_out[j].astype(BF16)
            routed = [_attn_layer(xs[b], b, mod_all, gain_mix, gain_ffn, wr, aw, w_out_bf, bias_tab, layer)
                      for b in range(B)]
        xs = [routed[b][0] for b in range(B)]
        pos = [_select(routed[b][2], tri) for b in range(B)]
        disp = [_dispatch(pos[b], routed[b][2], routed[b][1]) for b in range(B)]
        ys = [_ffn(disp[b][0], disp[b][2], b, mod_all, w_gate, w_up, w_down, layer) for b in range(B)]
        if layer < DEPTH - 1:
            xs = [_combine(ys[b], disp[b][1], xs[b]) for b in range(B)]
    out = _combine(ys[0], disp[0][1], xs[0], out_rows=B * S)
    for b in range(1, B):
        out = lax.dynamic_update_slice(out, _combine(ys[b], disp[b][1], xs[b]), (b * S, 0))
    return out.reshape(B, S, D)
```

```python
import functools
import math

import numpy as np
import jax
import jax.numpy as jnp
from jax import lax
from jax.experimental import pallas as pl
from jax.experimental.pallas import tpu as pltpu
from jax.experimental.pallas import tpu_sc as plsc

D = 1024
B = 2
S = 8192
DEPTH = 4
GROUPS = 4
GD = D // GROUPS
HD = 64
NH = 16
NKV = 4
GQA = NH // NKV
WINDOW = 128
BLK = 128
NBUCKETS = 32
MAXDIST = 128
E = 16
CAP = 2 * S // E
F = 2 * D
EPS = 1e-6
NEG_INF = -1e30

N1 = 128
N2 = 64
INV_NORM = 1.0 / math.sqrt(S * GD)

F32 = jnp.float32
BF16 = jnp.bfloat16


def _cparams(sem, vmem_mb=48):
    return pltpu.CompilerParams(dimension_semantics=sem, vmem_limit_bytes=vmem_mb * 1024 * 1024)


def _dft_tables():
    c = np.arange(GD)
    ang0 = 2.0 * np.pi * ((c[:, None] * c[None, :]) % GD) / GD
    m0 = np.concatenate([np.cos(ang0), -np.sin(ang0)], axis=1)
    k2 = np.arange(N2)
    ang2 = 2.0 * np.pi * ((k2[:, None] * k2[None, :]) % N2) / N2
    m2 = np.stack([np.cos(ang2), np.sin(ang2)], axis=2).reshape(N2, 2 * N2)
    return jnp.asarray(m0, BF16), jnp.asarray(m2, BF16)


def _stage1_table():
    s2 = np.arange(N2)[:, None, None]
    k1 = np.arange(N1)[None, :, None]
    s1 = np.arange(N1)[None, None, :]
    th = ((k1 * (N2 * s1 + s2)) % S) * (2.0 * np.pi / S)
    co, si = np.cos(th), np.sin(th)
    top = np.concatenate([co, si], axis=2)
    bot = np.concatenate([-si, co], axis=2)
    return jnp.asarray(np.concatenate([top, bot], axis=1).astype(np.float32), BF16)


def _bucket_table():
    q_off = np.arange(BLK)
    k_off = np.arange(3 * BLK) - BLK
    rel = k_off[:, None] - q_off[None, :]
    half = NBUCKETS // 2
    max_exact = half // 2
    ret = np.where(rel > 0, half, 0)
    n = np.abs(rel)
    nf = np.maximum(n, 1).astype(np.float32)
    ratio = (np.log(nf / np.float32(max_exact)) / np.float32(math.log(MAXDIST / max_exact))).astype(np.float32)
    large = max_exact + (ratio * np.float32(half - max_exact)).astype(np.int32)
    large = np.minimum(large, half - 1)
    bucket = ret + np.where(n < max_exact, n, large)
    return jnp.asarray(np.where(np.abs(rel) <= WINDOW, bucket, -1).astype(np.int32))


def _modulate(x, gain, shift, scale):
    ms = jnp.mean(x * x, axis=-1, keepdims=True)
    return x * lax.rsqrt(ms + EPS) * (gain * (1.0 + scale)) + shift


def _sigmoid(x):
    return 1.0 / (1.0 + jnp.exp(-x))


ADA_TN = 1536


def _ada_kernel(ct_ref, w_ref, b_ref, o_ref):
    ct = ct_ref[...]
    ca = ct * _sigmoid(ct)
    w = w_ref[0]
    for b in range(B):
        o_ref[0, b:b + 1, :] = jnp.sum(w * ca[:, b:b + 1], axis=0, keepdims=True) + b_ref[0]


def _ada(c, w_ada, b_ada):
    out = pl.pallas_call(
        _ada_kernel,
        grid=(DEPTH, 6 * D // ADA_TN),
        in_specs=[
            pl.BlockSpec((D, B), lambda l, j: (0, 0)),
            pl.BlockSpec((1, D, ADA_TN), lambda l, j: (l, 0, j)),
            pl.BlockSpec((1, 1, ADA_TN), lambda l, j: (l, 0, j)),
        ],
        out_specs=pl.BlockSpec((1, B, ADA_TN), lambda l, j: (l, 0, j)),
        out_shape=jax.ShapeDtypeStruct((DEPTH, B, 6 * D), F32),
        compiler_params=_cparams(("arbitrary", "arbitrary"), 32),
        name="ada_mod",
    )(c.T, w_ada, b_ada.reshape(DEPTH, 1, 6 * D))
    return out.reshape(DEPTH, B, 6, D)


def _mod_spec(layer, b):
    return pl.BlockSpec((1, 1, 6, D), lambda *_: (layer, b, 0, 0))


def _gain_spec(layer):
    return pl.BlockSpec((1, 1, D), lambda *_: (layer, 0, 0))


def _x_spec(x, b, tm):
    if x.ndim == 3:
        return pl.BlockSpec((None, tm, D), lambda i: (b, i, 0))
    return pl.BlockSpec((tm, D), lambda i: (i, 0))


EP = 128

_ROUTED_SHAPES = [jax.ShapeDtypeStruct((S, D), F32),
                  jax.ShapeDtypeStruct((S, D // 2), jnp.int32),
                  jax.ShapeDtypeStruct((E, S), F32)]


def _routed_specs(tm):
    return [pl.BlockSpec((tm, D), lambda i: (i, 0)),
            pl.BlockSpec((tm, D // 2), lambda i: (i, 0)),
            pl.BlockSpec((E, tm), lambda i: (0, i))]


F0_TM = 1024
SUB = 8
SEQ_PICKS = 2 * SUB


def _pack_bf16_pairs(h):
    hb = h.astype(BF16).astype(F32)
    lo = pltpu.bitcast(hb[:, :D // 2], jnp.int32)
    hi = pltpu.bitcast(hb[:, D // 2:], jnp.int32)
    return hi | lax.shift_right_logical(lo, jnp.int32(16))


def _unpack_bf16_pairs(xp):
    lo = pltpu.bitcast(xp << 16, F32).astype(BF16)
    hi = pltpu.bitcast(xp & jnp.int32(-65536), F32).astype(BF16)
    return jnp.concatenate([lo, hi], axis=1)


def _route(xn, m, gain_ffn, wr):
    h = _modulate(xn, gain_ffn, m[3:4], m[4:5])
    h1 = h.astype(BF16)
    h2 = (h - h1.astype(F32)).astype(BF16)
    part = jnp.dot(h1, wr, preferred_element_type=F32)
    logits = part[:, :EP] + part[:, EP:] + jnp.dot(h2, wr[:, :EP], preferred_element_type=F32)
    lt = logits.T[:E]
    ex = jnp.exp(lt - jnp.max(lt, axis=0, keepdims=True))
    return _pack_bf16_pairs(h), ex / jnp.sum(ex, axis=0, keepdims=True)


def _f0_kernel(x_ref, mod_ref, gain_ref, m0_ref, o_ref):
    m = mod_ref[0, 0]
    h = _modulate(x_ref[...], gain_ref[0], m[0:1], m[1:2]).astype(BF16)
    m0 = m0_ref[...]
    r = [jnp.dot(h[:, g * GD:(g + 1) * GD], m0, preferred_element_type=F32) for g in range(GROUPS)]
    o_ref[0] = _pack_bf16_pairs(jnp.concatenate([rg[:, :GD] for rg in r], axis=1))
    o_ref[1] = _pack_bf16_pairs(jnp.concatenate([rg[:, GD:] for rg in r], axis=1))


def _f1_kernel(w_ref, m1_ref, o_ref, scr_ref):
    n = o_ref.shape[0]
    for j in range(n):
        scr_ref[j] = w_ref[:, j, :]
    for j in range(n):
        w = _unpack_bf16_pairs(scr_ref[j])
        o_ref[j] = _pack_bf16_pairs(jnp.dot(m1_ref[j], w, preferred_element_type=F32))


def _f2_kernel(u_ref, m2_ref, o_ref, scr_ref):
    m2 = m2_ref[...]
    n = o_ref.shape[0]
    for k in range(n):
        scr_ref[k] = u_ref[:, k, :]
    for k in range(n):
        u = _unpack_bf16_pairs(scr_ref[k])
        o_ref[k] = _pack_bf16_pairs(jnp.dot(m2, u, preferred_element_type=F32))


def _f3_kernel(mp_ref, x_ref, w_ref, mod_ref, gf_ref, wr_ref, o_ref, h_ref, aff_ref, scr_ref):
    g1 = mod_ref[0, 0][2:3]
    for j in range(SUB):
        scr_ref[j * N1:(j + 1) * N1, :] = mp_ref[:, j, :]
    a = _unpack_bf16_pairs(scr_ref[...])
    y = jnp.dot(a, w_ref[...], preferred_element_type=F32)
    xn = x_ref[...] + (g1 * INV_NORM) * y
    o_ref[...] = xn
    h_ref[...], aff_ref[...] = _route(xn, mod_ref[0, 0], gf_ref[0], wr_ref[...])


def _fourier_layer(x, b, mod_all, gain, gain_ffn, wr, w_out_bf, layer, m0, m1, m2):
    wc = pl.pallas_call(
        _f0_kernel,
        grid=(S // F0_TM,),
        in_specs=[
            _x_spec(x, b, F0_TM),
            _mod_spec(layer, b), _gain_spec(layer),
            pl.BlockSpec((GD, 2 * GD), lambda i: (0, 0)),
        ],
        out_specs=pl.BlockSpec((2, F0_TM, D // 2), lambda i: (0, i, 0)),
        out_shape=jax.ShapeDtypeStruct((2, S, D // 2), jnp.int32),
        compiler_params=_cparams(("arbitrary",)),
        name="fourier_chan",
    )(x, mod_all, gain, m0)
    wc = wc.reshape(2 * N1, N2, D // 2)
    u = pl.pallas_call(
        _f1_kernel,
        grid=(N2 // SEQ_PICKS,),
        in_specs=[
            pl.BlockSpec((2 * N1, SEQ_PICKS, D // 2), lambda i: (0, i, 0)),
            pl.BlockSpec((SEQ_PICKS, 2 * N1, 2 * N1), lambda i: (i, 0, 0)),
        ],
        out_specs=pl.BlockSpec((SEQ_PICKS, 2 * N1, D // 2), lambda i: (i, 0, 0)),
        out_shape=jax.ShapeDtypeStruct((N2, 2 * N1, D // 2), jnp.int32),
        scratch_shapes=[pltpu.VMEM((SEQ_PICKS, 2 * N1, D // 2), jnp.int32)],
        compiler_params=_cparams(("arbitrary",)),
        name="fourier_seq1",
    )(wc, m1)
    u = u.reshape(2 * N2, N1, D // 2)
    mp = pl.pallas_call(
        _f2_kernel,
        grid=(N1 // SEQ_PICKS,),
        in_specs=[
            pl.BlockSpec((2 * N2, SEQ_PICKS, D // 2), lambda i: (0, i, 0)),
            pl.BlockSpec((N2, 2 * N2), lambda i: (0, 0)),
        ],
        out_specs=pl.BlockSpec((SEQ_PICKS, N2, D // 2), lambda i: (i, 0, 0)),
        out_shape=jax.ShapeDtypeStruct((N1, N2, D // 2), jnp.int32),
        scratch_shapes=[pltpu.VMEM((SEQ_PICKS, 2 * N2, D // 2), jnp.int32)],
        compiler_params=_cparams(("arbitrary",)),
        name="fourier_seq2",
    )(u, m2)
    return pl.pallas_call(
        _f3_kernel,
        grid=(N2 // SUB,),
        in_specs=[
            pl.BlockSpec((N1, SUB, D // 2), lambda i: (0, i, 0)),
            _x_spec(x, b, SUB * N1),
            pl.BlockSpec((D, D), lambda i: (0, 0)),
            _mod_spec(layer, b), _gain_spec(layer),
            pl.BlockSpec((D, 2 * EP), lambda i: (0, 0)),
        ],
        out_specs=_routed_specs(SUB * N1),
        out_shape=_ROUTED_SHAPES,
        scratch_shapes=[pltpu.VMEM((SUB * N1, D // 2), jnp.int32)],
        compiler_params=_cparams(("arbitrary",)),
        name="fourier_out",
    )(mp, x, w_out_bf, mod_all, gain_ffn, wr)


QKV_TM = 1024
TQ = 1024
NSB = TQ // BLK
KWIN = TQ + 2 * BLK
NBLK = S // BLK
KPAD = 128
VROWS = HD + 16
ATT_AHEAD = 1
LOG2E = math.log2(math.e)


def _bias_kernel(rb_ref, bucket_ref, o_ref):
    h = pl.program_id(0)
    bucket = bucket_ref[...]
    acc = jnp.full(bucket.shape, NEG_INF, F32)
    for k in range(NBUCKETS):
        acc = jnp.where(bucket == k, rb_ref[k, h] * LOG2E, acc)
    o_ref[0] = acc


def _bias_table(rel_bias, bucket_t):
    return pl.pallas_call(
        _bias_kernel,
        grid=(NH,),
        in_specs=[
            pl.BlockSpec(memory_space=pltpu.SMEM),
            pl.BlockSpec((3 * BLK, BLK), lambda h: (0, 0)),
        ],
        out_specs=pl.BlockSpec((1, 3 * BLK, BLK), lambda h: (h // GQA, 0, h % GQA)),
        out_shape=jax.ShapeDtypeStruct((NKV, 3 * BLK, GQA * BLK), F32),
        compiler_params=_cparams(("arbitrary",)),
        name="rel_bias_table",
    )(rel_bias, bucket_t)


def _qkv_kernel(x_ref, mod_ref, gain_ref, wqt_ref, wk_ref, wvt_ref, qg_ref, kg_ref, qt_ref, k_ref, vt_ref):
    m = mod_ref[0, 0]
    h = _modulate(x_ref[...], gain_ref[0], m[0:1], m[1:2]).astype(BF16)
    nt = (((1,), (1,)), ((), ()))
    qt = lax.dot_general(wqt_ref[...], h, nt, preferred_element_type=F32)
    tm = qt.shape[1]
    q3 = qt.reshape(NH, HD, tm)
    q3 = q3 * lax.rsqrt(jnp.mean(q3 * q3, axis=1, keepdims=True) + EPS)
    qt_ref[...] = (q3.reshape(NH * HD, tm) * qg_ref[...]).astype(BF16)
    k = jnp.dot(h, wk_ref[...], preferred_element_type=F32)
    for g in range(NKV):
        kg = k[:, g * KPAD:(g + 1) * KPAD]
        ms = jnp.sum(kg * kg, axis=-1, keepdims=True) * (1.0 / HD)
        k_ref[:, g * KPAD:(g + 1) * KPAD] = (kg * lax.rsqrt(ms + EPS) * kg_ref[...]).astype(BF16)
    vt_ref[...] = lax.dot_general(wvt_ref[...], h, nt, preferred_element_type=F32).astype(BF16)


def _attn_kernel(qt_ref, kp_ref, kc_ref, kn_ref, vp_ref, vc_ref, vn_ref, x_ref, w_ref, mod_ref,
                 bias_ref, sink_ref, gf_ref, wr_ref, o_ref, h_ref, aff_ref, att_ref, s_ref):
    i = pl.program_id(0)
    kwin = jnp.concatenate([kp_ref[...], kc_ref[...], kn_ref[...]], axis=0)
    vwin = jnp.concatenate([vp_ref[...], vc_ref[...], vn_ref[...]], axis=1)
    ones_rows = (lax.broadcasted_iota(jnp.int32, (VROWS - HD, KWIN), 0) == 0).astype(BF16)
    vaug = [jnp.concatenate([vwin[g * HD:(g + 1) * HD], ones_rows], axis=0) for g in range(NKV)]
    key_pos = i * TQ - BLK + lax.broadcasted_iota(jnp.int32, (KWIN, 1), 0)
    key_mask = jnp.where((key_pos >= 0) & (key_pos < S), 0.0, NEG_INF).astype(BF16)
    lane = lax.broadcasted_iota(jnp.int32, (1, NKV * KPAD), 1)
    kwin = jnp.where(lane % KPAD == HD, key_mask, kwin)
    q_ones = (lax.broadcasted_iota(jnp.int32, (KPAD - HD, GQA * BLK), 0) == 0).astype(BF16)
    items = [(g, r) for g in range(NKV) for r in range(NSB)]

    def scores(g, r):
        kr = kwin[r * BLK:r * BLK + 3 * BLK, g * KPAD:(g + 1) * KPAD]
        qg = jnp.concatenate(
            [qt_ref[(GQA * g + hh) * HD:(GQA * g + hh + 1) * HD, r * BLK:(r + 1) * BLK] for hh in range(GQA)],
            axis=1)
        qa = jnp.concatenate([qg, q_ones], axis=0)
        return jnp.dot(kr, qa, preferred_element_type=F32)

    def probs(s, g, r):
        sink = sink_ref[g] * LOG2E
        s = s + bias_ref[g]
        mx = jnp.maximum(jnp.max(s, axis=0, keepdims=True), sink)
        return jnp.exp2(s - mx).astype(BF16), jnp.exp2(sink - mx)

    nslot = ATT_AHEAD + 1
    for n in range(ATT_AHEAD):
        s_ref[n % nslot] = scores(*items[n])
    for n, (g, r) in enumerate(items):
        if n + ATT_AHEAD < len(items):
            s_ref[(n + ATT_AHEAD) % nslot] = scores(*items[n + ATT_AHEAD])
        p, psink = probs(s_ref[n % nslot], g, r)
        ot = jnp.dot(vaug[g][:, r * BLK:r * BLK + 3 * BLK], p, preferred_element_type=F32)
        ot = ot[:HD] * (1.0 / (ot[HD:HD + 1] + psink))
        for hh in range(GQA):
            hd0 = (GQA * g + hh) * HD
            att_ref[hd0:hd0 + HD, r * BLK:(r + 1) * BLK] = ot[:, hh * BLK:(hh + 1) * BLK]
    g1 = mod_ref[0, 0][2:3]
    att = att_ref[...].T.astype(BF16)
    y = jnp.dot(att, w_ref[...], preferred_element_type=F32)
    xn = x_ref[...] + g1 * y
    o_ref[...] = xn
    h_ref[...], aff_ref[...] = _route(xn, mod_ref[0, 0], gf_ref[0], wr_ref[...])


def _attn_weights(w_qkv, q_gain, k_gain, sink):
    wq, wk, wv = w_qkv[:, :NH * HD], w_qkv[:, NH * HD:(NH + NKV) * HD], w_qkv[:, (NH + NKV) * HD:]
    wqt = wq.T.astype(BF16)
    wvt = wv.T.astype(BF16)
    wk_pad = jnp.pad(wk.reshape(D, NKV, HD), ((0, 0), (0, 0), (0, KPAD - HD))).reshape(D, NKV * KPAD).astype(BF16)
    qg_col = jnp.tile(q_gain * (HD ** -0.5 * LOG2E), NH).reshape(NH * HD, 1)
    kg_row = jnp.pad(k_gain, (0, KPAD - HD)).reshape(1, KPAD)
    sink_row = jnp.repeat(sink, BLK).reshape(NKV, 1, GQA * BLK)
    return wqt, wk_pad, wvt, qg_col, kg_row, sink_row


def _attn_layer(x, b, mod_all, gain, gain_ffn, wr, aw, w_out_bf, bias_tab, layer):
    wqt, wk_pad, wvt, qg_col, kg_row, sink_row = aw
    qt, k, vt = pl.pallas_call(
        _qkv_kernel,
        grid=(S // QKV_TM,),
        in_specs=[
            _x_spec(x, b, QKV_TM),
            _mod_spec(layer, b), _gain_spec(layer),
            pl.BlockSpec((NH * HD, D), lambda i: (0, 0)),
            pl.BlockSpec((D, NKV * KPAD), lambda i: (0, 0)),
            pl.BlockSpec((NKV * HD, D), lambda i: (0, 0)),
            pl.BlockSpec((NH * HD, 1), lambda i: (0, 0)),
            pl.BlockSpec((1, KPAD), lambda i: (0, 0)),
        ],
        out_specs=[
            pl.BlockSpec((NH * HD, QKV_TM), lambda i: (0, i)),
            pl.BlockSpec((QKV_TM, NKV * KPAD), lambda i: (i, 0)),
            pl.BlockSpec((NKV * HD, QKV_TM), lambda i: (0, i)),
        ],
        out_shape=[
            jax.ShapeDtypeStruct((NH * HD, S), BF16),
            jax.ShapeDtypeStruct((S, NKV * KPAD), BF16),
            jax.ShapeDtypeStruct((NKV * HD, S), BF16),
        ],
        compiler_params=_cparams(("arbitrary",)),
        name="attn_qkv",
    )(x, mod_all, gain, wqt, wk_pad, wvt, qg_col, kg_row)

    kw = NKV * KPAD
    vw = NKV * HD
    kprev = pl.BlockSpec((BLK, kw), lambda i: (jnp.maximum(i * NSB - 1, 0), 0))
    kcur = pl.BlockSpec((TQ, kw), lambda i: (i, 0))
    knext = pl.BlockSpec((BLK, kw), lambda i: (jnp.minimum((i + 1) * NSB, NBLK - 1), 0))
    vprev = pl.BlockSpec((vw, BLK), lambda i: (0, jnp.maximum(i * NSB - 1, 0)))
    vcur = pl.BlockSpec((vw, TQ), lambda i: (0, i))
    vnext = pl.BlockSpec((vw, BLK), lambda i: (0, jnp.minimum((i + 1) * NSB, NBLK - 1)))
    return pl.pallas_call(
        _attn_kernel,
        grid=(S // TQ,),
        in_specs=[
            pl.BlockSpec((NH * HD, TQ), lambda i: (0, i)),
            kprev, kcur, knext, vprev, vcur, vnext,
            _x_spec(x, b, TQ),
            pl.BlockSpec((NH * HD, D), lambda i: (0, 0)),
            _mod_spec(layer, b),
            pl.BlockSpec((NKV, 3 * BLK, GQA * BLK), lambda i: (0, 0, 0)),
            pl.BlockSpec((NKV, 1, GQA * BLK), lambda i: (0, 0, 0)),
            _gain_spec(layer),
            pl.BlockSpec((D, 2 * EP), lambda i: (0, 0)),
        ],
        out_specs=_routed_specs(TQ),
        out_shape=_ROUTED_SHAPES,
        scratch_shapes=[pltpu.VMEM((NH * HD, TQ), F32), pltpu.VMEM((ATT_AHEAD + 1, 3 * BLK, GQA * BLK), F32)],
        compiler_params=_cparams(("arbitrary",)),
        name="attn_core",
    )(qt, k, k, k, vt, vt, vt, x, w_out_bf, mod_all, bias_tab, sink_row, gain_ffn, wr)


FFN_TF = 1024
FFN_RT = 512
FFN_WT = 256


def _select_kernel(aff_ref, tri_ref, pos_ref):
    aff = aff_ref[...]
    bits = pltpu.bitcast(aff, jnp.int32)
    rows = aff.shape[0]

    def count_ge(v):
        return jnp.sum((bits >= v).astype(jnp.int32), axis=1, keepdims=True)

    def body(t, cur):
        sh = 28 - 2 * t
        c1, c2, c3 = cur | (jnp.int32(1) << sh), cur | (jnp.int32(2) << sh), cur | (jnp.int32(3) << sh)
        n1, n2, n3 = count_ge(c1), count_ge(c2), count_ge(c3)
        return jnp.where(n3 >= CAP, c3, jnp.where(n2 >= CAP, c2, jnp.where(n1 >= CAP, c1, cur)))

    top = jnp.full((rows, 1), 1 << 30, jnp.int32)
    thr = lax.fori_loop(0, 15, body, jnp.where(count_ge(top) >= CAP, top, 0))
    gt = bits > thr
    eq = bits == thr
    need = CAP - jnp.sum(gt.astype(jnp.int32), axis=1, keepdims=True)
    tri = tri_ref[...]

    def cumsum_excl(mask_f):
        off = jnp.zeros((mask_f.shape[0], 1), F32)
        outs = []
        for j in range(S // 128):
            mj = mask_f[:, j * 128:(j + 1) * 128]
            loc = jnp.dot(mj.astype(BF16), tri, preferred_element_type=F32)
            outs.append(loc - mj + off)
            off = off + loc[:, 127:128]
        return jnp.concatenate(outs, axis=1)

    counts = cumsum_excl(jnp.concatenate([gt, eq], axis=0).astype(F32))
    gt_before, eq_before = counts[:rows], counts[rows:]
    needf = need.astype(F32)
    sel = gt | (eq & (eq_before < needf))
    pos = gt_before + jnp.minimum(eq_before, needf)
    pos_ref[...] = jnp.where(sel, pos.astype(jnp.int32), -1)


SC_CORES = 2
SC_SUBCORES = 16
SC_TILES = SC_CORES * SC_SUBCORES
DISPATCH_SLOTS = CAP * E // SC_TILES
DISPATCH_ROWS = 64
COMBINE_RANGE = 1024
COMBINE_ROWS = 32
SLAB = 128
NSLAB = D // SLAB


def _sc_mesh():
    return plsc.VectorSubcoreMesh(core_axis_name="c", subcore_axis_name="s",
                                  num_cores=SC_CORES, num_subcores=SC_SUBCORES)


def _dispatch_body(pos_hbm, aff_hbm, h_hbm, xin_hbm, idx_hbm, gate_hbm, pos_v, aff_v, idx_v, gate_v,
                   rows_a, rows_b, gsem_a, gsem_b, wsem_a, wsem_b):
    w = lax.axis_index("s") * SC_CORES + lax.axis_index("c")
    e = w // 2
    lo = (w % 2) * DISPATCH_SLOTS
    pltpu.sync_copy(pos_hbm.at[e], pos_v)
    pltpu.sync_copy(aff_hbm.at[e], aff_v)

    @pl.loop(0, S // 16)
    def _(i):
        p = pos_v[pl.ds(i * 16, 16)] - lo
        m = (p >= 0) & (p < DISPATCH_SLOTS)
        tok = lax.iota(jnp.int32, 16) + i * 16
        plsc.store_scatter(idx_v, [p], tok, mask=m)
        plsc.store_scatter(gate_v, [p], aff_v[pl.ds(i * 16, 16)], mask=m)

    pltpu.sync_copy(idx_v, idx_hbm.at[e, pl.ds(lo, DISPATCH_SLOTS)])
    pltpu.sync_copy(gate_v, gate_hbm.at[e, pl.ds(lo, DISPATCH_SLOTS)])

    bufs, gsems, wsems = (rows_a, rows_b), (gsem_a, gsem_b), (wsem_a, wsem_b)
    nchunk = DISPATCH_SLOTS // DISPATCH_ROWS

    def gather(j):
        return pltpu.async_copy(h_hbm.at[idx_v.at[pl.ds(j * DISPATCH_ROWS, DISPATCH_ROWS)]], bufs[j % 2], gsems[j % 2])

    pending_gather = gather(0)
    writes = [None, None]
    for j in range(nchunk):
        pending_gather.wait()
        writes[j % 2] = pltpu.async_copy(
            bufs[j % 2], xin_hbm.at[e, pl.ds(lo + j * DISPATCH_ROWS, DISPATCH_ROWS)], wsems[j % 2])
        if j + 1 < nchunk:
            if writes[(j + 1) % 2] is not None:
                writes[(j + 1) % 2].wait()
            pending_gather = gather(j + 1)
    writes[(nchunk - 2) % 2].wait()
    writes[(nchunk - 1) % 2].wait()


def _dispatch(pos, aff, h):
    return pl.kernel(
        _dispatch_body, mesh=_sc_mesh(),
        out_type=[jax.ShapeDtypeStruct((E, CAP, D // 2), jnp.int32),
                  jax.ShapeDtypeStruct((E, CAP), jnp.int32),
                  jax.ShapeDtypeStruct((E, CAP), F32)],
        scratch_types=[pltpu.VMEM((S,), jnp.int32), pltpu.VMEM((S,), F32),
                       pltpu.VMEM((DISPATCH_SLOTS,), jnp.int32), pltpu.VMEM((DISPATCH_SLOTS,), F32),
                       pltpu.VMEM((DISPATCH_ROWS, D // 2), jnp.int32), pltpu.VMEM((DISPATCH_ROWS, D // 2), jnp.int32),
                       pltpu.SemaphoreType.DMA, pltpu.SemaphoreType.DMA,
                       pltpu.SemaphoreType.DMA, pltpu.SemaphoreType.DMA],
        compiler_params=pltpu.CompilerParams(needs_layout_passes=False),
        name="moe_dispatch",
    )(pos, aff, h)


def _combine_body(e0, ne, y_hbm, idx_hbm, x_hbm, out_hbm, idx_v, li_v, *bufs):
    rows = bufs[:NSLAB]
    accs = bufs[NSLAB:]
    c = lax.axis_index("c")
    s = lax.axis_index("s")
    share = COMBINE_RANGE // SC_SUBCORES
    per_expert = SC_SUBCORES // ne
    nslot = CAP // per_expert
    el = s // per_expert
    lo = (s % per_expert) * nslot
    pltpu.sync_copy(idx_hbm.at[e0 + el, pl.ds(lo, nslot)], idx_v.at[pl.ds(0, nslot)])
    lane = lax.iota(jnp.int32, 16)

    @pl.loop(0, S // COMBINE_RANGE // SC_CORES)
    def _(r):
        t0 = (r * SC_CORES + c) * COMBINE_RANGE
        row0 = t0 + s * share
        pltpu.sync_copy(tuple(x_hbm.at[pl.ds(row0, share), pl.ds(q * SLAB, SLAB)] for q in range(NSLAB)),
                        tuple(accs[q].at[pl.ds(s * share, share)] for q in range(NSLAB)))
        plsc.subcore_barrier()

        @pl.loop(0, nslot // COMBINE_ROWS)
        def _(j):
            hits = jnp.zeros((16,), jnp.int32)
            for v in range(COMBINE_ROWS // 16):
                t = idx_v[pl.ds(j * COMBINE_ROWS + v * 16, 16)] - t0
                ok = (t >= 0) & (t < COMBINE_RANGE)
                li_v[pl.ds(v * 16, 16)] = jnp.where(ok, t, COMBINE_RANGE + lane)
                hits = hits + plsc.all_reduce_population_count(ok)

            @pl.when(jnp.max(hits) > 0)
            def _():
                pltpu.sync_copy(
                    tuple(y_hbm.at[el, pl.ds(lo + j * COMBINE_ROWS, COMBINE_ROWS), pl.ds(q * SLAB, SLAB)]
                          for q in range(NSLAB)),
                    tuple(rows))
                pltpu.sync_copy(tuple(rows), tuple(accs[q].at[li_v] for q in range(NSLAB)), add=True)

        plsc.subcore_barrier()
        pltpu.sync_copy(tuple(accs[q].at[pl.ds(s * share, share)] for q in range(NSLAB)),
                        tuple(out_hbm.at[pl.ds(row0, share), pl.ds(q * SLAB, SLAB)] for q in range(NSLAB)))


def _combine(y, idx, x, out_rows=S, e0=0):
    return pl.kernel(
        functools.partial(_combine_body, e0, y.shape[0]), mesh=_sc_mesh(),
        out_type=jax.ShapeDtypeStruct((out_rows, D), F32),
        scratch_types=[pltpu.VMEM((CAP,), jnp.int32), pltpu.VMEM((COMBINE_ROWS,), jnp.int32)]
        + [pltpu.VMEM((COMBINE_ROWS, SLAB), F32) for _ in range(NSLAB)]
        + [pltpu.VMEM_SHARED((COMBINE_RANGE + 16, SLAB), F32) for _ in range(NSLAB)],
        compiler_params=pltpu.CompilerParams(needs_layout_passes=False),
        name="moe_combine",
    )(y, idx, x)


def _ffn_kernel(e0, x_ref, wg_ref, wu_ref, wd_ref, gate_ref, mod_ref, o_ref):
    f = pl.program_id(1)
    last = pl.num_programs(1) - 1
    nt = FFN_TF // FFN_WT
    wg = [wg_ref[0, 0, :, j * FFN_WT:(j + 1) * FFN_WT].astype(BF16) for j in range(nt)]
    wu = [wu_ref[0, 0, :, j * FFN_WT:(j + 1) * FFN_WT].astype(BF16) for j in range(nt)]
    wd = wd_ref[0, 0].astype(BF16)
    is_first = f == 0
    is_last = f == last
    g2 = jnp.where(is_last, mod_ref[0, 0][5:6], 1.0)
    g_row = gate_ref[pl.ds(e0 + pl.program_id(0), 1), :]
    g_col = jnp.broadcast_to(g_row, (128, CAP)).T[:, 0:1]
    for r in range(CAP // FFN_RT):
        rows = slice(r * FFN_RT, (r + 1) * FFN_RT)
        xr = _unpack_bf16_pairs(x_ref[0, rows, :])
        acts = []
        for j in range(nt):
            g = jnp.dot(xr, wg[j], preferred_element_type=F32)
            u = jnp.dot(xr, wu[j], preferred_element_type=F32)
            acts.append((g * _sigmoid(g) * u).astype(BF16))
        y = jnp.dot(jnp.concatenate(acts, axis=1), wd, preferred_element_type=F32)
        prev = jnp.where(is_first, 0.0, o_ref[0, rows, :])
        gate = jnp.where(is_last, g_col[rows, :], 1.0)
        o_ref[0, rows, :] = (prev + y) * gate * g2


def _select(aff, tri):
    return pl.pallas_call(
        _select_kernel,
        grid=(1,),
        in_specs=[
            pl.BlockSpec((E, S), lambda i: (0, 0)),
            pl.BlockSpec((128, 128), lambda i: (0, 0)),
        ],
        out_specs=pl.BlockSpec((E, S), lambda i: (0, 0)),
        out_shape=jax.ShapeDtypeStruct((E, S), jnp.int32),
        compiler_params=_cparams(("arbitrary",)),
        name="moe_select",
    )(aff, tri)


def _ffn(xin, gate, b, mod_all, w_gate, w_up, w_down, layer, e0=0, ne=E):
    return pl.pallas_call(
        functools.partial(_ffn_kernel, e0),
        grid=(ne, F // FFN_TF),
        in_specs=[
            pl.BlockSpec((1, CAP, D // 2), lambda e, f: (e0 + e, 0, 0)),
            pl.BlockSpec((1, 1, D, FFN_TF), lambda e, f: (layer, e0 + e, 0, f)),
            pl.BlockSpec((1, 1, D, FFN_TF), lambda e, f: (layer, e0 + e, 0, f)),
            pl.BlockSpec((1, 1, FFN_TF, D), lambda e, f: (layer, e0 + e, f, 0)),
            pl.BlockSpec((E, CAP), lambda e, f: (0, 0)),
            pl.BlockSpec((1, 1, 6, D), lambda e, f: (layer, b, 0, 0)),
        ],
        out_specs=pl.BlockSpec((1, CAP, D), lambda e, f: (e, 0, 0)),
        out_shape=jax.ShapeDtypeStruct((ne, CAP, D), F32),
        compiler_params=_cparams(("arbitrary", "arbitrary"), 48),
        name="moe_ffn",
    )(xin, w_gate, w_up, w_down, gate, mod_all)


def kernel(x, c, w_ada, b_ada, norm_mix, norm_ffn, w_fourier_out, w_qkv, w_attn_out, q_gain, k_gain,
           sink, rel_bias, w_router, w_gate, w_up, w_down):
    m0, m2 = _dft_tables()
    m1 = _stage1_table()
    bucket_t = _bucket_table()
    tri = jnp.asarray(np.triu(np.ones((128, 128), np.float32)), BF16)
    mod_all = _ada(c, w_ada, b_ada)
    bias_tab = _bias_table(rel_bias, bucket_t)
    gain_mix = norm_mix.reshape(DEPTH, 1, D)
    gain_ffn = norm_ffn.reshape(DEPTH, 1, D)
    xs = [x, x]
    for layer in range(DEPTH):
        j = layer // 2
        wr = jnp.pad(w_router[layer], ((0, 0), (0, EP - E)))
        wr1 = wr.astype(BF16)
        wr = jnp.concatenate([wr1, (wr - wr1.astype(F32)).astype(BF16)], axis=1)
        if layer % 2 == 0:
            w_out_bf = w_fourier_out[j].astype(BF16)
            routed = [_fourier_layer(xs[b], b, mod_all, gain_mix, gain_ffn, wr, w_out_bf, layer, m0, m1, m2)
                      for b in range(B)]
        else:
            aw = _attn_weights(w_qkv[j], q_gain[j], k_gain[j], sink[j])
            w_out_bf = w_attn_out[j].astype(BF16)
            routed = [_attn_layer(xs[b], b, mod_all, gain_mix, gain_ffn, wr, aw, w_out_bf, bias_tab, layer)
                      for b in range(B)]
        xs = [routed[b][0] for b in range(B)]
        pos = [_select(routed[b][2], tri) for b in range(B)]
        disp = [_dispatch(pos[b], routed[b][2], routed[b][1]) for b in range(B)]
        if layer < DEPTH - 1:
            ys = [_ffn(disp[b][0], disp[b][2], b, mod_all, w_gate, w_up, w_down, layer) for b in range(B)]
            xs = [_combine(ys[b], disp[b][1], xs[b]) for b in range(B)]
    ffn = lambda b, e0, ne: _ffn(disp[b][0], disp[b][2], b, mod_all, w_gate, w_up, w_down, DEPTH - 1, e0, ne)
    out = _combine(ffn(0, 0, E), disp[0][1], xs[0], out_rows=B * S)
    for b in range(1, B):
        half = _combine(ffn(b, 0, E // 2), disp[b][1], xs[b])
        last = _combine(ffn(b, E // 2, E // 2), disp[b][1], half, e0=E // 2)
        out = lax.dynamic_update_slice(out, last, (b * S, 0))
    return out.reshape(B, S, D)
```

```python
import math

import numpy as np
import jax
import jax.numpy as jnp
from jax import lax
from jax.experimental import pallas as pl
from jax.experimental.pallas import tpu as pltpu
from jax.experimental.pallas import tpu_sc as plsc

D = 1024
B = 2
S = 8192
DEPTH = 4
GROUPS = 4
GD = D // GROUPS
HD = 64
NH = 16
NKV = 4
GQA = NH // NKV
WINDOW = 128
BLK = 128
NBUCKETS = 32
MAXDIST = 128
E = 16
CAP = 2 * S // E
F = 2 * D
EPS = 1e-6
NEG_INF = -1e30

N1 = 128
N2 = 64
INV_NORM = 1.0 / math.sqrt(S * GD)

F32 = jnp.float32
BF16 = jnp.bfloat16


def _cparams(sem, vmem_mb=48):
    return pltpu.CompilerParams(dimension_semantics=sem, vmem_limit_bytes=vmem_mb * 1024 * 1024)


def _dft_tables():
    c = np.arange(GD)
    ang0 = 2.0 * np.pi * ((c[:, None] * c[None, :]) % GD) / GD
    m0 = np.concatenate([np.cos(ang0), -np.sin(ang0)], axis=1)
    k2 = np.arange(N2)
    ang2 = 2.0 * np.pi * ((k2[:, None] * k2[None, :]) % N2) / N2
    m2 = np.stack([np.cos(ang2), np.sin(ang2)], axis=2).reshape(N2, 2 * N2)
    return jnp.asarray(m0, BF16), jnp.asarray(m2, BF16)


def _stage1_table():
    s2 = np.arange(N2)[:, None, None]
    k1 = np.arange(N1)[None, :, None]
    s1 = np.arange(N1)[None, None, :]
    th = ((k1 * (N2 * s1 + s2)) % S) * (2.0 * np.pi / S)
    co, si = np.cos(th), np.sin(th)
    top = np.concatenate([co, si], axis=2)
    bot = np.concatenate([-si, co], axis=2)
    return jnp.asarray(np.concatenate([top, bot], axis=1).astype(np.float32), BF16)


def _bucket_table():
    q_off = np.arange(BLK)
    k_off = np.arange(3 * BLK) - BLK
    rel = k_off[:, None] - q_off[None, :]
    half = NBUCKETS // 2
    max_exact = half // 2
    ret = np.where(rel > 0, half, 0)
    n = np.abs(rel)
    nf = np.maximum(n, 1).astype(np.float32)
    ratio = (np.log(nf / np.float32(max_exact)) / np.float32(math.log(MAXDIST / max_exact))).astype(np.float32)
    large = max_exact + (ratio * np.float32(half - max_exact)).astype(np.int32)
    large = np.minimum(large, half - 1)
    bucket = ret + np.where(n < max_exact, n, large)
    return jnp.asarray(np.where(np.abs(rel) <= WINDOW, bucket, -1).astype(np.int32))


def _modulate(x, gain, shift, scale):
    ms = jnp.mean(x * x, axis=-1, keepdims=True)
    return x * lax.rsqrt(ms + EPS) * (gain * (1.0 + scale)) + shift


def _sigmoid(x):
    return 1.0 / (1.0 + jnp.exp(-x))


ADA_TN = 1536


def _ada_kernel(ct_ref, w_ref, b_ref, o_ref):
    ct = ct_ref[...]
    ca = ct * _sigmoid(ct)
    w = w_ref[0]
    for b in range(B):
        o_ref[0, b:b + 1, :] = jnp.sum(w * ca[:, b:b + 1], axis=0, keepdims=True) + b_ref[0]


def _ada(c, w_ada, b_ada):
    out = pl.pallas_call(
        _ada_kernel,
        grid=(DEPTH, 6 * D // ADA_TN),
        in_specs=[
            pl.BlockSpec((D, B), lambda l, j: (0, 0)),
            pl.BlockSpec((1, D, ADA_TN), lambda l, j: (l, 0, j)),
            pl.BlockSpec((1, 1, ADA_TN), lambda l, j: (l, 0, j)),
        ],
        out_specs=pl.BlockSpec((1, B, ADA_TN), lambda l, j: (l, 0, j)),
        out_shape=jax.ShapeDtypeStruct((DEPTH, B, 6 * D), F32),
        compiler_params=_cparams(("arbitrary", "arbitrary"), 32),
        name="ada_mod",
    )(c.T, w_ada, b_ada.reshape(DEPTH, 1, 6 * D))
    return out.reshape(DEPTH, B, 6, D)


def _mod_spec(layer, b):
    return pl.BlockSpec((1, 1, 6, D), lambda *_: (layer, b, 0, 0))


def _gain_spec(layer):
    return pl.BlockSpec((1, 1, D), lambda *_: (layer, 0, 0))


def _x_spec(x, b, tm):
    if x.ndim == 3:
        return pl.BlockSpec((None, tm, D), lambda i: (b, i, 0))
    return pl.BlockSpec((tm, D), lambda i: (i, 0))


EP = 128

_ROUTED_SHAPES = [jax.ShapeDtypeStruct((S, D), F32),
                  jax.ShapeDtypeStruct((S, D // 2), jnp.int32),
                  jax.ShapeDtypeStruct((E, S), F32)]


def _routed_specs(tm):
    return [pl.BlockSpec((tm, D), lambda i: (i, 0)),
            pl.BlockSpec((tm, D // 2), lambda i: (i, 0)),
            pl.BlockSpec((E, tm), lambda i: (0, i))]


F0_TM = 1024
SUB = 8
SEQ_PICKS = 2 * SUB


def _pack_bf16_pairs(h):
    hb = h.astype(BF16).astype(F32)
    lo = pltpu.bitcast(hb[:, :D // 2], jnp.int32)
    hi = pltpu.bitcast(hb[:, D // 2:], jnp.int32)
    return hi | lax.shift_right_logical(lo, jnp.int32(16))


def _unpack_bf16_pairs(xp):
    lo = pltpu.bitcast(xp << 16, F32).astype(BF16)
    hi = pltpu.bitcast(xp & jnp.int32(-65536), F32).astype(BF16)
    return jnp.concatenate([lo, hi], axis=1)


def _route(xn, m, gain_ffn, wr):
    h = _modulate(xn, gain_ffn, m[3:4], m[4:5])
    h1 = h.astype(BF16)
    h2 = (h - h1.astype(F32)).astype(BF16)
    part = jnp.dot(h1, wr, preferred_element_type=F32)
    logits = part[:, :EP] + part[:, EP:] + jnp.dot(h2, wr[:, :EP], preferred_element_type=F32)
    lt = logits.T[:E]
    ex = jnp.exp(lt - jnp.max(lt, axis=0, keepdims=True))
    return _pack_bf16_pairs(h), ex / jnp.sum(ex, axis=0, keepdims=True)


def _f0_kernel(x_ref, mod_ref, gain_ref, m0_ref, o_ref):
    m = mod_ref[0, 0]
    h = _modulate(x_ref[...], gain_ref[0], m[0:1], m[1:2]).astype(BF16)
    m0 = m0_ref[...]
    r = [jnp.dot(h[:, g * GD:(g + 1) * GD], m0, preferred_element_type=F32) for g in range(GROUPS)]
    o_ref[0] = _pack_bf16_pairs(jnp.concatenate([rg[:, :GD] for rg in r], axis=1))
    o_ref[1] = _pack_bf16_pairs(jnp.concatenate([rg[:, GD:] for rg in r], axis=1))


def _f1_kernel(w_ref, m1_ref, o_ref, scr_ref):
    n = o_ref.shape[0]
    for j in range(n):
        scr_ref[j] = w_ref[:, j, :]
    for j in range(n):
        w = _unpack_bf16_pairs(scr_ref[j])
        o_ref[j] = _pack_bf16_pairs(jnp.dot(m1_ref[j], w, preferred_element_type=F32))


def _f2_kernel(u_ref, m2_ref, o_ref, scr_ref):
    m2 = m2_ref[...]
    n = o_ref.shape[0]
    for k in range(n):
        scr_ref[k] = u_ref[:, k, :]
    for k in range(n):
        u = _unpack_bf16_pairs(scr_ref[k])
        o_ref[k] = _pack_bf16_pairs(jnp.dot(m2, u, preferred_element_type=F32))


def _f3_kernel(mp_ref, x_ref, w_ref, mod_ref, gf_ref, wr_ref, o_ref, h_ref, aff_ref, scr_ref):
    g1 = mod_ref[0, 0][2:3]
    for j in range(SUB):
        scr_ref[j * N1:(j + 1) * N1, :] = mp_ref[:, j, :]
    a = _unpack_bf16_pairs(scr_ref[...])
    y = jnp.dot(a, w_ref[...], preferred_element_type=F32)
    xn = x_ref[...] + (g1 * INV_NORM) * y
    o_ref[...] = xn
    h_ref[...], aff_ref[...] = _route(xn, mod_ref[0, 0], gf_ref[0], wr_ref[...])


def _fourier_layer(x, b, mod_all, gain, gain_ffn, wr, w_out_bf, layer, m0, m1, m2):
    wc = pl.pallas_call(
        _f0_kernel,
        grid=(S // F0_TM,),
        in_specs=[
            _x_spec(x, b, F0_TM),
            _mod_spec(layer, b), _gain_spec(layer),
            pl.BlockSpec((GD, 2 * GD), lambda i: (0, 0)),
        ],
        out_specs=pl.BlockSpec((2, F0_TM, D // 2), lambda i: (0, i, 0)),
        out_shape=jax.ShapeDtypeStruct((2, S, D // 2), jnp.int32),
        compiler_params=_cparams(("arbitrary",)),
        name="fourier_chan",
    )(x, mod_all, gain, m0)
    wc = wc.reshape(2 * N1, N2, D // 2)
    u = pl.pallas_call(
        _f1_kernel,
        grid=(N2 // SEQ_PICKS,),
        in_specs=[
            pl.BlockSpec((2 * N1, SEQ_PICKS, D // 2), lambda i: (0, i, 0)),
            pl.BlockSpec((SEQ_PICKS, 2 * N1, 2 * N1), lambda i: (i, 0, 0)),
        ],
        out_specs=pl.BlockSpec((SEQ_PICKS, 2 * N1, D // 2), lambda i: (i, 0, 0)),
        out_shape=jax.ShapeDtypeStruct((N2, 2 * N1, D // 2), jnp.int32),
        scratch_shapes=[pltpu.VMEM((SEQ_PICKS, 2 * N1, D // 2), jnp.int32)],
        compiler_params=_cparams(("arbitrary",)),
        name="fourier_seq1",
    )(wc, m1)
    u = u.reshape(2 * N2, N1, D // 2)
    mp = pl.pallas_call(
        _f2_kernel,
        grid=(N1 // SEQ_PICKS,),
        in_specs=[
            pl.BlockSpec((2 * N2, SEQ_PICKS, D // 2), lambda i: (0, i, 0)),
            pl.BlockSpec((N2, 2 * N2), lambda i: (0, 0)),
        ],
        out_specs=pl.BlockSpec((SEQ_PICKS, N2, D // 2), lambda i: (i, 0, 0)),
        out_shape=jax.ShapeDtypeStruct((N1, N2, D // 2), jnp.int32),
        scratch_shapes=[pltpu.VMEM((SEQ_PICKS, 2 * N2, D // 2), jnp.int32)],
        compiler_params=_cparams(("arbitrary",)),
        name="fourier_seq2",
    )(u, m2)
    return pl.pallas_call(
        _f3_kernel,
        grid=(N2 // SUB,),
        in_specs=[
            pl.BlockSpec((N1, SUB, D // 2), lambda i: (0, i, 0)),
            _x_spec(x, b, SUB * N1),
            pl.BlockSpec((D, D), lambda i: (0, 0)),
            _mod_spec(layer, b), _gain_spec(layer),
            pl.BlockSpec((D, 2 * EP), lambda i: (0, 0)),
        ],
        out_specs=_routed_specs(SUB * N1),
        out_shape=_ROUTED_SHAPES,
        scratch_shapes=[pltpu.VMEM((SUB * N1, D // 2), jnp.int32)],
        compiler_params=_cparams(("arbitrary",)),
        name="fourier_out",
    )(mp, x, w_out_bf, mod_all, gain_ffn, wr)


QKV_TM = 1024
TQ = 1024
NSB = TQ // BLK
KWIN = TQ + 2 * BLK
NBLK = S // BLK
KPAD = 128
VROWS = HD + 16
ATT_AHEAD = 1
LOG2E = math.log2(math.e)


def _bias_kernel(rb_ref, bucket_ref, o_ref):
    h = pl.program_id(0)
    bucket = bucket_ref[...]
    acc = jnp.full(bucket.shape, NEG_INF, F32)
    for k in range(NBUCKETS):
        acc = jnp.where(bucket == k, rb_ref[k, h] * LOG2E, acc)
    o_ref[0] = acc


def _bias_table(rel_bias, bucket_t):
    return pl.pallas_call(
        _bias_kernel,
        grid=(NH,),
        in_specs=[
            pl.BlockSpec(memory_space=pltpu.SMEM),
            pl.BlockSpec((3 * BLK, BLK), lambda h: (0, 0)),
        ],
        out_specs=pl.BlockSpec((1, 3 * BLK, BLK), lambda h: (h // GQA, 0, h % GQA)),
        out_shape=jax.ShapeDtypeStruct((NKV, 3 * BLK, GQA * BLK), F32),
        compiler_params=_cparams(("arbitrary",)),
        name="rel_bias_table",
    )(rel_bias, bucket_t)


def _qkv_kernel(x_ref, mod_ref, gain_ref, wqt_ref, wk_ref, wvt_ref, qg_ref, kg_ref, qt_ref, k_ref, vt_ref):
    m = mod_ref[0, 0]
    h = _modulate(x_ref[...], gain_ref[0], m[0:1], m[1:2]).astype(BF16)
    nt = (((1,), (1,)), ((), ()))
    qt = lax.dot_general(wqt_ref[...], h, nt, preferred_element_type=F32)
    tm = qt.shape[1]
    q3 = qt.reshape(NH, HD, tm)
    q3 = q3 * lax.rsqrt(jnp.mean(q3 * q3, axis=1, keepdims=True) + EPS)
    qt_ref[...] = (q3.reshape(NH * HD, tm) * qg_ref[...]).astype(BF16)
    k = jnp.dot(h, wk_ref[...], preferred_element_type=F32)
    for g in range(NKV):
        kg = k[:, g * KPAD:(g + 1) * KPAD]
        ms = jnp.sum(kg * kg, axis=-1, keepdims=True) * (1.0 / HD)
        k_ref[:, g * KPAD:(g + 1) * KPAD] = (kg * lax.rsqrt(ms + EPS) * kg_ref[...]).astype(BF16)
    vt_ref[...] = lax.dot_general(wvt_ref[...], h, nt, preferred_element_type=F32).astype(BF16)


def _attn_kernel(qt_ref, kp_ref, kc_ref, kn_ref, vp_ref, vc_ref, vn_ref, x_ref, w_ref, mod_ref,
                 bias_ref, sink_ref, gf_ref, wr_ref, o_ref, h_ref, aff_ref, att_ref, s_ref):
    i = pl.program_id(0)
    kwin = jnp.concatenate([kp_ref[...], kc_ref[...], kn_ref[...]], axis=0)
    vwin = jnp.concatenate([vp_ref[...], vc_ref[...], vn_ref[...]], axis=1)
    ones_rows = (lax.broadcasted_iota(jnp.int32, (VROWS - HD, KWIN), 0) == 0).astype(BF16)
    vaug = [jnp.concatenate([vwin[g * HD:(g + 1) * HD], ones_rows], axis=0) for g in range(NKV)]
    key_pos = i * TQ - BLK + lax.broadcasted_iota(jnp.int32, (KWIN, 1), 0)
    key_mask = jnp.where((key_pos >= 0) & (key_pos < S), 0.0, NEG_INF).astype(BF16)
    lane = lax.broadcasted_iota(jnp.int32, (1, NKV * KPAD), 1)
    kwin = jnp.where(lane % KPAD == HD, key_mask, kwin)
    q_ones = (lax.broadcasted_iota(jnp.int32, (KPAD - HD, GQA * BLK), 0) == 0).astype(BF16)
    items = [(g, r) for g in range(NKV) for r in range(NSB)]

    def scores(g, r):
        kr = kwin[r * BLK:r * BLK + 3 * BLK, g * KPAD:(g + 1) * KPAD]
        qg = jnp.concatenate(
            [qt_ref[(GQA * g + hh) * HD:(GQA * g + hh + 1) * HD, r * BLK:(r + 1) * BLK] for hh in range(GQA)],
            axis=1)
        qa = jnp.concatenate([qg, q_ones], axis=0)
        return jnp.dot(kr, qa, preferred_element_type=F32)

    def probs(s, g, r):
        sink = sink_ref[g] * LOG2E
        s = s + bias_ref[g]
        mx = jnp.maximum(jnp.max(s, axis=0, keepdims=True), sink)
        return jnp.exp2(s - mx).astype(BF16), jnp.exp2(sink - mx)

    nslot = ATT_AHEAD + 1
    for n in range(ATT_AHEAD):
        s_ref[n % nslot] = scores(*items[n])
    for n, (g, r) in enumerate(items):
        if n + ATT_AHEAD < len(items):
            s_ref[(n + ATT_AHEAD) % nslot] = scores(*items[n + ATT_AHEAD])
        p, psink = probs(s_ref[n % nslot], g, r)
        ot = jnp.dot(vaug[g][:, r * BLK:r * BLK + 3 * BLK], p, preferred_element_type=F32)
        ot = ot[:HD] * (1.0 / (ot[HD:HD + 1] + psink))
        for hh in range(GQA):
            hd0 = (GQA * g + hh) * HD
            att_ref[hd0:hd0 + HD, r * BLK:(r + 1) * BLK] = ot[:, hh * BLK:(hh + 1) * BLK]
    g1 = mod_ref[0, 0][2:3]
    att = att_ref[...].T.astype(BF16)
    y = jnp.dot(att, w_ref[...], preferred_element_type=F32)
    xn = x_ref[...] + g1 * y
    o_ref[...] = xn
    h_ref[...], aff_ref[...] = _route(xn, mod_ref[0, 0], gf_ref[0], wr_ref[...])


def _attn_weights(w_qkv, q_gain, k_gain, sink):
    wq, wk, wv = w_qkv[:, :NH * HD], w_qkv[:, NH * HD:(NH + NKV) * HD], w_qkv[:, (NH + NKV) * HD:]
    wqt = wq.T.astype(BF16)
    wvt = wv.T.astype(BF16)
    wk_pad = jnp.pad(wk.reshape(D, NKV, HD), ((0, 0), (0, 0), (0, KPAD - HD))).reshape(D, NKV * KPAD).astype(BF16)
    qg_col = jnp.tile(q_gain * (HD ** -0.5 * LOG2E), NH).reshape(NH * HD, 1)
    kg_row = jnp.pad(k_gain, (0, KPAD - HD)).reshape(1, KPAD)
    sink_row = jnp.repeat(sink, BLK).reshape(NKV, 1, GQA * BLK)
    return wqt, wk_pad, wvt, qg_col, kg_row, sink_row


def _attn_layer(x, b, mod_all, gain, gain_ffn, wr, aw, w_out_bf, bias_tab, layer):
    wqt, wk_pad, wvt, qg_col, kg_row, sink_row = aw
    qt, k, vt = pl.pallas_call(
        _qkv_kernel,
        grid=(S // QKV_TM,),
        in_specs=[
            _x_spec(x, b, QKV_TM),
            _mod_spec(layer, b), _gain_spec(layer),
            pl.BlockSpec((NH * HD, D), lambda i: (0, 0)),
            pl.BlockSpec((D, NKV * KPAD), lambda i: (0, 0)),
            pl.BlockSpec((NKV * HD, D), lambda i: (0, 0)),
            pl.BlockSpec((NH * HD, 1), lambda i: (0, 0)),
            pl.BlockSpec((1, KPAD), lambda i: (0, 0)),
        ],
        out_specs=[
            pl.BlockSpec((NH * HD, QKV_TM), lambda i: (0, i)),
            pl.BlockSpec((QKV_TM, NKV * KPAD), lambda i: (i, 0)),
            pl.BlockSpec((NKV * HD, QKV_TM), lambda i: (0, i)),
        ],
        out_shape=[
            jax.ShapeDtypeStruct((NH * HD, S), BF16),
            jax.ShapeDtypeStruct((S, NKV * KPAD), BF16),
            jax.ShapeDtypeStruct((NKV * HD, S), BF16),
        ],
        compiler_params=_cparams(("arbitrary",)),
        name="attn_qkv",
    )(x, mod_all, gain, wqt, wk_pad, wvt, qg_col, kg_row)

    kw = NKV * KPAD
    vw = NKV * HD
    kprev = pl.BlockSpec((BLK, kw), lambda i: (jnp.maximum(i * NSB - 1, 0), 0))
    kcur = pl.BlockSpec((TQ, kw), lambda i: (i, 0))
    knext = pl.BlockSpec((BLK, kw), lambda i: (jnp.minimum((i + 1) * NSB, NBLK - 1), 0))
    vprev = pl.BlockSpec((vw, BLK), lambda i: (0, jnp.maximum(i * NSB - 1, 0)))
    vcur = pl.BlockSpec((vw, TQ), lambda i: (0, i))
    vnext = pl.BlockSpec((vw, BLK), lambda i: (0, jnp.minimum((i + 1) * NSB, NBLK - 1)))
    return pl.pallas_call(
        _attn_kernel,
        grid=(S // TQ,),
        in_specs=[
            pl.BlockSpec((NH * HD, TQ), lambda i: (0, i)),
            kprev, kcur, knext, vprev, vcur, vnext,
            _x_spec(x, b, TQ),
            pl.BlockSpec((NH * HD, D), lambda i: (0, 0)),
            _mod_spec(layer, b),
            pl.BlockSpec((NKV, 3 * BLK, GQA * BLK), lambda i: (0, 0, 0)),
            pl.BlockSpec((NKV, 1, GQA * BLK), lambda i: (0, 0, 0)),
            _gain_spec(layer),
            pl.BlockSpec((D, 2 * EP), lambda i: (0, 0)),
        ],
        out_specs=_routed_specs(TQ),
        out_shape=_ROUTED_SHAPES,
        scratch_shapes=[pltpu.VMEM((NH * HD, TQ), F32), pltpu.VMEM((ATT_AHEAD + 1, 3 * BLK, GQA * BLK), F32)],
        compiler_params=_cparams(("arbitrary",)),
        name="attn_core",
    )(qt, k, k, k, vt, vt, vt, x, w_out_bf, mod_all, bias_tab, sink_row, gain_ffn, wr)


FFN_TF = 1024
FFN_RT = 512
FFN_WT = 256


def _select_kernel(aff_ref, tri_ref, pos_ref):
    aff = aff_ref[...]
    bits = pltpu.bitcast(aff, jnp.int32)
    rows = aff.shape[0]

    def count_ge(v):
        return jnp.sum((bits >= v).astype(jnp.int32), axis=1, keepdims=True)

    def body(t, cur):
        sh = 28 - 2 * t
        c1, c2, c3 = cur | (jnp.int32(1) << sh), cur | (jnp.int32(2) << sh), cur | (jnp.int32(3) << sh)
        n1, n2, n3 = count_ge(c1), count_ge(c2), count_ge(c3)
        return jnp.where(n3 >= CAP, c3, jnp.where(n2 >= CAP, c2, jnp.where(n1 >= CAP, c1, cur)))

    top = jnp.full((rows, 1), 1 << 30, jnp.int32)
    thr = lax.fori_loop(0, 15, body, jnp.where(count_ge(top) >= CAP, top, 0))
    gt = bits > thr
    eq = bits == thr
    need = CAP - jnp.sum(gt.astype(jnp.int32), axis=1, keepdims=True)
    tri = tri_ref[...]

    def cumsum_excl(mask_f):
        off = jnp.zeros((mask_f.shape[0], 1), F32)
        outs = []
        for j in range(S // 128):
            mj = mask_f[:, j * 128:(j + 1) * 128]
            loc = jnp.dot(mj.astype(BF16), tri, preferred_element_type=F32)
            outs.append(loc - mj + off)
            off = off + loc[:, 127:128]
        return jnp.concatenate(outs, axis=1)

    counts = cumsum_excl(jnp.concatenate([gt, eq], axis=0).astype(F32))
    gt_before, eq_before = counts[:rows], counts[rows:]
    needf = need.astype(F32)
    sel = gt | (eq & (eq_before < needf))
    pos = gt_before + jnp.minimum(eq_before, needf)
    pos_ref[...] = jnp.where(sel, pos.astype(jnp.int32), -1)


SC_CORES = 2
SC_SUBCORES = 16
SC_TILES = SC_CORES * SC_SUBCORES
DISPATCH_SLOTS = CAP * E // SC_TILES
DISPATCH_ROWS = 64
COMBINE_RANGE = 1024
COMBINE_ROWS = 32
SLAB = 128
NSLAB = D // SLAB


def _sc_mesh():
    return plsc.VectorSubcoreMesh(core_axis_name="c", subcore_axis_name="s",
                                  num_cores=SC_CORES, num_subcores=SC_SUBCORES)


def _dispatch_body(pos_hbm, aff_hbm, h_hbm, xin_hbm, idx_hbm, gate_hbm, pos_v, aff_v, idx_v, gate_v,
                   rows_a, rows_b, gsem_a, gsem_b, wsem_a, wsem_b):
    w = lax.axis_index("s") * SC_CORES + lax.axis_index("c")
    e = w // 2
    lo = (w % 2) * DISPATCH_SLOTS
    pltpu.sync_copy(pos_hbm.at[e], pos_v)
    pltpu.sync_copy(aff_hbm.at[e], aff_v)

    @pl.loop(0, S // 16)
    def _(i):
        p = pos_v[pl.ds(i * 16, 16)] - lo
        m = (p >= 0) & (p < DISPATCH_SLOTS)
        tok = lax.iota(jnp.int32, 16) + i * 16
        plsc.store_scatter(idx_v, [p], tok, mask=m)
        plsc.store_scatter(gate_v, [p], aff_v[pl.ds(i * 16, 16)], mask=m)

    pltpu.sync_copy(idx_v, idx_hbm.at[e, pl.ds(lo, DISPATCH_SLOTS)])
    pltpu.sync_copy(gate_v, gate_hbm.at[e, pl.ds(lo, DISPATCH_SLOTS)])

    bufs, gsems, wsems = (rows_a, rows_b), (gsem_a, gsem_b), (wsem_a, wsem_b)
    nchunk = DISPATCH_SLOTS // DISPATCH_ROWS

    def gather(j):
        return pltpu.async_copy(h_hbm.at[idx_v.at[pl.ds(j * DISPATCH_ROWS, DISPATCH_ROWS)]], bufs[j % 2], gsems[j % 2])

    pending_gather = gather(0)
    writes = [None, None]
    for j in range(nchunk):
        pending_gather.wait()
        writes[j % 2] = pltpu.async_copy(
            bufs[j % 2], xin_hbm.at[e, pl.ds(lo + j * DISPATCH_ROWS, DISPATCH_ROWS)], wsems[j % 2])
        if j + 1 < nchunk:
            if writes[(j + 1) % 2] is not None:
                writes[(j + 1) % 2].wait()
            pending_gather = gather(j + 1)
    writes[(nchunk - 2) % 2].wait()
    writes[(nchunk - 1) % 2].wait()


def _dispatch(pos, aff, h):
    return pl.kernel(
        _dispatch_body, mesh=_sc_mesh(),
        out_type=[jax.ShapeDtypeStruct((E, CAP, D // 2), jnp.int32),
                  jax.ShapeDtypeStruct((E, CAP), jnp.int32),
                  jax.ShapeDtypeStruct((E, CAP), F32)],
        scratch_types=[pltpu.VMEM((S,), jnp.int32), pltpu.VMEM((S,), F32),
                       pltpu.VMEM((DISPATCH_SLOTS,), jnp.int32), pltpu.VMEM((DISPATCH_SLOTS,), F32),
                       pltpu.VMEM((DISPATCH_ROWS, D // 2), jnp.int32), pltpu.VMEM((DISPATCH_ROWS, D // 2), jnp.int32),
                       pltpu.SemaphoreType.DMA, pltpu.SemaphoreType.DMA,
                       pltpu.SemaphoreType.DMA, pltpu.SemaphoreType.DMA],
        compiler_params=pltpu.CompilerParams(needs_layout_passes=False),
        name="moe_dispatch",
    )(pos, aff, h)


def _combine_body(y_hbm, idx_hbm, x_hbm, out_hbm, idx_v, li_v, *bufs):
    rows = bufs[:NSLAB]
    accs = bufs[NSLAB:]
    c = lax.axis_index("c")
    s = lax.axis_index("s")
    share = COMBINE_RANGE // SC_SUBCORES
    pltpu.sync_copy(idx_hbm.at[s], idx_v)
    lane = lax.iota(jnp.int32, 16)

    @pl.loop(0, S // COMBINE_RANGE // SC_CORES)
    def _(r):
        t0 = (r * SC_CORES + c) * COMBINE_RANGE
        row0 = t0 + s * share
        pltpu.sync_copy(tuple(x_hbm.at[pl.ds(row0, share), pl.ds(q * SLAB, SLAB)] for q in range(NSLAB)),
                        tuple(accs[q].at[pl.ds(s * share, share)] for q in range(NSLAB)))
        plsc.subcore_barrier()

        @pl.loop(0, CAP // COMBINE_ROWS)
        def _(j):
            hits = jnp.zeros((16,), jnp.int32)
            for v in range(COMBINE_ROWS // 16):
                t = idx_v[pl.ds(j * COMBINE_ROWS + v * 16, 16)] - t0
                ok = (t >= 0) & (t < COMBINE_RANGE)
                li_v[pl.ds(v * 16, 16)] = jnp.where(ok, t, COMBINE_RANGE + lane)
                hits = hits + plsc.all_reduce_population_count(ok)

            @pl.when(jnp.max(hits) > 0)
            def _():
                pltpu.sync_copy(
                    tuple(y_hbm.at[s, pl.ds(j * COMBINE_ROWS, COMBINE_ROWS), pl.ds(q * SLAB, SLAB)]
                          for q in range(NSLAB)),
                    tuple(rows))
                pltpu.sync_copy(tuple(rows), tuple(accs[q].at[li_v] for q in range(NSLAB)), add=True)

        plsc.subcore_barrier()
        pltpu.sync_copy(tuple(accs[q].at[pl.ds(s * share, share)] for q in range(NSLAB)),
                        tuple(out_hbm.at[pl.ds(row0, share), pl.ds(q * SLAB, SLAB)] for q in range(NSLAB)))


def _combine(y, idx, x, out_rows=S):
    return pl.kernel(
        _combine_body, mesh=_sc_mesh(),
        out_type=jax.ShapeDtypeStruct((out_rows, D), F32),
        scratch_types=[pltpu.VMEM((CAP,), jnp.int32), pltpu.VMEM((COMBINE_ROWS,), jnp.int32)]
        + [pltpu.VMEM((COMBINE_ROWS, SLAB), F32) for _ in range(NSLAB)]
        + [pltpu.VMEM_SHARED((COMBINE_RANGE + 16, SLAB), F32) for _ in range(NSLAB)],
        compiler_params=pltpu.CompilerParams(needs_layout_passes=False),
        name="moe_combine",
    )(y, idx, x)


def _ffn_kernel(x_ref, wg_ref, wu_ref, wd_ref, gate_ref, mod_ref, o_ref):
    f = pl.program_id(1)
    last = pl.num_programs(1) - 1
    nt = FFN_TF // FFN_WT
    wg = [wg_ref[0, 0, :, j * FFN_WT:(j + 1) * FFN_WT].astype(BF16) for j in range(nt)]
    wu = [wu_ref[0, 0, :, j * FFN_WT:(j + 1) * FFN_WT].astype(BF16) for j in range(nt)]
    wd = wd_ref[0, 0].astype(BF16)
    is_first = f == 0
    is_last = f == last
    g2 = jnp.where(is_last, mod_ref[0, 0][5:6], 1.0)
    g_row = gate_ref[pl.ds(pl.program_id(0), 1), :]
    g_col = jnp.broadcast_to(g_row, (128, CAP)).T[:, 0:1]
    for r in range(CAP // FFN_RT):
        rows = slice(r * FFN_RT, (r + 1) * FFN_RT)
        xr = _unpack_bf16_pairs(x_ref[0, rows, :])
        acts = []
        for j in range(nt):
            g = jnp.dot(xr, wg[j], preferred_element_type=F32)
            u = jnp.dot(xr, wu[j], preferred_element_type=F32)
            acts.append((g * _sigmoid(g) * u).astype(BF16))
        y = jnp.dot(jnp.concatenate(acts, axis=1), wd, preferred_element_type=F32)
        prev = jnp.where(is_first, 0.0, o_ref[0, rows, :])
        gate = jnp.where(is_last, g_col[rows, :], 1.0)
        o_ref[0, rows, :] = (prev + y) * gate * g2


def _select(aff, tri):
    return pl.pallas_call(
        _select_kernel,
        grid=(1,),
        in_specs=[
            pl.BlockSpec((E, S), lambda i: (0, 0)),
            pl.BlockSpec((128, 128), lambda i: (0, 0)),
        ],
        out_specs=pl.BlockSpec((E, S), lambda i: (0, 0)),
        out_shape=jax.ShapeDtypeStruct((E, S), jnp.int32),
        compiler_params=_cparams(("arbitrary",)),
        name="moe_select",
    )(aff, tri)


def _ffn(xin, gate, b, mod_all, w_gate, w_up, w_down, layer):
    return pl.pallas_call(
        _ffn_kernel,
        grid=(E, F // FFN_TF),
        in_specs=[
            pl.BlockSpec((1, CAP, D // 2), lambda e, f: (e, 0, 0)),
            pl.BlockSpec((1, 1, D, FFN_TF), lambda e, f: (layer, e, 0, f)),
            pl.BlockSpec((1, 1, D, FFN_TF), lambda e, f: (layer, e, 0, f)),
            pl.BlockSpec((1, 1, FFN_TF, D), lambda e, f: (layer, e, f, 0)),
            pl.BlockSpec((E, CAP), lambda e, f: (0, 0)),
            pl.BlockSpec((1, 1, 6, D), lambda e, f: (layer, b, 0, 0)),
        ],
        out_specs=pl.BlockSpec((1, CAP, D), lambda e, f: (e, 0, 0)),
        out_shape=jax.ShapeDtypeStruct((E, CAP, D), F32),
        compiler_params=_cparams(("arbitrary", "arbitrary"), 48),
        name="moe_ffn",
    )(xin, w_gate, w_up, w_down, gate, mod_all)


def kernel(x, c, w_ada, b_ada, norm_mix, norm_ffn, w_fourier_out, w_qkv, w_attn_out, q_gain, k_gain,
           sink, rel_bias, w_router, w_gate, w_up, w_down):
    m0, m2 = _dft_tables()
    m1 = _stage1_table()
    bucket_t = _bucket_table()
    tri = jnp.asarray(np.triu(np.ones((128, 128), np.float32)), BF16)
    mod_all = _ada(c, w_ada, b_ada)
    bias_tab = _bias_table(rel_bias, bucket_t)
    gain_mix = norm_mix.reshape(DEPTH, 1, D)
    gain_ffn = norm_ffn.reshape(DEPTH, 1, D)
    xs = [x, x]
    for layer in range(DEPTH):
        j = layer // 2
        wr = jnp.pad(w_router[layer], ((0, 0), (0, EP - E)))
        wr1 = wr.astype(BF16)
        wr = jnp.concatenate([wr1, (wr - wr1.astype(F32)).astype(BF16)], axis=1)
        if layer % 2 == 0:
            w_out_bf = w_fourier_out[j].astype(BF16)
            routed = [_fourier_layer(xs[b], b, mod_all, gain_mix, gain_ffn, wr, w_out_bf, layer, m0, m1, m2)
                      for b in range(B)]
        else:
            aw = _attn_weights(w_qkv[j], q_gain[j], k_gain[j], sink[j])
            w_out_bf = w_attn_out[j].astype(BF16)
            routed = [_attn_layer(xs[b], b, mod_all, gain_mix, gain_ffn, wr, aw, w_out_bf, bias_tab, layer)
                      for b in range(B)]
        xs = [routed[b][0] for b in range(B)]
        pos = [_select(routed[b][2], tri) for b in range(B)]
        disp = [_dispatch(pos[b], routed[b][2], routed[b][1]) for b in range(B)]
        ys = [_ffn(disp[b][0], disp[b][2], b, mod_all, w_gate, w_up, w_down, layer) for b in range(B)]
        if layer < DEPTH - 1:
            xs = [_combine(ys[b], disp[b][1], xs[b]) for b in range(B)]
    out = _combine(ys[0], disp[0][1], xs[0], out_rows=B * S)
    for b in range(1, B):
        out = lax.dynamic_update_slice(out, _combine(ys[b], disp[b][1], xs[b]), (b * S, 0))
    return out.reshape(B, S, D)
```

```python
import math

import numpy as np
import jax
import jax.numpy as jnp
from jax import lax
from jax.experimental import pallas as pl
from jax.experimental.pallas import tpu as pltpu
from jax.experimental.pallas import tpu_sc as plsc

D = 1024
B = 2
S = 8192
DEPTH = 4
GROUPS = 4
GD = D // GROUPS
HD = 64
NH = 16
NKV = 4
GQA = NH // NKV
WINDOW = 128
BLK = 128
NBUCKETS = 32
MAXDIST = 128
E = 16
CAP = 2 * S // E
F = 2 * D
EPS = 1e-6
NEG_INF = -1e30

N1 = 128
N2 = 64
INV_NORM = 1.0 / math.sqrt(S * GD)

F32 = jnp.float32
BF16 = jnp.bfloat16


def _cparams(sem, vmem_mb=48):
    return pltpu.CompilerParams(dimension_semantics=sem, vmem_limit_bytes=vmem_mb * 1024 * 1024)


def _dft_tables():
    c = np.arange(GD)
    ang0 = 2.0 * np.pi * ((c[:, None] * c[None, :]) % GD) / GD
    m0 = np.concatenate([np.cos(ang0), -np.sin(ang0)], axis=1)
    k2 = np.arange(N2)
    ang2 = 2.0 * np.pi * ((k2[:, None] * k2[None, :]) % N2) / N2
    m2 = np.stack([np.cos(ang2), np.sin(ang2)], axis=2).reshape(N2, 2 * N2)
    return jnp.asarray(m0, BF16), jnp.asarray(m2, BF16)


def _stage1_table():
    s2 = np.arange(N2)[:, None, None]
    k1 = np.arange(N1)[None, :, None]
    s1 = np.arange(N1)[None, None, :]
    th = ((k1 * (N2 * s1 + s2)) % S) * (2.0 * np.pi / S)
    co, si = np.cos(th), np.sin(th)
    top = np.concatenate([co, si], axis=2)
    bot = np.concatenate([-si, co], axis=2)
    return jnp.asarray(np.concatenate([top, bot], axis=1).astype(np.float32), BF16)


def _bucket_table():
    q_off = np.arange(BLK)
    k_off = np.arange(3 * BLK) - BLK
    rel = k_off[:, None] - q_off[None, :]
    half = NBUCKETS // 2
    max_exact = half // 2
    ret = np.where(rel > 0, half, 0)
    n = np.abs(rel)
    nf = np.maximum(n, 1).astype(np.float32)
    ratio = (np.log(nf / np.float32(max_exact)) / np.float32(math.log(MAXDIST / max_exact))).astype(np.float32)
    large = max_exact + (ratio * np.float32(half - max_exact)).astype(np.int32)
    large = np.minimum(large, half - 1)
    bucket = ret + np.where(n < max_exact, n, large)
    return jnp.asarray(np.where(np.abs(rel) <= WINDOW, bucket, -1).astype(np.int32))


def _modulate(x, gain, shift, scale):
    ms = jnp.mean(x * x, axis=-1, keepdims=True)
    return x * lax.rsqrt(ms + EPS) * (gain * (1.0 + scale)) + shift


def _sigmoid(x):
    return 1.0 / (1.0 + jnp.exp(-x))


ADA_TN = 1536


def _ada_kernel(ct_ref, w_ref, b_ref, o_ref):
    ct = ct_ref[...]
    ca = ct * _sigmoid(ct)
    w = w_ref[0]
    for b in range(B):
        o_ref[0, b:b + 1, :] = jnp.sum(w * ca[:, b:b + 1], axis=0, keepdims=True) + b_ref[0]


def _ada(c, w_ada, b_ada):
    out = pl.pallas_call(
        _ada_kernel,
        grid=(DEPTH, 6 * D // ADA_TN),
        in_specs=[
            pl.BlockSpec((D, B), lambda l, j: (0, 0)),
            pl.BlockSpec((1, D, ADA_TN), lambda l, j: (l, 0, j)),
            pl.BlockSpec((1, 1, ADA_TN), lambda l, j: (l, 0, j)),
        ],
        out_specs=pl.BlockSpec((1, B, ADA_TN), lambda l, j: (l, 0, j)),
        out_shape=jax.ShapeDtypeStruct((DEPTH, B, 6 * D), F32),
        compiler_params=_cparams(("arbitrary", "arbitrary"), 32),
        name="ada_mod",
    )(c.T, w_ada, b_ada.reshape(DEPTH, 1, 6 * D))
    return out.reshape(DEPTH, B, 6, D)


def _mod_spec(layer, b):
    return pl.BlockSpec((1, 1, 6, D), lambda *_: (layer, b, 0, 0))


def _gain_spec(layer):
    return pl.BlockSpec((1, 1, D), lambda *_: (layer, 0, 0))


def _x_spec(x, b, tm):
    if x.ndim == 3:
        return pl.BlockSpec((None, tm, D), lambda i: (b, i, 0))
    return pl.BlockSpec((tm, D), lambda i: (i, 0))


EP = 128

_ROUTED_SHAPES = [jax.ShapeDtypeStruct((S, D), F32),
                  jax.ShapeDtypeStruct((S, D // 2), jnp.int32),
                  jax.ShapeDtypeStruct((E, S), F32)]


def _routed_specs(tm):
    return [pl.BlockSpec((tm, D), lambda i: (i, 0)),
            pl.BlockSpec((tm, D // 2), lambda i: (i, 0)),
            pl.BlockSpec((E, tm), lambda i: (0, i))]


F0_TM = 1024
SUB = 8
SEQ_PICKS = 2 * SUB


def _pack_bf16_pairs(h):
    hb = h.astype(BF16).astype(F32)
    lo = pltpu.bitcast(hb[:, :D // 2], jnp.int32)
    hi = pltpu.bitcast(hb[:, D // 2:], jnp.int32)
    return hi | lax.shift_right_logical(lo, jnp.int32(16))


def _unpack_bf16_pairs(xp):
    lo = pltpu.bitcast(xp << 16, F32).astype(BF16)
    hi = pltpu.bitcast(xp & jnp.int32(-65536), F32).astype(BF16)
    return jnp.concatenate([lo, hi], axis=1)


def _route(xn, m, gain_ffn, wr):
    h = _modulate(xn, gain_ffn, m[3:4], m[4:5])
    h1 = h.astype(BF16)
    h2 = (h - h1.astype(F32)).astype(BF16)
    part = jnp.dot(h1, wr, preferred_element_type=F32)
    logits = part[:, :EP] + part[:, EP:] + jnp.dot(h2, wr[:, :EP], preferred_element_type=F32)
    lt = logits.T[:E]
    ex = jnp.exp(lt - jnp.max(lt, axis=0, keepdims=True))
    return _pack_bf16_pairs(h), ex / jnp.sum(ex, axis=0, keepdims=True)


def _f0_kernel(x_ref, mod_ref, gain_ref, m0_ref, o_ref):
    m = mod_ref[0, 0]
    h = _modulate(x_ref[...], gain_ref[0], m[0:1], m[1:2]).astype(BF16)
    m0 = m0_ref[...]
    r = [jnp.dot(h[:, g * GD:(g + 1) * GD], m0, preferred_element_type=F32) for g in range(GROUPS)]
    o_ref[0] = _pack_bf16_pairs(jnp.concatenate([rg[:, :GD] for rg in r], axis=1))
    o_ref[1] = _pack_bf16_pairs(jnp.concatenate([rg[:, GD:] for rg in r], axis=1))


def _f1_kernel(w_ref, m1_ref, o_ref, scr_ref):
    n = o_ref.shape[0]
    for j in range(n):
        scr_ref[j] = w_ref[:, j, :]
    for j in range(n):
        w = _unpack_bf16_pairs(scr_ref[j])
        o_ref[j] = _pack_bf16_pairs(jnp.dot(m1_ref[j], w, preferred_element_type=F32))


def _f2_kernel(u_ref, m2_ref, o_ref, scr_ref):
    m2 = m2_ref[...]
    n = o_ref.shape[0]
    for k in range(n):
        scr_ref[k] = u_ref[:, k, :]
    for k in range(n):
        u = _unpack_bf16_pairs(scr_ref[k])
        o_ref[k] = _pack_bf16_pairs(jnp.dot(m2, u, preferred_element_type=F32))


def _f3_kernel(mp_ref, x_ref, w_ref, mod_ref, gf_ref, wr_ref, o_ref, h_ref, aff_ref, scr_ref):
    g1 = mod_ref[0, 0][2:3]
    for j in range(SUB):
        scr_ref[j * N1:(j + 1) * N1, :] = mp_ref[:, j, :]
    a = _unpack_bf16_pairs(scr_ref[...])
    y = jnp.dot(a, w_ref[...], preferred_element_type=F32)
    xn = x_ref[...] + (g1 * INV_NORM) * y
    o_ref[...] = xn
    h_ref[...], aff_ref[...] = _route(xn, mod_ref[0, 0], gf_ref[0], wr_ref[...])


def _fourier_layer(x, b, mod_all, gain, gain_ffn, wr, w_out_bf, layer, m0, m1, m2):
    wc = pl.pallas_call(
        _f0_kernel,
        grid=(S // F0_TM,),
        in_specs=[
            _x_spec(x, b, F0_TM),
            _mod_spec(layer, b), _gain_spec(layer),
            pl.BlockSpec((GD, 2 * GD), lambda i: (0, 0)),
        ],
        out_specs=pl.BlockSpec((2, F0_TM, D // 2), lambda i: (0, i, 0)),
        out_shape=jax.ShapeDtypeStruct((2, S, D // 2), jnp.int32),
        compiler_params=_cparams(("arbitrary",)),
        name="fourier_chan",
    )(x, mod_all, gain, m0)
    wc = wc.reshape(2 * N1, N2, D // 2)
    u = pl.pallas_call(
        _f1_kernel,
        grid=(N2 // SEQ_PICKS,),
        in_specs=[
            pl.BlockSpec((2 * N1, SEQ_PICKS, D // 2), lambda i: (0, i, 0)),
            pl.BlockSpec((SEQ_PICKS, 2 * N1, 2 * N1), lambda i: (i, 0, 0)),
        ],
        out_specs=pl.BlockSpec((SEQ_PICKS, 2 * N1, D // 2), lambda i: (i, 0, 0)),
        out_shape=jax.ShapeDtypeStruct((N2, 2 * N1, D // 2), jnp.int32),
        scratch_shapes=[pltpu.VMEM((SEQ_PICKS, 2 * N1, D // 2), jnp.int32)],
        compiler_params=_cparams(("arbitrary",)),
        name="fourier_seq1",
    )(wc, m1)
    u = u.reshape(2 * N2, N1, D // 2)
    mp = pl.pallas_call(
        _f2_kernel,
        grid=(N1 // SEQ_PICKS,),
        in_specs=[
            pl.BlockSpec((2 * N2, SEQ_PICKS, D // 2), lambda i: (0, i, 0)),
            pl.BlockSpec((N2, 2 * N2), lambda i: (0, 0)),
        ],
        out_specs=pl.BlockSpec((SEQ_PICKS, N2, D // 2), lambda i: (i, 0, 0)),
        out_shape=jax.ShapeDtypeStruct((N1, N2, D // 2), jnp.int32),
        scratch_shapes=[pltpu.VMEM((SEQ_PICKS, 2 * N2, D // 2), jnp.int32)],
        compiler_params=_cparams(("arbitrary",)),
        name="fourier_seq2",
    )(u, m2)
    return pl.pallas_call(
        _f3_kernel,
        grid=(N2 // SUB,),
        in_specs=[
            pl.BlockSpec((N1, SUB, D // 2), lambda i: (0, i, 0)),
            _x_spec(x, b, SUB * N1),
            pl.BlockSpec((D, D), lambda i: (0, 0)),
            _mod_spec(layer, b), _gain_spec(layer),
            pl.BlockSpec((D, 2 * EP), lambda i: (0, 0)),
        ],
        out_specs=_routed_specs(SUB * N1),
        out_shape=_ROUTED_SHAPES,
        scratch_shapes=[pltpu.VMEM((SUB * N1, D // 2), jnp.int32)],
        compiler_params=_cparams(("arbitrary",)),
        name="fourier_out",
    )(mp, x, w_out_bf, mod_all, gain_ffn, wr)


QKV_TM = 1024
TQ = 1024
NSB = TQ // BLK
KWIN = TQ + 2 * BLK
NBLK = S // BLK
KPAD = 128
VROWS = HD + 16
ATT_AHEAD = 1
LOG2E = math.log2(math.e)


def _bias_kernel(rb_ref, bucket_ref, o_ref):
    h = pl.program_id(0)
    bucket = bucket_ref[...]
    acc = jnp.full(bucket.shape, NEG_INF, F32)
    for k in range(NBUCKETS):
        acc = jnp.where(bucket == k, rb_ref[k, h] * LOG2E, acc)
    o_ref[0] = acc


def _bias_table(rel_bias, bucket_t):
    return pl.pallas_call(
        _bias_kernel,
        grid=(NH,),
        in_specs=[
            pl.BlockSpec(memory_space=pltpu.SMEM),
            pl.BlockSpec((3 * BLK, BLK), lambda h: (0, 0)),
        ],
        out_specs=pl.BlockSpec((1, 3 * BLK, BLK), lambda h: (h // GQA, 0, h % GQA)),
        out_shape=jax.ShapeDtypeStruct((NKV, 3 * BLK, GQA * BLK), F32),
        compiler_params=_cparams(("arbitrary",)),
        name="rel_bias_table",
    )(rel_bias, bucket_t)


def _qkv_kernel(x_ref, mod_ref, gain_ref, wqt_ref, wk_ref, wvt_ref, qg_ref, kg_ref, qt_ref, k_ref, vt_ref):
    m = mod_ref[0, 0]
    h = _modulate(x_ref[...], gain_ref[0], m[0:1], m[1:2]).astype(BF16)
    nt = (((1,), (1,)), ((), ()))
    qt = lax.dot_general(wqt_ref[...], h, nt, preferred_element_type=F32)
    tm = qt.shape[1]
    q3 = qt.reshape(NH, HD, tm)
    q3 = q3 * lax.rsqrt(jnp.mean(q3 * q3, axis=1, keepdims=True) + EPS)
    qt_ref[...] = (q3.reshape(NH * HD, tm) * qg_ref[...]).astype(BF16)
    k = jnp.dot(h, wk_ref[...], preferred_element_type=F32)
    for g in range(NKV):
        kg = k[:, g * KPAD:(g + 1) * KPAD]
        ms = jnp.sum(kg * kg, axis=-1, keepdims=True) * (1.0 / HD)
        k_ref[:, g * KPAD:(g + 1) * KPAD] = (kg * lax.rsqrt(ms + EPS) * kg_ref[...]).astype(BF16)
    vt_ref[...] = lax.dot_general(wvt_ref[...], h, nt, preferred_element_type=F32).astype(BF16)


def _attn_kernel(qt_ref, kp_ref, kc_ref, kn_ref, vp_ref, vc_ref, vn_ref, x_ref, w_ref, mod_ref,
                 bias_ref, sink_ref, gf_ref, wr_ref, o_ref, h_ref, aff_ref, att_ref, s_ref):
    i = pl.program_id(0)
    kwin = jnp.concatenate([kp_ref[...], kc_ref[...], kn_ref[...]], axis=0)
    vwin = jnp.concatenate([vp_ref[...], vc_ref[...], vn_ref[...]], axis=1)
    ones_rows = (lax.broadcasted_iota(jnp.int32, (VROWS - HD, KWIN), 0) == 0).astype(BF16)
    vaug = [jnp.concatenate([vwin[g * HD:(g + 1) * HD], ones_rows], axis=0) for g in range(NKV)]
    key_pos = i * TQ - BLK + lax.broadcasted_iota(jnp.int32, (KWIN, 1), 0)
    key_mask = jnp.where((key_pos >= 0) & (key_pos < S), 0.0, NEG_INF).astype(BF16)
    lane = lax.broadcasted_iota(jnp.int32, (1, NKV * KPAD), 1)
    kwin = jnp.where(lane % KPAD == HD, key_mask, kwin)
    q_ones = (lax.broadcasted_iota(jnp.int32, (KPAD - HD, GQA * BLK), 0) == 0).astype(BF16)
    items = [(g, r) for g in range(NKV) for r in range(NSB)]

    def scores(g, r):
        kr = kwin[r * BLK:r * BLK + 3 * BLK, g * KPAD:(g + 1) * KPAD]
        qg = jnp.concatenate(
            [qt_ref[(GQA * g + hh) * HD:(GQA * g + hh + 1) * HD, r * BLK:(r + 1) * BLK] for hh in range(GQA)],
            axis=1)
        qa = jnp.concatenate([qg, q_ones], axis=0)
        return jnp.dot(kr, qa, preferred_element_type=F32)

    def probs(s, g, r):
        sink = sink_ref[g] * LOG2E
        s = s + bias_ref[g]
        mx = jnp.maximum(jnp.max(s, axis=0, keepdims=True), sink)
        return jnp.exp2(s - mx).astype(BF16), jnp.exp2(sink - mx)

    nslot = ATT_AHEAD + 1
    for n in range(ATT_AHEAD):
        s_ref[n % nslot] = scores(*items[n])
    for n, (g, r) in enumerate(items):
        if n + ATT_AHEAD < len(items):
            s_ref[(n + ATT_AHEAD) % nslot] = scores(*items[n + ATT_AHEAD])
        p, psink = probs(s_ref[n % nslot], g, r)
        ot = jnp.dot(vaug[g][:, r * BLK:r * BLK + 3 * BLK], p, preferred_element_type=F32)
        ot = ot[:HD] * (1.0 / (ot[HD:HD + 1] + psink))
        for hh in range(GQA):
            hd0 = (GQA * g + hh) * HD
            att_ref[hd0:hd0 + HD, r * BLK:(r + 1) * BLK] = ot[:, hh * BLK:(hh + 1) * BLK]
    g1 = mod_ref[0, 0][2:3]
    att = att_ref[...].T.astype(BF16)
    y = jnp.dot(att, w_ref[...], preferred_element_type=F32)
    xn = x_ref[...] + g1 * y
    o_ref[...] = xn
    h_ref[...], aff_ref[...] = _route(xn, mod_ref[0, 0], gf_ref[0], wr_ref[...])


def _attn_weights(w_qkv, q_gain, k_gain, sink):
    wq, wk, wv = w_qkv[:, :NH * HD], w_qkv[:, NH * HD:(NH + NKV) * HD], w_qkv[:, (NH + NKV) * HD:]
    wqt = wq.T.astype(BF16)
    wvt = wv.T.astype(BF16)
    wk_pad = jnp.pad(wk.reshape(D, NKV, HD), ((0, 0), (0, 0), (0, KPAD - HD))).reshape(D, NKV * KPAD).astype(BF16)
    qg_col = jnp.tile(q_gain * (HD ** -0.5 * LOG2E), NH).reshape(NH * HD, 1)
    kg_row = jnp.pad(k_gain, (0, KPAD - HD)).reshape(1, KPAD)
    sink_row = jnp.repeat(sink, BLK).reshape(NKV, 1, GQA * BLK)
    return wqt, wk_pad, wvt, qg_col, kg_row, sink_row


def _attn_layer(x, b, mod_all, gain, gain_ffn, wr, aw, w_out_bf, bias_tab, layer):
    wqt, wk_pad, wvt, qg_col, kg_row, sink_row = aw
    qt, k, vt = pl.pallas_call(
        _qkv_kernel,
        grid=(S // QKV_TM,),
        in_specs=[
            _x_spec(x, b, QKV_TM),
            _mod_spec(layer, b), _gain_spec(layer),
            pl.BlockSpec((NH * HD, D), lambda i: (0, 0)),
            pl.BlockSpec((D, NKV * KPAD), lambda i: (0, 0)),
            pl.BlockSpec((NKV * HD, D), lambda i: (0, 0)),
            pl.BlockSpec((NH * HD, 1), lambda i: (0, 0)),
            pl.BlockSpec((1, KPAD), lambda i: (0, 0)),
        ],
        out_specs=[
            pl.BlockSpec((NH * HD, QKV_TM), lambda i: (0, i)),
            pl.BlockSpec((QKV_TM, NKV * KPAD), lambda i: (i, 0)),
            pl.BlockSpec((NKV * HD, QKV_TM), lambda i: (0, i)),
        ],
        out_shape=[
            jax.ShapeDtypeStruct((NH * HD, S), BF16),
            jax.ShapeDtypeStruct((S, NKV * KPAD), BF16),
            jax.ShapeDtypeStruct((NKV * HD, S), BF16),
        ],
        compiler_params=_cparams(("arbitrary",)),
        name="attn_qkv",
    )(x, mod_all, gain, wqt, wk_pad, wvt, qg_col, kg_row)

    kw = NKV * KPAD
    vw = NKV * HD
    kprev = pl.BlockSpec((BLK, kw), lambda i: (jnp.maximum(i * NSB - 1, 0), 0))
    kcur = pl.BlockSpec((TQ, kw), lambda i: (i, 0))
    knext = pl.BlockSpec((BLK, kw), lambda i: (jnp.minimum((i + 1) * NSB, NBLK - 1), 0))
    vprev = pl.BlockSpec((vw, BLK), lambda i: (0, jnp.maximum(i * NSB - 1, 0)))
    vcur = pl.BlockSpec((vw, TQ), lambda i: (0, i))
    vnext = pl.BlockSpec((vw, BLK), lambda i: (0, jnp.minimum((i + 1) * NSB, NBLK - 1)))
    return pl.pallas_call(
        _attn_kernel,
        grid=(S // TQ,),
        in_specs=[
            pl.BlockSpec((NH * HD, TQ), lambda i: (0, i)),
            kprev, kcur, knext, vprev, vcur, vnext,
            _x_spec(x, b, TQ),
            pl.BlockSpec((NH * HD, D), lambda i: (0, 0)),
            _mod_spec(layer, b),
            pl.BlockSpec((NKV, 3 * BLK, GQA * BLK), lambda i: (0, 0, 0)),
            pl.BlockSpec((NKV, 1, GQA * BLK), lambda i: (0, 0, 0)),
            _gain_spec(layer),
            pl.BlockSpec((D, 2 * EP), lambda i: (0, 0)),
        ],
        out_specs=_routed_specs(TQ),
        out_shape=_ROUTED_SHAPES,
        scratch_shapes=[pltpu.VMEM((NH * HD, TQ), F32), pltpu.VMEM((ATT_AHEAD + 1, 3 * BLK, GQA * BLK), F32)],
        compiler_params=_cparams(("arbitrary",)),
        name="attn_core",
    )(qt, k, k, k, vt, vt, vt, x, w_out_bf, mod_all, bias_tab, sink_row, gain_ffn, wr)


FFN_TF = 1024
FFN_RT = 512
FFN_WT = 256


def _select_kernel(aff_ref, tri_ref, pos_ref):
    aff = aff_ref[...]
    bits = pltpu.bitcast(aff, jnp.int32)
    rows = aff.shape[0]

    def count_ge(v):
        return jnp.sum((bits >= v).astype(jnp.int32), axis=1, keepdims=True)

    def body(t, cur):
        sh = 28 - 2 * t
        c1, c2, c3 = cur | (jnp.int32(1) << sh), cur | (jnp.int32(2) << sh), cur | (jnp.int32(3) << sh)
        n1, n2, n3 = count_ge(c1), count_ge(c2), count_ge(c3)
        return jnp.where(n3 >= CAP, c3, jnp.where(n2 >= CAP, c2, jnp.where(n1 >= CAP, c1, cur)))

    top = jnp.full((rows, 1), 1 << 30, jnp.int32)
    thr = lax.fori_loop(0, 15, body, jnp.where(count_ge(top) >= CAP, top, 0))
    gt = bits > thr
    eq = bits == thr
    need = CAP - jnp.sum(gt.astype(jnp.int32), axis=1, keepdims=True)
    tri = tri_ref[...]

    def cumsum_excl(mask_f):
        off = jnp.zeros((mask_f.shape[0], 1), F32)
        outs = []
        for j in range(S // 128):
            mj = mask_f[:, j * 128:(j + 1) * 128]
            loc = jnp.dot(mj.astype(BF16), tri, preferred_element_type=F32)
            outs.append(loc - mj + off)
            off = off + loc[:, 127:128]
        return jnp.concatenate(outs, axis=1)

    counts = cumsum_excl(jnp.concatenate([gt, eq], axis=0).astype(F32))
    gt_before, eq_before = counts[:rows], counts[rows:]
    needf = need.astype(F32)
    sel = gt | (eq & (eq_before < needf))
    pos = gt_before + jnp.minimum(eq_before, needf)
    pos_ref[...] = jnp.where(sel, pos.astype(jnp.int32), -1)


SC_CORES = 2
SC_SUBCORES = 16
SC_TILES = SC_CORES * SC_SUBCORES
DISPATCH_SLOTS = CAP * E // SC_TILES
DISPATCH_ROWS = 64
COMBINE_RANGE = 1024
COMBINE_ROWS = 32
SLAB = 128
NSLAB = D // SLAB


def _sc_mesh():
    return plsc.VectorSubcoreMesh(core_axis_name="c", subcore_axis_name="s",
                                  num_cores=SC_CORES, num_subcores=SC_SUBCORES)


def _dispatch_body(pos_hbm, aff_hbm, h_hbm, xin_hbm, idx_hbm, gate_hbm, pos_v, aff_v, idx_v, gate_v,
                   rows_a, rows_b, gsem_a, gsem_b, wsem_a, wsem_b):
    w = lax.axis_index("s") * SC_CORES + lax.axis_index("c")
    e = w // 2
    lo = (w % 2) * DISPATCH_SLOTS
    pltpu.sync_copy(pos_hbm.at[e], pos_v)
    pltpu.sync_copy(aff_hbm.at[e], aff_v)

    @pl.loop(0, S // 16)
    def _(i):
        p = pos_v[pl.ds(i * 16, 16)] - lo
        m = (p >= 0) & (p < DISPATCH_SLOTS)
        tok = lax.iota(jnp.int32, 16) + i * 16
        plsc.store_scatter(idx_v, [p], tok, mask=m)
        plsc.store_scatter(gate_v, [p], aff_v[pl.ds(i * 16, 16)], mask=m)

    pltpu.sync_copy(idx_v, idx_hbm.at[e, pl.ds(lo, DISPATCH_SLOTS)])
    pltpu.sync_copy(gate_v, gate_hbm.at[e, pl.ds(lo, DISPATCH_SLOTS)])

    bufs, gsems, wsems = (rows_a, rows_b), (gsem_a, gsem_b), (wsem_a, wsem_b)
    nchunk = DISPATCH_SLOTS // DISPATCH_ROWS

    def gather(j):
        return pltpu.async_copy(h_hbm.at[idx_v.at[pl.ds(j * DISPATCH_ROWS, DISPATCH_ROWS)]], bufs[j % 2], gsems[j % 2])

    pending_gather = gather(0)
    writes = [None, None]
    for j in range(nchunk):
        pending_gather.wait()
        writes[j % 2] = pltpu.async_copy(
            bufs[j % 2], xin_hbm.at[e, pl.ds(lo + j * DISPATCH_ROWS, DISPATCH_ROWS)], wsems[j % 2])
        if j + 1 < nchunk:
            if writes[(j + 1) % 2] is not None:
                writes[(j + 1) % 2].wait()
            pending_gather = gather(j + 1)
    writes[(nchunk - 2) % 2].wait()
    writes[(nchunk - 1) % 2].wait()


def _dispatch(pos, aff, h):
    return pl.kernel(
        _dispatch_body, mesh=_sc_mesh(),
        out_type=[jax.ShapeDtypeStruct((E, CAP, D // 2), jnp.int32),
                  jax.ShapeDtypeStruct((E, CAP), jnp.int32),
                  jax.ShapeDtypeStruct((E, CAP), F32)],
        scratch_types=[pltpu.VMEM((S,), jnp.int32), pltpu.VMEM((S,), F32),
                       pltpu.VMEM((DISPATCH_SLOTS,), jnp.int32), pltpu.VMEM((DISPATCH_SLOTS,), F32),
                       pltpu.VMEM((DISPATCH_ROWS, D // 2), jnp.int32), pltpu.VMEM((DISPATCH_ROWS, D // 2), jnp.int32),
                       pltpu.SemaphoreType.DMA, pltpu.SemaphoreType.DMA,
                       pltpu.SemaphoreType.DMA, pltpu.SemaphoreType.DMA],
        compiler_params=pltpu.CompilerParams(needs_layout_passes=False),
        name="moe_dispatch",
    )(pos, aff, h)


def _combine_ranges(seed, y_hbm, idx_hbm, out_hbm, idx_v, li_v, bufs):
    rows = bufs[:NSLAB]
    accs = bufs[NSLAB:]
    c = lax.axis_index("c")
    s = lax.axis_index("s")
    share = COMBINE_RANGE // SC_SUBCORES
    pltpu.sync_copy(idx_hbm.at[s], idx_v)
    lane = lax.iota(jnp.int32, 16)

    @pl.loop(0, S // COMBINE_RANGE // SC_CORES)
    def _(r):
        t0 = (r * SC_CORES + c) * COMBINE_RANGE
        row0 = t0 + s * share
        seed(row0, tuple(accs[q].at[pl.ds(s * share, share)] for q in range(NSLAB)))
        plsc.subcore_barrier()

        @pl.loop(0, CAP // COMBINE_ROWS)
        def _(j):
            hits = jnp.zeros((16,), jnp.int32)
            for v in range(COMBINE_ROWS // 16):
                t = idx_v[pl.ds(j * COMBINE_ROWS + v * 16, 16)] - t0
                ok = (t >= 0) & (t < COMBINE_RANGE)
                li_v[pl.ds(v * 16, 16)] = jnp.where(ok, t, COMBINE_RANGE + lane)
                hits = hits + plsc.all_reduce_population_count(ok)

            @pl.when(jnp.max(hits) > 0)
            def _():
                pltpu.sync_copy(
                    tuple(y_hbm.at[s, pl.ds(j * COMBINE_ROWS, COMBINE_ROWS), pl.ds(q * SLAB, SLAB)]
                          for q in range(NSLAB)),
                    tuple(rows))
                pltpu.sync_copy(tuple(rows), tuple(accs[q].at[li_v] for q in range(NSLAB)), add=True)

        plsc.subcore_barrier()
        pltpu.sync_copy(tuple(accs[q].at[pl.ds(s * share, share)] for q in range(NSLAB)),
                        tuple(out_hbm.at[pl.ds(row0, share), pl.ds(q * SLAB, SLAB)] for q in range(NSLAB)))


def _combine_body(y_hbm, idx_hbm, x_hbm, out_hbm, idx_v, li_v, *bufs):
    share = COMBINE_RANGE // SC_SUBCORES

    def seed(row0, dst):
        pltpu.sync_copy(tuple(x_hbm.at[pl.ds(row0, share), pl.ds(q * SLAB, SLAB)] for q in range(NSLAB)), dst)

    _combine_ranges(seed, y_hbm, idx_hbm, out_hbm, idx_v, li_v, bufs)


def _moe_sum_body(y_hbm, idx_hbm, out_hbm, idx_v, li_v, zero_v, *bufs):
    @pl.loop(0, zero_v.shape[0])
    def _(r):
        for v in range(SLAB // 16):
            zero_v[r, pl.ds(v * 16, 16)] = jnp.zeros((16,), F32)

    _combine_ranges(lambda row0, dst: pltpu.sync_copy(tuple(zero_v for _ in range(NSLAB)), dst),
                    y_hbm, idx_hbm, out_hbm, idx_v, li_v, bufs)


def _combine_scratch():
    return ([pltpu.VMEM((CAP,), jnp.int32), pltpu.VMEM((COMBINE_ROWS,), jnp.int32)],
            [pltpu.VMEM((COMBINE_ROWS, SLAB), F32) for _ in range(NSLAB)]
            + [pltpu.VMEM_SHARED((COMBINE_RANGE + 16, SLAB), F32) for _ in range(NSLAB)])


def _combine(y, idx, x, out_rows=S):
    head, tail = _combine_scratch()
    return pl.kernel(
        _combine_body, mesh=_sc_mesh(),
        out_type=jax.ShapeDtypeStruct((out_rows, D), F32),
        scratch_types=head + tail,
        compiler_params=pltpu.CompilerParams(needs_layout_passes=False),
        name="moe_combine",
    )(y, idx, x)


def _moe_sum(y, idx):
    head, tail = _combine_scratch()
    return pl.kernel(
        _moe_sum_body, mesh=_sc_mesh(),
        out_type=jax.ShapeDtypeStruct((S, D), F32),
        scratch_types=head + [pltpu.VMEM((COMBINE_RANGE // SC_SUBCORES, SLAB), F32)] + tail,
        compiler_params=pltpu.CompilerParams(needs_layout_passes=False),
        name="moe_sum",
    )(y, idx)


FINAL_TM = 1024


def _final_add_kernel(big_ref, x_ref, m_ref, o_ref):
    del big_ref
    o_ref[...] = x_ref[...] + m_ref[...]


def _final_add(big, x, moe, b):
    nb = S // FINAL_TM
    return pl.pallas_call(
        _final_add_kernel,
        grid=(nb,),
        in_specs=[
            pl.BlockSpec(memory_space=pl.ANY),
            pl.BlockSpec((FINAL_TM, D), lambda i: (i, 0)),
            pl.BlockSpec((FINAL_TM, D), lambda i: (i, 0)),
        ],
        out_specs=pl.BlockSpec((FINAL_TM, D), lambda i: (b * nb + i, 0)),
        out_shape=jax.ShapeDtypeStruct((B * S, D), F32),
        input_output_aliases={0: 0},
        compiler_params=_cparams(("arbitrary",)),
        name="final_add",
    )(big, x, moe)


def _ffn_kernel(x_ref, wg_ref, wu_ref, wd_ref, gate_ref, mod_ref, o_ref):
    f = pl.program_id(1)
    last = pl.num_programs(1) - 1
    nt = FFN_TF // FFN_WT
    wg = [wg_ref[0, 0, :, j * FFN_WT:(j + 1) * FFN_WT].astype(BF16) for j in range(nt)]
    wu = [wu_ref[0, 0, :, j * FFN_WT:(j + 1) * FFN_WT].astype(BF16) for j in range(nt)]
    wd = wd_ref[0, 0].astype(BF16)
    is_first = f == 0
    is_last = f == last
    g2 = jnp.where(is_last, mod_ref[0, 0][5:6], 1.0)
    g_row = gate_ref[pl.ds(pl.program_id(0), 1), :]
    g_col = jnp.broadcast_to(g_row, (128, CAP)).T[:, 0:1]
    for r in range(CAP // FFN_RT):
        rows = slice(r * FFN_RT, (r + 1) * FFN_RT)
        xr = _unpack_bf16_pairs(x_ref[0, rows, :])
        acts = []
        for j in range(nt):
            g = jnp.dot(xr, wg[j], preferred_element_type=F32)
            u = jnp.dot(xr, wu[j], preferred_element_type=F32)
            acts.append((g * _sigmoid(g) * u).astype(BF16))
        y = jnp.dot(jnp.concatenate(acts, axis=1), wd, preferred_element_type=F32)
        prev = jnp.where(is_first, 0.0, o_ref[0, rows, :])
        gate = jnp.where(is_last, g_col[rows, :], 1.0)
        o_ref[0, rows, :] = (prev + y) * gate * g2


def _select(aff, tri):
    return pl.pallas_call(
        _select_kernel,
        grid=(1,),
        in_specs=[
            pl.BlockSpec((E, S), lambda i: (0, 0)),
            pl.BlockSpec((128, 128), lambda i: (0, 0)),
        ],
        out_specs=pl.BlockSpec((E, S), lambda i: (0, 0)),
        out_shape=jax.ShapeDtypeStruct((E, S), jnp.int32),
        compiler_params=_cparams(("arbitrary",)),
        name="moe_select",
    )(aff, tri)


def _ffn(xin, gate, b, mod_all, w_gate, w_up, w_down, layer):
    return pl.pallas_call(
        _ffn_kernel,
        grid=(E, F // FFN_TF),
        in_specs=[
            pl.BlockSpec((1, CAP, D // 2), lambda e, f: (e, 0, 0)),
            pl.BlockSpec((1, 1, D, FFN_TF), lambda e, f: (layer, e, 0, f)),
            pl.BlockSpec((1, 1, D, FFN_TF), lambda e, f: (layer, e, 0, f)),
            pl.BlockSpec((1, 1, FFN_TF, D), lambda e, f: (layer, e, f, 0)),
            pl.BlockSpec((E, CAP), lambda e, f: (0, 0)),
            pl.BlockSpec((1, 1, 6, D), lambda e, f: (layer, b, 0, 0)),
        ],
        out_specs=pl.BlockSpec((1, CAP, D), lambda e, f: (e, 0, 0)),
        out_shape=jax.ShapeDtypeStruct((E, CAP, D), F32),
        compiler_params=_cparams(("arbitrary", "arbitrary"), 48),
        name="moe_ffn",
    )(xin, w_gate, w_up, w_down, gate, mod_all)


def kernel(x, c, w_ada, b_ada, norm_mix, norm_ffn, w_fourier_out, w_qkv, w_attn_out, q_gain, k_gain,
           sink, rel_bias, w_router, w_gate, w_up, w_down):
    m0, m2 = _dft_tables()
    m1 = _stage1_table()
    bucket_t = _bucket_table()
    tri = jnp.asarray(np.triu(np.ones((128, 128), np.float32)), BF16)
    mod_all = _ada(c, w_ada, b_ada)
    bias_tab = _bias_table(rel_bias, bucket_t)
    gain_mix = norm_mix.reshape(DEPTH, 1, D)
    gain_ffn = norm_ffn.reshape(DEPTH, 1, D)
    xs = [x, x]
    for layer in range(DEPTH):
        j = layer // 2
        wr = jnp.pad(w_router[layer], ((0, 0), (0, EP - E)))
        wr1 = wr.astype(BF16)
        wr = jnp.concatenate([wr1, (wr - wr1.astype(F32)).astype(BF16)], axis=1)
        if layer % 2 == 0:
            w_out_bf = w_fourier_out[j].astype(BF16)
            routed = [_fourier_layer(xs[b], b, mod_all, gain_mix, gain_ffn, wr, w_out_bf, layer, m0, m1, m2)
                      for b in range(B)]
        else:
            aw = _attn_weights(w_qkv[j], q_gain[j], k_gain[j], sink[j])
            w_out_bf = w_attn_out[j].astype(BF16)
            routed = [_attn_layer(xs[b], b, mod_all, gain_mix, gain_ffn, wr, aw, w_out_bf, bias_tab, layer)
                      for b in range(B)]
        xs = [routed[b][0] for b in range(B)]
        pos = [_select(routed[b][2], tri) for b in range(B)]
        disp = [_dispatch(pos[b], routed[b][2], routed[b][1]) for b in range(B)]
        ys = [_ffn(disp[b][0], disp[b][2], b, mod_all, w_gate, w_up, w_down, layer) for b in range(B)]
        if layer < DEPTH - 1:
            xs = [_combine(ys[b], disp[b][1], xs[b]) for b in range(B)]
    out = _combine(ys[0], disp[0][1], xs[0], out_rows=B * S)
    for b in range(1, B):
        out = _final_add(out, xs[b], _moe_sum(ys[b], disp[b][1]), b)
    return out.reshape(B, S, D)
```

```python
import math

import numpy as np
import jax
import jax.numpy as jnp
from jax import lax
from jax.experimental import pallas as pl
from jax.experimental.pallas import tpu as pltpu
from jax.experimental.pallas import tpu_sc as plsc

D = 1024
B = 2
S = 8192
DEPTH = 4
GROUPS = 4
GD = D // GROUPS
HD = 64
NH = 16
NKV = 4
GQA = NH // NKV
WINDOW = 128
BLK = 128
NBUCKETS = 32
MAXDIST = 128
E = 16
CAP = 2 * S // E
F = 2 * D
EPS = 1e-6
NEG_INF = -1e30

N1 = 128
N2 = 64
INV_NORM = 1.0 / math.sqrt(S * GD)

F32 = jnp.float32
BF16 = jnp.bfloat16


def _cparams(sem, vmem_mb=48):
    return pltpu.CompilerParams(dimension_semantics=sem, vmem_limit_bytes=vmem_mb * 1024 * 1024)


def _dft_tables():
    c = np.arange(GD)
    ang0 = 2.0 * np.pi * ((c[:, None] * c[None, :]) % GD) / GD
    m0 = np.concatenate([np.cos(ang0), -np.sin(ang0)], axis=1)
    k2 = np.arange(N2)
    ang2 = 2.0 * np.pi * ((k2[:, None] * k2[None, :]) % N2) / N2
    m2 = np.stack([np.cos(ang2), np.sin(ang2)], axis=2).reshape(N2, 2 * N2)
    return jnp.asarray(m0, BF16), jnp.asarray(m2, BF16)


def _stage1_table():
    s2 = np.arange(N2)[:, None, None]
    k1 = np.arange(N1)[None, :, None]
    s1 = np.arange(N1)[None, None, :]
    th = ((k1 * (N2 * s1 + s2)) % S) * (2.0 * np.pi / S)
    co, si = np.cos(th), np.sin(th)
    top = np.concatenate([co, si], axis=2)
    bot = np.concatenate([-si, co], axis=2)
    return jnp.asarray(np.concatenate([top, bot], axis=1).astype(np.float32), BF16)


def _bucket_table():
    q_off = np.arange(BLK)
    k_off = np.arange(3 * BLK) - BLK
    rel = k_off[:, None] - q_off[None, :]
    half = NBUCKETS // 2
    max_exact = half // 2
    ret = np.where(rel > 0, half, 0)
    n = np.abs(rel)
    nf = np.maximum(n, 1).astype(np.float32)
    ratio = (np.log(nf / np.float32(max_exact)) / np.float32(math.log(MAXDIST / max_exact))).astype(np.float32)
    large = max_exact + (ratio * np.float32(half - max_exact)).astype(np.int32)
    large = np.minimum(large, half - 1)
    bucket = ret + np.where(n < max_exact, n, large)
    return jnp.asarray(np.where(np.abs(rel) <= WINDOW, bucket, -1).astype(np.int32))


def _modulate(x, gain, shift, scale):
    ms = jnp.mean(x * x, axis=-1, keepdims=True)
    return x * lax.rsqrt(ms + EPS) * (gain * (1.0 + scale)) + shift


def _sigmoid(x):
    return 1.0 / (1.0 + jnp.exp(-x))


ADA_TN = 1536


def _ada_kernel(ct_ref, w_ref, b_ref, o_ref):
    ct = ct_ref[...]
    ca = ct * _sigmoid(ct)
    w = w_ref[0]
    for b in range(B):
        o_ref[0, b:b + 1, :] = jnp.sum(w * ca[:, b:b + 1], axis=0, keepdims=True) + b_ref[0]


def _ada(c, w_ada, b_ada):
    out = pl.pallas_call(
        _ada_kernel,
        grid=(DEPTH, 6 * D // ADA_TN),
        in_specs=[
            pl.BlockSpec((D, B), lambda l, j: (0, 0)),
            pl.BlockSpec((1, D, ADA_TN), lambda l, j: (l, 0, j)),
            pl.BlockSpec((1, 1, ADA_TN), lambda l, j: (l, 0, j)),
        ],
        out_specs=pl.BlockSpec((1, B, ADA_TN), lambda l, j: (l, 0, j)),
        out_shape=jax.ShapeDtypeStruct((DEPTH, B, 6 * D), F32),
        compiler_params=_cparams(("arbitrary", "arbitrary"), 32),
        name="ada_mod",
    )(c.T, w_ada, b_ada.reshape(DEPTH, 1, 6 * D))
    return out.reshape(DEPTH, B, 6, D)


def _mod_spec(layer, b):
    return pl.BlockSpec((1, 1, 6, D), lambda *_: (layer, b, 0, 0))


def _gain_spec(layer):
    return pl.BlockSpec((1, 1, D), lambda *_: (layer, 0, 0))


def _x_spec(x, b, tm):
    if x.ndim == 3:
        return pl.BlockSpec((None, tm, D), lambda i: (b, i, 0))
    return pl.BlockSpec((tm, D), lambda i: (i, 0))


EP = 128

_ROUTED_SHAPES = [jax.ShapeDtypeStruct((S, D), F32),
                  jax.ShapeDtypeStruct((S, D // 2), jnp.int32),
                  jax.ShapeDtypeStruct((E, S), F32)]


def _routed_specs(tm):
    return [pl.BlockSpec((tm, D), lambda i: (i, 0)),
            pl.BlockSpec((tm, D // 2), lambda i: (i, 0)),
            pl.BlockSpec((E, tm), lambda i: (0, i))]


F0_TM = 2048
SUB = 8
SEQ_PICKS = 2 * SUB


def _pack_bf16_pairs(h):
    hb = h.astype(BF16).astype(F32)
    lo = pltpu.bitcast(hb[:, :D // 2], jnp.int32)
    hi = pltpu.bitcast(hb[:, D // 2:], jnp.int32)
    return hi | lax.shift_right_logical(lo, jnp.int32(16))


def _unpack_bf16_pairs(xp):
    lo = pltpu.bitcast(xp << 16, F32).astype(BF16)
    hi = pltpu.bitcast(xp & jnp.int32(-65536), F32).astype(BF16)
    return jnp.concatenate([lo, hi], axis=1)


def _route(xn, m, gain_ffn, wr):
    h = _modulate(xn, gain_ffn, m[3:4], m[4:5])
    h1 = h.astype(BF16)
    h2 = (h - h1.astype(F32)).astype(BF16)
    part = jnp.dot(h1, wr, preferred_element_type=F32)
    logits = part[:, :EP] + part[:, EP:] + jnp.dot(h2, wr[:, :EP], preferred_element_type=F32)
    lt = logits.T[:E]
    ex = jnp.exp(lt - jnp.max(lt, axis=0, keepdims=True))
    return _pack_bf16_pairs(h), ex / jnp.sum(ex, axis=0, keepdims=True)


def _f0_kernel(x_ref, mod_ref, gain_ref, m0_ref, o_ref):
    m = mod_ref[0, 0]
    h = _modulate(x_ref[...], gain_ref[0], m[0:1], m[1:2]).astype(BF16)
    m0 = m0_ref[...]
    r = [jnp.dot(h[:, g * GD:(g + 1) * GD], m0, preferred_element_type=F32) for g in range(GROUPS)]
    o_ref[0] = _pack_bf16_pairs(jnp.concatenate([rg[:, :GD] for rg in r], axis=1))
    o_ref[1] = _pack_bf16_pairs(jnp.concatenate([rg[:, GD:] for rg in r], axis=1))


def _f1_kernel(w_ref, m1_ref, o_ref, scr_ref):
    n = o_ref.shape[0]
    for j in range(n):
        scr_ref[j] = w_ref[:, j, :]
    for j in range(n):
        w = _unpack_bf16_pairs(scr_ref[j])
        o_ref[j] = _pack_bf16_pairs(jnp.dot(m1_ref[j], w, preferred_element_type=F32))


def _f2_kernel(u_ref, m2_ref, o_ref, scr_ref):
    m2 = m2_ref[...]
    n = o_ref.shape[0]
    for k in range(n):
        scr_ref[k] = u_ref[:, k, :]
    for k in range(n):
        u = _unpack_bf16_pairs(scr_ref[k])
        o_ref[k] = _pack_bf16_pairs(jnp.dot(m2, u, preferred_element_type=F32))


def _f3_kernel(mp_ref, x_ref, w_ref, mod_ref, gf_ref, wr_ref, o_ref, h_ref, aff_ref, scr_ref):
    g1 = mod_ref[0, 0][2:3]
    for j in range(SUB):
        scr_ref[j * N1:(j + 1) * N1, :] = mp_ref[:, j, :]
    a = _unpack_bf16_pairs(scr_ref[...])
    y = jnp.dot(a, w_ref[...], preferred_element_type=F32)
    xn = x_ref[...] + (g1 * INV_NORM) * y
    o_ref[...] = xn
    h_ref[...], aff_ref[...] = _route(xn, mod_ref[0, 0], gf_ref[0], wr_ref[...])


def _fourier_layer(x, b, mod_all, gain, gain_ffn, wr, w_out_bf, layer, m0, m1, m2):
    wc = pl.pallas_call(
        _f0_kernel,
        grid=(S // F0_TM,),
        in_specs=[
            _x_spec(x, b, F0_TM),
            _mod_spec(layer, b), _gain_spec(layer),
            pl.BlockSpec((GD, 2 * GD), lambda i: (0, 0)),
        ],
        out_specs=pl.BlockSpec((2, F0_TM, D // 2), lambda i: (0, i, 0)),
        out_shape=jax.ShapeDtypeStruct((2, S, D // 2), jnp.int32),
        compiler_params=_cparams(("arbitrary",)),
        name="fourier_chan",
    )(x, mod_all, gain, m0)
    wc = wc.reshape(2 * N1, N2, D // 2)
    u = pl.pallas_call(
        _f1_kernel,
        grid=(N2 // SEQ_PICKS,),
        in_specs=[
            pl.BlockSpec((2 * N1, SEQ_PICKS, D // 2), lambda i: (0, i, 0)),
            pl.BlockSpec((SEQ_PICKS, 2 * N1, 2 * N1), lambda i: (i, 0, 0)),
        ],
        out_specs=pl.BlockSpec((SEQ_PICKS, 2 * N1, D // 2), lambda i: (i, 0, 0)),
        out_shape=jax.ShapeDtypeStruct((N2, 2 * N1, D // 2), jnp.int32),
        scratch_shapes=[pltpu.VMEM((SEQ_PICKS, 2 * N1, D // 2), jnp.int32)],
        compiler_params=_cparams(("arbitrary",)),
        name="fourier_seq1",
    )(wc, m1)
    u = u.reshape(2 * N2, N1, D // 2)
    mp = pl.pallas_call(
        _f2_kernel,
        grid=(N1 // SEQ_PICKS,),
        in_specs=[
            pl.BlockSpec((2 * N2, SEQ_PICKS, D // 2), lambda i: (0, i, 0)),
            pl.BlockSpec((N2, 2 * N2), lambda i: (0, 0)),
        ],
        out_specs=pl.BlockSpec((SEQ_PICKS, N2, D // 2), lambda i: (i, 0, 0)),
        out_shape=jax.ShapeDtypeStruct((N1, N2, D // 2), jnp.int32),
        scratch_shapes=[pltpu.VMEM((SEQ_PICKS, 2 * N2, D // 2), jnp.int32)],
        compiler_params=_cparams(("arbitrary",)),
        name="fourier_seq2",
    )(u, m2)
    return pl.pallas_call(
        _f3_kernel,
        grid=(N2 // SUB,),
        in_specs=[
            pl.BlockSpec((N1, SUB, D // 2), lambda i: (0, i, 0)),
            _x_spec(x, b, SUB * N1),
            pl.BlockSpec((D, D), lambda i: (0, 0)),
            _mod_spec(layer, b), _gain_spec(layer),
            pl.BlockSpec((D, 2 * EP), lambda i: (0, 0)),
        ],
        out_specs=_routed_specs(SUB * N1),
        out_shape=_ROUTED_SHAPES,
        scratch_shapes=[pltpu.VMEM((SUB * N1, D // 2), jnp.int32)],
        compiler_params=_cparams(("arbitrary",)),
        name="fourier_out",
    )(mp, x, w_out_bf, mod_all, gain_ffn, wr)


QKV_TM = 2048
TQ = 1024
NSB = TQ // BLK
KWIN = TQ + 2 * BLK
NBLK = S // BLK
KPAD = 128
VROWS = HD + 16
ATT_AHEAD = 1
LOG2E = math.log2(math.e)


def _bias_kernel(rb_ref, bucket_ref, o_ref):
    h = pl.program_id(0)
    bucket = bucket_ref[...]
    acc = jnp.full(bucket.shape, NEG_INF, F32)
    for k in range(NBUCKETS):
        acc = jnp.where(bucket == k, rb_ref[k, h] * LOG2E, acc)
    o_ref[0] = acc


def _bias_table(rel_bias, bucket_t):
    return pl.pallas_call(
        _bias_kernel,
        grid=(NH,),
        in_specs=[
            pl.BlockSpec(memory_space=pltpu.SMEM),
            pl.BlockSpec((3 * BLK, BLK), lambda h: (0, 0)),
        ],
        out_specs=pl.BlockSpec((1, 3 * BLK, BLK), lambda h: (h // GQA, 0, h % GQA)),
        out_shape=jax.ShapeDtypeStruct((NKV, 3 * BLK, GQA * BLK), F32),
        compiler_params=_cparams(("arbitrary",)),
        name="rel_bias_table",
    )(rel_bias, bucket_t)


def _qkv_kernel(x_ref, mod_ref, gain_ref, wqt_ref, wk_ref, wvt_ref, qg_ref, kg_ref, qt_ref, k_ref, vt_ref):
    m = mod_ref[0, 0]
    h = _modulate(x_ref[...], gain_ref[0], m[0:1], m[1:2]).astype(BF16)
    nt = (((1,), (1,)), ((), ()))
    qt = lax.dot_general(wqt_ref[...], h, nt, preferred_element_type=F32)
    tm = qt.shape[1]
    q3 = qt.reshape(NH, HD, tm)
    q3 = q3 * lax.rsqrt(jnp.mean(q3 * q3, axis=1, keepdims=True) + EPS)
    qt_ref[...] = (q3.reshape(NH * HD, tm) * qg_ref[...]).astype(BF16)
    k = jnp.dot(h, wk_ref[...], preferred_element_type=F32)
    for g in range(NKV):
        kg = k[:, g * KPAD:(g + 1) * KPAD]
        ms = jnp.sum(kg * kg, axis=-1, keepdims=True) * (1.0 / HD)
        k_ref[:, g * KPAD:(g + 1) * KPAD] = (kg * lax.rsqrt(ms + EPS) * kg_ref[...]).astype(BF16)
    vt_ref[...] = lax.dot_general(wvt_ref[...], h, nt, preferred_element_type=F32).astype(BF16)


def _attn_kernel(qt_ref, kp_ref, kc_ref, kn_ref, vp_ref, vc_ref, vn_ref, x_ref, w_ref, mod_ref,
                 bias_ref, sink_ref, gf_ref, wr_ref, o_ref, h_ref, aff_ref, att_ref, s_ref):
    i = pl.program_id(0)
    kwin = jnp.concatenate([kp_ref[...], kc_ref[...], kn_ref[...]], axis=0)
    vwin = jnp.concatenate([vp_ref[...], vc_ref[...], vn_ref[...]], axis=1)
    ones_rows = (lax.broadcasted_iota(jnp.int32, (VROWS - HD, KWIN), 0) == 0).astype(BF16)
    vaug = [jnp.concatenate([vwin[g * HD:(g + 1) * HD], ones_rows], axis=0) for g in range(NKV)]
    key_pos = i * TQ - BLK + lax.broadcasted_iota(jnp.int32, (KWIN, 1), 0)
    key_mask = jnp.where((key_pos >= 0) & (key_pos < S), 0.0, NEG_INF).astype(BF16)
    lane = lax.broadcasted_iota(jnp.int32, (1, NKV * KPAD), 1)
    kwin = jnp.where(lane % KPAD == HD, key_mask, kwin)
    q_ones = (lax.broadcasted_iota(jnp.int32, (KPAD - HD, GQA * BLK), 0) == 0).astype(BF16)
    items = [(g, r) for g in range(NKV) for r in range(NSB)]

    def scores(g, r):
        kr = kwin[r * BLK:r * BLK + 3 * BLK, g * KPAD:(g + 1) * KPAD]
        qg = jnp.concatenate(
            [qt_ref[(GQA * g + hh) * HD:(GQA * g + hh + 1) * HD, r * BLK:(r + 1) * BLK] for hh in range(GQA)],
            axis=1)
        qa = jnp.concatenate([qg, q_ones], axis=0)
        return jnp.dot(kr, qa, preferred_element_type=F32)

    def probs(s, g, r):
        sink = sink_ref[g] * LOG2E
        s = s + bias_ref[g]
        mx = jnp.maximum(jnp.max(s, axis=0, keepdims=True), sink)
        return jnp.exp2(s - mx).astype(BF16), jnp.exp2(sink - mx)

    nslot = ATT_AHEAD + 1
    for n in range(ATT_AHEAD):
        s_ref[n % nslot] = scores(*items[n])
    for n, (g, r) in enumerate(items):
        if n + ATT_AHEAD < len(items):
            s_ref[(n + ATT_AHEAD) % nslot] = scores(*items[n + ATT_AHEAD])
        p, psink = probs(s_ref[n % nslot], g, r)
        ot = jnp.dot(vaug[g][:, r * BLK:r * BLK + 3 * BLK], p, preferred_element_type=F32)
        ot = ot[:HD] * (1.0 / (ot[HD:HD + 1] + psink))
        for hh in range(GQA):
            hd0 = (GQA * g + hh) * HD
            att_ref[hd0:hd0 + HD, r * BLK:(r + 1) * BLK] = ot[:, hh * BLK:(hh + 1) * BLK]
    g1 = mod_ref[0, 0][2:3]
    att = att_ref[...].T.astype(BF16)
    y = jnp.dot(att, w_ref[...], preferred_element_type=F32)
    xn = x_ref[...] + g1 * y
    o_ref[...] = xn
    h_ref[...], aff_ref[...] = _route(xn, mod_ref[0, 0], gf_ref[0], wr_ref[...])


def _attn_weights(w_qkv, q_gain, k_gain, sink):
    wq, wk, wv = w_qkv[:, :NH * HD], w_qkv[:, NH * HD:(NH + NKV) * HD], w_qkv[:, (NH + NKV) * HD:]
    wqt = wq.T.astype(BF16)
    wvt = wv.T.astype(BF16)
    wk_pad = jnp.pad(wk.reshape(D, NKV, HD), ((0, 0), (0, 0), (0, KPAD - HD))).reshape(D, NKV * KPAD).astype(BF16)
    qg_col = jnp.tile(q_gain * (HD ** -0.5 * LOG2E), NH).reshape(NH * HD, 1)
    kg_row = jnp.pad(k_gain, (0, KPAD - HD)).reshape(1, KPAD)
    sink_row = jnp.repeat(sink, BLK).reshape(NKV, 1, GQA * BLK)
    return wqt, wk_pad, wvt, qg_col, kg_row, sink_row


def _attn_layer(x, b, mod_all, gain, gain_ffn, wr, aw, w_out_bf, bias_tab, layer):
    wqt, wk_pad, wvt, qg_col, kg_row, sink_row = aw
    qt, k, vt = pl.pallas_call(
        _qkv_kernel,
        grid=(S // QKV_TM,),
        in_specs=[
            _x_spec(x, b, QKV_TM),
            _mod_spec(layer, b), _gain_spec(layer),
            pl.BlockSpec((NH * HD, D), lambda i: (0, 0)),
            pl.BlockSpec((D, NKV * KPAD), lambda i: (0, 0)),
            pl.BlockSpec((NKV * HD, D), lambda i: (0, 0)),
            pl.BlockSpec((NH * HD, 1), lambda i: (0, 0)),
            pl.BlockSpec((1, KPAD), lambda i: (0, 0)),
        ],
        out_specs=[
            pl.BlockSpec((NH * HD, QKV_TM), lambda i: (0, i)),
            pl.BlockSpec((QKV_TM, NKV * KPAD), lambda i: (i, 0)),
            pl.BlockSpec((NKV * HD, QKV_TM), lambda i: (0, i)),
        ],
        out_shape=[
            jax.ShapeDtypeStruct((NH * HD, S), BF16),
            jax.ShapeDtypeStruct((S, NKV * KPAD), BF16),
            jax.ShapeDtypeStruct((NKV * HD, S), BF16),
        ],
        compiler_params=_cparams(("arbitrary",)),
        name="attn_qkv",
    )(x, mod_all, gain, wqt, wk_pad, wvt, qg_col, kg_row)

    kw = NKV * KPAD
    vw = NKV * HD
    kprev = pl.BlockSpec((BLK, kw), lambda i: (jnp.maximum(i * NSB - 1, 0), 0))
    kcur = pl.BlockSpec((TQ, kw), lambda i: (i, 0))
    knext = pl.BlockSpec((BLK, kw), lambda i: (jnp.minimum((i + 1) * NSB, NBLK - 1), 0))
    vprev = pl.BlockSpec((vw, BLK), lambda i: (0, jnp.maximum(i * NSB - 1, 0)))
    vcur = pl.BlockSpec((vw, TQ), lambda i: (0, i))
    vnext = pl.BlockSpec((vw, BLK), lambda i: (0, jnp.minimum((i + 1) * NSB, NBLK - 1)))
    return pl.pallas_call(
        _attn_kernel,
        grid=(S // TQ,),
        in_specs=[
            pl.BlockSpec((NH * HD, TQ), lambda i: (0, i)),
            kprev, kcur, knext, vprev, vcur, vnext,
            _x_spec(x, b, TQ),
            pl.BlockSpec((NH * HD, D), lambda i: (0, 0)),
            _mod_spec(layer, b),
            pl.BlockSpec((NKV, 3 * BLK, GQA * BLK), lambda i: (0, 0, 0)),
            pl.BlockSpec((NKV, 1, GQA * BLK), lambda i: (0, 0, 0)),
            _gain_spec(layer),
            pl.BlockSpec((D, 2 * EP), lambda i: (0, 0)),
        ],
        out_specs=_routed_specs(TQ),
        out_shape=_ROUTED_SHAPES,
        scratch_shapes=[pltpu.VMEM((NH * HD, TQ), F32), pltpu.VMEM((ATT_AHEAD + 1, 3 * BLK, GQA * BLK), F32)],
        compiler_params=_cparams(("arbitrary",)),
        name="attn_core",
    )(qt, k, k, k, vt, vt, vt, x, w_out_bf, mod_all, bias_tab, sink_row, gain_ffn, wr)


FFN_TF = 1024
FFN_RT = 512
FFN_WT = 256
SEL_BITS = 3


def _select_kernel(aff_ref, tri_ref, pos_ref):
    aff = aff_ref[...]
    bits = pltpu.bitcast(aff, jnp.int32)
    rows = aff.shape[0]

    def count_ge(v):
        return jnp.sum((bits >= v).astype(jnp.int32), axis=1, keepdims=True)

    def body(t, cur):
        sh = 30 - SEL_BITS * (t + 1)
        best = cur
        for digit in range(1, 1 << SEL_BITS):
            cand = cur | (jnp.int32(digit) << sh)
            best = jnp.where(count_ge(cand) >= CAP, cand, best)
        return best

    top = jnp.full((rows, 1), 1 << 30, jnp.int32)
    thr = lax.fori_loop(0, 30 // SEL_BITS, body, jnp.where(count_ge(top) >= CAP, top, 0))
    gt = bits > thr
    eq = bits == thr
    need = CAP - jnp.sum(gt.astype(jnp.int32), axis=1, keepdims=True)
    tri = tri_ref[...]

    def cumsum_excl(mask_f):
        off = jnp.zeros((mask_f.shape[0], 1), F32)
        outs = []
        for j in range(S // 128):
            mj = mask_f[:, j * 128:(j + 1) * 128]
            loc = jnp.dot(mj.astype(BF16), tri, preferred_element_type=F32)
            outs.append(loc - mj + off)
            off = off + loc[:, 127:128]
        return jnp.concatenate(outs, axis=1)

    counts = cumsum_excl(jnp.concatenate([gt, eq], axis=0).astype(F32))
    gt_before, eq_before = counts[:rows], counts[rows:]
    needf = need.astype(F32)
    sel = gt | (eq & (eq_before < needf))
    pos = gt_before + jnp.minimum(eq_before, needf)
    pos_ref[...] = jnp.where(sel, pos.astype(jnp.int32), -1)


SC_CORES = 2
SC_SUBCORES = 16
SC_TILES = SC_CORES * SC_SUBCORES
DISPATCH_SLOTS = CAP * E // SC_TILES
DISPATCH_ROWS = 64
COMBINE_RANGE = 1024
COMBINE_ROWS = 32
SLAB = 128
NSLAB = D // SLAB


def _sc_mesh():
    return plsc.VectorSubcoreMesh(core_axis_name="c", subcore_axis_name="s",
                                  num_cores=SC_CORES, num_subcores=SC_SUBCORES)


def _dispatch_body(pos_hbm, aff_hbm, h_hbm, xin_hbm, idx_hbm, gate_hbm, pos_v, aff_v, idx_v, gate_v,
                   rows_a, rows_b, gsem_a, gsem_b, wsem_a, wsem_b):
    w = lax.axis_index("s") * SC_CORES + lax.axis_index("c")
    e = w // 2
    lo = (w % 2) * DISPATCH_SLOTS
    pltpu.sync_copy(pos_hbm.at[e], pos_v)
    pltpu.sync_copy(aff_hbm.at[e], aff_v)

    @pl.loop(0, S // 16)
    def _(i):
        p = pos_v[pl.ds(i * 16, 16)] - lo
        m = (p >= 0) & (p < DISPATCH_SLOTS)
        tok = lax.iota(jnp.int32, 16) + i * 16
        plsc.store_scatter(idx_v, [p], tok, mask=m)
        plsc.store_scatter(gate_v, [p], aff_v[pl.ds(i * 16, 16)], mask=m)

    pltpu.sync_copy(idx_v, idx_hbm.at[e, pl.ds(lo, DISPATCH_SLOTS)])
    pltpu.sync_copy(gate_v, gate_hbm.at[e, pl.ds(lo, DISPATCH_SLOTS)])

    bufs, gsems, wsems = (rows_a, rows_b), (gsem_a, gsem_b), (wsem_a, wsem_b)
    nchunk = DISPATCH_SLOTS // DISPATCH_ROWS

    def gather(j):
        return pltpu.async_copy(h_hbm.at[idx_v.at[pl.ds(j * DISPATCH_ROWS, DISPATCH_ROWS)]], bufs[j % 2], gsems[j % 2])

    pending_gather = gather(0)
    writes = [None, None]
    for j in range(nchunk):
        pending_gather.wait()
        writes[j % 2] = pltpu.async_copy(
            bufs[j % 2], xin_hbm.at[e, pl.ds(lo + j * DISPATCH_ROWS, DISPATCH_ROWS)], wsems[j % 2])
        if j + 1 < nchunk:
            if writes[(j + 1) % 2] is not None:
                writes[(j + 1) % 2].wait()
            pending_gather = gather(j + 1)
    writes[(nchunk - 2) % 2].wait()
    writes[(nchunk - 1) % 2].wait()


def _dispatch(pos, aff, h):
    return pl.kernel(
        _dispatch_body, mesh=_sc_mesh(),
        out_type=[jax.ShapeDtypeStruct((E, CAP, D // 2), jnp.int32),
                  jax.ShapeDtypeStruct((E, CAP), jnp.int32),
                  jax.ShapeDtypeStruct((E, CAP), F32)],
        scratch_types=[pltpu.VMEM((S,), jnp.int32), pltpu.VMEM((S,), F32),
                       pltpu.VMEM((DISPATCH_SLOTS,), jnp.int32), pltpu.VMEM((DISPATCH_SLOTS,), F32),
                       pltpu.VMEM((DISPATCH_ROWS, D // 2), jnp.int32), pltpu.VMEM((DISPATCH_ROWS, D // 2), jnp.int32),
                       pltpu.SemaphoreType.DMA, pltpu.SemaphoreType.DMA,
                       pltpu.SemaphoreType.DMA, pltpu.SemaphoreType.DMA],
        compiler_params=pltpu.CompilerParams(needs_layout_passes=False),
        name="moe_dispatch",
    )(pos, aff, h)


def _combine_ranges(seed, y_hbm, idx_hbm, out_hbm, idx_v, li_v, bufs):
    rows = bufs[:NSLAB]
    accs = bufs[NSLAB:]
    c = lax.axis_index("c")
    s = lax.axis_index("s")
    share = COMBINE_RANGE // SC_SUBCORES
    pltpu.sync_copy(idx_hbm.at[s], idx_v)
    lane = lax.iota(jnp.int32, 16)

    @pl.loop(0, S // COMBINE_RANGE // SC_CORES)
    def _(r):
        t0 = (r * SC_CORES + c) * COMBINE_RANGE
        row0 = t0 + s * share
        seed(row0, tuple(accs[q].at[pl.ds(s * share, share)] for q in range(NSLAB)))
        plsc.subcore_barrier()

        @pl.loop(0, CAP // COMBINE_ROWS)
        def _(j):
            hits = jnp.zeros((16,), jnp.int32)
            for v in range(COMBINE_ROWS // 16):
                t = idx_v[pl.ds(j * COMBINE_ROWS + v * 16, 16)] - t0
                ok = (t >= 0) & (t < COMBINE_RANGE)
                li_v[pl.ds(v * 16, 16)] = jnp.where(ok, t, COMBINE_RANGE + lane)
                hits = hits + plsc.all_reduce_population_count(ok)

            @pl.when(jnp.max(hits) > 0)
            def _():
                pltpu.sync_copy(
                    tuple(y_hbm.at[s, pl.ds(j * COMBINE_ROWS, COMBINE_ROWS), pl.ds(q * SLAB, SLAB)]
                          for q in range(NSLAB)),
                    tuple(rows))
                pltpu.sync_copy(tuple(rows), tuple(accs[q].at[li_v] for q in range(NSLAB)), add=True)

        plsc.subcore_barrier()
        pltpu.sync_copy(tuple(accs[q].at[pl.ds(s * share, share)] for q in range(NSLAB)),
                        tuple(out_hbm.at[pl.ds(row0, share), pl.ds(q * SLAB, SLAB)] for q in range(NSLAB)))


def _combine_body(y_hbm, idx_hbm, x_hbm, out_hbm, idx_v, li_v, *bufs):
    share = COMBINE_RANGE // SC_SUBCORES

    def seed(row0, dst):
        pltpu.sync_copy(tuple(x_hbm.at[pl.ds(row0, share), pl.ds(q * SLAB, SLAB)] for q in range(NSLAB)), dst)

    _combine_ranges(seed, y_hbm, idx_hbm, out_hbm, idx_v, li_v, bufs)


def _moe_sum_body(y_hbm, idx_hbm, out_hbm, idx_v, li_v, zero_v, *bufs):
    @pl.loop(0, zero_v.shape[0])
    def _(r):
        for v in range(SLAB // 16):
            zero_v[r, pl.ds(v * 16, 16)] = jnp.zeros((16,), F32)

    _combine_ranges(lambda row0, dst: pltpu.sync_copy(tuple(zero_v for _ in range(NSLAB)), dst),
                    y_hbm, idx_hbm, out_hbm, idx_v, li_v, bufs)


def _combine_scratch():
    return ([pltpu.VMEM((CAP,), jnp.int32), pltpu.VMEM((COMBINE_ROWS,), jnp.int32)],
            [pltpu.VMEM((COMBINE_ROWS, SLAB), F32) for _ in range(NSLAB)]
            + [pltpu.VMEM_SHARED((COMBINE_RANGE + 16, SLAB), F32) for _ in range(NSLAB)])


def _combine(y, idx, x, out_rows=S):
    head, tail = _combine_scratch()
    return pl.kernel(
        _combine_body, mesh=_sc_mesh(),
        out_type=jax.ShapeDtypeStruct((out_rows, D), F32),
        scratch_types=head + tail,
        compiler_params=pltpu.CompilerParams(needs_layout_passes=False),
        name="moe_combine",
    )(y, idx, x)


def _moe_sum(y, idx):
    head, tail = _combine_scratch()
    return pl.kernel(
        _moe_sum_body, mesh=_sc_mesh(),
        out_type=jax.ShapeDtypeStruct((S, D), F32),
        scratch_types=head + [pltpu.VMEM((COMBINE_RANGE // SC_SUBCORES, SLAB), F32)] + tail,
        compiler_params=pltpu.CompilerParams(needs_layout_passes=False),
        name="moe_sum",
    )(y, idx)


FINAL_TM = 1024


def _final_add_kernel(big_ref, x_ref, m_ref, o_ref):
    del big_ref
    o_ref[...] = x_ref[...] + m_ref[...]


def _final_add(big, x, moe, b):
    nb = S // FINAL_TM
    return pl.pallas_call(
        _final_add_kernel,
        grid=(nb,),
        in_specs=[
            pl.BlockSpec(memory_space=pl.ANY),
            pl.BlockSpec((FINAL_TM, D), lambda i: (i, 0)),
            pl.BlockSpec((FINAL_TM, D), lambda i: (i, 0)),
        ],
        out_specs=pl.BlockSpec((FINAL_TM, D), lambda i: (b * nb + i, 0)),
        out_shape=jax.ShapeDtypeStruct((B * S, D), F32),
        input_output_aliases={0: 0},
        compiler_params=_cparams(("arbitrary",)),
        name="final_add",
    )(big, x, moe)


def _ffn_kernel(x_ref, wg_ref, wu_ref, wd_ref, gate_ref, mod_ref, o_ref):
    f = pl.program_id(1)
    last = pl.num_programs(1) - 1
    nt = FFN_TF // FFN_WT
    wg = [wg_ref[0, 0, :, j * FFN_WT:(j + 1) * FFN_WT].astype(BF16) for j in range(nt)]
    wu = [wu_ref[0, 0, :, j * FFN_WT:(j + 1) * FFN_WT].astype(BF16) for j in range(nt)]
    wd = wd_ref[0, 0].astype(BF16)
    is_first = f == 0
    is_last = f == last
    g2 = jnp.where(is_last, mod_ref[0, 0][5:6], 1.0)
    g_row = gate_ref[pl.ds(pl.program_id(0), 1), :]
    g_col = jnp.broadcast_to(g_row, (128, CAP)).T[:, 0:1]
    for r in range(CAP // FFN_RT):
        rows = slice(r * FFN_RT, (r + 1) * FFN_RT)
        xr = _unpack_bf16_pairs(x_ref[0, rows, :])
        acts = []
        for j in range(nt):
            g = jnp.dot(xr, wg[j], preferred_element_type=F32)
            u = jnp.dot(xr, wu[j], preferred_element_type=F32)
            acts.append((g * _sigmoid(g) * u).astype(BF16))
        y = jnp.dot(jnp.concatenate(acts, axis=1), wd, preferred_element_type=F32)
        prev = jnp.where(is_first, 0.0, o_ref[0, rows, :])
        gate = jnp.where(is_last, g_col[rows, :], 1.0)
        o_ref[0, rows, :] = (prev + y) * gate * g2


def _select(aff, tri):
    return pl.pallas_call(
        _select_kernel,
        grid=(1,),
        in_specs=[
            pl.BlockSpec((E, S), lambda i: (0, 0)),
            pl.BlockSpec((128, 128), lambda i: (0, 0)),
        ],
        out_specs=pl.BlockSpec((E, S), lambda i: (0, 0)),
        out_shape=jax.ShapeDtypeStruct((E, S), jnp.int32),
        compiler_params=_cparams(("arbitrary",)),
        name="moe_select",
    )(aff, tri)


def _ffn(xin, gate, b, mod_all, w_gate, w_up, w_down, layer):
    return pl.pallas_call(
        _ffn_kernel,
        grid=(E, F // FFN_TF),
        in_specs=[
            pl.BlockSpec((1, CAP, D // 2), lambda e, f: (e, 0, 0)),
            pl.BlockSpec((1, 1, D, FFN_TF), lambda e, f: (layer, e, 0, f)),
            pl.BlockSpec((1, 1, D, FFN_TF), lambda e, f: (layer, e, 0, f)),
            pl.BlockSpec((1, 1, FFN_TF, D), lambda e, f: (layer, e, f, 0)),
            pl.BlockSpec((E, CAP), lambda e, f: (0, 0)),
            pl.BlockSpec((1, 1, 6, D), lambda e, f: (layer, b, 0, 0)),
        ],
        out_specs=pl.BlockSpec((1, CAP, D), lambda e, f: (e, 0, 0)),
        out_shape=jax.ShapeDtypeStruct((E, CAP, D), F32),
        compiler_params=_cparams(("arbitrary", "arbitrary"), 48),
        name="moe_ffn",
    )(xin, w_gate, w_up, w_down, gate, mod_all)


def kernel(x, c, w_ada, b_ada, norm_mix, norm_ffn, w_fourier_out, w_qkv, w_attn_out, q_gain, k_gain,
           sink, rel_bias, w_router, w_gate, w_up, w_down):
    m0, m2 = _dft_tables()
    m1 = _stage1_table()
    bucket_t = _bucket_table()
    tri = jnp.asarray(np.triu(np.ones((128, 128), np.float32)), BF16)
    mod_all = _ada(c, w_ada, b_ada)
    bias_tab = _bias_table(rel_bias, bucket_t)
    gain_mix = norm_mix.reshape(DEPTH, 1, D)
    gain_ffn = norm_ffn.reshape(DEPTH, 1, D)
    xs = [x, x]
    for layer in range(DEPTH):
        j = layer // 2
        wr = jnp.pad(w_router[layer], ((0, 0), (0, EP - E)))
        wr1 = wr.astype(BF16)
        wr = jnp.concatenate([wr1, (wr - wr1.astype(F32)).astype(BF16)], axis=1)
        if layer % 2 == 0:
            w_out_bf = w_fourier_out[j].astype(BF16)
            routed = [_fourier_layer(xs[b], b, mod_all, gain_mix, gain_ffn, wr, w_out_bf, layer, m0, m1, m2)
                      for b in range(B)]
        else:
            aw = _attn_weights(w_qkv[j], q_gain[j], k_gain[j], sink[j])
            w_out_bf = w_attn_out[j].astype(BF16)
            routed = [_attn_layer(xs[b], b, mod_all, gain_mix, gain_ffn, wr, aw, w_out_bf, bias_tab, layer)
                      for b in range(B)]
        xs = [routed[b][0] for b in range(B)]
        pos = [_select(routed[b][2], tri) for b in range(B)]
        disp = [_dispatch(pos[b], routed[b][2], routed[b][1]) for b in range(B)]
        ys = [_ffn(disp[b][0], disp[b][2], b, mod_all, w_gate, w_up, w_down, layer) for b in range(B)]
        if layer < DEPTH - 1:
            xs = [_combine(ys[b], disp[b][1], xs[b]) for b in range(B)]
    out = _combine(ys[0], disp[0][1], xs[0], out_rows=B * S)
    for b in range(1, B):
        out = _final_add(out, xs[b], _moe_sum(ys[b], disp[b][1]), b)
    return out.reshape(B, S, D)
```

```python
import math

import numpy as np
import jax
import jax.numpy as jnp
from jax import lax
from jax.experimental import pallas as pl
from jax.experimental.pallas import tpu as pltpu
from jax.experimental.pallas import tpu_sc as plsc

D = 1024
B = 2
S = 8192
DEPTH = 4
GROUPS = 4
GD = D // GROUPS
HD = 64
NH = 16
NKV = 4
GQA = NH // NKV
WINDOW = 128
BLK = 128
NBUCKETS = 32
MAXDIST = 128
E = 16
CAP = 2 * S // E
F = 2 * D
EPS = 1e-6
NEG_INF = -1e30

N1 = 128
N2 = 64
INV_NORM = 1.0 / math.sqrt(S * GD)

F32 = jnp.float32
BF16 = jnp.bfloat16


def _cparams(sem, vmem_mb=48):
    return pltpu.CompilerParams(dimension_semantics=sem, vmem_limit_bytes=vmem_mb * 1024 * 1024)


def _dft_tables():
    c = np.arange(GD)
    ang0 = 2.0 * np.pi * ((c[:, None] * c[None, :]) % GD) / GD
    m0 = np.concatenate([np.cos(ang0), -np.sin(ang0)], axis=1)
    k2 = np.arange(N2)
    ang2 = 2.0 * np.pi * ((k2[:, None] * k2[None, :]) % N2) / N2
    m2 = np.stack([np.cos(ang2), np.sin(ang2)], axis=2).reshape(N2, 2 * N2)
    return jnp.asarray(m0, BF16), jnp.asarray(m2, BF16)


def _stage1_table():
    s2 = np.arange(N2)[:, None, None]
    k1 = np.arange(N1)[None, :, None]
    s1 = np.arange(N1)[None, None, :]
    th = ((k1 * (N2 * s1 + s2)) % S) * (2.0 * np.pi / S)
    co, si = np.cos(th), np.sin(th)
    top = np.concatenate([co, si], axis=2)
    bot = np.concatenate([-si, co], axis=2)
    return jnp.asarray(np.concatenate([top, bot], axis=1).astype(np.float32), BF16)


def _bucket_table():
    q_off = np.arange(BLK)
    k_off = np.arange(3 * BLK) - BLK
    rel = k_off[:, None] - q_off[None, :]
    half = NBUCKETS // 2
    max_exact = half // 2
    ret = np.where(rel > 0, half, 0)
    n = np.abs(rel)
    nf = np.maximum(n, 1).astype(np.float32)
    ratio = (np.log(nf / np.float32(max_exact)) / np.float32(math.log(MAXDIST / max_exact))).astype(np.float32)
    large = max_exact + (ratio * np.float32(half - max_exact)).astype(np.int32)
    large = np.minimum(large, half - 1)
    bucket = ret + np.where(n < max_exact, n, large)
    return jnp.asarray(np.where(np.abs(rel) <= WINDOW, bucket, -1).astype(np.int32))


def _modulate(x, gain, shift, scale):
    ms = jnp.mean(x * x, axis=-1, keepdims=True)
    return x * lax.rsqrt(ms + EPS) * (gain * (1.0 + scale)) + shift


def _sigmoid(x):
    return 1.0 / (1.0 + jnp.exp(-x))


ADA_TN = 1536


def _ada_kernel(ct_ref, w_ref, b_ref, o_ref):
    ct = ct_ref[...]
    ca = ct * _sigmoid(ct)
    w = w_ref[0]
    for b in range(B):
        o_ref[0, b:b + 1, :] = jnp.sum(w * ca[:, b:b + 1], axis=0, keepdims=True) + b_ref[0]


def _ada(c, w_ada, b_ada):
    out = pl.pallas_call(
        _ada_kernel,
        grid=(DEPTH, 6 * D // ADA_TN),
        in_specs=[
            pl.BlockSpec((D, B), lambda l, j: (0, 0)),
            pl.BlockSpec((1, D, ADA_TN), lambda l, j: (l, 0, j)),
            pl.BlockSpec((1, 1, ADA_TN), lambda l, j: (l, 0, j)),
        ],
        out_specs=pl.BlockSpec((1, B, ADA_TN), lambda l, j: (l, 0, j)),
        out_shape=jax.ShapeDtypeStruct((DEPTH, B, 6 * D), F32),
        compiler_params=_cparams(("arbitrary", "arbitrary"), 32),
        name="ada_mod",
    )(c.T, w_ada, b_ada.reshape(DEPTH, 1, 6 * D))
    return out.reshape(DEPTH, B, 6, D)


def _mod_spec(layer, b):
    return pl.BlockSpec((1, 1, 6, D), lambda *_: (layer, b, 0, 0))


def _gain_spec(layer):
    return pl.BlockSpec((1, 1, D), lambda *_: (layer, 0, 0))


def _x_spec(x, b, tm):
    if x.ndim == 3:
        return pl.BlockSpec((None, tm, D), lambda i: (b, i, 0))
    return pl.BlockSpec((tm, D), lambda i: (i, 0))


EP = 128

_ROUTED_SHAPES = [jax.ShapeDtypeStruct((S, D), F32),
                  jax.ShapeDtypeStruct((S, D // 2), jnp.int32),
                  jax.ShapeDtypeStruct((E, S), F32)]


def _routed_specs(tm):
    return [pl.BlockSpec((tm, D), lambda i: (i, 0)),
            pl.BlockSpec((tm, D // 2), lambda i: (i, 0)),
            pl.BlockSpec((E, tm), lambda i: (0, i))]


F0_TM = 1024
SUB = 8
SEQ_PICKS = 2 * SUB


def _pack_bf16_pairs(h):
    hb = h.astype(BF16).astype(F32)
    lo = pltpu.bitcast(hb[:, :D // 2], jnp.int32)
    hi = pltpu.bitcast(hb[:, D // 2:], jnp.int32)
    return hi | lax.shift_right_logical(lo, jnp.int32(16))


def _unpack_bf16_pairs(xp):
    lo = pltpu.bitcast(xp << 16, F32).astype(BF16)
    hi = pltpu.bitcast(xp & jnp.int32(-65536), F32).astype(BF16)
    return jnp.concatenate([lo, hi], axis=1)


def _route(xn, m, gain_ffn, wr):
    h = _modulate(xn, gain_ffn, m[3:4], m[4:5])
    h1 = h.astype(BF16)
    h2 = (h - h1.astype(F32)).astype(BF16)
    part = jnp.dot(h1, wr, preferred_element_type=F32)
    logits = part[:, :EP] + part[:, EP:] + jnp.dot(h2, wr[:, :EP], preferred_element_type=F32)
    lt = logits.T[:E]
    ex = jnp.exp(lt - jnp.max(lt, axis=0, keepdims=True))
    return _pack_bf16_pairs(h), ex / jnp.sum(ex, axis=0, keepdims=True)


def _f0_kernel(x_ref, mod_ref, gain_ref, m0_ref, o_ref):
    m = mod_ref[0, 0]
    h = _modulate(x_ref[...], gain_ref[0], m[0:1], m[1:2]).astype(BF16)
    m0 = m0_ref[...]
    r = [jnp.dot(h[:, g * GD:(g + 1) * GD], m0, preferred_element_type=F32) for g in range(GROUPS)]
    o_ref[0] = _pack_bf16_pairs(jnp.concatenate([rg[:, :GD] for rg in r], axis=1))
    o_ref[1] = _pack_bf16_pairs(jnp.concatenate([rg[:, GD:] for rg in r], axis=1))


def _f1_kernel(w_ref, m1_ref, o_ref, scr_ref):
    n = o_ref.shape[0]
    for j in range(n):
        scr_ref[j] = w_ref[:, j, :]
    for j in range(n):
        w = _unpack_bf16_pairs(scr_ref[j])
        o_ref[j] = _pack_bf16_pairs(jnp.dot(m1_ref[j], w, preferred_element_type=F32))


def _f2_kernel(u_ref, m2_ref, o_ref, scr_ref):
    m2 = m2_ref[...]
    n = o_ref.shape[0]
    for k in range(n):
        scr_ref[k] = u_ref[:, k, :]
    for k in range(n):
        u = _unpack_bf16_pairs(scr_ref[k])
        o_ref[k] = _pack_bf16_pairs(jnp.dot(m2, u, preferred_element_type=F32))


def _f3_kernel(mp_ref, x_ref, w_ref, mod_ref, gf_ref, wr_ref, o_ref, h_ref, aff_ref, scr_ref):
    g1 = mod_ref[0, 0][2:3]
    for j in range(SUB):
        scr_ref[j * N1:(j + 1) * N1, :] = mp_ref[:, j, :]
    a = _unpack_bf16_pairs(scr_ref[...])
    y = jnp.dot(a, w_ref[...].astype(BF16), preferred_element_type=F32)
    xn = x_ref[...] + (g1 * INV_NORM) * y
    o_ref[...] = xn
    h_ref[...], aff_ref[...] = _route(xn, mod_ref[0, 0], gf_ref[0], wr_ref[...])


def _fourier_layer(x, b, mod_all, gain, gain_ffn, wr, w_out, layer, m0, m1, m2):
    wc = pl.pallas_call(
        _f0_kernel,
        grid=(S // F0_TM,),
        in_specs=[
            _x_spec(x, b, F0_TM),
            _mod_spec(layer, b), _gain_spec(layer),
            pl.BlockSpec((GD, 2 * GD), lambda i: (0, 0)),
        ],
        out_specs=pl.BlockSpec((2, F0_TM, D // 2), lambda i: (0, i, 0)),
        out_shape=jax.ShapeDtypeStruct((2, S, D // 2), jnp.int32),
        compiler_params=_cparams(("arbitrary",)),
        name="fourier_chan",
    )(x, mod_all, gain, m0)
    wc = wc.reshape(2 * N1, N2, D // 2)
    u = pl.pallas_call(
        _f1_kernel,
        grid=(N2 // SEQ_PICKS,),
        in_specs=[
            pl.BlockSpec((2 * N1, SEQ_PICKS, D // 2), lambda i: (0, i, 0)),
            pl.BlockSpec((SEQ_PICKS, 2 * N1, 2 * N1), lambda i: (i, 0, 0)),
        ],
        out_specs=pl.BlockSpec((SEQ_PICKS, 2 * N1, D // 2), lambda i: (i, 0, 0)),
        out_shape=jax.ShapeDtypeStruct((N2, 2 * N1, D // 2), jnp.int32),
        scratch_shapes=[pltpu.VMEM((SEQ_PICKS, 2 * N1, D // 2), jnp.int32)],
        compiler_params=_cparams(("arbitrary",)),
        name="fourier_seq1",
    )(wc, m1)
    u = u.reshape(2 * N2, N1, D // 2)
    mp = pl.pallas_call(
        _f2_kernel,
        grid=(N1 // SEQ_PICKS,),
        in_specs=[
            pl.BlockSpec((2 * N2, SEQ_PICKS, D // 2), lambda i: (0, i, 0)),
            pl.BlockSpec((N2, 2 * N2), lambda i: (0, 0)),
        ],
        out_specs=pl.BlockSpec((SEQ_PICKS, N2, D // 2), lambda i: (i, 0, 0)),
        out_shape=jax.ShapeDtypeStruct((N1, N2, D // 2), jnp.int32),
        scratch_shapes=[pltpu.VMEM((SEQ_PICKS, 2 * N2, D // 2), jnp.int32)],
        compiler_params=_cparams(("arbitrary",)),
        name="fourier_seq2",
    )(u, m2)
    return pl.pallas_call(
        _f3_kernel,
        grid=(N2 // SUB,),
        in_specs=[
            pl.BlockSpec((N1, SUB, D // 2), lambda i: (0, i, 0)),
            _x_spec(x, b, SUB * N1),
            pl.BlockSpec((None, D, D), lambda i: (layer // 2, 0, 0)),
            _mod_spec(layer, b), _gain_spec(layer),
            pl.BlockSpec((D, 2 * EP), lambda i: (0, 0)),
        ],
        out_specs=_routed_specs(SUB * N1),
        out_shape=_ROUTED_SHAPES,
        scratch_shapes=[pltpu.VMEM((SUB * N1, D // 2), jnp.int32)],
        compiler_params=_cparams(("arbitrary",)),
        name="fourier_out",
    )(mp, x, w_out, mod_all, gain_ffn, wr)


QKV_TM = 1024
TQ = 1024
NSB = TQ // BLK
KWIN = TQ + 2 * BLK
NBLK = S // BLK
KPAD = 128
VROWS = HD + 16
ATT_AHEAD = 1
LOG2E = math.log2(math.e)


def _bias_kernel(rb_ref, bucket_ref, o_ref):
    h = pl.program_id(0)
    bucket = bucket_ref[...]
    acc = jnp.full(bucket.shape, NEG_INF, F32)
    for k in range(NBUCKETS):
        acc = jnp.where(bucket == k, rb_ref[k, h] * LOG2E, acc)
    o_ref[0] = acc


def _bias_table(rel_bias, bucket_t):
    return pl.pallas_call(
        _bias_kernel,
        grid=(NH,),
        in_specs=[
            pl.BlockSpec(memory_space=pltpu.SMEM),
            pl.BlockSpec((3 * BLK, BLK), lambda h: (0, 0)),
        ],
        out_specs=pl.BlockSpec((1, 3 * BLK, BLK), lambda h: (h // GQA, 0, h % GQA)),
        out_shape=jax.ShapeDtypeStruct((NKV, 3 * BLK, GQA * BLK), F32),
        compiler_params=_cparams(("arbitrary",)),
        name="rel_bias_table",
    )(rel_bias, bucket_t)


def _qkv_kernel(x_ref, mod_ref, gain_ref, wqt_ref, wk_ref, wvt_ref, qg_ref, kg_ref, qt_ref, k_ref, vt_ref):
    m = mod_ref[0, 0]
    h = _modulate(x_ref[...], gain_ref[0], m[0:1], m[1:2]).astype(BF16)
    nt = (((1,), (1,)), ((), ()))
    qt = lax.dot_general(wqt_ref[...], h, nt, preferred_element_type=F32)
    tm = qt.shape[1]
    q3 = qt.reshape(NH, HD, tm)
    q3 = q3 * lax.rsqrt(jnp.mean(q3 * q3, axis=1, keepdims=True) + EPS)
    qt_ref[...] = (q3.reshape(NH * HD, tm) * qg_ref[...]).astype(BF16)
    k = jnp.dot(h, wk_ref[...], preferred_element_type=F32)
    for g in range(NKV):
        kg = k[:, g * KPAD:(g + 1) * KPAD]
        ms = jnp.sum(kg * kg, axis=-1, keepdims=True) * (1.0 / HD)
        k_ref[:, g * KPAD:(g + 1) * KPAD] = (kg * lax.rsqrt(ms + EPS) * kg_ref[...]).astype(BF16)
    vt_ref[...] = lax.dot_general(wvt_ref[...], h, nt, preferred_element_type=F32).astype(BF16)


def _attn_kernel(qt_ref, kp_ref, kc_ref, kn_ref, vp_ref, vc_ref, vn_ref, x_ref, w_ref, mod_ref,
                 bias_ref, sink_ref, gf_ref, wr_ref, o_ref, h_ref, aff_ref, att_ref, s_ref):
    i = pl.program_id(0)
    kwin = jnp.concatenate([kp_ref[...], kc_ref[...], kn_ref[...]], axis=0)
    vwin = jnp.concatenate([vp_ref[...], vc_ref[...], vn_ref[...]], axis=1)
    ones_rows = (lax.broadcasted_iota(jnp.int32, (VROWS - HD, KWIN), 0) == 0).astype(BF16)
    vaug = [jnp.concatenate([vwin[g * HD:(g + 1) * HD], ones_rows], axis=0) for g in range(NKV)]
    key_pos = i * TQ - BLK + lax.broadcasted_iota(jnp.int32, (KWIN, 1), 0)
    key_mask = jnp.where((key_pos >= 0) & (key_pos < S), 0.0, NEG_INF).astype(BF16)
    lane = lax.broadcasted_iota(jnp.int32, (1, NKV * KPAD), 1)
    kwin = jnp.where(lane % KPAD == HD, key_mask, kwin)
    q_ones = (lax.broadcasted_iota(jnp.int32, (KPAD - HD, GQA * BLK), 0) == 0).astype(BF16)
    items = [(g, r) for g in range(NKV) for r in range(NSB)]

    def scores(g, r):
        kr = kwin[r * BLK:r * BLK + 3 * BLK, g * KPAD:(g + 1) * KPAD]
        qg = jnp.concatenate(
            [qt_ref[(GQA * g + hh) * HD:(GQA * g + hh + 1) * HD, r * BLK:(r + 1) * BLK] for hh in range(GQA)],
            axis=1)
        qa = jnp.concatenate([qg, q_ones], axis=0)
        return jnp.dot(kr, qa, preferred_element_type=F32)

    def probs(s, g, r):
        sink = sink_ref[g] * LOG2E
        s = s + bias_ref[g]
        mx = jnp.maximum(jnp.max(s, axis=0, keepdims=True), sink)
        return jnp.exp2(s - mx).astype(BF16), jnp.exp2(sink - mx)

    nslot = ATT_AHEAD + 1
    for n in range(ATT_AHEAD):
        s_ref[n % nslot] = scores(*items[n])
    for n, (g, r) in enumerate(items):
        if n + ATT_AHEAD < len(items):
            s_ref[(n + ATT_AHEAD) % nslot] = scores(*items[n + ATT_AHEAD])
        p, psink = probs(s_ref[n % nslot], g, r)
        ot = jnp.dot(vaug[g][:, r * BLK:r * BLK + 3 * BLK], p, preferred_element_type=F32)
        ot = ot[:HD] * (1.0 / (ot[HD:HD + 1] + psink))
        for hh in range(GQA):
            hd0 = (GQA * g + hh) * HD
            att_ref[hd0:hd0 + HD, r * BLK:(r + 1) * BLK] = ot[:, hh * BLK:(hh + 1) * BLK]
    g1 = mod_ref[0, 0][2:3]
    att = att_ref[...].T.astype(BF16)
    y = jnp.dot(att, w_ref[...].astype(BF16), preferred_element_type=F32)
    xn = x_ref[...] + g1 * y
    o_ref[...] = xn
    h_ref[...], aff_ref[...] = _route(xn, mod_ref[0, 0], gf_ref[0], wr_ref[...])


def _attn_weights(w_qkv, q_gain, k_gain, sink):
    wq, wk, wv = w_qkv[:, :NH * HD], w_qkv[:, NH * HD:(NH + NKV) * HD], w_qkv[:, (NH + NKV) * HD:]
    wqt = wq.T.astype(BF16)
    wvt = wv.T.astype(BF16)
    wk_pad = jnp.pad(wk.reshape(D, NKV, HD), ((0, 0), (0, 0), (0, KPAD - HD))).reshape(D, NKV * KPAD).astype(BF16)
    qg_col = jnp.tile(q_gain * (HD ** -0.5 * LOG2E), NH).reshape(NH * HD, 1)
    kg_row = jnp.pad(k_gain, (0, KPAD - HD)).reshape(1, KPAD)
    sink_row = jnp.repeat(sink, BLK).reshape(NKV, 1, GQA * BLK)
    return wqt, wk_pad, wvt, qg_col, kg_row, sink_row


def _attn_layer(x, b, mod_all, gain, gain_ffn, wr, aw, w_out, bias_tab, layer):
    wqt, wk_pad, wvt, qg_col, kg_row, sink_row = aw
    qt, k, vt = pl.pallas_call(
        _qkv_kernel,
        grid=(S // QKV_TM,),
        in_specs=[
            _x_spec(x, b, QKV_TM),
            _mod_spec(layer, b), _gain_spec(layer),
            pl.BlockSpec((NH * HD, D), lambda i: (0, 0)),
            pl.BlockSpec((D, NKV * KPAD), lambda i: (0, 0)),
            pl.BlockSpec((NKV * HD, D), lambda i: (0, 0)),
            pl.BlockSpec((NH * HD, 1), lambda i: (0, 0)),
            pl.BlockSpec((1, KPAD), lambda i: (0, 0)),
        ],
        out_specs=[
            pl.BlockSpec((NH * HD, QKV_TM), lambda i: (0, i)),
            pl.BlockSpec((QKV_TM, NKV * KPAD), lambda i: (i, 0)),
            pl.BlockSpec((NKV * HD, QKV_TM), lambda i: (0, i)),
        ],
        out_shape=[
            jax.ShapeDtypeStruct((NH * HD, S), BF16),
            jax.ShapeDtypeStruct((S, NKV * KPAD), BF16),
            jax.ShapeDtypeStruct((NKV * HD, S), BF16),
        ],
        compiler_params=_cparams(("arbitrary",)),
        name="attn_qkv",
    )(x, mod_all, gain, wqt, wk_pad, wvt, qg_col, kg_row)

    kw = NKV * KPAD
    vw = NKV * HD
    kprev = pl.BlockSpec((BLK, kw), lambda i: (jnp.maximum(i * NSB - 1, 0), 0))
    kcur = pl.BlockSpec((TQ, kw), lambda i: (i, 0))
    knext = pl.BlockSpec((BLK, kw), lambda i: (jnp.minimum((i + 1) * NSB, NBLK - 1), 0))
    vprev = pl.BlockSpec((vw, BLK), lambda i: (0, jnp.maximum(i * NSB - 1, 0)))
    vcur = pl.BlockSpec((vw, TQ), lambda i: (0, i))
    vnext = pl.BlockSpec((vw, BLK), lambda i: (0, jnp.minimum((i + 1) * NSB, NBLK - 1)))
    return pl.pallas_call(
        _attn_kernel,
        grid=(S // TQ,),
        in_specs=[
            pl.BlockSpec((NH * HD, TQ), lambda i: (0, i)),
            kprev, kcur, knext, vprev, vcur, vnext,
            _x_spec(x, b, TQ),
            pl.BlockSpec((None, NH * HD, D), lambda i: (layer // 2, 0, 0)),
            _mod_spec(layer, b),
            pl.BlockSpec((NKV, 3 * BLK, GQA * BLK), lambda i: (0, 0, 0)),
            pl.BlockSpec((NKV, 1, GQA * BLK), lambda i: (0, 0, 0)),
            _gain_spec(layer),
            pl.BlockSpec((D, 2 * EP), lambda i: (0, 0)),
        ],
        out_specs=_routed_specs(TQ),
        out_shape=_ROUTED_SHAPES,
        scratch_shapes=[pltpu.VMEM((NH * HD, TQ), F32), pltpu.VMEM((ATT_AHEAD + 1, 3 * BLK, GQA * BLK), F32)],
        compiler_params=_cparams(("arbitrary",)),
        name="attn_core",
    )(qt, k, k, k, vt, vt, vt, x, w_out, mod_all, bias_tab, sink_row, gain_ffn, wr)


FFN_TF = 1024
FFN_RT = 512
FFN_WT = 256


def _select_kernel(aff_ref, tri_ref, pos_ref):
    aff = aff_ref[...]
    bits = pltpu.bitcast(aff, jnp.int32)
    rows = aff.shape[0]

    def count_ge(v):
        return jnp.sum((bits >= v).astype(jnp.int32), axis=1, keepdims=True)

    def body(t, cur):
        sh = 28 - 2 * t
        c1, c2, c3 = cur | (jnp.int32(1) << sh), cur | (jnp.int32(2) << sh), cur | (jnp.int32(3) << sh)
        n1, n2, n3 = count_ge(c1), count_ge(c2), count_ge(c3)
        return jnp.where(n3 >= CAP, c3, jnp.where(n2 >= CAP, c2, jnp.where(n1 >= CAP, c1, cur)))

    top = jnp.full((rows, 1), 1 << 30, jnp.int32)
    thr = lax.fori_loop(0, 15, body, jnp.where(count_ge(top) >= CAP, top, 0))
    gt = bits > thr
    eq = bits == thr
    need = CAP - jnp.sum(gt.astype(jnp.int32), axis=1, keepdims=True)
    tri = tri_ref[...]

    def cumsum_excl(mask_f):
        off = jnp.zeros((mask_f.shape[0], 1), F32)
        outs = []
        for j in range(S // 128):
            mj = mask_f[:, j * 128:(j + 1) * 128]
            loc = jnp.dot(mj.astype(BF16), tri, preferred_element_type=F32)
            outs.append(loc - mj + off)
            off = off + loc[:, 127:128]
        return jnp.concatenate(outs, axis=1)

    counts = cumsum_excl(jnp.concatenate([gt, eq], axis=0).astype(F32))
    gt_before, eq_before = counts[:rows], counts[rows:]
    needf = need.astype(F32)
    sel = gt | (eq & (eq_before < needf))
    pos = gt_before + jnp.minimum(eq_before, needf)
    pos_ref[...] = jnp.where(sel, pos.astype(jnp.int32), -1)


SC_CORES = 2
SC_SUBCORES = 16
SC_TILES = SC_CORES * SC_SUBCORES
DISPATCH_SLOTS = CAP * E // SC_TILES
DISPATCH_ROWS = 64
COMBINE_RANGE = 1024
COMBINE_ROWS = 32
SLAB = 128
NSLAB = D // SLAB


def _sc_mesh():
    return plsc.VectorSubcoreMesh(core_axis_name="c", subcore_axis_name="s",
                                  num_cores=SC_CORES, num_subcores=SC_SUBCORES)


def _dispatch_body(pos_hbm, aff_hbm, h_hbm, xin_hbm, idx_hbm, gate_hbm, pos_v, aff_v, idx_v, gate_v,
                   rows_a, rows_b, gsem_a, gsem_b, wsem_a, wsem_b):
    w = lax.axis_index("s") * SC_CORES + lax.axis_index("c")
    e = w // 2
    lo = (w % 2) * DISPATCH_SLOTS
    pltpu.sync_copy(pos_hbm.at[e], pos_v)
    pltpu.sync_copy(aff_hbm.at[e], aff_v)

    @pl.loop(0, S // 16)
    def _(i):
        p = pos_v[pl.ds(i * 16, 16)] - lo
        m = (p >= 0) & (p < DISPATCH_SLOTS)
        tok = lax.iota(jnp.int32, 16) + i * 16
        plsc.store_scatter(idx_v, [p], tok, mask=m)
        plsc.store_scatter(gate_v, [p], aff_v[pl.ds(i * 16, 16)], mask=m)

    pltpu.sync_copy(idx_v, idx_hbm.at[e, pl.ds(lo, DISPATCH_SLOTS)])
    pltpu.sync_copy(gate_v, gate_hbm.at[e, pl.ds(lo, DISPATCH_SLOTS)])

    bufs, gsems, wsems = (rows_a, rows_b), (gsem_a, gsem_b), (wsem_a, wsem_b)
    nchunk = DISPATCH_SLOTS // DISPATCH_ROWS

    def gather(j):
        return pltpu.async_copy(h_hbm.at[idx_v.at[pl.ds(j * DISPATCH_ROWS, DISPATCH_ROWS)]], bufs[j % 2], gsems[j % 2])

    pending_gather = gather(0)
    writes = [None, None]
    for j in range(nchunk):
        pending_gather.wait()
        writes[j % 2] = pltpu.async_copy(
            bufs[j % 2], xin_hbm.at[e, pl.ds(lo + j * DISPATCH_ROWS, DISPATCH_ROWS)], wsems[j % 2])
        if j + 1 < nchunk:
            if writes[(j + 1) % 2] is not None:
                writes[(j + 1) % 2].wait()
            pending_gather = gather(j + 1)
    writes[(nchunk - 2) % 2].wait()
    writes[(nchunk - 1) % 2].wait()


def _dispatch(pos, aff, h):
    return pl.kernel(
        _dispatch_body, mesh=_sc_mesh(),
        out_type=[jax.ShapeDtypeStruct((E, CAP, D // 2), jnp.int32),
                  jax.ShapeDtypeStruct((E, CAP), jnp.int32),
                  jax.ShapeDtypeStruct((E, CAP), F32)],
        scratch_types=[pltpu.VMEM((S,), jnp.int32), pltpu.VMEM((S,), F32),
                       pltpu.VMEM((DISPATCH_SLOTS,), jnp.int32), pltpu.VMEM((DISPATCH_SLOTS,), F32),
                       pltpu.VMEM((DISPATCH_ROWS, D // 2), jnp.int32), pltpu.VMEM((DISPATCH_ROWS, D // 2), jnp.int32),
                       pltpu.SemaphoreType.DMA, pltpu.SemaphoreType.DMA,
                       pltpu.SemaphoreType.DMA, pltpu.SemaphoreType.DMA],
        compiler_params=pltpu.CompilerParams(needs_layout_passes=False),
        name="moe_dispatch",
    )(pos, aff, h)


def _combine_ranges(seed, y_hbm, idx_hbm, out_hbm, idx_v, li_v, bufs):
    rows = bufs[:NSLAB]
    accs = bufs[NSLAB:]
    c = lax.axis_index("c")
    s = lax.axis_index("s")
    share = COMBINE_RANGE // SC_SUBCORES
    pltpu.sync_copy(idx_hbm.at[s], idx_v)
    lane = lax.iota(jnp.int32, 16)

    @pl.loop(0, S // COMBINE_RANGE // SC_CORES)
    def _(r):
        t0 = (r * SC_CORES + c) * COMBINE_RANGE
        row0 = t0 + s * share
        seed(row0, tuple(accs[q].at[pl.ds(s * share, share)] for q in range(NSLAB)))
        plsc.subcore_barrier()

        @pl.loop(0, CAP // COMBINE_ROWS)
        def _(j):
            hits = jnp.zeros((16,), jnp.int32)
            for v in range(COMBINE_ROWS // 16):
                t = idx_v[pl.ds(j * COMBINE_ROWS + v * 16, 16)] - t0
                ok = (t >= 0) & (t < COMBINE_RANGE)
                li_v[pl.ds(v * 16, 16)] = jnp.where(ok, t, COMBINE_RANGE + lane)
                hits = hits + plsc.all_reduce_population_count(ok)

            @pl.when(jnp.max(hits) > 0)
            def _():
                pltpu.sync_copy(
                    tuple(y_hbm.at[s, pl.ds(j * COMBINE_ROWS, COMBINE_ROWS), pl.ds(q * SLAB, SLAB)]
                          for q in range(NSLAB)),
                    tuple(rows))
                pltpu.sync_copy(tuple(rows), tuple(accs[q].at[li_v] for q in range(NSLAB)), add=True)

        plsc.subcore_barrier()
        pltpu.sync_copy(tuple(accs[q].at[pl.ds(s * share, share)] for q in range(NSLAB)),
                        tuple(out_hbm.at[pl.ds(row0, share), pl.ds(q * SLAB, SLAB)] for q in range(NSLAB)))


def _combine_body(y_hbm, idx_hbm, x_hbm, out_hbm, idx_v, li_v, *bufs):
    share = COMBINE_RANGE // SC_SUBCORES

    def seed(row0, dst):
        pltpu.sync_copy(tuple(x_hbm.at[pl.ds(row0, share), pl.ds(q * SLAB, SLAB)] for q in range(NSLAB)), dst)

    _combine_ranges(seed, y_hbm, idx_hbm, out_hbm, idx_v, li_v, bufs)


def _moe_sum_body(y_hbm, idx_hbm, out_hbm, idx_v, li_v, zero_v, *bufs):
    @pl.loop(0, zero_v.shape[0])
    def _(r):
        for v in range(SLAB // 16):
            zero_v[r, pl.ds(v * 16, 16)] = jnp.zeros((16,), F32)

    _combine_ranges(lambda row0, dst: pltpu.sync_copy(tuple(zero_v for _ in range(NSLAB)), dst),
                    y_hbm, idx_hbm, out_hbm, idx_v, li_v, bufs)


def _combine_scratch():
    return ([pltpu.VMEM((CAP,), jnp.int32), pltpu.VMEM((COMBINE_ROWS,), jnp.int32)],
            [pltpu.VMEM((COMBINE_ROWS, SLAB), F32) for _ in range(NSLAB)]
            + [pltpu.VMEM_SHARED((COMBINE_RANGE + 16, SLAB), F32) for _ in range(NSLAB)])


def _combine(y, idx, x, out_rows=S):
    head, tail = _combine_scratch()
    return pl.kernel(
        _combine_body, mesh=_sc_mesh(),
        out_type=jax.ShapeDtypeStruct((out_rows, D), F32),
        scratch_types=head + tail,
        compiler_params=pltpu.CompilerParams(needs_layout_passes=False),
        name="moe_combine",
    )(y, idx, x)


def _moe_sum(y, idx):
    head, tail = _combine_scratch()
    return pl.kernel(
        _moe_sum_body, mesh=_sc_mesh(),
        out_type=jax.ShapeDtypeStruct((S, D), F32),
        scratch_types=head + [pltpu.VMEM((COMBINE_RANGE // SC_SUBCORES, SLAB), F32)] + tail,
        compiler_params=pltpu.CompilerParams(needs_layout_passes=False),
        name="moe_sum",
    )(y, idx)


FINAL_TM = 1024


def _final_add_kernel(big_ref, x_ref, m_ref, o_ref):
    del big_ref
    o_ref[...] = x_ref[...] + m_ref[...]


def _final_add(big, x, moe, b):
    nb = S // FINAL_TM
    return pl.pallas_call(
        _final_add_kernel,
        grid=(nb,),
        in_specs=[
            pl.BlockSpec(memory_space=pl.ANY),
            pl.BlockSpec((FINAL_TM, D), lambda i: (i, 0)),
            pl.BlockSpec((FINAL_TM, D), lambda i: (i, 0)),
        ],
        out_specs=pl.BlockSpec((FINAL_TM, D), lambda i: (b * nb + i, 0)),
        out_shape=jax.ShapeDtypeStruct((B * S, D), F32),
        input_output_aliases={0: 0},
        compiler_params=_cparams(("arbitrary",)),
        name="final_add",
    )(big, x, moe)


def _ffn_kernel(x_ref, wg_ref, wu_ref, wd_ref, gate_ref, mod_ref, o_ref):
    f = pl.program_id(1)
    last = pl.num_programs(1) - 1
    nt = FFN_TF // FFN_WT
    wg = [wg_ref[0, 0, :, j * FFN_WT:(j + 1) * FFN_WT].astype(BF16) for j in range(nt)]
    wu = [wu_ref[0, 0, :, j * FFN_WT:(j + 1) * FFN_WT].astype(BF16) for j in range(nt)]
    wd = wd_ref[0, 0].astype(BF16)
    is_first = f == 0
    is_last = f == last
    g2 = jnp.where(is_last, mod_ref[0, 0][5:6], 1.0)
    g_row = gate_ref[pl.ds(pl.program_id(0), 1), :]
    g_col = jnp.broadcast_to(g_row, (128, CAP)).T[:, 0:1]
    for r in range(CAP // FFN_RT):
        rows = slice(r * FFN_RT, (r + 1) * FFN_RT)
        xr = _unpack_bf16_pairs(x_ref[0, rows, :])
        acts = []
        for j in range(nt):
            g = jnp.dot(xr, wg[j], preferred_element_type=F32)
            u = jnp.dot(xr, wu[j], preferred_element_type=F32)
            acts.append((g * _sigmoid(g) * u).astype(BF16))
        y = jnp.dot(jnp.concatenate(acts, axis=1), wd, preferred_element_type=F32)
        prev = jnp.where(is_first, 0.0, o_ref[0, rows, :])
        gate = jnp.where(is_last, g_col[rows, :], 1.0)
        o_ref[0, rows, :] = (prev + y) * gate * g2


def _select(aff, tri):
    return pl.pallas_call(
        _select_kernel,
        grid=(1,),
        in_specs=[
            pl.BlockSpec((E, S), lambda i: (0, 0)),
            pl.BlockSpec((128, 128), lambda i: (0, 0)),
        ],
        out_specs=pl.BlockSpec((E, S), lambda i: (0, 0)),
        out_shape=jax.ShapeDtypeStruct((E, S), jnp.int32),
        compiler_params=_cparams(("arbitrary",)),
        name="moe_select",
    )(aff, tri)


def _ffn(xin, gate, b, mod_all, w_gate, w_up, w_down, layer):
    return pl.pallas_call(
        _ffn_kernel,
        grid=(E, F // FFN_TF),
        in_specs=[
            pl.BlockSpec((1, CAP, D // 2), lambda e, f: (e, 0, 0)),
            pl.BlockSpec((1, 1, D, FFN_TF), lambda e, f: (layer, e, 0, f)),
            pl.BlockSpec((1, 1, D, FFN_TF), lambda e, f: (layer, e, 0, f)),
            pl.BlockSpec((1, 1, FFN_TF, D), lambda e, f: (layer, e, f, 0)),
            pl.BlockSpec((E, CAP), lambda e, f: (0, 0)),
            pl.BlockSpec((1, 1, 6, D), lambda e, f: (layer, b, 0, 0)),
        ],
        out_specs=pl.BlockSpec((1, CAP, D), lambda e, f: (e, 0, 0)),
        out_shape=jax.ShapeDtypeStruct((E, CAP, D), F32),
        compiler_params=_cparams(("arbitrary", "arbitrary"), 48),
        name="moe_ffn",
    )(xin, w_gate, w_up, w_down, gate, mod_all)


def kernel(x, c, w_ada, b_ada, norm_mix, norm_ffn, w_fourier_out, w_qkv, w_attn_out, q_gain, k_gain,
           sink, rel_bias, w_router, w_gate, w_up, w_down):
    m0, m2 = _dft_tables()
    m1 = _stage1_table()
    bucket_t = _bucket_table()
    tri = jnp.asarray(np.triu(np.ones((128, 128), np.float32)), BF16)
    mod_all = _ada(c, w_ada, b_ada)
    bias_tab = _bias_table(rel_bias, bucket_t)
    gain_mix = norm_mix.reshape(DEPTH, 1, D)
    gain_ffn = norm_ffn.reshape(DEPTH, 1, D)
    xs = [x, x]
    for layer in range(DEPTH):
        j = layer // 2
        wr = jnp.pad(w_router[layer], ((0, 0), (0, EP - E)))
        wr1 = wr.astype(BF16)
        wr = jnp.concatenate([wr1, (wr - wr1.astype(F32)).astype(BF16)], axis=1)
        if layer % 2 == 0:
            routed = [_fourier_layer(xs[b], b, mod_all, gain_mix, gain_ffn, wr, w_fourier_out, layer, m0, m1, m2)
                      for b in range(B)]
        else:
            aw = _attn_weights(w_qkv[j], q_gain[j], k_gain[j], sink[j])
            routed = [_attn_layer(xs[b], b, mod_all, gain_mix, gain_ffn, wr, aw, w_attn_out, bias_tab, layer)
                      for b in range(B)]
        xs = [routed[b][0] for b in range(B)]
        pos = [_select(routed[b][2], tri) for b in range(B)]
        disp = [_dispatch(pos[b], routed[b][2], routed[b][1]) for b in range(B)]
        ys = [_ffn(disp[b][0], disp[b][2], b, mod_all, w_gate, w_up, w_down, layer) for b in range(B)]
        if layer < DEPTH - 1:
            xs = [_combine(ys[b], disp[b][1], xs[b]) for b in range(B)]
    out = _combine(ys[0], disp[0][1], xs[0], out_rows=B * S)
    for b in range(1, B):
        out = _final_add(out, xs[b], _moe_sum(ys[b], disp[b][1]), b)
    return out.reshape(B, S, D)
```

```python
import math

import numpy as np
import jax
import jax.numpy as jnp
from jax import lax
from jax.experimental import pallas as pl
from jax.experimental.pallas import tpu as pltpu
from jax.experimental.pallas import tpu_sc as plsc

D = 1024
B = 2
S = 8192
DEPTH = 4
GROUPS = 4
GD = D // GROUPS
HD = 64
NH = 16
NKV = 4
GQA = NH // NKV
WINDOW = 128
BLK = 128
NBUCKETS = 32
MAXDIST = 128
E = 16
CAP = 2 * S // E
F = 2 * D
EPS = 1e-6
NEG_INF = -1e30

N1 = 128
N2 = 64
INV_NORM = 1.0 / math.sqrt(S * GD)

F32 = jnp.float32
BF16 = jnp.bfloat16


def _cparams(sem, vmem_mb=48):
    return pltpu.CompilerParams(dimension_semantics=sem, vmem_limit_bytes=vmem_mb * 1024 * 1024)


def _dft_tables():
    c = np.arange(GD)
    ang0 = 2.0 * np.pi * ((c[:, None] * c[None, :]) % GD) / GD
    m0 = np.concatenate([np.cos(ang0), -np.sin(ang0)], axis=1)
    k2 = np.arange(N2)
    ang2 = 2.0 * np.pi * ((k2[:, None] * k2[None, :]) % N2) / N2
    m2 = np.stack([np.cos(ang2), np.sin(ang2)], axis=2).reshape(N2, 2 * N2)
    return jnp.asarray(m0, BF16), jnp.asarray(m2, BF16)


def _stage1_table():
    s2 = np.arange(N2)[:, None, None]
    k1 = np.arange(N1)[None, :, None]
    s1 = np.arange(N1)[None, None, :]
    th = ((k1 * (N2 * s1 + s2)) % S) * (2.0 * np.pi / S)
    co, si = np.cos(th), np.sin(th)
    top = np.concatenate([co, si], axis=2)
    bot = np.concatenate([-si, co], axis=2)
    return jnp.asarray(np.concatenate([top, bot], axis=1).astype(np.float32), BF16)


def _bucket_table():
    q_off = np.arange(BLK)
    k_off = np.arange(3 * BLK) - BLK
    rel = k_off[:, None] - q_off[None, :]
    half = NBUCKETS // 2
    max_exact = half // 2
    ret = np.where(rel > 0, half, 0)
    n = np.abs(rel)
    nf = np.maximum(n, 1).astype(np.float32)
    ratio = (np.log(nf / np.float32(max_exact)) / np.float32(math.log(MAXDIST / max_exact))).astype(np.float32)
    large = max_exact + (ratio * np.float32(half - max_exact)).astype(np.int32)
    large = np.minimum(large, half - 1)
    bucket = ret + np.where(n < max_exact, n, large)
    return jnp.asarray(np.where(np.abs(rel) <= WINDOW, bucket, -1).astype(np.int32))


def _modulate(x, gain, shift, scale):
    ms = jnp.mean(x * x, axis=-1, keepdims=True)
    return x * lax.rsqrt(ms + EPS) * (gain * (1.0 + scale)) + shift


def _sigmoid(x):
    return 1.0 / (1.0 + jnp.exp(-x))


ADA_TN = 2048


def _ada_kernel(ct_ref, w_ref, b_ref, o_ref):
    ct = ct_ref[...]
    ca = ct * _sigmoid(ct)
    w = w_ref[0]
    for b in range(B):
        o_ref[0, b:b + 1, :] = jnp.sum(w * ca[:, b:b + 1], axis=0, keepdims=True) + b_ref[0]


def _ada(c, w_ada, b_ada):
    out = pl.pallas_call(
        _ada_kernel,
        grid=(DEPTH, 6 * D // ADA_TN),
        in_specs=[
            pl.BlockSpec((D, B), lambda l, j: (0, 0)),
            pl.BlockSpec((1, D, ADA_TN), lambda l, j: (l, 0, j)),
            pl.BlockSpec((1, 1, ADA_TN), lambda l, j: (l, 0, j)),
        ],
        out_specs=pl.BlockSpec((1, B, ADA_TN), lambda l, j: (l, 0, j)),
        out_shape=jax.ShapeDtypeStruct((DEPTH, B, 6 * D), F32),
        compiler_params=_cparams(("arbitrary", "arbitrary"), 48),
        name="ada_mod",
    )(c.T, w_ada, b_ada.reshape(DEPTH, 1, 6 * D))
    return out.reshape(DEPTH, B, 6, D)


def _mod_spec(layer, b):
    return pl.BlockSpec((1, 1, 6, D), lambda *_: (layer, b, 0, 0))


def _gain_spec(layer):
    return pl.BlockSpec((1, 1, D), lambda *_: (layer, 0, 0))


def _x_spec(x, b, tm):
    if x.ndim == 3:
        return pl.BlockSpec((None, tm, D), lambda i: (b, i, 0))
    return pl.BlockSpec((tm, D), lambda i: (i, 0))


EP = 128

_ROUTED_SHAPES = [jax.ShapeDtypeStruct((S, D), F32),
                  jax.ShapeDtypeStruct((S, D // 2), jnp.int32),
                  jax.ShapeDtypeStruct((E, S), F32)]


def _routed_specs(tm):
    return [pl.BlockSpec((tm, D), lambda i: (i, 0)),
            pl.BlockSpec((tm, D // 2), lambda i: (i, 0)),
            pl.BlockSpec((E, tm), lambda i: (0, i))]


F0_TM = 1024
SUB = 8
SEQ_PICKS = 2 * SUB
SEQ2_PICKS = 4 * SUB


def _pack_bf16_pairs(h):
    hb = h.astype(BF16).astype(F32)
    lo = pltpu.bitcast(hb[:, :D // 2], jnp.int32)
    hi = pltpu.bitcast(hb[:, D // 2:], jnp.int32)
    return hi | lax.shift_right_logical(lo, jnp.int32(16))


def _unpack_bf16_pairs(xp):
    lo = pltpu.bitcast(xp << 16, F32).astype(BF16)
    hi = pltpu.bitcast(xp & jnp.int32(-65536), F32).astype(BF16)
    return jnp.concatenate([lo, hi], axis=1)


def _route(xn, m, gain_ffn, wr):
    h = _modulate(xn, gain_ffn, m[3:4], m[4:5])
    h1 = h.astype(BF16)
    h2 = (h - h1.astype(F32)).astype(BF16)
    part = jnp.dot(h1, wr, preferred_element_type=F32)
    logits = part[:, :EP] + part[:, EP:] + jnp.dot(h2, wr[:, :EP], preferred_element_type=F32)
    lt = logits.T[:E]
    ex = jnp.exp(lt - jnp.max(lt, axis=0, keepdims=True))
    return _pack_bf16_pairs(h), ex / jnp.sum(ex, axis=0, keepdims=True)


def _f0_kernel(x_ref, mod_ref, gain_ref, m0_ref, o_ref):
    m = mod_ref[0, 0]
    h = _modulate(x_ref[...], gain_ref[0], m[0:1], m[1:2]).astype(BF16)
    m0 = m0_ref[...]
    r = [jnp.dot(h[:, g * GD:(g + 1) * GD], m0, preferred_element_type=F32) for g in range(GROUPS)]
    o_ref[0] = _pack_bf16_pairs(jnp.concatenate([rg[:, :GD] for rg in r], axis=1))
    o_ref[1] = _pack_bf16_pairs(jnp.concatenate([rg[:, GD:] for rg in r], axis=1))


def _f1_kernel(w_ref, m1_ref, o_ref, scr_ref):
    n = o_ref.shape[0]
    for j in range(n):
        scr_ref[j] = w_ref[:, j, :]
    for j in range(n):
        w = _unpack_bf16_pairs(scr_ref[j])
        o_ref[j] = _pack_bf16_pairs(jnp.dot(m1_ref[j], w, preferred_element_type=F32))


def _f2_kernel(u_ref, m2_ref, o_ref, scr_ref):
    m2 = m2_ref[...]
    n = o_ref.shape[0]
    for k in range(n):
        scr_ref[k] = u_ref[:, k, :]
    for k in range(n):
        u = _unpack_bf16_pairs(scr_ref[k])
        o_ref[k] = _pack_bf16_pairs(jnp.dot(m2, u, preferred_element_type=F32))


def _f3_kernel(mp_ref, x_ref, w_ref, mod_ref, gf_ref, wr_ref, o_ref, h_ref, aff_ref, scr_ref):
    g1 = mod_ref[0, 0][2:3]
    for j in range(SUB):
        scr_ref[j * N1:(j + 1) * N1, :] = mp_ref[:, j, :]
    a = _unpack_bf16_pairs(scr_ref[...])
    y = jnp.dot(a, w_ref[...], preferred_element_type=F32)
    xn = x_ref[...] + (g1 * INV_NORM) * y
    o_ref[...] = xn
    h_ref[...], aff_ref[...] = _route(xn, mod_ref[0, 0], gf_ref[0], wr_ref[...])


def _fourier_layer(x, b, mod_all, gain, gain_ffn, wr, w_out_bf, layer, m0, m1, m2):
    wc = pl.pallas_call(
        _f0_kernel,
        grid=(S // F0_TM,),
        in_specs=[
            _x_spec(x, b, F0_TM),
            _mod_spec(layer, b), _gain_spec(layer),
            pl.BlockSpec((GD, 2 * GD), lambda i: (0, 0)),
        ],
        out_specs=pl.BlockSpec((2, F0_TM, D // 2), lambda i: (0, i, 0)),
        out_shape=jax.ShapeDtypeStruct((2, S, D // 2), jnp.int32),
        compiler_params=_cparams(("arbitrary",)),
        name="fourier_chan",
    )(x, mod_all, gain, m0)
    wc = wc.reshape(2 * N1, N2, D // 2)
    u = pl.pallas_call(
        _f1_kernel,
        grid=(N2 // SEQ_PICKS,),
        in_specs=[
            pl.BlockSpec((2 * N1, SEQ_PICKS, D // 2), lambda i: (0, i, 0)),
            pl.BlockSpec((SEQ_PICKS, 2 * N1, 2 * N1), lambda i: (i, 0, 0)),
        ],
        out_specs=pl.BlockSpec((SEQ_PICKS, 2 * N1, D // 2), lambda i: (i, 0, 0)),
        out_shape=jax.ShapeDtypeStruct((N2, 2 * N1, D // 2), jnp.int32),
        scratch_shapes=[pltpu.VMEM((SEQ_PICKS, 2 * N1, D // 2), jnp.int32)],
        compiler_params=_cparams(("arbitrary",)),
        name="fourier_seq1",
    )(wc, m1)
    u = u.reshape(2 * N2, N1, D // 2)
    mp = pl.pallas_call(
        _f2_kernel,
        grid=(N1 // SEQ2_PICKS,),
        in_specs=[
            pl.BlockSpec((2 * N2, SEQ2_PICKS, D // 2), lambda i: (0, i, 0)),
            pl.BlockSpec((N2, 2 * N2), lambda i: (0, 0)),
        ],
        out_specs=pl.BlockSpec((SEQ2_PICKS, N2, D // 2), lambda i: (i, 0, 0)),
        out_shape=jax.ShapeDtypeStruct((N1, N2, D // 2), jnp.int32),
        scratch_shapes=[pltpu.VMEM((SEQ2_PICKS, 2 * N2, D // 2), jnp.int32)],
        compiler_params=_cparams(("arbitrary",)),
        name="fourier_seq2",
    )(u, m2)
    return pl.pallas_call(
        _f3_kernel,
        grid=(N2 // SUB,),
        in_specs=[
            pl.BlockSpec((N1, SUB, D // 2), lambda i: (0, i, 0)),
            _x_spec(x, b, SUB * N1),
            pl.BlockSpec((D, D), lambda i: (0, 0)),
            _mod_spec(layer, b), _gain_spec(layer),
            pl.BlockSpec((D, 2 * EP), lambda i: (0, 0)),
        ],
        out_specs=_routed_specs(SUB * N1),
        out_shape=_ROUTED_SHAPES,
        scratch_shapes=[pltpu.VMEM((SUB * N1, D // 2), jnp.int32)],
        compiler_params=_cparams(("arbitrary",)),
        name="fourier_out",
    )(mp, x, w_out_bf, mod_all, gain_ffn, wr)


QKV_TM = 1024
TQ = 1024
NSB = TQ // BLK
KWIN = TQ + 2 * BLK
NBLK = S // BLK
KPAD = 128
VROWS = HD + 16
ATT_AHEAD = 1
LOG2E = math.log2(math.e)


def _bias_kernel(rb_ref, bucket_ref, o_ref):
    h = pl.program_id(0)
    bucket = bucket_ref[...]
    acc = jnp.full(bucket.shape, NEG_INF, F32)
    for k in range(NBUCKETS):
        acc = jnp.where(bucket == k, rb_ref[k, h] * LOG2E, acc)
    o_ref[0] = acc


def _bias_table(rel_bias, bucket_t):
    return pl.pallas_call(
        _bias_kernel,
        grid=(NH,),
        in_specs=[
            pl.BlockSpec(memory_space=pltpu.SMEM),
            pl.BlockSpec((3 * BLK, BLK), lambda h: (0, 0)),
        ],
        out_specs=pl.BlockSpec((1, 3 * BLK, BLK), lambda h: (h // GQA, 0, h % GQA)),
        out_shape=jax.ShapeDtypeStruct((NKV, 3 * BLK, GQA * BLK), F32),
        compiler_params=_cparams(("arbitrary",)),
        name="rel_bias_table",
    )(rel_bias, bucket_t)


def _qkv_kernel(x_ref, mod_ref, gain_ref, wqt_ref, wk_ref, wvt_ref, qg_ref, kg_ref, qt_ref, k_ref, vt_ref):
    m = mod_ref[0, 0]
    h = _modulate(x_ref[...], gain_ref[0], m[0:1], m[1:2]).astype(BF16)
    nt = (((1,), (1,)), ((), ()))
    qt = lax.dot_general(wqt_ref[...], h, nt, preferred_element_type=F32)
    tm = qt.shape[1]
    q3 = qt.reshape(NH, HD, tm)
    q3 = q3 * lax.rsqrt(jnp.mean(q3 * q3, axis=1, keepdims=True) + EPS)
    qt_ref[...] = (q3.reshape(NH * HD, tm) * qg_ref[...]).astype(BF16)
    k = jnp.dot(h, wk_ref[...], preferred_element_type=F32)
    for g in range(NKV):
        kg = k[:, g * KPAD:(g + 1) * KPAD]
        ms = jnp.sum(kg * kg, axis=-1, keepdims=True) * (1.0 / HD)
        k_ref[:, g * KPAD:(g + 1) * KPAD] = (kg * lax.rsqrt(ms + EPS) * kg_ref[...]).astype(BF16)
    vt_ref[...] = lax.dot_general(wvt_ref[...], h, nt, preferred_element_type=F32).astype(BF16)


def _attn_kernel(qt_ref, kp_ref, kc_ref, kn_ref, vp_ref, vc_ref, vn_ref, x_ref, w_ref, mod_ref,
                 bias_ref, sink_ref, gf_ref, wr_ref, o_ref, h_ref, aff_ref, att_ref, s_ref):
    i = pl.program_id(0)
    kwin = jnp.concatenate([kp_ref[...], kc_ref[...], kn_ref[...]], axis=0)
    vwin = jnp.concatenate([vp_ref[...], vc_ref[...], vn_ref[...]], axis=1)
    ones_rows = (lax.broadcasted_iota(jnp.int32, (VROWS - HD, KWIN), 0) == 0).astype(BF16)
    vaug = [jnp.concatenate([vwin[g * HD:(g + 1) * HD], ones_rows], axis=0) for g in range(NKV)]
    key_pos = i * TQ - BLK + lax.broadcasted_iota(jnp.int32, (KWIN, 1), 0)
    key_mask = jnp.where((key_pos >= 0) & (key_pos < S), 0.0, NEG_INF).astype(BF16)
    lane = lax.broadcasted_iota(jnp.int32, (1, NKV * KPAD), 1)
    kwin = jnp.where(lane % KPAD == HD, key_mask, kwin)
    q_ones = (lax.broadcasted_iota(jnp.int32, (KPAD - HD, GQA * BLK), 0) == 0).astype(BF16)
    items = [(g, r) for g in range(NKV) for r in range(NSB)]

    def scores(g, r):
        kr = kwin[r * BLK:r * BLK + 3 * BLK, g * KPAD:(g + 1) * KPAD]
        qg = jnp.concatenate(
            [qt_ref[(GQA * g + hh) * HD:(GQA * g + hh + 1) * HD, r * BLK:(r + 1) * BLK] for hh in range(GQA)],
            axis=1)
        qa = jnp.concatenate([qg, q_ones], axis=0)
        return jnp.dot(kr, qa, preferred_element_type=F32)

    def probs(s, g, r):
        sink = sink_ref[g] * LOG2E
        s = s + bias_ref[g]
        mx = jnp.maximum(jnp.max(s, axis=0, keepdims=True), sink)
        return jnp.exp2(s - mx).astype(BF16), jnp.exp2(sink - mx)

    nslot = ATT_AHEAD + 1
    for n in range(ATT_AHEAD):
        s_ref[n % nslot] = scores(*items[n])
    for n, (g, r) in enumerate(items):
        if n + ATT_AHEAD < len(items):
            s_ref[(n + ATT_AHEAD) % nslot] = scores(*items[n + ATT_AHEAD])
        p, psink = probs(s_ref[n % nslot], g, r)
        ot = jnp.dot(vaug[g][:, r * BLK:r * BLK + 3 * BLK], p, preferred_element_type=F32)
        ot = ot[:HD] * (1.0 / (ot[HD:HD + 1] + psink))
        for hh in range(GQA):
            hd0 = (GQA * g + hh) * HD
            att_ref[hd0:hd0 + HD, r * BLK:(r + 1) * BLK] = ot[:, hh * BLK:(hh + 1) * BLK]
    g1 = mod_ref[0, 0][2:3]
    att = att_ref[...].T.astype(BF16)
    y = jnp.dot(att, w_ref[...], preferred_element_type=F32)
    xn = x_ref[...] + g1 * y
    o_ref[...] = xn
    h_ref[...], aff_ref[...] = _route(xn, mod_ref[0, 0], gf_ref[0], wr_ref[...])


def _attn_weights(w_qkv, q_gain, k_gain, sink):
    wq, wk, wv = w_qkv[:, :NH * HD], w_qkv[:, NH * HD:(NH + NKV) * HD], w_qkv[:, (NH + NKV) * HD:]
    wqt = wq.T.astype(BF16)
    wvt = wv.T.astype(BF16)
    wk_pad = jnp.pad(wk.reshape(D, NKV, HD), ((0, 0), (0, 0), (0, KPAD - HD))).reshape(D, NKV * KPAD).astype(BF16)
    qg_col = jnp.tile(q_gain * (HD ** -0.5 * LOG2E), NH).reshape(NH * HD, 1)
    kg_row = jnp.pad(k_gain, (0, KPAD - HD)).reshape(1, KPAD)
    sink_row = jnp.repeat(sink, BLK).reshape(NKV, 1, GQA * BLK)
    return wqt, wk_pad, wvt, qg_col, kg_row, sink_row


def _attn_layer(x, b, mod_all, gain, gain_ffn, wr, aw, w_out_bf, bias_tab, layer):
    wqt, wk_pad, wvt, qg_col, kg_row, sink_row = aw
    qt, k, vt = pl.pallas_call(
        _qkv_kernel,
        grid=(S // QKV_TM,),
        in_specs=[
            _x_spec(x, b, QKV_TM),
            _mod_spec(layer, b), _gain_spec(layer),
            pl.BlockSpec((NH * HD, D), lambda i: (0, 0)),
            pl.BlockSpec((D, NKV * KPAD), lambda i: (0, 0)),
            pl.BlockSpec((NKV * HD, D), lambda i: (0, 0)),
            pl.BlockSpec((NH * HD, 1), lambda i: (0, 0)),
            pl.BlockSpec((1, KPAD), lambda i: (0, 0)),
        ],
        out_specs=[
            pl.BlockSpec((NH * HD, QKV_TM), lambda i: (0, i)),
            pl.BlockSpec((QKV_TM, NKV * KPAD), lambda i: (i, 0)),
            pl.BlockSpec((NKV * HD, QKV_TM), lambda i: (0, i)),
        ],
        out_shape=[
            jax.ShapeDtypeStruct((NH * HD, S), BF16),
            jax.ShapeDtypeStruct((S, NKV * KPAD), BF16),
            jax.ShapeDtypeStruct((NKV * HD, S), BF16),
        ],
        compiler_params=_cparams(("arbitrary",)),
        name="attn_qkv",
    )(x, mod_all, gain, wqt, wk_pad, wvt, qg_col, kg_row)

    kw = NKV * KPAD
    vw = NKV * HD
    kprev = pl.BlockSpec((BLK, kw), lambda i: (jnp.maximum(i * NSB - 1, 0), 0))
    kcur = pl.BlockSpec((TQ, kw), lambda i: (i, 0))
    knext = pl.BlockSpec((BLK, kw), lambda i: (jnp.minimum((i + 1) * NSB, NBLK - 1), 0))
    vprev = pl.BlockSpec((vw, BLK), lambda i: (0, jnp.maximum(i * NSB - 1, 0)))
    vcur = pl.BlockSpec((vw, TQ), lambda i: (0, i))
    vnext = pl.BlockSpec((vw, BLK), lambda i: (0, jnp.minimum((i + 1) * NSB, NBLK - 1)))
    return pl.pallas_call(
        _attn_kernel,
        grid=(S // TQ,),
        in_specs=[
            pl.BlockSpec((NH * HD, TQ), lambda i: (0, i)),
            kprev, kcur, knext, vprev, vcur, vnext,
            _x_spec(x, b, TQ),
            pl.BlockSpec((NH * HD, D), lambda i: (0, 0)),
            _mod_spec(layer, b),
            pl.BlockSpec((NKV, 3 * BLK, GQA * BLK), lambda i: (0, 0, 0)),
            pl.BlockSpec((NKV, 1, GQA * BLK), lambda i: (0, 0, 0)),
            _gain_spec(layer),
            pl.BlockSpec((D, 2 * EP), lambda i: (0, 0)),
        ],
        out_specs=_routed_specs(TQ),
        out_shape=_ROUTED_SHAPES,
        scratch_shapes=[pltpu.VMEM((NH * HD, TQ), F32), pltpu.VMEM((ATT_AHEAD + 1, 3 * BLK, GQA * BLK), F32)],
        compiler_params=_cparams(("arbitrary",)),
        name="attn_core",
    )(qt, k, k, k, vt, vt, vt, x, w_out_bf, mod_all, bias_tab, sink_row, gain_ffn, wr)


FFN_TF = 1024
FFN_RT = 512
FFN_WT = 256


def _select_kernel(aff_ref, tri_ref, pos_ref):
    aff = aff_ref[...]
    bits = pltpu.bitcast(aff, jnp.int32)
    rows = aff.shape[0]

    def count_ge(v):
        return jnp.sum((bits >= v).astype(jnp.int32), axis=1, keepdims=True)

    def body(t, cur):
        sh = 28 - 2 * t
        c1, c2, c3 = cur | (jnp.int32(1) << sh), cur | (jnp.int32(2) << sh), cur | (jnp.int32(3) << sh)
        n1, n2, n3 = count_ge(c1), count_ge(c2), count_ge(c3)
        return jnp.where(n3 >= CAP, c3, jnp.where(n2 >= CAP, c2, jnp.where(n1 >= CAP, c1, cur)))

    top = jnp.full((rows, 1), 1 << 30, jnp.int32)
    thr = lax.fori_loop(0, 15, body, jnp.where(count_ge(top) >= CAP, top, 0))
    gt = bits > thr
    eq = bits == thr
    need = CAP - jnp.sum(gt.astype(jnp.int32), axis=1, keepdims=True)
    tri = tri_ref[...]

    def cumsum_excl(mask_f):
        off = jnp.zeros((mask_f.shape[0], 1), F32)
        outs = []
        for j in range(S // 128):
            mj = mask_f[:, j * 128:(j + 1) * 128]
            loc = jnp.dot(mj.astype(BF16), tri, preferred_element_type=F32)
            outs.append(loc - mj + off)
            off = off + loc[:, 127:128]
        return jnp.concatenate(outs, axis=1)

    counts = cumsum_excl(jnp.concatenate([gt, eq], axis=0).astype(F32))
    gt_before, eq_before = counts[:rows], counts[rows:]
    needf = need.astype(F32)
    sel = gt | (eq & (eq_before < needf))
    pos = gt_before + jnp.minimum(eq_before, needf)
    pos_ref[...] = jnp.where(sel, pos.astype(jnp.int32), -1)


SC_CORES = 2
SC_SUBCORES = 16
SC_TILES = SC_CORES * SC_SUBCORES
DISPATCH_SLOTS = CAP * E // SC_TILES
DISPATCH_ROWS = 64
COMBINE_RANGE = 1024
COMBINE_ROWS = 32
SLAB = 128
NSLAB = D // SLAB


def _sc_mesh():
    return plsc.VectorSubcoreMesh(core_axis_name="c", subcore_axis_name="s",
                                  num_cores=SC_CORES, num_subcores=SC_SUBCORES)


def _dispatch_body(pos_hbm, aff_hbm, h_hbm, xin_hbm, idx_hbm, gate_hbm, pos_v, aff_v, idx_v, gate_v,
                   rows_a, rows_b, gsem_a, gsem_b, wsem_a, wsem_b):
    w = lax.axis_index("s") * SC_CORES + lax.axis_index("c")
    e = w // 2
    lo = (w % 2) * DISPATCH_SLOTS
    pltpu.sync_copy(pos_hbm.at[e], pos_v)
    pltpu.sync_copy(aff_hbm.at[e], aff_v)

    @pl.loop(0, S // 16)
    def _(i):
        p = pos_v[pl.ds(i * 16, 16)] - lo
        m = (p >= 0) & (p < DISPATCH_SLOTS)
        tok = lax.iota(jnp.int32, 16) + i * 16
        plsc.store_scatter(idx_v, [p], tok, mask=m)
        plsc.store_scatter(gate_v, [p], aff_v[pl.ds(i * 16, 16)], mask=m)

    pltpu.sync_copy(idx_v, idx_hbm.at[e, pl.ds(lo, DISPATCH_SLOTS)])
    pltpu.sync_copy(gate_v, gate_hbm.at[e, pl.ds(lo, DISPATCH_SLOTS)])

    bufs, gsems, wsems = (rows_a, rows_b), (gsem_a, gsem_b), (wsem_a, wsem_b)
    nchunk = DISPATCH_SLOTS // DISPATCH_ROWS

    def gather(j):
        return pltpu.async_copy(h_hbm.at[idx_v.at[pl.ds(j * DISPATCH_ROWS, DISPATCH_ROWS)]], bufs[j % 2], gsems[j % 2])

    pending_gather = gather(0)
    writes = [None, None]
    for j in range(nchunk):
        pending_gather.wait()
        writes[j % 2] = pltpu.async_copy(
            bufs[j % 2], xin_hbm.at[e, pl.ds(lo + j * DISPATCH_ROWS, DISPATCH_ROWS)], wsems[j % 2])
        if j + 1 < nchunk:
            if writes[(j + 1) % 2] is not None:
                writes[(j + 1) % 2].wait()
            pending_gather = gather(j + 1)
    writes[(nchunk - 2) % 2].wait()
    writes[(nchunk - 1) % 2].wait()


def _dispatch(pos, aff, h):
    return pl.kernel(
        _dispatch_body, mesh=_sc_mesh(),
        out_type=[jax.ShapeDtypeStruct((E, CAP, D // 2), jnp.int32),
                  jax.ShapeDtypeStruct((E, CAP), jnp.int32),
                  jax.ShapeDtypeStruct((E, CAP), F32)],
        scratch_types=[pltpu.VMEM((S,), jnp.int32), pltpu.VMEM((S,), F32),
                       pltpu.VMEM((DISPATCH_SLOTS,), jnp.int32), pltpu.VMEM((DISPATCH_SLOTS,), F32),
                       pltpu.VMEM((DISPATCH_ROWS, D // 2), jnp.int32), pltpu.VMEM((DISPATCH_ROWS, D // 2), jnp.int32),
                       pltpu.SemaphoreType.DMA, pltpu.SemaphoreType.DMA,
                       pltpu.SemaphoreType.DMA, pltpu.SemaphoreType.DMA],
        compiler_params=pltpu.CompilerParams(needs_layout_passes=False),
        name="moe_dispatch",
    )(pos, aff, h)


def _combine_ranges(seed, y_hbm, idx_hbm, out_hbm, idx_v, li_v, bufs):
    rows = bufs[:NSLAB]
    accs = bufs[NSLAB:]
    c = lax.axis_index("c")
    s = lax.axis_index("s")
    share = COMBINE_RANGE // SC_SUBCORES
    pltpu.sync_copy(idx_hbm.at[s], idx_v)
    lane = lax.iota(jnp.int32, 16)

    @pl.loop(0, S // COMBINE_RANGE // SC_CORES)
    def _(r):
        t0 = (r * SC_CORES + c) * COMBINE_RANGE
        row0 = t0 + s * share
        seed(row0, tuple(accs[q].at[pl.ds(s * share, share)] for q in range(NSLAB)))
        plsc.subcore_barrier()

        @pl.loop(0, CAP // COMBINE_ROWS)
        def _(j):
            hits = jnp.zeros((16,), jnp.int32)
            for v in range(COMBINE_ROWS // 16):
                t = idx_v[pl.ds(j * COMBINE_ROWS + v * 16, 16)] - t0
                ok = (t >= 0) & (t < COMBINE_RANGE)
                li_v[pl.ds(v * 16, 16)] = jnp.where(ok, t, COMBINE_RANGE + lane)
                hits = hits + plsc.all_reduce_population_count(ok)

            @pl.when(jnp.max(hits) > 0)
            def _():
                pltpu.sync_copy(
                    tuple(y_hbm.at[s, pl.ds(j * COMBINE_ROWS, COMBINE_ROWS), pl.ds(q * SLAB, SLAB)]
                          for q in range(NSLAB)),
                    tuple(rows))
                pltpu.sync_copy(tuple(rows), tuple(accs[q].at[li_v] for q in range(NSLAB)), add=True)

        plsc.subcore_barrier()
        pltpu.sync_copy(tuple(accs[q].at[pl.ds(s * share, share)] for q in range(NSLAB)),
                        tuple(out_hbm.at[pl.ds(row0, share), pl.ds(q * SLAB, SLAB)] for q in range(NSLAB)))


def _combine_body(y_hbm, idx_hbm, x_hbm, out_hbm, idx_v, li_v, *bufs):
    share = COMBINE_RANGE // SC_SUBCORES

    def seed(row0, dst):
        pltpu.sync_copy(tuple(x_hbm.at[pl.ds(row0, share), pl.ds(q * SLAB, SLAB)] for q in range(NSLAB)), dst)

    _combine_ranges(seed, y_hbm, idx_hbm, out_hbm, idx_v, li_v, bufs)


def _moe_sum_body(y_hbm, idx_hbm, out_hbm, idx_v, li_v, zero_v, *bufs):
    @pl.loop(0, zero_v.shape[0])
    def _(r):
        for v in range(SLAB // 16):
            zero_v[r, pl.ds(v * 16, 16)] = jnp.zeros((16,), F32)

    _combine_ranges(lambda row0, dst: pltpu.sync_copy(tuple(zero_v for _ in range(NSLAB)), dst),
                    y_hbm, idx_hbm, out_hbm, idx_v, li_v, bufs)


def _combine_scratch():
    return ([pltpu.VMEM((CAP,), jnp.int32), pltpu.VMEM((COMBINE_ROWS,), jnp.int32)],
            [pltpu.VMEM((COMBINE_ROWS, SLAB), F32) for _ in range(NSLAB)]
            + [pltpu.VMEM_SHARED((COMBINE_RANGE + 16, SLAB), F32) for _ in range(NSLAB)])


def _combine(y, idx, x, out_rows=S):
    head, tail = _combine_scratch()
    return pl.kernel(
        _combine_body, mesh=_sc_mesh(),
        out_type=jax.ShapeDtypeStruct((out_rows, D), F32),
        scratch_types=head + tail,
        compiler_params=pltpu.CompilerParams(needs_layout_passes=False),
        name="moe_combine",
    )(y, idx, x)


def _moe_sum(y, idx):
    head, tail = _combine_scratch()
    return pl.kernel(
        _moe_sum_body, mesh=_sc_mesh(),
        out_type=jax.ShapeDtypeStruct((S, D), F32),
        scratch_types=head + [pltpu.VMEM((COMBINE_RANGE // SC_SUBCORES, SLAB), F32)] + tail,
        compiler_params=pltpu.CompilerParams(needs_layout_passes=False),
        name="moe_sum",
    )(y, idx)


FINAL_TM = 1024


def _final_add_kernel(big_ref, x_ref, m_ref, o_ref):
    del big_ref
    o_ref[...] = x_ref[...] + m_ref[...]


def _final_add(big, x, moe, b):
    nb = S // FINAL_TM
    return pl.pallas_call(
        _final_add_kernel,
        grid=(nb,),
        in_specs=[
            pl.BlockSpec(memory_space=pl.ANY),
            pl.BlockSpec((FINAL_TM, D), lambda i: (i, 0)),
            pl.BlockSpec((FINAL_TM, D), lambda i: (i, 0)),
        ],
        out_specs=pl.BlockSpec((FINAL_TM, D), lambda i: (b * nb + i, 0)),
        out_shape=jax.ShapeDtypeStruct((B * S, D), F32),
        input_output_aliases={0: 0},
        compiler_params=_cparams(("arbitrary",)),
        name="final_add",
    )(big, x, moe)


def _ffn_kernel(x_ref, wg_ref, wu_ref, wd_ref, gate_ref, mod_ref, o_ref):
    f = pl.program_id(1)
    last = pl.num_programs(1) - 1
    nt = FFN_TF // FFN_WT
    wg = [wg_ref[0, 0, :, j * FFN_WT:(j + 1) * FFN_WT].astype(BF16) for j in range(nt)]
    wu = [wu_ref[0, 0, :, j * FFN_WT:(j + 1) * FFN_WT].astype(BF16) for j in range(nt)]
    wd = wd_ref[0, 0].astype(BF16)
    is_first = f == 0
    is_last = f == last
    g2 = jnp.where(is_last, mod_ref[0, 0][5:6], 1.0)
    g_row = gate_ref[pl.ds(pl.program_id(0), 1), :]
    g_col = jnp.broadcast_to(g_row, (128, CAP)).T[:, 0:1]
    for r in range(CAP // FFN_RT):
        rows = slice(r * FFN_RT, (r + 1) * FFN_RT)
        xr = _unpack_bf16_pairs(x_ref[0, rows, :])
        acts = []
        for j in range(nt):
            g = jnp.dot(xr, wg[j], preferred_element_type=F32)
            u = jnp.dot(xr, wu[j], preferred_element_type=F32)
            acts.append((g * _sigmoid(g) * u).astype(BF16))
        y = jnp.dot(jnp.concatenate(acts, axis=1), wd, preferred_element_type=F32)
        prev = jnp.where(is_first, 0.0, o_ref[0, rows, :])
        gate = jnp.where(is_last, g_col[rows, :], 1.0)
        o_ref[0, rows, :] = (prev + y) * gate * g2


def _select(aff, tri):
    return pl.pallas_call(
        _select_kernel,
        grid=(1,),
        in_specs=[
            pl.BlockSpec((E, S), lambda i: (0, 0)),
            pl.BlockSpec((128, 128), lambda i: (0, 0)),
        ],
        out_specs=pl.BlockSpec((E, S), lambda i: (0, 0)),
        out_shape=jax.ShapeDtypeStruct((E, S), jnp.int32),
        compiler_params=_cparams(("arbitrary",)),
        name="moe_select",
    )(aff, tri)


def _ffn(xin, gate, b, mod_all, w_gate, w_up, w_down, layer):
    return pl.pallas_call(
        _ffn_kernel,
        grid=(E, F // FFN_TF),
        in_specs=[
            pl.BlockSpec((1, CAP, D // 2), lambda e, f: (e, 0, 0)),
            pl.BlockSpec((1, 1, D, FFN_TF), lambda e, f: (layer, e, 0, f)),
            pl.BlockSpec((1, 1, D, FFN_TF), lambda e, f: (layer, e, 0, f)),
            pl.BlockSpec((1, 1, FFN_TF, D), lambda e, f: (layer, e, f, 0)),
            pl.BlockSpec((E, CAP), lambda e, f: (0, 0)),
            pl.BlockSpec((1, 1, 6, D), lambda e, f: (layer, b, 0, 0)),
        ],
        out_specs=pl.BlockSpec((1, CAP, D), lambda e, f: (e, 0, 0)),
        out_shape=jax.ShapeDtypeStruct((E, CAP, D), F32),
        compiler_params=_cparams(("arbitrary", "arbitrary"), 48),
        name="moe_ffn",
    )(xin, w_gate, w_up, w_down, gate, mod_all)


def kernel(x, c, w_ada, b_ada, norm_mix, norm_ffn, w_fourier_out, w_qkv, w_attn_out, q_gain, k_gain,
           sink, rel_bias, w_router, w_gate, w_up, w_down):
    m0, m2 = _dft_tables()
    m1 = _stage1_table()
    bucket_t = _bucket_table()
    tri = jnp.asarray(np.triu(np.ones((128, 128), np.float32)), BF16)
    mod_all = _ada(c, w_ada, b_ada)
    bias_tab = _bias_table(rel_bias, bucket_t)
    gain_mix = norm_mix.reshape(DEPTH, 1, D)
    gain_ffn = norm_ffn.reshape(DEPTH, 1, D)
    xs = [x, x]
    for layer in range(DEPTH):
        j = layer // 2
        wr = jnp.pad(w_router[layer], ((0, 0), (0, EP - E)))
        wr1 = wr.astype(BF16)
        wr = jnp.concatenate([wr1, (wr - wr1.astype(F32)).astype(BF16)], axis=1)
        if layer % 2 == 0:
            w_out_bf = w_fourier_out[j].astype(BF16)
            routed = [_fourier_layer(xs[b], b, mod_all, gain_mix, gain_ffn, wr, w_out_bf, layer, m0, m1, m2)
                      for b in range(B)]
        else:
            aw = _attn_weights(w_qkv[j], q_gain[j], k_gain[j], sink[j])
            w_out_bf = w_attn_out[j].astype(BF16)
            routed = [_attn_layer(xs[b], b, mod_all, gain_mix, gain_ffn, wr, aw, w_out_bf, bias_tab, layer)
                      for b in range(B)]
        xs = [routed[b][0] for b in range(B)]
        pos = [_select(routed[b][2], tri) for b in range(B)]
        disp = [_dispatch(pos[b], routed[b][2], routed[b][1]) for b in range(B)]
        ys = [_ffn(disp[b][0], disp[b][2], b, mod_all, w_gate, w_up, w_down, layer) for b in range(B)]
        if layer < DEPTH - 1:
            xs = [_combine(ys[b], disp[b][1], xs[b]) for b in range(B)]
    out = _combine(ys[0], disp[0][1], xs[0], out_rows=B * S)
    for b in range(1, B):
        out = _final_add(out, xs[b], _moe_sum(ys[b], disp[b][1]), b)
    return out.reshape(B, S, D)
```

```python
import functools
import math

import numpy as np
import jax
import jax.numpy as jnp
from jax import lax
from jax.experimental import pallas as pl
from jax.experimental.pallas import tpu as pltpu
from jax.experimental.pallas import tpu_sc as plsc

D = 1024
B = 2
S = 8192
DEPTH = 4
GROUPS = 4
GD = D // GROUPS
HD = 64
NH = 16
NKV = 4
GQA = NH // NKV
WINDOW = 128
BLK = 128
NBUCKETS = 32
MAXDIST = 128
E = 16
CAP = 2 * S // E
F = 2 * D
EPS = 1e-6
NEG_INF = -1e30

N1 = 128
N2 = 64
INV_NORM = 1.0 / math.sqrt(S * GD)

F32 = jnp.float32
BF16 = jnp.bfloat16


def _cparams(sem, vmem_mb=48):
    return pltpu.CompilerParams(dimension_semantics=sem, vmem_limit_bytes=vmem_mb * 1024 * 1024)


def _dft_tables():
    c = np.arange(GD)
    ang0 = 2.0 * np.pi * ((c[:, None] * c[None, :]) % GD) / GD
    m0 = np.concatenate([np.cos(ang0), -np.sin(ang0)], axis=1)
    k2 = np.arange(N2)
    ang2 = 2.0 * np.pi * ((k2[:, None] * k2[None, :]) % N2) / N2
    m2 = np.stack([np.cos(ang2), np.sin(ang2)], axis=2).reshape(N2, 2 * N2)
    return jnp.asarray(m0, BF16), jnp.asarray(m2, BF16)


def _stage1_table():
    s2 = np.arange(N2)[:, None, None]
    k1 = np.arange(N1)[None, :, None]
    s1 = np.arange(N1)[None, None, :]
    th = ((k1 * (N2 * s1 + s2)) % S) * (2.0 * np.pi / S)
    co, si = np.cos(th), np.sin(th)
    top = np.concatenate([co, si], axis=2)
    bot = np.concatenate([-si, co], axis=2)
    return jnp.asarray(np.concatenate([top, bot], axis=1).astype(np.float32), BF16)


def _bucket_table():
    q_off = np.arange(BLK)
    k_off = np.arange(3 * BLK) - BLK
    rel = k_off[:, None] - q_off[None, :]
    half = NBUCKETS // 2
    max_exact = half // 2
    ret = np.where(rel > 0, half, 0)
    n = np.abs(rel)
    nf = np.maximum(n, 1).astype(np.float32)
    ratio = (np.log(nf / np.float32(max_exact)) / np.float32(math.log(MAXDIST / max_exact))).astype(np.float32)
    large = max_exact + (ratio * np.float32(half - max_exact)).astype(np.int32)
    large = np.minimum(large, half - 1)
    bucket = ret + np.where(n < max_exact, n, large)
    return jnp.asarray(np.where(np.abs(rel) <= WINDOW, bucket, -1).astype(np.int32))


def _modulate(x, gain, shift, scale):
    ms = jnp.mean(x * x, axis=-1, keepdims=True)
    return x * lax.rsqrt(ms + EPS) * (gain * (1.0 + scale)) + shift


def _sigmoid(x):
    return 1.0 / (1.0 + jnp.exp(-x))


ADA_TN = 1536


def _ada_kernel(ct_ref, w_ref, b_ref, o_ref):
    ct = ct_ref[...]
    ca = ct * _sigmoid(ct)
    w = w_ref[0]
    for b in range(B):
        o_ref[0, b:b + 1, :] = jnp.sum(w * ca[:, b:b + 1], axis=0, keepdims=True) + b_ref[0]


def _ada(c, w_ada, b_ada):
    out = pl.pallas_call(
        _ada_kernel,
        grid=(DEPTH, 6 * D // ADA_TN),
        in_specs=[
            pl.BlockSpec((D, B), lambda l, j: (0, 0)),
            pl.BlockSpec((1, D, ADA_TN), lambda l, j: (l, 0, j)),
            pl.BlockSpec((1, 1, ADA_TN), lambda l, j: (l, 0, j)),
        ],
        out_specs=pl.BlockSpec((1, B, ADA_TN), lambda l, j: (l, 0, j)),
        out_shape=jax.ShapeDtypeStruct((DEPTH, B, 6 * D), F32),
        compiler_params=_cparams(("arbitrary", "arbitrary"), 32),
        name="ada_mod",
    )(c.T, w_ada, b_ada.reshape(DEPTH, 1, 6 * D))
    return out.reshape(DEPTH, B, 6, D)


def _mod_spec(layer, b):
    return pl.BlockSpec((1, 1, 6, D), lambda *_: (layer, b, 0, 0))


def _gain_spec(layer):
    return pl.BlockSpec((1, 1, D), lambda *_: (layer, 0, 0))


def _x_spec(x, b, tm):
    if x.ndim == 3:
        return pl.BlockSpec((None, tm, D), lambda i: (b, i, 0))
    return pl.BlockSpec((tm, D), lambda i: (i, 0))


EP = 128

_ROUTED_SHAPES = [jax.ShapeDtypeStruct((S, D), F32),
                  jax.ShapeDtypeStruct((S, D // 2), jnp.int32),
                  jax.ShapeDtypeStruct((E, S), F32)]


def _routed_specs(tm):
    return [pl.BlockSpec((tm, D), lambda i: (i, 0)),
            pl.BlockSpec((tm, D // 2), lambda i: (i, 0)),
            pl.BlockSpec((E, tm), lambda i: (0, i))]


F0_TM = 1024
SUB = 8
SEQ_PICKS = 2 * SUB


def _pack_bf16_pairs(h):
    hb = h.astype(BF16).astype(F32)
    lo = pltpu.bitcast(hb[:, :D // 2], jnp.int32)
    hi = pltpu.bitcast(hb[:, D // 2:], jnp.int32)
    return hi | lax.shift_right_logical(lo, jnp.int32(16))


def _unpack_bf16_pairs(xp):
    lo = pltpu.bitcast(xp << 16, F32).astype(BF16)
    hi = pltpu.bitcast(xp & jnp.int32(-65536), F32).astype(BF16)
    return jnp.concatenate([lo, hi], axis=1)


def _route(xn, m, gain_ffn, wr):
    h = _modulate(xn, gain_ffn, m[3:4], m[4:5])
    h1 = h.astype(BF16)
    h2 = (h - h1.astype(F32)).astype(BF16)
    part = jnp.dot(h1, wr, preferred_element_type=F32)
    logits = part[:, :EP] + part[:, EP:] + jnp.dot(h2, wr[:, :EP], preferred_element_type=F32)
    lt = logits.T[:E]
    ex = jnp.exp(lt - jnp.max(lt, axis=0, keepdims=True))
    return _pack_bf16_pairs(h), ex / jnp.sum(ex, axis=0, keepdims=True)


def _f0_kernel(x_ref, mod_ref, gain_ref, m0_ref, o_ref):
    m = mod_ref[0, 0]
    h = _modulate(x_ref[...], gain_ref[0], m[0:1], m[1:2]).astype(BF16)
    m0 = m0_ref[...]
    r = [jnp.dot(h[:, g * GD:(g + 1) * GD], m0, preferred_element_type=F32) for g in range(GROUPS)]
    o_ref[0] = _pack_bf16_pairs(jnp.concatenate([rg[:, :GD] for rg in r], axis=1))
    o_ref[1] = _pack_bf16_pairs(jnp.concatenate([rg[:, GD:] for rg in r], axis=1))


def _f1_kernel(w_ref, m1_ref, o_ref, scr_ref):
    n = o_ref.shape[0]
    for j in range(n):
        scr_ref[j] = w_ref[:, j, :]
    for j in range(n):
        w = _unpack_bf16_pairs(scr_ref[j])
        o_ref[j] = _pack_bf16_pairs(jnp.dot(m1_ref[j], w, preferred_element_type=F32))


def _f2_kernel(u_ref, m2_ref, o_ref, scr_ref):
    m2 = m2_ref[...]
    n = o_ref.shape[0]
    for k in range(n):
        scr_ref[k] = u_ref[:, k, :]
    for k in range(n):
        u = _unpack_bf16_pairs(scr_ref[k])
        o_ref[k] = _pack_bf16_pairs(jnp.dot(m2, u, preferred_element_type=F32))


def _f3_kernel(mp_ref, x_ref, w_ref, mod_ref, gf_ref, wr_ref, o_ref, h_ref, aff_ref, scr_ref):
    g1 = mod_ref[0, 0][2:3]
    for j in range(SUB):
        scr_ref[j * N1:(j + 1) * N1, :] = mp_ref[:, j, :]
    a = _unpack_bf16_pairs(scr_ref[...])
    y = jnp.dot(a, w_ref[...], preferred_element_type=F32)
    xn = x_ref[...] + (g1 * INV_NORM) * y
    o_ref[...] = xn
    h_ref[...], aff_ref[...] = _route(xn, mod_ref[0, 0], gf_ref[0], wr_ref[...])


def _fourier_layer(x, b, mod_all, gain, gain_ffn, wr, w_out_bf, layer, m0, m1, m2):
    wc = pl.pallas_call(
        _f0_kernel,
        grid=(S // F0_TM,),
        in_specs=[
            _x_spec(x, b, F0_TM),
            _mod_spec(layer, b), _gain_spec(layer),
            pl.BlockSpec((GD, 2 * GD), lambda i: (0, 0)),
        ],
        out_specs=pl.BlockSpec((2, F0_TM, D // 2), lambda i: (0, i, 0)),
        out_shape=jax.ShapeDtypeStruct((2, S, D // 2), jnp.int32),
        compiler_params=_cparams(("arbitrary",)),
        name="fourier_chan",
    )(x, mod_all, gain, m0)
    wc = wc.reshape(2 * N1, N2, D // 2)
    u = pl.pallas_call(
        _f1_kernel,
        grid=(N2 // SEQ_PICKS,),
        in_specs=[
            pl.BlockSpec((2 * N1, SEQ_PICKS, D // 2), lambda i: (0, i, 0)),
            pl.BlockSpec((SEQ_PICKS, 2 * N1, 2 * N1), lambda i: (i, 0, 0)),
        ],
        out_specs=pl.BlockSpec((SEQ_PICKS, 2 * N1, D // 2), lambda i: (i, 0, 0)),
        out_shape=jax.ShapeDtypeStruct((N2, 2 * N1, D // 2), jnp.int32),
        scratch_shapes=[pltpu.VMEM((SEQ_PICKS, 2 * N1, D // 2), jnp.int32)],
        compiler_params=_cparams(("arbitrary",)),
        name="fourier_seq1",
    )(wc, m1)
    u = u.reshape(2 * N2, N1, D // 2)
    mp = pl.pallas_call(
        _f2_kernel,
        grid=(N1 // SEQ_PICKS,),
        in_specs=[
            pl.BlockSpec((2 * N2, SEQ_PICKS, D // 2), lambda i: (0, i, 0)),
            pl.BlockSpec((N2, 2 * N2), lambda i: (0, 0)),
        ],
        out_specs=pl.BlockSpec((SEQ_PICKS, N2, D // 2), lambda i: (i, 0, 0)),
        out_shape=jax.ShapeDtypeStruct((N1, N2, D // 2), jnp.int32),
        scratch_shapes=[pltpu.VMEM((SEQ_PICKS, 2 * N2, D // 2), jnp.int32)],
        compiler_params=_cparams(("arbitrary",)),
        name="fourier_seq2",
    )(u, m2)
    return pl.pallas_call(
        _f3_kernel,
        grid=(N2 // SUB,),
        in_specs=[
            pl.BlockSpec((N1, SUB, D // 2), lambda i: (0, i, 0)),
            _x_spec(x, b, SUB * N1),
            pl.BlockSpec((D, D), lambda i: (0, 0)),
            _mod_spec(layer, b), _gain_spec(layer),
            pl.BlockSpec((D, 2 * EP), lambda i: (0, 0)),
        ],
        out_specs=_routed_specs(SUB * N1),
        out_shape=_ROUTED_SHAPES,
        scratch_shapes=[pltpu.VMEM((SUB * N1, D // 2), jnp.int32)],
        compiler_params=_cparams(("arbitrary",)),
        name="fourier_out",
    )(mp, x, w_out_bf, mod_all, gain_ffn, wr)


QKV_TM = 1024
TQ = 1024
NSB = TQ // BLK
KWIN = TQ + 2 * BLK
NBLK = S // BLK
KPAD = 128
VROWS = HD + 16
ATT_AHEAD = 1
LOG2E = math.log2(math.e)


def _bias_kernel(rb_ref, bucket_ref, o_ref):
    h = pl.program_id(0)
    bucket = bucket_ref[...]
    acc = jnp.full(bucket.shape, NEG_INF, F32)
    for k in range(NBUCKETS):
        acc = jnp.where(bucket == k, rb_ref[k, h] * LOG2E, acc)
    o_ref[0] = acc


def _bias_table(rel_bias, bucket_t):
    return pl.pallas_call(
        _bias_kernel,
        grid=(NH,),
        in_specs=[
            pl.BlockSpec(memory_space=pltpu.SMEM),
            pl.BlockSpec((3 * BLK, BLK), lambda h: (0, 0)),
        ],
        out_specs=pl.BlockSpec((1, 3 * BLK, BLK), lambda h: (h // GQA, 0, h % GQA)),
        out_shape=jax.ShapeDtypeStruct((NKV, 3 * BLK, GQA * BLK), F32),
        compiler_params=_cparams(("arbitrary",)),
        name="rel_bias_table",
    )(rel_bias, bucket_t)


def _qkv_kernel(x_ref, mod_ref, gain_ref, wqt_ref, wk_ref, wvt_ref, qg_ref, kg_ref, qt_ref, k_ref, vt_ref):
    m = mod_ref[0, 0]
    h = _modulate(x_ref[...], gain_ref[0], m[0:1], m[1:2]).astype(BF16)
    nt = (((1,), (1,)), ((), ()))
    qt = lax.dot_general(wqt_ref[...], h, nt, preferred_element_type=F32)
    tm = qt.shape[1]
    q3 = qt.reshape(NH, HD, tm)
    q3 = q3 * lax.rsqrt(jnp.mean(q3 * q3, axis=1, keepdims=True) + EPS)
    qt_ref[...] = (q3.reshape(NH * HD, tm) * qg_ref[...]).astype(BF16)
    k = jnp.dot(h, wk_ref[...], preferred_element_type=F32)
    for g in range(NKV):
        kg = k[:, g * KPAD:(g + 1) * KPAD]
        ms = jnp.sum(kg * kg, axis=-1, keepdims=True) * (1.0 / HD)
        k_ref[:, g * KPAD:(g + 1) * KPAD] = (kg * lax.rsqrt(ms + EPS) * kg_ref[...]).astype(BF16)
    vt_ref[...] = lax.dot_general(wvt_ref[...], h, nt, preferred_element_type=F32).astype(BF16)


def _attn_kernel(qt_ref, kp_ref, kc_ref, kn_ref, vp_ref, vc_ref, vn_ref, x_ref, w_ref, mod_ref,
                 bias_ref, sink_ref, gf_ref, wr_ref, o_ref, h_ref, aff_ref, att_ref, s_ref):
    i = pl.program_id(0)
    kwin = jnp.concatenate([kp_ref[...], kc_ref[...], kn_ref[...]], axis=0)
    vwin = jnp.concatenate([vp_ref[...], vc_ref[...], vn_ref[...]], axis=1)
    ones_rows = (lax.broadcasted_iota(jnp.int32, (VROWS - HD, KWIN), 0) == 0).astype(BF16)
    vaug = [jnp.concatenate([vwin[g * HD:(g + 1) * HD], ones_rows], axis=0) for g in range(NKV)]
    key_pos = i * TQ - BLK + lax.broadcasted_iota(jnp.int32, (KWIN, 1), 0)
    key_mask = jnp.where((key_pos >= 0) & (key_pos < S), 0.0, NEG_INF).astype(BF16)
    lane = lax.broadcasted_iota(jnp.int32, (1, NKV * KPAD), 1)
    kwin = jnp.where(lane % KPAD == HD, key_mask, kwin)
    q_ones = (lax.broadcasted_iota(jnp.int32, (KPAD - HD, GQA * BLK), 0) == 0).astype(BF16)
    items = [(g, r) for g in range(NKV) for r in range(NSB)]

    def scores(g, r):
        kr = kwin[r * BLK:r * BLK + 3 * BLK, g * KPAD:(g + 1) * KPAD]
        qg = jnp.concatenate(
            [qt_ref[(GQA * g + hh) * HD:(GQA * g + hh + 1) * HD, r * BLK:(r + 1) * BLK] for hh in range(GQA)],
            axis=1)
        qa = jnp.concatenate([qg, q_ones], axis=0)
        return jnp.dot(kr, qa, preferred_element_type=F32)

    def probs(s, g, r):
        sink = sink_ref[g] * LOG2E
        s = s + bias_ref[g]
        mx = jnp.maximum(jnp.max(s, axis=0, keepdims=True), sink)
        return jnp.exp2(s - mx).astype(BF16), jnp.exp2(sink - mx)

    nslot = ATT_AHEAD + 1
    for n in range(ATT_AHEAD):
        s_ref[n % nslot] = scores(*items[n])
    for n, (g, r) in enumerate(items):
        if n + ATT_AHEAD < len(items):
            s_ref[(n + ATT_AHEAD) % nslot] = scores(*items[n + ATT_AHEAD])
        p, psink = probs(s_ref[n % nslot], g, r)
        ot = jnp.dot(vaug[g][:, r * BLK:r * BLK + 3 * BLK], p, preferred_element_type=F32)
        ot = ot[:HD] * (1.0 / (ot[HD:HD + 1] + psink))
        for hh in range(GQA):
            hd0 = (GQA * g + hh) * HD
            att_ref[hd0:hd0 + HD, r * BLK:(r + 1) * BLK] = ot[:, hh * BLK:(hh + 1) * BLK]
    g1 = mod_ref[0, 0][2:3]
    att = att_ref[...].T.astype(BF16)
    y = jnp.dot(att, w_ref[...], preferred_element_type=F32)
    xn = x_ref[...] + g1 * y
    o_ref[...] = xn
    h_ref[...], aff_ref[...] = _route(xn, mod_ref[0, 0], gf_ref[0], wr_ref[...])


def _attn_weights(w_qkv, q_gain, k_gain, sink):
    wq, wk, wv = w_qkv[:, :NH * HD], w_qkv[:, NH * HD:(NH + NKV) * HD], w_qkv[:, (NH + NKV) * HD:]
    wqt = wq.T.astype(BF16)
    wvt = wv.T.astype(BF16)
    wk_pad = jnp.pad(wk.reshape(D, NKV, HD), ((0, 0), (0, 0), (0, KPAD - HD))).reshape(D, NKV * KPAD).astype(BF16)
    qg_col = jnp.tile(q_gain * (HD ** -0.5 * LOG2E), NH).reshape(NH * HD, 1)
    kg_row = jnp.pad(k_gain, (0, KPAD - HD)).reshape(1, KPAD)
    sink_row = jnp.repeat(sink, BLK).reshape(NKV, 1, GQA * BLK)
    return wqt, wk_pad, wvt, qg_col, kg_row, sink_row


def _attn_layer(x, b, mod_all, gain, gain_ffn, wr, aw, w_out_bf, bias_tab, layer):
    wqt, wk_pad, wvt, qg_col, kg_row, sink_row = aw
    qt, k, vt = pl.pallas_call(
        _qkv_kernel,
        grid=(S // QKV_TM,),
        in_specs=[
            _x_spec(x, b, QKV_TM),
            _mod_spec(layer, b), _gain_spec(layer),
            pl.BlockSpec((NH * HD, D), lambda i: (0, 0)),
            pl.BlockSpec((D, NKV * KPAD), lambda i: (0, 0)),
            pl.BlockSpec((NKV * HD, D), lambda i: (0, 0)),
            pl.BlockSpec((NH * HD, 1), lambda i: (0, 0)),
            pl.BlockSpec((1, KPAD), lambda i: (0, 0)),
        ],
        out_specs=[
            pl.BlockSpec((NH * HD, QKV_TM), lambda i: (0, i)),
            pl.BlockSpec((QKV_TM, NKV * KPAD), lambda i: (i, 0)),
            pl.BlockSpec((NKV * HD, QKV_TM), lambda i: (0, i)),
        ],
        out_shape=[
            jax.ShapeDtypeStruct((NH * HD, S), BF16),
            jax.ShapeDtypeStruct((S, NKV * KPAD), BF16),
            jax.ShapeDtypeStruct((NKV * HD, S), BF16),
        ],
        compiler_params=_cparams(("arbitrary",)),
        name="attn_qkv",
    )(x, mod_all, gain, wqt, wk_pad, wvt, qg_col, kg_row)

    kw = NKV * KPAD
    vw = NKV * HD
    kprev = pl.BlockSpec((BLK, kw), lambda i: (jnp.maximum(i * NSB - 1, 0), 0))
    kcur = pl.BlockSpec((TQ, kw), lambda i: (i, 0))
    knext = pl.BlockSpec((BLK, kw), lambda i: (jnp.minimum((i + 1) * NSB, NBLK - 1), 0))
    vprev = pl.BlockSpec((vw, BLK), lambda i: (0, jnp.maximum(i * NSB - 1, 0)))
    vcur = pl.BlockSpec((vw, TQ), lambda i: (0, i))
    vnext = pl.BlockSpec((vw, BLK), lambda i: (0, jnp.minimum((i + 1) * NSB, NBLK - 1)))
    return pl.pallas_call(
        _attn_kernel,
        grid=(S // TQ,),
        in_specs=[
            pl.BlockSpec((NH * HD, TQ), lambda i: (0, i)),
            kprev, kcur, knext, vprev, vcur, vnext,
            _x_spec(x, b, TQ),
            pl.BlockSpec((NH * HD, D), lambda i: (0, 0)),
            _mod_spec(layer, b),
            pl.BlockSpec((NKV, 3 * BLK, GQA * BLK), lambda i: (0, 0, 0)),
            pl.BlockSpec((NKV, 1, GQA * BLK), lambda i: (0, 0, 0)),
            _gain_spec(layer),
            pl.BlockSpec((D, 2 * EP), lambda i: (0, 0)),
        ],
        out_specs=_routed_specs(TQ),
        out_shape=_ROUTED_SHAPES,
        scratch_shapes=[pltpu.VMEM((NH * HD, TQ), F32), pltpu.VMEM((ATT_AHEAD + 1, 3 * BLK, GQA * BLK), F32)],
        compiler_params=_cparams(("arbitrary",)),
        name="attn_core",
    )(qt, k, k, k, vt, vt, vt, x, w_out_bf, mod_all, bias_tab, sink_row, gain_ffn, wr)


FFN_TF = 1024
FFN_RT = 512
FFN_WT = 256
FFN_WBUF = 3


def _select_kernel(aff_ref, tri_ref, pos_ref):
    aff = aff_ref[...]
    bits = pltpu.bitcast(aff, jnp.int32)
    rows = aff.shape[0]

    def count_ge(v):
        return jnp.sum((bits >= v).astype(jnp.int32), axis=1, keepdims=True)

    def body(t, cur):
        sh = 28 - 2 * t
        c1, c2, c3 = cur | (jnp.int32(1) << sh), cur | (jnp.int32(2) << sh), cur | (jnp.int32(3) << sh)
        n1, n2, n3 = count_ge(c1), count_ge(c2), count_ge(c3)
        return jnp.where(n3 >= CAP, c3, jnp.where(n2 >= CAP, c2, jnp.where(n1 >= CAP, c1, cur)))

    top = jnp.full((rows, 1), 1 << 30, jnp.int32)
    thr = lax.fori_loop(0, 15, body, jnp.where(count_ge(top) >= CAP, top, 0))
    gt = bits > thr
    eq = bits == thr
    need = CAP - jnp.sum(gt.astype(jnp.int32), axis=1, keepdims=True)
    tri = tri_ref[...]

    def cumsum_excl(mask_f):
        off = jnp.zeros((mask_f.shape[0], 1), F32)
        outs = []
        for j in range(S // 128):
            mj = mask_f[:, j * 128:(j + 1) * 128]
            loc = jnp.dot(mj.astype(BF16), tri, preferred_element_type=F32)
            outs.append(loc - mj + off)
            off = off + loc[:, 127:128]
        return jnp.concatenate(outs, axis=1)

    counts = cumsum_excl(jnp.concatenate([gt, eq], axis=0).astype(F32))
    gt_before, eq_before = counts[:rows], counts[rows:]
    needf = need.astype(F32)
    sel = gt | (eq & (eq_before < needf))
    pos = gt_before + jnp.minimum(eq_before, needf)
    pos_ref[...] = jnp.where(sel, pos.astype(jnp.int32), -1)


SC_CORES = 2
SC_SUBCORES = 16
SC_TILES = SC_CORES * SC_SUBCORES
DISPATCH_SLOTS = CAP * E // SC_TILES
DISPATCH_ROWS = 64
COMBINE_RANGE = 1024
COMBINE_ROWS = 32
SLAB = 128
NSLAB = D // SLAB


def _sc_mesh():
    return plsc.VectorSubcoreMesh(core_axis_name="c", subcore_axis_name="s",
                                  num_cores=SC_CORES, num_subcores=SC_SUBCORES)


def _dispatch_body(pos_hbm, aff_hbm, h_hbm, xin_hbm, idx_hbm, gate_hbm, pos_v, aff_v, idx_v, gate_v,
                   rows_a, rows_b, gsem_a, gsem_b, wsem_a, wsem_b):
    w = lax.axis_index("s") * SC_CORES + lax.axis_index("c")
    e = w // 2
    lo = (w % 2) * DISPATCH_SLOTS
    pltpu.sync_copy(pos_hbm.at[e], pos_v)
    pltpu.sync_copy(aff_hbm.at[e], aff_v)

    @pl.loop(0, S // 16)
    def _(i):
        p = pos_v[pl.ds(i * 16, 16)] - lo
        m = (p >= 0) & (p < DISPATCH_SLOTS)
        tok = lax.iota(jnp.int32, 16) + i * 16
        plsc.store_scatter(idx_v, [p], tok, mask=m)
        plsc.store_scatter(gate_v, [p], aff_v[pl.ds(i * 16, 16)], mask=m)

    pltpu.sync_copy(idx_v, idx_hbm.at[e, pl.ds(lo, DISPATCH_SLOTS)])
    pltpu.sync_copy(gate_v, gate_hbm.at[e, pl.ds(lo, DISPATCH_SLOTS)])

    bufs, gsems, wsems = (rows_a, rows_b), (gsem_a, gsem_b), (wsem_a, wsem_b)
    nchunk = DISPATCH_SLOTS // DISPATCH_ROWS

    def gather(j):
        return pltpu.async_copy(h_hbm.at[idx_v.at[pl.ds(j * DISPATCH_ROWS, DISPATCH_ROWS)]], bufs[j % 2], gsems[j % 2])

    pending_gather = gather(0)
    writes = [None, None]
    for j in range(nchunk):
        pending_gather.wait()
        writes[j % 2] = pltpu.async_copy(
            bufs[j % 2], xin_hbm.at[e, pl.ds(lo + j * DISPATCH_ROWS, DISPATCH_ROWS)], wsems[j % 2])
        if j + 1 < nchunk:
            if writes[(j + 1) % 2] is not None:
                writes[(j + 1) % 2].wait()
            pending_gather = gather(j + 1)
    writes[(nchunk - 2) % 2].wait()
    writes[(nchunk - 1) % 2].wait()


def _dispatch(pos, aff, h):
    return pl.kernel(
        _dispatch_body, mesh=_sc_mesh(),
        out_type=[jax.ShapeDtypeStruct((E, CAP, D // 2), jnp.int32),
                  jax.ShapeDtypeStruct((E, CAP), jnp.int32),
                  jax.ShapeDtypeStruct((E, CAP), F32)],
        scratch_types=[pltpu.VMEM((S,), jnp.int32), pltpu.VMEM((S,), F32),
                       pltpu.VMEM((DISPATCH_SLOTS,), jnp.int32), pltpu.VMEM((DISPATCH_SLOTS,), F32),
                       pltpu.VMEM((DISPATCH_ROWS, D // 2), jnp.int32), pltpu.VMEM((DISPATCH_ROWS, D // 2), jnp.int32),
                       pltpu.SemaphoreType.DMA, pltpu.SemaphoreType.DMA,
                       pltpu.SemaphoreType.DMA, pltpu.SemaphoreType.DMA],
        compiler_params=pltpu.CompilerParams(needs_layout_passes=False),
        name="moe_dispatch",
    )(pos, aff, h)


def _combine_ranges(seed, y_hbm, idx_hbm, out_hbm, idx_v, li_v, bufs):
    rows = bufs[:NSLAB]
    accs = bufs[NSLAB:]
    c = lax.axis_index("c")
    s = lax.axis_index("s")
    share = COMBINE_RANGE // SC_SUBCORES
    pltpu.sync_copy(idx_hbm.at[s], idx_v)
    lane = lax.iota(jnp.int32, 16)

    @pl.loop(0, S // COMBINE_RANGE // SC_CORES)
    def _(r):
        t0 = (r * SC_CORES + c) * COMBINE_RANGE
        row0 = t0 + s * share
        seed(row0, tuple(accs[q].at[pl.ds(s * share, share)] for q in range(NSLAB)))
        plsc.subcore_barrier()

        @pl.loop(0, CAP // COMBINE_ROWS)
        def _(j):
            hits = jnp.zeros((16,), jnp.int32)
            for v in range(COMBINE_ROWS // 16):
                t = idx_v[pl.ds(j * COMBINE_ROWS + v * 16, 16)] - t0
                ok = (t >= 0) & (t < COMBINE_RANGE)
                li_v[pl.ds(v * 16, 16)] = jnp.where(ok, t, COMBINE_RANGE + lane)
                hits = hits + plsc.all_reduce_population_count(ok)

            @pl.when(jnp.max(hits) > 0)
            def _():
                pltpu.sync_copy(
                    tuple(y_hbm.at[s, pl.ds(j * COMBINE_ROWS, COMBINE_ROWS), pl.ds(q * SLAB, SLAB)]
                          for q in range(NSLAB)),
                    tuple(rows))
                pltpu.sync_copy(tuple(rows), tuple(accs[q].at[li_v] for q in range(NSLAB)), add=True)

        plsc.subcore_barrier()
        pltpu.sync_copy(tuple(accs[q].at[pl.ds(s * share, share)] for q in range(NSLAB)),
                        tuple(out_hbm.at[pl.ds(row0, share), pl.ds(q * SLAB, SLAB)] for q in range(NSLAB)))


def _combine_body(y_hbm, idx_hbm, x_hbm, out_hbm, idx_v, li_v, *bufs):
    share = COMBINE_RANGE // SC_SUBCORES

    def seed(row0, dst):
        pltpu.sync_copy(tuple(x_hbm.at[pl.ds(row0, share), pl.ds(q * SLAB, SLAB)] for q in range(NSLAB)), dst)

    _combine_ranges(seed, y_hbm, idx_hbm, out_hbm, idx_v, li_v, bufs)


def _moe_sum_body(y_hbm, idx_hbm, out_hbm, idx_v, li_v, zero_v, *bufs):
    @pl.loop(0, zero_v.shape[0])
    def _(r):
        for v in range(SLAB // 16):
            zero_v[r, pl.ds(v * 16, 16)] = jnp.zeros((16,), F32)

    _combine_ranges(lambda row0, dst: pltpu.sync_copy(tuple(zero_v for _ in range(NSLAB)), dst),
                    y_hbm, idx_hbm, out_hbm, idx_v, li_v, bufs)


def _combine_scratch():
    return ([pltpu.VMEM((CAP,), jnp.int32), pltpu.VMEM((COMBINE_ROWS,), jnp.int32)],
            [pltpu.VMEM((COMBINE_ROWS, SLAB), F32) for _ in range(NSLAB)]
            + [pltpu.VMEM_SHARED((COMBINE_RANGE + 16, SLAB), F32) for _ in range(NSLAB)])


def _combine(y, idx, x, out_rows=S):
    head, tail = _combine_scratch()
    return pl.kernel(
        _combine_body, mesh=_sc_mesh(),
        out_type=jax.ShapeDtypeStruct((out_rows, D), F32),
        scratch_types=head + tail,
        compiler_params=pltpu.CompilerParams(needs_layout_passes=False),
        name="moe_combine",
    )(y, idx, x)


def _moe_sum(y, idx):
    head, tail = _combine_scratch()
    return pl.kernel(
        _moe_sum_body, mesh=_sc_mesh(),
        out_type=jax.ShapeDtypeStruct((S, D), F32),
        scratch_types=head + [pltpu.VMEM((COMBINE_RANGE // SC_SUBCORES, SLAB), F32)] + tail,
        compiler_params=pltpu.CompilerParams(needs_layout_passes=False),
        name="moe_sum",
    )(y, idx)


FINAL_TM = 1024


def _final_add_kernel(big_ref, x_ref, m_ref, o_ref):
    del big_ref
    o_ref[...] = x_ref[...] + m_ref[...]


def _final_add(big, x, moe, b):
    nb = S // FINAL_TM
    return pl.pallas_call(
        _final_add_kernel,
        grid=(nb,),
        in_specs=[
            pl.BlockSpec(memory_space=pl.ANY),
            pl.BlockSpec((FINAL_TM, D), lambda i: (i, 0)),
            pl.BlockSpec((FINAL_TM, D), lambda i: (i, 0)),
        ],
        out_specs=pl.BlockSpec((FINAL_TM, D), lambda i: (b * nb + i, 0)),
        out_shape=jax.ShapeDtypeStruct((B * S, D), F32),
        input_output_aliases={0: 0},
        compiler_params=_cparams(("arbitrary",)),
        name="final_add",
    )(big, x, moe)


def _ffn_step(gate_ref, mod_ref, indices, x_ref, wg_ref, wu_ref, wd_ref, o_ref):
    e, f = indices
    last = F // FFN_TF - 1
    nt = FFN_TF // FFN_WT
    wg = [wg_ref[0, 0, :, j * FFN_WT:(j + 1) * FFN_WT].astype(BF16) for j in range(nt)]
    wu = [wu_ref[0, 0, :, j * FFN_WT:(j + 1) * FFN_WT].astype(BF16) for j in range(nt)]
    wd = wd_ref[0, 0].astype(BF16)
    is_first = f == 0
    is_last = f == last
    g2 = jnp.where(is_last, mod_ref[0, 0][5:6], 1.0)
    g_row = gate_ref[pl.ds(e, 1), :]
    g_col = jnp.broadcast_to(g_row, (128, CAP)).T[:, 0:1]
    for r in range(CAP // FFN_RT):
        rows = slice(r * FFN_RT, (r + 1) * FFN_RT)
        xr = _unpack_bf16_pairs(x_ref[0, rows, :])
        acts = []
        for j in range(nt):
            g = jnp.dot(xr, wg[j], preferred_element_type=F32)
            u = jnp.dot(xr, wu[j], preferred_element_type=F32)
            acts.append((g * _sigmoid(g) * u).astype(BF16))
        y = jnp.dot(jnp.concatenate(acts, axis=1), wd, preferred_element_type=F32)
        prev = jnp.where(is_first, 0.0, o_ref[0, rows, :])
        gate = jnp.where(is_last, g_col[rows, :], 1.0)
        o_ref[0, rows, :] = (prev + y) * gate * g2


def _select(aff, tri):
    return pl.pallas_call(
        _select_kernel,
        grid=(1,),
        in_specs=[
            pl.BlockSpec((E, S), lambda i: (0, 0)),
            pl.BlockSpec((128, 128), lambda i: (0, 0)),
        ],
        out_specs=pl.BlockSpec((E, S), lambda i: (0, 0)),
        out_shape=jax.ShapeDtypeStruct((E, S), jnp.int32),
        compiler_params=_cparams(("arbitrary",)),
        name="moe_select",
    )(aff, tri)


def _ffn_kernel(layer, x_hbm, wg_hbm, wu_hbm, wd_hbm, gate_ref, mod_ref, o_hbm):
    deep = pl.Buffered(FFN_WBUF)
    pltpu.emit_pipeline(
        functools.partial(_ffn_step, gate_ref, mod_ref),
        grid=(E, F // FFN_TF),
        in_specs=[
            pl.BlockSpec((1, CAP, D // 2), lambda e, f: (e, 0, 0)),
            pl.BlockSpec((1, 1, D, FFN_TF), lambda e, f: (layer, e, 0, f), pipeline_mode=deep),
            pl.BlockSpec((1, 1, D, FFN_TF), lambda e, f: (layer, e, 0, f), pipeline_mode=deep),
            pl.BlockSpec((1, 1, FFN_TF, D), lambda e, f: (layer, e, f, 0), pipeline_mode=deep),
        ],
        out_specs=[pl.BlockSpec((1, CAP, D), lambda e, f: (e, 0, 0))],
        _explicit_indices=True,
    )(x_hbm, wg_hbm, wu_hbm, wd_hbm, o_hbm)


def _ffn(xin, gate, b, mod_all, w_gate, w_up, w_down, layer):
    hbm = pl.BlockSpec(memory_space=pl.ANY)
    return pl.pallas_call(
        functools.partial(_ffn_kernel, layer),
        grid=(1,),
        in_specs=[
            hbm, hbm, hbm, hbm,
            pl.BlockSpec((E, CAP), lambda i: (0, 0)),
            pl.BlockSpec((1, 1, 6, D), lambda i: (layer, b, 0, 0)),
        ],
        out_specs=hbm,
        out_shape=jax.ShapeDtypeStruct((E, CAP, D), F32),
        compiler_params=pltpu.CompilerParams(vmem_limit_bytes=58 * 1024 * 1024),
        name="moe_ffn",
    )(xin, w_gate, w_up, w_down, gate, mod_all)


def kernel(x, c, w_ada, b_ada, norm_mix, norm_ffn, w_fourier_out, w_qkv, w_attn_out, q_gain, k_gain,
           sink, rel_bias, w_router, w_gate, w_up, w_down):
    m0, m2 = _dft_tables()
    m1 = _stage1_table()
    bucket_t = _bucket_table()
    tri = jnp.asarray(np.triu(np.ones((128, 128), np.float32)), BF16)
    mod_all = _ada(c, w_ada, b_ada)
    bias_tab = _bias_table(rel_bias, bucket_t)
    gain_mix = norm_mix.reshape(DEPTH, 1, D)
    gain_ffn = norm_ffn.reshape(DEPTH, 1, D)
    xs = [x, x]
    for layer in range(DEPTH):
        j = layer // 2
        wr = jnp.pad(w_router[layer], ((0, 0), (0, EP - E)))
        wr1 = wr.astype(BF16)
        wr = jnp.concatenate([wr1, (wr - wr1.astype(F32)).astype(BF16)], axis=1)
        if layer % 2 == 0:
            w_out_bf = w_fourier_out[j].astype(BF16)
            routed = [_fourier_layer(xs[b], b, mod_all, gain_mix, gain_ffn, wr, w_out_bf, layer, m0, m1, m2)
                      for b in range(B)]
        else:
            aw = _attn_weights(w_qkv[j], q_gain[j], k_gain[j], sink[j])
            w_out_bf = w_attn_out[j].astype(BF16)
            routed = [_attn_layer(xs[b], b, mod_all, gain_mix, gain_ffn, wr, aw, w_out_bf, bias_tab, layer)
                      for b in range(B)]
        xs = [routed[b][0] for b in range(B)]
        pos = [_select(routed[b][2], tri) for b in range(B)]
        disp = [_dispatch(pos[b], routed[b][2], routed[b][1]) for b in range(B)]
        ys = [_ffn(disp[b][0], disp[b][2], b, mod_all, w_gate, w_up, w_down, layer) for b in range(B)]
        if layer < DEPTH - 1:
            xs = [_combine(ys[b], disp[b][1], xs[b]) for b in range(B)]
    out = _combine(ys[0], disp[0][1], xs[0], out_rows=B * S)
    for b in range(1, B):
        out = _final_add(out, xs[b], _moe_sum(ys[b], disp[b][1]), b)
    return out.reshape(B, S, D)
```

```python
import functools
import math

import numpy as np
import jax
import jax.numpy as jnp
from jax import lax
from jax.experimental import pallas as pl
from jax.experimental.pallas import tpu as pltpu
from jax.experimental.pallas import tpu_sc as plsc

D = 1024
B = 2
S = 8192
DEPTH = 4
GROUPS = 4
GD = D // GROUPS
HD = 64
NH = 16
NKV = 4
GQA = NH // NKV
WINDOW = 128
BLK = 128
NBUCKETS = 32
MAXDIST = 128
E = 16
CAP = 2 * S // E
F = 2 * D
EPS = 1e-6
NEG_INF = -1e30

N1 = 128
N2 = 64
INV_NORM = 1.0 / math.sqrt(S * GD)

F32 = jnp.float32
BF16 = jnp.bfloat16


def _cparams(sem, vmem_mb=48):
    return pltpu.CompilerParams(dimension_semantics=sem, vmem_limit_bytes=vmem_mb * 1024 * 1024)


def _dft_tables():
    c = np.arange(GD)
    ang0 = 2.0 * np.pi * ((c[:, None] * c[None, :]) % GD) / GD
    m0 = np.concatenate([np.cos(ang0), -np.sin(ang0)], axis=1)
    k2 = np.arange(N2)
    ang2 = 2.0 * np.pi * ((k2[:, None] * k2[None, :]) % N2) / N2
    m2 = np.stack([np.cos(ang2), np.sin(ang2)], axis=2).reshape(N2, 2 * N2)
    return jnp.asarray(m0, BF16), jnp.asarray(m2, BF16)


def _stage1_table():
    s2 = np.arange(N2)[:, None, None]
    k1 = np.arange(N1)[None, :, None]
    s1 = np.arange(N1)[None, None, :]
    th = ((k1 * (N2 * s1 + s2)) % S) * (2.0 * np.pi / S)
    co, si = np.cos(th), np.sin(th)
    top = np.concatenate([co, si], axis=2)
    bot = np.concatenate([-si, co], axis=2)
    return jnp.asarray(np.concatenate([top, bot], axis=1).astype(np.float32), BF16)


def _bucket_table():
    q_off = np.arange(BLK)
    k_off = np.arange(3 * BLK) - BLK
    rel = k_off[:, None] - q_off[None, :]
    half = NBUCKETS // 2
    max_exact = half // 2
    ret = np.where(rel > 0, half, 0)
    n = np.abs(rel)
    nf = np.maximum(n, 1).astype(np.float32)
    ratio = (np.log(nf / np.float32(max_exact)) / np.float32(math.log(MAXDIST / max_exact))).astype(np.float32)
    large = max_exact + (ratio * np.float32(half - max_exact)).astype(np.int32)
    large = np.minimum(large, half - 1)
    bucket = ret + np.where(n < max_exact, n, large)
    return jnp.asarray(np.where(np.abs(rel) <= WINDOW, bucket, -1).astype(np.int32))


def _modulate(x, gain, shift, scale):
    ms = jnp.mean(x * x, axis=-1, keepdims=True)
    return x * lax.rsqrt(ms + EPS) * (gain * (1.0 + scale)) + shift


def _sigmoid(x):
    return 1.0 / (1.0 + jnp.exp(-x))


ADA_TN = 1536


def _ada_kernel(ct_ref, w_ref, b_ref, o_ref):
    ct = ct_ref[...]
    ca = ct * _sigmoid(ct)
    w = w_ref[0]
    for b in range(B):
        o_ref[0, b:b + 1, :] = jnp.sum(w * ca[:, b:b + 1], axis=0, keepdims=True) + b_ref[0]


def _ada(c, w_ada, b_ada):
    out = pl.pallas_call(
        _ada_kernel,
        grid=(DEPTH, 6 * D // ADA_TN),
        in_specs=[
            pl.BlockSpec((D, B), lambda l, j: (0, 0)),
            pl.BlockSpec((1, D, ADA_TN), lambda l, j: (l, 0, j)),
            pl.BlockSpec((1, 1, ADA_TN), lambda l, j: (l, 0, j)),
        ],
        out_specs=pl.BlockSpec((1, B, ADA_TN), lambda l, j: (l, 0, j)),
        out_shape=jax.ShapeDtypeStruct((DEPTH, B, 6 * D), F32),
        compiler_params=_cparams(("arbitrary", "arbitrary"), 32),
        name="ada_mod",
    )(c.T, w_ada, b_ada.reshape(DEPTH, 1, 6 * D))
    return out.reshape(DEPTH, B, 6, D)


def _mod_spec(layer, b):
    return pl.BlockSpec((1, 1, 6, D), lambda *_: (layer, b, 0, 0))


def _gain_spec(layer):
    return pl.BlockSpec((1, 1, D), lambda *_: (layer, 0, 0))


def _x_spec(x, b, tm):
    if x.ndim == 3:
        return pl.BlockSpec((None, tm, D), lambda i: (b, i, 0))
    return pl.BlockSpec((tm, D), lambda i: (i, 0))


EP = 128

_ROUTED_SHAPES = [jax.ShapeDtypeStruct((S, D), F32),
                  jax.ShapeDtypeStruct((S, D // 2), jnp.int32),
                  jax.ShapeDtypeStruct((E, S), F32)]


def _routed_specs(tm):
    return [pl.BlockSpec((tm, D), lambda i: (i, 0)),
            pl.BlockSpec((tm, D // 2), lambda i: (i, 0)),
            pl.BlockSpec((E, tm), lambda i: (0, i))]


STREAM_BUFS = 3


def _streamed_call(body, streamed, consts, *, grid, in_specs, const_specs, out_specs, out_shape, scratch_shapes=(),
                   name):
    n_in, n_c, n_out = len(in_specs), len(const_specs), len(out_specs)
    deep = [pl.BlockSpec(sp.block_shape, sp.index_map, pipeline_mode=pl.Buffered(STREAM_BUFS)) for sp in in_specs]

    def kernel(*refs):
        ins, cs = refs[:n_in], refs[n_in:n_in + n_c]
        outs, scratch = refs[n_in + n_c:n_in + n_c + n_out], refs[n_in + n_c + n_out:]
        pltpu.emit_pipeline(lambda *blk: body(*blk[:n_in], *cs, *blk[n_in:], *scratch),
                            grid=grid, in_specs=deep, out_specs=list(out_specs))(*ins, *outs)

    hbm = pl.BlockSpec(memory_space=pl.ANY)
    return pl.pallas_call(
        kernel,
        grid=(1,),
        in_specs=[hbm] * n_in + list(const_specs),
        out_specs=[hbm] * n_out,
        out_shape=list(out_shape),
        scratch_shapes=list(scratch_shapes),
        compiler_params=pltpu.CompilerParams(vmem_limit_bytes=56 * 1024 * 1024),
        name=name,
    )(*streamed, *consts)


F0_TM = 1024
SUB = 8
SEQ_PICKS = SUB


def _pack_bf16_pairs(h):
    hb = h.astype(BF16).astype(F32)
    lo = pltpu.bitcast(hb[:, :D // 2], jnp.int32)
    hi = pltpu.bitcast(hb[:, D // 2:], jnp.int32)
    return hi | lax.shift_right_logical(lo, jnp.int32(16))


def _unpack_bf16_pairs(xp):
    lo = pltpu.bitcast(xp << 16, F32).astype(BF16)
    hi = pltpu.bitcast(xp & jnp.int32(-65536), F32).astype(BF16)
    return jnp.concatenate([lo, hi], axis=1)


def _route(xn, m, gain_ffn, wr):
    h = _modulate(xn, gain_ffn, m[3:4], m[4:5])
    h1 = h.astype(BF16)
    h2 = (h - h1.astype(F32)).astype(BF16)
    part = jnp.dot(h1, wr, preferred_element_type=F32)
    logits = part[:, :EP] + part[:, EP:] + jnp.dot(h2, wr[:, :EP], preferred_element_type=F32)
    lt = logits.T[:E]
    ex = jnp.exp(lt - jnp.max(lt, axis=0, keepdims=True))
    return _pack_bf16_pairs(h), ex / jnp.sum(ex, axis=0, keepdims=True)


def _f0_kernel(x_ref, mod_ref, gain_ref, m0_ref, o_ref):
    m = mod_ref[0, 0]
    h = _modulate(x_ref[...], gain_ref[0], m[0:1], m[1:2]).astype(BF16)
    m0 = m0_ref[...]
    r = [jnp.dot(h[:, g * GD:(g + 1) * GD], m0, preferred_element_type=F32) for g in range(GROUPS)]
    o_ref[0] = _pack_bf16_pairs(jnp.concatenate([rg[:, :GD] for rg in r], axis=1))
    o_ref[1] = _pack_bf16_pairs(jnp.concatenate([rg[:, GD:] for rg in r], axis=1))


def _f1_kernel(w_ref, m1_ref, o_ref, scr_ref):
    n = o_ref.shape[0]
    for j in range(n):
        scr_ref[j] = w_ref[:, j, :]
    for j in range(n):
        w = _unpack_bf16_pairs(scr_ref[j])
        o_ref[j] = _pack_bf16_pairs(jnp.dot(m1_ref[j], w, preferred_element_type=F32))


def _f2_kernel(u_ref, m2_ref, o_ref, scr_ref):
    m2 = m2_ref[...]
    n = o_ref.shape[0]
    for k in range(n):
        scr_ref[k] = u_ref[:, k, :]
    for k in range(n):
        u = _unpack_bf16_pairs(scr_ref[k])
        o_ref[k] = _pack_bf16_pairs(jnp.dot(m2, u, preferred_element_type=F32))


def _f3_kernel(mp_ref, x_ref, w_ref, mod_ref, gf_ref, wr_ref, o_ref, h_ref, aff_ref, scr_ref):
    g1 = mod_ref[0, 0][2:3]
    for j in range(SUB):
        scr_ref[j * N1:(j + 1) * N1, :] = mp_ref[:, j, :]
    a = _unpack_bf16_pairs(scr_ref[...])
    y = jnp.dot(a, w_ref[...], preferred_element_type=F32)
    xn = x_ref[...] + (g1 * INV_NORM) * y
    o_ref[...] = xn
    h_ref[...], aff_ref[...] = _route(xn, mod_ref[0, 0], gf_ref[0], wr_ref[...])


def _fourier_layer(x, b, mod_all, gain, gain_ffn, wr, w_out_bf, layer, m0, m1, m2):
    nb0 = S // F0_TM
    x2 = x.reshape(-1, D)
    row0 = b * nb0 if x.ndim == 3 else 0
    (wc,) = _streamed_call(
        _f0_kernel, (x2,), (mod_all, gain, m0),
        grid=(nb0,),
        in_specs=[pl.BlockSpec((F0_TM, D), lambda i: (row0 + i, 0))],
        const_specs=[_mod_spec(layer, b), _gain_spec(layer), pl.BlockSpec((GD, 2 * GD), lambda i: (0, 0))],
        out_specs=[pl.BlockSpec((2, F0_TM, D // 2), lambda i: (0, i, 0))],
        out_shape=[jax.ShapeDtypeStruct((2, S, D // 2), jnp.int32)],
        name="fourier_chan",
    )
    wc = wc.reshape(2 * N1, N2, D // 2)
    (u,) = _streamed_call(
        _f1_kernel, (wc, m1), (),
        grid=(N2 // SEQ_PICKS,),
        in_specs=[pl.BlockSpec((2 * N1, SEQ_PICKS, D // 2), lambda i: (0, i, 0)),
                  pl.BlockSpec((SEQ_PICKS, 2 * N1, 2 * N1), lambda i: (i, 0, 0))],
        const_specs=[],
        out_specs=[pl.BlockSpec((SEQ_PICKS, 2 * N1, D // 2), lambda i: (i, 0, 0))],
        out_shape=[jax.ShapeDtypeStruct((N2, 2 * N1, D // 2), jnp.int32)],
        scratch_shapes=[pltpu.VMEM((SEQ_PICKS, 2 * N1, D // 2), jnp.int32)],
        name="fourier_seq1",
    )
    u = u.reshape(2 * N2, N1, D // 2)
    (mp,) = _streamed_call(
        _f2_kernel, (u,), (m2,),
        grid=(N1 // SEQ_PICKS,),
        in_specs=[pl.BlockSpec((2 * N2, SEQ_PICKS, D // 2), lambda i: (0, i, 0))],
        const_specs=[pl.BlockSpec((N2, 2 * N2), lambda i: (0, 0))],
        out_specs=[pl.BlockSpec((SEQ_PICKS, N2, D // 2), lambda i: (i, 0, 0))],
        out_shape=[jax.ShapeDtypeStruct((N1, N2, D // 2), jnp.int32)],
        scratch_shapes=[pltpu.VMEM((SEQ_PICKS, 2 * N2, D // 2), jnp.int32)],
        name="fourier_seq2",
    )
    return pl.pallas_call(
        _f3_kernel,
        grid=(N2 // SUB,),
        in_specs=[
            pl.BlockSpec((N1, SUB, D // 2), lambda i: (0, i, 0)),
            _x_spec(x, b, SUB * N1),
            pl.BlockSpec((D, D), lambda i: (0, 0)),
            _mod_spec(layer, b), _gain_spec(layer),
            pl.BlockSpec((D, 2 * EP), lambda i: (0, 0)),
        ],
        out_specs=_routed_specs(SUB * N1),
        out_shape=_ROUTED_SHAPES,
        scratch_shapes=[pltpu.VMEM((SUB * N1, D // 2), jnp.int32)],
        compiler_params=_cparams(("arbitrary",)),
        name="fourier_out",
    )(mp, x, w_out_bf, mod_all, gain_ffn, wr)


QKV_TM = 1024
TQ = 1024
NSB = TQ // BLK
KWIN = TQ + 2 * BLK
NBLK = S // BLK
KPAD = 128
VROWS = HD + 16
ATT_AHEAD = 1
LOG2E = math.log2(math.e)


def _bias_kernel(rb_ref, bucket_ref, o_ref):
    h = pl.program_id(0)
    bucket = bucket_ref[...]
    acc = jnp.full(bucket.shape, NEG_INF, F32)
    for k in range(NBUCKETS):
        acc = jnp.where(bucket == k, rb_ref[k, h] * LOG2E, acc)
    o_ref[0] = acc


def _bias_table(rel_bias, bucket_t):
    return pl.pallas_call(
        _bias_kernel,
        grid=(NH,),
        in_specs=[
            pl.BlockSpec(memory_space=pltpu.SMEM),
            pl.BlockSpec((3 * BLK, BLK), lambda h: (0, 0)),
        ],
        out_specs=pl.BlockSpec((1, 3 * BLK, BLK), lambda h: (h // GQA, 0, h % GQA)),
        out_shape=jax.ShapeDtypeStruct((NKV, 3 * BLK, GQA * BLK), F32),
        compiler_params=_cparams(("arbitrary",)),
        name="rel_bias_table",
    )(rel_bias, bucket_t)


def _qkv_kernel(x_ref, mod_ref, gain_ref, wqt_ref, wk_ref, wvt_ref, qg_ref, kg_ref, qt_ref, k_ref, vt_ref):
    m = mod_ref[0, 0]
    h = _modulate(x_ref[...], gain_ref[0], m[0:1], m[1:2]).astype(BF16)
    nt = (((1,), (1,)), ((), ()))
    qt = lax.dot_general(wqt_ref[...], h, nt, preferred_element_type=F32)
    tm = qt.shape[1]
    q3 = qt.reshape(NH, HD, tm)
    q3 = q3 * lax.rsqrt(jnp.mean(q3 * q3, axis=1, keepdims=True) + EPS)
    qt_ref[...] = (q3.reshape(NH * HD, tm) * qg_ref[...]).astype(BF16)
    k = jnp.dot(h, wk_ref[...], preferred_element_type=F32)
    for g in range(NKV):
        kg = k[:, g * KPAD:(g + 1) * KPAD]
        ms = jnp.sum(kg * kg, axis=-1, keepdims=True) * (1.0 / HD)
        k_ref[:, g * KPAD:(g + 1) * KPAD] = (kg * lax.rsqrt(ms + EPS) * kg_ref[...]).astype(BF16)
    vt_ref[...] = lax.dot_general(wvt_ref[...], h, nt, preferred_element_type=F32).astype(BF16)


def _attn_kernel(qt_ref, kp_ref, kc_ref, kn_ref, vp_ref, vc_ref, vn_ref, x_ref, w_ref, mod_ref,
                 bias_ref, sink_ref, gf_ref, wr_ref, o_ref, h_ref, aff_ref, att_ref, s_ref):
    i = pl.program_id(0)
    kwin = jnp.concatenate([kp_ref[...], kc_ref[...], kn_ref[...]], axis=0)
    vwin = jnp.concatenate([vp_ref[...], vc_ref[...], vn_ref[...]], axis=1)
    ones_rows = (lax.broadcasted_iota(jnp.int32, (VROWS - HD, KWIN), 0) == 0).astype(BF16)
    vaug = [jnp.concatenate([vwin[g * HD:(g + 1) * HD], ones_rows], axis=0) for g in range(NKV)]
    key_pos = i * TQ - BLK + lax.broadcasted_iota(jnp.int32, (KWIN, 1), 0)
    key_mask = jnp.where((key_pos >= 0) & (key_pos < S), 0.0, NEG_INF).astype(BF16)
    lane = lax.broadcasted_iota(jnp.int32, (1, NKV * KPAD), 1)
    kwin = jnp.where(lane % KPAD == HD, key_mask, kwin)
    q_ones = (lax.broadcasted_iota(jnp.int32, (KPAD - HD, GQA * BLK), 0) == 0).astype(BF16)
    items = [(g, r) for g in range(NKV) for r in range(NSB)]

    def scores(g, r):
        kr = kwin[r * BLK:r * BLK + 3 * BLK, g * KPAD:(g + 1) * KPAD]
        qg = jnp.concatenate(
            [qt_ref[(GQA * g + hh) * HD:(GQA * g + hh + 1) * HD, r * BLK:(r + 1) * BLK] for hh in range(GQA)],
            axis=1)
        qa = jnp.concatenate([qg, q_ones], axis=0)
        return jnp.dot(kr, qa, preferred_element_type=F32)

    def probs(s, g, r):
        sink = sink_ref[g] * LOG2E
        s = s + bias_ref[g]
        mx = jnp.maximum(jnp.max(s, axis=0, keepdims=True), sink)
        return jnp.exp2(s - mx).astype(BF16), jnp.exp2(sink - mx)

    nslot = ATT_AHEAD + 1
    for n in range(ATT_AHEAD):
        s_ref[n % nslot] = scores(*items[n])
    for n, (g, r) in enumerate(items):
        if n + ATT_AHEAD < len(items):
            s_ref[(n + ATT_AHEAD) % nslot] = scores(*items[n + ATT_AHEAD])
        p, psink = probs(s_ref[n % nslot], g, r)
        ot = jnp.dot(vaug[g][:, r * BLK:r * BLK + 3 * BLK], p, preferred_element_type=F32)
        ot = ot[:HD] * (1.0 / (ot[HD:HD + 1] + psink))
        for hh in range(GQA):
            hd0 = (GQA * g + hh) * HD
            att_ref[hd0:hd0 + HD, r * BLK:(r + 1) * BLK] = ot[:, hh * BLK:(hh + 1) * BLK]
    g1 = mod_ref[0, 0][2:3]
    att = att_ref[...].T.astype(BF16)
    y = jnp.dot(att, w_ref[...], preferred_element_type=F32)
    xn = x_ref[...] + g1 * y
    o_ref[...] = xn
    h_ref[...], aff_ref[...] = _route(xn, mod_ref[0, 0], gf_ref[0], wr_ref[...])


def _attn_weights(w_qkv, q_gain, k_gain, sink):
    wq, wk, wv = w_qkv[:, :NH * HD], w_qkv[:, NH * HD:(NH + NKV) * HD], w_qkv[:, (NH + NKV) * HD:]
    wqt = wq.T.astype(BF16)
    wvt = wv.T.astype(BF16)
    wk_pad = jnp.pad(wk.reshape(D, NKV, HD), ((0, 0), (0, 0), (0, KPAD - HD))).reshape(D, NKV * KPAD).astype(BF16)
    qg_col = jnp.tile(q_gain * (HD ** -0.5 * LOG2E), NH).reshape(NH * HD, 1)
    kg_row = jnp.pad(k_gain, (0, KPAD - HD)).reshape(1, KPAD)
    sink_row = jnp.repeat(sink, BLK).reshape(NKV, 1, GQA * BLK)
    return wqt, wk_pad, wvt, qg_col, kg_row, sink_row


def _attn_layer(x, b, mod_all, gain, gain_ffn, wr, aw, w_out_bf, bias_tab, layer):
    wqt, wk_pad, wvt, qg_col, kg_row, sink_row = aw
    qt, k, vt = pl.pallas_call(
        _qkv_kernel,
        grid=(S // QKV_TM,),
        in_specs=[
            _x_spec(x, b, QKV_TM),
            _mod_spec(layer, b), _gain_spec(layer),
            pl.BlockSpec((NH * HD, D), lambda i: (0, 0)),
            pl.BlockSpec((D, NKV * KPAD), lambda i: (0, 0)),
            pl.BlockSpec((NKV * HD, D), lambda i: (0, 0)),
            pl.BlockSpec((NH * HD, 1), lambda i: (0, 0)),
            pl.BlockSpec((1, KPAD), lambda i: (0, 0)),
        ],
        out_specs=[
            pl.BlockSpec((NH * HD, QKV_TM), lambda i: (0, i)),
            pl.BlockSpec((QKV_TM, NKV * KPAD), lambda i: (i, 0)),
            pl.BlockSpec((NKV * HD, QKV_TM), lambda i: (0, i)),
        ],
        out_shape=[
            jax.ShapeDtypeStruct((NH * HD, S), BF16),
            jax.ShapeDtypeStruct((S, NKV * KPAD), BF16),
            jax.ShapeDtypeStruct((NKV * HD, S), BF16),
        ],
        compiler_params=_cparams(("arbitrary",)),
        name="attn_qkv",
    )(x, mod_all, gain, wqt, wk_pad, wvt, qg_col, kg_row)

    kw = NKV * KPAD
    vw = NKV * HD
    kprev = pl.BlockSpec((BLK, kw), lambda i: (jnp.maximum(i * NSB - 1, 0), 0))
    kcur = pl.BlockSpec((TQ, kw), lambda i: (i, 0))
    knext = pl.BlockSpec((BLK, kw), lambda i: (jnp.minimum((i + 1) * NSB, NBLK - 1), 0))
    vprev = pl.BlockSpec((vw, BLK), lambda i: (0, jnp.maximum(i * NSB - 1, 0)))
    vcur = pl.BlockSpec((vw, TQ), lambda i: (0, i))
    vnext = pl.BlockSpec((vw, BLK), lambda i: (0, jnp.minimum((i + 1) * NSB, NBLK - 1)))
    return pl.pallas_call(
        _attn_kernel,
        grid=(S // TQ,),
        in_specs=[
            pl.BlockSpec((NH * HD, TQ), lambda i: (0, i)),
            kprev, kcur, knext, vprev, vcur, vnext,
            _x_spec(x, b, TQ),
            pl.BlockSpec((NH * HD, D), lambda i: (0, 0)),
            _mod_spec(layer, b),
            pl.BlockSpec((NKV, 3 * BLK, GQA * BLK), lambda i: (0, 0, 0)),
            pl.BlockSpec((NKV, 1, GQA * BLK), lambda i: (0, 0, 0)),
            _gain_spec(layer),
            pl.BlockSpec((D, 2 * EP), lambda i: (0, 0)),
        ],
        out_specs=_routed_specs(TQ),
        out_shape=_ROUTED_SHAPES,
        scratch_shapes=[pltpu.VMEM((NH * HD, TQ), F32), pltpu.VMEM((ATT_AHEAD + 1, 3 * BLK, GQA * BLK), F32)],
        compiler_params=_cparams(("arbitrary",)),
        name="attn_core",
    )(qt, k, k, k, vt, vt, vt, x, w_out_bf, mod_all, bias_tab, sink_row, gain_ffn, wr)


FFN_TF = 1024
FFN_RT = 512
FFN_WT = 256
FFN_WBUF = 3


def _select_kernel(aff_ref, tri_ref, pos_ref):
    aff = aff_ref[...]
    bits = pltpu.bitcast(aff, jnp.int32)
    rows = aff.shape[0]

    def count_ge(v):
        return jnp.sum((bits >= v).astype(jnp.int32), axis=1, keepdims=True)

    def body(t, cur):
        sh = 28 - 2 * t
        c1, c2, c3 = cur | (jnp.int32(1) << sh), cur | (jnp.int32(2) << sh), cur | (jnp.int32(3) << sh)
        n1, n2, n3 = count_ge(c1), count_ge(c2), count_ge(c3)
        return jnp.where(n3 >= CAP, c3, jnp.where(n2 >= CAP, c2, jnp.where(n1 >= CAP, c1, cur)))

    top = jnp.full((rows, 1), 1 << 30, jnp.int32)
    thr = lax.fori_loop(0, 15, body, jnp.where(count_ge(top) >= CAP, top, 0))
    gt = bits > thr
    eq = bits == thr
    need = CAP - jnp.sum(gt.astype(jnp.int32), axis=1, keepdims=True)
    tri = tri_ref[...]

    def cumsum_excl(mask_f):
        off = jnp.zeros((mask_f.shape[0], 1), F32)
        outs = []
        for j in range(S // 128):
            mj = mask_f[:, j * 128:(j + 1) * 128]
            loc = jnp.dot(mj.astype(BF16), tri, preferred_element_type=F32)
            outs.append(loc - mj + off)
            off = off + loc[:, 127:128]
        return jnp.concatenate(outs, axis=1)

    counts = cumsum_excl(jnp.concatenate([gt, eq], axis=0).astype(F32))
    gt_before, eq_before = counts[:rows], counts[rows:]
    needf = need.astype(F32)
    sel = gt | (eq & (eq_before < needf))
    pos = gt_before + jnp.minimum(eq_before, needf)
    pos_ref[...] = jnp.where(sel, pos.astype(jnp.int32), -1)


SC_CORES = 2
SC_SUBCORES = 16
SC_TILES = SC_CORES * SC_SUBCORES
DISPATCH_SLOTS = CAP * E // SC_TILES
DISPATCH_ROWS = 64
COMBINE_RANGE = 1024
COMBINE_ROWS = 32
SLAB = 128
NSLAB = D // SLAB


def _sc_mesh():
    return plsc.VectorSubcoreMesh(core_axis_name="c", subcore_axis_name="s",
                                  num_cores=SC_CORES, num_subcores=SC_SUBCORES)


def _dispatch_body(pos_hbm, aff_hbm, h_hbm, xin_hbm, idx_hbm, gate_hbm, pos_v, aff_v, idx_v, gate_v,
                   rows_a, rows_b, gsem_a, gsem_b, wsem_a, wsem_b):
    w = lax.axis_index("s") * SC_CORES + lax.axis_index("c")
    e = w // 2
    lo = (w % 2) * DISPATCH_SLOTS
    pltpu.sync_copy(pos_hbm.at[e], pos_v)
    pltpu.sync_copy(aff_hbm.at[e], aff_v)

    @pl.loop(0, S // 16)
    def _(i):
        p = pos_v[pl.ds(i * 16, 16)] - lo
        m = (p >= 0) & (p < DISPATCH_SLOTS)
        tok = lax.iota(jnp.int32, 16) + i * 16
        plsc.store_scatter(idx_v, [p], tok, mask=m)
        plsc.store_scatter(gate_v, [p], aff_v[pl.ds(i * 16, 16)], mask=m)

    pltpu.sync_copy(idx_v, idx_hbm.at[e, pl.ds(lo, DISPATCH_SLOTS)])
    pltpu.sync_copy(gate_v, gate_hbm.at[e, pl.ds(lo, DISPATCH_SLOTS)])

    bufs, gsems, wsems = (rows_a, rows_b), (gsem_a, gsem_b), (wsem_a, wsem_b)
    nchunk = DISPATCH_SLOTS // DISPATCH_ROWS

    def gather(j):
        return pltpu.async_copy(h_hbm.at[idx_v.at[pl.ds(j * DISPATCH_ROWS, DISPATCH_ROWS)]], bufs[j % 2], gsems[j % 2])

    pending_gather = gather(0)
    writes = [None, None]
    for j in range(nchunk):
        pending_gather.wait()
        writes[j % 2] = pltpu.async_copy(
            bufs[j % 2], xin_hbm.at[e, pl.ds(lo + j * DISPATCH_ROWS, DISPATCH_ROWS)], wsems[j % 2])
        if j + 1 < nchunk:
            if writes[(j + 1) % 2] is not None:
                writes[(j + 1) % 2].wait()
            pending_gather = gather(j + 1)
    writes[(nchunk - 2) % 2].wait()
    writes[(nchunk - 1) % 2].wait()


def _dispatch(pos, aff, h):
    return pl.kernel(
        _dispatch_body, mesh=_sc_mesh(),
        out_type=[jax.ShapeDtypeStruct((E, CAP, D // 2), jnp.int32),
                  jax.ShapeDtypeStruct((E, CAP), jnp.int32),
                  jax.ShapeDtypeStruct((E, CAP), F32)],
        scratch_types=[pltpu.VMEM((S,), jnp.int32), pltpu.VMEM((S,), F32),
                       pltpu.VMEM((DISPATCH_SLOTS,), jnp.int32), pltpu.VMEM((DISPATCH_SLOTS,), F32),
                       pltpu.VMEM((DISPATCH_ROWS, D // 2), jnp.int32), pltpu.VMEM((DISPATCH_ROWS, D // 2), jnp.int32),
                       pltpu.SemaphoreType.DMA, pltpu.SemaphoreType.DMA,
                       pltpu.SemaphoreType.DMA, pltpu.SemaphoreType.DMA],
        compiler_params=pltpu.CompilerParams(needs_layout_passes=False),
        name="moe_dispatch",
    )(pos, aff, h)


def _combine_ranges(seed, y_hbm, idx_hbm, out_hbm, idx_v, li_v, bufs):
    rows = bufs[:NSLAB]
    accs = bufs[NSLAB:]
    c = lax.axis_index("c")
    s = lax.axis_index("s")
    share = COMBINE_RANGE // SC_SUBCORES
    pltpu.sync_copy(idx_hbm.at[s], idx_v)
    lane = lax.iota(jnp.int32, 16)

    @pl.loop(0, S // COMBINE_RANGE // SC_CORES)
    def _(r):
        t0 = (r * SC_CORES + c) * COMBINE_RANGE
        row0 = t0 + s * share
        seed(row0, tuple(accs[q].at[pl.ds(s * share, share)] for q in range(NSLAB)))
        plsc.subcore_barrier()

        @pl.loop(0, CAP // COMBINE_ROWS)
        def _(j):
            hits = jnp.zeros((16,), jnp.int32)
            for v in range(COMBINE_ROWS // 16):
                t = idx_v[pl.ds(j * COMBINE_ROWS + v * 16, 16)] - t0
                ok = (t >= 0) & (t < COMBINE_RANGE)
                li_v[pl.ds(v * 16, 16)] = jnp.where(ok, t, COMBINE_RANGE + lane)
                hits = hits + plsc.all_reduce_population_count(ok)

            @pl.when(jnp.max(hits) > 0)
            def _():
                pltpu.sync_copy(
                    tuple(y_hbm.at[s, pl.ds(j * COMBINE_ROWS, COMBINE_ROWS), pl.ds(q * SLAB, SLAB)]
                          for q in range(NSLAB)),
                    tuple(rows))
                pltpu.sync_copy(tuple(rows), tuple(accs[q].at[li_v] for q in range(NSLAB)), add=True)

        plsc.subcore_barrier()
        pltpu.sync_copy(tuple(accs[q].at[pl.ds(s * share, share)] for q in range(NSLAB)),
                        tuple(out_hbm.at[pl.ds(row0, share), pl.ds(q * SLAB, SLAB)] for q in range(NSLAB)))


def _combine_body(y_hbm, idx_hbm, x_hbm, out_hbm, idx_v, li_v, *bufs):
    share = COMBINE_RANGE // SC_SUBCORES

    def seed(row0, dst):
        pltpu.sync_copy(tuple(x_hbm.at[pl.ds(row0, share), pl.ds(q * SLAB, SLAB)] for q in range(NSLAB)), dst)

    _combine_ranges(seed, y_hbm, idx_hbm, out_hbm, idx_v, li_v, bufs)


def _moe_sum_body(y_hbm, idx_hbm, out_hbm, idx_v, li_v, zero_v, *bufs):
    @pl.loop(0, zero_v.shape[0])
    def _(r):
        for v in range(SLAB // 16):
            zero_v[r, pl.ds(v * 16, 16)] = jnp.zeros((16,), F32)

    _combine_ranges(lambda row0, dst: pltpu.sync_copy(tuple(zero_v for _ in range(NSLAB)), dst),
                    y_hbm, idx_hbm, out_hbm, idx_v, li_v, bufs)


def _combine_scratch():
    return ([pltpu.VMEM((CAP,), jnp.int32), pltpu.VMEM((COMBINE_ROWS,), jnp.int32)],
            [pltpu.VMEM((COMBINE_ROWS, SLAB), F32) for _ in range(NSLAB)]
            + [pltpu.VMEM_SHARED((COMBINE_RANGE + 16, SLAB), F32) for _ in range(NSLAB)])


def _combine(y, idx, x, out_rows=S):
    head, tail = _combine_scratch()
    return pl.kernel(
        _combine_body, mesh=_sc_mesh(),
        out_type=jax.ShapeDtypeStruct((out_rows, D), F32),
        scratch_types=head + tail,
        compiler_params=pltpu.CompilerParams(needs_layout_passes=False),
        name="moe_combine",
    )(y, idx, x)


def _moe_sum(y, idx):
    head, tail = _combine_scratch()
    return pl.kernel(
        _moe_sum_body, mesh=_sc_mesh(),
        out_type=jax.ShapeDtypeStruct((S, D), F32),
        scratch_types=head + [pltpu.VMEM((COMBINE_RANGE // SC_SUBCORES, SLAB), F32)] + tail,
        compiler_params=pltpu.CompilerParams(needs_layout_passes=False),
        name="moe_sum",
    )(y, idx)


FINAL_TM = 1024


def _final_add_kernel(big_ref, x_ref, m_ref, o_ref):
    del big_ref
    o_ref[...] = x_ref[...] + m_ref[...]


def _final_add(big, x, moe, b):
    nb = S // FINAL_TM
    return pl.pallas_call(
        _final_add_kernel,
        grid=(nb,),
        in_specs=[
            pl.BlockSpec(memory_space=pl.ANY),
            pl.BlockSpec((FINAL_TM, D), lambda i: (i, 0)),
            pl.BlockSpec((FINAL_TM, D), lambda i: (i, 0)),
        ],
        out_specs=pl.BlockSpec((FINAL_TM, D), lambda i: (b * nb + i, 0)),
        out_shape=jax.ShapeDtypeStruct((B * S, D), F32),
        input_output_aliases={0: 0},
        compiler_params=_cparams(("arbitrary",)),
        name="final_add",
    )(big, x, moe)


def _ffn_step(gate_ref, mod_ref, indices, x_ref, wg_ref, wu_ref, wd_ref, o_ref):
    e, f = indices
    last = F // FFN_TF - 1
    nt = FFN_TF // FFN_WT
    wg = [wg_ref[0, 0, :, j * FFN_WT:(j + 1) * FFN_WT].astype(BF16) for j in range(nt)]
    wu = [wu_ref[0, 0, :, j * FFN_WT:(j + 1) * FFN_WT].astype(BF16) for j in range(nt)]
    wd = wd_ref[0, 0].astype(BF16)
    is_first = f == 0
    is_last = f == last
    g2 = jnp.where(is_last, mod_ref[0, 0][5:6], 1.0)
    g_row = gate_ref[pl.ds(e, 1), :]
    g_col = jnp.broadcast_to(g_row, (128, CAP)).T[:, 0:1]
    for r in range(CAP // FFN_RT):
        rows = slice(r * FFN_RT, (r + 1) * FFN_RT)
        xr = _unpack_bf16_pairs(x_ref[0, rows, :])
        acts = []
        for j in range(nt):
            g = jnp.dot(xr, wg[j], preferred_element_type=F32)
            u = jnp.dot(xr, wu[j], preferred_element_type=F32)
            acts.append((g * _sigmoid(g) * u).astype(BF16))
        y = jnp.dot(jnp.concatenate(acts, axis=1), wd, preferred_element_type=F32)
        prev = jnp.where(is_first, 0.0, o_ref[0, rows, :])
        gate = jnp.where(is_last, g_col[rows, :], 1.0)
        o_ref[0, rows, :] = (prev + y) * gate * g2


def _select(aff, tri):
    return pl.pallas_call(
        _select_kernel,
        grid=(1,),
        in_specs=[
            pl.BlockSpec((E, S), lambda i: (0, 0)),
            pl.BlockSpec((128, 128), lambda i: (0, 0)),
        ],
        out_specs=pl.BlockSpec((E, S), lambda i: (0, 0)),
        out_shape=jax.ShapeDtypeStruct((E, S), jnp.int32),
        compiler_params=_cparams(("arbitrary",)),
        name="moe_select",
    )(aff, tri)


def _ffn_kernel(layer, x_hbm, wg_hbm, wu_hbm, wd_hbm, gate_ref, mod_ref, o_hbm):
    deep = pl.Buffered(FFN_WBUF)
    pltpu.emit_pipeline(
        functools.partial(_ffn_step, gate_ref, mod_ref),
        grid=(E, F // FFN_TF),
        in_specs=[
            pl.BlockSpec((1, CAP, D // 2), lambda e, f: (e, 0, 0)),
            pl.BlockSpec((1, 1, D, FFN_TF), lambda e, f: (layer, e, 0, f), pipeline_mode=deep),
            pl.BlockSpec((1, 1, D, FFN_TF), lambda e, f: (layer, e, 0, f), pipeline_mode=deep),
            pl.BlockSpec((1, 1, FFN_TF, D), lambda e, f: (layer, e, f, 0), pipeline_mode=deep),
        ],
        out_specs=[pl.BlockSpec((1, CAP, D), lambda e, f: (e, 0, 0))],
        _explicit_indices=True,
    )(x_hbm, wg_hbm, wu_hbm, wd_hbm, o_hbm)


def _ffn(xin, gate, b, mod_all, w_gate, w_up, w_down, layer):
    hbm = pl.BlockSpec(memory_space=pl.ANY)
    return pl.pallas_call(
        functools.partial(_ffn_kernel, layer),
        grid=(1,),
        in_specs=[
            hbm, hbm, hbm, hbm,
            pl.BlockSpec((E, CAP), lambda i: (0, 0)),
            pl.BlockSpec((1, 1, 6, D), lambda i: (layer, b, 0, 0)),
        ],
        out_specs=hbm,
        out_shape=jax.ShapeDtypeStruct((E, CAP, D), F32),
        compiler_params=pltpu.CompilerParams(vmem_limit_bytes=58 * 1024 * 1024),
        name="moe_ffn",
    )(xin, w_gate, w_up, w_down, gate, mod_all)


def kernel(x, c, w_ada, b_ada, norm_mix, norm_ffn, w_fourier_out, w_qkv, w_attn_out, q_gain, k_gain,
           sink, rel_bias, w_router, w_gate, w_up, w_down):
    m0, m2 = _dft_tables()
    m1 = _stage1_table()
    bucket_t = _bucket_table()
    tri = jnp.asarray(np.triu(np.ones((128, 128), np.float32)), BF16)
    mod_all = _ada(c, w_ada, b_ada)
    bias_tab = _bias_table(rel_bias, bucket_t)
    gain_mix = norm_mix.reshape(DEPTH, 1, D)
    gain_ffn = norm_ffn.reshape(DEPTH, 1, D)
    xs = [x, x]
    for layer in range(DEPTH):
        j = layer // 2
        wr = jnp.pad(w_router[layer], ((0, 0), (0, EP - E)))
        wr1 = wr.astype(BF16)
        wr = jnp.concatenate([wr1, (wr - wr1.astype(F32)).astype(BF16)], axis=1)
        if layer % 2 == 0:
            w_out_bf = w_fourier_out[j].astype(BF16)
            routed = [_fourier_layer(xs[b], b, mod_all, gain_mix, gain_ffn, wr, w_out_bf, layer, m0, m1, m2)
                      for b in range(B)]
        else:
            aw = _attn_weights(w_qkv[j], q_gain[j], k_gain[j], sink[j])
            w_out_bf = w_attn_out[j].astype(BF16)
            routed = [_attn_layer(xs[b], b, mod_all, gain_mix, gain_ffn, wr, aw, w_out_bf, bias_tab, layer)
                      for b in range(B)]
        xs = [routed[b][0] for b in range(B)]
        pos = [_select(routed[b][2], tri) for b in range(B)]
        disp = [_dispatch(pos[b], routed[b][2], routed[b][1]) for b in range(B)]
        ys = [_ffn(disp[b][0], disp[b][2], b, mod_all, w_gate, w_up, w_down, layer) for b in range(B)]
        if layer < DEPTH - 1:
            xs = [_combine(ys[b], disp[b][1], xs[b]) for b in range(B)]
    out = _combine(ys[0], disp[0][1], xs[0], out_rows=B * S)
    for b in range(1, B):
        out = _final_add(out, xs[b], _moe_sum(ys[b], disp[b][1]), b)
    return out.reshape(B, S, D)
```

```python
import functools
import math

import numpy as np
import jax
import jax.numpy as jnp
from jax import lax
from jax.experimental import pallas as pl
from jax.experimental.pallas import tpu as pltpu
from jax.experimental.pallas import tpu_sc as plsc

D = 1024
B = 2
S = 8192
DEPTH = 4
GROUPS = 4
GD = D // GROUPS
HD = 64
NH = 16
NKV = 4
GQA = NH // NKV
WINDOW = 128
BLK = 128
NBUCKETS = 32
MAXDIST = 128
E = 16
CAP = 2 * S // E
F = 2 * D
EPS = 1e-6
NEG_INF = -1e30

N1 = 128
N2 = 64
INV_NORM = 1.0 / math.sqrt(S * GD)

F32 = jnp.float32
BF16 = jnp.bfloat16


def _cparams(sem, vmem_mb=48):
    return pltpu.CompilerParams(dimension_semantics=sem, vmem_limit_bytes=vmem_mb * 1024 * 1024)


def _dft_tables():
    c = np.arange(GD)
    ang0 = 2.0 * np.pi * ((c[:, None] * c[None, :]) % GD) / GD
    m0 = np.concatenate([np.cos(ang0), -np.sin(ang0)], axis=1)
    k2 = np.arange(N2)
    ang2 = 2.0 * np.pi * ((k2[:, None] * k2[None, :]) % N2) / N2
    m2 = np.stack([np.cos(ang2), np.sin(ang2)], axis=2).reshape(N2, 2 * N2)
    return jnp.asarray(m0, BF16), jnp.asarray(m2, BF16)


def _stage1_table():
    s2 = np.arange(N2)[:, None, None]
    k1 = np.arange(N1)[None, :, None]
    s1 = np.arange(N1)[None, None, :]
    th = ((k1 * (N2 * s1 + s2)) % S) * (2.0 * np.pi / S)
    co, si = np.cos(th), np.sin(th)
    top = np.concatenate([co, si], axis=2)
    bot = np.concatenate([-si, co], axis=2)
    return jnp.asarray(np.concatenate([top, bot], axis=1).astype(np.float32), BF16)


def _bucket_table():
    q_off = np.arange(BLK)
    k_off = np.arange(3 * BLK) - BLK
    rel = k_off[:, None] - q_off[None, :]
    half = NBUCKETS // 2
    max_exact = half // 2
    ret = np.where(rel > 0, half, 0)
    n = np.abs(rel)
    nf = np.maximum(n, 1).astype(np.float32)
    ratio = (np.log(nf / np.float32(max_exact)) / np.float32(math.log(MAXDIST / max_exact))).astype(np.float32)
    large = max_exact + (ratio * np.float32(half - max_exact)).astype(np.int32)
    large = np.minimum(large, half - 1)
    bucket = ret + np.where(n < max_exact, n, large)
    return jnp.asarray(np.where(np.abs(rel) <= WINDOW, bucket, -1).astype(np.int32))


def _modulate(x, gain, shift, scale):
    ms = jnp.mean(x * x, axis=-1, keepdims=True)
    return x * lax.rsqrt(ms + EPS) * (gain * (1.0 + scale)) + shift


def _sigmoid(x):
    return 1.0 / (1.0 + jnp.exp(-x))


ADA_TN = 1536


def _ada_kernel(ct_ref, w_ref, b_ref, o_ref):
    ct = ct_ref[...]
    ca = ct * _sigmoid(ct)
    w = w_ref[0]
    for b in range(B):
        o_ref[0, b:b + 1, :] = jnp.sum(w * ca[:, b:b + 1], axis=0, keepdims=True) + b_ref[0]


def _ada(c, w_ada, b_ada):
    out = pl.pallas_call(
        _ada_kernel,
        grid=(DEPTH, 6 * D // ADA_TN),
        in_specs=[
            pl.BlockSpec((D, B), lambda l, j: (0, 0)),
            pl.BlockSpec((1, D, ADA_TN), lambda l, j: (l, 0, j)),
            pl.BlockSpec((1, 1, ADA_TN), lambda l, j: (l, 0, j)),
        ],
        out_specs=pl.BlockSpec((1, B, ADA_TN), lambda l, j: (l, 0, j)),
        out_shape=jax.ShapeDtypeStruct((DEPTH, B, 6 * D), F32),
        compiler_params=_cparams(("arbitrary", "arbitrary"), 32),
        name="ada_mod",
    )(c.T, w_ada, b_ada.reshape(DEPTH, 1, 6 * D))
    return out.reshape(DEPTH, B, 6, D)


def _mod_spec(layer, b):
    return pl.BlockSpec((1, 1, 6, D), lambda *_: (layer, b, 0, 0))


def _gain_spec(layer):
    return pl.BlockSpec((1, 1, D), lambda *_: (layer, 0, 0))


def _x_spec(x, b, tm):
    if x.ndim == 3:
        return pl.BlockSpec((None, tm, D), lambda i: (b, i, 0))
    return pl.BlockSpec((tm, D), lambda i: (i, 0))


EP = 128

_ROUTED_SHAPES = [jax.ShapeDtypeStruct((S, D), F32),
                  jax.ShapeDtypeStruct((S, D // 2), jnp.int32),
                  jax.ShapeDtypeStruct((E, S), F32)]


def _routed_specs(tm):
    return [pl.BlockSpec((tm, D), lambda i: (i, 0)),
            pl.BlockSpec((tm, D // 2), lambda i: (i, 0)),
            pl.BlockSpec((E, tm), lambda i: (0, i))]


STREAM_BUFS = 3


def _streamed_call(body, streamed, consts, *, grid, in_specs, const_specs, out_specs, out_shape, scratch_shapes=(),
                   name, with_index=False):
    n_in, n_c, n_out = len(in_specs), len(const_specs), len(out_specs)
    deep = [pl.BlockSpec(sp.block_shape, sp.index_map, pipeline_mode=pl.Buffered(STREAM_BUFS)) for sp in in_specs]

    def kernel(*refs):
        ins, cs = refs[:n_in], refs[n_in:n_in + n_c]
        outs, scratch = refs[n_in + n_c:n_in + n_c + n_out], refs[n_in + n_c + n_out:]
        def step(*blk):
            idx, blk = ((blk[0],), blk[1:]) if with_index else ((), blk)
            body(*idx, *blk[:n_in], *cs, *blk[n_in:], *scratch)

        pltpu.emit_pipeline(step, grid=grid, in_specs=deep, out_specs=list(out_specs),
                            _explicit_indices=with_index)(*ins, *outs)

    hbm = pl.BlockSpec(memory_space=pl.ANY)
    return pl.pallas_call(
        kernel,
        grid=(1,),
        in_specs=[hbm] * n_in + list(const_specs),
        out_specs=[hbm] * n_out,
        out_shape=list(out_shape),
        scratch_shapes=list(scratch_shapes),
        compiler_params=pltpu.CompilerParams(vmem_limit_bytes=56 * 1024 * 1024),
        name=name,
    )(*streamed, *consts)


F0_TM = 1024
SUB = 8
SEQ_PICKS = SUB


def _pack_bf16_pairs(h):
    hb = h.astype(BF16).astype(F32)
    lo = pltpu.bitcast(hb[:, :D // 2], jnp.int32)
    hi = pltpu.bitcast(hb[:, D // 2:], jnp.int32)
    return hi | lax.shift_right_logical(lo, jnp.int32(16))


def _unpack_bf16_pairs(xp):
    lo = pltpu.bitcast(xp << 16, F32).astype(BF16)
    hi = pltpu.bitcast(xp & jnp.int32(-65536), F32).astype(BF16)
    return jnp.concatenate([lo, hi], axis=1)


def _route(xn, m, gain_ffn, wr):
    h = _modulate(xn, gain_ffn, m[3:4], m[4:5])
    h1 = h.astype(BF16)
    h2 = (h - h1.astype(F32)).astype(BF16)
    part = jnp.dot(h1, wr, preferred_element_type=F32)
    logits = part[:, :EP] + part[:, EP:] + jnp.dot(h2, wr[:, :EP], preferred_element_type=F32)
    lt = logits.T[:E]
    ex = jnp.exp(lt - jnp.max(lt, axis=0, keepdims=True))
    return _pack_bf16_pairs(h), ex / jnp.sum(ex, axis=0, keepdims=True)


def _f0_kernel(x_ref, mod_ref, gain_ref, m0_ref, o_ref):
    m = mod_ref[0, 0]
    h = _modulate(x_ref[...], gain_ref[0], m[0:1], m[1:2]).astype(BF16)
    m0 = m0_ref[...]
    r = [jnp.dot(h[:, g * GD:(g + 1) * GD], m0, preferred_element_type=F32) for g in range(GROUPS)]
    o_ref[0] = _pack_bf16_pairs(jnp.concatenate([rg[:, :GD] for rg in r], axis=1))
    o_ref[1] = _pack_bf16_pairs(jnp.concatenate([rg[:, GD:] for rg in r], axis=1))


def _f1_kernel(w_ref, m1_ref, o_ref, scr_ref):
    n = o_ref.shape[0]
    for j in range(n):
        scr_ref[j] = w_ref[:, j, :]
    for j in range(n):
        w = _unpack_bf16_pairs(scr_ref[j])
        o_ref[j] = _pack_bf16_pairs(jnp.dot(m1_ref[j], w, preferred_element_type=F32))


def _f2_kernel(u_ref, m2_ref, o_ref, scr_ref):
    m2 = m2_ref[...]
    n = o_ref.shape[0]
    for k in range(n):
        scr_ref[k] = u_ref[:, k, :]
    for k in range(n):
        u = _unpack_bf16_pairs(scr_ref[k])
        o_ref[k] = _pack_bf16_pairs(jnp.dot(m2, u, preferred_element_type=F32))


def _f3_kernel(mp_ref, x_ref, w_ref, mod_ref, gf_ref, wr_ref, o_ref, h_ref, aff_ref, scr_ref):
    g1 = mod_ref[0, 0][2:3]
    for j in range(SUB):
        scr_ref[j * N1:(j + 1) * N1, :] = mp_ref[:, j, :]
    a = _unpack_bf16_pairs(scr_ref[...])
    y = jnp.dot(a, w_ref[...], preferred_element_type=F32)
    xn = x_ref[...] + (g1 * INV_NORM) * y
    o_ref[...] = xn
    h_ref[...], aff_ref[...] = _route(xn, mod_ref[0, 0], gf_ref[0], wr_ref[...])


def _fourier_layer(x, b, mod_all, gain, gain_ffn, wr, w_out_bf, layer, m0, m1, m2):
    nb0 = S // F0_TM
    x2 = x.reshape(-1, D)
    row0 = b * nb0 if x.ndim == 3 else 0
    (wc,) = _streamed_call(
        _f0_kernel, (x2,), (mod_all, gain, m0),
        grid=(nb0,),
        in_specs=[pl.BlockSpec((F0_TM, D), lambda i: (row0 + i, 0))],
        const_specs=[_mod_spec(layer, b), _gain_spec(layer), pl.BlockSpec((GD, 2 * GD), lambda i: (0, 0))],
        out_specs=[pl.BlockSpec((2, F0_TM, D // 2), lambda i: (0, i, 0))],
        out_shape=[jax.ShapeDtypeStruct((2, S, D // 2), jnp.int32)],
        name="fourier_chan",
    )
    wc = wc.reshape(2 * N1, N2, D // 2)
    (u,) = _streamed_call(
        _f1_kernel, (wc, m1), (),
        grid=(N2 // SEQ_PICKS,),
        in_specs=[pl.BlockSpec((2 * N1, SEQ_PICKS, D // 2), lambda i: (0, i, 0)),
                  pl.BlockSpec((SEQ_PICKS, 2 * N1, 2 * N1), lambda i: (i, 0, 0))],
        const_specs=[],
        out_specs=[pl.BlockSpec((SEQ_PICKS, 2 * N1, D // 2), lambda i: (i, 0, 0))],
        out_shape=[jax.ShapeDtypeStruct((N2, 2 * N1, D // 2), jnp.int32)],
        scratch_shapes=[pltpu.VMEM((SEQ_PICKS, 2 * N1, D // 2), jnp.int32)],
        name="fourier_seq1",
    )
    u = u.reshape(2 * N2, N1, D // 2)
    (mp,) = _streamed_call(
        _f2_kernel, (u,), (m2,),
        grid=(N1 // SEQ_PICKS,),
        in_specs=[pl.BlockSpec((2 * N2, SEQ_PICKS, D // 2), lambda i: (0, i, 0))],
        const_specs=[pl.BlockSpec((N2, 2 * N2), lambda i: (0, 0))],
        out_specs=[pl.BlockSpec((SEQ_PICKS, N2, D // 2), lambda i: (i, 0, 0))],
        out_shape=[jax.ShapeDtypeStruct((N1, N2, D // 2), jnp.int32)],
        scratch_shapes=[pltpu.VMEM((SEQ_PICKS, 2 * N2, D // 2), jnp.int32)],
        name="fourier_seq2",
    )
    nb3 = N2 // SUB
    row3 = b * nb3 if x.ndim == 3 else 0
    return _streamed_call(
        _f3_kernel, (mp, x2), (w_out_bf, mod_all, gain_ffn, wr),
        grid=(nb3,),
        in_specs=[pl.BlockSpec((N1, SUB, D // 2), lambda i: (0, i, 0)),
                  pl.BlockSpec((SUB * N1, D), lambda i: (row3 + i, 0))],
        const_specs=[pl.BlockSpec((D, D), lambda i: (0, 0)), _mod_spec(layer, b), _gain_spec(layer),
                     pl.BlockSpec((D, 2 * EP), lambda i: (0, 0))],
        out_specs=_routed_specs(SUB * N1),
        out_shape=_ROUTED_SHAPES,
        scratch_shapes=[pltpu.VMEM((SUB * N1, D // 2), jnp.int32)],
        name="fourier_out",
    )


QKV_TM = 1024
TQ = 1024
NSB = TQ // BLK
KWIN = TQ + 2 * BLK
NBLK = S // BLK
KPAD = 128
VROWS = HD + 16
ATT_AHEAD = 1
LOG2E = math.log2(math.e)


def _bias_kernel(rb_ref, bucket_ref, o_ref):
    h = pl.program_id(0)
    bucket = bucket_ref[...]
    acc = jnp.full(bucket.shape, NEG_INF, F32)
    for k in range(NBUCKETS):
        acc = jnp.where(bucket == k, rb_ref[k, h] * LOG2E, acc)
    o_ref[0] = acc


def _bias_table(rel_bias, bucket_t):
    return pl.pallas_call(
        _bias_kernel,
        grid=(NH,),
        in_specs=[
            pl.BlockSpec(memory_space=pltpu.SMEM),
            pl.BlockSpec((3 * BLK, BLK), lambda h: (0, 0)),
        ],
        out_specs=pl.BlockSpec((1, 3 * BLK, BLK), lambda h: (h // GQA, 0, h % GQA)),
        out_shape=jax.ShapeDtypeStruct((NKV, 3 * BLK, GQA * BLK), F32),
        compiler_params=_cparams(("arbitrary",)),
        name="rel_bias_table",
    )(rel_bias, bucket_t)


def _qkv_kernel(x_ref, mod_ref, gain_ref, wqt_ref, wk_ref, wvt_ref, qg_ref, kg_ref, qt_ref, k_ref, vt_ref):
    m = mod_ref[0, 0]
    h = _modulate(x_ref[...], gain_ref[0], m[0:1], m[1:2]).astype(BF16)
    nt = (((1,), (1,)), ((), ()))
    qt = lax.dot_general(wqt_ref[...], h, nt, preferred_element_type=F32)
    tm = qt.shape[1]
    q3 = qt.reshape(NH, HD, tm)
    q3 = q3 * lax.rsqrt(jnp.mean(q3 * q3, axis=1, keepdims=True) + EPS)
    qt_ref[...] = (q3.reshape(NH * HD, tm) * qg_ref[...]).astype(BF16)
    k = jnp.dot(h, wk_ref[...], preferred_element_type=F32)
    for g in range(NKV):
        kg = k[:, g * KPAD:(g + 1) * KPAD]
        ms = jnp.sum(kg * kg, axis=-1, keepdims=True) * (1.0 / HD)
        k_ref[:, g * KPAD:(g + 1) * KPAD] = (kg * lax.rsqrt(ms + EPS) * kg_ref[...]).astype(BF16)
    vt_ref[...] = lax.dot_general(wvt_ref[...], h, nt, preferred_element_type=F32).astype(BF16)


def _attn_kernel(idx, qt_ref, kp_ref, kc_ref, kn_ref, vp_ref, vc_ref, vn_ref, x_ref, w_ref, mod_ref,
                 bias_ref, sink_ref, gf_ref, wr_ref, o_ref, h_ref, aff_ref, att_ref, s_ref):
    i = idx[0]
    kwin = jnp.concatenate([kp_ref[...], kc_ref[...], kn_ref[...]], axis=0)
    vwin = jnp.concatenate([vp_ref[...], vc_ref[...], vn_ref[...]], axis=1)
    ones_rows = (lax.broadcasted_iota(jnp.int32, (VROWS - HD, KWIN), 0) == 0).astype(BF16)
    vaug = [jnp.concatenate([vwin[g * HD:(g + 1) * HD], ones_rows], axis=0) for g in range(NKV)]
    key_pos = i * TQ - BLK + lax.broadcasted_iota(jnp.int32, (KWIN, 1), 0)
    key_mask = jnp.where((key_pos >= 0) & (key_pos < S), 0.0, NEG_INF).astype(BF16)
    lane = lax.broadcasted_iota(jnp.int32, (1, NKV * KPAD), 1)
    kwin = jnp.where(lane % KPAD == HD, key_mask, kwin)
    q_ones = (lax.broadcasted_iota(jnp.int32, (KPAD - HD, GQA * BLK), 0) == 0).astype(BF16)
    items = [(g, r) for g in range(NKV) for r in range(NSB)]

    def scores(g, r):
        kr = kwin[r * BLK:r * BLK + 3 * BLK, g * KPAD:(g + 1) * KPAD]
        qg = jnp.concatenate(
            [qt_ref[(GQA * g + hh) * HD:(GQA * g + hh + 1) * HD, r * BLK:(r + 1) * BLK] for hh in range(GQA)],
            axis=1)
        qa = jnp.concatenate([qg, q_ones], axis=0)
        return jnp.dot(kr, qa, preferred_element_type=F32)

    def probs(s, g, r):
        sink = sink_ref[g] * LOG2E
        s = s + bias_ref[g]
        mx = jnp.maximum(jnp.max(s, axis=0, keepdims=True), sink)
        return jnp.exp2(s - mx).astype(BF16), jnp.exp2(sink - mx)

    nslot = ATT_AHEAD + 1
    for n in range(ATT_AHEAD):
        s_ref[n % nslot] = scores(*items[n])
    for n, (g, r) in enumerate(items):
        if n + ATT_AHEAD < len(items):
            s_ref[(n + ATT_AHEAD) % nslot] = scores(*items[n + ATT_AHEAD])
        p, psink = probs(s_ref[n % nslot], g, r)
        ot = jnp.dot(vaug[g][:, r * BLK:r * BLK + 3 * BLK], p, preferred_element_type=F32)
        ot = ot[:HD] * (1.0 / (ot[HD:HD + 1] + psink))
        for hh in range(GQA):
            hd0 = (GQA * g + hh) * HD
            att_ref[hd0:hd0 + HD, r * BLK:(r + 1) * BLK] = ot[:, hh * BLK:(hh + 1) * BLK]
    g1 = mod_ref[0, 0][2:3]
    att = att_ref[...].T.astype(BF16)
    y = jnp.dot(att, w_ref[...], preferred_element_type=F32)
    xn = x_ref[...] + g1 * y
    o_ref[...] = xn
    h_ref[...], aff_ref[...] = _route(xn, mod_ref[0, 0], gf_ref[0], wr_ref[...])


def _attn_weights(w_qkv, q_gain, k_gain, sink):
    wq, wk, wv = w_qkv[:, :NH * HD], w_qkv[:, NH * HD:(NH + NKV) * HD], w_qkv[:, (NH + NKV) * HD:]
    wqt = wq.T.astype(BF16)
    wvt = wv.T.astype(BF16)
    wk_pad = jnp.pad(wk.reshape(D, NKV, HD), ((0, 0), (0, 0), (0, KPAD - HD))).reshape(D, NKV * KPAD).astype(BF16)
    qg_col = jnp.tile(q_gain * (HD ** -0.5 * LOG2E), NH).reshape(NH * HD, 1)
    kg_row = jnp.pad(k_gain, (0, KPAD - HD)).reshape(1, KPAD)
    sink_row = jnp.repeat(sink, BLK).reshape(NKV, 1, GQA * BLK)
    return wqt, wk_pad, wvt, qg_col, kg_row, sink_row


def _attn_layer(x, b, mod_all, gain, gain_ffn, wr, aw, w_out_bf, bias_tab, layer):
    wqt, wk_pad, wvt, qg_col, kg_row, sink_row = aw
    qt, k, vt = _streamed_call(
        _qkv_kernel, (x,), (mod_all, gain, wqt, wk_pad, wvt, qg_col, kg_row),
        grid=(S // QKV_TM,),
        in_specs=[pl.BlockSpec((QKV_TM, D), lambda i: (i, 0))],
        const_specs=[
            _mod_spec(layer, b), _gain_spec(layer),
            pl.BlockSpec((NH * HD, D), lambda i: (0, 0)),
            pl.BlockSpec((D, NKV * KPAD), lambda i: (0, 0)),
            pl.BlockSpec((NKV * HD, D), lambda i: (0, 0)),
            pl.BlockSpec((NH * HD, 1), lambda i: (0, 0)),
            pl.BlockSpec((1, KPAD), lambda i: (0, 0)),
        ],
        out_specs=[
            pl.BlockSpec((NH * HD, QKV_TM), lambda i: (0, i)),
            pl.BlockSpec((QKV_TM, NKV * KPAD), lambda i: (i, 0)),
            pl.BlockSpec((NKV * HD, QKV_TM), lambda i: (0, i)),
        ],
        out_shape=[
            jax.ShapeDtypeStruct((NH * HD, S), BF16),
            jax.ShapeDtypeStruct((S, NKV * KPAD), BF16),
            jax.ShapeDtypeStruct((NKV * HD, S), BF16),
        ],
        name="attn_qkv",
    )

    kw = NKV * KPAD
    vw = NKV * HD
    kprev = pl.BlockSpec((BLK, kw), lambda i: (jnp.maximum(i * NSB - 1, 0), 0))
    kcur = pl.BlockSpec((TQ, kw), lambda i: (i, 0))
    knext = pl.BlockSpec((BLK, kw), lambda i: (jnp.minimum((i + 1) * NSB, NBLK - 1), 0))
    vprev = pl.BlockSpec((vw, BLK), lambda i: (0, jnp.maximum(i * NSB - 1, 0)))
    vcur = pl.BlockSpec((vw, TQ), lambda i: (0, i))
    vnext = pl.BlockSpec((vw, BLK), lambda i: (0, jnp.minimum((i + 1) * NSB, NBLK - 1)))
    return _streamed_call(
        _attn_kernel, (qt, k, k, k, vt, vt, vt, x), (w_out_bf, mod_all, bias_tab, sink_row, gain_ffn, wr),
        grid=(S // TQ,),
        in_specs=[pl.BlockSpec((NH * HD, TQ), lambda i: (0, i)),
                  kprev, kcur, knext, vprev, vcur, vnext,
                  pl.BlockSpec((TQ, D), lambda i: (i, 0))],
        const_specs=[
            pl.BlockSpec((NH * HD, D), lambda i: (0, 0)),
            _mod_spec(layer, b),
            pl.BlockSpec((NKV, 3 * BLK, GQA * BLK), lambda i: (0, 0, 0)),
            pl.BlockSpec((NKV, 1, GQA * BLK), lambda i: (0, 0, 0)),
            _gain_spec(layer),
            pl.BlockSpec((D, 2 * EP), lambda i: (0, 0)),
        ],
        out_specs=_routed_specs(TQ),
        out_shape=_ROUTED_SHAPES,
        scratch_shapes=[pltpu.VMEM((NH * HD, TQ), F32), pltpu.VMEM((ATT_AHEAD + 1, 3 * BLK, GQA * BLK), F32)],
        name="attn_core", with_index=True,
    )


FFN_TF = 1024
FFN_RT = 512
FFN_WT = 256
FFN_WBUF = 3


def _select_kernel(aff_ref, tri_ref, pos_ref):
    aff = aff_ref[...]
    bits = pltpu.bitcast(aff, jnp.int32)
    rows = aff.shape[0]

    def count_ge(v):
        return jnp.sum((bits >= v).astype(jnp.int32), axis=1, keepdims=True)

    def body(t, cur):
        sh = 28 - 2 * t
        c1, c2, c3 = cur | (jnp.int32(1) << sh), cur | (jnp.int32(2) << sh), cur | (jnp.int32(3) << sh)
        n1, n2, n3 = count_ge(c1), count_ge(c2), count_ge(c3)
        return jnp.where(n3 >= CAP, c3, jnp.where(n2 >= CAP, c2, jnp.where(n1 >= CAP, c1, cur)))

    top = jnp.full((rows, 1), 1 << 30, jnp.int32)
    thr = lax.fori_loop(0, 15, body, jnp.where(count_ge(top) >= CAP, top, 0))
    gt = bits > thr
    eq = bits == thr
    need = CAP - jnp.sum(gt.astype(jnp.int32), axis=1, keepdims=True)
    tri = tri_ref[...]

    def cumsum_excl(mask_f):
        off = jnp.zeros((mask_f.shape[0], 1), F32)
        outs = []
        for j in range(S // 128):
            mj = mask_f[:, j * 128:(j + 1) * 128]
            loc = jnp.dot(mj.astype(BF16), tri, preferred_element_type=F32)
            outs.append(loc - mj + off)
            off = off + loc[:, 127:128]
        return jnp.concatenate(outs, axis=1)

    counts = cumsum_excl(jnp.concatenate([gt, eq], axis=0).astype(F32))
    gt_before, eq_before = counts[:rows], counts[rows:]
    needf = need.astype(F32)
    sel = gt | (eq & (eq_before < needf))
    pos = gt_before + jnp.minimum(eq_before, needf)
    pos_ref[...] = jnp.where(sel, pos.astype(jnp.int32), -1)


SC_CORES = 2
SC_SUBCORES = 16
SC_TILES = SC_CORES * SC_SUBCORES
DISPATCH_SLOTS = CAP * E // SC_TILES
DISPATCH_ROWS = 64
COMBINE_RANGE = 1024
COMBINE_ROWS = 32
SLAB = 128
NSLAB = D // SLAB


def _sc_mesh():
    return plsc.VectorSubcoreMesh(core_axis_name="c", subcore_axis_name="s",
                                  num_cores=SC_CORES, num_subcores=SC_SUBCORES)


def _dispatch_body(pos_hbm, aff_hbm, h_hbm, xin_hbm, idx_hbm, gate_hbm, pos_v, aff_v, idx_v, gate_v,
                   rows_a, rows_b, gsem_a, gsem_b, wsem_a, wsem_b):
    w = lax.axis_index("s") * SC_CORES + lax.axis_index("c")
    e = w // 2
    lo = (w % 2) * DISPATCH_SLOTS
    pltpu.sync_copy(pos_hbm.at[e], pos_v)
    pltpu.sync_copy(aff_hbm.at[e], aff_v)

    @pl.loop(0, S // 16)
    def _(i):
        p = pos_v[pl.ds(i * 16, 16)] - lo
        m = (p >= 0) & (p < DISPATCH_SLOTS)
        tok = lax.iota(jnp.int32, 16) + i * 16
        plsc.store_scatter(idx_v, [p], tok, mask=m)
        plsc.store_scatter(gate_v, [p], aff_v[pl.ds(i * 16, 16)], mask=m)

    pltpu.sync_copy(idx_v, idx_hbm.at[e, pl.ds(lo, DISPATCH_SLOTS)])
    pltpu.sync_copy(gate_v, gate_hbm.at[e, pl.ds(lo, DISPATCH_SLOTS)])

    bufs, gsems, wsems = (rows_a, rows_b), (gsem_a, gsem_b), (wsem_a, wsem_b)
    nchunk = DISPATCH_SLOTS // DISPATCH_ROWS

    def gather(j):
        return pltpu.async_copy(h_hbm.at[idx_v.at[pl.ds(j * DISPATCH_ROWS, DISPATCH_ROWS)]], bufs[j % 2], gsems[j % 2])

    pending_gather = gather(0)
    writes = [None, None]
    for j in range(nchunk):
        pending_gather.wait()
        writes[j % 2] = pltpu.async_copy(
            bufs[j % 2], xin_hbm.at[e, pl.ds(lo + j * DISPATCH_ROWS, DISPATCH_ROWS)], wsems[j % 2])
        if j + 1 < nchunk:
            if writes[(j + 1) % 2] is not None:
                writes[(j + 1) % 2].wait()
            pending_gather = gather(j + 1)
    writes[(nchunk - 2) % 2].wait()
    writes[(nchunk - 1) % 2].wait()


def _dispatch(pos, aff, h):
    return pl.kernel(
        _dispatch_body, mesh=_sc_mesh(),
        out_type=[jax.ShapeDtypeStruct((E, CAP, D // 2), jnp.int32),
                  jax.ShapeDtypeStruct((E, CAP), jnp.int32),
                  jax.ShapeDtypeStruct((E, CAP), F32)],
        scratch_types=[pltpu.VMEM((S,), jnp.int32), pltpu.VMEM((S,), F32),
                       pltpu.VMEM((DISPATCH_SLOTS,), jnp.int32), pltpu.VMEM((DISPATCH_SLOTS,), F32),
                       pltpu.VMEM((DISPATCH_ROWS, D // 2), jnp.int32), pltpu.VMEM((DISPATCH_ROWS, D // 2), jnp.int32),
                       pltpu.SemaphoreType.DMA, pltpu.SemaphoreType.DMA,
                       pltpu.SemaphoreType.DMA, pltpu.SemaphoreType.DMA],
        compiler_params=pltpu.CompilerParams(needs_layout_passes=False),
        name="moe_dispatch",
    )(pos, aff, h)


def _combine_ranges(seed, y_hbm, idx_hbm, out_hbm, idx_v, li_v, bufs):
    rows = bufs[:NSLAB]
    accs = bufs[NSLAB:]
    c = lax.axis_index("c")
    s = lax.axis_index("s")
    share = COMBINE_RANGE // SC_SUBCORES
    pltpu.sync_copy(idx_hbm.at[s], idx_v)
    lane = lax.iota(jnp.int32, 16)

    @pl.loop(0, S // COMBINE_RANGE // SC_CORES)
    def _(r):
        t0 = (r * SC_CORES + c) * COMBINE_RANGE
        row0 = t0 + s * share
        seed(row0, tuple(accs[q].at[pl.ds(s * share, share)] for q in range(NSLAB)))
        plsc.subcore_barrier()

        @pl.loop(0, CAP // COMBINE_ROWS)
        def _(j):
            hits = jnp.zeros((16,), jnp.int32)
            for v in range(COMBINE_ROWS // 16):
                t = idx_v[pl.ds(j * COMBINE_ROWS + v * 16, 16)] - t0
                ok = (t >= 0) & (t < COMBINE_RANGE)
                li_v[pl.ds(v * 16, 16)] = jnp.where(ok, t, COMBINE_RANGE + lane)
                hits = hits + plsc.all_reduce_population_count(ok)

            @pl.when(jnp.max(hits) > 0)
            def _():
                pltpu.sync_copy(
                    tuple(y_hbm.at[s, pl.ds(j * COMBINE_ROWS, COMBINE_ROWS), pl.ds(q * SLAB, SLAB)]
                          for q in range(NSLAB)),
                    tuple(rows))
                pltpu.sync_copy(tuple(rows), tuple(accs[q].at[li_v] for q in range(NSLAB)), add=True)

        plsc.subcore_barrier()
        pltpu.sync_copy(tuple(accs[q].at[pl.ds(s * share, share)] for q in range(NSLAB)),
                        tuple(out_hbm.at[pl.ds(row0, share), pl.ds(q * SLAB, SLAB)] for q in range(NSLAB)))


def _combine_body(y_hbm, idx_hbm, x_hbm, out_hbm, idx_v, li_v, *bufs):
    share = COMBINE_RANGE // SC_SUBCORES

    def seed(row0, dst):
        pltpu.sync_copy(tuple(x_hbm.at[pl.ds(row0, share), pl.ds(q * SLAB, SLAB)] for q in range(NSLAB)), dst)

    _combine_ranges(seed, y_hbm, idx_hbm, out_hbm, idx_v, li_v, bufs)


def _moe_sum_body(y_hbm, idx_hbm, out_hbm, idx_v, li_v, zero_v, *bufs):
    @pl.loop(0, zero_v.shape[0])
    def _(r):
        for v in range(SLAB // 16):
            zero_v[r, pl.ds(v * 16, 16)] = jnp.zeros((16,), F32)

    _combine_ranges(lambda row0, dst: pltpu.sync_copy(tuple(zero_v for _ in range(NSLAB)), dst),
                    y_hbm, idx_hbm, out_hbm, idx_v, li_v, bufs)


def _combine_scratch():
    return ([pltpu.VMEM((CAP,), jnp.int32), pltpu.VMEM((COMBINE_ROWS,), jnp.int32)],
            [pltpu.VMEM((COMBINE_ROWS, SLAB), F32) for _ in range(NSLAB)]
            + [pltpu.VMEM_SHARED((COMBINE_RANGE + 16, SLAB), F32) for _ in range(NSLAB)])


def _combine(y, idx, x, out_rows=S):
    head, tail = _combine_scratch()
    return pl.kernel(
        _combine_body, mesh=_sc_mesh(),
        out_type=jax.ShapeDtypeStruct((out_rows, D), F32),
        scratch_types=head + tail,
        compiler_params=pltpu.CompilerParams(needs_layout_passes=False),
        name="moe_combine",
    )(y, idx, x)


def _moe_sum(y, idx):
    head, tail = _combine_scratch()
    return pl.kernel(
        _moe_sum_body, mesh=_sc_mesh(),
        out_type=jax.ShapeDtypeStruct((S, D), F32),
        scratch_types=head + [pltpu.VMEM((COMBINE_RANGE // SC_SUBCORES, SLAB), F32)] + tail,
        compiler_params=pltpu.CompilerParams(needs_layout_passes=False),
        name="moe_sum",
    )(y, idx)


FINAL_TM = 1024


def _final_add_kernel(big_ref, x_ref, m_ref, o_ref):
    del big_ref
    o_ref[...] = x_ref[...] + m_ref[...]


def _final_add(big, x, moe, b):
    nb = S // FINAL_TM
    return pl.pallas_call(
        _final_add_kernel,
        grid=(nb,),
        in_specs=[
            pl.BlockSpec(memory_space=pl.ANY),
            pl.BlockSpec((FINAL_TM, D), lambda i: (i, 0)),
            pl.BlockSpec((FINAL_TM, D), lambda i: (i, 0)),
        ],
        out_specs=pl.BlockSpec((FINAL_TM, D), lambda i: (b * nb + i, 0)),
        out_shape=jax.ShapeDtypeStruct((B * S, D), F32),
        input_output_aliases={0: 0},
        compiler_params=_cparams(("arbitrary",)),
        name="final_add",
    )(big, x, moe)


def _ffn_step(gate_ref, mod_ref, indices, x_ref, wg_ref, wu_ref, wd_ref, o_ref):
    e, f = indices
    last = F // FFN_TF - 1
    nt = FFN_TF // FFN_WT
    wg = [wg_ref[0, 0, :, j * FFN_WT:(j + 1) * FFN_WT].astype(BF16) for j in range(nt)]
    wu = [wu_ref[0, 0, :, j * FFN_WT:(j + 1) * FFN_WT].astype(BF16) for j in range(nt)]
    wd = wd_ref[0, 0].astype(BF16)
    is_first = f == 0
    is_last = f == last
    g2 = jnp.where(is_last, mod_ref[0, 0][5:6], 1.0)
    g_row = gate_ref[pl.ds(e, 1), :]
    g_col = jnp.broadcast_to(g_row, (128, CAP)).T[:, 0:1]
    for r in range(CAP // FFN_RT):
        rows = slice(r * FFN_RT, (r + 1) * FFN_RT)
        xr = _unpack_bf16_pairs(x_ref[0, rows, :])
        acts = []
        for j in range(nt):
            g = jnp.dot(xr, wg[j], preferred_element_type=F32)
            u = jnp.dot(xr, wu[j], preferred_element_type=F32)
            acts.append((g * _sigmoid(g) * u).astype(BF16))
        y = jnp.dot(jnp.concatenate(acts, axis=1), wd, preferred_element_type=F32)
        prev = jnp.where(is_first, 0.0, o_ref[0, rows, :])
        gate = jnp.where(is_last, g_col[rows, :], 1.0)
        o_ref[0, rows, :] = (prev + y) * gate * g2


def _select(aff, tri):
    return pl.pallas_call(
        _select_kernel,
        grid=(1,),
        in_specs=[
            pl.BlockSpec((E, S), lambda i: (0, 0)),
            pl.BlockSpec((128, 128), lambda i: (0, 0)),
        ],
        out_specs=pl.BlockSpec((E, S), lambda i: (0, 0)),
        out_shape=jax.ShapeDtypeStruct((E, S), jnp.int32),
        compiler_params=_cparams(("arbitrary",)),
        name="moe_select",
    )(aff, tri)


def _ffn_kernel(layer, x_hbm, wg_hbm, wu_hbm, wd_hbm, gate_ref, mod_ref, o_hbm):
    deep = pl.Buffered(FFN_WBUF)
    pltpu.emit_pipeline(
        functools.partial(_ffn_step, gate_ref, mod_ref),
        grid=(E, F // FFN_TF),
        in_specs=[
            pl.BlockSpec((1, CAP, D // 2), lambda e, f: (e, 0, 0)),
            pl.BlockSpec((1, 1, D, FFN_TF), lambda e, f: (layer, e, 0, f), pipeline_mode=deep),
            pl.BlockSpec((1, 1, D, FFN_TF), lambda e, f: (layer, e, 0, f), pipeline_mode=deep),
            pl.BlockSpec((1, 1, FFN_TF, D), lambda e, f: (layer, e, f, 0), pipeline_mode=deep),
        ],
        out_specs=[pl.BlockSpec((1, CAP, D), lambda e, f: (e, 0, 0))],
        _explicit_indices=True,
    )(x_hbm, wg_hbm, wu_hbm, wd_hbm, o_hbm)


def _ffn(xin, gate, b, mod_all, w_gate, w_up, w_down, layer):
    hbm = pl.BlockSpec(memory_space=pl.ANY)
    return pl.pallas_call(
        functools.partial(_ffn_kernel, layer),
        grid=(1,),
        in_specs=[
            hbm, hbm, hbm, hbm,
            pl.BlockSpec((E, CAP), lambda i: (0, 0)),
            pl.BlockSpec((1, 1, 6, D), lambda i: (layer, b, 0, 0)),
        ],
        out_specs=hbm,
        out_shape=jax.ShapeDtypeStruct((E, CAP, D), F32),
        compiler_params=pltpu.CompilerParams(vmem_limit_bytes=58 * 1024 * 1024),
        name="moe_ffn",
    )(xin, w_gate, w_up, w_down, gate, mod_all)


def kernel(x, c, w_ada, b_ada, norm_mix, norm_ffn, w_fourier_out, w_qkv, w_attn_out, q_gain, k_gain,
           sink, rel_bias, w_router, w_gate, w_up, w_down):
    m0, m2 = _dft_tables()
    m1 = _stage1_table()
    bucket_t = _bucket_table()
    tri = jnp.asarray(np.triu(np.ones((128, 128), np.float32)), BF16)
    mod_all = _ada(c, w_ada, b_ada)
    bias_tab = _bias_table(rel_bias, bucket_t)
    gain_mix = norm_mix.reshape(DEPTH, 1, D)
    gain_ffn = norm_ffn.reshape(DEPTH, 1, D)
    xs = [x, x]
    for layer in range(DEPTH):
        j = layer // 2
        wr = jnp.pad(w_router[layer], ((0, 0), (0, EP - E)))
        wr1 = wr.astype(BF16)
        wr = jnp.concatenate([wr1, (wr - wr1.astype(F32)).astype(BF16)], axis=1)
        if layer % 2 == 0:
            w_out_bf = w_fourier_out[j].astype(BF16)
            routed = [_fourier_layer(xs[b], b, mod_all, gain_mix, gain_ffn, wr, w_out_bf, layer, m0, m1, m2)
                      for b in range(B)]
        else:
            aw = _attn_weights(w_qkv[j], q_gain[j], k_gain[j], sink[j])
            w_out_bf = w_attn_out[j].astype(BF16)
            routed = [_attn_layer(xs[b], b, mod_all, gain_mix, gain_ffn, wr, aw, w_out_bf, bias_tab, layer)
                      for b in range(B)]
        xs = [routed[b][0] for b in range(B)]
        pos = [_select(routed[b][2], tri) for b in range(B)]
        disp = [_dispatch(pos[b], routed[b][2], routed[b][1]) for b in range(B)]
        ys = [_ffn(disp[b][0], disp[b][2], b, mod_all, w_gate, w_up, w_down, layer) for b in range(B)]
        if layer < DEPTH - 1:
            xs = [_combine(ys[b], disp[b][1], xs[b]) for b in range(B)]
    out = _combine(ys[0], disp[0][1], xs[0], out_rows=B * S)
    for b in range(1, B):
        out = _final_add(out, xs[b], _moe_sum(ys[b], disp[b][1]), b)
    return out.reshape(B, S, D)
```

```python
import functools
import math

import numpy as np
import jax
import jax.numpy as jnp
from jax import lax
from jax.experimental import pallas as pl
from jax.experimental.pallas import tpu as pltpu
from jax.experimental.pallas import tpu_sc as plsc

D = 1024
B = 2
S = 8192
DEPTH = 4
GROUPS = 4
GD = D // GROUPS
HD = 64
NH = 16
NKV = 4
GQA = NH // NKV
WINDOW = 128
BLK = 128
NBUCKETS = 32
MAXDIST = 128
E = 16
CAP = 2 * S // E
F = 2 * D
EPS = 1e-6
NEG_INF = -1e30

N1 = 128
N2 = 64
INV_NORM = 1.0 / math.sqrt(S * GD)

F32 = jnp.float32
BF16 = jnp.bfloat16


def _cparams(sem, vmem_mb=48):
    return pltpu.CompilerParams(dimension_semantics=sem, vmem_limit_bytes=vmem_mb * 1024 * 1024)


def _dft_tables():
    c = np.arange(GD)
    ang0 = 2.0 * np.pi * ((c[:, None] * c[None, :]) % GD) / GD
    m0 = np.concatenate([np.cos(ang0), -np.sin(ang0)], axis=1)
    k2 = np.arange(N2)
    ang2 = 2.0 * np.pi * ((k2[:, None] * k2[None, :]) % N2) / N2
    m2 = np.stack([np.cos(ang2), np.sin(ang2)], axis=2).reshape(N2, 2 * N2)
    return jnp.asarray(m0, BF16), jnp.asarray(m2, BF16)


def _stage1_table():
    s2 = np.arange(N2)[:, None, None]
    k1 = np.arange(N1)[None, :, None]
    s1 = np.arange(N1)[None, None, :]
    th = ((k1 * (N2 * s1 + s2)) % S) * (2.0 * np.pi / S)
    co, si = np.cos(th), np.sin(th)
    top = np.concatenate([co, si], axis=2)
    bot = np.concatenate([-si, co], axis=2)
    return jnp.asarray(np.concatenate([top, bot], axis=1).astype(np.float32), BF16)


def _bucket_table():
    q_off = np.arange(BLK)
    k_off = np.arange(3 * BLK) - BLK
    rel = k_off[:, None] - q_off[None, :]
    half = NBUCKETS // 2
    max_exact = half // 2
    ret = np.where(rel > 0, half, 0)
    n = np.abs(rel)
    nf = np.maximum(n, 1).astype(np.float32)
    ratio = (np.log(nf / np.float32(max_exact)) / np.float32(math.log(MAXDIST / max_exact))).astype(np.float32)
    large = max_exact + (ratio * np.float32(half - max_exact)).astype(np.int32)
    large = np.minimum(large, half - 1)
    bucket = ret + np.where(n < max_exact, n, large)
    return jnp.asarray(np.where(np.abs(rel) <= WINDOW, bucket, -1).astype(np.int32))


def _modulate(x, gain, shift, scale):
    ms = jnp.mean(x * x, axis=-1, keepdims=True)
    return x * lax.rsqrt(ms + EPS) * (gain * (1.0 + scale)) + shift


def _sigmoid(x):
    return 1.0 / (1.0 + jnp.exp(-x))


ADA_TN = 1536


def _ada_kernel(w_ref, b_ref, ct_ref, o_ref):
    ct = ct_ref[...]
    ca = ct * _sigmoid(ct)
    w = w_ref[0]
    for b in range(B):
        o_ref[0, b:b + 1, :] = jnp.sum(w * ca[:, b:b + 1], axis=0, keepdims=True) + b_ref[0]


def _ada(c, w_ada, b_ada):
    (out,) = _streamed_call(
        _ada_kernel, (w_ada, b_ada.reshape(DEPTH, 1, 6 * D)), (c.T,),
        grid=(DEPTH, 6 * D // ADA_TN),
        in_specs=[pl.BlockSpec((1, D, ADA_TN), lambda l, j: (l, 0, j)),
                  pl.BlockSpec((1, 1, ADA_TN), lambda l, j: (l, 0, j))],
        const_specs=[pl.BlockSpec((D, B), lambda i: (0, 0))],
        out_specs=[pl.BlockSpec((1, B, ADA_TN), lambda l, j: (l, 0, j))],
        out_shape=[jax.ShapeDtypeStruct((DEPTH, B, 6 * D), F32)],
        name="ada_mod",
    )
    return out.reshape(DEPTH, B, 6, D)


def _mod_spec(layer, b):
    return pl.BlockSpec((1, 1, 6, D), lambda *_: (layer, b, 0, 0))


def _gain_spec(layer):
    return pl.BlockSpec((1, 1, D), lambda *_: (layer, 0, 0))


def _x_spec(x, b, tm):
    if x.ndim == 3:
        return pl.BlockSpec((None, tm, D), lambda i: (b, i, 0))
    return pl.BlockSpec((tm, D), lambda i: (i, 0))


EP = 128

_ROUTED_SHAPES = [jax.ShapeDtypeStruct((S, D), F32),
                  jax.ShapeDtypeStruct((S, D // 2), jnp.int32),
                  jax.ShapeDtypeStruct((E, S), F32)]


def _routed_specs(tm):
    return [pl.BlockSpec((tm, D), lambda i: (i, 0)),
            pl.BlockSpec((tm, D // 2), lambda i: (i, 0)),
            pl.BlockSpec((E, tm), lambda i: (0, i))]


STREAM_BUFS = 4


def _streamed_call(body, streamed, consts, *, grid, in_specs, const_specs, out_specs, out_shape, scratch_shapes=(),
                   name):
    n_in, n_c, n_out = len(in_specs), len(const_specs), len(out_specs)
    deep = [pl.BlockSpec(sp.block_shape, sp.index_map, pipeline_mode=pl.Buffered(STREAM_BUFS)) for sp in in_specs]

    def kernel(*refs):
        ins, cs = refs[:n_in], refs[n_in:n_in + n_c]
        outs, scratch = refs[n_in + n_c:n_in + n_c + n_out], refs[n_in + n_c + n_out:]
        pltpu.emit_pipeline(lambda *blk: body(*blk[:n_in], *cs, *blk[n_in:], *scratch),
                            grid=grid, in_specs=deep, out_specs=list(out_specs))(*ins, *outs)

    hbm = pl.BlockSpec(memory_space=pl.ANY)
    return pl.pallas_call(
        kernel,
        grid=(1,),
        in_specs=[hbm] * n_in + list(const_specs),
        out_specs=[hbm] * n_out,
        out_shape=list(out_shape),
        scratch_shapes=list(scratch_shapes),
        compiler_params=pltpu.CompilerParams(vmem_limit_bytes=56 * 1024 * 1024),
        name=name,
    )(*streamed, *consts)


F0_TM = 1024
SUB = 8
SEQ_PICKS = SUB


def _pack_bf16_pairs(h):
    hb = h.astype(BF16).astype(F32)
    lo = pltpu.bitcast(hb[:, :D // 2], jnp.int32)
    hi = pltpu.bitcast(hb[:, D // 2:], jnp.int32)
    return hi | lax.shift_right_logical(lo, jnp.int32(16))


def _unpack_bf16_pairs(xp):
    lo = pltpu.bitcast(xp << 16, F32).astype(BF16)
    hi = pltpu.bitcast(xp & jnp.int32(-65536), F32).astype(BF16)
    return jnp.concatenate([lo, hi], axis=1)


def _route(xn, m, gain_ffn, wr):
    h = _modulate(xn, gain_ffn, m[3:4], m[4:5])
    h1 = h.astype(BF16)
    h2 = (h - h1.astype(F32)).astype(BF16)
    part = jnp.dot(h1, wr, preferred_element_type=F32)
    logits = part[:, :EP] + part[:, EP:] + jnp.dot(h2, wr[:, :EP], preferred_element_type=F32)
    lt = logits.T[:E]
    ex = jnp.exp(lt - jnp.max(lt, axis=0, keepdims=True))
    return _pack_bf16_pairs(h), ex / jnp.sum(ex, axis=0, keepdims=True)


def _f0_kernel(x_ref, mod_ref, gain_ref, m0_ref, o_ref):
    m = mod_ref[0, 0]
    h = _modulate(x_ref[...], gain_ref[0], m[0:1], m[1:2]).astype(BF16)
    m0 = m0_ref[...]
    r = [jnp.dot(h[:, g * GD:(g + 1) * GD], m0, preferred_element_type=F32) for g in range(GROUPS)]
    o_ref[0] = _pack_bf16_pairs(jnp.concatenate([rg[:, :GD] for rg in r], axis=1))
    o_ref[1] = _pack_bf16_pairs(jnp.concatenate([rg[:, GD:] for rg in r], axis=1))


def _f1_kernel(w_ref, m1_ref, o_ref, scr_ref):
    n = o_ref.shape[0]
    for j in range(n):
        scr_ref[j] = w_ref[:, j, :]
    for j in range(n):
        w = _unpack_bf16_pairs(scr_ref[j])
        o_ref[j] = _pack_bf16_pairs(jnp.dot(m1_ref[j], w, preferred_element_type=F32))


def _f2_kernel(u_ref, m2_ref, o_ref, scr_ref):
    m2 = m2_ref[...]
    n = o_ref.shape[0]
    for k in range(n):
        scr_ref[k] = u_ref[:, k, :]
    for k in range(n):
        u = _unpack_bf16_pairs(scr_ref[k])
        o_ref[k] = _pack_bf16_pairs(jnp.dot(m2, u, preferred_element_type=F32))


def _f3_kernel(mp_ref, x_ref, w_ref, mod_ref, gf_ref, wr_ref, o_ref, h_ref, aff_ref, scr_ref):
    g1 = mod_ref[0, 0][2:3]
    for j in range(SUB):
        scr_ref[j * N1:(j + 1) * N1, :] = mp_ref[:, j, :]
    a = _unpack_bf16_pairs(scr_ref[...])
    y = jnp.dot(a, w_ref[...], preferred_element_type=F32)
    xn = x_ref[...] + (g1 * INV_NORM) * y
    o_ref[...] = xn
    h_ref[...], aff_ref[...] = _route(xn, mod_ref[0, 0], gf_ref[0], wr_ref[...])


def _fourier_layer(x, b, mod_all, gain, gain_ffn, wr, w_out_bf, layer, m0, m1, m2):
    nb0 = S // F0_TM
    x2 = x.reshape(-1, D)
    row0 = b * nb0 if x.ndim == 3 else 0
    (wc,) = _streamed_call(
        _f0_kernel, (x2,), (mod_all, gain, m0),
        grid=(nb0,),
        in_specs=[pl.BlockSpec((F0_TM, D), lambda i: (row0 + i, 0))],
        const_specs=[_mod_spec(layer, b), _gain_spec(layer), pl.BlockSpec((GD, 2 * GD), lambda i: (0, 0))],
        out_specs=[pl.BlockSpec((2, F0_TM, D // 2), lambda i: (0, i, 0))],
        out_shape=[jax.ShapeDtypeStruct((2, S, D // 2), jnp.int32)],
        name="fourier_chan",
    )
    wc = wc.reshape(2 * N1, N2, D // 2)
    (u,) = _streamed_call(
        _f1_kernel, (wc, m1), (),
        grid=(N2 // SEQ_PICKS,),
        in_specs=[pl.BlockSpec((2 * N1, SEQ_PICKS, D // 2), lambda i: (0, i, 0)),
                  pl.BlockSpec((SEQ_PICKS, 2 * N1, 2 * N1), lambda i: (i, 0, 0))],
        const_specs=[],
        out_specs=[pl.BlockSpec((SEQ_PICKS, 2 * N1, D // 2), lambda i: (i, 0, 0))],
        out_shape=[jax.ShapeDtypeStruct((N2, 2 * N1, D // 2), jnp.int32)],
        scratch_shapes=[pltpu.VMEM((SEQ_PICKS, 2 * N1, D // 2), jnp.int32)],
        name="fourier_seq1",
    )
    u = u.reshape(2 * N2, N1, D // 2)
    (mp,) = _streamed_call(
        _f2_kernel, (u,), (m2,),
        grid=(N1 // SEQ_PICKS,),
        in_specs=[pl.BlockSpec((2 * N2, SEQ_PICKS, D // 2), lambda i: (0, i, 0))],
        const_specs=[pl.BlockSpec((N2, 2 * N2), lambda i: (0, 0))],
        out_specs=[pl.BlockSpec((SEQ_PICKS, N2, D // 2), lambda i: (i, 0, 0))],
        out_shape=[jax.ShapeDtypeStruct((N1, N2, D // 2), jnp.int32)],
        scratch_shapes=[pltpu.VMEM((SEQ_PICKS, 2 * N2, D // 2), jnp.int32)],
        name="fourier_seq2",
    )
    nb3 = N2 // SUB
    row3 = b * nb3 if x.ndim == 3 else 0
    return _streamed_call(
        _f3_kernel, (mp, x2), (w_out_bf, mod_all, gain_ffn, wr),
        grid=(nb3,),
        in_specs=[pl.BlockSpec((N1, SUB, D // 2), lambda i: (0, i, 0)),
                  pl.BlockSpec((SUB * N1, D), lambda i: (row3 + i, 0))],
        const_specs=[pl.BlockSpec((D, D), lambda i: (0, 0)), _mod_spec(layer, b), _gain_spec(layer),
                     pl.BlockSpec((D, 2 * EP), lambda i: (0, 0))],
        out_specs=_routed_specs(SUB * N1),
        out_shape=_ROUTED_SHAPES,
        scratch_shapes=[pltpu.VMEM((SUB * N1, D // 2), jnp.int32)],
        name="fourier_out",
    )


QKV_TM = 1024
TQ = 1024
NSB = TQ // BLK
KWIN = TQ + 2 * BLK
NBLK = S // BLK
KPAD = 128
VROWS = HD + 16
ATT_AHEAD = 1
LOG2E = math.log2(math.e)


def _bias_kernel(rb_ref, bucket_ref, o_ref):
    h = pl.program_id(0)
    bucket = bucket_ref[...]
    acc = jnp.full(bucket.shape, NEG_INF, F32)
    for k in range(NBUCKETS):
        acc = jnp.where(bucket == k, rb_ref[k, h] * LOG2E, acc)
    o_ref[0] = acc


def _bias_table(rel_bias, bucket_t):
    return pl.pallas_call(
        _bias_kernel,
        grid=(NH,),
        in_specs=[
            pl.BlockSpec(memory_space=pltpu.SMEM),
            pl.BlockSpec((3 * BLK, BLK), lambda h: (0, 0)),
        ],
        out_specs=pl.BlockSpec((1, 3 * BLK, BLK), lambda h: (h // GQA, 0, h % GQA)),
        out_shape=jax.ShapeDtypeStruct((NKV, 3 * BLK, GQA * BLK), F32),
        compiler_params=_cparams(("arbitrary",)),
        name="rel_bias_table",
    )(rel_bias, bucket_t)


def _qkv_kernel(x_ref, mod_ref, gain_ref, wqt_ref, wk_ref, wvt_ref, qg_ref, kg_ref, qt_ref, k_ref, vt_ref):
    m = mod_ref[0, 0]
    h = _modulate(x_ref[...], gain_ref[0], m[0:1], m[1:2]).astype(BF16)
    nt = (((1,), (1,)), ((), ()))
    qt = lax.dot_general(wqt_ref[...], h, nt, preferred_element_type=F32)
    tm = qt.shape[1]
    q3 = qt.reshape(NH, HD, tm)
    q3 = q3 * lax.rsqrt(jnp.mean(q3 * q3, axis=1, keepdims=True) + EPS)
    qt_ref[...] = (q3.reshape(NH * HD, tm) * qg_ref[...]).astype(BF16)
    k = jnp.dot(h, wk_ref[...], preferred_element_type=F32)
    for g in range(NKV):
        kg = k[:, g * KPAD:(g + 1) * KPAD]
        ms = jnp.sum(kg * kg, axis=-1, keepdims=True) * (1.0 / HD)
        k_ref[:, g * KPAD:(g + 1) * KPAD] = (kg * lax.rsqrt(ms + EPS) * kg_ref[...]).astype(BF16)
    vt_ref[...] = lax.dot_general(wvt_ref[...], h, nt, preferred_element_type=F32).astype(BF16)


def _attn_kernel(qt_ref, kp_ref, kc_ref, kn_ref, vp_ref, vc_ref, vn_ref, x_ref, w_ref, mod_ref,
                 bias_ref, sink_ref, gf_ref, wr_ref, o_ref, h_ref, aff_ref, att_ref, s_ref):
    i = pl.program_id(0)
    kwin = jnp.concatenate([kp_ref[...], kc_ref[...], kn_ref[...]], axis=0)
    vwin = jnp.concatenate([vp_ref[...], vc_ref[...], vn_ref[...]], axis=1)
    ones_rows = (lax.broadcasted_iota(jnp.int32, (VROWS - HD, KWIN), 0) == 0).astype(BF16)
    vaug = [jnp.concatenate([vwin[g * HD:(g + 1) * HD], ones_rows], axis=0) for g in range(NKV)]
    key_pos = i * TQ - BLK + lax.broadcasted_iota(jnp.int32, (KWIN, 1), 0)
    key_mask = jnp.where((key_pos >= 0) & (key_pos < S), 0.0, NEG_INF).astype(BF16)
    lane = lax.broadcasted_iota(jnp.int32, (1, NKV * KPAD), 1)
    kwin = jnp.where(lane % KPAD == HD, key_mask, kwin)
    q_ones = (lax.broadcasted_iota(jnp.int32, (KPAD - HD, GQA * BLK), 0) == 0).astype(BF16)
    items = [(g, r) for g in range(NKV) for r in range(NSB)]

    def scores(g, r):
        kr = kwin[r * BLK:r * BLK + 3 * BLK, g * KPAD:(g + 1) * KPAD]
        qg = jnp.concatenate(
            [qt_ref[(GQA * g + hh) * HD:(GQA * g + hh + 1) * HD, r * BLK:(r + 1) * BLK] for hh in range(GQA)],
            axis=1)
        qa = jnp.concatenate([qg, q_ones], axis=0)
        return jnp.dot(kr, qa, preferred_element_type=F32)

    def probs(s, g, r):
        sink = sink_ref[g] * LOG2E
        s = s + bias_ref[g]
        mx = jnp.maximum(jnp.max(s, axis=0, keepdims=True), sink)
        return jnp.exp2(s - mx).astype(BF16), jnp.exp2(sink - mx)

    nslot = ATT_AHEAD + 1
    for n in range(ATT_AHEAD):
        s_ref[n % nslot] = scores(*items[n])
    for n, (g, r) in enumerate(items):
        if n + ATT_AHEAD < len(items):
            s_ref[(n + ATT_AHEAD) % nslot] = scores(*items[n + ATT_AHEAD])
        p, psink = probs(s_ref[n % nslot], g, r)
        ot = jnp.dot(vaug[g][:, r * BLK:r * BLK + 3 * BLK], p, preferred_element_type=F32)
        ot = ot[:HD] * (1.0 / (ot[HD:HD + 1] + psink))
        for hh in range(GQA):
            hd0 = (GQA * g + hh) * HD
            att_ref[hd0:hd0 + HD, r * BLK:(r + 1) * BLK] = ot[:, hh * BLK:(hh + 1) * BLK]
    g1 = mod_ref[0, 0][2:3]
    att = att_ref[...].T.astype(BF16)
    y = jnp.dot(att, w_ref[...], preferred_element_type=F32)
    xn = x_ref[...] + g1 * y
    o_ref[...] = xn
    h_ref[...], aff_ref[...] = _route(xn, mod_ref[0, 0], gf_ref[0], wr_ref[...])


def _attn_weights(w_qkv, q_gain, k_gain, sink):
    wq, wk, wv = w_qkv[:, :NH * HD], w_qkv[:, NH * HD:(NH + NKV) * HD], w_qkv[:, (NH + NKV) * HD:]
    wqt = wq.T.astype(BF16)
    wvt = wv.T.astype(BF16)
    wk_pad = jnp.pad(wk.reshape(D, NKV, HD), ((0, 0), (0, 0), (0, KPAD - HD))).reshape(D, NKV * KPAD).astype(BF16)
    qg_col = jnp.tile(q_gain * (HD ** -0.5 * LOG2E), NH).reshape(NH * HD, 1)
    kg_row = jnp.pad(k_gain, (0, KPAD - HD)).reshape(1, KPAD)
    sink_row = jnp.repeat(sink, BLK).reshape(NKV, 1, GQA * BLK)
    return wqt, wk_pad, wvt, qg_col, kg_row, sink_row


def _attn_layer(x, b, mod_all, gain, gain_ffn, wr, aw, w_out_bf, bias_tab, layer):
    wqt, wk_pad, wvt, qg_col, kg_row, sink_row = aw
    qt, k, vt = _streamed_call(
        _qkv_kernel, (x,), (mod_all, gain, wqt, wk_pad, wvt, qg_col, kg_row),
        grid=(S // QKV_TM,),
        in_specs=[pl.BlockSpec((QKV_TM, D), lambda i: (i, 0))],
        const_specs=[
            _mod_spec(layer, b), _gain_spec(layer),
            pl.BlockSpec((NH * HD, D), lambda i: (0, 0)),
            pl.BlockSpec((D, NKV * KPAD), lambda i: (0, 0)),
            pl.BlockSpec((NKV * HD, D), lambda i: (0, 0)),
            pl.BlockSpec((NH * HD, 1), lambda i: (0, 0)),
            pl.BlockSpec((1, KPAD), lambda i: (0, 0)),
        ],
        out_specs=[
            pl.BlockSpec((NH * HD, QKV_TM), lambda i: (0, i)),
            pl.BlockSpec((QKV_TM, NKV * KPAD), lambda i: (i, 0)),
            pl.BlockSpec((NKV * HD, QKV_TM), lambda i: (0, i)),
        ],
        out_shape=[
            jax.ShapeDtypeStruct((NH * HD, S), BF16),
            jax.ShapeDtypeStruct((S, NKV * KPAD), BF16),
            jax.ShapeDtypeStruct((NKV * HD, S), BF16),
        ],
        name="attn_qkv",
    )

    kw = NKV * KPAD
    vw = NKV * HD
    kprev = pl.BlockSpec((BLK, kw), lambda i: (jnp.maximum(i * NSB - 1, 0), 0))
    kcur = pl.BlockSpec((TQ, kw), lambda i: (i, 0))
    knext = pl.BlockSpec((BLK, kw), lambda i: (jnp.minimum((i + 1) * NSB, NBLK - 1), 0))
    vprev = pl.BlockSpec((vw, BLK), lambda i: (0, jnp.maximum(i * NSB - 1, 0)))
    vcur = pl.BlockSpec((vw, TQ), lambda i: (0, i))
    vnext = pl.BlockSpec((vw, BLK), lambda i: (0, jnp.minimum((i + 1) * NSB, NBLK - 1)))
    return pl.pallas_call(
        _attn_kernel,
        grid=(S // TQ,),
        in_specs=[
            pl.BlockSpec((NH * HD, TQ), lambda i: (0, i)),
            kprev, kcur, knext, vprev, vcur, vnext,
            _x_spec(x, b, TQ),
            pl.BlockSpec((NH * HD, D), lambda i: (0, 0)),
            _mod_spec(layer, b),
            pl.BlockSpec((NKV, 3 * BLK, GQA * BLK), lambda i: (0, 0, 0)),
            pl.BlockSpec((NKV, 1, GQA * BLK), lambda i: (0, 0, 0)),
            _gain_spec(layer),
            pl.BlockSpec((D, 2 * EP), lambda i: (0, 0)),
        ],
        out_specs=_routed_specs(TQ),
        out_shape=_ROUTED_SHAPES,
        scratch_shapes=[pltpu.VMEM((NH * HD, TQ), F32), pltpu.VMEM((ATT_AHEAD + 1, 3 * BLK, GQA * BLK), F32)],
        compiler_params=_cparams(("arbitrary",)),
        name="attn_core",
    )(qt, k, k, k, vt, vt, vt, x, w_out_bf, mod_all, bias_tab, sink_row, gain_ffn, wr)


FFN_TF = 1024
FFN_RT = 512
FFN_WT = 256
FFN_WBUF = 3


def _select_kernel(aff_ref, tri_ref, pos_ref):
    aff = aff_ref[...]
    bits = pltpu.bitcast(aff, jnp.int32)
    rows = aff.shape[0]

    def count_ge(v):
        return jnp.sum((bits >= v).astype(jnp.int32), axis=1, keepdims=True)

    def body(t, cur):
        sh = 28 - 2 * t
        c1, c2, c3 = cur | (jnp.int32(1) << sh), cur | (jnp.int32(2) << sh), cur | (jnp.int32(3) << sh)
        n1, n2, n3 = count_ge(c1), count_ge(c2), count_ge(c3)
        return jnp.where(n3 >= CAP, c3, jnp.where(n2 >= CAP, c2, jnp.where(n1 >= CAP, c1, cur)))

    top = jnp.full((rows, 1), 1 << 30, jnp.int32)
    thr = lax.fori_loop(0, 15, body, jnp.where(count_ge(top) >= CAP, top, 0))
    gt = bits > thr
    eq = bits == thr
    need = CAP - jnp.sum(gt.astype(jnp.int32), axis=1, keepdims=True)
    tri = tri_ref[...]

    def cumsum_excl(mask_f):
        off = jnp.zeros((mask_f.shape[0], 1), F32)
        outs = []
        for j in range(S // 128):
            mj = mask_f[:, j * 128:(j + 1) * 128]
            loc = jnp.dot(mj.astype(BF16), tri, preferred_element_type=F32)
            outs.append(loc - mj + off)
            off = off + loc[:, 127:128]
        return jnp.concatenate(outs, axis=1)

    counts = cumsum_excl(jnp.concatenate([gt, eq], axis=0).astype(F32))
    gt_before, eq_before = counts[:rows], counts[rows:]
    needf = need.astype(F32)
    sel = gt | (eq & (eq_before < needf))
    pos = gt_before + jnp.minimum(eq_before, needf)
    pos_ref[...] = jnp.where(sel, pos.astype(jnp.int32), -1)


SC_CORES = 2
SC_SUBCORES = 16
SC_TILES = SC_CORES * SC_SUBCORES
DISPATCH_SLOTS = CAP * E // SC_TILES
DISPATCH_ROWS = 64
COMBINE_RANGE = 1024
COMBINE_ROWS = 32
SLAB = 128
NSLAB = D // SLAB


def _sc_mesh():
    return plsc.VectorSubcoreMesh(core_axis_name="c", subcore_axis_name="s",
                                  num_cores=SC_CORES, num_subcores=SC_SUBCORES)


def _dispatch_body(pos_hbm, aff_hbm, h_hbm, xin_hbm, idx_hbm, gate_hbm, pos_v, aff_v, idx_v, gate_v,
                   rows_a, rows_b, gsem_a, gsem_b, wsem_a, wsem_b):
    w = lax.axis_index("s") * SC_CORES + lax.axis_index("c")
    e = w // 2
    lo = (w % 2) * DISPATCH_SLOTS
    pltpu.sync_copy(pos_hbm.at[e], pos_v)
    pltpu.sync_copy(aff_hbm.at[e], aff_v)

    @pl.loop(0, S // 16)
    def _(i):
        p = pos_v[pl.ds(i * 16, 16)] - lo
        m = (p >= 0) & (p < DISPATCH_SLOTS)
        tok = lax.iota(jnp.int32, 16) + i * 16
        plsc.store_scatter(idx_v, [p], tok, mask=m)
        plsc.store_scatter(gate_v, [p], aff_v[pl.ds(i * 16, 16)], mask=m)

    pltpu.sync_copy(idx_v, idx_hbm.at[e, pl.ds(lo, DISPATCH_SLOTS)])
    pltpu.sync_copy(gate_v, gate_hbm.at[e, pl.ds(lo, DISPATCH_SLOTS)])

    bufs, gsems, wsems = (rows_a, rows_b), (gsem_a, gsem_b), (wsem_a, wsem_b)
    nchunk = DISPATCH_SLOTS // DISPATCH_ROWS

    def gather(j):
        return pltpu.async_copy(h_hbm.at[idx_v.at[pl.ds(j * DISPATCH_ROWS, DISPATCH_ROWS)]], bufs[j % 2], gsems[j % 2])

    pending_gather = gather(0)
    writes = [None, None]
    for j in range(nchunk):
        pending_gather.wait()
        writes[j % 2] = pltpu.async_copy(
            bufs[j % 2], xin_hbm.at[e, pl.ds(lo + j * DISPATCH_ROWS, DISPATCH_ROWS)], wsems[j % 2])
        if j + 1 < nchunk:
            if writes[(j + 1) % 2] is not None:
                writes[(j + 1) % 2].wait()
            pending_gather = gather(j + 1)
    writes[(nchunk - 2) % 2].wait()
    writes[(nchunk - 1) % 2].wait()


def _dispatch(pos, aff, h):
    return pl.kernel(
        _dispatch_body, mesh=_sc_mesh(),
        out_type=[jax.ShapeDtypeStruct((E, CAP, D // 2), jnp.int32),
                  jax.ShapeDtypeStruct((E, CAP), jnp.int32),
                  jax.ShapeDtypeStruct((E, CAP), F32)],
        scratch_types=[pltpu.VMEM((S,), jnp.int32), pltpu.VMEM((S,), F32),
                       pltpu.VMEM((DISPATCH_SLOTS,), jnp.int32), pltpu.VMEM((DISPATCH_SLOTS,), F32),
                       pltpu.VMEM((DISPATCH_ROWS, D // 2), jnp.int32), pltpu.VMEM((DISPATCH_ROWS, D // 2), jnp.int32),
                       pltpu.SemaphoreType.DMA, pltpu.SemaphoreType.DMA,
                       pltpu.SemaphoreType.DMA, pltpu.SemaphoreType.DMA],
        compiler_params=pltpu.CompilerParams(needs_layout_passes=False),
        name="moe_dispatch",
    )(pos, aff, h)


def _combine_ranges(seed, y_hbm, idx_hbm, out_hbm, idx_v, li_v, bufs):
    rows = bufs[:NSLAB]
    accs = bufs[NSLAB:]
    c = lax.axis_index("c")
    s = lax.axis_index("s")
    share = COMBINE_RANGE // SC_SUBCORES
    pltpu.sync_copy(idx_hbm.at[s], idx_v)
    lane = lax.iota(jnp.int32, 16)

    @pl.loop(0, S // COMBINE_RANGE // SC_CORES)
    def _(r):
        t0 = (r * SC_CORES + c) * COMBINE_RANGE
        row0 = t0 + s * share
        seed(row0, tuple(accs[q].at[pl.ds(s * share, share)] for q in range(NSLAB)))
        plsc.subcore_barrier()

        @pl.loop(0, CAP // COMBINE_ROWS)
        def _(j):
            hits = jnp.zeros((16,), jnp.int32)
            for v in range(COMBINE_ROWS // 16):
                t = idx_v[pl.ds(j * COMBINE_ROWS + v * 16, 16)] - t0
                ok = (t >= 0) & (t < COMBINE_RANGE)
                li_v[pl.ds(v * 16, 16)] = jnp.where(ok, t, COMBINE_RANGE + lane)
                hits = hits + plsc.all_reduce_population_count(ok)

            @pl.when(jnp.max(hits) > 0)
            def _():
                pltpu.sync_copy(
                    tuple(y_hbm.at[s, pl.ds(j * COMBINE_ROWS, COMBINE_ROWS), pl.ds(q * SLAB, SLAB)]
                          for q in range(NSLAB)),
                    tuple(rows))
                pltpu.sync_copy(tuple(rows), tuple(accs[q].at[li_v] for q in range(NSLAB)), add=True)

        plsc.subcore_barrier()
        pltpu.sync_copy(tuple(accs[q].at[pl.ds(s * share, share)] for q in range(NSLAB)),
                        tuple(out_hbm.at[pl.ds(row0, share), pl.ds(q * SLAB, SLAB)] for q in range(NSLAB)))


def _combine_body(y_hbm, idx_hbm, x_hbm, out_hbm, idx_v, li_v, *bufs):
    share = COMBINE_RANGE // SC_SUBCORES

    def seed(row0, dst):
        pltpu.sync_copy(tuple(x_hbm.at[pl.ds(row0, share), pl.ds(q * SLAB, SLAB)] for q in range(NSLAB)), dst)

    _combine_ranges(seed, y_hbm, idx_hbm, out_hbm, idx_v, li_v, bufs)


def _moe_sum_body(y_hbm, idx_hbm, out_hbm, idx_v, li_v, zero_v, *bufs):
    @pl.loop(0, zero_v.shape[0])
    def _(r):
        for v in range(SLAB // 16):
            zero_v[r, pl.ds(v * 16, 16)] = jnp.zeros((16,), F32)

    _combine_ranges(lambda row0, dst: pltpu.sync_copy(tuple(zero_v for _ in range(NSLAB)), dst),
                    y_hbm, idx_hbm, out_hbm, idx_v, li_v, bufs)


def _combine_scratch():
    return ([pltpu.VMEM((CAP,), jnp.int32), pltpu.VMEM((COMBINE_ROWS,), jnp.int32)],
            [pltpu.VMEM((COMBINE_ROWS, SLAB), F32) for _ in range(NSLAB)]
            + [pltpu.VMEM_SHARED((COMBINE_RANGE + 16, SLAB), F32) for _ in range(NSLAB)])


def _combine(y, idx, x, out_rows=S):
    head, tail = _combine_scratch()
    return pl.kernel(
        _combine_body, mesh=_sc_mesh(),
        out_type=jax.ShapeDtypeStruct((out_rows, D), F32),
        scratch_types=head + tail,
        compiler_params=pltpu.CompilerParams(needs_layout_passes=False),
        name="moe_combine",
    )(y, idx, x)


def _moe_sum(y, idx):
    head, tail = _combine_scratch()
    return pl.kernel(
        _moe_sum_body, mesh=_sc_mesh(),
        out_type=jax.ShapeDtypeStruct((S, D), F32),
        scratch_types=head + [pltpu.VMEM((COMBINE_RANGE // SC_SUBCORES, SLAB), F32)] + tail,
        compiler_params=pltpu.CompilerParams(needs_layout_passes=False),
        name="moe_sum",
    )(y, idx)


FINAL_TM = 1024


def _final_add_kernel(big_ref, x_ref, m_ref, o_ref):
    del big_ref
    o_ref[...] = x_ref[...] + m_ref[...]


def _final_add(big, x, moe, b):
    nb = S // FINAL_TM
    return pl.pallas_call(
        _final_add_kernel,
        grid=(nb,),
        in_specs=[
            pl.BlockSpec(memory_space=pl.ANY),
            pl.BlockSpec((FINAL_TM, D), lambda i: (i, 0)),
            pl.BlockSpec((FINAL_TM, D), lambda i: (i, 0)),
        ],
        out_specs=pl.BlockSpec((FINAL_TM, D), lambda i: (b * nb + i, 0)),
        out_shape=jax.ShapeDtypeStruct((B * S, D), F32),
        input_output_aliases={0: 0},
        compiler_params=_cparams(("arbitrary",)),
        name="final_add",
    )(big, x, moe)


def _ffn_step(gate_ref, mod_ref, indices, x_ref, wg_ref, wu_ref, wd_ref, o_ref):
    e, f = indices
    last = F // FFN_TF - 1
    nt = FFN_TF // FFN_WT
    wg = [wg_ref[0, 0, :, j * FFN_WT:(j + 1) * FFN_WT].astype(BF16) for j in range(nt)]
    wu = [wu_ref[0, 0, :, j * FFN_WT:(j + 1) * FFN_WT].astype(BF16) for j in range(nt)]
    wd = wd_ref[0, 0].astype(BF16)
    is_first = f == 0
    is_last = f == last
    g2 = jnp.where(is_last, mod_ref[0, 0][5:6], 1.0)
    g_row = gate_ref[pl.ds(e, 1), :]
    g_col = jnp.broadcast_to(g_row, (128, CAP)).T[:, 0:1]
    for r in range(CAP // FFN_RT):
        rows = slice(r * FFN_RT, (r + 1) * FFN_RT)
        xr = _unpack_bf16_pairs(x_ref[0, rows, :])
        acts = []
        for j in range(nt):
            g = jnp.dot(xr, wg[j], preferred_element_type=F32)
            u = jnp.dot(xr, wu[j], preferred_element_type=F32)
            acts.append((g * _sigmoid(g) * u).astype(BF16))
        y = jnp.dot(jnp.concatenate(acts, axis=1), wd, preferred_element_type=F32)
        prev = jnp.where(is_first, 0.0, o_ref[0, rows, :])
        gate = jnp.where(is_last, g_col[rows, :], 1.0)
        o_ref[0, rows, :] = (prev + y) * gate * g2


def _select(aff, tri):
    return pl.pallas_call(
        _select_kernel,
        grid=(1,),
        in_specs=[
            pl.BlockSpec((E, S), lambda i: (0, 0)),
            pl.BlockSpec((128, 128), lambda i: (0, 0)),
        ],
        out_specs=pl.BlockSpec((E, S), lambda i: (0, 0)),
        out_shape=jax.ShapeDtypeStruct((E, S), jnp.int32),
        compiler_params=_cparams(("arbitrary",)),
        name="moe_select",
    )(aff, tri)


def _ffn_kernel(layer, x_hbm, wg_hbm, wu_hbm, wd_hbm, gate_ref, mod_ref, o_hbm):
    deep = pl.Buffered(FFN_WBUF)
    pltpu.emit_pipeline(
        functools.partial(_ffn_step, gate_ref, mod_ref),
        grid=(E, F // FFN_TF),
        in_specs=[
            pl.BlockSpec((1, CAP, D // 2), lambda e, f: (e, 0, 0)),
            pl.BlockSpec((1, 1, D, FFN_TF), lambda e, f: (layer, e, 0, f), pipeline_mode=deep),
            pl.BlockSpec((1, 1, D, FFN_TF), lambda e, f: (layer, e, 0, f), pipeline_mode=deep),
            pl.BlockSpec((1, 1, FFN_TF, D), lambda e, f: (layer, e, f, 0), pipeline_mode=deep),
        ],
        out_specs=[pl.BlockSpec((1, CAP, D), lambda e, f: (e, 0, 0))],
        _explicit_indices=True,
    )(x_hbm, wg_hbm, wu_hbm, wd_hbm, o_hbm)


def _ffn(xin, gate, b, mod_all, w_gate, w_up, w_down, layer):
    hbm = pl.BlockSpec(memory_space=pl.ANY)
    return pl.pallas_call(
        functools.partial(_ffn_kernel, layer),
        grid=(1,),
        in_specs=[
            hbm, hbm, hbm, hbm,
            pl.BlockSpec((E, CAP), lambda i: (0, 0)),
            pl.BlockSpec((1, 1, 6, D), lambda i: (layer, b, 0, 0)),
        ],
        out_specs=hbm,
        out_shape=jax.ShapeDtypeStruct((E, CAP, D), F32),
        compiler_params=pltpu.CompilerParams(vmem_limit_bytes=58 * 1024 * 1024),
        name="moe_ffn",
    )(xin, w_gate, w_up, w_down, gate, mod_all)


def kernel(x, c, w_ada, b_ada, norm_mix, norm_ffn, w_fourier_out, w_qkv, w_attn_out, q_gain, k_gain,
           sink, rel_bias, w_router, w_gate, w_up, w_down):
    m0, m2 = _dft_tables()
    m1 = _stage1_table()
    bucket_t = _bucket_table()
    tri = jnp.asarray(np.triu(np.ones((128, 128), np.float32)), BF16)
    mod_all = _ada(c, w_ada, b_ada)
    bias_tab = _bias_table(rel_bias, bucket_t)
    gain_mix = norm_mix.reshape(DEPTH, 1, D)
    gain_ffn = norm_ffn.reshape(DEPTH, 1, D)
    xs = [x, x]
    for layer in range(DEPTH):
        j = layer // 2
        wr = jnp.pad(w_router[layer], ((0, 0), (0, EP - E)))
        wr1 = wr.astype(BF16)
        wr = jnp.concatenate([wr1, (wr - wr1.astype(F32)).astype(BF16)], axis=1)
        if layer % 2 == 0:
            w_out_bf = w_fourier_out[j].astype(BF16)
            routed = [_fourier_layer(xs[b], b, mod_all, gain_mix, gain_ffn, wr, w_out_bf, layer, m0, m1, m2)
                      for b in range(B)]
        else:
            aw = _attn_weights(w_qkv[j], q_gain[j], k_gain[j], sink[j])
            w_out_bf = w_attn_out[j].astype(BF16)
            routed = [_attn_layer(xs[b], b, mod_all, gain_mix, gain_ffn, wr, aw, w_out_bf, bias_tab, layer)
                      for b in range(B)]
        xs = [routed[b][0] for b in range(B)]
        pos = [_select(routed[b][2], tri) for b in range(B)]
        disp = [_dispatch(pos[b], routed[b][2], routed[b][1]) for b in range(B)]
        ys = [_ffn(disp[b][0], disp[b][2], b, mod_all, w_gate, w_up, w_down, layer) for b in range(B)]
        if layer < DEPTH - 1:
            xs = [_combine(ys[b], disp[b][1], xs[b]) for b in range(B)]
    out = _combine(ys[0], disp[0][1], xs[0], out_rows=B * S)
    for b in range(1, B):
        out = _final_add(out, xs[b], _moe_sum(ys[b], disp[b][1]), b)
    return out.reshape(B, S, D)
```
